```python
import math
import jax, jax.numpy as jnp
from jax import lax
import numpy as np


D_MODEL = 2048
BATCH = 16
SEQ = 256
DEPTH = 2
DEC_BATCH = 2
DEC_SEQ = 1024
PAST_LEN = 512

GRID_W = 64
D_ATTN = 1024
D_SSD = 1024
D_MIX = D_ATTN + D_SSD
HEAD_DIM = 128
N_Q_HEADS = D_ATTN // HEAD_DIM
N_KV_HEADS = 2
Q_PER_KV = N_Q_HEADS // N_KV_HEADS
KV_DIM = N_KV_HEADS * HEAD_DIM
ROT_HALF = HEAD_DIM // 2
ROPE_THETA = 10000.0
Q_BLOCK = 128
SSD_HEAD_DIM = 64
SSD_HEADS = D_SSD // SSD_HEAD_DIM
SSD_GROUPS = 2
HEADS_PER_GROUP = SSD_HEADS // SSD_GROUPS
D_STATE = 128
D_CONV = 3
CONV_DIM = D_SSD + 2 * SSD_GROUPS * D_STATE
CHUNK = 128
N_IN = D_ATTN + 2 * KV_DIM + D_SSD + CONV_DIM + 2 * SSD_HEADS
F_DENSE = 5632
N_EXPERTS = 8
TOP_K = 2
F_EXPERT = 1024
EPS = 1e-6

kernel_name = 'hybrid_ssd_gqa_prefix_diffusion_step'


def rmsnorm(x, g):
    xf = x.astype(jnp.float32)
    y = xf * lax.rsqrt(jnp.mean(xf * xf, axis=-1, keepdims=True) + EPS)
    return (y * g.astype(jnp.float32)).astype(x.dtype)


def split_points():
    pts, acc = [], 0
    for s in (D_ATTN, KV_DIM, KV_DIM, D_SSD, CONV_DIM):
        acc += s
        pts.append(acc)
    return pts


def axial_angles(n):
    rows = n // GRID_W
    t_row = jnp.repeat(jnp.arange(rows, dtype=jnp.float32), GRID_W)
    t_col = jnp.tile(jnp.arange(GRID_W, dtype=jnp.float32), rows)
    inv = 1.0 / (ROPE_THETA ** (jnp.arange(0, ROT_HALF, 2, dtype=jnp.float32) / ROT_HALF))
    return t_row[:, None] * inv, t_col[:, None] * inv


def rotate(x, ang):
    cos = jnp.cos(ang)[None, :, None, :].astype(x.dtype)
    sin = jnp.sin(ang)[None, :, None, :].astype(x.dtype)
    x1, x2 = jnp.split(x, 2, axis=-1)
    return jnp.concatenate([x1 * cos - x2 * sin, x2 * cos + x1 * sin], axis=-1)


def axial_rope(x, ang_row, ang_col):
    return jnp.concatenate([rotate(x[..., :ROT_HALF], ang_row), rotate(x[..., ROT_HALF:], ang_col)], axis=-1)


def block_attention(q, k, v):
    b, lq = q.shape[0], q.shape[1]
    nb = lq // Q_BLOCK
    qb = q.reshape(b, nb, Q_BLOCK, N_KV_HEADS, Q_PER_KV, HEAD_DIM).transpose(1, 0, 2, 3, 4, 5)

    def one_block(qblk):
        s = jnp.einsum('bqgrd,bkgd->bgrqk', qblk, k).astype(jnp.float32) * (HEAD_DIM ** -0.5)
        pr = jax.nn.softmax(s, axis=-1).astype(v.dtype)
        return jnp.einsum('bgrqk,bkgd->bqgrd', pr, v)

    o = lax.map(one_block, qb)
    return o.transpose(1, 0, 2, 3, 4, 5).reshape(b, lq, N_Q_HEADS * HEAD_DIM)


def dwconv(x, w, bias):
    y = lax.conv_general_dilated(x, w[:, None, :].astype(x.dtype), window_strides=(1,),
                                 padding=[(D_CONV // 2, D_CONV // 2)],
                                 dimension_numbers=('NWC', 'WIO', 'NWC'),
                                 feature_group_count=x.shape[-1])
    return y + bias.astype(x.dtype)


def ssd_scan(x, dt, a, bmat, cmat, h0):
    b, n, h, pdim = x.shape
    nc = n // CHUNK
    xc = x.astype(jnp.float32).reshape(b, nc, CHUNK, h, pdim)
    dtc = dt.reshape(b, nc, CHUNK, h)
    bc = bmat.astype(jnp.float32).reshape(b, nc, CHUNK, h, D_STATE)
    cc = cmat.astype(jnp.float32).reshape(b, nc, CHUNK, h, D_STATE)
    cs = jnp.cumsum(dtc * a, axis=2)
    lower = jnp.tril(jnp.ones((CHUNK, CHUNK), bool))[None, None, :, :, None]
    decay = jnp.exp(jnp.where(lower, cs[:, :, :, None, :] - cs[:, :, None, :, :], -jnp.inf))
    scores = jnp.einsum('bcthn,bcshn->bctsh', cc, bc) * decay * dtc[:, :, None, :, :]
    y_diag = jnp.einsum('bctsh,bcshp->bcthp', scores, xc)
    to_end = jnp.exp(cs[:, :, -1:, :] - cs) * dtc
    chunk_states = jnp.einsum('bcshp,bcshn->bchpn', to_end[..., None] * xc, bc)
    chunk_decay = jnp.exp(cs[:, :, -1, :])

    def step(state, inp):
        st, dec = inp
        return state * dec[:, :, None, None] + st, state

    final, entering = lax.scan(step, h0.astype(jnp.float32),
                               (jnp.moveaxis(chunk_states, 1, 0), jnp.moveaxis(chunk_decay, 1, 0)))
    entering = jnp.moveaxis(entering, 0, 1)
    y_off = jnp.einsum('bcthn,bchpn->bcthp', cc, entering) * jnp.exp(cs)[..., None]
    return (y_diag + y_off).reshape(b, n, h, pdim), final


def ssd_branch(xbc, z, dt_raw, p, s_f0, s_b0):
    b, n, _ = xbc.shape
    xbc = jax.nn.silu(dwconv(xbc, p['conv_w'], p['conv_b']))
    xs, bm, cm = jnp.split(xbc, [D_SSD, D_SSD + SSD_GROUPS * D_STATE], axis=-1)
    x = xs.reshape(b, n, SSD_HEADS, SSD_HEAD_DIM)
    bh = jnp.repeat(bm.reshape(b, n, SSD_GROUPS, D_STATE), HEADS_PER_GROUP, axis=2)
    ch = jnp.repeat(cm.reshape(b, n, SSD_GROUPS, D_STATE), HEADS_PER_GROUP, axis=2)
    dtr = dt_raw.astype(jnp.float32)
    dt_f = jax.nn.softplus(dtr[..., :SSD_HEADS] + p['dt_bias_fwd'].astype(jnp.float32))
    dt_b = jax.nn.softplus(dtr[..., SSD_HEADS:] + p['dt_bias_bwd'].astype(jnp.float32))
    a_f = -jnp.exp(p['a_log_fwd'].astype(jnp.float32))
    a_b = -jnp.exp(p['a_log_bwd'].astype(jnp.float32))
    y_f, s_f = ssd_scan(x, dt_f, a_f, bh, ch, s_f0)
    flip = lambda t: jnp.flip(t, axis=1)
    y_b, s_b = ssd_scan(flip(x), flip(dt_b), a_b, flip(bh), flip(ch), s_b0)
    y = y_f + flip(y_b) + p['d_skip'].astype(jnp.float32)[:, None] * x.astype(jnp.float32)
    y = y.reshape(b, n, D_SSD).astype(z.dtype) * jax.nn.silu(z)
    return rmsnorm(y, p['ssd_norm_g']), s_f, s_b


def token_mix(h, p, ctx):
    b, n, _ = h.shape
    q, k, v, z, xbc, dt_raw = jnp.split(h @ p['w_in'], split_points(), axis=-1)
    q = rmsnorm(q.reshape(b, n, N_Q_HEADS, HEAD_DIM), p['q_norm_g'])
    k = rmsnorm(k.reshape(b, n, N_KV_HEADS, HEAD_DIM), p['k_norm_g'])
    v = v.reshape(b, n, N_KV_HEADS, HEAD_DIM)
    if ctx is None:
        o = block_attention(q, k, v)
        s_f0 = jnp.zeros((b, SSD_HEADS, SSD_HEAD_DIM, D_STATE), jnp.float32)
        s_b0 = s_f0
    else:
        ctx_k, ctx_v, s_f0, s_b0 = ctx
        ang_r, ang_c = axial_angles(n)
        qr = axial_rope(q, ang_r, ang_c)
        kr = axial_rope(k, ang_r, ang_c)
        o = block_attention(qr, jnp.concatenate([kr, ctx_k.astype(k.dtype)], axis=1),
                            jnp.concatenate([v, ctx_v.astype(v.dtype)], axis=1))
    o = rmsnorm(o, p['attn_out_g'])
    y_ssd, s_f, s_b = ssd_branch(xbc, z, dt_raw, p, s_f0, s_b0)
    out = jnp.concatenate([o, y_ssd], axis=-1) @ p['w_out']
    return out, (k, v, s_f, s_b)


def swiglu(h, wg, wu, wd):
    return (jax.nn.silu(h @ wg) * (h @ wu)) @ wd


def moe_swiglu(h, router_w, wg, wu, wd):
    b, n, d = h.shape
    t = h.reshape(b * n, d)
    probs = jax.nn.softmax((t @ router_w).astype(jnp.float32), axis=-1)
    top_p, top_i = lax.top_k(probs, TOP_K)
    top_p = top_p / jnp.sum(top_p, axis=-1, keepdims=True)
    gates = jnp.sum(jax.nn.one_hot(top_i, N_EXPERTS, dtype=jnp.float32) * top_p[..., None], axis=1)
    hid = jax.nn.silu(jnp.einsum('td,edf->tef', t, wg)) * jnp.einsum('td,edf->tef', t, wu)
    hid = hid * gates.astype(hid.dtype)[..., None]
    return jnp.einsum('tef,efd->td', hid, wd).reshape(b, n, d)


def layer_step(x, cond, p, ffn, ctx):
    mods = jnp.split(jax.nn.silu(cond) @ p['ada_w'] + p['ada_b'], 6, axis=-1)
    sh1, sc1, g1, sh2, sc2, g2 = [m[:, None, :] for m in mods]
    hmix = rmsnorm(x, p['norm1_g']) * (1 + sc1) + sh1
    mix, ctx_out = token_mix(hmix, p, ctx)
    x = x + g1 * mix
    hff = rmsnorm(x, p['norm2_g']) * (1 + sc2) + sh2
    x = x + g2 * ffn(hff)
    return x, ctx_out


def setup_inputs(seed: int = 0) -> dict:
    key = jax.random.key(seed)
    ks = iter(jax.random.split(key, 48))
    f32 = jnp.float32

    def nrm(shape, scale):
        return jax.random.normal(next(ks), shape, f32) * scale

    def gain(shape):
        return 1.0 + nrm(shape, 0.02)

    def dt_bias(shape):
        u = jax.random.uniform(next(ks), shape, f32)
        dt = jnp.exp(u * (math.log(0.1) - math.log(0.001)) + math.log(0.001))
        return dt + jnp.log(-jnp.expm1(-dt))

    def a_log(shape):
        return jnp.log(jax.random.uniform(next(ks), shape, f32, 1.0, 16.0))

    n_dense = (DEPTH + 1) // 2
    n_moe = DEPTH // 2
    return {
        'x_prompt': nrm((BATCH, SEQ, D_MODEL), 1.0),
        'x_sample': nrm((DEC_BATCH, DEC_SEQ, D_MODEL), 1.0),
        'c': nrm((DEC_BATCH, D_MODEL), 1.0),
        'cache_k': nrm((DEC_BATCH, DEPTH, PAST_LEN, N_KV_HEADS, HEAD_DIM), 1.0),
        'cache_v': nrm((DEC_BATCH, DEPTH, PAST_LEN, N_KV_HEADS, HEAD_DIM), 1.0),
        'state_ssm_fwd': nrm((DEC_BATCH, DEPTH, SSD_HEADS, SSD_HEAD_DIM, D_STATE), 0.5),
        'state_ssm_bwd': nrm((DEC_BATCH, DEPTH, SSD_HEADS, SSD_HEAD_DIM, D_STATE), 0.5),
        'c_ctx': nrm((D_MODEL,), 1.0),
        'ada_w': nrm((DEPTH, D_MODEL, 6 * D_MODEL), 0.5 * D_MODEL ** -0.5),
        'ada_b': nrm((DEPTH, 6 * D_MODEL), 0.02),
        'norm1_g': gain((DEPTH, D_MODEL)),
        'norm2_g': gain((DEPTH, D_MODEL)),
        'w_in': nrm((DEPTH, D_MODEL, N_IN), D_MODEL ** -0.5),
        'q_norm_g': gain((DEPTH, HEAD_DIM)),
        'k_norm_g': gain((DEPTH, HEAD_DIM)),
        'conv_w': nrm((DEPTH, D_CONV, CONV_DIM), D_CONV ** -0.5),
        'conv_b': nrm((DEPTH, CONV_DIM), 0.02),
        'a_log_fwd': a_log((DEPTH, SSD_HEADS)),
        'a_log_bwd': a_log((DEPTH, SSD_HEADS)),
        'dt_bias_fwd': dt_bias((DEPTH, SSD_HEADS)),
        'dt_bias_bwd': dt_bias((DEPTH, SSD_HEADS)),
        'd_skip': gain((DEPTH, SSD_HEADS)),
        'ssd_norm_g': gain((DEPTH, D_SSD)),
        'attn_out_g': gain((DEPTH, D_ATTN)),
        'w_out': nrm((DEPTH, D_MIX, D_MODEL), D_MIX ** -0.5),
        'ffn_w_gate': nrm((n_dense, D_MODEL, F_DENSE), D_MODEL ** -0.5),
        'ffn_w_up': nrm((n_dense, D_MODEL, F_DENSE), D_MODEL ** -0.5),
        'ffn_w_down': nrm((n_dense, F_DENSE, D_MODEL), F_DENSE ** -0.5),
        'router_w': nrm((n_moe, D_MODEL, N_EXPERTS), D_MODEL ** -0.5),
        'moe_w_gate': nrm((n_moe, N_EXPERTS, D_MODEL, F_EXPERT), D_MODEL ** -0.5),
        'moe_w_up': nrm((n_moe, N_EXPERTS, D_MODEL, F_EXPERT), D_MODEL ** -0.5),
        'moe_w_down': nrm((n_moe, N_EXPERTS, F_EXPERT, D_MODEL), F_EXPERT ** -0.5),
        'final_norm_g': gain((D_MODEL,)),
    }


def reference(x_prompt, x_sample, c, cache_k, cache_v, state_ssm_fwd, state_ssm_bwd, c_ctx,
              ada_w, ada_b, norm1_g, norm2_g, w_in, q_norm_g, k_norm_g, conv_w, conv_b,
              a_log_fwd, a_log_bwd, dt_bias_fwd, dt_bias_bwd, d_skip, ssd_norm_g, attn_out_g,
              w_out, ffn_w_gate, ffn_w_up, ffn_w_down, router_w, moe_w_gate, moe_w_up,
              moe_w_down, final_norm_g):
    xp, xs = x_prompt, x_sample
    cond_ctx = c_ctx[None, :]
    ks, vs, sfs, sbs = [], [], [], []
    for i in range(DEPTH):
        p = {
            'ada_w': ada_w[i], 'ada_b': ada_b[i], 'norm1_g': norm1_g[i], 'norm2_g': norm2_g[i],
            'w_in': w_in[i], 'q_norm_g': q_norm_g[i], 'k_norm_g': k_norm_g[i],
            'conv_w': conv_w[i], 'conv_b': conv_b[i],
            'a_log_fwd': a_log_fwd[i], 'a_log_bwd': a_log_bwd[i],
            'dt_bias_fwd': dt_bias_fwd[i], 'dt_bias_bwd': dt_bias_bwd[i], 'd_skip': d_skip[i],
            'ssd_norm_g': ssd_norm_g[i], 'attn_out_g': attn_out_g[i], 'w_out': w_out[i],
        }
        j = i // 2
        if i % 2 == 0:
            def ffn(hh, j=j):
                return swiglu(hh, ffn_w_gate[j], ffn_w_up[j], ffn_w_down[j])
        else:
            def ffn(hh, j=j):
                return moe_swiglu(hh, router_w[j], moe_w_gate[j], moe_w_up[j], moe_w_down[j])
        xp, (kc, vc, sf, sb) = layer_step(xp, cond_ctx, p, ffn, None)
        ks.append(kc)
        vs.append(vc)
        sfs.append(sf)
        sbs.append(sb)
        ctx = (cache_k[:, i], cache_v[:, i], state_ssm_fwd[:, i], state_ssm_bwd[:, i])
        xs, _ = layer_step(xs, c, p, ffn, ctx)
    y_prompt = rmsnorm(xp, final_norm_g)
    y_sample = rmsnorm(xs, final_norm_g)
    new_cache_k = jnp.stack(ks, axis=1)
    new_cache_v = jnp.stack(vs, axis=1)
    new_state_ssm_fwd = jnp.stack(sfs, axis=1).astype(x_prompt.dtype)
    new_state_ssm_bwd = jnp.stack(sbs, axis=1).astype(x_prompt.dtype)
    return (y_prompt, y_sample, new_cache_k, new_cache_v, new_state_ssm_fwd, new_state_ssm_bwd)
```

```python
import functools

import jax
import jax.numpy as jnp
from jax import lax
from jax.experimental import pallas as pl
from jax.experimental.pallas import tpu as pltpu

F32 = jnp.float32
BF16 = jnp.bfloat16

D_MODEL = 2048
BATCH = 16
SEQ = 256
DEPTH = 2
DEC_BATCH = 2
DEC_SEQ = 1024
PAST_LEN = 512
GRID_W = 64
D_ATTN = 1024
D_SSD = 1024
HEAD_DIM = 128
N_Q_HEADS = 8
N_KV_HEADS = 2
Q_PER_KV = 4
KV_DIM = 256
ROT_HALF = 64
ROPE_THETA = 10000.0
SSD_HEAD_DIM = 64
SSD_HEADS = 16
SSD_GROUPS = 2
HEADS_PER_GROUP = 8
D_STATE = 128
CONV_DIM = 1536
CHUNK = 128
N_MAIN = 4096
F_DENSE = 5632
N_EXPERTS = 8
F_EXPERT = 1024
EPS = 1e-6

T_PROMPT = BATCH * SEQ
T_SAMPLE = DEC_BATCH * DEC_SEQ
T_ALL = T_PROMPT + T_SAMPLE
N_COND = 16

VMEM_LIMIT = 56 * 1024 * 1024


def _cparams(sem):
    return pltpu.CompilerParams(dimension_semantics=sem, vmem_limit_bytes=VMEM_LIMIT)


def _mod_group(i, tm):
    return jnp.maximum(0, (i * tm - T_PROMPT + DEC_SEQ) // DEC_SEQ)


def _silu(x):
    return x * jax.nn.sigmoid(x)


def _rms(x, g):
    ms = jnp.mean(x * x, axis=-1, keepdims=True)
    return x * lax.rsqrt(ms + EPS) * g


def _dot(a, b):
    return jnp.dot(a, b, preferred_element_type=F32)


def _dot_nt(a, b):
    return lax.dot_general(a, b, (((1,), (1,)), ((), ())), preferred_element_type=F32)


def _split3(x):
    hi = x.astype(BF16)
    r1 = x - hi.astype(F32)
    mid = r1.astype(BF16)
    r2 = r1 - mid.astype(F32)
    return hi, mid, r2.astype(BF16)


def _ada_kernel(c_ref, w_ref, b_ref, o_ref):
    s = _silu(c_ref[...]).astype(BF16)
    o_ref[...] = _dot(s, w_ref[...].astype(BF16)) + b_ref[...]


def _ada_mods(cond, ada_w, ada_b):
    tn = 1024
    n_out = 6 * D_MODEL
    return pl.pallas_call(
        _ada_kernel,
        grid=(DEPTH, n_out // tn),
        in_specs=[
            pl.BlockSpec((N_COND, D_MODEL), lambda l, j: (0, 0)),
            pl.BlockSpec((None, D_MODEL, tn), lambda l, j: (l, 0, j)),
            pl.BlockSpec((None, 1, tn), lambda l, j: (l, 0, j)),
        ],
        out_specs=pl.BlockSpec((None, N_COND, tn), lambda l, j: (l, 0, j)),
        out_shape=jax.ShapeDtypeStruct((DEPTH, N_COND, n_out), F32),
        compiler_params=_cparams(("parallel", "parallel")),
        name="ada_mods",
    )(cond, ada_w, ada_b.reshape(DEPTH, 1, n_out))


def _inproj_kernel(x_ref, mod_ref, g_ref, w_ref, wdt_ref, proj_ref, dt_ref, h_scr):
    @pl.when(pl.program_id(1) == 0)
    def _():
        h = _rms(x_ref[...], g_ref[...]) * (1.0 + mod_ref[1:2, :]) + mod_ref[0:1, :]
        hb = h.astype(BF16)
        h_scr[...] = hb
        dt_ref[...] = _dot(hb, wdt_ref[...].astype(BF16))

    proj_ref[...] = _dot(h_scr[...], w_ref[...].astype(BF16))


def _inproj(x, mods, g, w_in, w_dt):
    tm, tn = 1024, 512
    return pl.pallas_call(
        _inproj_kernel,
        grid=(T_ALL // tm, N_MAIN // tn),
        in_specs=[
            pl.BlockSpec((tm, D_MODEL), lambda i, j: (i, 0)),
            pl.BlockSpec((None, 6, D_MODEL), lambda i, j: (_mod_group(i, tm), 0, 0)),
            pl.BlockSpec((1, D_MODEL), lambda i, j: (0, 0)),
            pl.BlockSpec((D_MODEL, tn), lambda i, j: (0, j)),
            pl.BlockSpec((D_MODEL, 128), lambda i, j: (0, 0)),
        ],
        out_specs=[
            pl.BlockSpec((tm, tn), lambda i, j: (i, j)),
            pl.BlockSpec((tm, 128), lambda i, j: (i, 0)),
        ],
        out_shape=[
            jax.ShapeDtypeStruct((T_ALL, N_MAIN), F32),
            jax.ShapeDtypeStruct((T_ALL, 128), F32),
        ],
        scratch_shapes=[pltpu.VMEM((tm, D_MODEL), BF16)],
        compiler_params=_cparams(("parallel", "arbitrary")),
        name="inproj",
    )(x, mods, g, w_in, w_dt)


def _rope(x, cos, sin_signed):
    lane = lax.broadcasted_iota(jnp.int32, x.shape, 1)
    first = (lane // (ROT_HALF // 2)) % 2 == 0
    swapped = jnp.where(first, pltpu.roll(x, HEAD_DIM - ROT_HALF // 2, 1), pltpu.roll(x, ROT_HALF // 2, 1))
    return x * cos + swapped * sin_signed


def _attn_kernel(*refs, nk_new, has_ctx):
    if has_ctx:
        (q_ref, kv_ref, qg_ref, kg_ref, og_ref, ck_ref, cv_ref, cq_ref, sq_ref, ckk_ref, skk_ref,
         o_ref, kb_scr, vb_scr, o_scr) = refs
    else:
        (q_ref, kv_ref, qg_ref, kg_ref, og_ref, o_ref, ko_ref, vo_ref, kb_scr, vb_scr, o_scr) = refs

    @pl.when(pl.program_id(1) == 0)
    def _():
        for g in range(N_KV_HEADS):
            sl = slice(g * HEAD_DIM, (g + 1) * HEAD_DIM)
            kn = _rms(kv_ref[:, sl], kg_ref[...])
            if has_ctx:
                kb_scr[0:nk_new, sl] = _rope(kn, ckk_ref[...], skk_ref[...]).astype(BF16)
                kb_scr[nk_new:, sl] = ck_ref[:, sl].astype(BF16)
            else:
                ko_ref[:, sl] = kn
                kb_scr[:, sl] = kn.astype(BF16)
        v = kv_ref[:, KV_DIM:]
        vb_scr[0:nk_new, :] = v.astype(BF16)
        if has_ctx:
            vb_scr[nk_new:, :] = cv_ref[...].astype(BF16)
        else:
            vo_ref[...] = v

    scale = HEAD_DIM ** -0.5
    for h in range(N_Q_HEADS):
        g = h // Q_PER_KV
        sl = slice(h * HEAD_DIM, (h + 1) * HEAD_DIM)
        gsl = slice(g * HEAD_DIM, (g + 1) * HEAD_DIM)
        qn = _rms(q_ref[:, sl], qg_ref[...])
        if has_ctx:
            qn = _rope(qn, cq_ref[...], sq_ref[...])
        s = _dot_nt(qn.astype(BF16), kb_scr[:, gsl]) * scale
        e = jnp.exp(s - jnp.max(s, axis=-1, keepdims=True))
        l = jnp.sum(e, axis=-1, keepdims=True)
        o_scr[:, sl] = _dot(e.astype(BF16), vb_scr[:, gsl]) / l
    o_ref[...] = _rms(o_scr[...], og_ref[...]).astype(BF16)


def _attention_prompt(proj, qg, kg, og):
    n = SEQ
    kern = functools.partial(_attn_kernel, nk_new=n, has_ctx=False)
    vec = lambda w: pl.BlockSpec((1, w), lambda b, i: (0, 0))
    return pl.pallas_call(
        kern,
        grid=(BATCH, 1),
        in_specs=[
            pl.BlockSpec((n, D_ATTN), lambda b, i: (b, 0)),
            pl.BlockSpec((n, 2 * KV_DIM), lambda b, i: (b, 2)),
            vec(HEAD_DIM), vec(HEAD_DIM), vec(D_ATTN),
        ],
        out_specs=[
            pl.BlockSpec((n, D_ATTN), lambda b, i: (b, 0)),
            pl.BlockSpec((None, n, KV_DIM), lambda b, i: (b, 0, 0)),
            pl.BlockSpec((None, n, KV_DIM), lambda b, i: (b, 0, 0)),
        ],
        out_shape=[
            jax.ShapeDtypeStruct((T_PROMPT, D_ATTN), BF16),
            jax.ShapeDtypeStruct((BATCH, n, KV_DIM), F32),
            jax.ShapeDtypeStruct((BATCH, n, KV_DIM), F32),
        ],
        scratch_shapes=[
            pltpu.VMEM((n, KV_DIM), BF16), pltpu.VMEM((n, KV_DIM), BF16), pltpu.VMEM((n, D_ATTN), F32),
        ],
        compiler_params=_cparams(("parallel", "arbitrary")),
        name="attn_prompt",
    )(proj, proj, qg, kg, og)


def _attention_sample(proj, qg, kg, og, ck, cv, cos, sin_signed, layer):
    n, tq = DEC_SEQ, 512
    nq = n // tq
    nk = n + PAST_LEN
    kern = functools.partial(_attn_kernel, nk_new=n, has_ctx=True)
    vec = lambda w: pl.BlockSpec((1, w), lambda b, i: (0, 0))
    q_blk0 = T_PROMPT // tq
    kv_blk0 = T_PROMPT // n
    return pl.pallas_call(
        kern,
        grid=(DEC_BATCH, nq),
        in_specs=[
            pl.BlockSpec((tq, D_ATTN), lambda b, i: (q_blk0 + b * nq + i, 0)),
            pl.BlockSpec((n, 2 * KV_DIM), lambda b, i: (kv_blk0 + b, 2)),
            vec(HEAD_DIM), vec(HEAD_DIM), vec(D_ATTN),
            pl.BlockSpec((None, None, PAST_LEN, KV_DIM), lambda b, i: (b, layer, 0, 0)),
            pl.BlockSpec((None, None, PAST_LEN, KV_DIM), lambda b, i: (b, layer, 0, 0)),
            pl.BlockSpec((tq, HEAD_DIM), lambda b, i: (i, 0)),
            pl.BlockSpec((tq, HEAD_DIM), lambda b, i: (i, 0)),
            pl.BlockSpec((n, HEAD_DIM), lambda b, i: (0, 0)),
            pl.BlockSpec((n, HEAD_DIM), lambda b, i: (0, 0)),
        ],
        out_specs=pl.BlockSpec((tq, D_ATTN), lambda b, i: (b * nq + i, 0)),
        out_shape=jax.ShapeDtypeStruct((T_SAMPLE, D_ATTN), BF16),
        scratch_shapes=[
            pltpu.VMEM((nk, KV_DIM), BF16), pltpu.VMEM((nk, KV_DIM), BF16), pltpu.VMEM((tq, D_ATTN), F32),
        ],
        compiler_params=_cparams(("parallel", "arbitrary")),
        name="attn_sample",
    )(proj, proj, qg, kg, og, ck, cv, cos, sin_signed, cos, sin_signed)


def _conv_silu(x, w, b):
    n = x.shape[0]
    row = lax.broadcasted_iota(jnp.int32, (n, 1), 0)
    prev = jnp.where(row == 0, 0.0, pltpu.roll(x, 1, 0))
    nxt = jnp.where(row == n - 1, 0.0, pltpu.roll(x, n - 1, 0))
    return _silu(prev * w[0:1, :] + x * w[1:2, :] + nxt * w[2:3, :] + b)


def _softplus(x):
    return jnp.maximum(x, 0.0) + jnp.log1p(jnp.exp(-jnp.abs(x)))


def _pair_cols(m, li, lo_lanes):
    return jnp.where(lo_lanes, m[:, li:li + 1], m[:, li + 1:li + 2])


def _ssd_kernel(*refs, n, has_init):
    if has_init:
        (za_ref, zb_ref, xa_ref, xb_ref, bc_ref, dt_ref, cw_ref, cb_ref, dtb_ref, alog_ref, dsk_ref, ng_ref,
         sf0_ref, sb0_ref, y_ref, xc_scr, bcc_scr, dts_scr, y_scr, s_scr) = refs
    else:
        (za_ref, zb_ref, xa_ref, xb_ref, bc_ref, dt_ref, cw_ref, cb_ref, dtb_ref, alog_ref, dsk_ref, ng_ref,
         y_ref, sf_ref, sb_ref, xc_scr, bcc_scr, dts_scr, y_scr, s_scr) = refs
    nc = n // CHUNK
    gw = HEADS_PER_GROUP * SSD_HEAD_DIM

    xc_scr[:, 0:gw] = _conv_silu(xa_ref[...], cw_ref[:, 0:gw], cb_ref[:, 0:gw])
    xc_scr[:, gw:] = _conv_silu(xb_ref[...], cw_ref[:, gw:2 * gw], cb_ref[:, gw:2 * gw])
    bcc_scr[...] = _conv_silu(bc_ref[...], cw_ref[:, 2 * gw:], cb_ref[:, 2 * gw:])
    y_scr[...] = xc_scr[...] * dsk_ref[...]
    dts_scr[...] = _softplus(dt_ref[...] + dtb_ref[...])
    for g in range(SSD_GROUPS):
        if has_init:
            s_scr[0, g] = sf0_ref[g * HEADS_PER_GROUP:(g + 1) * HEADS_PER_GROUP].reshape(gw, D_STATE).T
            s_scr[1, g] = sb0_ref[g * HEADS_PER_GROUP:(g + 1) * HEADS_PER_GROUP].reshape(gw, D_STATE).T
        else:
            s_scr[0, g] = jnp.zeros((D_STATE, gw), F32)
            s_scr[1, g] = jnp.zeros((D_STATE, gw), F32)

    a_row = -jnp.exp(alog_ref[...])
    ri = lax.broadcasted_iota(jnp.int32, (CHUNK, CHUNK), 0)
    ci = lax.broadcasted_iota(jnp.int32, (CHUNK, CHUNK), 1)
    lo_lanes = ci < SSD_HEAD_DIM
    lo_row = lo_lanes[0:1, :]

    def scan_chunk(dirn, c):
        mask = (ci <= ri) if dirn == 0 else (ci >= ri)
        tri = mask.astype(BF16)
        lane0 = dirn * SSD_HEADS
        r0 = pl.multiple_of(c * CHUNK, CHUNK)
        rows = pl.ds(r0, CHUNK)
        dt = dts_scr[rows, :]
        d1, d2, d3 = _split3(dt * a_row)
        cs = _dot(tri, d1) + _dot(tri, d2) + _dot(tri, d3)
        cs_t = cs.T
        dt_t = dt.T
        total = cs[CHUNK - 1:CHUNK, :] if dirn == 0 else cs[0:1, :]
        e_in = jnp.exp(cs)
        to_end = jnp.exp(total - cs) * dt
        dec = jnp.exp(total)
        for g in range(SSD_GROUPS):
            bm = bcc_scr[rows, g * D_STATE:(g + 1) * D_STATE]
            cm = bcc_scr[rows, (SSD_GROUPS + g) * D_STATE:(SSD_GROUPS + g + 1) * D_STATE].astype(BF16)
            gmat = _dot_nt(cm, bm.astype(BF16))
            st = s_scr[dirn, g]
            y_off = _dot(cm, st.astype(BF16))
            xs_parts, dec_parts = [], []
            for j in range(HEADS_PER_GROUP // 2):
                li = lane0 + g * HEADS_PER_GROUP + 2 * j
                lanes = slice((g * 4 + j) * 128, (g * 4 + j + 1) * 128)
                a_parts = []
                for hh in range(2):
                    diff = cs[:, li + hh:li + hh + 1] - cs_t[li + hh:li + hh + 1, :]
                    w = jnp.exp(jnp.where(mask, diff, -jnp.inf)) * dt_t[li + hh:li + hh + 1, :]
                    a_parts.append((gmat * w).astype(BF16))
                x_pair = xc_scr[rows, lanes]
                x2 = jnp.concatenate([jnp.where(lo_lanes, x_pair, 0.0), jnp.where(lo_lanes, 0.0, x_pair)],
                                     axis=0).astype(BF16)
                y_diag = _dot(jnp.concatenate(a_parts, axis=1), x2)
                y_pair = y_diag + y_off[:, j * 128:(j + 1) * 128] * _pair_cols(e_in, li, lo_lanes)
                y_scr[rows, lanes] = y_scr[rows, lanes] + y_pair
                xs_parts.append((x_pair * _pair_cols(to_end, li, lo_lanes)).astype(BF16))
                dec_parts.append(_pair_cols(dec, li, lo_row))
            ds = _dot(bm.T.astype(BF16), jnp.concatenate(xs_parts, axis=1))
            s_scr[dirn, g] = st * jnp.concatenate(dec_parts, axis=1) + ds

    def body(i, carry):
        scan_chunk(0, i)
        scan_chunk(1, nc - 1 - i)
        return carry

    lax.fori_loop(0, nc, body, 0)

    def finish(c, carry):
        rows = pl.ds(pl.multiple_of(c * CHUNK, CHUNK), CHUNK)
        ya = y_scr[rows, 0:gw] * _silu(za_ref[rows, :])
        yb = y_scr[rows, gw:] * _silu(zb_ref[rows, :])
        ms = (jnp.sum(ya * ya, axis=-1, keepdims=True) + jnp.sum(yb * yb, axis=-1, keepdims=True)) / D_SSD
        inv = lax.rsqrt(ms + EPS)
        y_ref[rows, 0:gw] = (ya * inv * ng_ref[:, 0:gw]).astype(BF16)
        y_ref[rows, gw:] = (yb * inv * ng_ref[:, gw:]).astype(BF16)
        return carry

    lax.fori_loop(0, nc, finish, 0)

    if not has_init:
        for g in range(SSD_GROUPS):
            hs = slice(g * HEADS_PER_GROUP, (g + 1) * HEADS_PER_GROUP)
            sf_ref[hs] = s_scr[0, g].T.reshape(HEADS_PER_GROUP, SSD_HEAD_DIM, D_STATE)
            sb_ref[hs] = s_scr[1, g].T.reshape(HEADS_PER_GROUP, SSD_HEAD_DIM, D_STATE)


def _ssd(proj, dt_raw, p, n, nb, row_blk0, init):
    has_init = init is not None
    kern = functools.partial(_ssd_kernel, n=n, has_init=has_init)
    col = lambda cb: pl.BlockSpec((n, 512), lambda b: (row_blk0 + b, cb))
    vec = lambda r, w: pl.BlockSpec((r, w), lambda b: (0, 0))
    state_spec = pl.BlockSpec((None, SSD_HEADS, SSD_HEAD_DIM, D_STATE), lambda b: (b, 0, 0, 0))
    in_specs = [
        col(3), col(4), col(5), col(6), col(7),
        pl.BlockSpec((n, 128), lambda b: (row_blk0 + b, 0)),
        vec(3, CONV_DIM), vec(1, CONV_DIM), vec(1, 128), vec(1, 128), vec(1, D_SSD), vec(1, D_SSD),
    ]
    args = [proj, proj, proj, proj, proj, dt_raw, p['conv_w'], p['conv_b'], p['dt_bias'], p['a_log'],
            p['d_skip'], p['ssd_norm_g']]
    y_spec = pl.BlockSpec((n, D_SSD), lambda b: (b, 0))
    y_shape = jax.ShapeDtypeStruct((nb * n, D_SSD), BF16)
    if has_init:
        in_specs += [state_spec, state_spec]
        args += list(init)
        out_specs, out_shape = y_spec, y_shape
    else:
        st_shape = jax.ShapeDtypeStruct((nb, SSD_HEADS, SSD_HEAD_DIM, D_STATE), F32)
        out_specs, out_shape = [y_spec, state_spec, state_spec], [y_shape, st_shape, st_shape]
    return pl.pallas_call(
        kern,
        grid=(nb,),
        in_specs=in_specs,
        out_specs=out_specs,
        out_shape=out_shape,
        scratch_shapes=[
            pltpu.VMEM((n, D_SSD), F32), pltpu.VMEM((n, 512), F32), pltpu.VMEM((n, 128), F32),
            pltpu.VMEM((n, D_SSD), F32), pltpu.VMEM((2, SSD_GROUPS, D_STATE, 512), F32),
        ],
        compiler_params=_cparams(("parallel",)),
        name="ssd_sample" if has_init else "ssd_prompt",
    )(*args)


def _outproj_kernel(*refs, with_router):
    if with_router:
        o_ref, y_ref, x_ref, mod_ref, g_ref, w_ref, rw_ref, xo_ref, h_ref, gates_ref = refs
    else:
        o_ref, y_ref, x_ref, mod_ref, g_ref, w_ref, xo_ref, h_ref = refs
    a = jnp.concatenate([o_ref[...], y_ref[...]], axis=1)
    xn = x_ref[...] + mod_ref[2:3, :] * _dot(a, w_ref[...])
    xo_ref[...] = xn
    h = _rms(xn, g_ref[...]) * (1.0 + mod_ref[4:5, :]) + mod_ref[3:4, :]
    h_ref[...] = h.astype(BF16)
    if with_router:
        h1, h2, _ = _split3(h)
        w1, w2, _ = _split3(rw_ref[...])
        logits = _dot(h1, w1) + _dot(h1, w2) + _dot(h2, w1)
        lane = lax.broadcasted_iota(jnp.int32, logits.shape, 1)
        logits = jnp.where(lane < N_EXPERTS, logits, -jnp.inf)
        e = jnp.exp(logits - jnp.max(logits, axis=-1, keepdims=True))
        probs = e / jnp.sum(e, axis=-1, keepdims=True)
        p1 = jnp.max(probs, axis=-1, keepdims=True)
        i1 = jnp.min(jnp.where(probs == p1, lane, 128), axis=-1, keepdims=True)
        rest = jnp.where(lane == i1, -1.0, probs)
        p2 = jnp.max(rest, axis=-1, keepdims=True)
        i2 = jnp.min(jnp.where(rest == p2, lane, 128), axis=-1, keepdims=True)
        top = jnp.where(lane == i1, p1, 0.0) + jnp.where(lane == i2, p2, 0.0)
        gates_ref[...] = top / (p1 + p2)


def _outproj(o, y, x, mods, g, w_out, router_w):
    tm = 512
    with_router = router_w is not None
    kern = functools.partial(_outproj_kernel, with_router=with_router)
    in_specs = [
        pl.BlockSpec((tm, D_ATTN), lambda i: (i, 0)),
        pl.BlockSpec((tm, D_SSD), lambda i: (i, 0)),
        pl.BlockSpec((tm, D_MODEL), lambda i: (i, 0)),
        pl.BlockSpec((None, 6, D_MODEL), lambda i: (_mod_group(i, tm), 0, 0)),
        pl.BlockSpec((1, D_MODEL), lambda i: (0, 0)),
        pl.BlockSpec((D_MODEL, D_MODEL), lambda i: (0, 0)),
    ]
    args = [o, y, x, mods, g, w_out]
    out_specs = [pl.BlockSpec((tm, D_MODEL), lambda i: (i, 0)), pl.BlockSpec((tm, D_MODEL), lambda i: (i, 0))]
    out_shape = [jax.ShapeDtypeStruct((T_ALL, D_MODEL), F32), jax.ShapeDtypeStruct((T_ALL, D_MODEL), BF16)]
    if with_router:
        in_specs.append(pl.BlockSpec((D_MODEL, 128), lambda i: (0, 0)))
        args.append(router_w)
        out_specs.append(pl.BlockSpec((tm, 128), lambda i: (i, 0)))
        out_shape.append(jax.ShapeDtypeStruct((T_ALL, 128), F32))
    return pl.pallas_call(
        kern,
        grid=(T_ALL // tm,),
        in_specs=in_specs,
        out_specs=out_specs,
        out_shape=out_shape,
        compiler_params=_cparams(("parallel",)),
        name="outproj_router" if with_router else "outproj",
    )(*args)


def _ffn_kernel(*refs, gated, final_norm, tiles_per_expert):
    refs = list(refs)
    h_ref, x_ref, mod_ref = refs[:3]
    k = 3
    gates_ref = None
    if gated:
        gates_ref = refs[k]
        k += 1
    wg_ref, wu_ref, wd_ref = refs[k:k + 3]
    k += 3
    fg_ref = None
    if final_norm:
        fg_ref = refs[k]
        k += 1
    out_ref, acc_ref = refs[k], refs[k + 1]
    f = pl.program_id(1)

    @pl.when(f == 0)
    def _():
        acc_ref[...] = jnp.zeros_like(acc_ref)

    h = h_ref[...]
    hid = _silu(_dot(h, wg_ref[...])) * _dot(h, wu_ref[...])
    if gated:
        gt = gates_ref[...]
        lane = lax.broadcasted_iota(jnp.int32, gt.shape, 1)
        hid = hid * jnp.sum(jnp.where(lane == f // tiles_per_expert, gt, 0.0), axis=-1, keepdims=True)
    acc_ref[...] += _dot(hid.astype(BF16), wd_ref[...])

    @pl.when(f == pl.num_programs(1) - 1)
    def _():
        xo = x_ref[...] + mod_ref[5:6, :] * acc_ref[...]
        if final_norm:
            xo = _rms(xo, fg_ref[...])
        out_ref[...] = xo


def _ffn(h, x, mods, wg, wu, wd, gates, final_g):
    tm = 512
    gated = gates is not None
    final_norm = final_g is not None
    tf = 512
    tpe = F_EXPERT // tf
    kern = functools.partial(_ffn_kernel, gated=gated, final_norm=final_norm, tiles_per_expert=tpe)
    in_specs = [
        pl.BlockSpec((tm, D_MODEL), lambda i, f: (i, 0)),
        pl.BlockSpec((tm, D_MODEL), lambda i, f: (i, 0)),
        pl.BlockSpec((None, 6, D_MODEL), lambda i, f: (_mod_group(i, tm), 0, 0)),
    ]
    args = [h, x, mods]
    if gated:
        nf = N_EXPERTS * tpe
        in_specs.append(pl.BlockSpec((tm, 128), lambda i, f: (i, 0)))
        args.append(gates)
        in_specs += [
            pl.BlockSpec((None, D_MODEL, tf), lambda i, f: (f // tpe, 0, f % tpe)),
            pl.BlockSpec((None, D_MODEL, tf), lambda i, f: (f // tpe, 0, f % tpe)),
            pl.BlockSpec((None, tf, D_MODEL), lambda i, f: (f // tpe, f % tpe, 0)),
        ]
    else:
        nf = F_DENSE // tf
        in_specs += [
            pl.BlockSpec((D_MODEL, tf), lambda i, f: (0, f)),
            pl.BlockSpec((D_MODEL, tf), lambda i, f: (0, f)),
            pl.BlockSpec((tf, D_MODEL), lambda i, f: (f, 0)),
        ]
    args += [wg, wu, wd]
    if final_norm:
        in_specs.append(pl.BlockSpec((1, D_MODEL), lambda i, f: (0, 0)))
        args.append(final_g)
    return pl.pallas_call(
        kern,
        grid=(T_ALL // tm, nf),
        in_specs=in_specs,
        out_specs=pl.BlockSpec((tm, D_MODEL), lambda i, f: (i, 0)),
        out_shape=jax.ShapeDtypeStruct((T_ALL, D_MODEL), F32),
        scratch_shapes=[pltpu.VMEM((tm, D_MODEL), F32)],
        compiler_params=_cparams(("parallel", "arbitrary")),
        name="moe_ffn" if gated else "dense_ffn",
    )(*args)


def _rope_tables():
    n = DEC_SEQ
    rows = n // GRID_W
    t_row = jnp.repeat(jnp.arange(rows, dtype=F32), GRID_W)
    t_col = jnp.tile(jnp.arange(GRID_W, dtype=F32), rows)
    inv = 1.0 / (ROPE_THETA ** (jnp.arange(0, ROT_HALF, 2, dtype=F32) / ROT_HALF))
    ar, ac = t_row[:, None] * inv, t_col[:, None] * inv
    cos = jnp.concatenate([jnp.cos(ar), jnp.cos(ar), jnp.cos(ac), jnp.cos(ac)], axis=-1)
    sin_signed = jnp.concatenate([-jnp.sin(ar), jnp.sin(ar), -jnp.sin(ac), jnp.sin(ac)], axis=-1)
    return cos, sin_signed


def _pad_lanes(v, width=128):
    return jnp.pad(v, ((0, 0), (0, width - v.shape[-1])))


def kernel(x_prompt, x_sample, c, cache_k, cache_v, state_ssm_fwd, state_ssm_bwd, c_ctx, ada_w, ada_b, norm1_g, norm2_g, w_in, q_norm_g, k_norm_g, conv_w, conv_b, a_log_fwd, a_log_bwd, dt_bias_fwd, dt_bias_bwd, d_skip, ssd_norm_g, attn_out_g, w_out, ffn_w_gate, ffn_w_up, ffn_w_down, router_w, moe_w_gate, moe_w_up, moe_w_down, final_norm_g):
    x = jnp.concatenate([x_prompt.reshape(T_PROMPT, D_MODEL), x_sample.reshape(T_SAMPLE, D_MODEL)], axis=0)
    cond = jnp.concatenate([c_ctx[None, :], c, jnp.zeros((N_COND - 1 - DEC_BATCH, D_MODEL), F32)], axis=0)
    mods_all = _ada_mods(cond, ada_w, ada_b).reshape(DEPTH, N_COND, 6, D_MODEL)
    cos, sin_signed = _rope_tables()
    ck = cache_k.reshape(DEC_BATCH, DEPTH, PAST_LEN, KV_DIM)
    cv = cache_v.reshape(DEC_BATCH, DEPTH, PAST_LEN, KV_DIM)

    ks, vs, sfs, sbs = [], [], [], []
    for l in range(DEPTH):
        mods = mods_all[l]
        w_dt = _pad_lanes(w_in[l][:, N_MAIN:])
        proj, dt_raw = _inproj(x, mods, norm1_g[l][None, :], w_in[l], w_dt)

        qg, kg, og = q_norm_g[l][None, :], k_norm_g[l][None, :], attn_out_g[l][None, :]
        o_p, k_p, v_p = _attention_prompt(proj, qg, kg, og)
        o_s = _attention_sample(proj, qg, kg, og, ck, cv, cos, sin_signed, l)

        p = {
            'conv_w': conv_w[l], 'conv_b': conv_b[l][None, :],
            'dt_bias': _pad_lanes(jnp.concatenate([dt_bias_fwd[l], dt_bias_bwd[l]])[None, :]),
            'a_log': _pad_lanes(jnp.concatenate([a_log_fwd[l], a_log_bwd[l]])[None, :]),
            'd_skip': jnp.repeat(d_skip[l], SSD_HEAD_DIM)[None, :],
            'ssd_norm_g': ssd_norm_g[l][None, :],
        }
        y_p, sf, sb = _ssd(proj, dt_raw, p, SEQ, BATCH, 0, None)
        y_s = _ssd(proj, dt_raw, p, DEC_SEQ, DEC_BATCH, T_PROMPT // DEC_SEQ,
                   (state_ssm_fwd[:, l], state_ssm_bwd[:, l]))
        ks.append(k_p.reshape(BATCH, SEQ, N_KV_HEADS, HEAD_DIM))
        vs.append(v_p.reshape(BATCH, SEQ, N_KV_HEADS, HEAD_DIM))
        sfs.append(sf)
        sbs.append(sb)

        o = jnp.concatenate([o_p, o_s], axis=0)
        y = jnp.concatenate([y_p, y_s], axis=0)
        final_g = final_norm_g[None, :] if l == DEPTH - 1 else None
        j = l // 2
        if l % 2 == 0:
            x, h = _outproj(o, y, x, mods, norm2_g[l][None, :], w_out[l].astype(BF16), None)
            x = _ffn(h, x, mods, ffn_w_gate[j].astype(BF16), ffn_w_up[j].astype(BF16),
                     ffn_w_down[j].astype(BF16), None, final_g)
        else:
            x, h, gates = _outproj(o, y, x, mods, norm2_g[l][None, :], w_out[l].astype(BF16),
                                   _pad_lanes(router_w[j]))
            x = _ffn(h, x, mods, moe_w_gate[j].astype(BF16), moe_w_up[j].astype(BF16),
                     moe_w_down[j].astype(BF16), gates, final_g)

    y_prompt = x[:T_PROMPT].reshape(BATCH, SEQ, D_MODEL)
    y_sample = x[T_PROMPT:].reshape(DEC_BATCH, DEC_SEQ, D_MODEL)
    return (y_prompt, y_sample, jnp.stack(ks, axis=1), jnp.stack(vs, axis=1),
            jnp.stack(sfs, axis=1), jnp.stack(sbs, axis=1))
```

```python
import functools

import jax
import jax.numpy as jnp
from jax import lax
from jax.experimental import pallas as pl
from jax.experimental.pallas import tpu as pltpu

F32 = jnp.float32
BF16 = jnp.bfloat16

D_MODEL = 2048
BATCH = 16
SEQ = 256
DEPTH = 2
DEC_BATCH = 2
DEC_SEQ = 1024
PAST_LEN = 512
GRID_W = 64
D_ATTN = 1024
D_SSD = 1024
HEAD_DIM = 128
N_Q_HEADS = 8
N_KV_HEADS = 2
Q_PER_KV = 4
KV_DIM = 256
ROT_HALF = 64
ROPE_THETA = 10000.0
SSD_HEAD_DIM = 64
SSD_HEADS = 16
SSD_GROUPS = 2
HEADS_PER_GROUP = 8
D_STATE = 128
CONV_DIM = 1536
CHUNK = 128
N_MAIN = 4096
F_DENSE = 5632
N_EXPERTS = 8
F_EXPERT = 1024
EPS = 1e-6

T_PROMPT = BATCH * SEQ
T_SAMPLE = DEC_BATCH * DEC_SEQ
T_ALL = T_PROMPT + T_SAMPLE
N_COND = 16

VMEM_LIMIT = 56 * 1024 * 1024


def _cparams(sem):
    return pltpu.CompilerParams(dimension_semantics=sem, vmem_limit_bytes=VMEM_LIMIT)


def _mod_group(i, tm):
    return jnp.maximum(0, (i * tm - T_PROMPT + DEC_SEQ) // DEC_SEQ)


def _silu(x):
    return x * jax.nn.sigmoid(x)


def _rms(x, g):
    ms = jnp.mean(x * x, axis=-1, keepdims=True)
    return x * lax.rsqrt(ms + EPS) * g


def _dot(a, b):
    return jnp.dot(a, b, preferred_element_type=F32)


def _dot_nt(a, b):
    return lax.dot_general(a, b, (((1,), (1,)), ((), ())), preferred_element_type=F32)


def _split3(x):
    hi = x.astype(BF16)
    r1 = x - hi.astype(F32)
    mid = r1.astype(BF16)
    r2 = r1 - mid.astype(F32)
    return hi, mid, r2.astype(BF16)


def _ada_kernel(c_ref, w_ref, b_ref, o_ref):
    s = _silu(c_ref[...]).astype(BF16)
    o_ref[...] = _dot(s, w_ref[...].astype(BF16)) + b_ref[...]


def _ada_mods(cond, ada_w, ada_b):
    tn = 1024
    n_out = 6 * D_MODEL
    return pl.pallas_call(
        _ada_kernel,
        grid=(DEPTH, n_out // tn),
        in_specs=[
            pl.BlockSpec((N_COND, D_MODEL), lambda l, j: (0, 0)),
            pl.BlockSpec((None, D_MODEL, tn), lambda l, j: (l, 0, j)),
            pl.BlockSpec((None, 1, tn), lambda l, j: (l, 0, j)),
        ],
        out_specs=pl.BlockSpec((None, N_COND, tn), lambda l, j: (l, 0, j)),
        out_shape=jax.ShapeDtypeStruct((DEPTH, N_COND, n_out), F32),
        compiler_params=_cparams(("parallel", "parallel")),
        name="ada_mods",
    )(cond, ada_w, ada_b.reshape(DEPTH, 1, n_out))


def _inproj_kernel(x_ref, mod_ref, g_ref, w_ref, wdt_ref, proj_ref, dt_ref, h_scr):
    @pl.when(pl.program_id(1) == 0)
    def _():
        h = _rms(x_ref[...], g_ref[...]) * (1.0 + mod_ref[1:2, :]) + mod_ref[0:1, :]
        hb = h.astype(BF16)
        h_scr[...] = hb
        dt_ref[...] = _dot(hb, wdt_ref[...].astype(BF16))

    proj_ref[...] = _dot(h_scr[...], w_ref[...].astype(BF16))


def _inproj(x, mods, g, w_in, w_dt):
    tm, tn = 1024, 512
    return pl.pallas_call(
        _inproj_kernel,
        grid=(T_ALL // tm, N_MAIN // tn),
        in_specs=[
            pl.BlockSpec((tm, D_MODEL), lambda i, j: (i, 0)),
            pl.BlockSpec((None, 6, D_MODEL), lambda i, j: (_mod_group(i, tm), 0, 0)),
            pl.BlockSpec((1, D_MODEL), lambda i, j: (0, 0)),
            pl.BlockSpec((D_MODEL, tn), lambda i, j: (0, j)),
            pl.BlockSpec((D_MODEL, 128), lambda i, j: (0, 0)),
        ],
        out_specs=[
            pl.BlockSpec((tm, tn), lambda i, j: (i, j)),
            pl.BlockSpec((tm, 128), lambda i, j: (i, 0)),
        ],
        out_shape=[
            jax.ShapeDtypeStruct((T_ALL, N_MAIN), F32),
            jax.ShapeDtypeStruct((T_ALL, 128), F32),
        ],
        scratch_shapes=[pltpu.VMEM((tm, D_MODEL), BF16)],
        compiler_params=_cparams(("parallel", "arbitrary")),
        name="inproj",
    )(x, mods, g, w_in, w_dt)


def _rope(x, cos, sin_signed):
    lane = lax.broadcasted_iota(jnp.int32, x.shape, 1)
    first = (lane // (ROT_HALF // 2)) % 2 == 0
    swapped = jnp.where(first, pltpu.roll(x, HEAD_DIM - ROT_HALF // 2, 1), pltpu.roll(x, ROT_HALF // 2, 1))
    return x * cos + swapped * sin_signed


def _attn_kernel(*refs, nk_new, has_ctx):
    if has_ctx:
        (q_ref, kv_ref, qg_ref, kg_ref, og_ref, ck_ref, cv_ref, cq_ref, sq_ref, ckk_ref, skk_ref,
         o_ref, kb_scr, vb_scr, o_scr) = refs
    else:
        (q_ref, kv_ref, qg_ref, kg_ref, og_ref, o_ref, ko_ref, vo_ref, kb_scr, vb_scr, o_scr) = refs

    @pl.when(pl.program_id(1) == 0)
    def _():
        for g in range(N_KV_HEADS):
            sl = slice(g * HEAD_DIM, (g + 1) * HEAD_DIM)
            kn = _rms(kv_ref[:, sl], kg_ref[...])
            if has_ctx:
                kb_scr[0:nk_new, sl] = _rope(kn, ckk_ref[...], skk_ref[...]).astype(BF16)
                kb_scr[nk_new:, sl] = ck_ref[:, sl].astype(BF16)
            else:
                ko_ref[:, sl] = kn
                kb_scr[:, sl] = kn.astype(BF16)
        v = kv_ref[:, KV_DIM:]
        vb_scr[0:nk_new, :] = v.astype(BF16)
        if has_ctx:
            vb_scr[nk_new:, :] = cv_ref[...].astype(BF16)
        else:
            vo_ref[...] = v

    scale = HEAD_DIM ** -0.5
    for h in range(N_Q_HEADS):
        g = h // Q_PER_KV
        sl = slice(h * HEAD_DIM, (h + 1) * HEAD_DIM)
        gsl = slice(g * HEAD_DIM, (g + 1) * HEAD_DIM)
        qn = _rms(q_ref[:, sl], qg_ref[...])
        if has_ctx:
            qn = _rope(qn, cq_ref[...], sq_ref[...])
        s = _dot_nt(qn.astype(BF16), kb_scr[:, gsl]) * scale
        e = jnp.exp(s - jnp.max(s, axis=-1, keepdims=True))
        l = jnp.sum(e, axis=-1, keepdims=True)
        o_scr[:, sl] = _dot(e.astype(BF16), vb_scr[:, gsl]) / l
    o_ref[...] = _rms(o_scr[...], og_ref[...]).astype(BF16)


def _attention_prompt(proj, qg, kg, og):
    n = SEQ
    kern = functools.partial(_attn_kernel, nk_new=n, has_ctx=False)
    vec = lambda w: pl.BlockSpec((1, w), lambda b, i: (0, 0))
    return pl.pallas_call(
        kern,
        grid=(BATCH, 1),
        in_specs=[
            pl.BlockSpec((n, D_ATTN), lambda b, i: (b, 0)),
            pl.BlockSpec((n, 2 * KV_DIM), lambda b, i: (b, 2)),
            vec(HEAD_DIM), vec(HEAD_DIM), vec(D_ATTN),
        ],
        out_specs=[
            pl.BlockSpec((n, D_ATTN), lambda b, i: (b, 0)),
            pl.BlockSpec((None, n, KV_DIM), lambda b, i: (b, 0, 0)),
            pl.BlockSpec((None, n, KV_DIM), lambda b, i: (b, 0, 0)),
        ],
        out_shape=[
            jax.ShapeDtypeStruct((T_PROMPT, D_ATTN), BF16),
            jax.ShapeDtypeStruct((BATCH, n, KV_DIM), F32),
            jax.ShapeDtypeStruct((BATCH, n, KV_DIM), F32),
        ],
        scratch_shapes=[
            pltpu.VMEM((n, KV_DIM), BF16), pltpu.VMEM((n, KV_DIM), BF16), pltpu.VMEM((n, D_ATTN), F32),
        ],
        compiler_params=_cparams(("parallel", "arbitrary")),
        name="attn_prompt",
    )(proj, proj, qg, kg, og)


def _attention_sample(proj, qg, kg, og, ck, cv, cos, sin_signed, layer):
    n, tq = DEC_SEQ, 512
    nq = n // tq
    nk = n + PAST_LEN
    kern = functools.partial(_attn_kernel, nk_new=n, has_ctx=True)
    vec = lambda w: pl.BlockSpec((1, w), lambda b, i: (0, 0))
    q_blk0 = T_PROMPT // tq
    kv_blk0 = T_PROMPT // n
    return pl.pallas_call(
        kern,
        grid=(DEC_BATCH, nq),
        in_specs=[
            pl.BlockSpec((tq, D_ATTN), lambda b, i: (q_blk0 + b * nq + i, 0)),
            pl.BlockSpec((n, 2 * KV_DIM), lambda b, i: (kv_blk0 + b, 2)),
            vec(HEAD_DIM), vec(HEAD_DIM), vec(D_ATTN),
            pl.BlockSpec((None, None, PAST_LEN, KV_DIM), lambda b, i: (b, layer, 0, 0)),
            pl.BlockSpec((None, None, PAST_LEN, KV_DIM), lambda b, i: (b, layer, 0, 0)),
            pl.BlockSpec((tq, HEAD_DIM), lambda b, i: (i, 0)),
            pl.BlockSpec((tq, HEAD_DIM), lambda b, i: (i, 0)),
            pl.BlockSpec((n, HEAD_DIM), lambda b, i: (0, 0)),
            pl.BlockSpec((n, HEAD_DIM), lambda b, i: (0, 0)),
        ],
        out_specs=pl.BlockSpec((tq, D_ATTN), lambda b, i: (b * nq + i, 0)),
        out_shape=jax.ShapeDtypeStruct((T_SAMPLE, D_ATTN), BF16),
        scratch_shapes=[
            pltpu.VMEM((nk, KV_DIM), BF16), pltpu.VMEM((nk, KV_DIM), BF16), pltpu.VMEM((tq, D_ATTN), F32),
        ],
        compiler_params=_cparams(("parallel", "arbitrary")),
        name="attn_sample",
    )(proj, proj, qg, kg, og, ck, cv, cos, sin_signed, cos, sin_signed)


def _conv_silu(x, w, b):
    n = x.shape[0]
    row = lax.broadcasted_iota(jnp.int32, (n, 1), 0)
    prev = jnp.where(row == 0, 0.0, pltpu.roll(x, 1, 0))
    nxt = jnp.where(row == n - 1, 0.0, pltpu.roll(x, n - 1, 0))
    return _silu(prev * w[0:1, :] + x * w[1:2, :] + nxt * w[2:3, :] + b)


def _softplus(x):
    return jnp.maximum(x, 0.0) + jnp.log1p(jnp.exp(-jnp.abs(x)))


def _pair_cols(m, li, lo_lanes):
    return jnp.where(lo_lanes, m[:, li:li + 1], m[:, li + 1:li + 2])


def _ssd_kernel(*refs, n, has_init):
    if has_init:
        (za_ref, zb_ref, xa_ref, xb_ref, bc_ref, dt_ref, cw_ref, cb_ref, dtb_ref, alog_ref, dsk_ref, ng_ref,
         sf0_ref, sb0_ref, y_ref, xc_scr, bcc_scr, dts_scr, y_scr, s_scr) = refs
    else:
        (za_ref, zb_ref, xa_ref, xb_ref, bc_ref, dt_ref, cw_ref, cb_ref, dtb_ref, alog_ref, dsk_ref, ng_ref,
         y_ref, sf_ref, sb_ref, xc_scr, bcc_scr, dts_scr, y_scr, s_scr) = refs
    nc = n // CHUNK
    gw = HEADS_PER_GROUP * SSD_HEAD_DIM

    xc_scr[:, 0:gw] = _conv_silu(xa_ref[...], cw_ref[:, 0:gw], cb_ref[:, 0:gw])
    xc_scr[:, gw:] = _conv_silu(xb_ref[...], cw_ref[:, gw:2 * gw], cb_ref[:, gw:2 * gw])
    bcc_scr[...] = _conv_silu(bc_ref[...], cw_ref[:, 2 * gw:], cb_ref[:, 2 * gw:])
    y_scr[...] = xc_scr[...] * dsk_ref[...]
    dts_scr[...] = _softplus(dt_ref[...] + dtb_ref[...])
    for g in range(SSD_GROUPS):
        if has_init:
            s_scr[0, g] = sf0_ref[g * HEADS_PER_GROUP:(g + 1) * HEADS_PER_GROUP].reshape(gw, D_STATE).T
            s_scr[1, g] = sb0_ref[g * HEADS_PER_GROUP:(g + 1) * HEADS_PER_GROUP].reshape(gw, D_STATE).T
        else:
            s_scr[0, g] = jnp.zeros((D_STATE, gw), F32)
            s_scr[1, g] = jnp.zeros((D_STATE, gw), F32)

    a_row = -jnp.exp(alog_ref[...])
    ri = lax.broadcasted_iota(jnp.int32, (CHUNK, CHUNK), 0)
    ci = lax.broadcasted_iota(jnp.int32, (CHUNK, CHUNK), 1)
    lo_lanes = ci < SSD_HEAD_DIM
    lo_row = lo_lanes[0:1, :]

    def scan_chunk(dirn, c):
        mask = (ci <= ri) if dirn == 0 else (ci >= ri)
        tri = mask.astype(BF16)
        lane0 = dirn * SSD_HEADS
        r0 = pl.multiple_of(c * CHUNK, CHUNK)
        rows = pl.ds(r0, CHUNK)
        dt = dts_scr[rows, :]
        d1, d2, d3 = _split3(dt * a_row)
        cs = _dot(tri, d1) + _dot(tri, d2) + _dot(tri, d3)
        cs_t = cs.T
        dt_t = dt.T
        total = cs[CHUNK - 1:CHUNK, :] if dirn == 0 else cs[0:1, :]
        e_in = jnp.exp(cs)
        to_end = jnp.exp(total - cs) * dt
        dec = jnp.exp(total)
        for g in range(SSD_GROUPS):
            bm = bcc_scr[rows, g * D_STATE:(g + 1) * D_STATE]
            cm = bcc_scr[rows, (SSD_GROUPS + g) * D_STATE:(SSD_GROUPS + g + 1) * D_STATE].astype(BF16)
            gmat = _dot_nt(cm, bm.astype(BF16))
            st = s_scr[dirn, g]
            y_off = _dot(cm, st.astype(BF16))
            xs_parts, dec_parts = [], []
            for j in range(HEADS_PER_GROUP // 2):
                li = lane0 + g * HEADS_PER_GROUP + 2 * j
                lanes = slice((g * 4 + j) * 128, (g * 4 + j + 1) * 128)
                a_parts = []
                for hh in range(2):
                    diff = cs[:, li + hh:li + hh + 1] - cs_t[li + hh:li + hh + 1, :]
                    w = jnp.exp(jnp.where(mask, diff, -jnp.inf)) * dt_t[li + hh:li + hh + 1, :]
                    a_parts.append((gmat * w).astype(BF16))
                x_pair = xc_scr[rows, lanes]
                x2 = jnp.concatenate([jnp.where(lo_lanes, x_pair, 0.0), jnp.where(lo_lanes, 0.0, x_pair)],
                                     axis=0).astype(BF16)
                y_diag = _dot(jnp.concatenate(a_parts, axis=1), x2)
                y_pair = y_diag + y_off[:, j * 128:(j + 1) * 128] * _pair_cols(e_in, li, lo_lanes)
                y_scr[rows, lanes] = y_scr[rows, lanes] + y_pair
                xs_parts.append((x_pair * _pair_cols(to_end, li, lo_lanes)).astype(BF16))
                dec_parts.append(_pair_cols(dec, li, lo_row))
            ds = _dot(bm.T.astype(BF16), jnp.concatenate(xs_parts, axis=1))
            s_scr[dirn, g] = st * jnp.concatenate(dec_parts, axis=1) + ds

    def body(i, carry):
        scan_chunk(0, i)
        scan_chunk(1, nc - 1 - i)
        return carry

    lax.fori_loop(0, nc, body, 0)

    def finish(c, carry):
        rows = pl.ds(pl.multiple_of(c * CHUNK, CHUNK), CHUNK)
        ya = y_scr[rows, 0:gw] * _silu(za_ref[rows, :])
        yb = y_scr[rows, gw:] * _silu(zb_ref[rows, :])
        ms = (jnp.sum(ya * ya, axis=-1, keepdims=True) + jnp.sum(yb * yb, axis=-1, keepdims=True)) / D_SSD
        inv = lax.rsqrt(ms + EPS)
        y_ref[rows, 0:gw] = (ya * inv * ng_ref[:, 0:gw]).astype(BF16)
        y_ref[rows, gw:] = (yb * inv * ng_ref[:, gw:]).astype(BF16)
        return carry

    lax.fori_loop(0, nc, finish, 0)

    if not has_init:
        for g in range(SSD_GROUPS):
            hs = slice(g * HEADS_PER_GROUP, (g + 1) * HEADS_PER_GROUP)
            sf_ref[hs] = s_scr[0, g].T.reshape(HEADS_PER_GROUP, SSD_HEAD_DIM, D_STATE)
            sb_ref[hs] = s_scr[1, g].T.reshape(HEADS_PER_GROUP, SSD_HEAD_DIM, D_STATE)


def _ssd(proj, dt_raw, p, n, nb, row_blk0, init):
    has_init = init is not None
    kern = functools.partial(_ssd_kernel, n=n, has_init=has_init)
    col = lambda cb: pl.BlockSpec((n, 512), lambda b: (row_blk0 + b, cb))
    vec = lambda r, w: pl.BlockSpec((r, w), lambda b: (0, 0))
    state_spec = pl.BlockSpec((None, SSD_HEADS, SSD_HEAD_DIM, D_STATE), lambda b: (b, 0, 0, 0))
    in_specs = [
        col(3), col(4), col(5), col(6), col(7),
        pl.BlockSpec((n, 128), lambda b: (row_blk0 + b, 0)),
        vec(3, CONV_DIM), vec(1, CONV_DIM), vec(1, 128), vec(1, 128), vec(1, D_SSD), vec(1, D_SSD),
    ]
    args = [proj, proj, proj, proj, proj, dt_raw, p['conv_w'], p['conv_b'], p['dt_bias'], p['a_log'],
            p['d_skip'], p['ssd_norm_g']]
    y_spec = pl.BlockSpec((n, D_SSD), lambda b: (b, 0))
    y_shape = jax.ShapeDtypeStruct((nb * n, D_SSD), BF16)
    if has_init:
        in_specs += [state_spec, state_spec]
        args += list(init)
        out_specs, out_shape = y_spec, y_shape
    else:
        st_shape = jax.ShapeDtypeStruct((nb, SSD_HEADS, SSD_HEAD_DIM, D_STATE), F32)
        out_specs, out_shape = [y_spec, state_spec, state_spec], [y_shape, st_shape, st_shape]
    return pl.pallas_call(
        kern,
        grid=(nb,),
        in_specs=in_specs,
        out_specs=out_specs,
        out_shape=out_shape,
        scratch_shapes=[
            pltpu.VMEM((n, D_SSD), F32), pltpu.VMEM((n, 512), F32), pltpu.VMEM((n, 128), F32),
            pltpu.VMEM((n, D_SSD), F32), pltpu.VMEM((2, SSD_GROUPS, D_STATE, 512), F32),
        ],
        compiler_params=_cparams(("parallel",)),
        name="ssd_sample" if has_init else "ssd_prompt",
    )(*args)


def _outproj_kernel(*refs, with_router):
    if with_router:
        o_ref, y_ref, x_ref, mod_ref, g_ref, w_ref, rw_ref, xo_ref, h_ref, meta_ref, cnt_ref, carry_scr = refs
    else:
        o_ref, y_ref, x_ref, mod_ref, g_ref, w_ref, xo_ref, h_ref = refs
    a = jnp.concatenate([o_ref[...], y_ref[...]], axis=1)
    xn = x_ref[...] + mod_ref[2:3, :] * _dot(a, w_ref[...])
    xo_ref[...] = xn
    h = _rms(xn, g_ref[...]) * (1.0 + mod_ref[4:5, :]) + mod_ref[3:4, :]
    h_ref[...] = h.astype(h_ref.dtype)
    if with_router:
        tm = h.shape[0]

        @pl.when(pl.program_id(0) == 0)
        def _():
            carry_scr[...] = jnp.zeros_like(carry_scr)

        h1, h2, _ = _split3(h)
        w1, w2, _ = _split3(rw_ref[...])
        logits = _dot_nt(w1, h1) + _dot_nt(w2, h1) + _dot_nt(w1, h2)
        row = lax.broadcasted_iota(jnp.int32, logits.shape, 0)
        logits = jnp.where(row < N_EXPERTS, logits, -jnp.inf)
        e = jnp.exp(logits - jnp.max(logits, axis=0, keepdims=True))
        probs = e / jnp.sum(e, axis=0, keepdims=True)
        p1 = jnp.max(probs, axis=0, keepdims=True)
        i1 = jnp.min(jnp.where(probs == p1, row, 16), axis=0, keepdims=True)
        rest = jnp.where(row == i1, -1.0, probs)
        p2 = jnp.max(rest, axis=0, keepdims=True)
        i2 = jnp.min(jnp.where(rest == p2, row, 16), axis=0, keepdims=True)
        hit1, hit2 = row == i1, row == i2
        onehot = jnp.where(hit1 | hit2, 1.0, 0.0)
        ti = lax.broadcasted_iota(jnp.int32, (tm, tm), 0)
        tj = lax.broadcasted_iota(jnp.int32, (tm, tm), 1)
        before = jnp.where(ti < tj, 1.0, 0.0).astype(BF16)
        rank = carry_scr[:, 0:1] + _dot(onehot.astype(BF16), before)
        r1 = jnp.sum(jnp.where(hit1, rank, 0.0), axis=0, keepdims=True)
        r2 = jnp.sum(jnp.where(hit2, rank, 0.0), axis=0, keepdims=True)
        carry_scr[...] = carry_scr[...] + jnp.sum(onehot, axis=1, keepdims=True)
        cnt_ref[...] = carry_scr[...]
        r8 = lax.broadcasted_iota(jnp.int32, (8, tm), 0)
        vals = [p1 / (p1 + p2), p2 / (p1 + p2), i1.astype(F32), i2.astype(F32), r1, r2]
        meta = jnp.zeros((8, tm), F32)
        for k, v in enumerate(vals):
            meta = jnp.where(r8 == k, v, meta)
        meta_ref[...] = meta


def _outproj(o, y, x, mods, g, w_out, router_wt):
    tm = 512
    with_router = router_wt is not None
    kern = functools.partial(_outproj_kernel, with_router=with_router)
    in_specs = [
        pl.BlockSpec((tm, D_ATTN), lambda i: (i, 0)),
        pl.BlockSpec((tm, D_SSD), lambda i: (i, 0)),
        pl.BlockSpec((tm, D_MODEL), lambda i: (i, 0)),
        pl.BlockSpec((None, 6, D_MODEL), lambda i: (_mod_group(i, tm), 0, 0)),
        pl.BlockSpec((1, D_MODEL), lambda i: (0, 0)),
        pl.BlockSpec((D_MODEL, D_MODEL), lambda i: (0, 0)),
    ]
    args = [o, y, x, mods, g, w_out]
    out_specs = [pl.BlockSpec((tm, D_MODEL), lambda i: (i, 0)), pl.BlockSpec((tm, D_MODEL), lambda i: (i, 0))]
    out_shape = [jax.ShapeDtypeStruct((T_ALL, D_MODEL), F32),
                 jax.ShapeDtypeStruct((T_ALL, D_MODEL), F32 if with_router else BF16)]
    scratch = []
    if with_router:
        in_specs.append(pl.BlockSpec((16, D_MODEL), lambda i: (0, 0)))
        args.append(router_wt)
        out_specs += [pl.BlockSpec((8, tm), lambda i: (0, i)), pl.BlockSpec((16, 128), lambda i: (0, 0))]
        out_shape += [jax.ShapeDtypeStruct((8, T_ALL), F32), jax.ShapeDtypeStruct((16, 128), F32)]
        scratch = [pltpu.VMEM((16, 128), F32)]
    return pl.pallas_call(
        kern,
        grid=(T_ALL // tm,),
        in_specs=in_specs,
        out_specs=out_specs,
        out_shape=out_shape,
        scratch_shapes=scratch,
        compiler_params=_cparams(("arbitrary",)),
        name="outproj_router" if with_router else "outproj",
    )(*args)


MOE_ROWS = 2 * T_ALL
MOE_TILE = 256
MOE_TILES = MOE_ROWS // MOE_TILE
MOE_VISITS = MOE_TILES + N_EXPERTS - 1


def _row_copy(src, s, dst, d, sem):
    return pltpu.make_async_copy(src.at[pl.ds(s, 1)], dst.at[pl.ds(d, 1)], sem)


def _dispatch_kernel(p1_ref, p2_ref, h_ref, xs_ref, sem):
    tm = h_ref.shape[0]

    def issue(r, c):
        _row_copy(h_ref, r, xs_ref, p1_ref[0, 0, r], sem.at[0]).start()
        _row_copy(h_ref, r, xs_ref, p2_ref[0, 0, r], sem.at[1]).start()
        return c

    lax.fori_loop(0, tm, issue, 0, unroll=8)
    pltpu.make_async_copy(h_ref, xs_ref.at[pl.ds(0, tm)], sem.at[0]).wait()
    pltpu.make_async_copy(h_ref, xs_ref.at[pl.ds(0, tm)], sem.at[1]).wait()


def _dispatch(h, pos1, pos2):
    tm = 512
    nt = T_ALL // tm
    idx = lambda: pl.BlockSpec((1, 1, tm), lambda i: (i, 0, 0), memory_space=pltpu.SMEM)
    return pl.pallas_call(
        _dispatch_kernel,
        grid=(nt,),
        in_specs=[idx(), idx(), pl.BlockSpec((tm, D_MODEL), lambda i: (i, 0))],
        out_specs=pl.BlockSpec(memory_space=pl.ANY),
        out_shape=jax.ShapeDtypeStruct((MOE_ROWS, D_MODEL), F32),
        scratch_shapes=[pltpu.SemaphoreType.DMA((2,))],
        compiler_params=_cparams(("arbitrary",)),
        name="moe_dispatch",
    )(pos1.reshape(nt, 1, tm), pos2.reshape(nt, 1, tm), h)


def _experts_kernel(vt_ref, ve_ref, nv_ref, lo_ref, hi_ref, xs_ref, wg_ref, wu_ref, wd_ref, y_ref):
    v = pl.program_id(0)

    @pl.when(v < nv_ref[0])
    def _():
        e = ve_ref[v]
        x = xs_ref[...].astype(BF16)
        hid = _silu(_dot(x, wg_ref[...])) * _dot(x, wu_ref[...])
        y = _dot(hid.astype(BF16), wd_ref[...])
        row = vt_ref[v] * MOE_TILE + lax.broadcasted_iota(jnp.int32, (MOE_TILE, 1), 0)
        mine = (row >= lo_ref[e]) & (row < hi_ref[e])
        first_visit = (v == 0) | (vt_ref[jnp.maximum(v - 1, 0)] != vt_ref[v])

        @pl.when(first_visit)
        def _():
            y_ref[...] = jnp.where(mine, y, 0.0)

        @pl.when(jnp.logical_not(first_visit))
        def _():
            y_ref[...] = jnp.where(mine, y, y_ref[...])


def _experts(xs, wg, wu, wd, vt, ve, nv, lo, hi):
    grid_spec = pltpu.PrefetchScalarGridSpec(
        num_scalar_prefetch=5,
        grid=(MOE_VISITS,),
        in_specs=[
            pl.BlockSpec((MOE_TILE, D_MODEL), lambda v, vt, ve, nv, lo, hi: (vt[v], 0)),
            pl.BlockSpec((None, D_MODEL, F_EXPERT), lambda v, vt, ve, nv, lo, hi: (ve[v], 0, 0)),
            pl.BlockSpec((None, D_MODEL, F_EXPERT), lambda v, vt, ve, nv, lo, hi: (ve[v], 0, 0)),
            pl.BlockSpec((None, F_EXPERT, D_MODEL), lambda v, vt, ve, nv, lo, hi: (ve[v], 0, 0)),
        ],
        out_specs=pl.BlockSpec((MOE_TILE, D_MODEL), lambda v, vt, ve, nv, lo, hi: (vt[v], 0)),
    )
    return pl.pallas_call(
        _experts_kernel,
        grid_spec=grid_spec,
        out_shape=jax.ShapeDtypeStruct((MOE_ROWS, D_MODEL), F32),
        compiler_params=_cparams(("arbitrary",)),
        name="moe_experts",
    )(vt, ve, nv, lo, hi, xs, wg, wu, wd)


def _combine_kernel(p1c_ref, p2c_ref, p1n_ref, p2n_ref, y_hbm, x_ref, mod_ref, gate_ref, fg_ref, out_ref,
                    ya_buf, yb_buf, sem, *, final_norm):
    i = pl.program_id(0)
    n = pl.num_programs(0)
    tm = x_ref.shape[0]
    slot = i % 2

    def gather(pa_ref, pb_ref, s):
        def issue(r, c):
            _row_copy(y_hbm, pa_ref[0, 0, r], ya_buf.at[s], r, sem.at[0, s]).start()
            _row_copy(y_hbm, pb_ref[0, 0, r], yb_buf.at[s], r, sem.at[1, s]).start()
            return c

        lax.fori_loop(0, tm, issue, 0, unroll=8)

    @pl.when(i == 0)
    def _():
        gather(p1c_ref, p2c_ref, 0)

    @pl.when(i + 1 < n)
    def _():
        gather(p1n_ref, p2n_ref, 1 - slot)

    pltpu.make_async_copy(y_hbm.at[pl.ds(0, tm)], ya_buf.at[slot], sem.at[0, slot]).wait()
    pltpu.make_async_copy(y_hbm.at[pl.ds(0, tm)], yb_buf.at[slot], sem.at[1, slot]).wait()
    g = gate_ref[...]
    mix = g[:, 0:1] * ya_buf[slot] + g[:, 1:2] * yb_buf[slot]
    xo = x_ref[...] + mod_ref[5:6, :] * mix
    if final_norm:
        xo = _rms(xo, fg_ref[...])
    out_ref[...] = xo


def _combine(y, x, mods, gate_cols, pos1, pos2, final_g):
    tm = 256
    nt = T_ALL // tm
    final_norm = final_g is not None
    if not final_norm:
        final_g = jnp.ones((1, D_MODEL), F32)
    cur = lambda: pl.BlockSpec((1, 1, tm), lambda i: (i, 0, 0), memory_space=pltpu.SMEM)
    nxt = lambda: pl.BlockSpec((1, 1, tm), lambda i: (jnp.minimum(i + 1, nt - 1), 0, 0), memory_space=pltpu.SMEM)
    p1, p2 = pos1.reshape(nt, 1, tm), pos2.reshape(nt, 1, tm)
    return pl.pallas_call(
        functools.partial(_combine_kernel, final_norm=final_norm),
        grid=(nt,),
        in_specs=[
            cur(), cur(), nxt(), nxt(),
            pl.BlockSpec(memory_space=pl.ANY),
            pl.BlockSpec((tm, D_MODEL), lambda i: (i, 0)),
            pl.BlockSpec((None, 6, D_MODEL), lambda i: (_mod_group(i, tm), 0, 0)),
            pl.BlockSpec((tm, 128), lambda i: (i, 0)),
            pl.BlockSpec((1, D_MODEL), lambda i: (0, 0)),
        ],
        out_specs=pl.BlockSpec((tm, D_MODEL), lambda i: (i, 0)),
        out_shape=jax.ShapeDtypeStruct((T_ALL, D_MODEL), F32),
        scratch_shapes=[
            pltpu.VMEM((2, tm, D_MODEL), F32), pltpu.VMEM((2, tm, D_MODEL), F32),
            pltpu.SemaphoreType.DMA((2, 2)),
        ],
        compiler_params=_cparams(("arbitrary",)),
        name="moe_combine",
    )(p1, p2, p1, p2, y, x, mods, gate_cols, final_g)


def _route_plan(meta, counts):
    i1, i2 = meta[2].astype(jnp.int32), meta[3].astype(jnp.int32)
    r1, r2 = meta[4].astype(jnp.int32), meta[5].astype(jnp.int32)
    cnt = counts[:N_EXPERTS, 0].astype(jnp.int32)
    hi = jnp.cumsum(cnt)
    lo = hi - cnt
    ex = jnp.arange(N_EXPERTS, dtype=jnp.int32)
    pos1 = jnp.sum(jnp.where(i1[:, None] == ex[None, :], lo[None, :], 0), axis=1) + r1
    pos2 = jnp.sum(jnp.where(i2[:, None] == ex[None, :], lo[None, :], 0), axis=1) + r2
    first_tile = lo // MOE_TILE
    n_vis_e = jnp.where(cnt > 0, (hi - 1) // MOE_TILE - first_tile + 1, 0)
    vis_hi = jnp.cumsum(n_vis_e)
    vis_lo = vis_hi - n_vis_e
    nv = vis_hi[-1]
    v = jnp.minimum(jnp.arange(MOE_VISITS, dtype=jnp.int32), nv - 1)
    ve = jnp.minimum(jnp.sum(v[:, None] >= vis_hi[None, :], axis=1), N_EXPERTS - 1).astype(jnp.int32)
    pick = lambda tab: jnp.sum(jnp.where(ve[:, None] == ex[None, :], tab[None, :], 0), axis=1)
    vt = (pick(first_tile) + v - pick(vis_lo)).astype(jnp.int32)
    return pos1, pos2, vt, ve, nv.reshape(1).astype(jnp.int32), lo.astype(jnp.int32), hi.astype(jnp.int32)


def _ffn_kernel(*refs, gated, final_norm, tiles_per_expert):
    refs = list(refs)
    h_ref, x_ref, mod_ref = refs[:3]
    k = 3
    gates_ref = None
    if gated:
        gates_ref = refs[k]
        k += 1
    wg_ref, wu_ref, wd_ref = refs[k:k + 3]
    k += 3
    fg_ref = None
    if final_norm:
        fg_ref = refs[k]
        k += 1
    out_ref, acc_ref = refs[k], refs[k + 1]
    f = pl.program_id(1)

    @pl.when(f == 0)
    def _():
        acc_ref[...] = jnp.zeros_like(acc_ref)

    h = h_ref[...]
    hid = _silu(_dot(h, wg_ref[...])) * _dot(h, wu_ref[...])
    if gated:
        gt = gates_ref[...]
        lane = lax.broadcasted_iota(jnp.int32, gt.shape, 1)
        hid = hid * jnp.sum(jnp.where(lane == f // tiles_per_expert, gt, 0.0), axis=-1, keepdims=True)
    acc_ref[...] += _dot(hid.astype(BF16), wd_ref[...])

    @pl.when(f == pl.num_programs(1) - 1)
    def _():
        xo = x_ref[...] + mod_ref[5:6, :] * acc_ref[...]
        if final_norm:
            xo = _rms(xo, fg_ref[...])
        out_ref[...] = xo


def _ffn(h, x, mods, wg, wu, wd, gates, final_g):
    tm = 512
    gated = gates is not None
    final_norm = final_g is not None
    tf = 512
    tpe = F_EXPERT // tf
    kern = functools.partial(_ffn_kernel, gated=gated, final_norm=final_norm, tiles_per_expert=tpe)
    in_specs = [
        pl.BlockSpec((tm, D_MODEL), lambda i, f: (i, 0)),
        pl.BlockSpec((tm, D_MODEL), lambda i, f: (i, 0)),
        pl.BlockSpec((None, 6, D_MODEL), lambda i, f: (_mod_group(i, tm), 0, 0)),
    ]
    args = [h, x, mods]
    if gated:
        nf = N_EXPERTS * tpe
        in_specs.append(pl.BlockSpec((tm, 128), lambda i, f: (i, 0)))
        args.append(gates)
        in_specs += [
            pl.BlockSpec((None, D_MODEL, tf), lambda i, f: (f // tpe, 0, f % tpe)),
            pl.BlockSpec((None, D_MODEL, tf), lambda i, f: (f // tpe, 0, f % tpe)),
            pl.BlockSpec((None, tf, D_MODEL), lambda i, f: (f // tpe, f % tpe, 0)),
        ]
    else:
        nf = F_DENSE // tf
        in_specs += [
            pl.BlockSpec((D_MODEL, tf), lambda i, f: (0, f)),
            pl.BlockSpec((D_MODEL, tf), lambda i, f: (0, f)),
            pl.BlockSpec((tf, D_MODEL), lambda i, f: (f, 0)),
        ]
    args += [wg, wu, wd]
    if final_norm:
        in_specs.append(pl.BlockSpec((1, D_MODEL), lambda i, f: (0, 0)))
        args.append(final_g)
    return pl.pallas_call(
        kern,
        grid=(T_ALL // tm, nf),
        in_specs=in_specs,
        out_specs=pl.BlockSpec((tm, D_MODEL), lambda i, f: (i, 0)),
        out_shape=jax.ShapeDtypeStruct((T_ALL, D_MODEL), F32),
        scratch_shapes=[pltpu.VMEM((tm, D_MODEL), F32)],
        compiler_params=_cparams(("parallel", "arbitrary")),
        name="moe_ffn" if gated else "dense_ffn",
    )(*args)


def _rope_tables():
    n = DEC_SEQ
    rows = n // GRID_W
    t_row = jnp.repeat(jnp.arange(rows, dtype=F32), GRID_W)
    t_col = jnp.tile(jnp.arange(GRID_W, dtype=F32), rows)
    inv = 1.0 / (ROPE_THETA ** (jnp.arange(0, ROT_HALF, 2, dtype=F32) / ROT_HALF))
    ar, ac = t_row[:, None] * inv, t_col[:, None] * inv
    cos = jnp.concatenate([jnp.cos(ar), jnp.cos(ar), jnp.cos(ac), jnp.cos(ac)], axis=-1)
    sin_signed = jnp.concatenate([-jnp.sin(ar), jnp.sin(ar), -jnp.sin(ac), jnp.sin(ac)], axis=-1)
    return cos, sin_signed


def _pad_lanes(v, width=128):
    return jnp.pad(v, ((0, 0), (0, width - v.shape[-1])))


def kernel(x_prompt, x_sample, c, cache_k, cache_v, state_ssm_fwd, state_ssm_bwd, c_ctx, ada_w, ada_b, norm1_g, norm2_g, w_in, q_norm_g, k_norm_g, conv_w, conv_b, a_log_fwd, a_log_bwd, dt_bias_fwd, dt_bias_bwd, d_skip, ssd_norm_g, attn_out_g, w_out, ffn_w_gate, ffn_w_up, ffn_w_down, router_w, moe_w_gate, moe_w_up, moe_w_down, final_norm_g):
    x = jnp.concatenate([x_prompt.reshape(T_PROMPT, D_MODEL), x_sample.reshape(T_SAMPLE, D_MODEL)], axis=0)
    cond = jnp.concatenate([c_ctx[None, :], c, jnp.zeros((N_COND - 1 - DEC_BATCH, D_MODEL), F32)], axis=0)
    mods_all = _ada_mods(cond, ada_w, ada_b).reshape(DEPTH, N_COND, 6, D_MODEL)
    cos, sin_signed = _rope_tables()
    ck = cache_k.reshape(DEC_BATCH, DEPTH, PAST_LEN, KV_DIM)
    cv = cache_v.reshape(DEC_BATCH, DEPTH, PAST_LEN, KV_DIM)

    ks, vs, sfs, sbs = [], [], [], []
    for l in range(DEPTH):
        mods = mods_all[l]
        w_dt = _pad_lanes(w_in[l][:, N_MAIN:])
        proj, dt_raw = _inproj(x, mods, norm1_g[l][None, :], w_in[l], w_dt)

        qg, kg, og = q_norm_g[l][None, :], k_norm_g[l][None, :], attn_out_g[l][None, :]
        o_p, k_p, v_p = _attention_prompt(proj, qg, kg, og)
        o_s = _attention_sample(proj, qg, kg, og, ck, cv, cos, sin_signed, l)

        p = {
            'conv_w': conv_w[l], 'conv_b': conv_b[l][None, :],
            'dt_bias': _pad_lanes(jnp.concatenate([dt_bias_fwd[l], dt_bias_bwd[l]])[None, :]),
            'a_log': _pad_lanes(jnp.concatenate([a_log_fwd[l], a_log_bwd[l]])[None, :]),
            'd_skip': jnp.repeat(d_skip[l], SSD_HEAD_DIM)[None, :],
            'ssd_norm_g': ssd_norm_g[l][None, :],
        }
        y_p, sf, sb = _ssd(proj, dt_raw, p, SEQ, BATCH, 0, None)
        y_s = _ssd(proj, dt_raw, p, DEC_SEQ, DEC_BATCH, T_PROMPT // DEC_SEQ,
                   (state_ssm_fwd[:, l], state_ssm_bwd[:, l]))
        ks.append(k_p.reshape(BATCH, SEQ, N_KV_HEADS, HEAD_DIM))
        vs.append(v_p.reshape(BATCH, SEQ, N_KV_HEADS, HEAD_DIM))
        sfs.append(sf)
        sbs.append(sb)

        o = jnp.concatenate([o_p, o_s], axis=0)
        y = jnp.concatenate([y_p, y_s], axis=0)
        final_g = final_norm_g[None, :] if l == DEPTH - 1 else None
        j = l // 2
        if l % 2 == 0:
            x, h = _outproj(o, y, x, mods, norm2_g[l][None, :], w_out[l].astype(BF16), None)
            x = _ffn(h, x, mods, ffn_w_gate[j].astype(BF16), ffn_w_up[j].astype(BF16),
                     ffn_w_down[j].astype(BF16), None, final_g)
        else:
            router_wt = jnp.pad(router_w[j].T, ((0, 16 - N_EXPERTS), (0, 0)))
            x, h, meta, counts = _outproj(o, y, x, mods, norm2_g[l][None, :], w_out[l].astype(BF16), router_wt)
            pos1, pos2, vt, ve, nv, lo, hi = _route_plan(meta, counts)
            xs = _dispatch(h, pos1, pos2)
            ys = _experts(xs, moe_w_gate[j].astype(BF16), moe_w_up[j].astype(BF16),
                          moe_w_down[j].astype(BF16), vt, ve, nv, lo, hi)
            x = _combine(ys, x, mods, _pad_lanes(meta[:2].T), pos1, pos2, final_g)

    y_prompt = x[:T_PROMPT].reshape(BATCH, SEQ, D_MODEL)
    y_sample = x[T_PROMPT:].reshape(DEC_BATCH, DEC_SEQ, D_MODEL)
    return (y_prompt, y_sample, jnp.stack(ks, axis=1), jnp.stack(vs, axis=1),
            jnp.stack(sfs, axis=1), jnp.stack(sbs, axis=1))
```

```python
import functools

import jax
import jax.numpy as jnp
from jax import lax
from jax.experimental import pallas as pl
from jax.experimental.pallas import tpu as pltpu

F32 = jnp.float32
BF16 = jnp.bfloat16

D_MODEL = 2048
BATCH = 16
SEQ = 256
DEPTH = 2
DEC_BATCH = 2
DEC_SEQ = 1024
PAST_LEN = 512
GRID_W = 64
D_ATTN = 1024
D_SSD = 1024
HEAD_DIM = 128
N_Q_HEADS = 8
N_KV_HEADS = 2
Q_PER_KV = 4
KV_DIM = 256
ROT_HALF = 64
ROPE_THETA = 10000.0
SSD_HEAD_DIM = 64
SSD_HEADS = 16
SSD_GROUPS = 2
HEADS_PER_GROUP = 8
D_STATE = 128
CONV_DIM = 1536
CHUNK = 128
N_MAIN = 4096
F_DENSE = 5632
N_EXPERTS = 8
F_EXPERT = 1024
EPS = 1e-6

T_PROMPT = BATCH * SEQ
T_SAMPLE = DEC_BATCH * DEC_SEQ
T_ALL = T_PROMPT + T_SAMPLE
N_COND = 16

VMEM_LIMIT = 56 * 1024 * 1024


def _cparams(sem):
    return pltpu.CompilerParams(dimension_semantics=sem, vmem_limit_bytes=VMEM_LIMIT)


def _mod_group(i, tm):
    return jnp.maximum(0, (i * tm - T_PROMPT + DEC_SEQ) // DEC_SEQ)


def _silu(x):
    return x * jax.nn.sigmoid(x)


def _rms(x, g):
    ms = jnp.mean(x * x, axis=-1, keepdims=True)
    return x * lax.rsqrt(ms + EPS) * g


def _dot(a, b):
    return jnp.dot(a, b, preferred_element_type=F32)


def _dot_nt(a, b):
    return lax.dot_general(a, b, (((1,), (1,)), ((), ())), preferred_element_type=F32)


def _split3(x):
    hi = x.astype(BF16)
    r1 = x - hi.astype(F32)
    mid = r1.astype(BF16)
    r2 = r1 - mid.astype(F32)
    return hi, mid, r2.astype(BF16)


def _ada_kernel(c_ref, w_ref, b_ref, o_ref):
    s = _silu(c_ref[...]).astype(BF16)
    o_ref[...] = _dot(s, w_ref[...].astype(BF16)) + b_ref[...]


def _ada_mods(cond, ada_w, ada_b):
    tn = 1024
    n_out = 6 * D_MODEL
    return pl.pallas_call(
        _ada_kernel,
        grid=(DEPTH, n_out // tn),
        in_specs=[
            pl.BlockSpec((N_COND, D_MODEL), lambda l, j: (0, 0)),
            pl.BlockSpec((None, D_MODEL, tn), lambda l, j: (l, 0, j)),
            pl.BlockSpec((None, 1, tn), lambda l, j: (l, 0, j)),
        ],
        out_specs=pl.BlockSpec((None, N_COND, tn), lambda l, j: (l, 0, j)),
        out_shape=jax.ShapeDtypeStruct((DEPTH, N_COND, n_out), F32),
        compiler_params=_cparams(("parallel", "parallel")),
        name="ada_mods",
    )(cond, ada_w, ada_b.reshape(DEPTH, 1, n_out))


def _inproj_kernel(x_ref, mod_ref, g_ref, w_ref, wdt_ref, proj_ref, dt_ref, h_scr):
    @pl.when(pl.program_id(1) == 0)
    def _():
        h = _rms(x_ref[...], g_ref[...]) * (1.0 + mod_ref[1:2, :]) + mod_ref[0:1, :]
        hb = h.astype(BF16)
        h_scr[...] = hb
        dt_ref[...] = _dot(hb, wdt_ref[...].astype(BF16))

    proj_ref[...] = _dot(h_scr[...], w_ref[...].astype(BF16))


_ANY = pl.BlockSpec(memory_space=pl.ANY)


def _inproj(x, row0, mods, g, w_in, layer, w_dt):
    tm, tn = 1024, 512
    blk0 = row0 // tm
    nrows = x.shape[0]
    return pl.pallas_call(
        _inproj_kernel,
        grid=(nrows // tm, N_MAIN // tn),
        in_specs=[
            pl.BlockSpec((tm, D_MODEL), lambda i, j: (i, 0)),
            pl.BlockSpec((None, 6, D_MODEL), lambda i, j: (_mod_group(blk0 + i, tm), 0, 0)),
            pl.BlockSpec((1, D_MODEL), lambda i, j: (0, 0)),
            pl.BlockSpec((None, D_MODEL, tn), lambda i, j: (layer, 0, j)),
            pl.BlockSpec((D_MODEL, 128), lambda i, j: (0, 0)),
        ],
        out_specs=[
            pl.BlockSpec((tm, tn), lambda i, j: (i, j)),
            pl.BlockSpec((tm, 128), lambda i, j: (i, 0)),
        ],
        out_shape=[
            jax.ShapeDtypeStruct((nrows, N_MAIN), F32),
            jax.ShapeDtypeStruct((nrows, 128), F32),
        ],
        scratch_shapes=[pltpu.VMEM((tm, D_MODEL), BF16)],
        compiler_params=_cparams(("parallel", "arbitrary")),
        name="inproj",
    )(x, mods, g, w_in, w_dt)


def _rope(x, cos, sin_signed):
    lane = lax.broadcasted_iota(jnp.int32, x.shape, 1)
    first = (lane // (ROT_HALF // 2)) % 2 == 0
    swapped = jnp.where(first, pltpu.roll(x, HEAD_DIM - ROT_HALF // 2, 1), pltpu.roll(x, ROT_HALF // 2, 1))
    return x * cos + swapped * sin_signed


def _attn_kernel(*refs, nk_new, has_ctx):
    if has_ctx:
        (q_ref, kv_ref, qg_ref, kg_ref, og_ref, ck_ref, cv_ref, cq_ref, sq_ref, ckk_ref, skk_ref,
         o_ref, kb_scr, vb_scr, o_scr) = refs
    else:
        (q_ref, kv_ref, qg_ref, kg_ref, og_ref, o_ref, ko_ref, vo_ref, kb_scr, vb_scr, o_scr) = refs

    @pl.when(pl.program_id(1) == 0)
    def _():
        for g in range(N_KV_HEADS):
            sl = slice(g * HEAD_DIM, (g + 1) * HEAD_DIM)
            kn = _rms(kv_ref[:, sl], kg_ref[...])
            if has_ctx:
                kb_scr[0:nk_new, sl] = _rope(kn, ckk_ref[...], skk_ref[...]).astype(BF16)
                kb_scr[nk_new:, sl] = ck_ref[:, sl].astype(BF16)
            else:
                ko_ref[:, sl] = kn
                kb_scr[:, sl] = kn.astype(BF16)
        v = kv_ref[:, KV_DIM:]
        vb_scr[0:nk_new, :] = v.astype(BF16)
        if has_ctx:
            vb_scr[nk_new:, :] = cv_ref[...].astype(BF16)
        else:
            vo_ref[...] = v

    scale = HEAD_DIM ** -0.5
    for h in range(N_Q_HEADS):
        g = h // Q_PER_KV
        sl = slice(h * HEAD_DIM, (h + 1) * HEAD_DIM)
        gsl = slice(g * HEAD_DIM, (g + 1) * HEAD_DIM)
        qn = _rms(q_ref[:, sl], qg_ref[...])
        if has_ctx:
            qn = _rope(qn, cq_ref[...], sq_ref[...])
        s = _dot_nt(qn.astype(BF16), kb_scr[:, gsl]) * scale
        e = jnp.exp(s - jnp.max(s, axis=-1, keepdims=True))
        l = jnp.sum(e, axis=-1, keepdims=True)
        o_scr[:, sl] = _dot(e.astype(BF16), vb_scr[:, gsl]) / l
    o_ref[...] = _rms(o_scr[...], og_ref[...]).astype(BF16)


def _attention_prompt(proj, row0, qg, kg, og):
    n = SEQ
    blk0 = row0 // n
    kern = functools.partial(_attn_kernel, nk_new=n, has_ctx=False)
    vec = lambda w: pl.BlockSpec((1, w), lambda b, i: (0, 0))
    return pl.pallas_call(
        kern,
        grid=(BATCH, 1),
        in_specs=[
            pl.BlockSpec((n, D_ATTN), lambda b, i: (blk0 + b, 0)),
            pl.BlockSpec((n, 2 * KV_DIM), lambda b, i: (blk0 + b, 2)),
            vec(HEAD_DIM), vec(HEAD_DIM), vec(D_ATTN),
        ],
        out_specs=[
            pl.BlockSpec((n, D_ATTN), lambda b, i: (b, 0)),
            pl.BlockSpec((None, n, KV_DIM), lambda b, i: (b, 0, 0)),
            pl.BlockSpec((None, n, KV_DIM), lambda b, i: (b, 0, 0)),
        ],
        out_shape=[
            jax.ShapeDtypeStruct((T_PROMPT, D_ATTN), BF16),
            jax.ShapeDtypeStruct((BATCH, n, KV_DIM), F32),
            jax.ShapeDtypeStruct((BATCH, n, KV_DIM), F32),
        ],
        scratch_shapes=[
            pltpu.VMEM((n, KV_DIM), BF16), pltpu.VMEM((n, KV_DIM), BF16), pltpu.VMEM((n, D_ATTN), F32),
        ],
        compiler_params=_cparams(("parallel", "arbitrary")),
        name="attn_prompt",
    )(proj, proj, qg, kg, og)


def _attention_sample(proj, row0, qg, kg, og, ck, cv, cos, sin_signed, layer):
    n, tq = DEC_SEQ, 512
    nq = n // tq
    nk = n + PAST_LEN
    kern = functools.partial(_attn_kernel, nk_new=n, has_ctx=True)
    vec = lambda w: pl.BlockSpec((1, w), lambda b, i: (0, 0))
    q_blk0 = row0 // tq
    kv_blk0 = row0 // n
    return pl.pallas_call(
        kern,
        grid=(DEC_BATCH, nq),
        in_specs=[
            pl.BlockSpec((tq, D_ATTN), lambda b, i: (q_blk0 + b * nq + i, 0)),
            pl.BlockSpec((n, 2 * KV_DIM), lambda b, i: (kv_blk0 + b, 2)),
            vec(HEAD_DIM), vec(HEAD_DIM), vec(D_ATTN),
            pl.BlockSpec((None, None, PAST_LEN, KV_DIM), lambda b, i: (b, layer, 0, 0)),
            pl.BlockSpec((None, None, PAST_LEN, KV_DIM), lambda b, i: (b, layer, 0, 0)),
            pl.BlockSpec((tq, HEAD_DIM), lambda b, i: (i, 0)),
            pl.BlockSpec((tq, HEAD_DIM), lambda b, i: (i, 0)),
            pl.BlockSpec((n, HEAD_DIM), lambda b, i: (0, 0)),
            pl.BlockSpec((n, HEAD_DIM), lambda b, i: (0, 0)),
        ],
        out_specs=pl.BlockSpec((tq, D_ATTN), lambda b, i: (b * nq + i, 0)),
        out_shape=jax.ShapeDtypeStruct((T_SAMPLE, D_ATTN), BF16),
        scratch_shapes=[
            pltpu.VMEM((nk, KV_DIM), BF16), pltpu.VMEM((nk, KV_DIM), BF16), pltpu.VMEM((tq, D_ATTN), F32),
        ],
        compiler_params=_cparams(("parallel", "arbitrary")),
        name="attn_sample",
    )(proj, proj, qg, kg, og, ck, cv, cos, sin_signed, cos, sin_signed)


def _conv_silu(x, w, b):
    n = x.shape[0]
    row = lax.broadcasted_iota(jnp.int32, (n, 1), 0)
    prev = jnp.where(row == 0, 0.0, pltpu.roll(x, 1, 0))
    nxt = jnp.where(row == n - 1, 0.0, pltpu.roll(x, n - 1, 0))
    return _silu(prev * w[0:1, :] + x * w[1:2, :] + nxt * w[2:3, :] + b)


def _softplus(x):
    return jnp.maximum(x, 0.0) + jnp.log1p(jnp.exp(-jnp.abs(x)))


def _pair_cols(m, li, lo_lanes):
    return jnp.where(lo_lanes, m[:, li:li + 1], m[:, li + 1:li + 2])


def _ssd_kernel(*refs, n, has_init, n_prev):
    if has_init:
        (za_ref, zb_ref, xa_ref, xb_ref, bc_ref, dt_ref, cw_ref, cb_ref, dtb_ref, alog_ref, dsk_ref, ng_ref,
         sf0_ref, sb0_ref, y_ref, xc_scr, bcc_scr, dts_scr, y_scr, s_scr) = refs
    elif n_prev:
        (za_ref, zb_ref, xa_ref, xb_ref, bc_ref, dt_ref, cw_ref, cb_ref, dtb_ref, alog_ref, dsk_ref, ng_ref,
         psf_ref, psb_ref, y_ref, sf_ref, sb_ref, xc_scr, bcc_scr, dts_scr, y_scr, s_scr) = refs
    else:
        (za_ref, zb_ref, xa_ref, xb_ref, bc_ref, dt_ref, cw_ref, cb_ref, dtb_ref, alog_ref, dsk_ref, ng_ref,
         y_ref, sf_ref, sb_ref, xc_scr, bcc_scr, dts_scr, y_scr, s_scr) = refs
    nc = n // CHUNK
    gw = HEADS_PER_GROUP * SSD_HEAD_DIM

    xc_scr[:, 0:gw] = _conv_silu(xa_ref[...], cw_ref[:, 0:gw], cb_ref[:, 0:gw])
    xc_scr[:, gw:] = _conv_silu(xb_ref[...], cw_ref[:, gw:2 * gw], cb_ref[:, gw:2 * gw])
    bcc_scr[...] = _conv_silu(bc_ref[...], cw_ref[:, 2 * gw:], cb_ref[:, 2 * gw:])
    y_scr[...] = xc_scr[...] * dsk_ref[...]
    dts_scr[...] = _softplus(dt_ref[...] + dtb_ref[...])
    for g in range(SSD_GROUPS):
        if has_init:
            s_scr[0, g] = sf0_ref[g * HEADS_PER_GROUP:(g + 1) * HEADS_PER_GROUP].reshape(gw, D_STATE).T
            s_scr[1, g] = sb0_ref[g * HEADS_PER_GROUP:(g + 1) * HEADS_PER_GROUP].reshape(gw, D_STATE).T
        else:
            s_scr[0, g] = jnp.zeros((D_STATE, gw), F32)
            s_scr[1, g] = jnp.zeros((D_STATE, gw), F32)

    a_row = -jnp.exp(alog_ref[...])
    ri = lax.broadcasted_iota(jnp.int32, (CHUNK, CHUNK), 0)
    ci = lax.broadcasted_iota(jnp.int32, (CHUNK, CHUNK), 1)
    lo_lanes = ci < SSD_HEAD_DIM
    lo_row = lo_lanes[0:1, :]

    def scan_chunk(dirn, c):
        mask = (ci <= ri) if dirn == 0 else (ci >= ri)
        tri = mask.astype(BF16)
        lane0 = dirn * SSD_HEADS
        r0 = pl.multiple_of(c * CHUNK, CHUNK)
        rows = pl.ds(r0, CHUNK)
        dt = dts_scr[rows, :]
        d1, d2, d3 = _split3(dt * a_row)
        cs = _dot(tri, d1) + _dot(tri, d2) + _dot(tri, d3)
        cs_t = cs.T
        dt_t = dt.T
        total = cs[CHUNK - 1:CHUNK, :] if dirn == 0 else cs[0:1, :]
        e_in = jnp.exp(cs)
        to_end = jnp.exp(total - cs) * dt
        dec = jnp.exp(total)
        for g in range(SSD_GROUPS):
            bm = bcc_scr[rows, g * D_STATE:(g + 1) * D_STATE]
            cm = bcc_scr[rows, (SSD_GROUPS + g) * D_STATE:(SSD_GROUPS + g + 1) * D_STATE].astype(BF16)
            gmat = _dot_nt(cm, bm.astype(BF16))
            st = s_scr[dirn, g]
            y_off = _dot(cm, st.astype(BF16))
            xs_parts, dec_parts = [], []
            for j in range(HEADS_PER_GROUP // 2):
                li = lane0 + g * HEADS_PER_GROUP + 2 * j
                lanes = slice((g * 4 + j) * 128, (g * 4 + j + 1) * 128)
                a_parts = []
                for hh in range(2):
                    diff = cs[:, li + hh:li + hh + 1] - cs_t[li + hh:li + hh + 1, :]
                    w = jnp.exp(jnp.where(mask, diff, -jnp.inf)) * dt_t[li + hh:li + hh + 1, :]
                    a_parts.append((gmat * w).astype(BF16))
                x_pair = xc_scr[rows, lanes]
                x2 = jnp.concatenate([jnp.where(lo_lanes, x_pair, 0.0), jnp.where(lo_lanes, 0.0, x_pair)],
                                     axis=0).astype(BF16)
                y_diag = _dot(jnp.concatenate(a_parts, axis=1), x2)
                y_pair = y_diag + y_off[:, j * 128:(j + 1) * 128] * _pair_cols(e_in, li, lo_lanes)
                y_scr[rows, lanes] = y_scr[rows, lanes] + y_pair
                xs_parts.append((x_pair * _pair_cols(to_end, li, lo_lanes)).astype(BF16))
                dec_parts.append(_pair_cols(dec, li, lo_row))
            ds = _dot(bm.T.astype(BF16), jnp.concatenate(xs_parts, axis=1))
            s_scr[dirn, g] = st * jnp.concatenate(dec_parts, axis=1) + ds

    def body(i, carry):
        scan_chunk(0, i)
        scan_chunk(1, nc - 1 - i)
        return carry

    lax.fori_loop(0, nc, body, 0)

    def finish(c, carry):
        rows = pl.ds(pl.multiple_of(c * CHUNK, CHUNK), CHUNK)
        ya = y_scr[rows, 0:gw] * _silu(za_ref[rows, :])
        yb = y_scr[rows, gw:] * _silu(zb_ref[rows, :])
        ms = (jnp.sum(ya * ya, axis=-1, keepdims=True) + jnp.sum(yb * yb, axis=-1, keepdims=True)) / D_SSD
        inv = lax.rsqrt(ms + EPS)
        y_ref[rows, 0:gw] = (ya * inv * ng_ref[:, 0:gw]).astype(BF16)
        y_ref[rows, gw:] = (yb * inv * ng_ref[:, gw:]).astype(BF16)
        return carry

    lax.fori_loop(0, nc, finish, 0)

    if not has_init:
        if n_prev:
            sf_ref[0:n_prev] = psf_ref[...]
            sb_ref[0:n_prev] = psb_ref[...]
        for g in range(SSD_GROUPS):
            hs = slice(g * HEADS_PER_GROUP, (g + 1) * HEADS_PER_GROUP)
            sf_ref[n_prev, hs] = s_scr[0, g].T.reshape(HEADS_PER_GROUP, SSD_HEAD_DIM, D_STATE)
            sb_ref[n_prev, hs] = s_scr[1, g].T.reshape(HEADS_PER_GROUP, SSD_HEAD_DIM, D_STATE)


def _ssd(proj, dt_raw, p, n, nb, row_blk0, layer, init, prev_states):
    has_init = init is not None
    n_prev = 0 if has_init else layer
    kern = functools.partial(_ssd_kernel, n=n, has_init=has_init, n_prev=n_prev)
    col = lambda cb: pl.BlockSpec((n, 512), lambda b: (row_blk0 + b, cb))
    vec = lambda r, w: pl.BlockSpec((r, w), lambda b: (0, 0))
    layers_spec = lambda k: pl.BlockSpec((None, k, SSD_HEADS, SSD_HEAD_DIM, D_STATE), lambda b: (b, 0, 0, 0, 0))
    in_specs = [
        col(3), col(4), col(5), col(6), col(7),
        pl.BlockSpec((n, 128), lambda b: (row_blk0 + b, 0)),
        vec(3, CONV_DIM), vec(1, CONV_DIM), vec(1, 128), vec(1, 128), vec(1, D_SSD), vec(1, D_SSD),
    ]
    args = [proj, proj, proj, proj, proj, dt_raw, p['conv_w'], p['conv_b'], p['dt_bias'], p['a_log'],
            p['d_skip'], p['ssd_norm_g']]
    y_spec = pl.BlockSpec((n, D_SSD), lambda b: (b, 0))
    y_shape = jax.ShapeDtypeStruct((nb * n, D_SSD), BF16)
    if has_init:
        init_spec = pl.BlockSpec((None, None, SSD_HEADS, SSD_HEAD_DIM, D_STATE), lambda b: (b, layer, 0, 0, 0))
        in_specs += [init_spec, init_spec]
        args += list(init)
        out_specs, out_shape = y_spec, y_shape
    else:
        if n_prev:
            in_specs += [layers_spec(n_prev), layers_spec(n_prev)]
            args += list(prev_states)
        st_shape = jax.ShapeDtypeStruct((nb, layer + 1, SSD_HEADS, SSD_HEAD_DIM, D_STATE), F32)
        out_specs = [y_spec, layers_spec(layer + 1), layers_spec(layer + 1)]
        out_shape = [y_shape, st_shape, st_shape]
    return pl.pallas_call(
        kern,
        grid=(nb,),
        in_specs=in_specs,
        out_specs=out_specs,
        out_shape=out_shape,
        scratch_shapes=[
            pltpu.VMEM((n, D_SSD), F32), pltpu.VMEM((n, 512), F32), pltpu.VMEM((n, 128), F32),
            pltpu.VMEM((n, D_SSD), F32), pltpu.VMEM((2, SSD_GROUPS, D_STATE, 512), F32),
        ],
        compiler_params=_cparams(("parallel",)),
        name="ssd_sample" if has_init else "ssd_prompt",
    )(*args)


def _seg_specs(segs, tm):
    specs, bounds, start = [], [], 0
    for a in segs:
        b0, nblk = start // tm, a.shape[0] // tm
        specs.append(pl.BlockSpec((tm, a.shape[1]), lambda i, b0=b0, nblk=nblk: (jnp.clip(i - b0, 0, nblk - 1), 0)))
        bounds.append(b0)
        start += a.shape[0]
    return specs, tuple(bounds)


def _seg_pick(refs, bounds):
    i = pl.program_id(0)
    v = refs[0][...]
    for ref, b0 in zip(refs[1:], bounds[1:]):
        v = jnp.where(i >= b0, ref[...], v)
    return v


def _outproj_kernel(*refs, with_router, o_bounds, y_bounds, x_bounds):
    refs = list(refs)
    o_refs = [refs.pop(0) for _ in o_bounds]
    y_refs = [refs.pop(0) for _ in y_bounds]
    x_refs = [refs.pop(0) for _ in x_bounds]
    if with_router:
        mod_ref, g_ref, w_ref, rw_ref, xo_ref, h_ref, meta_ref, cnt_ref, carry_scr = refs
    else:
        mod_ref, g_ref, w_ref, xo_ref, h_ref = refs
    a = jnp.concatenate([_seg_pick(o_refs, o_bounds), _seg_pick(y_refs, y_bounds)], axis=1)
    xn = _seg_pick(x_refs, x_bounds) + mod_ref[2:3, :] * _dot(a, w_ref[...])
    xo_ref[...] = xn
    h = _rms(xn, g_ref[...]) * (1.0 + mod_ref[4:5, :]) + mod_ref[3:4, :]
    h_ref[...] = h.astype(h_ref.dtype)
    if with_router:
        tm = h.shape[0]

        @pl.when(pl.program_id(0) == 0)
        def _():
            carry_scr[...] = jnp.zeros_like(carry_scr)

        h1, h2, _ = _split3(h)
        w1, w2, _ = _split3(rw_ref[...])
        logits = _dot_nt(w1, h1) + _dot_nt(w2, h1) + _dot_nt(w1, h2)
        row = lax.broadcasted_iota(jnp.int32, logits.shape, 0)
        logits = jnp.where(row < N_EXPERTS, logits, -jnp.inf)
        e = jnp.exp(logits - jnp.max(logits, axis=0, keepdims=True))
        probs = e / jnp.sum(e, axis=0, keepdims=True)
        p1 = jnp.max(probs, axis=0, keepdims=True)
        i1 = jnp.min(jnp.where(probs == p1, row, 16), axis=0, keepdims=True)
        rest = jnp.where(row == i1, -1.0, probs)
        p2 = jnp.max(rest, axis=0, keepdims=True)
        i2 = jnp.min(jnp.where(rest == p2, row, 16), axis=0, keepdims=True)
        hit1, hit2 = row == i1, row == i2
        onehot = jnp.where(hit1 | hit2, 1.0, 0.0)
        ti = lax.broadcasted_iota(jnp.int32, (tm, tm), 0)
        tj = lax.broadcasted_iota(jnp.int32, (tm, tm), 1)
        before = jnp.where(ti < tj, 1.0, 0.0).astype(BF16)
        rank = carry_scr[:, 0:1] + _dot(onehot.astype(BF16), before)
        r1 = jnp.sum(jnp.where(hit1, rank, 0.0), axis=0, keepdims=True)
        r2 = jnp.sum(jnp.where(hit2, rank, 0.0), axis=0, keepdims=True)
        carry_scr[...] = carry_scr[...] + jnp.sum(onehot, axis=1, keepdims=True)
        cnt_ref[...] = carry_scr[...]
        r8 = lax.broadcasted_iota(jnp.int32, (8, tm), 0)
        vals = [p1 / (p1 + p2), p2 / (p1 + p2), i1.astype(F32), i2.astype(F32), r1, r2]
        meta = jnp.zeros((8, tm), F32)
        for k, v in enumerate(vals):
            meta = jnp.where(r8 == k, v, meta)
        meta_ref[...] = meta


def _outproj(o_segs, y_segs, x_segs, mods, g, w_out, router_wt):
    tm = 512
    with_router = router_wt is not None
    o_specs, o_bounds = _seg_specs(o_segs, tm)
    y_specs, y_bounds = _seg_specs(y_segs, tm)
    x_specs, x_bounds = _seg_specs(x_segs, tm)
    kern = functools.partial(_outproj_kernel, with_router=with_router, o_bounds=o_bounds, y_bounds=y_bounds,
                             x_bounds=x_bounds)
    in_specs = o_specs + y_specs + x_specs + [
        pl.BlockSpec((None, 6, D_MODEL), lambda i: (_mod_group(i, tm), 0, 0)),
        pl.BlockSpec((1, D_MODEL), lambda i: (0, 0)),
        pl.BlockSpec((D_MODEL, D_MODEL), lambda i: (0, 0), pipeline_mode=pl.Buffered(1)),
    ]
    args = list(o_segs) + list(y_segs) + list(x_segs) + [mods, g, w_out]
    row_spec = pl.BlockSpec((tm, D_MODEL), lambda i: (i, 0))
    out_specs = [row_spec, row_spec]
    out_shape = [jax.ShapeDtypeStruct((T_ALL, D_MODEL), F32),
                 jax.ShapeDtypeStruct((T_ALL, D_MODEL), F32 if with_router else BF16)]
    scratch = []
    if with_router:
        in_specs.append(pl.BlockSpec((16, D_MODEL), lambda i: (0, 0)))
        args.append(router_wt)
        out_specs += [pl.BlockSpec((8, tm), lambda i: (0, i)), pl.BlockSpec((16, 128), lambda i: (0, 0))]
        out_shape += [jax.ShapeDtypeStruct((8, T_ALL), F32), jax.ShapeDtypeStruct((16, 128), F32)]
        scratch = [pltpu.VMEM((16, 128), F32)]
    return pl.pallas_call(
        kern,
        grid=(T_ALL // tm,),
        in_specs=in_specs,
        out_specs=out_specs,
        out_shape=out_shape,
        scratch_shapes=scratch,
        compiler_params=_cparams(("arbitrary",)),
        name="outproj_router" if with_router else "outproj",
    )(*args)


MOE_ROWS = 2 * T_ALL
MOE_TILE = 256
MOE_TILES = MOE_ROWS // MOE_TILE
MOE_VISITS = MOE_TILES + N_EXPERTS - 1


def _row_copy(src, s, dst, d, sem):
    return pltpu.make_async_copy(src.at[pl.ds(s, 1)], dst.at[pl.ds(d, 1)], sem)


def _dispatch_kernel(p1_ref, p2_ref, h_ref, xs_ref, sem):
    tm = h_ref.shape[0]

    def issue(r, c):
        _row_copy(h_ref, r, xs_ref, p1_ref[0, 0, r], sem.at[0]).start()
        _row_copy(h_ref, r, xs_ref, p2_ref[0, 0, r], sem.at[1]).start()
        return c

    lax.fori_loop(0, tm, issue, 0, unroll=8)
    pltpu.make_async_copy(h_ref, xs_ref.at[pl.ds(0, tm)], sem.at[0]).wait()
    pltpu.make_async_copy(h_ref, xs_ref.at[pl.ds(0, tm)], sem.at[1]).wait()


def _dispatch(h, pos1, pos2):
    tm = 512
    nt = T_ALL // tm
    idx = lambda: pl.BlockSpec((1, 1, tm), lambda i: (i, 0, 0), memory_space=pltpu.SMEM)
    return pl.pallas_call(
        _dispatch_kernel,
        grid=(nt,),
        in_specs=[idx(), idx(), pl.BlockSpec((tm, D_MODEL), lambda i: (i, 0))],
        out_specs=pl.BlockSpec(memory_space=pl.ANY),
        out_shape=jax.ShapeDtypeStruct((MOE_ROWS, D_MODEL), F32),
        scratch_shapes=[pltpu.SemaphoreType.DMA((2,))],
        compiler_params=_cparams(("arbitrary",)),
        name="moe_dispatch",
    )(pos1.reshape(nt, 1, tm), pos2.reshape(nt, 1, tm), h)


def _experts_kernel(vt_ref, ve_ref, nv_ref, lo_ref, hi_ref, xs_ref, wg_ref, wu_ref, wd_ref, y_ref):
    v = pl.program_id(0)

    @pl.when(v < nv_ref[0])
    def _():
        e = ve_ref[v]
        x = xs_ref[...].astype(BF16)
        hid = _silu(_dot(x, wg_ref[...])) * _dot(x, wu_ref[...])
        y = _dot(hid.astype(BF16), wd_ref[...])
        row = vt_ref[v] * MOE_TILE + lax.broadcasted_iota(jnp.int32, (MOE_TILE, 1), 0)
        mine = (row >= lo_ref[e]) & (row < hi_ref[e])
        first_visit = (v == 0) | (vt_ref[jnp.maximum(v - 1, 0)] != vt_ref[v])

        @pl.when(first_visit)
        def _():
            y_ref[...] = jnp.where(mine, y, 0.0)

        @pl.when(jnp.logical_not(first_visit))
        def _():
            y_ref[...] = jnp.where(mine, y, y_ref[...])


def _experts(xs, wg, wu, wd, vt, ve, nv, lo, hi):
    grid_spec = pltpu.PrefetchScalarGridSpec(
        num_scalar_prefetch=5,
        grid=(MOE_VISITS,),
        in_specs=[
            pl.BlockSpec((MOE_TILE, D_MODEL), lambda v, vt, ve, nv, lo, hi: (vt[v], 0)),
            pl.BlockSpec((None, D_MODEL, F_EXPERT), lambda v, vt, ve, nv, lo, hi: (ve[v], 0, 0)),
            pl.BlockSpec((None, D_MODEL, F_EXPERT), lambda v, vt, ve, nv, lo, hi: (ve[v], 0, 0)),
            pl.BlockSpec((None, F_EXPERT, D_MODEL), lambda v, vt, ve, nv, lo, hi: (ve[v], 0, 0)),
        ],
        out_specs=pl.BlockSpec((MOE_TILE, D_MODEL), lambda v, vt, ve, nv, lo, hi: (vt[v], 0)),
    )
    return pl.pallas_call(
        _experts_kernel,
        grid_spec=grid_spec,
        out_shape=jax.ShapeDtypeStruct((MOE_ROWS, D_MODEL), F32),
        compiler_params=_cparams(("arbitrary",)),
        name="moe_experts",
    )(vt, ve, nv, lo, hi, xs, wg, wu, wd)


def _combine_kernel(p1c_ref, p2c_ref, p1n_ref, p2n_ref, y_hbm, x_ref, mod_ref, gate_ref, fg_ref,
                    outp_ref, outs_ref, ya_buf, yb_buf, sem):
    i = pl.program_id(0)
    n = pl.num_programs(0)
    tm = x_ref.shape[0]
    slot = i % 2

    def gather(pa_ref, pb_ref, s):
        def issue(r, c):
            _row_copy(y_hbm, pa_ref[0, 0, r], ya_buf.at[s], r, sem.at[0, s]).start()
            _row_copy(y_hbm, pb_ref[0, 0, r], yb_buf.at[s], r, sem.at[1, s]).start()
            return c

        lax.fori_loop(0, tm, issue, 0, unroll=8)

    @pl.when(i == 0)
    def _():
        gather(p1c_ref, p2c_ref, 0)

    @pl.when(i + 1 < n)
    def _():
        gather(p1n_ref, p2n_ref, 1 - slot)

    pltpu.make_async_copy(y_hbm.at[pl.ds(0, tm)], ya_buf.at[slot], sem.at[0, slot]).wait()
    pltpu.make_async_copy(y_hbm.at[pl.ds(0, tm)], yb_buf.at[slot], sem.at[1, slot]).wait()
    g = gate_ref[...]
    mix = g[:, 0:1] * ya_buf[slot] + g[:, 1:2] * yb_buf[slot]
    xo = _rms(x_ref[...] + mod_ref[5:6, :] * mix, fg_ref[...])

    @pl.when(i < T_PROMPT // tm)
    def _():
        outp_ref[...] = xo

    @pl.when(i >= T_PROMPT // tm)
    def _():
        outs_ref[...] = xo


def _combine(y, x, mods, gate_cols, pos1, pos2, final_g):
    tm = 256
    nt = T_ALL // tm
    ntp = T_PROMPT // tm
    cur = lambda: pl.BlockSpec((1, 1, tm), lambda i: (i, 0, 0), memory_space=pltpu.SMEM)
    nxt = lambda: pl.BlockSpec((1, 1, tm), lambda i: (jnp.minimum(i + 1, nt - 1), 0, 0), memory_space=pltpu.SMEM)
    p1, p2 = pos1.reshape(nt, 1, tm), pos2.reshape(nt, 1, tm)
    return pl.pallas_call(
        _combine_kernel,
        grid=(nt,),
        in_specs=[
            cur(), cur(), nxt(), nxt(),
            pl.BlockSpec(memory_space=pl.ANY),
            pl.BlockSpec((tm, D_MODEL), lambda i: (i, 0)),
            pl.BlockSpec((None, 6, D_MODEL), lambda i: (_mod_group(i, tm), 0, 0)),
            pl.BlockSpec((tm, 128), lambda i: (i, 0)),
            pl.BlockSpec((1, D_MODEL), lambda i: (0, 0)),
        ],
        out_specs=[
            pl.BlockSpec((tm, D_MODEL), lambda i: (jnp.minimum(i, ntp - 1), 0)),
            pl.BlockSpec((tm, D_MODEL), lambda i: (jnp.maximum(i - ntp, 0), 0)),
        ],
        out_shape=[
            jax.ShapeDtypeStruct((T_PROMPT, D_MODEL), F32), jax.ShapeDtypeStruct((T_SAMPLE, D_MODEL), F32),
        ],
        scratch_shapes=[
            pltpu.VMEM((2, tm, D_MODEL), F32), pltpu.VMEM((2, tm, D_MODEL), F32),
            pltpu.SemaphoreType.DMA((2, 2)),
        ],
        compiler_params=_cparams(("arbitrary",)),
        name="moe_combine",
    )(p1, p2, p1, p2, y, x, mods, gate_cols, final_g)


def _route_plan(meta, counts):
    i1, i2 = meta[2].astype(jnp.int32), meta[3].astype(jnp.int32)
    r1, r2 = meta[4].astype(jnp.int32), meta[5].astype(jnp.int32)
    cnt = counts[:N_EXPERTS, 0].astype(jnp.int32)
    hi = jnp.cumsum(cnt)
    lo = hi - cnt
    ex = jnp.arange(N_EXPERTS, dtype=jnp.int32)
    pos1 = jnp.sum(jnp.where(i1[:, None] == ex[None, :], lo[None, :], 0), axis=1) + r1
    pos2 = jnp.sum(jnp.where(i2[:, None] == ex[None, :], lo[None, :], 0), axis=1) + r2
    first_tile = lo // MOE_TILE
    n_vis_e = jnp.where(cnt > 0, (hi - 1) // MOE_TILE - first_tile + 1, 0)
    vis_hi = jnp.cumsum(n_vis_e)
    vis_lo = vis_hi - n_vis_e
    nv = vis_hi[-1]
    v = jnp.minimum(jnp.arange(MOE_VISITS, dtype=jnp.int32), nv - 1)
    ve = jnp.minimum(jnp.sum(v[:, None] >= vis_hi[None, :], axis=1), N_EXPERTS - 1).astype(jnp.int32)
    pick = lambda tab: jnp.sum(jnp.where(ve[:, None] == ex[None, :], tab[None, :], 0), axis=1)
    vt = (pick(first_tile) + v - pick(vis_lo)).astype(jnp.int32)
    return pos1, pos2, vt, ve, nv.reshape(1).astype(jnp.int32), lo.astype(jnp.int32), hi.astype(jnp.int32)


def _ffn_kernel(h_ref, x_hbm, mod_ref, wg_ref, wu_ref, wd_ref, out_ref, x_buf, sem):
    i, f = pl.program_id(0), pl.program_id(1)
    tm = h_ref.shape[0]
    x_copy = pltpu.make_async_copy(x_hbm.at[pl.ds(pl.multiple_of(i * tm, tm), tm)], x_buf, sem)

    @pl.when(f == 0)
    def _():
        x_copy.start()
        out_ref[...] = jnp.zeros_like(out_ref)

    h = h_ref[...]
    hid = _silu(_dot(h, wg_ref[0].astype(BF16))) * _dot(h, wu_ref[0].astype(BF16))
    out_ref[...] += _dot(hid.astype(BF16), wd_ref[0].astype(BF16))

    @pl.when(f == pl.num_programs(1) - 1)
    def _():
        x_copy.wait()
        out_ref[...] = x_buf[...] + mod_ref[5:6, :] * out_ref[...]


def _ffn(h, x, mods, wg, wu, wd, j):
    tm, tf = 1024, 256
    return pl.pallas_call(
        _ffn_kernel,
        grid=(T_ALL // tm, F_DENSE // tf),
        in_specs=[
            pl.BlockSpec((tm, D_MODEL), lambda i, f: (i, 0)),
            _ANY,
            pl.BlockSpec((None, 6, D_MODEL), lambda i, f: (_mod_group(i, tm), 0, 0)),
            pl.BlockSpec((1, D_MODEL, tf), lambda i, f: (j, 0, f)),
            pl.BlockSpec((1, D_MODEL, tf), lambda i, f: (j, 0, f)),
            pl.BlockSpec((1, tf, D_MODEL), lambda i, f: (j, f, 0)),
        ],
        out_specs=pl.BlockSpec((tm, D_MODEL), lambda i, f: (i, 0)),
        out_shape=jax.ShapeDtypeStruct((T_ALL, D_MODEL), F32),
        scratch_shapes=[pltpu.VMEM((tm, D_MODEL), F32), pltpu.SemaphoreType.DMA(())],
        compiler_params=_cparams(("parallel", "arbitrary")),
        name="dense_ffn",
    )(h, x, mods, wg, wu, wd)


def _rope_tables():
    n = DEC_SEQ
    rows = n // GRID_W
    t_row = jnp.repeat(jnp.arange(rows, dtype=F32), GRID_W)
    t_col = jnp.tile(jnp.arange(GRID_W, dtype=F32), rows)
    inv = 1.0 / (ROPE_THETA ** (jnp.arange(0, ROT_HALF, 2, dtype=F32) / ROT_HALF))
    ar, ac = t_row[:, None] * inv, t_col[:, None] * inv
    cos = jnp.concatenate([jnp.cos(ar), jnp.cos(ar), jnp.cos(ac), jnp.cos(ac)], axis=-1)
    sin_signed = jnp.concatenate([-jnp.sin(ar), jnp.sin(ar), -jnp.sin(ac), jnp.sin(ac)], axis=-1)
    return cos, sin_signed


def _pad_lanes(v, width=128):
    return jnp.pad(v, ((0, 0), (0, width - v.shape[-1])))


def kernel(x_prompt, x_sample, c, cache_k, cache_v, state_ssm_fwd, state_ssm_bwd, c_ctx, ada_w, ada_b, norm1_g, norm2_g, w_in, q_norm_g, k_norm_g, conv_w, conv_b, a_log_fwd, a_log_bwd, dt_bias_fwd, dt_bias_bwd, d_skip, ssd_norm_g, attn_out_g, w_out, ffn_w_gate, ffn_w_up, ffn_w_down, router_w, moe_w_gate, moe_w_up, moe_w_down, final_norm_g):
    assert DEPTH % 2 == 0
    cond = jnp.concatenate([c_ctx[None, :], c, jnp.zeros((N_COND - 1 - DEC_BATCH, D_MODEL), F32)], axis=0)
    mods_all = _ada_mods(cond, ada_w, ada_b).reshape(DEPTH, N_COND, 6, D_MODEL)
    cos, sin_signed = _rope_tables()
    ck = cache_k.reshape(DEC_BATCH, DEPTH, PAST_LEN, KV_DIM)
    cv = cache_v.reshape(DEC_BATCH, DEPTH, PAST_LEN, KV_DIM)

    x_segs = [x_prompt.reshape(T_PROMPT, D_MODEL), x_sample.reshape(T_SAMPLE, D_MODEL)]
    ks, vs, states = [], [], None
    for l in range(DEPTH):
        mods = mods_all[l]
        w_dt = _pad_lanes(lax.slice(w_in, (l, 0, N_MAIN), (l + 1, D_MODEL, w_in.shape[2]))[0])
        g1 = norm1_g[l][None, :]
        if len(x_segs) == 2:
            proj_p, dt_p = _inproj(x_segs[0], 0, mods, g1, w_in, l, w_dt)
            proj_s, dt_s = _inproj(x_segs[1], T_PROMPT, mods, g1, w_in, l, w_dt)
            row_s = 0
        else:
            proj_p, dt_p = proj_s, dt_s = _inproj(x_segs[0], 0, mods, g1, w_in, l, w_dt)
            row_s = T_PROMPT

        qg, kg, og = q_norm_g[l][None, :], k_norm_g[l][None, :], attn_out_g[l][None, :]
        o_p, k_p, v_p = _attention_prompt(proj_p, 0, qg, kg, og)
        o_s = _attention_sample(proj_s, row_s, qg, kg, og, ck, cv, cos, sin_signed, l)

        p = {
            'conv_w': conv_w[l], 'conv_b': conv_b[l][None, :],
            'dt_bias': _pad_lanes(jnp.concatenate([dt_bias_fwd[l], dt_bias_bwd[l]])[None, :]),
            'a_log': _pad_lanes(jnp.concatenate([a_log_fwd[l], a_log_bwd[l]])[None, :]),
            'd_skip': jnp.repeat(d_skip[l], SSD_HEAD_DIM)[None, :],
            'ssd_norm_g': ssd_norm_g[l][None, :],
        }
        y_p, sf, sb = _ssd(proj_p, dt_p, p, SEQ, BATCH, 0, l, None, states)
        states = (sf, sb)
        y_s = _ssd(proj_s, dt_s, p, DEC_SEQ, DEC_BATCH, row_s // DEC_SEQ, l, (state_ssm_fwd, state_ssm_bwd), None)
        ks.append(k_p.reshape(BATCH, SEQ, N_KV_HEADS, HEAD_DIM))
        vs.append(v_p.reshape(BATCH, SEQ, N_KV_HEADS, HEAD_DIM))

        j = l // 2
        g2, w_o = norm2_g[l][None, :], w_out[l].astype(BF16)
        if l % 2 == 0:
            x, h = _outproj([o_p, o_s], [y_p, y_s], x_segs, mods, g2, w_o, None)
            x_segs = [_ffn(h, x, mods, ffn_w_gate, ffn_w_up, ffn_w_down, j)]
        else:
            router_wt = jnp.pad(router_w[j].T, ((0, 16 - N_EXPERTS), (0, 0)))
            x, h, meta, counts = _outproj([o_p, o_s], [y_p, y_s], x_segs, mods, g2, w_o, router_wt)
            pos1, pos2, vt, ve, nv, lo, hi = _route_plan(meta, counts)
            xs = _dispatch(h, pos1, pos2)
            ys = _experts(xs, moe_w_gate[j].astype(BF16), moe_w_up[j].astype(BF16),
                          moe_w_down[j].astype(BF16), vt, ve, nv, lo, hi)
            y_prompt, y_sample = _combine(ys, x, mods, _pad_lanes(meta[:2].T), pos1, pos2, final_norm_g[None, :])

    return (y_prompt.reshape(BATCH, SEQ, D_MODEL), y_sample.reshape(DEC_BATCH, DEC_SEQ, D_MODEL),
            jnp.stack(ks, axis=1), jnp.stack(vs, axis=1), states[0], states[1])
```

```python
import functools

import jax
import jax.numpy as jnp
from jax import lax
from jax.experimental import pallas as pl
from jax.experimental.pallas import tpu as pltpu

F32 = jnp.float32
BF16 = jnp.bfloat16

D_MODEL = 2048
BATCH = 16
SEQ = 256
DEPTH = 2
DEC_BATCH = 2
DEC_SEQ = 1024
PAST_LEN = 512
GRID_W = 64
D_ATTN = 1024
D_SSD = 1024
HEAD_DIM = 128
N_Q_HEADS = 8
N_KV_HEADS = 2
Q_PER_KV = 4
KV_DIM = 256
ROT_HALF = 64
ROPE_THETA = 10000.0
SSD_HEAD_DIM = 64
SSD_HEADS = 16
SSD_GROUPS = 2
HEADS_PER_GROUP = 8
D_STATE = 128
CONV_DIM = 1536
CHUNK = 128
N_MAIN = 4096
F_DENSE = 5632
N_EXPERTS = 8
F_EXPERT = 1024
EPS = 1e-6

T_PROMPT = BATCH * SEQ
T_SAMPLE = DEC_BATCH * DEC_SEQ
T_ALL = T_PROMPT + T_SAMPLE
N_COND = 16

VMEM_LIMIT = 56 * 1024 * 1024


def _cparams(sem):
    return pltpu.CompilerParams(dimension_semantics=sem, vmem_limit_bytes=VMEM_LIMIT)


def _mod_group(i, tm):
    return jnp.maximum(0, (i * tm - T_PROMPT + DEC_SEQ) // DEC_SEQ)


def _silu(x):
    return x * jax.nn.sigmoid(x)


def _rms(x, g):
    ms = jnp.mean(x * x, axis=-1, keepdims=True)
    return x * lax.rsqrt(ms + EPS) * g


def _dot(a, b):
    return jnp.dot(a, b, preferred_element_type=F32)


def _dot_nt(a, b):
    return lax.dot_general(a, b, (((1,), (1,)), ((), ())), preferred_element_type=F32)


def _split3(x):
    hi = x.astype(BF16)
    r1 = x - hi.astype(F32)
    mid = r1.astype(BF16)
    r2 = r1 - mid.astype(F32)
    return hi, mid, r2.astype(BF16)


def _ada_kernel(c_ref, w_ref, b_ref, o_ref):
    s = _silu(c_ref[...]).astype(BF16)
    o_ref[...] = _dot(s, w_ref[...].astype(BF16)) + b_ref[...]


def _ada_mods(cond, ada_w, ada_b):
    tn = 1024
    n_out = 6 * D_MODEL
    return pl.pallas_call(
        _ada_kernel,
        grid=(DEPTH, n_out // tn),
        in_specs=[
            pl.BlockSpec((N_COND, D_MODEL), lambda l, j: (0, 0)),
            pl.BlockSpec((None, D_MODEL, tn), lambda l, j: (l, 0, j)),
            pl.BlockSpec((None, 1, tn), lambda l, j: (l, 0, j)),
        ],
        out_specs=pl.BlockSpec((None, N_COND, tn), lambda l, j: (l, 0, j)),
        out_shape=jax.ShapeDtypeStruct((DEPTH, N_COND, n_out), F32),
        compiler_params=_cparams(("parallel", "parallel")),
        name="ada_mods",
    )(cond, ada_w, ada_b.reshape(DEPTH, 1, n_out))


def _inproj_kernel(x_ref, mod_ref, g_ref, w_ref, wdt_ref, proj_ref, dt_ref, h_scr):
    @pl.when(pl.program_id(1) == 0)
    def _():
        h = _rms(x_ref[...], g_ref[...]) * (1.0 + mod_ref[1:2, :]) + mod_ref[0:1, :]
        hb = h.astype(BF16)
        h_scr[...] = hb
        n_dt = wdt_ref.shape[0]
        wdt = jnp.concatenate([wdt_ref[...], jnp.zeros((128 - n_dt, D_MODEL), F32)], axis=0)
        dt_ref[...] = _dot_nt(hb, wdt.astype(BF16))

    proj_ref[...] = _dot_nt(h_scr[...], w_ref[...].astype(BF16))


_ANY = pl.BlockSpec(memory_space=pl.ANY)


def _inproj(x, row0, mods, g, w_in_t, layer):
    tm, tn = 1024, 512
    blk0 = row0 // tm
    nrows = x.shape[0]
    n_dt = w_in_t.shape[1] - N_MAIN
    return pl.pallas_call(
        _inproj_kernel,
        grid=(nrows // tm, N_MAIN // tn),
        in_specs=[
            pl.BlockSpec((tm, D_MODEL), lambda i, j: (i, 0)),
            pl.BlockSpec((None, 6, D_MODEL), lambda i, j: (_mod_group(blk0 + i, tm), 0, 0)),
            pl.BlockSpec((1, D_MODEL), lambda i, j: (0, 0)),
            pl.BlockSpec((None, tn, D_MODEL), lambda i, j: (layer, j, 0)),
            pl.BlockSpec((None, n_dt, D_MODEL), lambda i, j: (layer, N_MAIN // n_dt, 0)),
        ],
        out_specs=[
            pl.BlockSpec((tm, tn), lambda i, j: (i, j)),
            pl.BlockSpec((tm, 128), lambda i, j: (i, 0)),
        ],
        out_shape=[
            jax.ShapeDtypeStruct((nrows, N_MAIN), F32),
            jax.ShapeDtypeStruct((nrows, 128), F32),
        ],
        scratch_shapes=[pltpu.VMEM((tm, D_MODEL), BF16)],
        compiler_params=_cparams(("parallel", "arbitrary")),
        name="inproj",
    )(x, mods, g, w_in_t, w_in_t)


def _rope(x, cos, sin_signed):
    lane = lax.broadcasted_iota(jnp.int32, x.shape, 1)
    first = (lane // (ROT_HALF // 2)) % 2 == 0
    swapped = jnp.where(first, pltpu.roll(x, HEAD_DIM - ROT_HALF // 2, 1), pltpu.roll(x, ROT_HALF // 2, 1))
    return x * cos + swapped * sin_signed


def _attn_kernel(*refs, nk_new, has_ctx):
    if has_ctx:
        (q_ref, kv_ref, qg_ref, kg_ref, og_ref, ck_ref, cv_ref, cq_ref, sq_ref, ckk_ref, skk_ref,
         o_ref, kb_scr, vb_scr, o_scr) = refs
    else:
        (q_ref, kv_ref, qg_ref, kg_ref, og_ref, o_ref, ko_ref, vo_ref, kb_scr, vb_scr, o_scr) = refs

    @pl.when(pl.program_id(1) == 0)
    def _():
        for g in range(N_KV_HEADS):
            sl = slice(g * HEAD_DIM, (g + 1) * HEAD_DIM)
            kn = _rms(kv_ref[:, sl], kg_ref[...])
            if has_ctx:
                kb_scr[0:nk_new, sl] = _rope(kn, ckk_ref[...], skk_ref[...]).astype(BF16)
                kb_scr[nk_new:, sl] = ck_ref[:, sl].astype(BF16)
            else:
                ko_ref[:, sl] = kn
                kb_scr[:, sl] = kn.astype(BF16)
        v = kv_ref[:, KV_DIM:]
        vb_scr[0:nk_new, :] = v.astype(BF16)
        if has_ctx:
            vb_scr[nk_new:, :] = cv_ref[...].astype(BF16)
        else:
            vo_ref[...] = v

    scale = HEAD_DIM ** -0.5
    for h in range(N_Q_HEADS):
        g = h // Q_PER_KV
        sl = slice(h * HEAD_DIM, (h + 1) * HEAD_DIM)
        gsl = slice(g * HEAD_DIM, (g + 1) * HEAD_DIM)
        qn = _rms(q_ref[:, sl], qg_ref[...])
        if has_ctx:
            qn = _rope(qn, cq_ref[...], sq_ref[...])
        s = _dot_nt(qn.astype(BF16), kb_scr[:, gsl]) * scale
        e = jnp.exp(s - jnp.max(s, axis=-1, keepdims=True))
        l = jnp.sum(e, axis=-1, keepdims=True)
        o_scr[:, sl] = _dot(e.astype(BF16), vb_scr[:, gsl]) / l
    o_ref[...] = _rms(o_scr[...], og_ref[...]).astype(BF16)


def _attention_prompt(proj, row0, qg, kg, og):
    n = SEQ
    blk0 = row0 // n
    kern = functools.partial(_attn_kernel, nk_new=n, has_ctx=False)
    vec = lambda w: pl.BlockSpec((1, w), lambda b, i: (0, 0))
    return pl.pallas_call(
        kern,
        grid=(BATCH, 1),
        in_specs=[
            pl.BlockSpec((n, D_ATTN), lambda b, i: (blk0 + b, 0)),
            pl.BlockSpec((n, 2 * KV_DIM), lambda b, i: (blk0 + b, 2)),
            vec(HEAD_DIM), vec(HEAD_DIM), vec(D_ATTN),
        ],
        out_specs=[
            pl.BlockSpec((n, D_ATTN), lambda b, i: (b, 0)),
            pl.BlockSpec((None, n, KV_DIM), lambda b, i: (b, 0, 0)),
            pl.BlockSpec((None, n, KV_DIM), lambda b, i: (b, 0, 0)),
        ],
        out_shape=[
            jax.ShapeDtypeStruct((T_PROMPT, D_ATTN), BF16),
            jax.ShapeDtypeStruct((BATCH, n, KV_DIM), F32),
            jax.ShapeDtypeStruct((BATCH, n, KV_DIM), F32),
        ],
        scratch_shapes=[
            pltpu.VMEM((n, KV_DIM), BF16), pltpu.VMEM((n, KV_DIM), BF16), pltpu.VMEM((n, D_ATTN), F32),
        ],
        compiler_params=_cparams(("parallel", "arbitrary")),
        name="attn_prompt",
    )(proj, proj, qg, kg, og)


def _attention_sample(proj, row0, qg, kg, og, ck, cv, cos, sin_signed, layer):
    n, tq = DEC_SEQ, 512
    nq = n // tq
    nk = n + PAST_LEN
    kern = functools.partial(_attn_kernel, nk_new=n, has_ctx=True)
    vec = lambda w: pl.BlockSpec((1, w), lambda b, i: (0, 0))
    q_blk0 = row0 // tq
    kv_blk0 = row0 // n
    return pl.pallas_call(
        kern,
        grid=(DEC_BATCH, nq),
        in_specs=[
            pl.BlockSpec((tq, D_ATTN), lambda b, i: (q_blk0 + b * nq + i, 0)),
            pl.BlockSpec((n, 2 * KV_DIM), lambda b, i: (kv_blk0 + b, 2)),
            vec(HEAD_DIM), vec(HEAD_DIM), vec(D_ATTN),
            pl.BlockSpec((None, None, PAST_LEN, KV_DIM), lambda b, i: (b, layer, 0, 0)),
            pl.BlockSpec((None, None, PAST_LEN, KV_DIM), lambda b, i: (b, layer, 0, 0)),
            pl.BlockSpec((tq, HEAD_DIM), lambda b, i: (i, 0)),
            pl.BlockSpec((tq, HEAD_DIM), lambda b, i: (i, 0)),
            pl.BlockSpec((n, HEAD_DIM), lambda b, i: (0, 0)),
            pl.BlockSpec((n, HEAD_DIM), lambda b, i: (0, 0)),
        ],
        out_specs=pl.BlockSpec((tq, D_ATTN), lambda b, i: (b * nq + i, 0)),
        out_shape=jax.ShapeDtypeStruct((T_SAMPLE, D_ATTN), BF16),
        scratch_shapes=[
            pltpu.VMEM((nk, KV_DIM), BF16), pltpu.VMEM((nk, KV_DIM), BF16), pltpu.VMEM((tq, D_ATTN), F32),
        ],
        compiler_params=_cparams(("parallel", "arbitrary")),
        name="attn_sample",
    )(proj, proj, qg, kg, og, ck, cv, cos, sin_signed, cos, sin_signed)


def _conv_silu(x, w, b):
    n = x.shape[0]
    row = lax.broadcasted_iota(jnp.int32, (n, 1), 0)
    prev = jnp.where(row == 0, 0.0, pltpu.roll(x, 1, 0))
    nxt = jnp.where(row == n - 1, 0.0, pltpu.roll(x, n - 1, 0))
    return _silu(prev * w[0:1, :] + x * w[1:2, :] + nxt * w[2:3, :] + b)


def _softplus(x):
    return jnp.maximum(x, 0.0) + jnp.log1p(jnp.exp(-jnp.abs(x)))


def _pair_cols(m, li, lo_lanes):
    return jnp.where(lo_lanes, m[:, li:li + 1], m[:, li + 1:li + 2])


def _ssd_kernel(*refs, n, has_init, n_prev):
    if has_init:
        (za_ref, zb_ref, xa_ref, xb_ref, bc_ref, dt_ref, cw_ref, cb_ref, dtb_ref, alog_ref, dsk_ref, ng_ref,
         sf0_ref, sb0_ref, y_ref, xc_scr, bcc_scr, dts_scr, y_scr, s_scr) = refs
    elif n_prev:
        (za_ref, zb_ref, xa_ref, xb_ref, bc_ref, dt_ref, cw_ref, cb_ref, dtb_ref, alog_ref, dsk_ref, ng_ref,
         psf_ref, psb_ref, y_ref, sf_ref, sb_ref, xc_scr, bcc_scr, dts_scr, y_scr, s_scr) = refs
    else:
        (za_ref, zb_ref, xa_ref, xb_ref, bc_ref, dt_ref, cw_ref, cb_ref, dtb_ref, alog_ref, dsk_ref, ng_ref,
         y_ref, sf_ref, sb_ref, xc_scr, bcc_scr, dts_scr, y_scr, s_scr) = refs
    nc = n // CHUNK
    gw = HEADS_PER_GROUP * SSD_HEAD_DIM

    xc_scr[:, 0:gw] = _conv_silu(xa_ref[...], cw_ref[:, 0:gw], cb_ref[:, 0:gw])
    xc_scr[:, gw:] = _conv_silu(xb_ref[...], cw_ref[:, gw:2 * gw], cb_ref[:, gw:2 * gw])
    bcc_scr[...] = _conv_silu(bc_ref[...], cw_ref[:, 2 * gw:], cb_ref[:, 2 * gw:])
    y_scr[...] = xc_scr[...] * dsk_ref[...]
    dts_scr[...] = _softplus(dt_ref[...] + dtb_ref[...])
    for g in range(SSD_GROUPS):
        if has_init:
            s_scr[0, g] = sf0_ref[g * HEADS_PER_GROUP:(g + 1) * HEADS_PER_GROUP].reshape(gw, D_STATE).T
            s_scr[1, g] = sb0_ref[g * HEADS_PER_GROUP:(g + 1) * HEADS_PER_GROUP].reshape(gw, D_STATE).T
        else:
            s_scr[0, g] = jnp.zeros((D_STATE, gw), F32)
            s_scr[1, g] = jnp.zeros((D_STATE, gw), F32)

    a_row = -jnp.exp(alog_ref[...])
    ri = lax.broadcasted_iota(jnp.int32, (CHUNK, CHUNK), 0)
    ci = lax.broadcasted_iota(jnp.int32, (CHUNK, CHUNK), 1)
    lo_lanes = ci < SSD_HEAD_DIM
    lo_row = lo_lanes[0:1, :]

    def scan_chunk(dirn, c):
        mask = (ci <= ri) if dirn == 0 else (ci >= ri)
        tri = mask.astype(BF16)
        lane0 = dirn * SSD_HEADS
        r0 = pl.multiple_of(c * CHUNK, CHUNK)
        rows = pl.ds(r0, CHUNK)
        dt = dts_scr[rows, :]
        d1, d2, d3 = _split3(dt * a_row)
        cs = _dot(tri, d1) + _dot(tri, d2) + _dot(tri, d3)
        cs_t = cs.T
        dt_t = dt.T
        total = cs[CHUNK - 1:CHUNK, :] if dirn == 0 else cs[0:1, :]
        e_in = jnp.exp(cs)
        to_end = jnp.exp(total - cs) * dt
        dec = jnp.exp(total)
        for g in range(SSD_GROUPS):
            bm = bcc_scr[rows, g * D_STATE:(g + 1) * D_STATE]
            cm = bcc_scr[rows, (SSD_GROUPS + g) * D_STATE:(SSD_GROUPS + g + 1) * D_STATE].astype(BF16)
            gmat = _dot_nt(cm, bm.astype(BF16))
            st = s_scr[dirn, g]
            y_off = _dot(cm, st.astype(BF16))
            xs_parts, dec_parts = [], []
            for j in range(HEADS_PER_GROUP // 2):
                li = lane0 + g * HEADS_PER_GROUP + 2 * j
                lanes = slice((g * 4 + j) * 128, (g * 4 + j + 1) * 128)
                a_parts = []
                for hh in range(2):
                    diff = cs[:, li + hh:li + hh + 1] - cs_t[li + hh:li + hh + 1, :]
                    w = jnp.exp(jnp.where(mask, diff, -jnp.inf)) * dt_t[li + hh:li + hh + 1, :]
                    a_parts.append((gmat * w).astype(BF16))
                x_pair = xc_scr[rows, lanes]
                x2 = jnp.concatenate([jnp.where(lo_lanes, x_pair, 0.0), jnp.where(lo_lanes, 0.0, x_pair)],
                                     axis=0).astype(BF16)
                y_diag = _dot(jnp.concatenate(a_parts, axis=1), x2)
                y_pair = y_diag + y_off[:, j * 128:(j + 1) * 128] * _pair_cols(e_in, li, lo_lanes)
                y_scr[rows, lanes] = y_scr[rows, lanes] + y_pair
                xs_parts.append((x_pair * _pair_cols(to_end, li, lo_lanes)).astype(BF16))
                dec_parts.append(_pair_cols(dec, li, lo_row))
            ds = _dot(bm.T.astype(BF16), jnp.concatenate(xs_parts, axis=1))
            s_scr[dirn, g] = st * jnp.concatenate(dec_parts, axis=1) + ds

    def body(i, carry):
        scan_chunk(0, i)
        scan_chunk(1, nc - 1 - i)
        return carry

    lax.fori_loop(0, nc, body, 0)

    def finish(c, carry):
        rows = pl.ds(pl.multiple_of(c * CHUNK, CHUNK), CHUNK)
        ya = y_scr[rows, 0:gw] * _silu(za_ref[rows, :])
        yb = y_scr[rows, gw:] * _silu(zb_ref[rows, :])
        ms = (jnp.sum(ya * ya, axis=-1, keepdims=True) + jnp.sum(yb * yb, axis=-1, keepdims=True)) / D_SSD
        inv = lax.rsqrt(ms + EPS)
        y_ref[rows, 0:gw] = (ya * inv * ng_ref[:, 0:gw]).astype(BF16)
        y_ref[rows, gw:] = (yb * inv * ng_ref[:, gw:]).astype(BF16)
        return carry

    lax.fori_loop(0, nc, finish, 0)

    if not has_init:
        if n_prev:
            sf_ref[0:n_prev] = psf_ref[...]
            sb_ref[0:n_prev] = psb_ref[...]
        for g in range(SSD_GROUPS):
            hs = slice(g * HEADS_PER_GROUP, (g + 1) * HEADS_PER_GROUP)
            sf_ref[n_prev, hs] = s_scr[0, g].T.reshape(HEADS_PER_GROUP, SSD_HEAD_DIM, D_STATE)
            sb_ref[n_prev, hs] = s_scr[1, g].T.reshape(HEADS_PER_GROUP, SSD_HEAD_DIM, D_STATE)


def _ssd(proj, dt_raw, p, n, nb, row_blk0, layer, init, prev_states):
    has_init = init is not None
    n_prev = 0 if has_init else layer
    kern = functools.partial(_ssd_kernel, n=n, has_init=has_init, n_prev=n_prev)
    col = lambda cb: pl.BlockSpec((n, 512), lambda b: (row_blk0 + b, cb))
    vec = lambda r, w: pl.BlockSpec((r, w), lambda b: (0, 0))
    layers_spec = lambda k: pl.BlockSpec((None, k, SSD_HEADS, SSD_HEAD_DIM, D_STATE), lambda b: (b, 0, 0, 0, 0))
    in_specs = [
        col(3), col(4), col(5), col(6), col(7),
        pl.BlockSpec((n, 128), lambda b: (row_blk0 + b, 0)),
        vec(3, CONV_DIM), vec(1, CONV_DIM), vec(1, 128), vec(1, 128), vec(1, D_SSD), vec(1, D_SSD),
    ]
    args = [proj, proj, proj, proj, proj, dt_raw, p['conv_w'], p['conv_b'], p['dt_bias'], p['a_log'],
            p['d_skip'], p['ssd_norm_g']]
    y_spec = pl.BlockSpec((n, D_SSD), lambda b: (b, 0))
    y_shape = jax.ShapeDtypeStruct((nb * n, D_SSD), BF16)
    if has_init:
        init_spec = pl.BlockSpec((None, None, SSD_HEADS, SSD_HEAD_DIM, D_STATE), lambda b: (b, layer, 0, 0, 0))
        in_specs += [init_spec, init_spec]
        args += list(init)
        out_specs, out_shape = y_spec, y_shape
    else:
        if n_prev:
            in_specs += [layers_spec(n_prev), layers_spec(n_prev)]
            args += list(prev_states)
        st_shape = jax.ShapeDtypeStruct((nb, layer + 1, SSD_HEADS, SSD_HEAD_DIM, D_STATE), F32)
        out_specs = [y_spec, layers_spec(layer + 1), layers_spec(layer + 1)]
        out_shape = [y_shape, st_shape, st_shape]
    return pl.pallas_call(
        kern,
        grid=(nb,),
        in_specs=in_specs,
        out_specs=out_specs,
        out_shape=out_shape,
        scratch_shapes=[
            pltpu.VMEM((n, D_SSD), F32), pltpu.VMEM((n, 512), F32), pltpu.VMEM((n, 128), F32),
            pltpu.VMEM((n, D_SSD), F32), pltpu.VMEM((2, SSD_GROUPS, D_STATE, 512), F32),
        ],
        compiler_params=_cparams(("parallel",)),
        name="ssd_sample" if has_init else "ssd_prompt",
    )(*args)


def _seg_specs(segs, tm):
    specs, bounds, start = [], [], 0
    for a in segs:
        b0, nblk = start // tm, a.shape[0] // tm
        specs.append(pl.BlockSpec((tm, a.shape[1]), lambda i, b0=b0, nblk=nblk: (jnp.clip(i - b0, 0, nblk - 1), 0)))
        bounds.append(b0)
        start += a.shape[0]
    return specs, tuple(bounds)


def _seg_pick(refs, bounds):
    i = pl.program_id(0)
    v = refs[0][...]
    for ref, b0 in zip(refs[1:], bounds[1:]):
        v = jnp.where(i >= b0, ref[...], v)
    return v


def _outproj_kernel(*refs, with_router, o_bounds, y_bounds, x_bounds):
    refs = list(refs)
    o_refs = [refs.pop(0) for _ in o_bounds]
    y_refs = [refs.pop(0) for _ in y_bounds]
    x_refs = [refs.pop(0) for _ in x_bounds]
    if with_router:
        mod_ref, g_ref, w_ref, rw_ref, xo_ref, h_ref, meta_ref, cnt_ref, carry_scr = refs
    else:
        mod_ref, g_ref, w_ref, xo_ref, h_ref = refs
    a = jnp.concatenate([_seg_pick(o_refs, o_bounds), _seg_pick(y_refs, y_bounds)], axis=1)
    xn = _seg_pick(x_refs, x_bounds) + mod_ref[2:3, :] * _dot(a, w_ref[...])
    xo_ref[...] = xn
    h = _rms(xn, g_ref[...]) * (1.0 + mod_ref[4:5, :]) + mod_ref[3:4, :]
    h_ref[...] = h.astype(h_ref.dtype)
    if with_router:
        tm = h.shape[0]

        @pl.when(pl.program_id(0) == 0)
        def _():
            carry_scr[...] = jnp.zeros_like(carry_scr)

        h1, h2, _ = _split3(h)
        w1, w2, _ = _split3(rw_ref[...])
        logits = _dot_nt(w1, h1) + _dot_nt(w2, h1) + _dot_nt(w1, h2)
        row = lax.broadcasted_iota(jnp.int32, logits.shape, 0)
        logits = jnp.where(row < N_EXPERTS, logits, -jnp.inf)
        e = jnp.exp(logits - jnp.max(logits, axis=0, keepdims=True))
        probs = e / jnp.sum(e, axis=0, keepdims=True)
        p1 = jnp.max(probs, axis=0, keepdims=True)
        i1 = jnp.min(jnp.where(probs == p1, row, 16), axis=0, keepdims=True)
        rest = jnp.where(row == i1, -1.0, probs)
        p2 = jnp.max(rest, axis=0, keepdims=True)
        i2 = jnp.min(jnp.where(rest == p2, row, 16), axis=0, keepdims=True)
        hit1, hit2 = row == i1, row == i2
        onehot = jnp.where(hit1 | hit2, 1.0, 0.0)
        ti = lax.broadcasted_iota(jnp.int32, (tm, tm), 0)
        tj = lax.broadcasted_iota(jnp.int32, (tm, tm), 1)
        before = jnp.where(ti < tj, 1.0, 0.0).astype(BF16)
        rank = carry_scr[:, 0:1] + _dot(onehot.astype(BF16), before)
        r1 = jnp.sum(jnp.where(hit1, rank, 0.0), axis=0, keepdims=True)
        r2 = jnp.sum(jnp.where(hit2, rank, 0.0), axis=0, keepdims=True)
        carry_scr[...] = carry_scr[...] + jnp.sum(onehot, axis=1, keepdims=True)
        cnt_ref[...] = carry_scr[...]
        r8 = lax.broadcasted_iota(jnp.int32, (8, tm), 0)
        vals = [p1 / (p1 + p2), p2 / (p1 + p2), i1.astype(F32), i2.astype(F32), r1, r2]
        meta = jnp.zeros((8, tm), F32)
        for k, v in enumerate(vals):
            meta = jnp.where(r8 == k, v, meta)
        meta_ref[...] = meta


def _outproj(o_segs, y_segs, x_segs, mods, g, w_out, router_wt):
    tm = 512
    with_router = router_wt is not None
    o_specs, o_bounds = _seg_specs(o_segs, tm)
    y_specs, y_bounds = _seg_specs(y_segs, tm)
    x_specs, x_bounds = _seg_specs(x_segs, tm)
    kern = functools.partial(_outproj_kernel, with_router=with_router, o_bounds=o_bounds, y_bounds=y_bounds,
                             x_bounds=x_bounds)
    in_specs = o_specs + y_specs + x_specs + [
        pl.BlockSpec((None, 6, D_MODEL), lambda i: (_mod_group(i, tm), 0, 0)),
        pl.BlockSpec((1, D_MODEL), lambda i: (0, 0)),
        pl.BlockSpec((D_MODEL, D_MODEL), lambda i: (0, 0), pipeline_mode=pl.Buffered(1)),
    ]
    args = list(o_segs) + list(y_segs) + list(x_segs) + [mods, g, w_out]
    row_spec = pl.BlockSpec((tm, D_MODEL), lambda i: (i, 0))
    out_specs = [row_spec, row_spec]
    out_shape = [jax.ShapeDtypeStruct((T_ALL, D_MODEL), F32),
                 jax.ShapeDtypeStruct((T_ALL, D_MODEL), F32 if with_router else BF16)]
    scratch = []
    if with_router:
        in_specs.append(pl.BlockSpec((16, D_MODEL), lambda i: (0, 0)))
        args.append(router_wt)
        out_specs += [pl.BlockSpec((8, tm), lambda i: (0, i)), pl.BlockSpec((16, 128), lambda i: (0, 0))]
        out_shape += [jax.ShapeDtypeStruct((8, T_ALL), F32), jax.ShapeDtypeStruct((16, 128), F32)]
        scratch = [pltpu.VMEM((16, 128), F32)]
    return pl.pallas_call(
        kern,
        grid=(T_ALL // tm,),
        in_specs=in_specs,
        out_specs=out_specs,
        out_shape=out_shape,
        scratch_shapes=scratch,
        compiler_params=_cparams(("arbitrary",)),
        name="outproj_router" if with_router else "outproj",
    )(*args)


MOE_ROWS = 2 * T_ALL
MOE_TILE = 256
MOE_TILES = MOE_ROWS // MOE_TILE
MOE_VISITS = MOE_TILES + N_EXPERTS - 1


def _row_copy(src, s, dst, d, sem):
    return pltpu.make_async_copy(src.at[pl.ds(s, 1)], dst.at[pl.ds(d, 1)], sem)


def _dispatch_kernel(p1_ref, p2_ref, h_ref, xs_ref, sem):
    tm = h_ref.shape[0]

    def issue(r, c):
        _row_copy(h_ref, r, xs_ref, p1_ref[0, 0, r], sem.at[0]).start()
        _row_copy(h_ref, r, xs_ref, p2_ref[0, 0, r], sem.at[1]).start()
        return c

    lax.fori_loop(0, tm, issue, 0, unroll=8)
    pltpu.make_async_copy(h_ref, xs_ref.at[pl.ds(0, tm)], sem.at[0]).wait()
    pltpu.make_async_copy(h_ref, xs_ref.at[pl.ds(0, tm)], sem.at[1]).wait()


def _dispatch(h, pos1, pos2):
    tm = 512
    nt = T_ALL // tm
    idx = lambda: pl.BlockSpec((1, 1, tm), lambda i: (i, 0, 0), memory_space=pltpu.SMEM)
    return pl.pallas_call(
        _dispatch_kernel,
        grid=(nt,),
        in_specs=[idx(), idx(), pl.BlockSpec((tm, D_MODEL), lambda i: (i, 0))],
        out_specs=pl.BlockSpec(memory_space=pl.ANY),
        out_shape=jax.ShapeDtypeStruct((MOE_ROWS, D_MODEL), F32),
        scratch_shapes=[pltpu.SemaphoreType.DMA((2,))],
        compiler_params=_cparams(("arbitrary",)),
        name="moe_dispatch",
    )(pos1.reshape(nt, 1, tm), pos2.reshape(nt, 1, tm), h)


def _experts_kernel(vt_ref, ve_ref, nv_ref, lo_ref, hi_ref, xs_ref, wg_ref, wu_ref, wd_ref, y_ref):
    v = pl.program_id(0)

    @pl.when(v < nv_ref[0])
    def _():
        e = ve_ref[v]
        x = xs_ref[...].astype(BF16)
        hid = _silu(_dot(x, wg_ref[...])) * _dot(x, wu_ref[...])
        y = _dot(hid.astype(BF16), wd_ref[...])
        row = vt_ref[v] * MOE_TILE + lax.broadcasted_iota(jnp.int32, (MOE_TILE, 1), 0)
        mine = (row >= lo_ref[e]) & (row < hi_ref[e])
        first_visit = (v == 0) | (vt_ref[jnp.maximum(v - 1, 0)] != vt_ref[v])

        @pl.when(first_visit)
        def _():
            y_ref[...] = jnp.where(mine, y, 0.0)

        @pl.when(jnp.logical_not(first_visit))
        def _():
            y_ref[...] = jnp.where(mine, y, y_ref[...])


def _experts(xs, wg, wu, wd, vt, ve, nv, lo, hi):
    grid_spec = pltpu.PrefetchScalarGridSpec(
        num_scalar_prefetch=5,
        grid=(MOE_VISITS,),
        in_specs=[
            pl.BlockSpec((MOE_TILE, D_MODEL), lambda v, vt, ve, nv, lo, hi: (vt[v], 0)),
            pl.BlockSpec((None, D_MODEL, F_EXPERT), lambda v, vt, ve, nv, lo, hi: (ve[v], 0, 0)),
            pl.BlockSpec((None, D_MODEL, F_EXPERT), lambda v, vt, ve, nv, lo, hi: (ve[v], 0, 0)),
            pl.BlockSpec((None, F_EXPERT, D_MODEL), lambda v, vt, ve, nv, lo, hi: (ve[v], 0, 0)),
        ],
        out_specs=pl.BlockSpec((MOE_TILE, D_MODEL), lambda v, vt, ve, nv, lo, hi: (vt[v], 0)),
    )
    return pl.pallas_call(
        _experts_kernel,
        grid_spec=grid_spec,
        out_shape=jax.ShapeDtypeStruct((MOE_ROWS, D_MODEL), F32),
        compiler_params=_cparams(("arbitrary",)),
        name="moe_experts",
    )(vt, ve, nv, lo, hi, xs, wg, wu, wd)


def _combine_kernel(p1c_ref, p2c_ref, p1n_ref, p2n_ref, y_hbm, x_ref, mod_ref, gate_ref, fg_ref,
                    outp_ref, outs_ref, ya_buf, yb_buf, sem):
    i = pl.program_id(0)
    n = pl.num_programs(0)
    tm = x_ref.shape[0]
    slot = i % 2

    def gather(pa_ref, pb_ref, s):
        def issue(r, c):
            _row_copy(y_hbm, pa_ref[0, 0, r], ya_buf.at[s], r, sem.at[0, s]).start()
            _row_copy(y_hbm, pb_ref[0, 0, r], yb_buf.at[s], r, sem.at[1, s]).start()
            return c

        lax.fori_loop(0, tm, issue, 0, unroll=8)

    @pl.when(i == 0)
    def _():
        gather(p1c_ref, p2c_ref, 0)

    @pl.when(i + 1 < n)
    def _():
        gather(p1n_ref, p2n_ref, 1 - slot)

    pltpu.make_async_copy(y_hbm.at[pl.ds(0, tm)], ya_buf.at[slot], sem.at[0, slot]).wait()
    pltpu.make_async_copy(y_hbm.at[pl.ds(0, tm)], yb_buf.at[slot], sem.at[1, slot]).wait()
    g = gate_ref[...]
    mix = g[:, 0:1] * ya_buf[slot] + g[:, 1:2] * yb_buf[slot]
    xo = _rms(x_ref[...] + mod_ref[5:6, :] * mix, fg_ref[...])

    @pl.when(i < T_PROMPT // tm)
    def _():
        outp_ref[...] = xo

    @pl.when(i >= T_PROMPT // tm)
    def _():
        outs_ref[...] = xo


def _combine(y, x, mods, gate_cols, pos1, pos2, final_g):
    tm = 256
    nt = T_ALL // tm
    ntp = T_PROMPT // tm
    cur = lambda: pl.BlockSpec((1, 1, tm), lambda i: (i, 0, 0), memory_space=pltpu.SMEM)
    nxt = lambda: pl.BlockSpec((1, 1, tm), lambda i: (jnp.minimum(i + 1, nt - 1), 0, 0), memory_space=pltpu.SMEM)
    p1, p2 = pos1.reshape(nt, 1, tm), pos2.reshape(nt, 1, tm)
    return pl.pallas_call(
        _combine_kernel,
        grid=(nt,),
        in_specs=[
            cur(), cur(), nxt(), nxt(),
            pl.BlockSpec(memory_space=pl.ANY),
            pl.BlockSpec((tm, D_MODEL), lambda i: (i, 0)),
            pl.BlockSpec((None, 6, D_MODEL), lambda i: (_mod_group(i, tm), 0, 0)),
            pl.BlockSpec((tm, 128), lambda i: (i, 0)),
            pl.BlockSpec((1, D_MODEL), lambda i: (0, 0)),
        ],
        out_specs=[
            pl.BlockSpec((tm, D_MODEL), lambda i: (jnp.minimum(i, ntp - 1), 0)),
            pl.BlockSpec((tm, D_MODEL), lambda i: (jnp.maximum(i - ntp, 0), 0)),
        ],
        out_shape=[
            jax.ShapeDtypeStruct((T_PROMPT, D_MODEL), F32), jax.ShapeDtypeStruct((T_SAMPLE, D_MODEL), F32),
        ],
        scratch_shapes=[
            pltpu.VMEM((2, tm, D_MODEL), F32), pltpu.VMEM((2, tm, D_MODEL), F32),
            pltpu.SemaphoreType.DMA((2, 2)),
        ],
        compiler_params=_cparams(("arbitrary",)),
        name="moe_combine",
    )(p1, p2, p1, p2, y, x, mods, gate_cols, final_g)


def _route_plan(meta, counts):
    i1, i2 = meta[2].astype(jnp.int32), meta[3].astype(jnp.int32)
    r1, r2 = meta[4].astype(jnp.int32), meta[5].astype(jnp.int32)
    cnt = counts[:N_EXPERTS, 0].astype(jnp.int32)
    hi = jnp.cumsum(cnt)
    lo = hi - cnt
    ex = jnp.arange(N_EXPERTS, dtype=jnp.int32)
    pos1 = jnp.sum(jnp.where(i1[:, None] == ex[None, :], lo[None, :], 0), axis=1) + r1
    pos2 = jnp.sum(jnp.where(i2[:, None] == ex[None, :], lo[None, :], 0), axis=1) + r2
    first_tile = lo // MOE_TILE
    n_vis_e = jnp.where(cnt > 0, (hi - 1) // MOE_TILE - first_tile + 1, 0)
    vis_hi = jnp.cumsum(n_vis_e)
    vis_lo = vis_hi - n_vis_e
    nv = vis_hi[-1]
    v = jnp.minimum(jnp.arange(MOE_VISITS, dtype=jnp.int32), nv - 1)
    ve = jnp.minimum(jnp.sum(v[:, None] >= vis_hi[None, :], axis=1), N_EXPERTS - 1).astype(jnp.int32)
    pick = lambda tab: jnp.sum(jnp.where(ve[:, None] == ex[None, :], tab[None, :], 0), axis=1)
    vt = (pick(first_tile) + v - pick(vis_lo)).astype(jnp.int32)
    return pos1, pos2, vt, ve, nv.reshape(1).astype(jnp.int32), lo.astype(jnp.int32), hi.astype(jnp.int32)


def _ffn_kernel(*refs, n_cast):
    h_ref, x_hbm, mod_ref, wg_ref, wu_ref, wd_ref = refs[:6]
    cast_in = refs[6:6 + n_cast]
    out_ref = refs[6 + n_cast]
    cast_out = refs[7 + n_cast:7 + 2 * n_cast]
    x_buf, sem = refs[7 + 2 * n_cast:]
    for src, dst in zip(cast_in, cast_out):
        dst[...] = src[...].astype(BF16)
    i, f = pl.program_id(0), pl.program_id(1)
    tm = h_ref.shape[0]
    x_copy = pltpu.make_async_copy(x_hbm.at[pl.ds(pl.multiple_of(i * tm, tm), tm)], x_buf, sem)

    @pl.when(f == 0)
    def _():
        x_copy.start()
        out_ref[...] = jnp.zeros_like(out_ref)

    h = h_ref[...]
    hid = _silu(_dot(h, wg_ref[0].astype(BF16))) * _dot(h, wu_ref[0].astype(BF16))
    out_ref[...] += _dot(hid.astype(BF16), wd_ref[0].astype(BF16))

    @pl.when(f == pl.num_programs(1) - 1)
    def _():
        x_copy.wait()
        out_ref[...] = x_buf[...] + mod_ref[5:6, :] * out_ref[...]


FFN_CAST_STEPS = 128


def _ffn(h, x, mods, wg, wu, wd, j, to_bf16):
    tm, tf = 1024, 256
    nf = F_DENSE // tf
    assert (T_ALL // tm) * nf >= FFN_CAST_STEPS

    def cast_spec(a):
        rows = a.shape[0] // FFN_CAST_STEPS
        return pl.BlockSpec((rows, a.shape[1]), lambda i, f: (jnp.minimum(i * nf + f, FFN_CAST_STEPS - 1), 0))

    cast_specs = [cast_spec(a) for a in to_bf16]
    outs = pl.pallas_call(
        functools.partial(_ffn_kernel, n_cast=len(to_bf16)),
        grid=(T_ALL // tm, nf),
        in_specs=[
            pl.BlockSpec((tm, D_MODEL), lambda i, f: (i, 0)),
            _ANY,
            pl.BlockSpec((None, 6, D_MODEL), lambda i, f: (_mod_group(i, tm), 0, 0)),
            pl.BlockSpec((1, D_MODEL, tf), lambda i, f: (j, 0, f)),
            pl.BlockSpec((1, D_MODEL, tf), lambda i, f: (j, 0, f)),
            pl.BlockSpec((1, tf, D_MODEL), lambda i, f: (j, f, 0)),
        ] + cast_specs,
        out_specs=[pl.BlockSpec((tm, D_MODEL), lambda i, f: (i, 0))] + cast_specs,
        out_shape=[jax.ShapeDtypeStruct((T_ALL, D_MODEL), F32)]
        + [jax.ShapeDtypeStruct(a.shape, BF16) for a in to_bf16],
        scratch_shapes=[pltpu.VMEM((tm, D_MODEL), F32), pltpu.SemaphoreType.DMA(())],
        compiler_params=_cparams(("arbitrary", "arbitrary")),
        name="dense_ffn",
    )(h, x, mods, wg, wu, wd, *to_bf16)
    return outs[0], outs[1:]


def _rope_tables():
    n = DEC_SEQ
    rows = n // GRID_W
    t_row = jnp.repeat(jnp.arange(rows, dtype=F32), GRID_W)
    t_col = jnp.tile(jnp.arange(GRID_W, dtype=F32), rows)
    inv = 1.0 / (ROPE_THETA ** (jnp.arange(0, ROT_HALF, 2, dtype=F32) / ROT_HALF))
    ar, ac = t_row[:, None] * inv, t_col[:, None] * inv
    cos = jnp.concatenate([jnp.cos(ar), jnp.cos(ar), jnp.cos(ac), jnp.cos(ac)], axis=-1)
    sin_signed = jnp.concatenate([-jnp.sin(ar), jnp.sin(ar), -jnp.sin(ac), jnp.sin(ac)], axis=-1)
    return cos, sin_signed


def _pad_lanes(v, width=128):
    return jnp.pad(v, ((0, 0), (0, width - v.shape[-1])))


def kernel(x_prompt, x_sample, c, cache_k, cache_v, state_ssm_fwd, state_ssm_bwd, c_ctx, ada_w, ada_b, norm1_g, norm2_g, w_in, q_norm_g, k_norm_g, conv_w, conv_b, a_log_fwd, a_log_bwd, dt_bias_fwd, dt_bias_bwd, d_skip, ssd_norm_g, attn_out_g, w_out, ffn_w_gate, ffn_w_up, ffn_w_down, router_w, moe_w_gate, moe_w_up, moe_w_down, final_norm_g):
    assert DEPTH % 2 == 0
    cond = jnp.concatenate([c_ctx[None, :], c, jnp.zeros((N_COND - 1 - DEC_BATCH, D_MODEL), F32)], axis=0)
    mods_all = _ada_mods(cond, ada_w, ada_b).reshape(DEPTH, N_COND, 6, D_MODEL)
    cos, sin_signed = _rope_tables()
    ck = cache_k.reshape(DEC_BATCH, DEPTH, PAST_LEN, KV_DIM)
    cv = cache_v.reshape(DEC_BATCH, DEPTH, PAST_LEN, KV_DIM)

    w_in_t = jnp.swapaxes(w_in, 1, 2)
    x_segs = [x_prompt.reshape(T_PROMPT, D_MODEL), x_sample.reshape(T_SAMPLE, D_MODEL)]
    ks, vs, states = [], [], None
    for l in range(DEPTH):
        mods = mods_all[l]
        g1 = norm1_g[l][None, :]
        if len(x_segs) == 2:
            proj_p, dt_p = _inproj(x_segs[0], 0, mods, g1, w_in_t, l)
            proj_s, dt_s = _inproj(x_segs[1], T_PROMPT, mods, g1, w_in_t, l)
            row_s = 0
        else:
            proj_p, dt_p = proj_s, dt_s = _inproj(x_segs[0], 0, mods, g1, w_in_t, l)
            row_s = T_PROMPT

        qg, kg, og = q_norm_g[l][None, :], k_norm_g[l][None, :], attn_out_g[l][None, :]
        o_p, k_p, v_p = _attention_prompt(proj_p, 0, qg, kg, og)
        o_s = _attention_sample(proj_s, row_s, qg, kg, og, ck, cv, cos, sin_signed, l)

        p = {
            'conv_w': conv_w[l], 'conv_b': conv_b[l][None, :],
            'dt_bias': _pad_lanes(jnp.concatenate([dt_bias_fwd[l], dt_bias_bwd[l]])[None, :]),
            'a_log': _pad_lanes(jnp.concatenate([a_log_fwd[l], a_log_bwd[l]])[None, :]),
            'd_skip': jnp.repeat(d_skip[l], SSD_HEAD_DIM)[None, :],
            'ssd_norm_g': ssd_norm_g[l][None, :],
        }
        y_p, sf, sb = _ssd(proj_p, dt_p, p, SEQ, BATCH, 0, l, None, states)
        states = (sf, sb)
        y_s = _ssd(proj_s, dt_s, p, DEC_SEQ, DEC_BATCH, row_s // DEC_SEQ, l, (state_ssm_fwd, state_ssm_bwd), None)
        ks.append(k_p.reshape(BATCH, SEQ, N_KV_HEADS, HEAD_DIM))
        vs.append(v_p.reshape(BATCH, SEQ, N_KV_HEADS, HEAD_DIM))

        j = l // 2
        g2, w_o = norm2_g[l][None, :], w_out[l].astype(BF16)
        if l % 2 == 0:
            x, h = _outproj([o_p, o_s], [y_p, y_s], x_segs, mods, g2, w_o, None)
            x, (eg, eu, ed) = _ffn(h, x, mods, ffn_w_gate, ffn_w_up, ffn_w_down, j, [
                moe_w_gate[j].reshape(N_EXPERTS * D_MODEL, F_EXPERT),
                moe_w_up[j].reshape(N_EXPERTS * D_MODEL, F_EXPERT),
                moe_w_down[j].reshape(N_EXPERTS * F_EXPERT, D_MODEL)])
            expert_w = (eg.reshape(N_EXPERTS, D_MODEL, F_EXPERT), eu.reshape(N_EXPERTS, D_MODEL, F_EXPERT),
                        ed.reshape(N_EXPERTS, F_EXPERT, D_MODEL))
            x_segs = [x]
        else:
            router_wt = jnp.pad(router_w[j].T, ((0, 16 - N_EXPERTS), (0, 0)))
            x, h, meta, counts = _outproj([o_p, o_s], [y_p, y_s], x_segs, mods, g2, w_o, router_wt)
            pos1, pos2, vt, ve, nv, lo, hi = _route_plan(meta, counts)
            xs = _dispatch(h, pos1, pos2)
            ys = _experts(xs, *expert_w, vt, ve, nv, lo, hi)
            y_prompt, y_sample = _combine(ys, x, mods, _pad_lanes(meta[:2].T), pos1, pos2, final_norm_g[None, :])

    return (y_prompt.reshape(BATCH, SEQ, D_MODEL), y_sample.reshape(DEC_BATCH, DEC_SEQ, D_MODEL),
            jnp.stack(ks, axis=1), jnp.stack(vs, axis=1), states[0], states[1])
```

```python
import functools

import jax
import jax.numpy as jnp
from jax import lax
from jax.experimental import pallas as pl
from jax.experimental.pallas import tpu as pltpu

F32 = jnp.float32
BF16 = jnp.bfloat16

D_MODEL = 2048
BATCH = 16
SEQ = 256
DEPTH = 2
DEC_BATCH = 2
DEC_SEQ = 1024
PAST_LEN = 512
GRID_W = 64
D_ATTN = 1024
D_SSD = 1024
HEAD_DIM = 128
N_Q_HEADS = 8
N_KV_HEADS = 2
Q_PER_KV = 4
KV_DIM = 256
ROT_HALF = 64
ROPE_THETA = 10000.0
SSD_HEAD_DIM = 64
SSD_HEADS = 16
SSD_GROUPS = 2
HEADS_PER_GROUP = 8
D_STATE = 128
CONV_DIM = 1536
CHUNK = 128
N_MAIN = 4096
F_DENSE = 5632
N_EXPERTS = 8
F_EXPERT = 1024
EPS = 1e-6

T_PROMPT = BATCH * SEQ
T_SAMPLE = DEC_BATCH * DEC_SEQ
T_ALL = T_PROMPT + T_SAMPLE
N_COND = 16

VMEM_LIMIT = 56 * 1024 * 1024


def _cparams(sem):
    return pltpu.CompilerParams(dimension_semantics=sem, vmem_limit_bytes=VMEM_LIMIT)


def _mod_group(i, tm):
    return jnp.maximum(0, (i * tm - T_PROMPT + DEC_SEQ) // DEC_SEQ)


def _silu(x):
    return x * jax.nn.sigmoid(x)


def _rms(x, g):
    ms = jnp.mean(x * x, axis=-1, keepdims=True)
    return x * lax.rsqrt(ms + EPS) * g


def _dot(a, b):
    return jnp.dot(a, b, preferred_element_type=F32)


def _dot_nt(a, b):
    return lax.dot_general(a, b, (((1,), (1,)), ((), ())), preferred_element_type=F32)


def _split3(x):
    hi = x.astype(BF16)
    r1 = x - hi.astype(F32)
    mid = r1.astype(BF16)
    r2 = r1 - mid.astype(F32)
    return hi, mid, r2.astype(BF16)


def _ada_kernel(c_ref, w_ref, b_ref, o_ref):
    s = _silu(c_ref[...]).astype(BF16)
    o_ref[...] = _dot(s, w_ref[...].astype(BF16)) + b_ref[...]


def _ada_mods(cond, ada_w, ada_b):
    tn = 1024
    n_out = 6 * D_MODEL
    return pl.pallas_call(
        _ada_kernel,
        grid=(DEPTH, n_out // tn),
        in_specs=[
            pl.BlockSpec((N_COND, D_MODEL), lambda l, j: (0, 0)),
            pl.BlockSpec((None, D_MODEL, tn), lambda l, j: (l, 0, j)),
            pl.BlockSpec((None, 1, tn), lambda l, j: (l, 0, j)),
        ],
        out_specs=pl.BlockSpec((None, N_COND, tn), lambda l, j: (l, 0, j)),
        out_shape=jax.ShapeDtypeStruct((DEPTH, N_COND, n_out), F32),
        compiler_params=_cparams(("parallel", "parallel")),
        name="ada_mods",
    )(cond, ada_w, ada_b.reshape(DEPTH, 1, n_out))


def _inproj_kernel(x_ref, mod_ref, g_ref, w_ref, wdt_ref, proj_ref, dt_ref, h_scr):
    @pl.when(pl.program_id(1) == 0)
    def _():
        h = _rms(x_ref[...], g_ref[...]) * (1.0 + mod_ref[1:2, :]) + mod_ref[0:1, :]
        hb = h.astype(BF16)
        h_scr[...] = hb
        n_dt = wdt_ref.shape[0]
        wdt = jnp.concatenate([wdt_ref[...], jnp.zeros((128 - n_dt, D_MODEL), F32)], axis=0)
        dt_ref[...] = _dot_nt(hb, wdt.astype(BF16))

    proj_ref[...] = _dot_nt(h_scr[...], w_ref[...].astype(BF16))


_ANY = pl.BlockSpec(memory_space=pl.ANY)


def _inproj(x, row0, mods, g, w_in_t, layer):
    tm, tn = 1024, 512
    blk0 = row0 // tm
    nrows = x.shape[0]
    n_dt = w_in_t.shape[1] - N_MAIN
    return pl.pallas_call(
        _inproj_kernel,
        grid=(nrows // tm, N_MAIN // tn),
        in_specs=[
            pl.BlockSpec((tm, D_MODEL), lambda i, j: (i, 0)),
            pl.BlockSpec((None, 6, D_MODEL), lambda i, j: (_mod_group(blk0 + i, tm), 0, 0)),
            pl.BlockSpec((1, D_MODEL), lambda i, j: (0, 0)),
            pl.BlockSpec((None, tn, D_MODEL), lambda i, j: (layer, j, 0)),
            pl.BlockSpec((None, n_dt, D_MODEL), lambda i, j: (layer, N_MAIN // n_dt, 0)),
        ],
        out_specs=[
            pl.BlockSpec((tm, tn), lambda i, j: (i, j)),
            pl.BlockSpec((tm, 128), lambda i, j: (i, 0)),
        ],
        out_shape=[
            jax.ShapeDtypeStruct((nrows, N_MAIN), F32),
            jax.ShapeDtypeStruct((nrows, 128), F32),
        ],
        scratch_shapes=[pltpu.VMEM((tm, D_MODEL), BF16)],
        compiler_params=_cparams(("parallel", "arbitrary")),
        name="inproj",
    )(x, mods, g, w_in_t, w_in_t)


def _rope(x, cos, sin_signed):
    lane = lax.broadcasted_iota(jnp.int32, x.shape, 1)
    first = (lane // (ROT_HALF // 2)) % 2 == 0
    swapped = jnp.where(first, pltpu.roll(x, HEAD_DIM - ROT_HALF // 2, 1), pltpu.roll(x, ROT_HALF // 2, 1))
    return x * cos + swapped * sin_signed


def _attn_kernel(*refs, nk_new, has_ctx):
    if has_ctx:
        (q_ref, kv_ref, qg_ref, kg_ref, og_ref, ck_ref, cv_ref, cq_ref, sq_ref, ckk_ref, skk_ref,
         o_ref, kb_scr, vb_scr, o_scr) = refs
    else:
        (q_ref, kv_ref, qg_ref, kg_ref, og_ref, o_ref, ko_ref, vo_ref, kb_scr, vb_scr, o_scr) = refs

    @pl.when(pl.program_id(1) == 0)
    def _():
        for g in range(N_KV_HEADS):
            sl = slice(g * HEAD_DIM, (g + 1) * HEAD_DIM)
            kn = _rms(kv_ref[:, sl], kg_ref[...])
            if has_ctx:
                kb_scr[0:nk_new, sl] = _rope(kn, ckk_ref[...], skk_ref[...]).astype(BF16)
                kb_scr[nk_new:, sl] = ck_ref[:, sl].astype(BF16)
            else:
                ko_ref[:, sl] = kn
                kb_scr[:, sl] = kn.astype(BF16)
        v = kv_ref[:, KV_DIM:]
        vb_scr[0:nk_new, :] = v.astype(BF16)
        if has_ctx:
            vb_scr[nk_new:, :] = cv_ref[...].astype(BF16)
        else:
            vo_ref[...] = v

    scale = HEAD_DIM ** -0.5
    for h in range(N_Q_HEADS):
        g = h // Q_PER_KV
        sl = slice(h * HEAD_DIM, (h + 1) * HEAD_DIM)
        gsl = slice(g * HEAD_DIM, (g + 1) * HEAD_DIM)
        qn = _rms(q_ref[:, sl], qg_ref[...])
        if has_ctx:
            qn = _rope(qn, cq_ref[...], sq_ref[...])
        s = _dot_nt(qn.astype(BF16), kb_scr[:, gsl]) * scale
        e = jnp.exp(s - jnp.max(s, axis=-1, keepdims=True))
        l = jnp.sum(e, axis=-1, keepdims=True)
        o_scr[:, sl] = _dot(e.astype(BF16), vb_scr[:, gsl]) / l
    o_ref[...] = _rms(o_scr[...], og_ref[...]).astype(BF16)


def _attention_prompt(proj, row0, qg, kg, og):
    n = SEQ
    blk0 = row0 // n
    kern = functools.partial(_attn_kernel, nk_new=n, has_ctx=False)
    vec = lambda w: pl.BlockSpec((1, w), lambda b, i: (0, 0))
    return pl.pallas_call(
        kern,
        grid=(BATCH, 1),
        in_specs=[
            pl.BlockSpec((n, D_ATTN), lambda b, i: (blk0 + b, 0)),
            pl.BlockSpec((n, 2 * KV_DIM), lambda b, i: (blk0 + b, 2)),
            vec(HEAD_DIM), vec(HEAD_DIM), vec(D_ATTN),
        ],
        out_specs=[
            pl.BlockSpec((n, D_ATTN), lambda b, i: (b, 0)),
            pl.BlockSpec((None, n, KV_DIM), lambda b, i: (b, 0, 0)),
            pl.BlockSpec((None, n, KV_DIM), lambda b, i: (b, 0, 0)),
        ],
        out_shape=[
            jax.ShapeDtypeStruct((T_PROMPT, D_ATTN), BF16),
            jax.ShapeDtypeStruct((BATCH, n, KV_DIM), F32),
            jax.ShapeDtypeStruct((BATCH, n, KV_DIM), F32),
        ],
        scratch_shapes=[
            pltpu.VMEM((n, KV_DIM), BF16), pltpu.VMEM((n, KV_DIM), BF16), pltpu.VMEM((n, D_ATTN), F32),
        ],
        compiler_params=_cparams(("parallel", "arbitrary")),
        name="attn_prompt",
    )(proj, proj, qg, kg, og)


def _attention_sample(proj, row0, qg, kg, og, ck, cv, cos, sin_signed, layer):
    n, tq = DEC_SEQ, 512
    nq = n // tq
    nk = n + PAST_LEN
    kern = functools.partial(_attn_kernel, nk_new=n, has_ctx=True)
    vec = lambda w: pl.BlockSpec((1, w), lambda b, i: (0, 0))
    q_blk0 = row0 // tq
    kv_blk0 = row0 // n
    return pl.pallas_call(
        kern,
        grid=(DEC_BATCH, nq),
        in_specs=[
            pl.BlockSpec((tq, D_ATTN), lambda b, i: (q_blk0 + b * nq + i, 0)),
            pl.BlockSpec((n, 2 * KV_DIM), lambda b, i: (kv_blk0 + b, 2)),
            vec(HEAD_DIM), vec(HEAD_DIM), vec(D_ATTN),
            pl.BlockSpec((None, None, PAST_LEN, KV_DIM), lambda b, i: (b, layer, 0, 0)),
            pl.BlockSpec((None, None, PAST_LEN, KV_DIM), lambda b, i: (b, layer, 0, 0)),
            pl.BlockSpec((tq, HEAD_DIM), lambda b, i: (i, 0)),
            pl.BlockSpec((tq, HEAD_DIM), lambda b, i: (i, 0)),
            pl.BlockSpec((n, HEAD_DIM), lambda b, i: (0, 0)),
            pl.BlockSpec((n, HEAD_DIM), lambda b, i: (0, 0)),
        ],
        out_specs=pl.BlockSpec((tq, D_ATTN), lambda b, i: (b * nq + i, 0)),
        out_shape=jax.ShapeDtypeStruct((T_SAMPLE, D_ATTN), BF16),
        scratch_shapes=[
            pltpu.VMEM((nk, KV_DIM), BF16), pltpu.VMEM((nk, KV_DIM), BF16), pltpu.VMEM((tq, D_ATTN), F32),
        ],
        compiler_params=_cparams(("parallel", "arbitrary")),
        name="attn_sample",
    )(proj, proj, qg, kg, og, ck, cv, cos, sin_signed, cos, sin_signed)


def _conv_silu(x, w, b):
    n = x.shape[0]
    row = lax.broadcasted_iota(jnp.int32, (n, 1), 0)
    prev = jnp.where(row == 0, 0.0, pltpu.roll(x, 1, 0))
    nxt = jnp.where(row == n - 1, 0.0, pltpu.roll(x, n - 1, 0))
    return _silu(prev * w[0:1, :] + x * w[1:2, :] + nxt * w[2:3, :] + b)


def _softplus(x):
    return jnp.maximum(x, 0.0) + jnp.log1p(jnp.exp(-jnp.abs(x)))


def _ssd_kernel(*refs, n, has_init, n_prev):
    if has_init:
        (za_ref, zb_ref, xa_ref, xb_ref, bc_ref, dt_ref, cw_ref, cb_ref, dtb_ref, alog_ref, dsk_ref, ng_ref,
         sf0_ref, sb0_ref, y_ref, xc_scr, bcc_scr, dts_scr, xt_scr, yt_scr, s_scr) = refs
    elif n_prev:
        (za_ref, zb_ref, xa_ref, xb_ref, bc_ref, dt_ref, cw_ref, cb_ref, dtb_ref, alog_ref, dsk_ref, ng_ref,
         psf_ref, psb_ref, y_ref, sf_ref, sb_ref, xc_scr, bcc_scr, dts_scr, xt_scr, yt_scr, s_scr) = refs
    else:
        (za_ref, zb_ref, xa_ref, xb_ref, bc_ref, dt_ref, cw_ref, cb_ref, dtb_ref, alog_ref, dsk_ref, ng_ref,
         y_ref, sf_ref, sb_ref, xc_scr, bcc_scr, dts_scr, xt_scr, yt_scr, s_scr) = refs
    nc = n // CHUNK
    gw = HEADS_PER_GROUP * SSD_HEAD_DIM

    xc_scr[:, 0:gw] = _conv_silu(xa_ref[...], cw_ref[:, 0:gw], cb_ref[:, 0:gw])
    xc_scr[:, gw:] = _conv_silu(xb_ref[...], cw_ref[:, gw:2 * gw], cb_ref[:, gw:2 * gw])
    bcc_scr[...] = _conv_silu(bc_ref[...], cw_ref[:, 2 * gw:], cb_ref[:, 2 * gw:])
    dts_scr[...] = _softplus(dt_ref[...] + dtb_ref[...])
    for g in range(SSD_GROUPS):
        hs = slice(g * HEADS_PER_GROUP, (g + 1) * HEADS_PER_GROUP)
        if has_init:
            s_scr[0, g] = sf0_ref[hs].reshape(gw, D_STATE)
            s_scr[1, g] = sb0_ref[hs].reshape(gw, D_STATE)
        else:
            s_scr[0, g] = jnp.zeros((gw, D_STATE), F32)
            s_scr[1, g] = jnp.zeros((gw, D_STATE), F32)

    def to_channel_major(c, carry):
        rows = pl.ds(pl.multiple_of(c * CHUNK, CHUNK), CHUNK)
        xt_scr[c] = xc_scr[rows, :].T
        yt_scr[c] = jnp.zeros((D_SSD, CHUNK), F32)
        return carry

    lax.fori_loop(0, nc, to_channel_major, 0)

    a_row = -jnp.exp(alog_ref[...])
    ri = lax.broadcasted_iota(jnp.int32, (CHUNK, CHUNK), 0)
    ci = lax.broadcasted_iota(jnp.int32, (CHUNK, CHUNK), 1)
    lower, upper = ci <= ri, ci >= ri
    n_dirs_heads = 2 * SSD_HEADS

    def scan_chunk(dirn, c):
        tri = (lower if dirn == 0 else upper).astype(BF16)
        valid_st = upper if dirn == 0 else lower
        tri_t = valid_st.astype(BF16)
        row0 = dirn * SSD_HEADS
        rows = pl.ds(pl.multiple_of(c * CHUNK, CHUNK), CHUNK)
        dt = dts_scr[rows, :]
        d = dt * a_row
        d1, d2, d3 = _split3(d)
        cs = _dot(tri, d1) + _dot(tri, d2) + _dot(tri, d3)
        dt_t = dt.T[0:n_dirs_heads, :]
        e1, e2, e3 = _split3(d.T[0:n_dirs_heads, :])
        cs_t = _dot(e1, tri_t) + _dot(e2, tri_t) + _dot(e3, tri_t)
        total = cs_t[:, CHUNK - 1:CHUNK] if dirn == 0 else cs_t[:, 0:1]
        e_in_t = jnp.exp(cs_t)
        to_end_t = jnp.exp(total - cs_t) * dt_t
        dec_t = jnp.broadcast_to(jnp.exp(total), (n_dirs_heads, D_STATE))
        for g in range(SSD_GROUPS):
            bm = bcc_scr[rows, g * D_STATE:(g + 1) * D_STATE].astype(BF16)
            cm = bcc_scr[rows, (SSD_GROUPS + g) * D_STATE:(SSD_GROUPS + g + 1) * D_STATE]
            g_st = _dot_nt(bm, cm.astype(BF16))
            c_nt = cm.T
            st = s_scr[dirn, g]
            xs_parts, dec_parts = [], []
            for hh in range(HEADS_PER_GROUP):
                h = g * HEADS_PER_GROUP + hh
                r = row0 + h
                ch = slice(h * SSD_HEAD_DIM, (h + 1) * SSD_HEAD_DIM)
                x_t = xt_scr[c, ch, :]
                diff = cs_t[r:r + 1, :] - cs[:, r:r + 1]
                a_st = (g_st * jnp.exp(jnp.where(valid_st, diff, -jnp.inf))).astype(BF16)
                c_e = (c_nt * e_in_t[r:r + 1, :]).astype(BF16)
                x_dt = (x_t * dt_t[r:r + 1, :]).astype(BF16)
                s_h = st[hh * SSD_HEAD_DIM:(hh + 1) * SSD_HEAD_DIM, :].astype(BF16)
                y_h = _dot(jnp.concatenate([x_dt, s_h], axis=1), jnp.concatenate([a_st, c_e], axis=0))
                yt_scr[c, ch, :] = yt_scr[c, ch, :] + y_h
                xs_parts.append((x_t * to_end_t[r:r + 1, :]).astype(BF16))
                dec_parts.append(jnp.broadcast_to(dec_t[r:r + 1, :], (SSD_HEAD_DIM, D_STATE)))
            ds = _dot(jnp.concatenate(xs_parts, axis=0), bm)
            s_scr[dirn, g] = st * jnp.concatenate(dec_parts, axis=0) + ds

    def body(i, carry):
        scan_chunk(0, i)
        scan_chunk(1, nc - 1 - i)
        return carry

    lax.fori_loop(0, nc, body, 0)

    def finish(c, carry):
        rows = pl.ds(pl.multiple_of(c * CHUNK, CHUNK), CHUNK)
        y = yt_scr[c].T + xc_scr[rows, :] * dsk_ref[...]
        ya = y[:, 0:gw] * _silu(za_ref[rows, :])
        yb = y[:, gw:] * _silu(zb_ref[rows, :])
        ms = (jnp.sum(ya * ya, axis=-1, keepdims=True) + jnp.sum(yb * yb, axis=-1, keepdims=True)) / D_SSD
        inv = lax.rsqrt(ms + EPS)
        y_ref[rows, 0:gw] = (ya * inv * ng_ref[:, 0:gw]).astype(BF16)
        y_ref[rows, gw:] = (yb * inv * ng_ref[:, gw:]).astype(BF16)
        return carry

    lax.fori_loop(0, nc, finish, 0)

    if not has_init:
        if n_prev:
            sf_ref[0:n_prev] = psf_ref[...]
            sb_ref[0:n_prev] = psb_ref[...]
        for g in range(SSD_GROUPS):
            hs = slice(g * HEADS_PER_GROUP, (g + 1) * HEADS_PER_GROUP)
            sf_ref[n_prev, hs] = s_scr[0, g].reshape(HEADS_PER_GROUP, SSD_HEAD_DIM, D_STATE)
            sb_ref[n_prev, hs] = s_scr[1, g].reshape(HEADS_PER_GROUP, SSD_HEAD_DIM, D_STATE)


def _ssd(proj, dt_raw, p, n, nb, row_blk0, layer, init, prev_states):
    has_init = init is not None
    n_prev = 0 if has_init else layer
    kern = functools.partial(_ssd_kernel, n=n, has_init=has_init, n_prev=n_prev)
    col = lambda cb: pl.BlockSpec((n, 512), lambda b: (row_blk0 + b, cb))
    vec = lambda r, w: pl.BlockSpec((r, w), lambda b: (0, 0))
    layers_spec = lambda k: pl.BlockSpec((None, k, SSD_HEADS, SSD_HEAD_DIM, D_STATE), lambda b: (b, 0, 0, 0, 0))
    in_specs = [
        col(3), col(4), col(5), col(6), col(7),
        pl.BlockSpec((n, 128), lambda b: (row_blk0 + b, 0)),
        vec(3, CONV_DIM), vec(1, CONV_DIM), vec(1, 128), vec(1, 128), vec(1, D_SSD), vec(1, D_SSD),
    ]
    args = [proj, proj, proj, proj, proj, dt_raw, p['conv_w'], p['conv_b'], p['dt_bias'], p['a_log'],
            p['d_skip'], p['ssd_norm_g']]
    y_spec = pl.BlockSpec((n, D_SSD), lambda b: (b, 0))
    y_shape = jax.ShapeDtypeStruct((nb * n, D_SSD), BF16)
    if has_init:
        init_spec = pl.BlockSpec((None, None, SSD_HEADS, SSD_HEAD_DIM, D_STATE), lambda b: (b, layer, 0, 0, 0))
        in_specs += [init_spec, init_spec]
        args += list(init)
        out_specs, out_shape = y_spec, y_shape
    else:
        if n_prev:
            in_specs += [layers_spec(n_prev), layers_spec(n_prev)]
            args += list(prev_states)
        st_shape = jax.ShapeDtypeStruct((nb, layer + 1, SSD_HEADS, SSD_HEAD_DIM, D_STATE), F32)
        out_specs = [y_spec, layers_spec(layer + 1), layers_spec(layer + 1)]
        out_shape = [y_shape, st_shape, st_shape]
    return pl.pallas_call(
        kern,
        grid=(nb,),
        in_specs=in_specs,
        out_specs=out_specs,
        out_shape=out_shape,
        scratch_shapes=[
            pltpu.VMEM((n, D_SSD), F32), pltpu.VMEM((n, 512), F32), pltpu.VMEM((n, 128), F32),
            pltpu.VMEM((n // CHUNK, D_SSD, CHUNK), F32), pltpu.VMEM((n // CHUNK, D_SSD, CHUNK), F32),
            pltpu.VMEM((2, SSD_GROUPS, 512, D_STATE), F32),
        ],
        compiler_params=_cparams(("parallel",)),
        name="ssd_sample" if has_init else "ssd_prompt",
    )(*args)


def _seg_specs(segs, tm):
    specs, bounds, start = [], [], 0
    for a in segs:
        b0, nblk = start // tm, a.shape[0] // tm
        specs.append(pl.BlockSpec((tm, a.shape[1]), lambda i, b0=b0, nblk=nblk: (jnp.clip(i - b0, 0, nblk - 1), 0)))
        bounds.append(b0)
        start += a.shape[0]
    return specs, tuple(bounds)


def _seg_pick(refs, bounds):
    i = pl.program_id(0)
    v = refs[0][...]
    for ref, b0 in zip(refs[1:], bounds[1:]):
        v = jnp.where(i >= b0, ref[...], v)
    return v


def _outproj_kernel(*refs, with_router, o_bounds, y_bounds, x_bounds):
    refs = list(refs)
    o_refs = [refs.pop(0) for _ in o_bounds]
    y_refs = [refs.pop(0) for _ in y_bounds]
    x_refs = [refs.pop(0) for _ in x_bounds]
    if with_router:
        mod_ref, g_ref, w_ref, rw_ref, xo_ref, h_ref, meta_ref, cnt_ref, carry_scr = refs
    else:
        mod_ref, g_ref, w_ref, xo_ref, h_ref = refs
    a = jnp.concatenate([_seg_pick(o_refs, o_bounds), _seg_pick(y_refs, y_bounds)], axis=1)
    xn = _seg_pick(x_refs, x_bounds) + mod_ref[2:3, :] * _dot(a, w_ref[...])
    xo_ref[...] = xn
    h = _rms(xn, g_ref[...]) * (1.0 + mod_ref[4:5, :]) + mod_ref[3:4, :]
    h_ref[...] = h.astype(h_ref.dtype)
    if with_router:
        tm = h.shape[0]

        @pl.when(pl.program_id(0) == 0)
        def _():
            carry_scr[...] = jnp.zeros_like(carry_scr)

        h1, h2, _ = _split3(h)
        w1, w2, _ = _split3(rw_ref[...])
        logits = _dot_nt(w1, h1) + _dot_nt(w2, h1) + _dot_nt(w1, h2)
        row = lax.broadcasted_iota(jnp.int32, logits.shape, 0)
        logits = jnp.where(row < N_EXPERTS, logits, -jnp.inf)
        e = jnp.exp(logits - jnp.max(logits, axis=0, keepdims=True))
        probs = e / jnp.sum(e, axis=0, keepdims=True)
        p1 = jnp.max(probs, axis=0, keepdims=True)
        i1 = jnp.min(jnp.where(probs == p1, row, 16), axis=0, keepdims=True)
        rest = jnp.where(row == i1, -1.0, probs)
        p2 = jnp.max(rest, axis=0, keepdims=True)
        i2 = jnp.min(jnp.where(rest == p2, row, 16), axis=0, keepdims=True)
        hit1, hit2 = row == i1, row == i2
        onehot = jnp.where(hit1 | hit2, 1.0, 0.0)
        ti = lax.broadcasted_iota(jnp.int32, (tm, tm), 0)
        tj = lax.broadcasted_iota(jnp.int32, (tm, tm), 1)
        before = jnp.where(ti < tj, 1.0, 0.0).astype(BF16)
        rank = carry_scr[:, 0:1] + _dot(onehot.astype(BF16), before)
        r1 = jnp.sum(jnp.where(hit1, rank, 0.0), axis=0, keepdims=True)
        r2 = jnp.sum(jnp.where(hit2, rank, 0.0), axis=0, keepdims=True)
        carry_scr[...] = carry_scr[...] + jnp.sum(onehot, axis=1, keepdims=True)
        cnt_ref[...] = carry_scr[...]
        r8 = lax.broadcasted_iota(jnp.int32, (8, tm), 0)
        vals = [p1 / (p1 + p2), p2 / (p1 + p2), i1.astype(F32), i2.astype(F32), r1, r2]
        meta = jnp.zeros((8, tm), F32)
        for k, v in enumerate(vals):
            meta = jnp.where(r8 == k, v, meta)
        meta_ref[...] = meta


def _outproj(o_segs, y_segs, x_segs, mods, g, w_out, router_wt):
    tm = 512
    with_router = router_wt is not None
    o_specs, o_bounds = _seg_specs(o_segs, tm)
    y_specs, y_bounds = _seg_specs(y_segs, tm)
    x_specs, x_bounds = _seg_specs(x_segs, tm)
    kern = functools.partial(_outproj_kernel, with_router=with_router, o_bounds=o_bounds, y_bounds=y_bounds,
                             x_bounds=x_bounds)
    in_specs = o_specs + y_specs + x_specs + [
        pl.BlockSpec((None, 6, D_MODEL), lambda i: (_mod_group(i, tm), 0, 0)),
        pl.BlockSpec((1, D_MODEL), lambda i: (0, 0)),
        pl.BlockSpec((D_MODEL, D_MODEL), lambda i: (0, 0), pipeline_mode=pl.Buffered(1)),
    ]
    args = list(o_segs) + list(y_segs) + list(x_segs) + [mods, g, w_out]
    row_spec = pl.BlockSpec((tm, D_MODEL), lambda i: (i, 0))
    out_specs = [row_spec, row_spec]
    out_shape = [jax.ShapeDtypeStruct((T_ALL, D_MODEL), F32),
                 jax.ShapeDtypeStruct((T_ALL, D_MODEL), F32 if with_router else BF16)]
    scratch = []
    if with_router:
        in_specs.append(pl.BlockSpec((16, D_MODEL), lambda i: (0, 0)))
        args.append(router_wt)
        out_specs += [pl.BlockSpec((8, tm), lambda i: (0, i)), pl.BlockSpec((16, 128), lambda i: (0, 0))]
        out_shape += [jax.ShapeDtypeStruct((8, T_ALL), F32), jax.ShapeDtypeStruct((16, 128), F32)]
        scratch = [pltpu.VMEM((16, 128), F32)]
    return pl.pallas_call(
        kern,
        grid=(T_ALL // tm,),
        in_specs=in_specs,
        out_specs=out_specs,
        out_shape=out_shape,
        scratch_shapes=scratch,
        compiler_params=_cparams(("arbitrary",)),
        name="outproj_router" if with_router else "outproj",
    )(*args)


MOE_ROWS = 2 * T_ALL
MOE_TILE = 256
MOE_TILES = MOE_ROWS // MOE_TILE
MOE_VISITS = MOE_TILES + N_EXPERTS - 1


def _row_copy(src, s, dst, d, sem):
    return pltpu.make_async_copy(src.at[pl.ds(s, 1)], dst.at[pl.ds(d, 1)], sem)


def _dispatch_kernel(p1_ref, p2_ref, h_ref, xs_ref, sem):
    tm = h_ref.shape[0]

    def issue(r, c):
        _row_copy(h_ref, r, xs_ref, p1_ref[0, 0, r], sem.at[0]).start()
        _row_copy(h_ref, r, xs_ref, p2_ref[0, 0, r], sem.at[1]).start()
        return c

    lax.fori_loop(0, tm, issue, 0, unroll=8)
    pltpu.make_async_copy(h_ref, xs_ref.at[pl.ds(0, tm)], sem.at[0]).wait()
    pltpu.make_async_copy(h_ref, xs_ref.at[pl.ds(0, tm)], sem.at[1]).wait()


def _dispatch(h, pos1, pos2):
    tm = 512
    nt = T_ALL // tm
    idx = lambda: pl.BlockSpec((1, 1, tm), lambda i: (i, 0, 0), memory_space=pltpu.SMEM)
    return pl.pallas_call(
        _dispatch_kernel,
        grid=(nt,),
        in_specs=[idx(), idx(), pl.BlockSpec((tm, D_MODEL), lambda i: (i, 0))],
        out_specs=pl.BlockSpec(memory_space=pl.ANY),
        out_shape=jax.ShapeDtypeStruct((MOE_ROWS, D_MODEL), F32),
        scratch_shapes=[pltpu.SemaphoreType.DMA((2,))],
        compiler_params=_cparams(("arbitrary",)),
        name="moe_dispatch",
    )(pos1.reshape(nt, 1, tm), pos2.reshape(nt, 1, tm), h)


def _experts_kernel(vt_ref, ve_ref, nv_ref, lo_ref, hi_ref, xs_ref, wg_ref, wu_ref, wd_ref, y_ref):
    v = pl.program_id(0)

    @pl.when(v < nv_ref[0])
    def _():
        e = ve_ref[v]
        x = xs_ref[...].astype(BF16)
        hid = _silu(_dot(x, wg_ref[...])) * _dot(x, wu_ref[...])
        y = _dot(hid.astype(BF16), wd_ref[...])
        row = vt_ref[v] * MOE_TILE + lax.broadcasted_iota(jnp.int32, (MOE_TILE, 1), 0)
        mine = (row >= lo_ref[e]) & (row < hi_ref[e])
        first_visit = (v == 0) | (vt_ref[jnp.maximum(v - 1, 0)] != vt_ref[v])

        @pl.when(first_visit)
        def _():
            y_ref[...] = jnp.where(mine, y, 0.0)

        @pl.when(jnp.logical_not(first_visit))
        def _():
            y_ref[...] = jnp.where(mine, y, y_ref[...])


def _experts(xs, wg, wu, wd, vt, ve, nv, lo, hi):
    grid_spec = pltpu.PrefetchScalarGridSpec(
        num_scalar_prefetch=5,
        grid=(MOE_VISITS,),
        in_specs=[
            pl.BlockSpec((MOE_TILE, D_MODEL), lambda v, vt, ve, nv, lo, hi: (vt[v], 0)),
            pl.BlockSpec((None, D_MODEL, F_EXPERT), lambda v, vt, ve, nv, lo, hi: (ve[v], 0, 0)),
            pl.BlockSpec((None, D_MODEL, F_EXPERT), lambda v, vt, ve, nv, lo, hi: (ve[v], 0, 0)),
            pl.BlockSpec((None, F_EXPERT, D_MODEL), lambda v, vt, ve, nv, lo, hi: (ve[v], 0, 0)),
        ],
        out_specs=pl.BlockSpec((MOE_TILE, D_MODEL), lambda v, vt, ve, nv, lo, hi: (vt[v], 0)),
    )
    return pl.pallas_call(
        _experts_kernel,
        grid_spec=grid_spec,
        out_shape=jax.ShapeDtypeStruct((MOE_ROWS, D_MODEL), F32),
        compiler_params=_cparams(("arbitrary",)),
        name="moe_experts",
    )(vt, ve, nv, lo, hi, xs, wg, wu, wd)


def _combine_kernel(p1c_ref, p2c_ref, p1n_ref, p2n_ref, y_hbm, x_ref, mod_ref, gate_ref, fg_ref,
                    outp_ref, outs_ref, ya_buf, yb_buf, sem):
    i = pl.program_id(0)
    n = pl.num_programs(0)
    tm = x_ref.shape[0]
    slot = i % 2

    def gather(pa_ref, pb_ref, s):
        def issue(r, c):
            _row_copy(y_hbm, pa_ref[0, 0, r], ya_buf.at[s], r, sem.at[0, s]).start()
            _row_copy(y_hbm, pb_ref[0, 0, r], yb_buf.at[s], r, sem.at[1, s]).start()
            return c

        lax.fori_loop(0, tm, issue, 0, unroll=8)

    @pl.when(i == 0)
    def _():
        gather(p1c_ref, p2c_ref, 0)

    @pl.when(i + 1 < n)
    def _():
        gather(p1n_ref, p2n_ref, 1 - slot)

    pltpu.make_async_copy(y_hbm.at[pl.ds(0, tm)], ya_buf.at[slot], sem.at[0, slot]).wait()
    pltpu.make_async_copy(y_hbm.at[pl.ds(0, tm)], yb_buf.at[slot], sem.at[1, slot]).wait()
    g = gate_ref[...]
    mix = g[:, 0:1] * ya_buf[slot] + g[:, 1:2] * yb_buf[slot]
    xo = _rms(x_ref[...] + mod_ref[5:6, :] * mix, fg_ref[...])

    @pl.when(i < T_PROMPT // tm)
    def _():
        outp_ref[...] = xo

    @pl.when(i >= T_PROMPT // tm)
    def _():
        outs_ref[...] = xo


def _combine(y, x, mods, gate_cols, pos1, pos2, final_g):
    tm = 256
    nt = T_ALL // tm
    ntp = T_PROMPT // tm
    cur = lambda: pl.BlockSpec((1, 1, tm), lambda i: (i, 0, 0), memory_space=pltpu.SMEM)
    nxt = lambda: pl.BlockSpec((1, 1, tm), lambda i: (jnp.minimum(i + 1, nt - 1), 0, 0), memory_space=pltpu.SMEM)
    p1, p2 = pos1.reshape(nt, 1, tm), pos2.reshape(nt, 1, tm)
    return pl.pallas_call(
        _combine_kernel,
        grid=(nt,),
        in_specs=[
            cur(), cur(), nxt(), nxt(),
            pl.BlockSpec(memory_space=pl.ANY),
            pl.BlockSpec((tm, D_MODEL), lambda i: (i, 0)),
            pl.BlockSpec((None, 6, D_MODEL), lambda i: (_mod_group(i, tm), 0, 0)),
            pl.BlockSpec((tm, 128), lambda i: (i, 0)),
            pl.BlockSpec((1, D_MODEL), lambda i: (0, 0)),
        ],
        out_specs=[
            pl.BlockSpec((tm, D_MODEL), lambda i: (jnp.minimum(i, ntp - 1), 0)),
            pl.BlockSpec((tm, D_MODEL), lambda i: (jnp.maximum(i - ntp, 0), 0)),
        ],
        out_shape=[
            jax.ShapeDtypeStruct((T_PROMPT, D_MODEL), F32), jax.ShapeDtypeStruct((T_SAMPLE, D_MODEL), F32),
        ],
        scratch_shapes=[
            pltpu.VMEM((2, tm, D_MODEL), F32), pltpu.VMEM((2, tm, D_MODEL), F32),
            pltpu.SemaphoreType.DMA((2, 2)),
        ],
        compiler_params=_cparams(("arbitrary",)),
        name="moe_combine",
    )(p1, p2, p1, p2, y, x, mods, gate_cols, final_g)


def _route_plan(meta, counts):
    i1, i2 = meta[2].astype(jnp.int32), meta[3].astype(jnp.int32)
    r1, r2 = meta[4].astype(jnp.int32), meta[5].astype(jnp.int32)
    cnt = counts[:N_EXPERTS, 0].astype(jnp.int32)
    hi = jnp.cumsum(cnt)
    lo = hi - cnt
    ex = jnp.arange(N_EXPERTS, dtype=jnp.int32)
    pos1 = jnp.sum(jnp.where(i1[:, None] == ex[None, :], lo[None, :], 0), axis=1) + r1
    pos2 = jnp.sum(jnp.where(i2[:, None] == ex[None, :], lo[None, :], 0), axis=1) + r2
    first_tile = lo // MOE_TILE
    n_vis_e = jnp.where(cnt > 0, (hi - 1) // MOE_TILE - first_tile + 1, 0)
    vis_hi = jnp.cumsum(n_vis_e)
    vis_lo = vis_hi - n_vis_e
    nv = vis_hi[-1]
    v = jnp.minimum(jnp.arange(MOE_VISITS, dtype=jnp.int32), nv - 1)
    ve = jnp.minimum(jnp.sum(v[:, None] >= vis_hi[None, :], axis=1), N_EXPERTS - 1).astype(jnp.int32)
    pick = lambda tab: jnp.sum(jnp.where(ve[:, None] == ex[None, :], tab[None, :], 0), axis=1)
    vt = (pick(first_tile) + v - pick(vis_lo)).astype(jnp.int32)
    return pos1, pos2, vt, ve, nv.reshape(1).astype(jnp.int32), lo.astype(jnp.int32), hi.astype(jnp.int32)


def _ffn_kernel(*refs, n_cast):
    h_ref, x_hbm, mod_ref, wg_ref, wu_ref, wd_ref = refs[:6]
    cast_in = refs[6:6 + n_cast]
    out_ref = refs[6 + n_cast]
    cast_out = refs[7 + n_cast:7 + 2 * n_cast]
    x_buf, sem = refs[7 + 2 * n_cast:]
    for src, dst in zip(cast_in, cast_out):
        dst[...] = src[...].astype(BF16)
    i, f = pl.program_id(0), pl.program_id(1)
    tm = h_ref.shape[0]
    x_copy = pltpu.make_async_copy(x_hbm.at[pl.ds(pl.multiple_of(i * tm, tm), tm)], x_buf, sem)

    @pl.when(f == 0)
    def _():
        x_copy.start()
        out_ref[...] = jnp.zeros_like(out_ref)

    h = h_ref[...]
    hid = _silu(_dot(h, wg_ref[0].astype(BF16))) * _dot(h, wu_ref[0].astype(BF16))
    out_ref[...] += _dot(hid.astype(BF16), wd_ref[0].astype(BF16))

    @pl.when(f == pl.num_programs(1) - 1)
    def _():
        x_copy.wait()
        out_ref[...] = x_buf[...] + mod_ref[5:6, :] * out_ref[...]


FFN_CAST_STEPS = 128


def _ffn(h, x, mods, wg, wu, wd, j, to_bf16):
    tm, tf = 1024, 256
    nf = F_DENSE // tf
    assert (T_ALL // tm) * nf >= FFN_CAST_STEPS

    def cast_spec(a):
        rows = a.shape[0] // FFN_CAST_STEPS
        return pl.BlockSpec((rows, a.shape[1]), lambda i, f: (jnp.minimum(i * nf + f, FFN_CAST_STEPS - 1), 0))

    cast_specs = [cast_spec(a) for a in to_bf16]
    outs = pl.pallas_call(
        functools.partial(_ffn_kernel, n_cast=len(to_bf16)),
        grid=(T_ALL // tm, nf),
        in_specs=[
            pl.BlockSpec((tm, D_MODEL), lambda i, f: (i, 0)),
            _ANY,
            pl.BlockSpec((None, 6, D_MODEL), lambda i, f: (_mod_group(i, tm), 0, 0)),
            pl.BlockSpec((1, D_MODEL, tf), lambda i, f: (j, 0, f)),
            pl.BlockSpec((1, D_MODEL, tf), lambda i, f: (j, 0, f)),
            pl.BlockSpec((1, tf, D_MODEL), lambda i, f: (j, f, 0)),
        ] + cast_specs,
        out_specs=[pl.BlockSpec((tm, D_MODEL), lambda i, f: (i, 0))] + cast_specs,
        out_shape=[jax.ShapeDtypeStruct((T_ALL, D_MODEL), F32)]
        + [jax.ShapeDtypeStruct(a.shape, BF16) for a in to_bf16],
        scratch_shapes=[pltpu.VMEM((tm, D_MODEL), F32), pltpu.SemaphoreType.DMA(())],
        compiler_params=_cparams(("arbitrary", "arbitrary")),
        name="dense_ffn",
    )(h, x, mods, wg, wu, wd, *to_bf16)
    return outs[0], outs[1:]


def _rope_tables():
    n = DEC_SEQ
    rows = n // GRID_W
    t_row = jnp.repeat(jnp.arange(rows, dtype=F32), GRID_W)
    t_col = jnp.tile(jnp.arange(GRID_W, dtype=F32), rows)
    inv = 1.0 / (ROPE_THETA ** (jnp.arange(0, ROT_HALF, 2, dtype=F32) / ROT_HALF))
    ar, ac = t_row[:, None] * inv, t_col[:, None] * inv
    cos = jnp.concatenate([jnp.cos(ar), jnp.cos(ar), jnp.cos(ac), jnp.cos(ac)], axis=-1)
    sin_signed = jnp.concatenate([-jnp.sin(ar), jnp.sin(ar), -jnp.sin(ac), jnp.sin(ac)], axis=-1)
    return cos, sin_signed


def _pad_lanes(v, width=128):
    return jnp.pad(v, ((0, 0), (0, width - v.shape[-1])))


def kernel(x_prompt, x_sample, c, cache_k, cache_v, state_ssm_fwd, state_ssm_bwd, c_ctx, ada_w, ada_b, norm1_g, norm2_g, w_in, q_norm_g, k_norm_g, conv_w, conv_b, a_log_fwd, a_log_bwd, dt_bias_fwd, dt_bias_bwd, d_skip, ssd_norm_g, attn_out_g, w_out, ffn_w_gate, ffn_w_up, ffn_w_down, router_w, moe_w_gate, moe_w_up, moe_w_down, final_norm_g):
    assert DEPTH % 2 == 0
    cond = jnp.concatenate([c_ctx[None, :], c, jnp.zeros((N_COND - 1 - DEC_BATCH, D_MODEL), F32)], axis=0)
    mods_all = _ada_mods(cond, ada_w, ada_b).reshape(DEPTH, N_COND, 6, D_MODEL)
    cos, sin_signed = _rope_tables()
    ck = cache_k.reshape(DEC_BATCH, DEPTH, PAST_LEN, KV_DIM)
    cv = cache_v.reshape(DEC_BATCH, DEPTH, PAST_LEN, KV_DIM)

    w_in_t = jnp.swapaxes(w_in, 1, 2)
    x_segs = [x_prompt.reshape(T_PROMPT, D_MODEL), x_sample.reshape(T_SAMPLE, D_MODEL)]
    ks, vs, states = [], [], None
    for l in range(DEPTH):
        mods = mods_all[l]
        g1 = norm1_g[l][None, :]
        if len(x_segs) == 2:
            proj_p, dt_p = _inproj(x_segs[0], 0, mods, g1, w_in_t, l)
            proj_s, dt_s = _inproj(x_segs[1], T_PROMPT, mods, g1, w_in_t, l)
            row_s = 0
        else:
            proj_p, dt_p = proj_s, dt_s = _inproj(x_segs[0], 0, mods, g1, w_in_t, l)
            row_s = T_PROMPT

        qg, kg, og = q_norm_g[l][None, :], k_norm_g[l][None, :], attn_out_g[l][None, :]
        o_p, k_p, v_p = _attention_prompt(proj_p, 0, qg, kg, og)
        o_s = _attention_sample(proj_s, row_s, qg, kg, og, ck, cv, cos, sin_signed, l)

        p = {
            'conv_w': conv_w[l], 'conv_b': conv_b[l][None, :],
            'dt_bias': _pad_lanes(jnp.concatenate([dt_bias_fwd[l], dt_bias_bwd[l]])[None, :]),
            'a_log': _pad_lanes(jnp.concatenate([a_log_fwd[l], a_log_bwd[l]])[None, :]),
            'd_skip': jnp.repeat(d_skip[l], SSD_HEAD_DIM)[None, :],
            'ssd_norm_g': ssd_norm_g[l][None, :],
        }
        y_p, sf, sb = _ssd(proj_p, dt_p, p, SEQ, BATCH, 0, l, None, states)
        states = (sf, sb)
        y_s = _ssd(proj_s, dt_s, p, DEC_SEQ, DEC_BATCH, row_s // DEC_SEQ, l, (state_ssm_fwd, state_ssm_bwd), None)
        ks.append(k_p.reshape(BATCH, SEQ, N_KV_HEADS, HEAD_DIM))
        vs.append(v_p.reshape(BATCH, SEQ, N_KV_HEADS, HEAD_DIM))

        j = l // 2
        g2, w_o = norm2_g[l][None, :], w_out[l].astype(BF16)
        if l % 2 == 0:
            x, h = _outproj([o_p, o_s], [y_p, y_s], x_segs, mods, g2, w_o, None)
            x, (eg, eu, ed) = _ffn(h, x, mods, ffn_w_gate, ffn_w_up, ffn_w_down, j, [
                moe_w_gate[j].reshape(N_EXPERTS * D_MODEL, F_EXPERT),
                moe_w_up[j].reshape(N_EXPERTS * D_MODEL, F_EXPERT),
                moe_w_down[j].reshape(N_EXPERTS * F_EXPERT, D_MODEL)])
            expert_w = (eg.reshape(N_EXPERTS, D_MODEL, F_EXPERT), eu.reshape(N_EXPERTS, D_MODEL, F_EXPERT),
                        ed.reshape(N_EXPERTS, F_EXPERT, D_MODEL))
            x_segs = [x]
        else:
            router_wt = jnp.pad(router_w[j].T, ((0, 16 - N_EXPERTS), (0, 0)))
            x, h, meta, counts = _outproj([o_p, o_s], [y_p, y_s], x_segs, mods, g2, w_o, router_wt)
            pos1, pos2, vt, ve, nv, lo, hi = _route_plan(meta, counts)
            xs = _dispatch(h, pos1, pos2)
            ys = _experts(xs, *expert_w, vt, ve, nv, lo, hi)
            y_prompt, y_sample = _combine(ys, x, mods, _pad_lanes(meta[:2].T), pos1, pos2, final_norm_g[None, :])

    return (y_prompt.reshape(BATCH, SEQ, D_MODEL), y_sample.reshape(DEC_BATCH, DEC_SEQ, D_MODEL),
            jnp.stack(ks, axis=1), jnp.stack(vs, axis=1), states[0], states[1])
```

```python
import functools

import jax
import jax.numpy as jnp
from jax import lax
from jax.experimental import pallas as pl
from jax.experimental.pallas import tpu as pltpu

F32 = jnp.float32
BF16 = jnp.bfloat16

D_MODEL = 2048
BATCH = 16
SEQ = 256
DEPTH = 2
DEC_BATCH = 2
DEC_SEQ = 1024
PAST_LEN = 512
GRID_W = 64
D_ATTN = 1024
D_SSD = 1024
HEAD_DIM = 128
N_Q_HEADS = 8
N_KV_HEADS = 2
Q_PER_KV = 4
KV_DIM = 256
ROT_HALF = 64
ROPE_THETA = 10000.0
SSD_HEAD_DIM = 64
SSD_HEADS = 16
SSD_GROUPS = 2
HEADS_PER_GROUP = 8
D_STATE = 128
CONV_DIM = 1536
CHUNK = 128
N_MAIN = 4096
F_DENSE = 5632
N_EXPERTS = 8
F_EXPERT = 1024
EPS = 1e-6

T_PROMPT = BATCH * SEQ
T_SAMPLE = DEC_BATCH * DEC_SEQ
T_ALL = T_PROMPT + T_SAMPLE
N_COND = 16

VMEM_LIMIT = 56 * 1024 * 1024


def _cparams(sem):
    return pltpu.CompilerParams(dimension_semantics=sem, vmem_limit_bytes=VMEM_LIMIT)


def _mod_group(i, tm):
    return jnp.maximum(0, (i * tm - T_PROMPT + DEC_SEQ) // DEC_SEQ)


def _silu(x):
    return x * jax.nn.sigmoid(x)


def _rms(x, g):
    ms = jnp.mean(x * x, axis=-1, keepdims=True)
    return x * lax.rsqrt(ms + EPS) * g


def _dot(a, b):
    return jnp.dot(a, b, preferred_element_type=F32)


def _dot_nt(a, b):
    return lax.dot_general(a, b, (((1,), (1,)), ((), ())), preferred_element_type=F32)


def _split3(x):
    hi = x.astype(BF16)
    r1 = x - hi.astype(F32)
    mid = r1.astype(BF16)
    r2 = r1 - mid.astype(F32)
    return hi, mid, r2.astype(BF16)


def _ada_kernel(c_ref, w_ref, b_ref, o_ref):
    s = _silu(c_ref[...]).astype(BF16)
    o_ref[...] = _dot(s, w_ref[...].astype(BF16)) + b_ref[...]


def _ada_mods(cond, ada_w, ada_b):
    tn = 1024
    n_out = 6 * D_MODEL
    return pl.pallas_call(
        _ada_kernel,
        grid=(DEPTH, n_out // tn),
        in_specs=[
            pl.BlockSpec((N_COND, D_MODEL), lambda l, j: (0, 0)),
            pl.BlockSpec((None, D_MODEL, tn), lambda l, j: (l, 0, j)),
            pl.BlockSpec((None, 1, tn), lambda l, j: (l, 0, j)),
        ],
        out_specs=pl.BlockSpec((None, N_COND, tn), lambda l, j: (l, 0, j)),
        out_shape=jax.ShapeDtypeStruct((DEPTH, N_COND, n_out), F32),
        compiler_params=_cparams(("parallel", "parallel")),
        name="ada_mods",
    )(cond, ada_w, ada_b.reshape(DEPTH, 1, n_out))


def _modnorm_kernel(*refs, x_bounds):
    x_refs = refs[:len(x_bounds)]
    mod_ref, g_ref, h_ref = refs[len(x_bounds):]
    h = _rms(_seg_pick(x_refs, x_bounds), g_ref[...]) * (1.0 + mod_ref[1:2, :]) + mod_ref[0:1, :]
    h_ref[...] = h.astype(BF16)


def _modnorm(x_segs, mods, g):
    tm = 512
    x_specs, x_bounds = _seg_specs(x_segs, tm)
    return pl.pallas_call(
        functools.partial(_modnorm_kernel, x_bounds=x_bounds),
        grid=(T_ALL // tm,),
        in_specs=x_specs + [
            pl.BlockSpec((None, 6, D_MODEL), lambda i: (_mod_group(i, tm), 0, 0)),
            pl.BlockSpec((1, D_MODEL), lambda i: (0, 0)),
        ],
        out_specs=pl.BlockSpec((tm, D_MODEL), lambda i: (i, 0)),
        out_shape=jax.ShapeDtypeStruct((T_ALL, D_MODEL), BF16),
        compiler_params=_cparams(("parallel",)),
        name="modnorm",
    )(*x_segs, mods, g)


def _inproj_kernel(h_ref, w_ref, wdt_ref, proj_ref, dt_ref):
    @pl.when(pl.program_id(1) == 0)
    def _():
        n_dt = wdt_ref.shape[0]
        wdt = jnp.concatenate([wdt_ref[...], jnp.zeros((128 - n_dt, D_MODEL), F32)], axis=0)
        dt_ref[...] = _dot_nt(h_ref[...], wdt.astype(BF16))

    proj_ref[...] = _dot_nt(h_ref[...], w_ref[...].astype(BF16))


_ANY = pl.BlockSpec(memory_space=pl.ANY)


def _inproj(h, w_in_t, layer):
    tm, tn = 2048, 512
    n_dt = w_in_t.shape[1] - N_MAIN
    return pl.pallas_call(
        _inproj_kernel,
        grid=(T_ALL // tm, N_MAIN // tn),
        in_specs=[
            pl.BlockSpec((tm, D_MODEL), lambda i, j: (i, 0)),
            pl.BlockSpec((None, tn, D_MODEL), lambda i, j: (layer, j, 0)),
            pl.BlockSpec((None, n_dt, D_MODEL), lambda i, j: (layer, N_MAIN // n_dt, 0)),
        ],
        out_specs=[
            pl.BlockSpec((tm, tn), lambda i, j: (i, j)),
            pl.BlockSpec((tm, 128), lambda i, j: (i, 0)),
        ],
        out_shape=[
            jax.ShapeDtypeStruct((T_ALL, N_MAIN), F32),
            jax.ShapeDtypeStruct((T_ALL, 128), F32),
        ],
        compiler_params=_cparams(("parallel", "arbitrary")),
        name="inproj",
    )(h, w_in_t, w_in_t)


def _rope(x, cos, sin_signed):
    lane = lax.broadcasted_iota(jnp.int32, x.shape, 1)
    first = (lane // (ROT_HALF // 2)) % 2 == 0
    swapped = jnp.where(first, pltpu.roll(x, HEAD_DIM - ROT_HALF // 2, 1), pltpu.roll(x, ROT_HALF // 2, 1))
    return x * cos + swapped * sin_signed


def _attn_kernel(*refs, nk_new, has_ctx):
    if has_ctx:
        (q_ref, kv_ref, qg_ref, kg_ref, og_ref, ck_ref, cv_ref, cq_ref, sq_ref, ckk_ref, skk_ref,
         o_ref, kb_scr, vb_scr, o_scr) = refs
    else:
        (q_ref, kv_ref, qg_ref, kg_ref, og_ref, o_ref, ko_ref, vo_ref, kb_scr, vb_scr, o_scr) = refs

    @pl.when(pl.program_id(1) == 0)
    def _():
        for g in range(N_KV_HEADS):
            sl = slice(g * HEAD_DIM, (g + 1) * HEAD_DIM)
            kn = _rms(kv_ref[:, sl], kg_ref[...])
            if has_ctx:
                kb_scr[0:nk_new, sl] = _rope(kn, ckk_ref[...], skk_ref[...]).astype(BF16)
                kb_scr[nk_new:, sl] = ck_ref[:, sl].astype(BF16)
            else:
                ko_ref[:, sl] = kn
                kb_scr[:, sl] = kn.astype(BF16)
        v = kv_ref[:, KV_DIM:]
        vb_scr[0:nk_new, :] = v.astype(BF16)
        if has_ctx:
            vb_scr[nk_new:, :] = cv_ref[...].astype(BF16)
        else:
            vo_ref[...] = v

    scale = HEAD_DIM ** -0.5
    for h in range(N_Q_HEADS):
        g = h // Q_PER_KV
        sl = slice(h * HEAD_DIM, (h + 1) * HEAD_DIM)
        gsl = slice(g * HEAD_DIM, (g + 1) * HEAD_DIM)
        qn = _rms(q_ref[:, sl], qg_ref[...])
        if has_ctx:
            qn = _rope(qn, cq_ref[...], sq_ref[...])
        s = _dot_nt(qn.astype(BF16), kb_scr[:, gsl]) * scale
        e = jnp.exp(s - jnp.max(s, axis=-1, keepdims=True))
        l = jnp.sum(e, axis=-1, keepdims=True)
        o_scr[:, sl] = _dot(e.astype(BF16), vb_scr[:, gsl]) / l
    o_ref[...] = _rms(o_scr[...], og_ref[...]).astype(BF16)


def _attention_prompt(proj, row0, qg, kg, og):
    n = SEQ
    blk0 = row0 // n
    kern = functools.partial(_attn_kernel, nk_new=n, has_ctx=False)
    vec = lambda w: pl.BlockSpec((1, w), lambda b, i: (0, 0))
    return pl.pallas_call(
        kern,
        grid=(BATCH, 1),
        in_specs=[
            pl.BlockSpec((n, D_ATTN), lambda b, i: (blk0 + b, 0)),
            pl.BlockSpec((n, 2 * KV_DIM), lambda b, i: (blk0 + b, 2)),
            vec(HEAD_DIM), vec(HEAD_DIM), vec(D_ATTN),
        ],
        out_specs=[
            pl.BlockSpec((n, D_ATTN), lambda b, i: (b, 0)),
            pl.BlockSpec((None, n, KV_DIM), lambda b, i: (b, 0, 0)),
            pl.BlockSpec((None, n, KV_DIM), lambda b, i: (b, 0, 0)),
        ],
        out_shape=[
            jax.ShapeDtypeStruct((T_PROMPT, D_ATTN), BF16),
            jax.ShapeDtypeStruct((BATCH, n, KV_DIM), F32),
            jax.ShapeDtypeStruct((BATCH, n, KV_DIM), F32),
        ],
        scratch_shapes=[
            pltpu.VMEM((n, KV_DIM), BF16), pltpu.VMEM((n, KV_DIM), BF16), pltpu.VMEM((n, D_ATTN), F32),
        ],
        compiler_params=_cparams(("parallel", "arbitrary")),
        name="attn_prompt",
    )(proj, proj, qg, kg, og)


def _attention_sample(proj, row0, qg, kg, og, ck, cv, cos, sin_signed, layer):
    n, tq = DEC_SEQ, 512
    nq = n // tq
    nk = n + PAST_LEN
    kern = functools.partial(_attn_kernel, nk_new=n, has_ctx=True)
    vec = lambda w: pl.BlockSpec((1, w), lambda b, i: (0, 0))
    q_blk0 = row0 // tq
    kv_blk0 = row0 // n
    return pl.pallas_call(
        kern,
        grid=(DEC_BATCH, nq),
        in_specs=[
            pl.BlockSpec((tq, D_ATTN), lambda b, i: (q_blk0 + b * nq + i, 0)),
            pl.BlockSpec((n, 2 * KV_DIM), lambda b, i: (kv_blk0 + b, 2)),
            vec(HEAD_DIM), vec(HEAD_DIM), vec(D_ATTN),
            pl.BlockSpec((None, None, PAST_LEN, KV_DIM), lambda b, i: (b, layer, 0, 0)),
            pl.BlockSpec((None, None, PAST_LEN, KV_DIM), lambda b, i: (b, layer, 0, 0)),
            pl.BlockSpec((tq, HEAD_DIM), lambda b, i: (i, 0)),
            pl.BlockSpec((tq, HEAD_DIM), lambda b, i: (i, 0)),
            pl.BlockSpec((n, HEAD_DIM), lambda b, i: (0, 0)),
            pl.BlockSpec((n, HEAD_DIM), lambda b, i: (0, 0)),
        ],
        out_specs=pl.BlockSpec((tq, D_ATTN), lambda b, i: (b * nq + i, 0)),
        out_shape=jax.ShapeDtypeStruct((T_SAMPLE, D_ATTN), BF16),
        scratch_shapes=[
            pltpu.VMEM((nk, KV_DIM), BF16), pltpu.VMEM((nk, KV_DIM), BF16), pltpu.VMEM((tq, D_ATTN), F32),
        ],
        compiler_params=_cparams(("parallel", "arbitrary")),
        name="attn_sample",
    )(proj, proj, qg, kg, og, ck, cv, cos, sin_signed, cos, sin_signed)


def _conv_silu(x, w, b):
    n = x.shape[0]
    row = lax.broadcasted_iota(jnp.int32, (n, 1), 0)
    prev = jnp.where(row == 0, 0.0, pltpu.roll(x, 1, 0))
    nxt = jnp.where(row == n - 1, 0.0, pltpu.roll(x, n - 1, 0))
    return _silu(prev * w[0:1, :] + x * w[1:2, :] + nxt * w[2:3, :] + b)


def _softplus(x):
    return jnp.maximum(x, 0.0) + jnp.log1p(jnp.exp(-jnp.abs(x)))


def _ssd_kernel(*refs, n, has_init, n_prev):
    if has_init:
        (za_ref, zb_ref, xa_ref, xb_ref, bc_ref, dt_ref, cw_ref, cb_ref, dtb_ref, alog_ref, dsk_ref, ng_ref,
         sf0_ref, sb0_ref, y_ref, xc_scr, bcc_scr, dts_scr, xt_scr, yt_scr, s_scr) = refs
    elif n_prev:
        (za_ref, zb_ref, xa_ref, xb_ref, bc_ref, dt_ref, cw_ref, cb_ref, dtb_ref, alog_ref, dsk_ref, ng_ref,
         psf_ref, psb_ref, y_ref, sf_ref, sb_ref, xc_scr, bcc_scr, dts_scr, xt_scr, yt_scr, s_scr) = refs
    else:
        (za_ref, zb_ref, xa_ref, xb_ref, bc_ref, dt_ref, cw_ref, cb_ref, dtb_ref, alog_ref, dsk_ref, ng_ref,
         y_ref, sf_ref, sb_ref, xc_scr, bcc_scr, dts_scr, xt_scr, yt_scr, s_scr) = refs
    nc = n // CHUNK
    gw = HEADS_PER_GROUP * SSD_HEAD_DIM

    xc_scr[:, 0:gw] = _conv_silu(xa_ref[...], cw_ref[:, 0:gw], cb_ref[:, 0:gw])
    xc_scr[:, gw:] = _conv_silu(xb_ref[...], cw_ref[:, gw:2 * gw], cb_ref[:, gw:2 * gw])
    bcc_scr[...] = _conv_silu(bc_ref[...], cw_ref[:, 2 * gw:], cb_ref[:, 2 * gw:])
    dts_scr[...] = _softplus(dt_ref[...] + dtb_ref[...])
    for g in range(SSD_GROUPS):
        hs = slice(g * HEADS_PER_GROUP, (g + 1) * HEADS_PER_GROUP)
        if has_init:
            s_scr[0, g] = sf0_ref[hs].reshape(gw, D_STATE)
            s_scr[1, g] = sb0_ref[hs].reshape(gw, D_STATE)
        else:
            s_scr[0, g] = jnp.zeros((gw, D_STATE), F32)
            s_scr[1, g] = jnp.zeros((gw, D_STATE), F32)

    def to_channel_major(c, carry):
        rows = pl.ds(pl.multiple_of(c * CHUNK, CHUNK), CHUNK)
        xt_scr[c] = xc_scr[rows, :].T
        yt_scr[c] = jnp.zeros((D_SSD, CHUNK), F32)
        return carry

    lax.fori_loop(0, nc, to_channel_major, 0)

    a_row = -jnp.exp(alog_ref[...])
    ri = lax.broadcasted_iota(jnp.int32, (CHUNK, CHUNK), 0)
    ci = lax.broadcasted_iota(jnp.int32, (CHUNK, CHUNK), 1)
    lower, upper = ci <= ri, ci >= ri
    n_dirs_heads = 2 * SSD_HEADS

    def scan_chunk(dirn, c):
        tri = (lower if dirn == 0 else upper).astype(BF16)
        valid_st = upper if dirn == 0 else lower
        tri_t = valid_st.astype(BF16)
        row0 = dirn * SSD_HEADS
        rows = pl.ds(pl.multiple_of(c * CHUNK, CHUNK), CHUNK)
        dt = dts_scr[rows, :]
        d = dt * a_row
        d1, d2, d3 = _split3(d)
        cs = _dot(tri, d1) + _dot(tri, d2) + _dot(tri, d3)
        dt_t = dt.T[0:n_dirs_heads, :]
        e1, e2, e3 = _split3(d.T[0:n_dirs_heads, :])
        cs_t = _dot(e1, tri_t) + _dot(e2, tri_t) + _dot(e3, tri_t)
        total = cs_t[:, CHUNK - 1:CHUNK] if dirn == 0 else cs_t[:, 0:1]
        e_in_t = jnp.exp(cs_t)
        to_end_t = jnp.exp(total - cs_t) * dt_t
        dec_t = jnp.broadcast_to(jnp.exp(total), (n_dirs_heads, D_STATE))
        for g in range(SSD_GROUPS):
            bm = bcc_scr[rows, g * D_STATE:(g + 1) * D_STATE].astype(BF16)
            cm = bcc_scr[rows, (SSD_GROUPS + g) * D_STATE:(SSD_GROUPS + g + 1) * D_STATE]
            g_st = _dot_nt(bm, cm.astype(BF16))
            c_nt = cm.T
            st = s_scr[dirn, g]
            xs_parts, dec_parts = [], []
            for hh in range(HEADS_PER_GROUP):
                h = g * HEADS_PER_GROUP + hh
                r = row0 + h
                ch = slice(h * SSD_HEAD_DIM, (h + 1) * SSD_HEAD_DIM)
                x_t = xt_scr[c, ch, :]
                diff = cs_t[r:r + 1, :] - cs[:, r:r + 1]
                a_st = (g_st * jnp.exp(jnp.where(valid_st, diff, -jnp.inf))).astype(BF16)
                c_e = (c_nt * e_in_t[r:r + 1, :]).astype(BF16)
                x_dt = (x_t * dt_t[r:r + 1, :]).astype(BF16)
                s_h = st[hh * SSD_HEAD_DIM:(hh + 1) * SSD_HEAD_DIM, :].astype(BF16)
                y_h = _dot(jnp.concatenate([x_dt, s_h], axis=1), jnp.concatenate([a_st, c_e], axis=0))
                yt_scr[c, ch, :] = yt_scr[c, ch, :] + y_h
                xs_parts.append((x_t * to_end_t[r:r + 1, :]).astype(BF16))
                dec_parts.append(jnp.broadcast_to(dec_t[r:r + 1, :], (SSD_HEAD_DIM, D_STATE)))
            ds = _dot(jnp.concatenate(xs_parts, axis=0), bm)
            s_scr[dirn, g] = st * jnp.concatenate(dec_parts, axis=0) + ds

    def body(i, carry):
        scan_chunk(0, i)
        scan_chunk(1, nc - 1 - i)
        return carry

    lax.fori_loop(0, nc, body, 0)

    def finish(c, carry):
        rows = pl.ds(pl.multiple_of(c * CHUNK, CHUNK), CHUNK)
        y = yt_scr[c].T + xc_scr[rows, :] * dsk_ref[...]
        ya = y[:, 0:gw] * _silu(za_ref[rows, :])
        yb = y[:, gw:] * _silu(zb_ref[rows, :])
        ms = (jnp.sum(ya * ya, axis=-1, keepdims=True) + jnp.sum(yb * yb, axis=-1, keepdims=True)) / D_SSD
        inv = lax.rsqrt(ms + EPS)
        y_ref[rows, 0:gw] = (ya * inv * ng_ref[:, 0:gw]).astype(BF16)
        y_ref[rows, gw:] = (yb * inv * ng_ref[:, gw:]).astype(BF16)
        return carry

    lax.fori_loop(0, nc, finish, 0)

    if not has_init:
        if n_prev:
            sf_ref[0:n_prev] = psf_ref[...]
            sb_ref[0:n_prev] = psb_ref[...]
        for g in range(SSD_GROUPS):
            hs = slice(g * HEADS_PER_GROUP, (g + 1) * HEADS_PER_GROUP)
            sf_ref[n_prev, hs] = s_scr[0, g].reshape(HEADS_PER_GROUP, SSD_HEAD_DIM, D_STATE)
            sb_ref[n_prev, hs] = s_scr[1, g].reshape(HEADS_PER_GROUP, SSD_HEAD_DIM, D_STATE)


def _ssd(proj, dt_raw, p, n, nb, row_blk0, layer, init, prev_states):
    has_init = init is not None
    n_prev = 0 if has_init else layer
    kern = functools.partial(_ssd_kernel, n=n, has_init=has_init, n_prev=n_prev)
    col = lambda cb: pl.BlockSpec((n, 512), lambda b: (row_blk0 + b, cb))
    vec = lambda r, w: pl.BlockSpec((r, w), lambda b: (0, 0))
    layers_spec = lambda k: pl.BlockSpec((None, k, SSD_HEADS, SSD_HEAD_DIM, D_STATE), lambda b: (b, 0, 0, 0, 0))
    in_specs = [
        col(3), col(4), col(5), col(6), col(7),
        pl.BlockSpec((n, 128), lambda b: (row_blk0 + b, 0)),
        vec(3, CONV_DIM), vec(1, CONV_DIM), vec(1, 128), vec(1, 128), vec(1, D_SSD), vec(1, D_SSD),
    ]
    args = [proj, proj, proj, proj, proj, dt_raw, p['conv_w'], p['conv_b'], p['dt_bias'], p['a_log'],
            p['d_skip'], p['ssd_norm_g']]
    y_spec = pl.BlockSpec((n, D_SSD), lambda b: (b, 0))
    y_shape = jax.ShapeDtypeStruct((nb * n, D_SSD), BF16)
    if has_init:
        init_spec = pl.BlockSpec((None, None, SSD_HEADS, SSD_HEAD_DIM, D_STATE), lambda b: (b, layer, 0, 0, 0))
        in_specs += [init_spec, init_spec]
        args += list(init)
        out_specs, out_shape = y_spec, y_shape
    else:
        if n_prev:
            in_specs += [layers_spec(n_prev), layers_spec(n_prev)]
            args += list(prev_states)
        st_shape = jax.ShapeDtypeStruct((nb, layer + 1, SSD_HEADS, SSD_HEAD_DIM, D_STATE), F32)
        out_specs = [y_spec, layers_spec(layer + 1), layers_spec(layer + 1)]
        out_shape = [y_shape, st_shape, st_shape]
    return pl.pallas_call(
        kern,
        grid=(nb,),
        in_specs=in_specs,
        out_specs=out_specs,
        out_shape=out_shape,
        scratch_shapes=[
            pltpu.VMEM((n, D_SSD), F32), pltpu.VMEM((n, 512), F32), pltpu.VMEM((n, 128), F32),
            pltpu.VMEM((n // CHUNK, D_SSD, CHUNK), F32), pltpu.VMEM((n // CHUNK, D_SSD, CHUNK), F32),
            pltpu.VMEM((2, SSD_GROUPS, 512, D_STATE), F32),
        ],
        compiler_params=_cparams(("parallel",)),
        name="ssd_sample" if has_init else "ssd_prompt",
    )(*args)


def _seg_specs(segs, tm):
    specs, bounds, start = [], [], 0
    for a in segs:
        b0, nblk = start // tm, a.shape[0] // tm
        specs.append(pl.BlockSpec((tm, a.shape[1]), lambda i, b0=b0, nblk=nblk: (jnp.clip(i - b0, 0, nblk - 1), 0)))
        bounds.append(b0)
        start += a.shape[0]
    return specs, tuple(bounds)


def _seg_pick(refs, bounds, rows=slice(None)):
    i = pl.program_id(0)
    v = refs[0][rows, :]
    for ref, b0 in zip(refs[1:], bounds[1:]):
        v = jnp.where(i >= b0, ref[rows, :], v)
    return v


def _outproj_kernel(*refs, with_router, o_bounds, y_bounds, x_bounds):
    refs = list(refs)
    o_refs = [refs.pop(0) for _ in o_bounds]
    y_refs = [refs.pop(0) for _ in y_bounds]
    x_refs = [refs.pop(0) for _ in x_bounds]
    if with_router:
        mod_ref, g_ref, w_ref, rw_ref, xo_ref, h_ref, meta_ref, cnt_ref, carry_scr = refs
    else:
        mod_ref, g_ref, w_ref, xo_ref, h_ref = refs
    tm = xo_ref.shape[0]
    n_sub = 2
    for r in range(n_sub):
        rows = slice(r * tm // n_sub, (r + 1) * tm // n_sub)
        a = jnp.concatenate([_seg_pick(o_refs, o_bounds, rows), _seg_pick(y_refs, y_bounds, rows)], axis=1)
        xn = _seg_pick(x_refs, x_bounds, rows) + mod_ref[2:3, :] * _dot(a, w_ref[...])
        xo_ref[rows, :] = xn
        h = _rms(xn, g_ref[...]) * (1.0 + mod_ref[4:5, :]) + mod_ref[3:4, :]
        h_ref[rows, :] = h.astype(h_ref.dtype)
    if with_router:
        h = h_ref[...]

        @pl.when(pl.program_id(0) == 0)
        def _():
            carry_scr[...] = jnp.zeros_like(carry_scr)

        h1, h2, _ = _split3(h)
        w1, w2, _ = _split3(rw_ref[...])
        logits = _dot_nt(w1, h1) + _dot_nt(w2, h1) + _dot_nt(w1, h2)
        row = lax.broadcasted_iota(jnp.int32, logits.shape, 0)
        logits = jnp.where(row < N_EXPERTS, logits, -jnp.inf)
        e = jnp.exp(logits - jnp.max(logits, axis=0, keepdims=True))
        probs = e / jnp.sum(e, axis=0, keepdims=True)
        p1 = jnp.max(probs, axis=0, keepdims=True)
        i1 = jnp.min(jnp.where(probs == p1, row, 16), axis=0, keepdims=True)
        rest = jnp.where(row == i1, -1.0, probs)
        p2 = jnp.max(rest, axis=0, keepdims=True)
        i2 = jnp.min(jnp.where(rest == p2, row, 16), axis=0, keepdims=True)
        hit1, hit2 = row == i1, row == i2
        onehot = jnp.where(hit1 | hit2, 1.0, 0.0)
        ti = lax.broadcasted_iota(jnp.int32, (tm, tm), 0)
        tj = lax.broadcasted_iota(jnp.int32, (tm, tm), 1)
        before = jnp.where(ti < tj, 1.0, 0.0).astype(BF16)
        rank = carry_scr[:, 0:1] + _dot(onehot.astype(BF16), before)
        r1 = jnp.sum(jnp.where(hit1, rank, 0.0), axis=0, keepdims=True)
        r2 = jnp.sum(jnp.where(hit2, rank, 0.0), axis=0, keepdims=True)
        carry_scr[...] = carry_scr[...] + jnp.sum(onehot, axis=1, keepdims=True)
        cnt_ref[...] = carry_scr[...]
        r8 = lax.broadcasted_iota(jnp.int32, (8, tm), 0)
        vals = [p1 / (p1 + p2), p2 / (p1 + p2), i1.astype(F32), i2.astype(F32), r1, r2]
        meta = jnp.zeros((8, tm), F32)
        for k, v in enumerate(vals):
            meta = jnp.where(r8 == k, v, meta)
        meta_ref[...] = meta


def _outproj(o_segs, y_segs, x_segs, mods, g, w_out, router_wt):
    tm = 512
    with_router = router_wt is not None
    o_specs, o_bounds = _seg_specs(o_segs, tm)
    y_specs, y_bounds = _seg_specs(y_segs, tm)
    x_specs, x_bounds = _seg_specs(x_segs, tm)
    kern = functools.partial(_outproj_kernel, with_router=with_router, o_bounds=o_bounds, y_bounds=y_bounds,
                             x_bounds=x_bounds)
    in_specs = o_specs + y_specs + x_specs + [
        pl.BlockSpec((None, 6, D_MODEL), lambda i: (_mod_group(i, tm), 0, 0)),
        pl.BlockSpec((1, D_MODEL), lambda i: (0, 0)),
        pl.BlockSpec((D_MODEL, D_MODEL), lambda i: (0, 0), pipeline_mode=pl.Buffered(1)),
    ]
    args = list(o_segs) + list(y_segs) + list(x_segs) + [mods, g, w_out]
    row_spec = pl.BlockSpec((tm, D_MODEL), lambda i: (i, 0))
    out_specs = [row_spec, row_spec]
    out_shape = [jax.ShapeDtypeStruct((T_ALL, D_MODEL), F32),
                 jax.ShapeDtypeStruct((T_ALL, D_MODEL), F32 if with_router else BF16)]
    scratch = []
    if with_router:
        in_specs.append(pl.BlockSpec((16, D_MODEL), lambda i: (0, 0)))
        args.append(router_wt)
        out_specs += [pl.BlockSpec((8, tm), lambda i: (0, i)), pl.BlockSpec((16, 128), lambda i: (0, 0))]
        out_shape += [jax.ShapeDtypeStruct((8, T_ALL), F32), jax.ShapeDtypeStruct((16, 128), F32)]
        scratch = [pltpu.VMEM((16, 128), F32)]
    return pl.pallas_call(
        kern,
        grid=(T_ALL // tm,),
        in_specs=in_specs,
        out_specs=out_specs,
        out_shape=out_shape,
        scratch_shapes=scratch,
        compiler_params=_cparams(("arbitrary",)),
        name="outproj_router" if with_router else "outproj",
    )(*args)


MOE_ROWS = 2 * T_ALL
MOE_TILE = 256
MOE_TILES = MOE_ROWS // MOE_TILE
MOE_VISITS = MOE_TILES + N_EXPERTS - 1


def _row_copy(src, s, dst, d, sem):
    return pltpu.make_async_copy(src.at[pl.ds(s, 1)], dst.at[pl.ds(d, 1)], sem)


def _dispatch_kernel(p1_ref, p2_ref, h_ref, xs_ref, sem):
    tm = h_ref.shape[0]

    def issue(r, c):
        _row_copy(h_ref, r, xs_ref, p1_ref[0, 0, r], sem.at[0]).start()
        _row_copy(h_ref, r, xs_ref, p2_ref[0, 0, r], sem.at[1]).start()
        return c

    lax.fori_loop(0, tm, issue, 0, unroll=8)
    pltpu.make_async_copy(h_ref, xs_ref.at[pl.ds(0, tm)], sem.at[0]).wait()
    pltpu.make_async_copy(h_ref, xs_ref.at[pl.ds(0, tm)], sem.at[1]).wait()


def _dispatch(h, pos1, pos2):
    tm = 512
    nt = T_ALL // tm
    idx = lambda: pl.BlockSpec((1, 1, tm), lambda i: (i, 0, 0), memory_space=pltpu.SMEM)
    return pl.pallas_call(
        _dispatch_kernel,
        grid=(nt,),
        in_specs=[idx(), idx(), pl.BlockSpec((tm, D_MODEL), lambda i: (i, 0))],
        out_specs=pl.BlockSpec(memory_space=pl.ANY),
        out_shape=jax.ShapeDtypeStruct((MOE_ROWS, D_MODEL), F32),
        scratch_shapes=[pltpu.SemaphoreType.DMA((2,))],
        compiler_params=_cparams(("arbitrary",)),
        name="moe_dispatch",
    )(pos1.reshape(nt, 1, tm), pos2.reshape(nt, 1, tm), h)


def _experts_kernel(vt_ref, ve_ref, nv_ref, lo_ref, hi_ref, xs_ref, wg_ref, wu_ref, wd_ref, y_ref):
    v = pl.program_id(0)

    @pl.when(v < nv_ref[0])
    def _():
        e = ve_ref[v]
        x = xs_ref[...].astype(BF16)
        hid = _silu(_dot(x, wg_ref[...])) * _dot(x, wu_ref[...])
        y = _dot(hid.astype(BF16), wd_ref[...])
        row = vt_ref[v] * MOE_TILE + lax.broadcasted_iota(jnp.int32, (MOE_TILE, 1), 0)
        mine = (row >= lo_ref[e]) & (row < hi_ref[e])
        first_visit = (v == 0) | (vt_ref[jnp.maximum(v - 1, 0)] != vt_ref[v])

        @pl.when(first_visit)
        def _():
            y_ref[...] = jnp.where(mine, y, 0.0)

        @pl.when(jnp.logical_not(first_visit))
        def _():
            y_ref[...] = jnp.where(mine, y, y_ref[...])


def _experts(xs, wg, wu, wd, vt, ve, nv, lo, hi):
    grid_spec = pltpu.PrefetchScalarGridSpec(
        num_scalar_prefetch=5,
        grid=(MOE_VISITS,),
        in_specs=[
            pl.BlockSpec((MOE_TILE, D_MODEL), lambda v, vt, ve, nv, lo, hi: (vt[v], 0)),
            pl.BlockSpec((None, D_MODEL, F_EXPERT), lambda v, vt, ve, nv, lo, hi: (ve[v], 0, 0)),
            pl.BlockSpec((None, D_MODEL, F_EXPERT), lambda v, vt, ve, nv, lo, hi: (ve[v], 0, 0)),
            pl.BlockSpec((None, F_EXPERT, D_MODEL), lambda v, vt, ve, nv, lo, hi: (ve[v], 0, 0)),
        ],
        out_specs=pl.BlockSpec((MOE_TILE, D_MODEL), lambda v, vt, ve, nv, lo, hi: (vt[v], 0)),
    )
    return pl.pallas_call(
        _experts_kernel,
        grid_spec=grid_spec,
        out_shape=jax.ShapeDtypeStruct((MOE_ROWS, D_MODEL), F32),
        compiler_params=_cparams(("arbitrary",)),
        name="moe_experts",
    )(vt, ve, nv, lo, hi, xs, wg, wu, wd)


def _combine_kernel(p1c_ref, p2c_ref, p1n_ref, p2n_ref, y_hbm, x_ref, mod_ref, gate_ref, fg_ref,
                    outp_ref, outs_ref, ya_buf, yb_buf, sem):
    i = pl.program_id(0)
    n = pl.num_programs(0)
    tm = x_ref.shape[0]
    slot = i % 2

    def gather(pa_ref, pb_ref, s):
        def issue(r, c):
            _row_copy(y_hbm, pa_ref[0, 0, r], ya_buf.at[s], r, sem.at[0, s]).start()
            _row_copy(y_hbm, pb_ref[0, 0, r], yb_buf.at[s], r, sem.at[1, s]).start()
            return c

        lax.fori_loop(0, tm, issue, 0, unroll=8)

    @pl.when(i == 0)
    def _():
        gather(p1c_ref, p2c_ref, 0)

    @pl.when(i + 1 < n)
    def _():
        gather(p1n_ref, p2n_ref, 1 - slot)

    pltpu.make_async_copy(y_hbm.at[pl.ds(0, tm)], ya_buf.at[slot], sem.at[0, slot]).wait()
    pltpu.make_async_copy(y_hbm.at[pl.ds(0, tm)], yb_buf.at[slot], sem.at[1, slot]).wait()
    g = gate_ref[...]
    mix = g[:, 0:1] * ya_buf[slot] + g[:, 1:2] * yb_buf[slot]
    xo = _rms(x_ref[...] + mod_ref[5:6, :] * mix, fg_ref[...])

    @pl.when(i < T_PROMPT // tm)
    def _():
        outp_ref[...] = xo

    @pl.when(i >= T_PROMPT // tm)
    def _():
        outs_ref[...] = xo


def _combine(y, x, mods, gate_cols, pos1, pos2, final_g):
    tm = 256
    nt = T_ALL // tm
    ntp = T_PROMPT // tm
    cur = lambda: pl.BlockSpec((1, 1, tm), lambda i: (i, 0, 0), memory_space=pltpu.SMEM)
    nxt = lambda: pl.BlockSpec((1, 1, tm), lambda i: (jnp.minimum(i + 1, nt - 1), 0, 0), memory_space=pltpu.SMEM)
    p1, p2 = pos1.reshape(nt, 1, tm), pos2.reshape(nt, 1, tm)
    return pl.pallas_call(
        _combine_kernel,
        grid=(nt,),
        in_specs=[
            cur(), cur(), nxt(), nxt(),
            pl.BlockSpec(memory_space=pl.ANY),
            pl.BlockSpec((tm, D_MODEL), lambda i: (i, 0)),
            pl.BlockSpec((None, 6, D_MODEL), lambda i: (_mod_group(i, tm), 0, 0)),
            pl.BlockSpec((tm, 128), lambda i: (i, 0)),
            pl.BlockSpec((1, D_MODEL), lambda i: (0, 0)),
        ],
        out_specs=[
            pl.BlockSpec((tm, D_MODEL), lambda i: (jnp.minimum(i, ntp - 1), 0)),
            pl.BlockSpec((tm, D_MODEL), lambda i: (jnp.maximum(i - ntp, 0), 0)),
        ],
        out_shape=[
            jax.ShapeDtypeStruct((T_PROMPT, D_MODEL), F32), jax.ShapeDtypeStruct((T_SAMPLE, D_MODEL), F32),
        ],
        scratch_shapes=[
            pltpu.VMEM((2, tm, D_MODEL), F32), pltpu.VMEM((2, tm, D_MODEL), F32),
            pltpu.SemaphoreType.DMA((2, 2)),
        ],
        compiler_params=_cparams(("arbitrary",)),
        name="moe_combine",
    )(p1, p2, p1, p2, y, x, mods, gate_cols, final_g)


def _route_plan(meta, counts):
    i1, i2 = meta[2].astype(jnp.int32), meta[3].astype(jnp.int32)
    r1, r2 = meta[4].astype(jnp.int32), meta[5].astype(jnp.int32)
    cnt = counts[:N_EXPERTS, 0].astype(jnp.int32)
    hi = jnp.cumsum(cnt)
    lo = hi - cnt
    ex = jnp.arange(N_EXPERTS, dtype=jnp.int32)
    pos1 = jnp.sum(jnp.where(i1[:, None] == ex[None, :], lo[None, :], 0), axis=1) + r1
    pos2 = jnp.sum(jnp.where(i2[:, None] == ex[None, :], lo[None, :], 0), axis=1) + r2
    first_tile = lo // MOE_TILE
    n_vis_e = jnp.where(cnt > 0, (hi - 1) // MOE_TILE - first_tile + 1, 0)
    vis_hi = jnp.cumsum(n_vis_e)
    vis_lo = vis_hi - n_vis_e
    nv = vis_hi[-1]
    v = jnp.minimum(jnp.arange(MOE_VISITS, dtype=jnp.int32), nv - 1)
    ve = jnp.minimum(jnp.sum(v[:, None] >= vis_hi[None, :], axis=1), N_EXPERTS - 1).astype(jnp.int32)
    pick = lambda tab: jnp.sum(jnp.where(ve[:, None] == ex[None, :], tab[None, :], 0), axis=1)
    vt = (pick(first_tile) + v - pick(vis_lo)).astype(jnp.int32)
    return pos1, pos2, vt, ve, nv.reshape(1).astype(jnp.int32), lo.astype(jnp.int32), hi.astype(jnp.int32)


def _ffn_kernel(*refs, n_cast):
    h_ref, x_hbm, mod_ref, wg_ref, wu_ref, wd_ref = refs[:6]
    cast_in = refs[6:6 + n_cast]
    out_ref = refs[6 + n_cast]
    cast_out = refs[7 + n_cast:7 + 2 * n_cast]
    x_buf, sem = refs[7 + 2 * n_cast:]
    for src, dst in zip(cast_in, cast_out):
        dst[...] = src[...].astype(BF16)
    i, f = pl.program_id(0), pl.program_id(1)
    tm = h_ref.shape[0]
    x_copy = pltpu.make_async_copy(x_hbm.at[pl.ds(pl.multiple_of(i * tm, tm), tm)], x_buf, sem)

    @pl.when(f == 0)
    def _():
        x_copy.start()
        out_ref[...] = jnp.zeros_like(out_ref)

    h = h_ref[...]
    hid = _silu(_dot(h, wg_ref[0].astype(BF16))) * _dot(h, wu_ref[0].astype(BF16))
    out_ref[...] += _dot(hid.astype(BF16), wd_ref[0].astype(BF16))

    @pl.when(f == pl.num_programs(1) - 1)
    def _():
        x_copy.wait()
        out_ref[...] = x_buf[...] + mod_ref[5:6, :] * out_ref[...]


FFN_CAST_STEPS = 128


def _ffn(h, x, mods, wg, wu, wd, j, to_bf16):
    tm, tf = 1024, 256
    nf = F_DENSE // tf
    assert (T_ALL // tm) * nf >= FFN_CAST_STEPS

    def cast_spec(a):
        rows = a.shape[0] // FFN_CAST_STEPS
        return pl.BlockSpec((rows, a.shape[1]), lambda i, f: (jnp.minimum(i * nf + f, FFN_CAST_STEPS - 1), 0))

    cast_specs = [cast_spec(a) for a in to_bf16]
    outs = pl.pallas_call(
        functools.partial(_ffn_kernel, n_cast=len(to_bf16)),
        grid=(T_ALL // tm, nf),
        in_specs=[
            pl.BlockSpec((tm, D_MODEL), lambda i, f: (i, 0)),
            _ANY,
            pl.BlockSpec((None, 6, D_MODEL), lambda i, f: (_mod_group(i, tm), 0, 0)),
            pl.BlockSpec((1, D_MODEL, tf), lambda i, f: (j, 0, f)),
            pl.BlockSpec((1, D_MODEL, tf), lambda i, f: (j, 0, f)),
            pl.BlockSpec((1, tf, D_MODEL), lambda i, f: (j, f, 0)),
        ] + cast_specs,
        out_specs=[pl.BlockSpec((tm, D_MODEL), lambda i, f: (i, 0))] + cast_specs,
        out_shape=[jax.ShapeDtypeStruct((T_ALL, D_MODEL), F32)]
        + [jax.ShapeDtypeStruct(a.shape, BF16) for a in to_bf16],
        scratch_shapes=[pltpu.VMEM((tm, D_MODEL), F32), pltpu.SemaphoreType.DMA(())],
        compiler_params=_cparams(("arbitrary", "arbitrary")),
        name="dense_ffn",
    )(h, x, mods, wg, wu, wd, *to_bf16)
    return outs[0], outs[1:]


def _rope_tables():
    n = DEC_SEQ
    rows = n // GRID_W
    t_row = jnp.repeat(jnp.arange(rows, dtype=F32), GRID_W)
    t_col = jnp.tile(jnp.arange(GRID_W, dtype=F32), rows)
    inv = 1.0 / (ROPE_THETA ** (jnp.arange(0, ROT_HALF, 2, dtype=F32) / ROT_HALF))
    ar, ac = t_row[:, None] * inv, t_col[:, None] * inv
    cos = jnp.concatenate([jnp.cos(ar), jnp.cos(ar), jnp.cos(ac), jnp.cos(ac)], axis=-1)
    sin_signed = jnp.concatenate([-jnp.sin(ar), jnp.sin(ar), -jnp.sin(ac), jnp.sin(ac)], axis=-1)
    return cos, sin_signed


def _pad_lanes(v, width=128):
    return jnp.pad(v, ((0, 0), (0, width - v.shape[-1])))


def kernel(x_prompt, x_sample, c, cache_k, cache_v, state_ssm_fwd, state_ssm_bwd, c_ctx, ada_w, ada_b, norm1_g, norm2_g, w_in, q_norm_g, k_norm_g, conv_w, conv_b, a_log_fwd, a_log_bwd, dt_bias_fwd, dt_bias_bwd, d_skip, ssd_norm_g, attn_out_g, w_out, ffn_w_gate, ffn_w_up, ffn_w_down, router_w, moe_w_gate, moe_w_up, moe_w_down, final_norm_g):
    assert DEPTH % 2 == 0
    cond = jnp.concatenate([c_ctx[None, :], c, jnp.zeros((N_COND - 1 - DEC_BATCH, D_MODEL), F32)], axis=0)
    mods_all = _ada_mods(cond, ada_w, ada_b).reshape(DEPTH, N_COND, 6, D_MODEL)
    cos, sin_signed = _rope_tables()
    ck = cache_k.reshape(DEC_BATCH, DEPTH, PAST_LEN, KV_DIM)
    cv = cache_v.reshape(DEC_BATCH, DEPTH, PAST_LEN, KV_DIM)

    w_in_t = jnp.swapaxes(w_in, 1, 2)
    x_segs = [x_prompt.reshape(T_PROMPT, D_MODEL), x_sample.reshape(T_SAMPLE, D_MODEL)]
    ks, vs, states = [], [], None
    for l in range(DEPTH):
        mods = mods_all[l]
        proj, dt_raw = _inproj(_modnorm(x_segs, mods, norm1_g[l][None, :]), w_in_t, l)

        qg, kg, og = q_norm_g[l][None, :], k_norm_g[l][None, :], attn_out_g[l][None, :]
        o_p, k_p, v_p = _attention_prompt(proj, 0, qg, kg, og)
        o_s = _attention_sample(proj, T_PROMPT, qg, kg, og, ck, cv, cos, sin_signed, l)

        p = {
            'conv_w': conv_w[l], 'conv_b': conv_b[l][None, :],
            'dt_bias': _pad_lanes(jnp.concatenate([dt_bias_fwd[l], dt_bias_bwd[l]])[None, :]),
            'a_log': _pad_lanes(jnp.concatenate([a_log_fwd[l], a_log_bwd[l]])[None, :]),
            'd_skip': jnp.repeat(d_skip[l], SSD_HEAD_DIM)[None, :],
            'ssd_norm_g': ssd_norm_g[l][None, :],
        }
        y_p, sf, sb = _ssd(proj, dt_raw, p, SEQ, BATCH, 0, l, None, states)
        states = (sf, sb)
        y_s = _ssd(proj, dt_raw, p, DEC_SEQ, DEC_BATCH, T_PROMPT // DEC_SEQ, l, (state_ssm_fwd, state_ssm_bwd), None)
        ks.append(k_p.reshape(BATCH, SEQ, N_KV_HEADS, HEAD_DIM))
        vs.append(v_p.reshape(BATCH, SEQ, N_KV_HEADS, HEAD_DIM))

        j = l // 2
        g2, w_o = norm2_g[l][None, :], w_out[l].astype(BF16)
        if l % 2 == 0:
            x, h = _outproj([o_p, o_s], [y_p, y_s], x_segs, mods, g2, w_o, None)
            x, (eg, eu, ed) = _ffn(h, x, mods, ffn_w_gate, ffn_w_up, ffn_w_down, j, [
                moe_w_gate[j].reshape(N_EXPERTS * D_MODEL, F_EXPERT),
                moe_w_up[j].reshape(N_EXPERTS * D_MODEL, F_EXPERT),
                moe_w_down[j].reshape(N_EXPERTS * F_EXPERT, D_MODEL)])
            expert_w = (eg.reshape(N_EXPERTS, D_MODEL, F_EXPERT), eu.reshape(N_EXPERTS, D_MODEL, F_EXPERT),
                        ed.reshape(N_EXPERTS, F_EXPERT, D_MODEL))
            x_segs = [x]
        else:
            router_wt = jnp.pad(router_w[j].T, ((0, 16 - N_EXPERTS), (0, 0)))
            x, h, meta, counts = _outproj([o_p, o_s], [y_p, y_s], x_segs, mods, g2, w_o, router_wt)
            pos1, pos2, vt, ve, nv, lo, hi = _route_plan(meta, counts)
            xs = _dispatch(h, pos1, pos2)
            ys = _experts(xs, *expert_w, vt, ve, nv, lo, hi)
            y_prompt, y_sample = _combine(ys, x, mods, _pad_lanes(meta[:2].T), pos1, pos2, final_norm_g[None, :])

    return (y_prompt.reshape(BATCH, SEQ, D_MODEL), y_sample.reshape(DEC_BATCH, DEC_SEQ, D_MODEL),
            jnp.stack(ks, axis=1), jnp.stack(vs, axis=1), states[0], states[1])
```

```python
import functools

import jax
import jax.numpy as jnp
from jax import lax
from jax.experimental import pallas as pl
from jax.experimental.pallas import tpu as pltpu

F32 = jnp.float32
BF16 = jnp.bfloat16

D_MODEL = 2048
BATCH = 16
SEQ = 256
DEPTH = 2
DEC_BATCH = 2
DEC_SEQ = 1024
PAST_LEN = 512
GRID_W = 64
D_ATTN = 1024
D_SSD = 1024
HEAD_DIM = 128
N_Q_HEADS = 8
N_KV_HEADS = 2
Q_PER_KV = 4
KV_DIM = 256
ROT_HALF = 64
ROPE_THETA = 10000.0
SSD_HEAD_DIM = 64
SSD_HEADS = 16
SSD_GROUPS = 2
HEADS_PER_GROUP = 8
D_STATE = 128
CONV_DIM = 1536
CHUNK = 128
N_MAIN = 4096
F_DENSE = 5632
N_EXPERTS = 8
F_EXPERT = 1024
EPS = 1e-6

T_PROMPT = BATCH * SEQ
T_SAMPLE = DEC_BATCH * DEC_SEQ
T_ALL = T_PROMPT + T_SAMPLE
N_COND = 16

VMEM_LIMIT = 56 * 1024 * 1024


def _cparams(sem):
    return pltpu.CompilerParams(dimension_semantics=sem, vmem_limit_bytes=VMEM_LIMIT)


def _mod_group(i, tm):
    return jnp.maximum(0, (i * tm - T_PROMPT + DEC_SEQ) // DEC_SEQ)


def _silu(x):
    return x * jax.nn.sigmoid(x)


def _rms(x, g):
    ms = jnp.mean(x * x, axis=-1, keepdims=True)
    return x * lax.rsqrt(ms + EPS) * g


def _dot(a, b):
    return jnp.dot(a, b, preferred_element_type=F32)


def _dot_nt(a, b):
    return lax.dot_general(a, b, (((1,), (1,)), ((), ())), preferred_element_type=F32)


def _split3(x):
    hi = x.astype(BF16)
    r1 = x - hi.astype(F32)
    mid = r1.astype(BF16)
    r2 = r1 - mid.astype(F32)
    return hi, mid, r2.astype(BF16)


def _ada_kernel(c_ref, w_ref, b_ref, o_ref):
    s = _silu(c_ref[...]).astype(BF16)
    o_ref[...] = _dot(s, w_ref[...].astype(BF16)) + b_ref[...]


def _ada_mods(cond, ada_w, ada_b):
    tn = 1024
    n_out = 6 * D_MODEL
    return pl.pallas_call(
        _ada_kernel,
        grid=(DEPTH, n_out // tn),
        in_specs=[
            pl.BlockSpec((N_COND, D_MODEL), lambda l, j: (0, 0)),
            pl.BlockSpec((None, D_MODEL, tn), lambda l, j: (l, 0, j)),
            pl.BlockSpec((None, 1, tn), lambda l, j: (l, 0, j)),
        ],
        out_specs=pl.BlockSpec((None, N_COND, tn), lambda l, j: (l, 0, j)),
        out_shape=jax.ShapeDtypeStruct((DEPTH, N_COND, n_out), F32),
        compiler_params=_cparams(("parallel", "parallel")),
        name="ada_mods",
    )(cond, ada_w, ada_b.reshape(DEPTH, 1, n_out))


INPROJ_NORM_ROWS = 1024


def _inproj_kernel(*refs, seg_rows):
    n_seg = len(seg_rows)
    x_hbms = refs[:n_seg]
    mods_ref, g_ref, w_ref, wdt_ref, proj_ref, dt_ref, x_buf, h_scr, sem = refs[n_seg:]
    i, j = pl.program_id(0), pl.program_id(1)
    tm = x_buf.shape[0]

    def fetch(tile):
        start = 0
        for x_hbm, nrows in zip(x_hbms, seg_rows):
            b0, nb = start // tm, nrows // tm
            start += nrows

            @pl.when((tile >= b0) & (tile < b0 + nb))
            def _():
                r0 = pl.multiple_of((tile - b0) * tm, tm)
                pltpu.make_async_copy(x_hbm.at[pl.ds(r0, tm)], x_buf, sem).start()

    @pl.when(j == 0)
    def _():
        @pl.when(i == 0)
        def _():
            fetch(i)

        pltpu.make_async_copy(x_hbms[0].at[pl.ds(0, tm)], x_buf, sem).wait()
        for k in range(tm // INPROJ_NORM_ROWS):
            rows = slice(k * INPROJ_NORM_ROWS, (k + 1) * INPROJ_NORM_ROWS)
            mod = mods_ref[_mod_group(i * (tm // INPROJ_NORM_ROWS) + k, INPROJ_NORM_ROWS)]
            h = _rms(x_buf[rows, :], g_ref[...]) * (1.0 + mod[1:2, :]) + mod[0:1, :]
            h_scr[rows, :] = h.astype(BF16)

        @pl.when(i + 1 < pl.num_programs(0))
        def _():
            fetch(i + 1)

        n_dt = wdt_ref.shape[0]
        wdt = jnp.concatenate([wdt_ref[...], jnp.zeros((128 - n_dt, D_MODEL), F32)], axis=0)
        dt_ref[...] = _dot_nt(h_scr[...], wdt.astype(BF16))

    proj_ref[...] = _dot_nt(h_scr[...], w_ref[...].astype(BF16))


_ANY = pl.BlockSpec(memory_space=pl.ANY)


def _inproj(x_segs, mods, g, w_in_t, layer):
    tm, tn = 2048, 512
    n_dt = w_in_t.shape[1] - N_MAIN
    seg_rows = tuple(a.shape[0] for a in x_segs)
    assert all(r % tm == 0 for r in seg_rows) and sum(seg_rows) == T_ALL
    return pl.pallas_call(
        functools.partial(_inproj_kernel, seg_rows=seg_rows),
        grid=(T_ALL // tm, N_MAIN // tn),
        in_specs=[_ANY] * len(x_segs) + [
            pl.BlockSpec(mods.shape, lambda i, j: (0, 0, 0)),
            pl.BlockSpec((1, D_MODEL), lambda i, j: (0, 0)),
            pl.BlockSpec((None, tn, D_MODEL), lambda i, j: (layer, j, 0)),
            pl.BlockSpec((None, n_dt, D_MODEL), lambda i, j: (layer, N_MAIN // n_dt, 0)),
        ],
        out_specs=[
            pl.BlockSpec((tm, tn), lambda i, j: (i, j)),
            pl.BlockSpec((tm, 128), lambda i, j: (i, 0)),
        ],
        out_shape=[
            jax.ShapeDtypeStruct((T_ALL, N_MAIN), F32),
            jax.ShapeDtypeStruct((T_ALL, 128), F32),
        ],
        scratch_shapes=[pltpu.VMEM((tm, D_MODEL), F32), pltpu.VMEM((tm, D_MODEL), BF16),
                        pltpu.SemaphoreType.DMA(())],
        compiler_params=_cparams(("arbitrary", "arbitrary")),
        name="inproj",
    )(*x_segs, mods, g, w_in_t, w_in_t)


def _rope(x, cos, sin_signed):
    lane = lax.broadcasted_iota(jnp.int32, x.shape, 1)
    first = (lane // (ROT_HALF // 2)) % 2 == 0
    swapped = jnp.where(first, pltpu.roll(x, HEAD_DIM - ROT_HALF // 2, 1), pltpu.roll(x, ROT_HALF // 2, 1))
    return x * cos + swapped * sin_signed


def _attn_kernel(*refs, nk_new, has_ctx):
    if has_ctx:
        (q_ref, kv_ref, qg_ref, kg_ref, og_ref, ck_ref, cv_ref, cq_ref, sq_ref, ckk_ref, skk_ref,
         o_ref, kb_scr, vb_scr, o_scr) = refs
    else:
        (q_ref, kv_ref, qg_ref, kg_ref, og_ref, o_ref, ko_ref, vo_ref, kb_scr, vb_scr, o_scr) = refs

    @pl.when(pl.program_id(1) == 0)
    def _():
        for g in range(N_KV_HEADS):
            sl = slice(g * HEAD_DIM, (g + 1) * HEAD_DIM)
            kn = _rms(kv_ref[:, sl], kg_ref[...])
            if has_ctx:
                kb_scr[0:nk_new, sl] = _rope(kn, ckk_ref[...], skk_ref[...]).astype(BF16)
                kb_scr[nk_new:, sl] = ck_ref[:, sl].astype(BF16)
            else:
                ko_ref[:, sl] = kn
                kb_scr[:, sl] = kn.astype(BF16)
        v = kv_ref[:, KV_DIM:]
        vb_scr[0:nk_new, :] = v.astype(BF16)
        if has_ctx:
            vb_scr[nk_new:, :] = cv_ref[...].astype(BF16)
        else:
            vo_ref[...] = v

    scale_log2e = HEAD_DIM ** -0.5 * 1.4426950408889634
    for h in range(N_Q_HEADS):
        g = h // Q_PER_KV
        sl = slice(h * HEAD_DIM, (h + 1) * HEAD_DIM)
        gsl = slice(g * HEAD_DIM, (g + 1) * HEAD_DIM)
        qn = _rms(q_ref[:, sl], qg_ref[...])
        if has_ctx:
            qn = _rope(qn, cq_ref[...], sq_ref[...])
        s = _dot_nt(qn.astype(BF16), kb_scr[:, gsl])
        e = jnp.exp2((s - jnp.max(s, axis=-1, keepdims=True)) * scale_log2e)
        l = jnp.sum(e, axis=-1, keepdims=True)
        o_scr[:, sl] = _dot(e.astype(BF16), vb_scr[:, gsl]) / l
    o_ref[...] = _rms(o_scr[...], og_ref[...]).astype(BF16)


def _attention_prompt(proj, row0, qg, kg, og):
    n = SEQ
    blk0 = row0 // n
    kern = functools.partial(_attn_kernel, nk_new=n, has_ctx=False)
    vec = lambda w: pl.BlockSpec((1, w), lambda b, i: (0, 0))
    return pl.pallas_call(
        kern,
        grid=(BATCH, 1),
        in_specs=[
            pl.BlockSpec((n, D_ATTN), lambda b, i: (blk0 + b, 0)),
            pl.BlockSpec((n, 2 * KV_DIM), lambda b, i: (blk0 + b, 2)),
            vec(HEAD_DIM), vec(HEAD_DIM), vec(D_ATTN),
        ],
        out_specs=[
            pl.BlockSpec((n, D_ATTN), lambda b, i: (b, 0)),
            pl.BlockSpec((None, n, KV_DIM), lambda b, i: (b, 0, 0)),
            pl.BlockSpec((None, n, KV_DIM), lambda b, i: (b, 0, 0)),
        ],
        out_shape=[
            jax.ShapeDtypeStruct((T_PROMPT, D_ATTN), BF16),
            jax.ShapeDtypeStruct((BATCH, n, KV_DIM), F32),
            jax.ShapeDtypeStruct((BATCH, n, KV_DIM), F32),
        ],
        scratch_shapes=[
            pltpu.VMEM((n, KV_DIM), BF16), pltpu.VMEM((n, KV_DIM), BF16), pltpu.VMEM((n, D_ATTN), F32),
        ],
        compiler_params=_cparams(("parallel", "arbitrary")),
        name="attn_prompt",
    )(proj, proj, qg, kg, og)


def _attention_sample(proj, row0, qg, kg, og, ck, cv, cos, sin_signed, layer):
    n, tq = DEC_SEQ, 512
    nq = n // tq
    nk = n + PAST_LEN
    kern = functools.partial(_attn_kernel, nk_new=n, has_ctx=True)
    vec = lambda w: pl.BlockSpec((1, w), lambda b, i: (0, 0))
    q_blk0 = row0 // tq
    kv_blk0 = row0 // n
    return pl.pallas_call(
        kern,
        grid=(DEC_BATCH, nq),
        in_specs=[
            pl.BlockSpec((tq, D_ATTN), lambda b, i: (q_blk0 + b * nq + i, 0)),
            pl.BlockSpec((n, 2 * KV_DIM), lambda b, i: (kv_blk0 + b, 2)),
            vec(HEAD_DIM), vec(HEAD_DIM), vec(D_ATTN),
            pl.BlockSpec((None, None, PAST_LEN, KV_DIM), lambda b, i: (b, layer, 0, 0)),
            pl.BlockSpec((None, None, PAST_LEN, KV_DIM), lambda b, i: (b, layer, 0, 0)),
            pl.BlockSpec((tq, HEAD_DIM), lambda b, i: (i, 0)),
            pl.BlockSpec((tq, HEAD_DIM), lambda b, i: (i, 0)),
            pl.BlockSpec((n, HEAD_DIM), lambda b, i: (0, 0)),
            pl.BlockSpec((n, HEAD_DIM), lambda b, i: (0, 0)),
        ],
        out_specs=pl.BlockSpec((tq, D_ATTN), lambda b, i: (b * nq + i, 0)),
        out_shape=jax.ShapeDtypeStruct((T_SAMPLE, D_ATTN), BF16),
        scratch_shapes=[
            pltpu.VMEM((nk, KV_DIM), BF16), pltpu.VMEM((nk, KV_DIM), BF16), pltpu.VMEM((tq, D_ATTN), F32),
        ],
        compiler_params=_cparams(("parallel", "arbitrary")),
        name="attn_sample",
    )(proj, proj, qg, kg, og, ck, cv, cos, sin_signed, cos, sin_signed)


def _conv_silu(x, w, b):
    n = x.shape[0]
    row = lax.broadcasted_iota(jnp.int32, (n, 1), 0)
    prev = jnp.where(row == 0, 0.0, pltpu.roll(x, 1, 0))
    nxt = jnp.where(row == n - 1, 0.0, pltpu.roll(x, n - 1, 0))
    return _silu(prev * w[0:1, :] + x * w[1:2, :] + nxt * w[2:3, :] + b)


def _softplus(x):
    return jnp.maximum(x, 0.0) + jnp.log1p(jnp.exp(-jnp.abs(x)))


def _ssd_kernel(*refs, n, has_init, n_prev):
    if has_init:
        (za_ref, zb_ref, xa_ref, xb_ref, bc_ref, dt_ref, cw_ref, cb_ref, dtb_ref, alog_ref, dsk_ref, ng_ref,
         sf0_ref, sb0_ref, y_ref, xc_scr, bcc_scr, dts_scr, xt_scr, yt_scr, s_scr) = refs
    elif n_prev:
        (za_ref, zb_ref, xa_ref, xb_ref, bc_ref, dt_ref, cw_ref, cb_ref, dtb_ref, alog_ref, dsk_ref, ng_ref,
         psf_ref, psb_ref, y_ref, sf_ref, sb_ref, xc_scr, bcc_scr, dts_scr, xt_scr, yt_scr, s_scr) = refs
    else:
        (za_ref, zb_ref, xa_ref, xb_ref, bc_ref, dt_ref, cw_ref, cb_ref, dtb_ref, alog_ref, dsk_ref, ng_ref,
         y_ref, sf_ref, sb_ref, xc_scr, bcc_scr, dts_scr, xt_scr, yt_scr, s_scr) = refs
    nc = n // CHUNK
    gw = HEADS_PER_GROUP * SSD_HEAD_DIM

    xc_scr[:, 0:gw] = _conv_silu(xa_ref[...], cw_ref[:, 0:gw], cb_ref[:, 0:gw])
    xc_scr[:, gw:] = _conv_silu(xb_ref[...], cw_ref[:, gw:2 * gw], cb_ref[:, gw:2 * gw])
    bcc_scr[...] = _conv_silu(bc_ref[...], cw_ref[:, 2 * gw:], cb_ref[:, 2 * gw:])
    dts_scr[...] = _softplus(dt_ref[...] + dtb_ref[...])
    for g in range(SSD_GROUPS):
        hs = slice(g * HEADS_PER_GROUP, (g + 1) * HEADS_PER_GROUP)
        if has_init:
            s_scr[0, g] = sf0_ref[hs].reshape(gw, D_STATE)
            s_scr[1, g] = sb0_ref[hs].reshape(gw, D_STATE)
        else:
            s_scr[0, g] = jnp.zeros((gw, D_STATE), F32)
            s_scr[1, g] = jnp.zeros((gw, D_STATE), F32)

    def to_channel_major(c, carry):
        rows = pl.ds(pl.multiple_of(c * CHUNK, CHUNK), CHUNK)
        xt_scr[c] = xc_scr[rows, :].T
        yt_scr[c] = jnp.zeros((D_SSD, CHUNK), F32)
        return carry

    lax.fori_loop(0, nc, to_channel_major, 0)

    a_row = -jnp.exp(alog_ref[...])
    ri = lax.broadcasted_iota(jnp.int32, (CHUNK, CHUNK), 0)
    ci = lax.broadcasted_iota(jnp.int32, (CHUNK, CHUNK), 1)
    lower, upper = ci <= ri, ci >= ri
    n_dirs_heads = 2 * SSD_HEADS

    def scan_chunk(dirn, c):
        tri = (lower if dirn == 0 else upper).astype(BF16)
        valid_st = upper if dirn == 0 else lower
        tri_t = valid_st.astype(BF16)
        row0 = dirn * SSD_HEADS
        rows = pl.ds(pl.multiple_of(c * CHUNK, CHUNK), CHUNK)
        dt = dts_scr[rows, :]
        d = dt * a_row
        d1, d2, d3 = _split3(d)
        cs = _dot(tri, d1) + _dot(tri, d2) + _dot(tri, d3)
        dt_t = dt.T[0:n_dirs_heads, :]
        e1, e2, e3 = _split3(d.T[0:n_dirs_heads, :])
        cs_t = _dot(e1, tri_t) + _dot(e2, tri_t) + _dot(e3, tri_t)
        total = cs_t[:, CHUNK - 1:CHUNK] if dirn == 0 else cs_t[:, 0:1]
        e_in_t = jnp.exp(cs_t)
        to_end_t = jnp.exp(total - cs_t) * dt_t
        dec_t = jnp.broadcast_to(jnp.exp(total), (n_dirs_heads, D_STATE))
        for g in range(SSD_GROUPS):
            bm = bcc_scr[rows, g * D_STATE:(g + 1) * D_STATE].astype(BF16)
            cm = bcc_scr[rows, (SSD_GROUPS + g) * D_STATE:(SSD_GROUPS + g + 1) * D_STATE]
            g_st = _dot_nt(bm, cm.astype(BF16))
            c_nt = cm.T
            st = s_scr[dirn, g]
            xs_parts, dec_parts = [], []
            for hh in range(HEADS_PER_GROUP):
                h = g * HEADS_PER_GROUP + hh
                r = row0 + h
                ch = slice(h * SSD_HEAD_DIM, (h + 1) * SSD_HEAD_DIM)
                x_t = xt_scr[c, ch, :]
                diff = cs_t[r:r + 1, :] - cs[:, r:r + 1]
                a_st = (g_st * jnp.exp(jnp.where(valid_st, diff, -jnp.inf))).astype(BF16)
                c_e = (c_nt * e_in_t[r:r + 1, :]).astype(BF16)
                x_dt = (x_t * dt_t[r:r + 1, :]).astype(BF16)
                s_h = st[hh * SSD_HEAD_DIM:(hh + 1) * SSD_HEAD_DIM, :].astype(BF16)
                y_h = _dot(jnp.concatenate([x_dt, s_h], axis=1), jnp.concatenate([a_st, c_e], axis=0))
                yt_scr[c, ch, :] = yt_scr[c, ch, :] + y_h
                xs_parts.append((x_t * to_end_t[r:r + 1, :]).astype(BF16))
                dec_parts.append(jnp.broadcast_to(dec_t[r:r + 1, :], (SSD_HEAD_DIM, D_STATE)))
            ds = _dot(jnp.concatenate(xs_parts, axis=0), bm)
            s_scr[dirn, g] = st * jnp.concatenate(dec_parts, axis=0) + ds

    def body(i, carry):
        scan_chunk(0, i)
        scan_chunk(1, nc - 1 - i)
        return carry

    lax.fori_loop(0, nc, body, 0)

    def finish(c, carry):
        rows = pl.ds(pl.multiple_of(c * CHUNK, CHUNK), CHUNK)
        y = yt_scr[c].T + xc_scr[rows, :] * dsk_ref[...]
        ya = y[:, 0:gw] * _silu(za_ref[rows, :])
        yb = y[:, gw:] * _silu(zb_ref[rows, :])
        ms = (jnp.sum(ya * ya, axis=-1, keepdims=True) + jnp.sum(yb * yb, axis=-1, keepdims=True)) / D_SSD
        inv = lax.rsqrt(ms + EPS)
        y_ref[rows, 0:gw] = (ya * inv * ng_ref[:, 0:gw]).astype(BF16)
        y_ref[rows, gw:] = (yb * inv * ng_ref[:, gw:]).astype(BF16)
        return carry

    lax.fori_loop(0, nc, finish, 0)

    if not has_init:
        if n_prev:
            sf_ref[0:n_prev] = psf_ref[...]
            sb_ref[0:n_prev] = psb_ref[...]
        for g in range(SSD_GROUPS):
            hs = slice(g * HEADS_PER_GROUP, (g + 1) * HEADS_PER_GROUP)
            sf_ref[n_prev, hs] = s_scr[0, g].reshape(HEADS_PER_GROUP, SSD_HEAD_DIM, D_STATE)
            sb_ref[n_prev, hs] = s_scr[1, g].reshape(HEADS_PER_GROUP, SSD_HEAD_DIM, D_STATE)


def _ssd(proj, dt_raw, p, n, nb, row_blk0, layer, init, prev_states):
    has_init = init is not None
    n_prev = 0 if has_init else layer
    kern = functools.partial(_ssd_kernel, n=n, has_init=has_init, n_prev=n_prev)
    col = lambda cb: pl.BlockSpec((n, 512), lambda b: (row_blk0 + b, cb))
    vec = lambda r, w: pl.BlockSpec((r, w), lambda b: (0, 0))
    layers_spec = lambda k: pl.BlockSpec((None, k, SSD_HEADS, SSD_HEAD_DIM, D_STATE), lambda b: (b, 0, 0, 0, 0))
    in_specs = [
        col(3), col(4), col(5), col(6), col(7),
        pl.BlockSpec((n, 128), lambda b: (row_blk0 + b, 0)),
        vec(3, CONV_DIM), vec(1, CONV_DIM), vec(1, 128), vec(1, 128), vec(1, D_SSD), vec(1, D_SSD),
    ]
    args = [proj, proj, proj, proj, proj, dt_raw, p['conv_w'], p['conv_b'], p['dt_bias'], p['a_log'],
            p['d_skip'], p['ssd_norm_g']]
    y_spec = pl.BlockSpec((n, D_SSD), lambda b: (b, 0))
    y_shape = jax.ShapeDtypeStruct((nb * n, D_SSD), BF16)
    if has_init:
        init_spec = pl.BlockSpec((None, None, SSD_HEADS, SSD_HEAD_DIM, D_STATE), lambda b: (b, layer, 0, 0, 0))
        in_specs += [init_spec, init_spec]
        args += list(init)
        out_specs, out_shape = y_spec, y_shape
    else:
        if n_prev:
            in_specs += [layers_spec(n_prev), layers_spec(n_prev)]
            args += list(prev_states)
        st_shape = jax.ShapeDtypeStruct((nb, layer + 1, SSD_HEADS, SSD_HEAD_DIM, D_STATE), F32)
        out_specs = [y_spec, layers_spec(layer + 1), layers_spec(layer + 1)]
        out_shape = [y_shape, st_shape, st_shape]
    return pl.pallas_call(
        kern,
        grid=(nb,),
        in_specs=in_specs,
        out_specs=out_specs,
        out_shape=out_shape,
        scratch_shapes=[
            pltpu.VMEM((n, D_SSD), F32), pltpu.VMEM((n, 512), F32), pltpu.VMEM((n, 128), F32),
            pltpu.VMEM((n // CHUNK, D_SSD, CHUNK), F32), pltpu.VMEM((n // CHUNK, D_SSD, CHUNK), F32),
            pltpu.VMEM((2, SSD_GROUPS, 512, D_STATE), F32),
        ],
        compiler_params=_cparams(("parallel",)),
        name="ssd_sample" if has_init else "ssd_prompt",
    )(*args)


def _seg_specs(segs, tm):
    specs, bounds, start = [], [], 0
    for a in segs:
        b0, nblk = start // tm, a.shape[0] // tm
        specs.append(pl.BlockSpec((tm, a.shape[1]), lambda i, b0=b0, nblk=nblk: (jnp.clip(i - b0, 0, nblk - 1), 0)))
        bounds.append(b0)
        start += a.shape[0]
    return specs, tuple(bounds)


def _seg_pick(refs, bounds):
    i = pl.program_id(0)
    v = refs[0][...]
    for ref, b0 in zip(refs[1:], bounds[1:]):
        v = jnp.where(i >= b0, ref[...], v)
    return v


def _outproj_kernel(*refs, with_router, o_bounds, y_bounds, x_bounds):
    refs = list(refs)
    o_refs = [refs.pop(0) for _ in o_bounds]
    y_refs = [refs.pop(0) for _ in y_bounds]
    x_refs = [refs.pop(0) for _ in x_bounds]
    if with_router:
        mod_ref, g_ref, w_ref, rw_ref, xo_ref, h_ref, meta_ref, cnt_ref, carry_scr = refs
    else:
        mod_ref, g_ref, w_ref, xo_ref, h_ref = refs
    tm = xo_ref.shape[0]
    a = jnp.concatenate([_seg_pick(o_refs, o_bounds), _seg_pick(y_refs, y_bounds)], axis=1)
    xn = _seg_pick(x_refs, x_bounds) + mod_ref[2:3, :] * _dot(a, w_ref[...])
    xo_ref[...] = xn
    h = _rms(xn, g_ref[...]) * (1.0 + mod_ref[4:5, :]) + mod_ref[3:4, :]
    h_ref[...] = h.astype(h_ref.dtype)
    if with_router:

        @pl.when(pl.program_id(0) == 0)
        def _():
            carry_scr[...] = jnp.zeros_like(carry_scr)

        h1, h2, _ = _split3(h)
        w1, w2, _ = _split3(rw_ref[...])
        logits = _dot_nt(w1, h1) + _dot_nt(w2, h1) + _dot_nt(w1, h2)
        row = lax.broadcasted_iota(jnp.int32, logits.shape, 0)
        logits = jnp.where(row < N_EXPERTS, logits, -jnp.inf)
        e = jnp.exp(logits - jnp.max(logits, axis=0, keepdims=True))
        probs = e / jnp.sum(e, axis=0, keepdims=True)
        p1 = jnp.max(probs, axis=0, keepdims=True)
        i1 = jnp.min(jnp.where(probs == p1, row, 16), axis=0, keepdims=True)
        rest = jnp.where(row == i1, -1.0, probs)
        p2 = jnp.max(rest, axis=0, keepdims=True)
        i2 = jnp.min(jnp.where(rest == p2, row, 16), axis=0, keepdims=True)
        hit1, hit2 = row == i1, row == i2
        onehot = jnp.where(hit1 | hit2, 1.0, 0.0)
        ti = lax.broadcasted_iota(jnp.int32, (tm, tm), 0)
        tj = lax.broadcasted_iota(jnp.int32, (tm, tm), 1)
        before = jnp.where(ti < tj, 1.0, 0.0).astype(BF16)
        rank = carry_scr[:, 0:1] + _dot(onehot.astype(BF16), before)
        r1 = jnp.sum(jnp.where(hit1, rank, 0.0), axis=0, keepdims=True)
        r2 = jnp.sum(jnp.where(hit2, rank, 0.0), axis=0, keepdims=True)
        carry_scr[...] = carry_scr[...] + jnp.sum(onehot, axis=1, keepdims=True)
        cnt_ref[...] = carry_scr[...]
        r8 = lax.broadcasted_iota(jnp.int32, (8, tm), 0)
        vals = [p1 / (p1 + p2), p2 / (p1 + p2), i1.astype(F32), i2.astype(F32), r1, r2]
        meta = jnp.zeros((8, tm), F32)
        for k, v in enumerate(vals):
            meta = jnp.where(r8 == k, v, meta)
        meta_ref[...] = meta


def _outproj(o_segs, y_segs, x_segs, mods, g, w_out, router_wt):
    tm = 512
    with_router = router_wt is not None
    o_specs, o_bounds = _seg_specs(o_segs, tm)
    y_specs, y_bounds = _seg_specs(y_segs, tm)
    x_specs, x_bounds = _seg_specs(x_segs, tm)
    kern = functools.partial(_outproj_kernel, with_router=with_router, o_bounds=o_bounds, y_bounds=y_bounds,
                             x_bounds=x_bounds)
    in_specs = o_specs + y_specs + x_specs + [
        pl.BlockSpec((None, 6, D_MODEL), lambda i: (_mod_group(i, tm), 0, 0)),
        pl.BlockSpec((1, D_MODEL), lambda i: (0, 0)),
        pl.BlockSpec((D_MODEL, D_MODEL), lambda i: (0, 0), pipeline_mode=pl.Buffered(1)),
    ]
    args = list(o_segs) + list(y_segs) + list(x_segs) + [mods, g, w_out]
    row_spec = pl.BlockSpec((tm, D_MODEL), lambda i: (i, 0))
    out_specs = [row_spec, row_spec]
    out_shape = [jax.ShapeDtypeStruct((T_ALL, D_MODEL), F32),
                 jax.ShapeDtypeStruct((T_ALL, D_MODEL), F32 if with_router else BF16)]
    scratch = []
    if with_router:
        in_specs.append(pl.BlockSpec((16, D_MODEL), lambda i: (0, 0)))
        args.append(router_wt)
        out_specs += [pl.BlockSpec((8, tm), lambda i: (0, i)), pl.BlockSpec((16, 128), lambda i: (0, 0))]
        out_shape += [jax.ShapeDtypeStruct((8, T_ALL), F32), jax.ShapeDtypeStruct((16, 128), F32)]
        scratch = [pltpu.VMEM((16, 128), F32)]
    return pl.pallas_call(
        kern,
        grid=(T_ALL // tm,),
        in_specs=in_specs,
        out_specs=out_specs,
        out_shape=out_shape,
        scratch_shapes=scratch,
        compiler_params=_cparams(("arbitrary",)),
        name="outproj_router" if with_router else "outproj",
    )(*args)


MOE_ROWS = 2 * T_ALL
MOE_TILE = 256
MOE_TILES = MOE_ROWS // MOE_TILE
MOE_VISITS = MOE_TILES + N_EXPERTS - 1


def _row_copy(src, s, dst, d, sem):
    return pltpu.make_async_copy(src.at[pl.ds(s, 1)], dst.at[pl.ds(d, 1)], sem)


def _dispatch_kernel(p1_ref, p2_ref, h_ref, xs_ref, sem):
    tm = h_ref.shape[0]

    def issue(r, c):
        _row_copy(h_ref, r, xs_ref, p1_ref[0, 0, r], sem.at[0]).start()
        _row_copy(h_ref, r, xs_ref, p2_ref[0, 0, r], sem.at[1]).start()
        return c

    lax.fori_loop(0, tm, issue, 0, unroll=8)
    pltpu.make_async_copy(h_ref, xs_ref.at[pl.ds(0, tm)], sem.at[0]).wait()
    pltpu.make_async_copy(h_ref, xs_ref.at[pl.ds(0, tm)], sem.at[1]).wait()


def _dispatch(h, pos1, pos2):
    tm = 512
    nt = T_ALL // tm
    idx = lambda: pl.BlockSpec((1, 1, tm), lambda i: (i, 0, 0), memory_space=pltpu.SMEM)
    return pl.pallas_call(
        _dispatch_kernel,
        grid=(nt,),
        in_specs=[idx(), idx(), pl.BlockSpec((tm, D_MODEL), lambda i: (i, 0))],
        out_specs=pl.BlockSpec(memory_space=pl.ANY),
        out_shape=jax.ShapeDtypeStruct((MOE_ROWS, D_MODEL), F32),
        scratch_shapes=[pltpu.SemaphoreType.DMA((2,))],
        compiler_params=_cparams(("arbitrary",)),
        name="moe_dispatch",
    )(pos1.reshape(nt, 1, tm), pos2.reshape(nt, 1, tm), h)


def _experts_kernel(vt_ref, ve_ref, nv_ref, lo_ref, hi_ref, xs_ref, wg_ref, wu_ref, wd_ref, y_ref):
    v = pl.program_id(0)

    @pl.when(v < nv_ref[0])
    def _():
        e = ve_ref[v]
        x = xs_ref[...].astype(BF16)
        hid = _silu(_dot(x, wg_ref[...])) * _dot(x, wu_ref[...])
        y = _dot(hid.astype(BF16), wd_ref[...])
        row = vt_ref[v] * MOE_TILE + lax.broadcasted_iota(jnp.int32, (MOE_TILE, 1), 0)
        mine = (row >= lo_ref[e]) & (row < hi_ref[e])
        first_visit = (v == 0) | (vt_ref[jnp.maximum(v - 1, 0)] != vt_ref[v])

        @pl.when(first_visit)
        def _():
            y_ref[...] = jnp.where(mine, y, 0.0)

        @pl.when(jnp.logical_not(first_visit))
        def _():
            y_ref[...] = jnp.where(mine, y, y_ref[...])


def _experts(xs, wg, wu, wd, vt, ve, nv, lo, hi):
    grid_spec = pltpu.PrefetchScalarGridSpec(
        num_scalar_prefetch=5,
        grid=(MOE_VISITS,),
        in_specs=[
            pl.BlockSpec((MOE_TILE, D_MODEL), lambda v, vt, ve, nv, lo, hi: (vt[v], 0)),
            pl.BlockSpec((None, D_MODEL, F_EXPERT), lambda v, vt, ve, nv, lo, hi: (ve[v], 0, 0)),
            pl.BlockSpec((None, D_MODEL, F_EXPERT), lambda v, vt, ve, nv, lo, hi: (ve[v], 0, 0)),
            pl.BlockSpec((None, F_EXPERT, D_MODEL), lambda v, vt, ve, nv, lo, hi: (ve[v], 0, 0)),
        ],
        out_specs=pl.BlockSpec((MOE_TILE, D_MODEL), lambda v, vt, ve, nv, lo, hi: (vt[v], 0)),
    )
    return pl.pallas_call(
        _experts_kernel,
        grid_spec=grid_spec,
        out_shape=jax.ShapeDtypeStruct((MOE_ROWS, D_MODEL), F32),
        compiler_params=_cparams(("arbitrary",)),
        name="moe_experts",
    )(vt, ve, nv, lo, hi, xs, wg, wu, wd)


def _combine_kernel(p1c_ref, p2c_ref, p1n_ref, p2n_ref, y_hbm, x_ref, mod_ref, gate_ref, fg_ref,
                    outp_ref, outs_ref, ya_buf, yb_buf, sem):
    i = pl.program_id(0)
    n = pl.num_programs(0)
    tm = x_ref.shape[0]
    slot = i % 2

    def gather(pa_ref, pb_ref, s):
        def issue(r, c):
            _row_copy(y_hbm, pa_ref[0, 0, r], ya_buf.at[s], r, sem.at[0, s]).start()
            _row_copy(y_hbm, pb_ref[0, 0, r], yb_buf.at[s], r, sem.at[1, s]).start()
            return c

        lax.fori_loop(0, tm, issue, 0, unroll=8)

    @pl.when(i == 0)
    def _():
        gather(p1c_ref, p2c_ref, 0)

    @pl.when(i + 1 < n)
    def _():
        gather(p1n_ref, p2n_ref, 1 - slot)

    pltpu.make_async_copy(y_hbm.at[pl.ds(0, tm)], ya_buf.at[slot], sem.at[0, slot]).wait()
    pltpu.make_async_copy(y_hbm.at[pl.ds(0, tm)], yb_buf.at[slot], sem.at[1, slot]).wait()
    g = gate_ref[...]
    mix = g[:, 0:1] * ya_buf[slot] + g[:, 1:2] * yb_buf[slot]
    xo = _rms(x_ref[...] + mod_ref[5:6, :] * mix, fg_ref[...])

    @pl.when(i < T_PROMPT // tm)
    def _():
        outp_ref[...] = xo

    @pl.when(i >= T_PROMPT // tm)
    def _():
        outs_ref[...] = xo


def _combine(y, x, mods, gate_cols, pos1, pos2, final_g):
    tm = 256
    nt = T_ALL // tm
    ntp = T_PROMPT // tm
    cur = lambda: pl.BlockSpec((1, 1, tm), lambda i: (i, 0, 0), memory_space=pltpu.SMEM)
    nxt = lambda: pl.BlockSpec((1, 1, tm), lambda i: (jnp.minimum(i + 1, nt - 1), 0, 0), memory_space=pltpu.SMEM)
    p1, p2 = pos1.reshape(nt, 1, tm), pos2.reshape(nt, 1, tm)
    return pl.pallas_call(
        _combine_kernel,
        grid=(nt,),
        in_specs=[
            cur(), cur(), nxt(), nxt(),
            pl.BlockSpec(memory_space=pl.ANY),
            pl.BlockSpec((tm, D_MODEL), lambda i: (i, 0)),
            pl.BlockSpec((None, 6, D_MODEL), lambda i: (_mod_group(i, tm), 0, 0)),
            pl.BlockSpec((tm, 128), lambda i: (i, 0)),
            pl.BlockSpec((1, D_MODEL), lambda i: (0, 0)),
        ],
        out_specs=[
            pl.BlockSpec((tm, D_MODEL), lambda i: (jnp.minimum(i, ntp - 1), 0)),
            pl.BlockSpec((tm, D_MODEL), lambda i: (jnp.maximum(i - ntp, 0), 0)),
        ],
        out_shape=[
            jax.ShapeDtypeStruct((T_PROMPT, D_MODEL), F32), jax.ShapeDtypeStruct((T_SAMPLE, D_MODEL), F32),
        ],
        scratch_shapes=[
            pltpu.VMEM((2, tm, D_MODEL), F32), pltpu.VMEM((2, tm, D_MODEL), F32),
            pltpu.SemaphoreType.DMA((2, 2)),
        ],
        compiler_params=_cparams(("arbitrary",)),
        name="moe_combine",
    )(p1, p2, p1, p2, y, x, mods, gate_cols, final_g)


def _route_plan(meta, counts):
    i1, i2 = meta[2].astype(jnp.int32), meta[3].astype(jnp.int32)
    r1, r2 = meta[4].astype(jnp.int32), meta[5].astype(jnp.int32)
    cnt = counts[:N_EXPERTS, 0].astype(jnp.int32)
    hi = jnp.cumsum(cnt)
    lo = hi - cnt
    ex = jnp.arange(N_EXPERTS, dtype=jnp.int32)
    pos1 = jnp.sum(jnp.where(i1[:, None] == ex[None, :], lo[None, :], 0), axis=1) + r1
    pos2 = jnp.sum(jnp.where(i2[:, None] == ex[None, :], lo[None, :], 0), axis=1) + r2
    first_tile = lo // MOE_TILE
    n_vis_e = jnp.where(cnt > 0, (hi - 1) // MOE_TILE - first_tile + 1, 0)
    vis_hi = jnp.cumsum(n_vis_e)
    vis_lo = vis_hi - n_vis_e
    nv = vis_hi[-1]
    v = jnp.minimum(jnp.arange(MOE_VISITS, dtype=jnp.int32), nv - 1)
    ve = jnp.minimum(jnp.sum(v[:, None] >= vis_hi[None, :], axis=1), N_EXPERTS - 1).astype(jnp.int32)
    pick = lambda tab: jnp.sum(jnp.where(ve[:, None] == ex[None, :], tab[None, :], 0), axis=1)
    vt = (pick(first_tile) + v - pick(vis_lo)).astype(jnp.int32)
    return pos1, pos2, vt, ve, nv.reshape(1).astype(jnp.int32), lo.astype(jnp.int32), hi.astype(jnp.int32)


def _ffn_kernel(*refs, n_cast):
    h_ref, x_hbm, mod_ref, wg_ref, wu_ref, wd_ref = refs[:6]
    cast_in = refs[6:6 + n_cast]
    out_ref = refs[6 + n_cast]
    cast_out = refs[7 + n_cast:7 + 2 * n_cast]
    x_buf, sem = refs[7 + 2 * n_cast:]
    for src, dst in zip(cast_in, cast_out):
        dst[...] = src[...].astype(BF16)
    i, f = pl.program_id(0), pl.program_id(1)
    tm = h_ref.shape[0]
    x_copy = pltpu.make_async_copy(x_hbm.at[pl.ds(pl.multiple_of(i * tm, tm), tm)], x_buf, sem)

    @pl.when(f == 0)
    def _():
        x_copy.start()
        out_ref[...] = jnp.zeros_like(out_ref)

    h = h_ref[...]
    hid = _silu(_dot(h, wg_ref[0].astype(BF16))) * _dot(h, wu_ref[0].astype(BF16))
    out_ref[...] += _dot(hid.astype(BF16), wd_ref[0].astype(BF16))

    @pl.when(f == pl.num_programs(1) - 1)
    def _():
        x_copy.wait()
        out_ref[...] = x_buf[...] + mod_ref[5:6, :] * out_ref[...]


FFN_CAST_STEPS = 128


def _ffn(h, x, mods, wg, wu, wd, j, to_bf16):
    tm, tf = 1024, 256
    nf = F_DENSE // tf
    assert (T_ALL // tm) * nf >= FFN_CAST_STEPS

    def cast_spec(a):
        rows = a.shape[0] // FFN_CAST_STEPS
        return pl.BlockSpec((rows, a.shape[1]), lambda i, f: (jnp.minimum(i * nf + f, FFN_CAST_STEPS - 1), 0))

    cast_specs = [cast_spec(a) for a in to_bf16]
    outs = pl.pallas_call(
        functools.partial(_ffn_kernel, n_cast=len(to_bf16)),
        grid=(T_ALL // tm, nf),
        in_specs=[
            pl.BlockSpec((tm, D_MODEL), lambda i, f: (i, 0)),
            _ANY,
            pl.BlockSpec((None, 6, D_MODEL), lambda i, f: (_mod_group(i, tm), 0, 0)),
            pl.BlockSpec((1, D_MODEL, tf), lambda i, f: (j, 0, f)),
            pl.BlockSpec((1, D_MODEL, tf), lambda i, f: (j, 0, f)),
            pl.BlockSpec((1, tf, D_MODEL), lambda i, f: (j, f, 0)),
        ] + cast_specs,
        out_specs=[pl.BlockSpec((tm, D_MODEL), lambda i, f: (i, 0))] + cast_specs,
        out_shape=[jax.ShapeDtypeStruct((T_ALL, D_MODEL), F32)]
        + [jax.ShapeDtypeStruct(a.shape, BF16) for a in to_bf16],
        scratch_shapes=[pltpu.VMEM((tm, D_MODEL), F32), pltpu.SemaphoreType.DMA(())],
        compiler_params=_cparams(("arbitrary", "arbitrary")),
        name="dense_ffn",
    )(h, x, mods, wg, wu, wd, *to_bf16)
    return outs[0], outs[1:]


def _rope_tables():
    n = DEC_SEQ
    rows = n // GRID_W
    t_row = jnp.repeat(jnp.arange(rows, dtype=F32), GRID_W)
    t_col = jnp.tile(jnp.arange(GRID_W, dtype=F32), rows)
    inv = 1.0 / (ROPE_THETA ** (jnp.arange(0, ROT_HALF, 2, dtype=F32) / ROT_HALF))
    ar, ac = t_row[:, None] * inv, t_col[:, None] * inv
    cos = jnp.concatenate([jnp.cos(ar), jnp.cos(ar), jnp.cos(ac), jnp.cos(ac)], axis=-1)
    sin_signed = jnp.concatenate([-jnp.sin(ar), jnp.sin(ar), -jnp.sin(ac), jnp.sin(ac)], axis=-1)
    return cos, sin_signed


def _pad_lanes(v, width=128):
    return jnp.pad(v, ((0, 0), (0, width - v.shape[-1])))


def kernel(x_prompt, x_sample, c, cache_k, cache_v, state_ssm_fwd, state_ssm_bwd, c_ctx, ada_w, ada_b, norm1_g, norm2_g, w_in, q_norm_g, k_norm_g, conv_w, conv_b, a_log_fwd, a_log_bwd, dt_bias_fwd, dt_bias_bwd, d_skip, ssd_norm_g, attn_out_g, w_out, ffn_w_gate, ffn_w_up, ffn_w_down, router_w, moe_w_gate, moe_w_up, moe_w_down, final_norm_g):
    assert DEPTH % 2 == 0
    cond = jnp.concatenate([c_ctx[None, :], c, jnp.zeros((N_COND - 1 - DEC_BATCH, D_MODEL), F32)], axis=0)
    mods_all = _ada_mods(cond, ada_w, ada_b).reshape(DEPTH, N_COND, 6, D_MODEL)
    cos, sin_signed = _rope_tables()
    ck = cache_k.reshape(DEC_BATCH, DEPTH, PAST_LEN, KV_DIM)
    cv = cache_v.reshape(DEC_BATCH, DEPTH, PAST_LEN, KV_DIM)

    w_in_t = jnp.swapaxes(w_in, 1, 2)
    x_segs = [x_prompt.reshape(T_PROMPT, D_MODEL), x_sample.reshape(T_SAMPLE, D_MODEL)]
    ks, vs, states = [], [], None
    for l in range(DEPTH):
        mods = mods_all[l]
        proj, dt_raw = _inproj(x_segs, mods, norm1_g[l][None, :], w_in_t, l)

        qg, kg, og = q_norm_g[l][None, :], k_norm_g[l][None, :], attn_out_g[l][None, :]
        o_p, k_p, v_p = _attention_prompt(proj, 0, qg, kg, og)
        o_s = _attention_sample(proj, T_PROMPT, qg, kg, og, ck, cv, cos, sin_signed, l)

        p = {
            'conv_w': conv_w[l], 'conv_b': conv_b[l][None, :],
            'dt_bias': _pad_lanes(jnp.concatenate([dt_bias_fwd[l], dt_bias_bwd[l]])[None, :]),
            'a_log': _pad_lanes(jnp.concatenate([a_log_fwd[l], a_log_bwd[l]])[None, :]),
            'd_skip': jnp.repeat(d_skip[l], SSD_HEAD_DIM)[None, :],
            'ssd_norm_g': ssd_norm_g[l][None, :],
        }
        y_p, sf, sb = _ssd(proj, dt_raw, p, SEQ, BATCH, 0, l, None, states)
        states = (sf, sb)
        y_s = _ssd(proj, dt_raw, p, DEC_SEQ, DEC_BATCH, T_PROMPT // DEC_SEQ, l, (state_ssm_fwd, state_ssm_bwd), None)
        ks.append(k_p.reshape(BATCH, SEQ, N_KV_HEADS, HEAD_DIM))
        vs.append(v_p.reshape(BATCH, SEQ, N_KV_HEADS, HEAD_DIM))

        j = l // 2
        g2, w_o = norm2_g[l][None, :], w_out[l].astype(BF16)
        if l % 2 == 0:
            x, h = _outproj([o_p, o_s], [y_p, y_s], x_segs, mods, g2, w_o, None)
            x, (eg, eu, ed) = _ffn(h, x, mods, ffn_w_gate, ffn_w_up, ffn_w_down, j, [
                moe_w_gate[j].reshape(N_EXPERTS * D_MODEL, F_EXPERT),
                moe_w_up[j].reshape(N_EXPERTS * D_MODEL, F_EXPERT),
                moe_w_down[j].reshape(N_EXPERTS * F_EXPERT, D_MODEL)])
            expert_w = (eg.reshape(N_EXPERTS, D_MODEL, F_EXPERT), eu.reshape(N_EXPERTS, D_MODEL, F_EXPERT),
                        ed.reshape(N_EXPERTS, F_EXPERT, D_MODEL))
            x_segs = [x]
        else:
            router_wt = jnp.pad(router_w[j].T, ((0, 16 - N_EXPERTS), (0, 0)))
            x, h, meta, counts = _outproj([o_p, o_s], [y_p, y_s], x_segs, mods, g2, w_o, router_wt)
            pos1, pos2, vt, ve, nv, lo, hi = _route_plan(meta, counts)
            xs = _dispatch(h, pos1, pos2)
            ys = _experts(xs, *expert_w, vt, ve, nv, lo, hi)
            y_prompt, y_sample = _combine(ys, x, mods, _pad_lanes(meta[:2].T), pos1, pos2, final_norm_g[None, :])

    return (y_prompt.reshape(BATCH, SEQ, D_MODEL), y_sample.reshape(DEC_BATCH, DEC_SEQ, D_MODEL),
            jnp.stack(ks, axis=1), jnp.stack(vs, axis=1), states[0], states[1])
```

```python
import functools

import jax
import jax.numpy as jnp
from jax import lax
from jax.experimental import pallas as pl
from jax.experimental.pallas import tpu as pltpu

F32 = jnp.float32
BF16 = jnp.bfloat16

D_MODEL = 2048
BATCH = 16
SEQ = 256
DEPTH = 2
DEC_BATCH = 2
DEC_SEQ = 1024
PAST_LEN = 512
GRID_W = 64
D_ATTN = 1024
D_SSD = 1024
HEAD_DIM = 128
N_Q_HEADS = 8
N_KV_HEADS = 2
Q_PER_KV = 4
KV_DIM = 256
ROT_HALF = 64
ROPE_THETA = 10000.0
SSD_HEAD_DIM = 64
SSD_HEADS = 16
SSD_GROUPS = 2
HEADS_PER_GROUP = 8
D_STATE = 128
CONV_DIM = 1536
CHUNK = 128
N_MAIN = 4096
F_DENSE = 5632
N_EXPERTS = 8
F_EXPERT = 1024
EPS = 1e-6

T_PROMPT = BATCH * SEQ
T_SAMPLE = DEC_BATCH * DEC_SEQ
T_ALL = T_PROMPT + T_SAMPLE
N_COND = 16

VMEM_LIMIT = 56 * 1024 * 1024


def _cparams(sem):
    return pltpu.CompilerParams(dimension_semantics=sem, vmem_limit_bytes=VMEM_LIMIT)


def _mod_group(i, tm):
    return jnp.maximum(0, (i * tm - T_PROMPT + DEC_SEQ) // DEC_SEQ)


def _silu(x):
    return x * jax.nn.sigmoid(x)


def _rms(x, g):
    ms = jnp.mean(x * x, axis=-1, keepdims=True)
    return x * lax.rsqrt(ms + EPS) * g


def _dot(a, b):
    return jnp.dot(a, b, preferred_element_type=F32)


def _dot_nt(a, b):
    return lax.dot_general(a, b, (((1,), (1,)), ((), ())), preferred_element_type=F32)


def _split3(x):
    hi = x.astype(BF16)
    r1 = x - hi.astype(F32)
    mid = r1.astype(BF16)
    r2 = r1 - mid.astype(F32)
    return hi, mid, r2.astype(BF16)


def _ada_kernel(c_ref, w_ref, b_ref, o_ref):
    s = _silu(c_ref[...]).astype(BF16)
    o_ref[...] = _dot(s, w_ref[...].astype(BF16)) + b_ref[...]


def _ada_mods(cond, ada_w, ada_b):
    tn = 1024
    n_out = 6 * D_MODEL
    return pl.pallas_call(
        _ada_kernel,
        grid=(DEPTH, n_out // tn),
        in_specs=[
            pl.BlockSpec((N_COND, D_MODEL), lambda l, j: (0, 0)),
            pl.BlockSpec((None, D_MODEL, tn), lambda l, j: (l, 0, j)),
            pl.BlockSpec((None, 1, tn), lambda l, j: (l, 0, j)),
        ],
        out_specs=pl.BlockSpec((None, N_COND, tn), lambda l, j: (l, 0, j)),
        out_shape=jax.ShapeDtypeStruct((DEPTH, N_COND, n_out), F32),
        compiler_params=_cparams(("parallel", "parallel")),
        name="ada_mods",
    )(cond, ada_w, ada_b.reshape(DEPTH, 1, n_out))


def _side_cast_specs(parts, n_steps, inner):
    in_specs, out_specs, out_shapes = [], [], []
    for a, row0, nrows in parts:
        rows = nrows // n_steps
        blk0 = row0 // rows
        step = lambda i, j: jnp.minimum(i * inner + j, n_steps - 1)
        in_specs.append(pl.BlockSpec((rows, a.shape[1]), lambda i, j, blk0=blk0: (blk0 + step(i, j), 0)))
        out_specs.append(pl.BlockSpec((rows, a.shape[1]), lambda i, j: (step(i, j), 0)))
        out_shapes.append(jax.ShapeDtypeStruct((nrows, a.shape[1]), BF16))
    return in_specs, out_specs, out_shapes


def _side_cast(cast_in, cast_out):
    for src, dst in zip(cast_in, cast_out):
        dst[...] = src[...].astype(BF16)


INPROJ_NORM_ROWS = 1024
INPROJ_CAST_STEPS = 16


def _inproj_kernel(*refs, seg_rows, n_cast):
    n_seg = len(seg_rows)
    x_hbms = refs[:n_seg]
    mods_ref, g_ref, w_ref, wdt_ref = refs[n_seg:n_seg + 4]
    cast_in = refs[n_seg + 4:n_seg + 4 + n_cast]
    proj_ref, dt_ref = refs[n_seg + 4 + n_cast:n_seg + 6 + n_cast]
    cast_out = refs[n_seg + 6 + n_cast:n_seg + 6 + 2 * n_cast]
    x_buf, h_scr, sem = refs[n_seg + 6 + 2 * n_cast:]
    _side_cast(cast_in, cast_out)
    i, j = pl.program_id(0), pl.program_id(1)
    tm = x_buf.shape[0]

    def fetch(tile):
        start = 0
        for x_hbm, nrows in zip(x_hbms, seg_rows):
            b0, nb = start // tm, nrows // tm
            start += nrows

            @pl.when((tile >= b0) & (tile < b0 + nb))
            def _():
                r0 = pl.multiple_of((tile - b0) * tm, tm)
                pltpu.make_async_copy(x_hbm.at[pl.ds(r0, tm)], x_buf, sem).start()

    @pl.when(j == 0)
    def _():
        @pl.when(i == 0)
        def _():
            fetch(i)

        pltpu.make_async_copy(x_hbms[0].at[pl.ds(0, tm)], x_buf, sem).wait()
        for k in range(tm // INPROJ_NORM_ROWS):
            rows = slice(k * INPROJ_NORM_ROWS, (k + 1) * INPROJ_NORM_ROWS)
            mod = mods_ref[_mod_group(i * (tm // INPROJ_NORM_ROWS) + k, INPROJ_NORM_ROWS)]
            h = _rms(x_buf[rows, :], g_ref[...]) * (1.0 + mod[1:2, :]) + mod[0:1, :]
            h_scr[rows, :] = h.astype(BF16)

        @pl.when(i + 1 < pl.num_programs(0))
        def _():
            fetch(i + 1)

        n_dt = wdt_ref.shape[0]
        wdt = jnp.concatenate([wdt_ref[...], jnp.zeros((128 - n_dt, D_MODEL), F32)], axis=0)
        dt_ref[...] = _dot_nt(h_scr[...], wdt.astype(BF16))

    proj_ref[...] = _dot_nt(h_scr[...], w_ref[...].astype(BF16))


_ANY = pl.BlockSpec(memory_space=pl.ANY)


def _inproj(x_segs, mods, g, w_in_t, layer, to_bf16):
    tm, tn = 2048, 512
    n_dt = w_in_t.shape[1] - N_MAIN
    nj = N_MAIN // tn
    seg_rows = tuple(a.shape[0] for a in x_segs)
    assert all(r % tm == 0 for r in seg_rows) and sum(seg_rows) == T_ALL
    assert (T_ALL // tm) * nj >= INPROJ_CAST_STEPS
    cast_in, cast_out, cast_shapes = _side_cast_specs(to_bf16, INPROJ_CAST_STEPS, nj)
    outs = pl.pallas_call(
        functools.partial(_inproj_kernel, seg_rows=seg_rows, n_cast=len(to_bf16)),
        grid=(T_ALL // tm, nj),
        in_specs=[_ANY] * len(x_segs) + [
            pl.BlockSpec(mods.shape, lambda i, j: (0, 0, 0)),
            pl.BlockSpec((1, D_MODEL), lambda i, j: (0, 0)),
            pl.BlockSpec((None, tn, D_MODEL), lambda i, j: (layer, j, 0)),
            pl.BlockSpec((None, n_dt, D_MODEL), lambda i, j: (layer, N_MAIN // n_dt, 0)),
        ] + cast_in,
        out_specs=[
            pl.BlockSpec((tm, tn), lambda i, j: (i, j)),
            pl.BlockSpec((tm, 128), lambda i, j: (i, 0)),
        ] + cast_out,
        out_shape=[
            jax.ShapeDtypeStruct((T_ALL, N_MAIN), F32),
            jax.ShapeDtypeStruct((T_ALL, 128), F32),
        ] + cast_shapes,
        scratch_shapes=[pltpu.VMEM((tm, D_MODEL), F32), pltpu.VMEM((tm, D_MODEL), BF16),
                        pltpu.SemaphoreType.DMA(())],
        compiler_params=_cparams(("arbitrary", "arbitrary")),
        name="inproj",
    )(*x_segs, mods, g, w_in_t, w_in_t, *[part[0] for part in to_bf16])
    return outs[0], outs[1], outs[2:]


def _rope(x, cos, sin_signed):
    lane = lax.broadcasted_iota(jnp.int32, x.shape, 1)
    first = (lane // (ROT_HALF // 2)) % 2 == 0
    swapped = jnp.where(first, pltpu.roll(x, HEAD_DIM - ROT_HALF // 2, 1), pltpu.roll(x, ROT_HALF // 2, 1))
    return x * cos + swapped * sin_signed


def _attn_kernel(*refs, nk_new, has_ctx, n_prev):
    if has_ctx:
        (q_ref, kv_ref, qg_ref, kg_ref, og_ref, ck_ref, cv_ref, cq_ref, sq_ref, ckk_ref, skk_ref,
         o_ref, kb_scr, vb_scr, o_scr) = refs
    elif n_prev:
        (q_ref, kv_ref, qg_ref, kg_ref, og_ref, pk_ref, pv_ref, o_ref, ko_ref, vo_ref,
         kb_scr, vb_scr, o_scr) = refs
    else:
        (q_ref, kv_ref, qg_ref, kg_ref, og_ref, o_ref, ko_ref, vo_ref, kb_scr, vb_scr, o_scr) = refs

    @pl.when(pl.program_id(1) == 0)
    def _():
        if not has_ctx and n_prev:
            ko_ref[0:n_prev] = pk_ref[...]
            vo_ref[0:n_prev] = pv_ref[...]
        for g in range(N_KV_HEADS):
            sl = slice(g * HEAD_DIM, (g + 1) * HEAD_DIM)
            kn = _rms(kv_ref[:, sl], kg_ref[...])
            v = kv_ref[:, KV_DIM + g * HEAD_DIM:KV_DIM + (g + 1) * HEAD_DIM]
            vb_scr[0:nk_new, sl] = v.astype(BF16)
            if has_ctx:
                kb_scr[0:nk_new, sl] = _rope(kn, ckk_ref[...], skk_ref[...]).astype(BF16)
                kb_scr[nk_new:, sl] = ck_ref[:, g, :].astype(BF16)
                vb_scr[nk_new:, sl] = cv_ref[:, g, :].astype(BF16)
            else:
                kb_scr[:, sl] = kn.astype(BF16)
                ko_ref[n_prev, :, g, :] = kn
                vo_ref[n_prev, :, g, :] = v

    scale_log2e = HEAD_DIM ** -0.5 * 1.4426950408889634
    for h in range(N_Q_HEADS):
        g = h // Q_PER_KV
        sl = slice(h * HEAD_DIM, (h + 1) * HEAD_DIM)
        gsl = slice(g * HEAD_DIM, (g + 1) * HEAD_DIM)
        qn = _rms(q_ref[:, sl], qg_ref[...])
        if has_ctx:
            qn = _rope(qn, cq_ref[...], sq_ref[...])
        s = _dot_nt(qn.astype(BF16), kb_scr[:, gsl])
        e = jnp.exp2((s - jnp.max(s, axis=-1, keepdims=True)) * scale_log2e)
        l = jnp.sum(e, axis=-1, keepdims=True)
        o_scr[:, sl] = _dot(e.astype(BF16), vb_scr[:, gsl]) / l
    o_ref[...] = _rms(o_scr[...], og_ref[...]).astype(BF16)


def _attention_prompt(proj, row0, qg, kg, og, layer, prev_kv):
    n = SEQ
    blk0 = row0 // n
    kern = functools.partial(_attn_kernel, nk_new=n, has_ctx=False, n_prev=layer)
    vec = lambda w: pl.BlockSpec((1, w), lambda b, i: (0, 0))
    cache_spec = lambda k: pl.BlockSpec((None, k, n, N_KV_HEADS, HEAD_DIM), lambda b, i: (b, 0, 0, 0, 0))
    cache_shape = jax.ShapeDtypeStruct((BATCH, layer + 1, n, N_KV_HEADS, HEAD_DIM), F32)
    in_specs = [
        pl.BlockSpec((n, D_ATTN), lambda b, i: (blk0 + b, 0)),
        pl.BlockSpec((n, 2 * KV_DIM), lambda b, i: (blk0 + b, 2)),
        vec(HEAD_DIM), vec(HEAD_DIM), vec(D_ATTN),
    ]
    args = [proj, proj, qg, kg, og]
    if layer:
        in_specs += [cache_spec(layer), cache_spec(layer)]
        args += list(prev_kv)
    return pl.pallas_call(
        kern,
        grid=(BATCH, 1),
        in_specs=in_specs,
        out_specs=[pl.BlockSpec((n, D_ATTN), lambda b, i: (b, 0)), cache_spec(layer + 1), cache_spec(layer + 1)],
        out_shape=[jax.ShapeDtypeStruct((T_PROMPT, D_ATTN), BF16), cache_shape, cache_shape],
        scratch_shapes=[
            pltpu.VMEM((n, KV_DIM), BF16), pltpu.VMEM((n, KV_DIM), BF16), pltpu.VMEM((n, D_ATTN), F32),
        ],
        compiler_params=_cparams(("parallel", "arbitrary")),
        name="attn_prompt",
    )(*args)


def _attention_sample(proj, row0, qg, kg, og, ck, cv, cos, sin_signed, layer):
    n, tq = DEC_SEQ, 512
    nq = n // tq
    nk = n + PAST_LEN
    kern = functools.partial(_attn_kernel, nk_new=n, has_ctx=True, n_prev=0)
    vec = lambda w: pl.BlockSpec((1, w), lambda b, i: (0, 0))
    q_blk0 = row0 // tq
    kv_blk0 = row0 // n
    return pl.pallas_call(
        kern,
        grid=(DEC_BATCH, nq),
        in_specs=[
            pl.BlockSpec((tq, D_ATTN), lambda b, i: (q_blk0 + b * nq + i, 0)),
            pl.BlockSpec((n, 2 * KV_DIM), lambda b, i: (kv_blk0 + b, 2)),
            vec(HEAD_DIM), vec(HEAD_DIM), vec(D_ATTN),
            pl.BlockSpec((None, None, PAST_LEN, N_KV_HEADS, HEAD_DIM), lambda b, i: (b, layer, 0, 0, 0)),
            pl.BlockSpec((None, None, PAST_LEN, N_KV_HEADS, HEAD_DIM), lambda b, i: (b, layer, 0, 0, 0)),
            pl.BlockSpec((tq, HEAD_DIM), lambda b, i: (i, 0)),
            pl.BlockSpec((tq, HEAD_DIM), lambda b, i: (i, 0)),
            pl.BlockSpec((n, HEAD_DIM), lambda b, i: (0, 0)),
            pl.BlockSpec((n, HEAD_DIM), lambda b, i: (0, 0)),
        ],
        out_specs=pl.BlockSpec((tq, D_ATTN), lambda b, i: (b * nq + i, 0)),
        out_shape=jax.ShapeDtypeStruct((T_SAMPLE, D_ATTN), BF16),
        scratch_shapes=[
            pltpu.VMEM((nk, KV_DIM), BF16), pltpu.VMEM((nk, KV_DIM), BF16), pltpu.VMEM((tq, D_ATTN), F32),
        ],
        compiler_params=_cparams(("parallel", "arbitrary")),
        name="attn_sample",
    )(proj, proj, qg, kg, og, ck, cv, cos, sin_signed, cos, sin_signed)


def _conv_silu(x, w, b):
    n = x.shape[0]
    row = lax.broadcasted_iota(jnp.int32, (n, 1), 0)
    prev = jnp.where(row == 0, 0.0, pltpu.roll(x, 1, 0))
    nxt = jnp.where(row == n - 1, 0.0, pltpu.roll(x, n - 1, 0))
    return _silu(prev * w[0:1, :] + x * w[1:2, :] + nxt * w[2:3, :] + b)


def _softplus(x):
    return jnp.maximum(x, 0.0) + jnp.log1p(jnp.exp(-jnp.abs(x)))


def _ssd_kernel(*refs, n, has_init, n_prev):
    if has_init:
        (za_ref, zb_ref, xa_ref, xb_ref, bc_ref, dt_ref, cw_ref, cb_ref, dtb_ref, alog_ref, dsk_ref, ng_ref,
         sf0_ref, sb0_ref, y_ref, xc_scr, bcc_scr, dts_scr, xt_scr, yt_scr, s_scr) = refs
    elif n_prev:
        (za_ref, zb_ref, xa_ref, xb_ref, bc_ref, dt_ref, cw_ref, cb_ref, dtb_ref, alog_ref, dsk_ref, ng_ref,
         psf_ref, psb_ref, y_ref, sf_ref, sb_ref, xc_scr, bcc_scr, dts_scr, xt_scr, yt_scr, s_scr) = refs
    else:
        (za_ref, zb_ref, xa_ref, xb_ref, bc_ref, dt_ref, cw_ref, cb_ref, dtb_ref, alog_ref, dsk_ref, ng_ref,
         y_ref, sf_ref, sb_ref, xc_scr, bcc_scr, dts_scr, xt_scr, yt_scr, s_scr) = refs
    nc = n // CHUNK
    gw = HEADS_PER_GROUP * SSD_HEAD_DIM

    xc_scr[:, 0:gw] = _conv_silu(xa_ref[...], cw_ref[:, 0:gw], cb_ref[:, 0:gw])
    xc_scr[:, gw:] = _conv_silu(xb_ref[...], cw_ref[:, gw:2 * gw], cb_ref[:, gw:2 * gw])
    bcc_scr[...] = _conv_silu(bc_ref[...], cw_ref[:, 2 * gw:], cb_ref[:, 2 * gw:])
    dts_scr[...] = _softplus(dt_ref[...] + dtb_ref[...])
    for g in range(SSD_GROUPS):
        hs = slice(g * HEADS_PER_GROUP, (g + 1) * HEADS_PER_GROUP)
        if has_init:
            s_scr[0, g] = sf0_ref[hs].reshape(gw, D_STATE)
            s_scr[1, g] = sb0_ref[hs].reshape(gw, D_STATE)
        else:
            s_scr[0, g] = jnp.zeros((gw, D_STATE), F32)
            s_scr[1, g] = jnp.zeros((gw, D_STATE), F32)

    def to_channel_major(c, carry):
        rows = pl.ds(pl.multiple_of(c * CHUNK, CHUNK), CHUNK)
        xt_scr[c] = xc_scr[rows, :].T
        yt_scr[c] = jnp.zeros((D_SSD, CHUNK), F32)
        return carry

    lax.fori_loop(0, nc, to_channel_major, 0)

    a_row = -jnp.exp(alog_ref[...])
    ri = lax.broadcasted_iota(jnp.int32, (CHUNK, CHUNK), 0)
    ci = lax.broadcasted_iota(jnp.int32, (CHUNK, CHUNK), 1)
    lower, upper = ci <= ri, ci >= ri
    n_dirs_heads = 2 * SSD_HEADS

    def scan_chunk(dirn, c):
        tri = (lower if dirn == 0 else upper).astype(BF16)
        valid_st = upper if dirn == 0 else lower
        tri_t = valid_st.astype(BF16)
        row0 = dirn * SSD_HEADS
        rows = pl.ds(pl.multiple_of(c * CHUNK, CHUNK), CHUNK)
        dt = dts_scr[rows, :]
        d = dt * a_row
        d1, d2, d3 = _split3(d)
        cs = _dot(tri, d1) + _dot(tri, d2) + _dot(tri, d3)
        dt_t = dt.T[0:n_dirs_heads, :]
        e1, e2, e3 = _split3(d.T[0:n_dirs_heads, :])
        cs_t = _dot(e1, tri_t) + _dot(e2, tri_t) + _dot(e3, tri_t)
        total = cs_t[:, CHUNK - 1:CHUNK] if dirn == 0 else cs_t[:, 0:1]
        e_in_t = jnp.exp(cs_t)
        to_end_t = jnp.exp(total - cs_t) * dt_t
        dec_t = jnp.broadcast_to(jnp.exp(total), (n_dirs_heads, D_STATE))
        for g in range(SSD_GROUPS):
            bm = bcc_scr[rows, g * D_STATE:(g + 1) * D_STATE].astype(BF16)
            cm = bcc_scr[rows, (SSD_GROUPS + g) * D_STATE:(SSD_GROUPS + g + 1) * D_STATE]
            g_st = _dot_nt(bm, cm.astype(BF16))
            c_nt = cm.T
            st = s_scr[dirn, g]
            xs_parts, dec_parts = [], []
            for hh in range(HEADS_PER_GROUP):
                h = g * HEADS_PER_GROUP + hh
                r = row0 + h
                ch = slice(h * SSD_HEAD_DIM, (h + 1) * SSD_HEAD_DIM)
                x_t = xt_scr[c, ch, :]
                diff = cs_t[r:r + 1, :] - cs[:, r:r + 1]
                a_st = (g_st * jnp.exp(jnp.where(valid_st, diff, -jnp.inf))).astype(BF16)
                c_e = (c_nt * e_in_t[r:r + 1, :]).astype(BF16)
                x_dt = (x_t * dt_t[r:r + 1, :]).astype(BF16)
                s_h = st[hh * SSD_HEAD_DIM:(hh + 1) * SSD_HEAD_DIM, :].astype(BF16)
                y_h = _dot(jnp.concatenate([x_dt, s_h], axis=1), jnp.concatenate([a_st, c_e], axis=0))
                yt_scr[c, ch, :] = yt_scr[c, ch, :] + y_h
                xs_parts.append((x_t * to_end_t[r:r + 1, :]).astype(BF16))
                dec_parts.append(jnp.broadcast_to(dec_t[r:r + 1, :], (SSD_HEAD_DIM, D_STATE)))
            ds = _dot(jnp.concatenate(xs_parts, axis=0), bm)
            s_scr[dirn, g] = st * jnp.concatenate(dec_parts, axis=0) + ds

    def body(i, carry):
        scan_chunk(0, i)
        scan_chunk(1, nc - 1 - i)
        return carry

    lax.fori_loop(0, nc, body, 0)

    def finish(c, carry):
        rows = pl.ds(pl.multiple_of(c * CHUNK, CHUNK), CHUNK)
        y = yt_scr[c].T + xc_scr[rows, :] * dsk_ref[...]
        ya = y[:, 0:gw] * _silu(za_ref[rows, :])
        yb = y[:, gw:] * _silu(zb_ref[rows, :])
        ms = (jnp.sum(ya * ya, axis=-1, keepdims=True) + jnp.sum(yb * yb, axis=-1, keepdims=True)) / D_SSD
        inv = lax.rsqrt(ms + EPS)
        y_ref[rows, 0:gw] = (ya * inv * ng_ref[:, 0:gw]).astype(BF16)
        y_ref[rows, gw:] = (yb * inv * ng_ref[:, gw:]).astype(BF16)
        return carry

    lax.fori_loop(0, nc, finish, 0)

    if not has_init:
        if n_prev:
            sf_ref[0:n_prev] = psf_ref[...]
            sb_ref[0:n_prev] = psb_ref[...]
        for g in range(SSD_GROUPS):
            hs = slice(g * HEADS_PER_GROUP, (g + 1) * HEADS_PER_GROUP)
            sf_ref[n_prev, hs] = s_scr[0, g].reshape(HEADS_PER_GROUP, SSD_HEAD_DIM, D_STATE)
            sb_ref[n_prev, hs] = s_scr[1, g].reshape(HEADS_PER_GROUP, SSD_HEAD_DIM, D_STATE)


def _ssd(proj, dt_raw, p, n, nb, row_blk0, layer, init, prev_states):
    has_init = init is not None
    n_prev = 0 if has_init else layer
    kern = functools.partial(_ssd_kernel, n=n, has_init=has_init, n_prev=n_prev)
    col = lambda cb: pl.BlockSpec((n, 512), lambda b: (row_blk0 + b, cb))
    vec = lambda r, w: pl.BlockSpec((r, w), lambda b: (0, 0))
    layers_spec = lambda k: pl.BlockSpec((None, k, SSD_HEADS, SSD_HEAD_DIM, D_STATE), lambda b: (b, 0, 0, 0, 0))
    in_specs = [
        col(3), col(4), col(5), col(6), col(7),
        pl.BlockSpec((n, 128), lambda b: (row_blk0 + b, 0)),
        vec(3, CONV_DIM), vec(1, CONV_DIM), vec(1, 128), vec(1, 128), vec(1, D_SSD), vec(1, D_SSD),
    ]
    args = [proj, proj, proj, proj, proj, dt_raw, p['conv_w'], p['conv_b'], p['dt_bias'], p['a_log'],
            p['d_skip'], p['ssd_norm_g']]
    y_spec = pl.BlockSpec((n, D_SSD), lambda b: (b, 0))
    y_shape = jax.ShapeDtypeStruct((nb * n, D_SSD), BF16)
    if has_init:
        init_spec = pl.BlockSpec((None, None, SSD_HEADS, SSD_HEAD_DIM, D_STATE), lambda b: (b, layer, 0, 0, 0))
        in_specs += [init_spec, init_spec]
        args += list(init)
        out_specs, out_shape = y_spec, y_shape
    else:
        if n_prev:
            in_specs += [layers_spec(n_prev), layers_spec(n_prev)]
            args += list(prev_states)
        st_shape = jax.ShapeDtypeStruct((nb, layer + 1, SSD_HEADS, SSD_HEAD_DIM, D_STATE), F32)
        out_specs = [y_spec, layers_spec(layer + 1), layers_spec(layer + 1)]
        out_shape = [y_shape, st_shape, st_shape]
    return pl.pallas_call(
        kern,
        grid=(nb,),
        in_specs=in_specs,
        out_specs=out_specs,
        out_shape=out_shape,
        scratch_shapes=[
            pltpu.VMEM((n, D_SSD), F32), pltpu.VMEM((n, 512), F32), pltpu.VMEM((n, 128), F32),
            pltpu.VMEM((n // CHUNK, D_SSD, CHUNK), F32), pltpu.VMEM((n // CHUNK, D_SSD, CHUNK), F32),
            pltpu.VMEM((2, SSD_GROUPS, 512, D_STATE), F32),
        ],
        compiler_params=_cparams(("parallel",)),
        name="ssd_sample" if has_init else "ssd_prompt",
    )(*args)


def _seg_specs(segs, tm):
    specs, bounds, start = [], [], 0
    for a in segs:
        b0, nblk = start // tm, a.shape[0] // tm
        specs.append(pl.BlockSpec((tm, a.shape[1]), lambda i, b0=b0, nblk=nblk: (jnp.clip(i - b0, 0, nblk - 1), 0)))
        bounds.append(b0)
        start += a.shape[0]
    return specs, tuple(bounds)


def _seg_pick(refs, bounds):
    i = pl.program_id(0)
    v = refs[0][...]
    for ref, b0 in zip(refs[1:], bounds[1:]):
        v = jnp.where(i >= b0, ref[...], v)
    return v


def _outproj_kernel(*refs, with_router, o_bounds, y_bounds, x_bounds):
    refs = list(refs)
    o_refs = [refs.pop(0) for _ in o_bounds]
    y_refs = [refs.pop(0) for _ in y_bounds]
    x_refs = [refs.pop(0) for _ in x_bounds]
    if with_router:
        mod_ref, g_ref, w_ref, rw_ref, xo_ref, h_ref, meta_ref, cnt_ref, carry_scr = refs
    else:
        mod_ref, g_ref, w_ref, xo_ref, h_ref = refs
    tm = xo_ref.shape[0]
    a = jnp.concatenate([_seg_pick(o_refs, o_bounds), _seg_pick(y_refs, y_bounds)], axis=1)
    xn = _seg_pick(x_refs, x_bounds) + mod_ref[2:3, :] * _dot(a, w_ref[...])
    xo_ref[...] = xn
    h = _rms(xn, g_ref[...]) * (1.0 + mod_ref[4:5, :]) + mod_ref[3:4, :]
    h_ref[...] = h.astype(h_ref.dtype)
    if with_router:

        @pl.when(pl.program_id(0) == 0)
        def _():
            carry_scr[...] = jnp.zeros_like(carry_scr)

        h1, h2, _ = _split3(h)
        w1, w2, _ = _split3(rw_ref[...])
        logits = _dot_nt(w1, h1) + _dot_nt(w2, h1) + _dot_nt(w1, h2)
        row = lax.broadcasted_iota(jnp.int32, logits.shape, 0)
        logits = jnp.where(row < N_EXPERTS, logits, -jnp.inf)
        e = jnp.exp(logits - jnp.max(logits, axis=0, keepdims=True))
        probs = e / jnp.sum(e, axis=0, keepdims=True)
        p1 = jnp.max(probs, axis=0, keepdims=True)
        i1 = jnp.min(jnp.where(probs == p1, row, 16), axis=0, keepdims=True)
        rest = jnp.where(row == i1, -1.0, probs)
        p2 = jnp.max(rest, axis=0, keepdims=True)
        i2 = jnp.min(jnp.where(rest == p2, row, 16), axis=0, keepdims=True)
        hit1, hit2 = row == i1, row == i2
        onehot = jnp.where(hit1 | hit2, 1.0, 0.0)
        ti = lax.broadcasted_iota(jnp.int32, (tm, tm), 0)
        tj = lax.broadcasted_iota(jnp.int32, (tm, tm), 1)
        before = jnp.where(ti < tj, 1.0, 0.0).astype(BF16)
        rank = carry_scr[:, 0:1] + _dot(onehot.astype(BF16), before)
        r1 = jnp.sum(jnp.where(hit1, rank, 0.0), axis=0, keepdims=True)
        r2 = jnp.sum(jnp.where(hit2, rank, 0.0), axis=0, keepdims=True)
        carry_scr[...] = carry_scr[...] + jnp.sum(onehot, axis=1, keepdims=True)
        cnt_ref[...] = carry_scr[...]
        r8 = lax.broadcasted_iota(jnp.int32, (8, tm), 0)
        vals = [p1 / (p1 + p2), p2 / (p1 + p2), i1.astype(F32), i2.astype(F32), r1, r2]
        meta = jnp.zeros((8, tm), F32)
        for k, v in enumerate(vals):
            meta = jnp.where(r8 == k, v, meta)
        meta_ref[...] = meta


def _outproj(o_segs, y_segs, x_segs, mods, g, w_out, router_wt):
    tm = 512
    with_router = router_wt is not None
    o_specs, o_bounds = _seg_specs(o_segs, tm)
    y_specs, y_bounds = _seg_specs(y_segs, tm)
    x_specs, x_bounds = _seg_specs(x_segs, tm)
    kern = functools.partial(_outproj_kernel, with_router=with_router, o_bounds=o_bounds, y_bounds=y_bounds,
                             x_bounds=x_bounds)
    in_specs = o_specs + y_specs + x_specs + [
        pl.BlockSpec((None, 6, D_MODEL), lambda i: (_mod_group(i, tm), 0, 0)),
        pl.BlockSpec((1, D_MODEL), lambda i: (0, 0)),
        pl.BlockSpec((D_MODEL, D_MODEL), lambda i: (0, 0), pipeline_mode=pl.Buffered(1)),
    ]
    args = list(o_segs) + list(y_segs) + list(x_segs) + [mods, g, w_out]
    row_spec = pl.BlockSpec((tm, D_MODEL), lambda i: (i, 0))
    out_specs = [row_spec, row_spec]
    out_shape = [jax.ShapeDtypeStruct((T_ALL, D_MODEL), F32),
                 jax.ShapeDtypeStruct((T_ALL, D_MODEL), F32 if with_router else BF16)]
    scratch = []
    if with_router:
        in_specs.append(pl.BlockSpec((16, D_MODEL), lambda i: (0, 0)))
        args.append(router_wt)
        out_specs += [pl.BlockSpec((8, tm), lambda i: (0, i)), pl.BlockSpec((16, 128), lambda i: (0, 0))]
        out_shape += [jax.ShapeDtypeStruct((8, T_ALL), F32), jax.ShapeDtypeStruct((16, 128), F32)]
        scratch = [pltpu.VMEM((16, 128), F32)]
    return pl.pallas_call(
        kern,
        grid=(T_ALL // tm,),
        in_specs=in_specs,
        out_specs=out_specs,
        out_shape=out_shape,
        scratch_shapes=scratch,
        compiler_params=_cparams(("arbitrary",)),
        name="outproj_router" if with_router else "outproj",
    )(*args)


MOE_ROWS = 2 * T_ALL
MOE_TILE = 256
MOE_TILES = MOE_ROWS // MOE_TILE
MOE_VISITS = MOE_TILES + N_EXPERTS - 1


def _row_copy(src, s, dst, d, sem):
    return pltpu.make_async_copy(src.at[pl.ds(s, 1)], dst.at[pl.ds(d, 1)], sem)


def _dispatch_kernel(p1_ref, p2_ref, h_ref, xs_ref, sem):
    tm = h_ref.shape[0]

    def issue(r, c):
        _row_copy(h_ref, r, xs_ref, p1_ref[0, 0, r], sem.at[0]).start()
        _row_copy(h_ref, r, xs_ref, p2_ref[0, 0, r], sem.at[1]).start()
        return c

    lax.fori_loop(0, tm, issue, 0, unroll=8)
    pltpu.make_async_copy(h_ref, xs_ref.at[pl.ds(0, tm)], sem.at[0]).wait()
    pltpu.make_async_copy(h_ref, xs_ref.at[pl.ds(0, tm)], sem.at[1]).wait()


def _dispatch(h, pos1, pos2):
    tm = 512
    nt = T_ALL // tm
    idx = lambda: pl.BlockSpec((1, 1, tm), lambda i: (i, 0, 0), memory_space=pltpu.SMEM)
    return pl.pallas_call(
        _dispatch_kernel,
        grid=(nt,),
        in_specs=[idx(), idx(), pl.BlockSpec((tm, D_MODEL), lambda i: (i, 0))],
        out_specs=pl.BlockSpec(memory_space=pl.ANY),
        out_shape=jax.ShapeDtypeStruct((MOE_ROWS, D_MODEL), F32),
        scratch_shapes=[pltpu.SemaphoreType.DMA((2,))],
        compiler_params=_cparams(("arbitrary",)),
        name="moe_dispatch",
    )(pos1.reshape(nt, 1, tm), pos2.reshape(nt, 1, tm), h)


def _experts_kernel(vt_ref, ve_ref, nv_ref, lo_ref, hi_ref, xs_ref, wg_ref, wu_ref, wd_ref, y_ref):
    v = pl.program_id(0)

    @pl.when(v < nv_ref[0])
    def _():
        e = ve_ref[v]
        x = xs_ref[...].astype(BF16)
        hid = _silu(_dot(x, wg_ref[...])) * _dot(x, wu_ref[...])
        y = _dot(hid.astype(BF16), wd_ref[...])
        row = vt_ref[v] * MOE_TILE + lax.broadcasted_iota(jnp.int32, (MOE_TILE, 1), 0)
        mine = (row >= lo_ref[e]) & (row < hi_ref[e])
        first_visit = (v == 0) | (vt_ref[jnp.maximum(v - 1, 0)] != vt_ref[v])

        @pl.when(first_visit)
        def _():
            y_ref[...] = jnp.where(mine, y, 0.0)

        @pl.when(jnp.logical_not(first_visit))
        def _():
            y_ref[...] = jnp.where(mine, y, y_ref[...])


def _experts(xs, wg, wu, wd, vt, ve, nv, lo, hi):
    grid_spec = pltpu.PrefetchScalarGridSpec(
        num_scalar_prefetch=5,
        grid=(MOE_VISITS,),
        in_specs=[
            pl.BlockSpec((MOE_TILE, D_MODEL), lambda v, vt, ve, nv, lo, hi: (vt[v], 0)),
            pl.BlockSpec((None, D_MODEL, F_EXPERT), lambda v, vt, ve, nv, lo, hi: (ve[v], 0, 0)),
            pl.BlockSpec((None, D_MODEL, F_EXPERT), lambda v, vt, ve, nv, lo, hi: (ve[v], 0, 0)),
            pl.BlockSpec((None, F_EXPERT, D_MODEL), lambda v, vt, ve, nv, lo, hi: (ve[v], 0, 0)),
        ],
        out_specs=pl.BlockSpec((MOE_TILE, D_MODEL), lambda v, vt, ve, nv, lo, hi: (vt[v], 0)),
    )
    return pl.pallas_call(
        _experts_kernel,
        grid_spec=grid_spec,
        out_shape=jax.ShapeDtypeStruct((MOE_ROWS, D_MODEL), F32),
        compiler_params=_cparams(("arbitrary",)),
        name="moe_experts",
    )(vt, ve, nv, lo, hi, xs, wg, wu, wd)


def _combine_kernel(p1c_ref, p2c_ref, p1n_ref, p2n_ref, y_hbm, x_ref, mod_ref, gate_ref, fg_ref,
                    outp_ref, outs_ref, ya_buf, yb_buf, sem):
    i = pl.program_id(0)
    n = pl.num_programs(0)
    tm = x_ref.shape[0]
    slot = i % 2

    def gather(pa_ref, pb_ref, s):
        def issue(r, c):
            _row_copy(y_hbm, pa_ref[0, 0, r], ya_buf.at[s], r, sem.at[0, s]).start()
            _row_copy(y_hbm, pb_ref[0, 0, r], yb_buf.at[s], r, sem.at[1, s]).start()
            return c

        lax.fori_loop(0, tm, issue, 0, unroll=8)

    @pl.when(i == 0)
    def _():
        gather(p1c_ref, p2c_ref, 0)

    @pl.when(i + 1 < n)
    def _():
        gather(p1n_ref, p2n_ref, 1 - slot)

    pltpu.make_async_copy(y_hbm.at[pl.ds(0, tm)], ya_buf.at[slot], sem.at[0, slot]).wait()
    pltpu.make_async_copy(y_hbm.at[pl.ds(0, tm)], yb_buf.at[slot], sem.at[1, slot]).wait()
    g = gate_ref[...]
    mix = g[:, 0:1] * ya_buf[slot] + g[:, 1:2] * yb_buf[slot]
    xo = _rms(x_ref[...] + mod_ref[5:6, :] * mix, fg_ref[...])

    @pl.when(i < T_PROMPT // tm)
    def _():
        outp_ref[...] = xo

    @pl.when(i >= T_PROMPT // tm)
    def _():
        outs_ref[...] = xo


def _combine(y, x, mods, gate_cols, pos1, pos2, final_g):
    tm = 256
    nt = T_ALL // tm
    ntp = T_PROMPT // tm
    cur = lambda: pl.BlockSpec((1, 1, tm), lambda i: (i, 0, 0), memory_space=pltpu.SMEM)
    nxt = lambda: pl.BlockSpec((1, 1, tm), lambda i: (jnp.minimum(i + 1, nt - 1), 0, 0), memory_space=pltpu.SMEM)
    p1, p2 = pos1.reshape(nt, 1, tm), pos2.reshape(nt, 1, tm)
    return pl.pallas_call(
        _combine_kernel,
        grid=(nt,),
        in_specs=[
            cur(), cur(), nxt(), nxt(),
            pl.BlockSpec(memory_space=pl.ANY),
            pl.BlockSpec((tm, D_MODEL), lambda i: (i, 0)),
            pl.BlockSpec((None, 6, D_MODEL), lambda i: (_mod_group(i, tm), 0, 0)),
            pl.BlockSpec((tm, 128), lambda i: (i, 0)),
            pl.BlockSpec((1, D_MODEL), lambda i: (0, 0)),
        ],
        out_specs=[
            pl.BlockSpec((tm, D_MODEL), lambda i: (jnp.minimum(i, ntp - 1), 0)),
            pl.BlockSpec((tm, D_MODEL), lambda i: (jnp.maximum(i - ntp, 0), 0)),
        ],
        out_shape=[
            jax.ShapeDtypeStruct((T_PROMPT, D_MODEL), F32), jax.ShapeDtypeStruct((T_SAMPLE, D_MODEL), F32),
        ],
        scratch_shapes=[
            pltpu.VMEM((2, tm, D_MODEL), F32), pltpu.VMEM((2, tm, D_MODEL), F32),
            pltpu.SemaphoreType.DMA((2, 2)),
        ],
        compiler_params=_cparams(("arbitrary",)),
        name="moe_combine",
    )(p1, p2, p1, p2, y, x, mods, gate_cols, final_g)


def _route_plan(meta, counts):
    i1, i2 = meta[2].astype(jnp.int32), meta[3].astype(jnp.int32)
    r1, r2 = meta[4].astype(jnp.int32), meta[5].astype(jnp.int32)
    cnt = counts[:N_EXPERTS, 0].astype(jnp.int32)
    hi = jnp.cumsum(cnt)
    lo = hi - cnt
    ex = jnp.arange(N_EXPERTS, dtype=jnp.int32)
    pos1 = jnp.sum(jnp.where(i1[:, None] == ex[None, :], lo[None, :], 0), axis=1) + r1
    pos2 = jnp.sum(jnp.where(i2[:, None] == ex[None, :], lo[None, :], 0), axis=1) + r2
    first_tile = lo // MOE_TILE
    n_vis_e = jnp.where(cnt > 0, (hi - 1) // MOE_TILE - first_tile + 1, 0)
    vis_hi = jnp.cumsum(n_vis_e)
    vis_lo = vis_hi - n_vis_e
    nv = vis_hi[-1]
    v = jnp.minimum(jnp.arange(MOE_VISITS, dtype=jnp.int32), nv - 1)
    ve = jnp.minimum(jnp.sum(v[:, None] >= vis_hi[None, :], axis=1), N_EXPERTS - 1).astype(jnp.int32)
    pick = lambda tab: jnp.sum(jnp.where(ve[:, None] == ex[None, :], tab[None, :], 0), axis=1)
    vt = (pick(first_tile) + v - pick(vis_lo)).astype(jnp.int32)
    return pos1, pos2, vt, ve, nv.reshape(1).astype(jnp.int32), lo.astype(jnp.int32), hi.astype(jnp.int32)


def _ffn_kernel(*refs, n_cast):
    h_ref, x_hbm, mod_ref, wg_ref, wu_ref, wd_ref = refs[:6]
    cast_in = refs[6:6 + n_cast]
    out_ref = refs[6 + n_cast]
    cast_out = refs[7 + n_cast:7 + 2 * n_cast]
    x_buf, sem = refs[7 + 2 * n_cast:]
    _side_cast(cast_in, cast_out)
    i, f = pl.program_id(0), pl.program_id(1)
    tm = h_ref.shape[0]
    x_copy = pltpu.make_async_copy(x_hbm.at[pl.ds(pl.multiple_of(i * tm, tm), tm)], x_buf, sem)

    @pl.when(f == 0)
    def _():
        x_copy.start()
        out_ref[...] = jnp.zeros_like(out_ref)

    h = h_ref[...]
    hid = _silu(_dot(h, wg_ref[0].astype(BF16))) * _dot(h, wu_ref[0].astype(BF16))
    out_ref[...] += _dot(hid.astype(BF16), wd_ref[0].astype(BF16))

    @pl.when(f == pl.num_programs(1) - 1)
    def _():
        x_copy.wait()
        out_ref[...] = x_buf[...] + mod_ref[5:6, :] * out_ref[...]


FFN_CAST_STEPS = 128


def _ffn(h, x, mods, wg, wu, wd, j, to_bf16):
    tm, tf = 1024, 256
    nf = F_DENSE // tf
    assert (T_ALL // tm) * nf >= FFN_CAST_STEPS

    cast_in, cast_out, cast_shapes = _side_cast_specs(to_bf16, FFN_CAST_STEPS, nf)
    outs = pl.pallas_call(
        functools.partial(_ffn_kernel, n_cast=len(to_bf16)),
        grid=(T_ALL // tm, nf),
        in_specs=[
            pl.BlockSpec((tm, D_MODEL), lambda i, f: (i, 0)),
            _ANY,
            pl.BlockSpec((None, 6, D_MODEL), lambda i, f: (_mod_group(i, tm), 0, 0)),
            pl.BlockSpec((1, D_MODEL, tf), lambda i, f: (j, 0, f)),
            pl.BlockSpec((1, D_MODEL, tf), lambda i, f: (j, 0, f)),
            pl.BlockSpec((1, tf, D_MODEL), lambda i, f: (j, f, 0)),
        ] + cast_in,
        out_specs=[pl.BlockSpec((tm, D_MODEL), lambda i, f: (i, 0))] + cast_out,
        out_shape=[jax.ShapeDtypeStruct((T_ALL, D_MODEL), F32)] + cast_shapes,
        scratch_shapes=[pltpu.VMEM((tm, D_MODEL), F32), pltpu.SemaphoreType.DMA(())],
        compiler_params=_cparams(("arbitrary", "arbitrary")),
        name="dense_ffn",
    )(h, x, mods, wg, wu, wd, *[part[0] for part in to_bf16])
    return outs[0], outs[1:]


def _rope_tables():
    n = DEC_SEQ
    rows = n // GRID_W
    t_row = jnp.repeat(jnp.arange(rows, dtype=F32), GRID_W)
    t_col = jnp.tile(jnp.arange(GRID_W, dtype=F32), rows)
    inv = 1.0 / (ROPE_THETA ** (jnp.arange(0, ROT_HALF, 2, dtype=F32) / ROT_HALF))
    ar, ac = t_row[:, None] * inv, t_col[:, None] * inv
    cos = jnp.concatenate([jnp.cos(ar), jnp.cos(ar), jnp.cos(ac), jnp.cos(ac)], axis=-1)
    sin_signed = jnp.concatenate([-jnp.sin(ar), jnp.sin(ar), -jnp.sin(ac), jnp.sin(ac)], axis=-1)
    return cos, sin_signed


def _pad_lanes(v, width=128):
    return jnp.pad(v, ((0, 0), (0, width - v.shape[-1])))


def kernel(x_prompt, x_sample, c, cache_k, cache_v, state_ssm_fwd, state_ssm_bwd, c_ctx, ada_w, ada_b, norm1_g, norm2_g, w_in, q_norm_g, k_norm_g, conv_w, conv_b, a_log_fwd, a_log_bwd, dt_bias_fwd, dt_bias_bwd, d_skip, ssd_norm_g, attn_out_g, w_out, ffn_w_gate, ffn_w_up, ffn_w_down, router_w, moe_w_gate, moe_w_up, moe_w_down, final_norm_g):
    assert DEPTH % 2 == 0
    cond = jnp.concatenate([c_ctx[None, :], c, jnp.zeros((N_COND - 1 - DEC_BATCH, D_MODEL), F32)], axis=0)
    mods_all = _ada_mods(cond, ada_w, ada_b).reshape(DEPTH, N_COND, 6, D_MODEL)
    cos, sin_signed = _rope_tables()

    w_in_t = jnp.swapaxes(w_in, 1, 2)
    x_segs = [x_prompt.reshape(T_PROMPT, D_MODEL), x_sample.reshape(T_SAMPLE, D_MODEL)]
    kv, states = None, None
    for l in range(DEPTH):
        mods = mods_all[l]
        proj, dt_raw, (w_o,) = _inproj(x_segs, mods, norm1_g[l][None, :], w_in_t, l, [(w_out.reshape(DEPTH * D_MODEL, D_MODEL), l * D_MODEL, D_MODEL)])

        qg, kg, og = q_norm_g[l][None, :], k_norm_g[l][None, :], attn_out_g[l][None, :]
        o_p, k_all, v_all = _attention_prompt(proj, 0, qg, kg, og, l, kv)
        kv = (k_all, v_all)
        o_s = _attention_sample(proj, T_PROMPT, qg, kg, og, cache_k, cache_v, cos, sin_signed, l)

        p = {
            'conv_w': conv_w[l], 'conv_b': conv_b[l][None, :],
            'dt_bias': _pad_lanes(jnp.concatenate([dt_bias_fwd[l], dt_bias_bwd[l]])[None, :]),
            'a_log': _pad_lanes(jnp.concatenate([a_log_fwd[l], a_log_bwd[l]])[None, :]),
            'd_skip': jnp.repeat(d_skip[l], SSD_HEAD_DIM)[None, :],
            'ssd_norm_g': ssd_norm_g[l][None, :],
        }
        y_p, sf, sb = _ssd(proj, dt_raw, p, SEQ, BATCH, 0, l, None, states)
        states = (sf, sb)
        y_s = _ssd(proj, dt_raw, p, DEC_SEQ, DEC_BATCH, T_PROMPT // DEC_SEQ, l, (state_ssm_fwd, state_ssm_bwd), None)

        j = l // 2
        g2 = norm2_g[l][None, :]
        if l % 2 == 0:
            x, h = _outproj([o_p, o_s], [y_p, y_s], x_segs, mods, g2, w_o, None)
            n_up, n_down = N_EXPERTS * D_MODEL, N_EXPERTS * F_EXPERT
            x, (eg, eu, ed) = _ffn(h, x, mods, ffn_w_gate, ffn_w_up, ffn_w_down, j, [
                (moe_w_gate.reshape(-1, F_EXPERT), j * n_up, n_up),
                (moe_w_up.reshape(-1, F_EXPERT), j * n_up, n_up),
                (moe_w_down.reshape(-1, D_MODEL), j * n_down, n_down)])
            expert_w = (eg.reshape(N_EXPERTS, D_MODEL, F_EXPERT), eu.reshape(N_EXPERTS, D_MODEL, F_EXPERT),
                        ed.reshape(N_EXPERTS, F_EXPERT, D_MODEL))
            x_segs = [x]
        else:
            router_wt = jnp.pad(router_w[j].T, ((0, 16 - N_EXPERTS), (0, 0)))
            x, h, meta, counts = _outproj([o_p, o_s], [y_p, y_s], x_segs, mods, g2, w_o, router_wt)
            pos1, pos2, vt, ve, nv, lo, hi = _route_plan(meta, counts)
            xs = _dispatch(h, pos1, pos2)
            ys = _experts(xs, *expert_w, vt, ve, nv, lo, hi)
            y_prompt, y_sample = _combine(ys, x, mods, _pad_lanes(meta[:2].T), pos1, pos2, final_norm_g[None, :])

    return (y_prompt.reshape(BATCH, SEQ, D_MODEL), y_sample.reshape(DEC_BATCH, DEC_SEQ, D_MODEL),
            kv[0], kv[1], states[0], states[1])
```

```python
import functools

import jax
import jax.numpy as jnp
from jax import lax
from jax.experimental import pallas as pl
from jax.experimental.pallas import tpu as pltpu

F32 = jnp.float32
BF16 = jnp.bfloat16

D_MODEL = 2048
BATCH = 16
SEQ = 256
DEPTH = 2
DEC_BATCH = 2
DEC_SEQ = 1024
PAST_LEN = 512
GRID_W = 64
D_ATTN = 1024
D_SSD = 1024
HEAD_DIM = 128
N_Q_HEADS = 8
N_KV_HEADS = 2
Q_PER_KV = 4
KV_DIM = 256
ROT_HALF = 64
ROPE_THETA = 10000.0
SSD_HEAD_DIM = 64
SSD_HEADS = 16
SSD_GROUPS = 2
HEADS_PER_GROUP = 8
D_STATE = 128
CONV_DIM = 1536
CHUNK = 128
N_MAIN = 4096
F_DENSE = 5632
N_EXPERTS = 8
F_EXPERT = 1024
EPS = 1e-6

T_PROMPT = BATCH * SEQ
T_SAMPLE = DEC_BATCH * DEC_SEQ
T_ALL = T_PROMPT + T_SAMPLE
N_COND = 16

VMEM_LIMIT = 56 * 1024 * 1024


def _cparams(sem):
    return pltpu.CompilerParams(dimension_semantics=sem, vmem_limit_bytes=VMEM_LIMIT)


def _mod_group(i, tm):
    return jnp.maximum(0, (i * tm - T_PROMPT + DEC_SEQ) // DEC_SEQ)


def _silu(x):
    return x * jax.nn.sigmoid(x)


def _rms(x, g):
    ms = jnp.mean(x * x, axis=-1, keepdims=True)
    return x * lax.rsqrt(ms + EPS) * g


def _dot(a, b):
    return jnp.dot(a, b, preferred_element_type=F32)


def _dot_nt(a, b):
    return lax.dot_general(a, b, (((1,), (1,)), ((), ())), preferred_element_type=F32)


def _split3(x):
    hi = x.astype(BF16)
    r1 = x - hi.astype(F32)
    mid = r1.astype(BF16)
    r2 = r1 - mid.astype(F32)
    return hi, mid, r2.astype(BF16)


def _ada_kernel(c_ref, w_ref, b_ref, o_ref):
    s = _silu(c_ref[...]).astype(BF16)
    o_ref[...] = _dot(s, w_ref[...].astype(BF16)) + b_ref[...]


def _ada_mods(cond, ada_w, ada_b):
    tn = 1024
    n_out = 6 * D_MODEL
    return pl.pallas_call(
        _ada_kernel,
        grid=(DEPTH, n_out // tn),
        in_specs=[
            pl.BlockSpec((N_COND, D_MODEL), lambda l, j: (0, 0)),
            pl.BlockSpec((None, D_MODEL, tn), lambda l, j: (l, 0, j)),
            pl.BlockSpec((None, 1, tn), lambda l, j: (l, 0, j)),
        ],
        out_specs=pl.BlockSpec((None, N_COND, tn), lambda l, j: (l, 0, j)),
        out_shape=jax.ShapeDtypeStruct((DEPTH, N_COND, n_out), F32),
        compiler_params=_cparams(("parallel", "parallel")),
        name="ada_mods",
    )(cond, ada_w, ada_b.reshape(DEPTH, 1, n_out))


def _side_cast_specs(parts, n_steps, inner):
    in_specs, out_specs, out_shapes = [], [], []
    for a, row0, nrows in parts:
        rows = nrows // n_steps
        blk0 = row0 // rows
        step = lambda i, j: jnp.minimum(i * inner + j, n_steps - 1)
        in_specs.append(pl.BlockSpec((rows, a.shape[1]), lambda i, j, blk0=blk0: (blk0 + step(i, j), 0)))
        out_specs.append(pl.BlockSpec((rows, a.shape[1]), lambda i, j: (step(i, j), 0)))
        out_shapes.append(jax.ShapeDtypeStruct((nrows, a.shape[1]), BF16))
    return in_specs, out_specs, out_shapes


def _side_cast(cast_in, cast_out):
    for src, dst in zip(cast_in, cast_out):
        dst[...] = src[...].astype(BF16)


INPROJ_NORM_ROWS = 1024
INPROJ_CAST_STEPS = 16


def _inproj_kernel(*refs, seg_rows, n_cast):
    n_seg = len(seg_rows)
    x_hbms = refs[:n_seg]
    mods_ref, g_ref, w_ref, wdt_ref = refs[n_seg:n_seg + 4]
    cast_in = refs[n_seg + 4:n_seg + 4 + n_cast]
    proj_ref, dt_ref = refs[n_seg + 4 + n_cast:n_seg + 6 + n_cast]
    cast_out = refs[n_seg + 6 + n_cast:n_seg + 6 + 2 * n_cast]
    x_buf, h_scr, sem = refs[n_seg + 6 + 2 * n_cast:]
    _side_cast(cast_in, cast_out)
    i, j = pl.program_id(0), pl.program_id(1)
    tm = x_buf.shape[0]

    def fetch(tile):
        start = 0
        for x_hbm, nrows in zip(x_hbms, seg_rows):
            b0, nb = start // tm, nrows // tm
            start += nrows

            @pl.when((tile >= b0) & (tile < b0 + nb))
            def _():
                r0 = pl.multiple_of((tile - b0) * tm, tm)
                pltpu.make_async_copy(x_hbm.at[pl.ds(r0, tm)], x_buf, sem).start()

    @pl.when(j == 0)
    def _():
        @pl.when(i == 0)
        def _():
            fetch(i)

        pltpu.make_async_copy(x_hbms[0].at[pl.ds(0, tm)], x_buf, sem).wait()
        for k in range(tm // INPROJ_NORM_ROWS):
            rows = slice(k * INPROJ_NORM_ROWS, (k + 1) * INPROJ_NORM_ROWS)
            mod = mods_ref[_mod_group(i * (tm // INPROJ_NORM_ROWS) + k, INPROJ_NORM_ROWS)]
            h = _rms(x_buf[rows, :], g_ref[...]) * (1.0 + mod[1:2, :]) + mod[0:1, :]
            h_scr[rows, :] = h.astype(BF16)

        @pl.when(i + 1 < pl.num_programs(0))
        def _():
            fetch(i + 1)

        n_dt = wdt_ref.shape[0]
        wdt = jnp.concatenate([wdt_ref[...], jnp.zeros((128 - n_dt, D_MODEL), F32)], axis=0)
        dt_ref[...] = _dot_nt(h_scr[...], wdt.astype(BF16))

    proj_ref[...] = _dot_nt(h_scr[...], w_ref[...].astype(BF16))


_ANY = pl.BlockSpec(memory_space=pl.ANY)


def _inproj(x_segs, mods, g, w_in_t, layer, to_bf16):
    tm, tn = 2048, 512
    n_dt = w_in_t.shape[1] - N_MAIN
    nj = N_MAIN // tn
    seg_rows = tuple(a.shape[0] for a in x_segs)
    assert all(r % tm == 0 for r in seg_rows) and sum(seg_rows) == T_ALL
    assert (T_ALL // tm) * nj >= INPROJ_CAST_STEPS
    cast_in, cast_out, cast_shapes = _side_cast_specs(to_bf16, INPROJ_CAST_STEPS, nj)
    outs = pl.pallas_call(
        functools.partial(_inproj_kernel, seg_rows=seg_rows, n_cast=len(to_bf16)),
        grid=(T_ALL // tm, nj),
        in_specs=[_ANY] * len(x_segs) + [
            pl.BlockSpec(mods.shape, lambda i, j: (0, 0, 0)),
            pl.BlockSpec((1, D_MODEL), lambda i, j: (0, 0)),
            pl.BlockSpec((None, tn, D_MODEL), lambda i, j: (layer, j, 0)),
            pl.BlockSpec((None, n_dt, D_MODEL), lambda i, j: (layer, N_MAIN // n_dt, 0)),
        ] + cast_in,
        out_specs=[
            pl.BlockSpec((tm, tn), lambda i, j: (i, j)),
            pl.BlockSpec((tm, 128), lambda i, j: (i, 0)),
        ] + cast_out,
        out_shape=[
            jax.ShapeDtypeStruct((T_ALL, N_MAIN), F32),
            jax.ShapeDtypeStruct((T_ALL, 128), F32),
        ] + cast_shapes,
        scratch_shapes=[pltpu.VMEM((tm, D_MODEL), F32), pltpu.VMEM((tm, D_MODEL), BF16),
                        pltpu.SemaphoreType.DMA(())],
        compiler_params=_cparams(("arbitrary", "arbitrary")),
        name="inproj",
    )(*x_segs, mods, g, w_in_t, w_in_t, *[part[0] for part in to_bf16])
    return outs[0], outs[1], outs[2:]


def _rope(x, cos, sin_signed):
    lane = lax.broadcasted_iota(jnp.int32, x.shape, 1)
    first = (lane // (ROT_HALF // 2)) % 2 == 0
    swapped = jnp.where(first, pltpu.roll(x, HEAD_DIM - ROT_HALF // 2, 1), pltpu.roll(x, ROT_HALF // 2, 1))
    return x * cos + swapped * sin_signed


def _attn_kernel(*refs, nk_new, has_ctx, n_prev):
    if has_ctx:
        (q_ref, kv_ref, qg_ref, kg_ref, og_ref, ck_ref, cv_ref, cq_ref, sq_ref, ckk_ref, skk_ref,
         o_ref, kb_scr, vb_scr, o_scr) = refs
    elif n_prev:
        (q_ref, kv_ref, qg_ref, kg_ref, og_ref, pk_ref, pv_ref, o_ref, ko_ref, vo_ref,
         kb_scr, vb_scr, o_scr) = refs
    else:
        (q_ref, kv_ref, qg_ref, kg_ref, og_ref, o_ref, ko_ref, vo_ref, kb_scr, vb_scr, o_scr) = refs

    @pl.when(pl.program_id(1) == 0)
    def _():
        if not has_ctx and n_prev:
            ko_ref[0:n_prev] = pk_ref[...]
            vo_ref[0:n_prev] = pv_ref[...]
        for g in range(N_KV_HEADS):
            sl = slice(g * HEAD_DIM, (g + 1) * HEAD_DIM)
            kn = _rms(kv_ref[:, sl], kg_ref[...])
            v = kv_ref[:, KV_DIM + g * HEAD_DIM:KV_DIM + (g + 1) * HEAD_DIM]
            vsl = slice(2 * g * HEAD_DIM, (2 * g + 1) * HEAD_DIM)
            vb_scr[0:nk_new, vsl] = v.astype(BF16)
            vb_scr[:, (2 * g + 1) * HEAD_DIM:(2 * g + 2) * HEAD_DIM] = jnp.ones((vb_scr.shape[0], HEAD_DIM), BF16)
            if has_ctx:
                kb_scr[0:nk_new, sl] = _rope(kn, ckk_ref[...], skk_ref[...]).astype(BF16)
                kb_scr[nk_new:, sl] = ck_ref[:, g, :].astype(BF16)
                vb_scr[nk_new:, vsl] = cv_ref[:, g, :].astype(BF16)
            else:
                kb_scr[:, sl] = kn.astype(BF16)
                ko_ref[n_prev, :, g, :] = kn
                vo_ref[n_prev, :, g, :] = v

    scale_log2e = HEAD_DIM ** -0.5 * 1.4426950408889634
    for h in range(N_Q_HEADS):
        g = h // Q_PER_KV
        sl = slice(h * HEAD_DIM, (h + 1) * HEAD_DIM)
        gsl = slice(g * HEAD_DIM, (g + 1) * HEAD_DIM)
        qn = _rms(q_ref[:, sl], qg_ref[...])
        if has_ctx:
            qn = _rope(qn, cq_ref[...], sq_ref[...])
        s = _dot_nt((qn * scale_log2e).astype(BF16), kb_scr[:, gsl])
        e = jnp.exp2(s - jnp.max(s, axis=-1, keepdims=True))
        pv = _dot(e.astype(BF16), vb_scr[:, 2 * g * HEAD_DIM:(2 * g + 2) * HEAD_DIM])
        o_scr[:, sl] = pv[:, :HEAD_DIM] / pv[:, HEAD_DIM:]
    o_ref[...] = _rms(o_scr[...], og_ref[...]).astype(BF16)


def _attention_prompt(proj, row0, qg, kg, og, layer, prev_kv):
    n = SEQ
    blk0 = row0 // n
    kern = functools.partial(_attn_kernel, nk_new=n, has_ctx=False, n_prev=layer)
    vec = lambda w: pl.BlockSpec((1, w), lambda b, i: (0, 0))
    cache_spec = lambda k: pl.BlockSpec((None, k, n, N_KV_HEADS, HEAD_DIM), lambda b, i: (b, 0, 0, 0, 0))
    cache_shape = jax.ShapeDtypeStruct((BATCH, layer + 1, n, N_KV_HEADS, HEAD_DIM), F32)
    in_specs = [
        pl.BlockSpec((n, D_ATTN), lambda b, i: (blk0 + b, 0)),
        pl.BlockSpec((n, 2 * KV_DIM), lambda b, i: (blk0 + b, 2)),
        vec(HEAD_DIM), vec(HEAD_DIM), vec(D_ATTN),
    ]
    args = [proj, proj, qg, kg, og]
    if layer:
        in_specs += [cache_spec(layer), cache_spec(layer)]
        args += list(prev_kv)
    return pl.pallas_call(
        kern,
        grid=(BATCH, 1),
        in_specs=in_specs,
        out_specs=[pl.BlockSpec((n, D_ATTN), lambda b, i: (b, 0)), cache_spec(layer + 1), cache_spec(layer + 1)],
        out_shape=[jax.ShapeDtypeStruct((T_PROMPT, D_ATTN), BF16), cache_shape, cache_shape],
        scratch_shapes=[
            pltpu.VMEM((n, KV_DIM), BF16), pltpu.VMEM((n, 2 * KV_DIM), BF16), pltpu.VMEM((n, D_ATTN), F32),
        ],
        compiler_params=_cparams(("parallel", "arbitrary")),
        name="attn_prompt",
    )(*args)


def _attention_sample(proj, row0, qg, kg, og, ck, cv, cos, sin_signed, layer):
    n, tq = DEC_SEQ, 512
    nq = n // tq
    nk = n + PAST_LEN
    kern = functools.partial(_attn_kernel, nk_new=n, has_ctx=True, n_prev=0)
    vec = lambda w: pl.BlockSpec((1, w), lambda b, i: (0, 0))
    q_blk0 = row0 // tq
    kv_blk0 = row0 // n
    return pl.pallas_call(
        kern,
        grid=(DEC_BATCH, nq),
        in_specs=[
            pl.BlockSpec((tq, D_ATTN), lambda b, i: (q_blk0 + b * nq + i, 0)),
            pl.BlockSpec((n, 2 * KV_DIM), lambda b, i: (kv_blk0 + b, 2)),
            vec(HEAD_DIM), vec(HEAD_DIM), vec(D_ATTN),
            pl.BlockSpec((None, None, PAST_LEN, N_KV_HEADS, HEAD_DIM), lambda b, i: (b, layer, 0, 0, 0)),
            pl.BlockSpec((None, None, PAST_LEN, N_KV_HEADS, HEAD_DIM), lambda b, i: (b, layer, 0, 0, 0)),
            pl.BlockSpec((tq, HEAD_DIM), lambda b, i: (i, 0)),
            pl.BlockSpec((tq, HEAD_DIM), lambda b, i: (i, 0)),
            pl.BlockSpec((n, HEAD_DIM), lambda b, i: (0, 0)),
            pl.BlockSpec((n, HEAD_DIM), lambda b, i: (0, 0)),
        ],
        out_specs=pl.BlockSpec((tq, D_ATTN), lambda b, i: (b * nq + i, 0)),
        out_shape=jax.ShapeDtypeStruct((T_SAMPLE, D_ATTN), BF16),
        scratch_shapes=[
            pltpu.VMEM((nk, KV_DIM), BF16), pltpu.VMEM((nk, 2 * KV_DIM), BF16), pltpu.VMEM((tq, D_ATTN), F32),
        ],
        compiler_params=_cparams(("parallel", "arbitrary")),
        name="attn_sample",
    )(proj, proj, qg, kg, og, ck, cv, cos, sin_signed, cos, sin_signed)


def _conv_silu(x, w, b):
    n = x.shape[0]
    row = lax.broadcasted_iota(jnp.int32, (n, 1), 0)
    prev = jnp.where(row == 0, 0.0, pltpu.roll(x, 1, 0))
    nxt = jnp.where(row == n - 1, 0.0, pltpu.roll(x, n - 1, 0))
    return _silu(prev * w[0:1, :] + x * w[1:2, :] + nxt * w[2:3, :] + b)


def _softplus(x):
    return jnp.maximum(x, 0.0) + jnp.log1p(jnp.exp(-jnp.abs(x)))


def _ssd_kernel(*refs, n, has_init, n_prev):
    if has_init:
        (za_ref, zb_ref, xa_ref, xb_ref, bc_ref, dt_ref, cw_ref, cb_ref, dtb_ref, alog_ref, dsk_ref, ng_ref,
         sf0_ref, sb0_ref, y_ref, xc_scr, bcc_scr, dts_scr, xt_scr, yt_scr, s_scr) = refs
    elif n_prev:
        (za_ref, zb_ref, xa_ref, xb_ref, bc_ref, dt_ref, cw_ref, cb_ref, dtb_ref, alog_ref, dsk_ref, ng_ref,
         psf_ref, psb_ref, y_ref, sf_ref, sb_ref, xc_scr, bcc_scr, dts_scr, xt_scr, yt_scr, s_scr) = refs
    else:
        (za_ref, zb_ref, xa_ref, xb_ref, bc_ref, dt_ref, cw_ref, cb_ref, dtb_ref, alog_ref, dsk_ref, ng_ref,
         y_ref, sf_ref, sb_ref, xc_scr, bcc_scr, dts_scr, xt_scr, yt_scr, s_scr) = refs
    nc = n // CHUNK
    gw = HEADS_PER_GROUP * SSD_HEAD_DIM

    xc_scr[:, 0:gw] = _conv_silu(xa_ref[...], cw_ref[:, 0:gw], cb_ref[:, 0:gw])
    xc_scr[:, gw:] = _conv_silu(xb_ref[...], cw_ref[:, gw:2 * gw], cb_ref[:, gw:2 * gw])
    bcc_scr[...] = _conv_silu(bc_ref[...], cw_ref[:, 2 * gw:], cb_ref[:, 2 * gw:])
    dts_scr[...] = _softplus(dt_ref[...] + dtb_ref[...])
    for g in range(SSD_GROUPS):
        hs = slice(g * HEADS_PER_GROUP, (g + 1) * HEADS_PER_GROUP)
        if has_init:
            s_scr[0, g] = sf0_ref[hs].reshape(gw, D_STATE)
            s_scr[1, g] = sb0_ref[hs].reshape(gw, D_STATE)
        else:
            s_scr[0, g] = jnp.zeros((gw, D_STATE), F32)
            s_scr[1, g] = jnp.zeros((gw, D_STATE), F32)

    def to_channel_major(c, carry):
        rows = pl.ds(pl.multiple_of(c * CHUNK, CHUNK), CHUNK)
        xt_scr[c] = xc_scr[rows, :].T
        yt_scr[c] = jnp.zeros((D_SSD, CHUNK), F32)
        return carry

    lax.fori_loop(0, nc, to_channel_major, 0)

    a_row = -jnp.exp(alog_ref[...])
    ri = lax.broadcasted_iota(jnp.int32, (CHUNK, CHUNK), 0)
    ci = lax.broadcasted_iota(jnp.int32, (CHUNK, CHUNK), 1)
    lower, upper = ci <= ri, ci >= ri
    n_dirs_heads = 2 * SSD_HEADS

    def scan_chunk(dirn, c):
        tri = (lower if dirn == 0 else upper).astype(BF16)
        valid_st = upper if dirn == 0 else lower
        tri_t = valid_st.astype(BF16)
        row0 = dirn * SSD_HEADS
        rows = pl.ds(pl.multiple_of(c * CHUNK, CHUNK), CHUNK)
        dt = dts_scr[rows, :]
        d = dt * a_row
        d1, d2, d3 = _split3(d)
        cs = _dot(tri, d1) + _dot(tri, d2) + _dot(tri, d3)
        dt_t = dt.T[0:n_dirs_heads, :]
        e1, e2, e3 = _split3(d.T[0:n_dirs_heads, :])
        cs_t = _dot(e1, tri_t) + _dot(e2, tri_t) + _dot(e3, tri_t)
        total = cs_t[:, CHUNK - 1:CHUNK] if dirn == 0 else cs_t[:, 0:1]
        e_in_t = jnp.exp(cs_t)
        to_end_t = jnp.exp(total - cs_t) * dt_t
        dec_t = jnp.broadcast_to(jnp.exp(total), (n_dirs_heads, D_STATE))
        for g in range(SSD_GROUPS):
            bm = bcc_scr[rows, g * D_STATE:(g + 1) * D_STATE].astype(BF16)
            cm = bcc_scr[rows, (SSD_GROUPS + g) * D_STATE:(SSD_GROUPS + g + 1) * D_STATE]
            g_st = _dot_nt(bm, cm.astype(BF16))
            c_nt = cm.T
            st = s_scr[dirn, g]
            xs_parts, dec_parts = [], []
            for hh in range(HEADS_PER_GROUP):
                h = g * HEADS_PER_GROUP + hh
                r = row0 + h
                ch = slice(h * SSD_HEAD_DIM, (h + 1) * SSD_HEAD_DIM)
                x_t = xt_scr[c, ch, :]
                diff = cs_t[r:r + 1, :] - cs[:, r:r + 1]
                a_st = (g_st * jnp.exp(jnp.where(valid_st, diff, -jnp.inf))).astype(BF16)
                c_e = (c_nt * e_in_t[r:r + 1, :]).astype(BF16)
                x_dt = (x_t * dt_t[r:r + 1, :]).astype(BF16)
                s_h = st[hh * SSD_HEAD_DIM:(hh + 1) * SSD_HEAD_DIM, :].astype(BF16)
                y_h = _dot(jnp.concatenate([x_dt, s_h], axis=1), jnp.concatenate([a_st, c_e], axis=0))
                yt_scr[c, ch, :] = yt_scr[c, ch, :] + y_h
                xs_parts.append((x_t * to_end_t[r:r + 1, :]).astype(BF16))
                dec_parts.append(jnp.broadcast_to(dec_t[r:r + 1, :], (SSD_HEAD_DIM, D_STATE)))
            ds = _dot(jnp.concatenate(xs_parts, axis=0), bm)
            s_scr[dirn, g] = st * jnp.concatenate(dec_parts, axis=0) + ds

    def body(i, carry):
        scan_chunk(0, i)
        scan_chunk(1, nc - 1 - i)
        return carry

    lax.fori_loop(0, nc, body, 0)

    def finish(c, carry):
        rows = pl.ds(pl.multiple_of(c * CHUNK, CHUNK), CHUNK)
        y = yt_scr[c].T + xc_scr[rows, :] * dsk_ref[...]
        ya = y[:, 0:gw] * _silu(za_ref[rows, :])
        yb = y[:, gw:] * _silu(zb_ref[rows, :])
        ms = (jnp.sum(ya * ya, axis=-1, keepdims=True) + jnp.sum(yb * yb, axis=-1, keepdims=True)) / D_SSD
        inv = lax.rsqrt(ms + EPS)
        y_ref[rows, 0:gw] = (ya * inv * ng_ref[:, 0:gw]).astype(BF16)
        y_ref[rows, gw:] = (yb * inv * ng_ref[:, gw:]).astype(BF16)
        return carry

    lax.fori_loop(0, nc, finish, 0)

    if not has_init:
        if n_prev:
            sf_ref[0:n_prev] = psf_ref[...]
            sb_ref[0:n_prev] = psb_ref[...]
        for g in range(SSD_GROUPS):
            hs = slice(g * HEADS_PER_GROUP, (g + 1) * HEADS_PER_GROUP)
            sf_ref[n_prev, hs] = s_scr[0, g].reshape(HEADS_PER_GROUP, SSD_HEAD_DIM, D_STATE)
            sb_ref[n_prev, hs] = s_scr[1, g].reshape(HEADS_PER_GROUP, SSD_HEAD_DIM, D_STATE)


def _ssd(proj, dt_raw, p, n, nb, row_blk0, layer, init, prev_states):
    has_init = init is not None
    n_prev = 0 if has_init else layer
    kern = functools.partial(_ssd_kernel, n=n, has_init=has_init, n_prev=n_prev)
    col = lambda cb: pl.BlockSpec((n, 512), lambda b: (row_blk0 + b, cb))
    vec = lambda r, w: pl.BlockSpec((r, w), lambda b: (0, 0))
    layers_spec = lambda k: pl.BlockSpec((None, k, SSD_HEADS, SSD_HEAD_DIM, D_STATE), lambda b: (b, 0, 0, 0, 0))
    in_specs = [
        col(3), col(4), col(5), col(6), col(7),
        pl.BlockSpec((n, 128), lambda b: (row_blk0 + b, 0)),
        vec(3, CONV_DIM), vec(1, CONV_DIM), vec(1, 128), vec(1, 128), vec(1, D_SSD), vec(1, D_SSD),
    ]
    args = [proj, proj, proj, proj, proj, dt_raw, p['conv_w'], p['conv_b'], p['dt_bias'], p['a_log'],
            p['d_skip'], p['ssd_norm_g']]
    y_spec = pl.BlockSpec((n, D_SSD), lambda b: (b, 0))
    y_shape = jax.ShapeDtypeStruct((nb * n, D_SSD), BF16)
    if has_init:
        init_spec = pl.BlockSpec((None, None, SSD_HEADS, SSD_HEAD_DIM, D_STATE), lambda b: (b, layer, 0, 0, 0))
        in_specs += [init_spec, init_spec]
        args += list(init)
        out_specs, out_shape = y_spec, y_shape
    else:
        if n_prev:
            in_specs += [layers_spec(n_prev), layers_spec(n_prev)]
            args += list(prev_states)
        st_shape = jax.ShapeDtypeStruct((nb, layer + 1, SSD_HEADS, SSD_HEAD_DIM, D_STATE), F32)
        out_specs = [y_spec, layers_spec(layer + 1), layers_spec(layer + 1)]
        out_shape = [y_shape, st_shape, st_shape]
    return pl.pallas_call(
        kern,
        grid=(nb,),
        in_specs=in_specs,
        out_specs=out_specs,
        out_shape=out_shape,
        scratch_shapes=[
            pltpu.VMEM((n, D_SSD), F32), pltpu.VMEM((n, 512), F32), pltpu.VMEM((n, 128), F32),
            pltpu.VMEM((n // CHUNK, D_SSD, CHUNK), F32), pltpu.VMEM((n // CHUNK, D_SSD, CHUNK), F32),
            pltpu.VMEM((2, SSD_GROUPS, 512, D_STATE), F32),
        ],
        compiler_params=_cparams(("parallel",)),
        name="ssd_sample" if has_init else "ssd_prompt",
    )(*args)


def _seg_specs(segs, tm):
    specs, bounds, start = [], [], 0
    for a in segs:
        b0, nblk = start // tm, a.shape[0] // tm
        specs.append(pl.BlockSpec((tm, a.shape[1]), lambda i, b0=b0, nblk=nblk: (jnp.clip(i - b0, 0, nblk - 1), 0)))
        bounds.append(b0)
        start += a.shape[0]
    return specs, tuple(bounds)


def _seg_pick(refs, bounds):
    i = pl.program_id(0)
    v = refs[0][...]
    for ref, b0 in zip(refs[1:], bounds[1:]):
        v = jnp.where(i >= b0, ref[...], v)
    return v


def _outproj_kernel(*refs, with_router, o_bounds, y_bounds, x_bounds):
    refs = list(refs)
    o_refs = [refs.pop(0) for _ in o_bounds]
    y_refs = [refs.pop(0) for _ in y_bounds]
    x_refs = [refs.pop(0) for _ in x_bounds]
    if with_router:
        mod_ref, g_ref, w_ref, rw_ref, xo_ref, h_ref, meta_ref, cnt_ref, carry_scr = refs
    else:
        mod_ref, g_ref, w_ref, xo_ref, h_ref = refs
    tm = xo_ref.shape[0]
    a = jnp.concatenate([_seg_pick(o_refs, o_bounds), _seg_pick(y_refs, y_bounds)], axis=1)
    xn = _seg_pick(x_refs, x_bounds) + mod_ref[2:3, :] * _dot(a, w_ref[...])
    xo_ref[...] = xn
    h = _rms(xn, g_ref[...]) * (1.0 + mod_ref[4:5, :]) + mod_ref[3:4, :]
    h_ref[...] = h.astype(h_ref.dtype)
    if with_router:

        @pl.when(pl.program_id(0) == 0)
        def _():
            carry_scr[...] = jnp.zeros_like(carry_scr)

        h1, h2, _ = _split3(h)
        w1, w2, _ = _split3(rw_ref[...])
        logits = _dot_nt(w1, h1) + _dot_nt(w2, h1) + _dot_nt(w1, h2)
        row = lax.broadcasted_iota(jnp.int32, logits.shape, 0)
        logits = jnp.where(row < N_EXPERTS, logits, -jnp.inf)
        e = jnp.exp(logits - jnp.max(logits, axis=0, keepdims=True))
        probs = e / jnp.sum(e, axis=0, keepdims=True)
        p1 = jnp.max(probs, axis=0, keepdims=True)
        i1 = jnp.min(jnp.where(probs == p1, row, 16), axis=0, keepdims=True)
        rest = jnp.where(row == i1, -1.0, probs)
        p2 = jnp.max(rest, axis=0, keepdims=True)
        i2 = jnp.min(jnp.where(rest == p2, row, 16), axis=0, keepdims=True)
        hit1, hit2 = row == i1, row == i2
        onehot = jnp.where(hit1 | hit2, 1.0, 0.0)
        ti = lax.broadcasted_iota(jnp.int32, (tm, tm), 0)
        tj = lax.broadcasted_iota(jnp.int32, (tm, tm), 1)
        before = jnp.where(ti < tj, 1.0, 0.0).astype(BF16)
        rank = carry_scr[:, 0:1] + _dot(onehot.astype(BF16), before)
        r1 = jnp.sum(jnp.where(hit1, rank, 0.0), axis=0, keepdims=True)
        r2 = jnp.sum(jnp.where(hit2, rank, 0.0), axis=0, keepdims=True)
        carry_scr[...] = carry_scr[...] + jnp.sum(onehot, axis=1, keepdims=True)
        cnt_ref[...] = carry_scr[...]
        r8 = lax.broadcasted_iota(jnp.int32, (8, tm), 0)
        vals = [p1 / (p1 + p2), p2 / (p1 + p2), i1.astype(F32), i2.astype(F32), r1, r2]
        meta = jnp.zeros((8, tm), F32)
        for k, v in enumerate(vals):
            meta = jnp.where(r8 == k, v, meta)
        meta_ref[...] = meta


def _outproj(o_segs, y_segs, x_segs, mods, g, w_out, router_wt):
    tm = 512
    with_router = router_wt is not None
    o_specs, o_bounds = _seg_specs(o_segs, tm)
    y_specs, y_bounds = _seg_specs(y_segs, tm)
    x_specs, x_bounds = _seg_specs(x_segs, tm)
    kern = functools.partial(_outproj_kernel, with_router=with_router, o_bounds=o_bounds, y_bounds=y_bounds,
                             x_bounds=x_bounds)
    in_specs = o_specs + y_specs + x_specs + [
        pl.BlockSpec((None, 6, D_MODEL), lambda i: (_mod_group(i, tm), 0, 0)),
        pl.BlockSpec((1, D_MODEL), lambda i: (0, 0)),
        pl.BlockSpec((D_MODEL, D_MODEL), lambda i: (0, 0), pipeline_mode=pl.Buffered(1)),
    ]
    args = list(o_segs) + list(y_segs) + list(x_segs) + [mods, g, w_out]
    row_spec = pl.BlockSpec((tm, D_MODEL), lambda i: (i, 0))
    out_specs = [row_spec, row_spec]
    out_shape = [jax.ShapeDtypeStruct((T_ALL, D_MODEL), F32),
                 jax.ShapeDtypeStruct((T_ALL, D_MODEL), F32 if with_router else BF16)]
    scratch = []
    if with_router:
        in_specs.append(pl.BlockSpec((16, D_MODEL), lambda i: (0, 0)))
        args.append(router_wt)
        out_specs += [pl.BlockSpec((8, tm), lambda i: (0, i)), pl.BlockSpec((16, 128), lambda i: (0, 0))]
        out_shape += [jax.ShapeDtypeStruct((8, T_ALL), F32), jax.ShapeDtypeStruct((16, 128), F32)]
        scratch = [pltpu.VMEM((16, 128), F32)]
    return pl.pallas_call(
        kern,
        grid=(T_ALL // tm,),
        in_specs=in_specs,
        out_specs=out_specs,
        out_shape=out_shape,
        scratch_shapes=scratch,
        compiler_params=_cparams(("arbitrary",)),
        name="outproj_router" if with_router else "outproj",
    )(*args)


MOE_ROWS = 2 * T_ALL
MOE_TILE = 256
MOE_TILES = MOE_ROWS // MOE_TILE
MOE_VISITS = MOE_TILES + N_EXPERTS - 1


def _row_copy(src, s, dst, d, sem):
    return pltpu.make_async_copy(src.at[pl.ds(s, 1)], dst.at[pl.ds(d, 1)], sem)


def _dispatch_kernel(p1_ref, p2_ref, h_ref, xs_ref, sem):
    tm = h_ref.shape[0]

    def issue(r, c):
        _row_copy(h_ref, r, xs_ref, p1_ref[0, 0, r], sem.at[0]).start()
        _row_copy(h_ref, r, xs_ref, p2_ref[0, 0, r], sem.at[1]).start()
        return c

    lax.fori_loop(0, tm, issue, 0, unroll=8)
    pltpu.make_async_copy(h_ref, xs_ref.at[pl.ds(0, tm)], sem.at[0]).wait()
    pltpu.make_async_copy(h_ref, xs_ref.at[pl.ds(0, tm)], sem.at[1]).wait()


def _dispatch(h, pos1, pos2):
    tm = 512
    nt = T_ALL // tm
    idx = lambda: pl.BlockSpec((1, 1, tm), lambda i: (i, 0, 0), memory_space=pltpu.SMEM)
    return pl.pallas_call(
        _dispatch_kernel,
        grid=(nt,),
        in_specs=[idx(), idx(), pl.BlockSpec((tm, D_MODEL), lambda i: (i, 0))],
        out_specs=pl.BlockSpec(memory_space=pl.ANY),
        out_shape=jax.ShapeDtypeStruct((MOE_ROWS, D_MODEL), F32),
        scratch_shapes=[pltpu.SemaphoreType.DMA((2,))],
        compiler_params=_cparams(("arbitrary",)),
        name="moe_dispatch",
    )(pos1.reshape(nt, 1, tm), pos2.reshape(nt, 1, tm), h)


def _experts_kernel(vt_ref, ve_ref, nv_ref, lo_ref, hi_ref, xs_ref, wg_ref, wu_ref, wd_ref, y_ref):
    v = pl.program_id(0)

    @pl.when(v < nv_ref[0])
    def _():
        e = ve_ref[v]
        x = xs_ref[...].astype(BF16)
        hid = _silu(_dot(x, wg_ref[...])) * _dot(x, wu_ref[...])
        y = _dot(hid.astype(BF16), wd_ref[...])
        row = vt_ref[v] * MOE_TILE + lax.broadcasted_iota(jnp.int32, (MOE_TILE, 1), 0)
        mine = (row >= lo_ref[e]) & (row < hi_ref[e])
        first_visit = (v == 0) | (vt_ref[jnp.maximum(v - 1, 0)] != vt_ref[v])

        @pl.when(first_visit)
        def _():
            y_ref[...] = jnp.where(mine, y, 0.0)

        @pl.when(jnp.logical_not(first_visit))
        def _():
            y_ref[...] = jnp.where(mine, y, y_ref[...])


def _experts(xs, wg, wu, wd, vt, ve, nv, lo, hi):
    grid_spec = pltpu.PrefetchScalarGridSpec(
        num_scalar_prefetch=5,
        grid=(MOE_VISITS,),
        in_specs=[
            pl.BlockSpec((MOE_TILE, D_MODEL), lambda v, vt, ve, nv, lo, hi: (vt[v], 0)),
            pl.BlockSpec((None, D_MODEL, F_EXPERT), lambda v, vt, ve, nv, lo, hi: (ve[v], 0, 0)),
            pl.BlockSpec((None, D_MODEL, F_EXPERT), lambda v, vt, ve, nv, lo, hi: (ve[v], 0, 0)),
            pl.BlockSpec((None, F_EXPERT, D_MODEL), lambda v, vt, ve, nv, lo, hi: (ve[v], 0, 0)),
        ],
        out_specs=pl.BlockSpec((MOE_TILE, D_MODEL), lambda v, vt, ve, nv, lo, hi: (vt[v], 0)),
    )
    return pl.pallas_call(
        _experts_kernel,
        grid_spec=grid_spec,
        out_shape=jax.ShapeDtypeStruct((MOE_ROWS, D_MODEL), F32),
        compiler_params=_cparams(("arbitrary",)),
        name="moe_experts",
    )(vt, ve, nv, lo, hi, xs, wg, wu, wd)


def _combine_kernel(p1c_ref, p2c_ref, p1n_ref, p2n_ref, y_hbm, x_ref, mod_ref, gate_ref, fg_ref,
                    outp_ref, outs_ref, ya_buf, yb_buf, sem):
    i = pl.program_id(0)
    n = pl.num_programs(0)
    tm = x_ref.shape[0]
    slot = i % 2

    def gather(pa_ref, pb_ref, s):
        def issue(r, c):
            _row_copy(y_hbm, pa_ref[0, 0, r], ya_buf.at[s], r, sem.at[0, s]).start()
            _row_copy(y_hbm, pb_ref[0, 0, r], yb_buf.at[s], r, sem.at[1, s]).start()
            return c

        lax.fori_loop(0, tm, issue, 0, unroll=8)

    @pl.when(i == 0)
    def _():
        gather(p1c_ref, p2c_ref, 0)

    @pl.when(i + 1 < n)
    def _():
        gather(p1n_ref, p2n_ref, 1 - slot)

    pltpu.make_async_copy(y_hbm.at[pl.ds(0, tm)], ya_buf.at[slot], sem.at[0, slot]).wait()
    pltpu.make_async_copy(y_hbm.at[pl.ds(0, tm)], yb_buf.at[slot], sem.at[1, slot]).wait()
    g = gate_ref[...]
    mix = g[:, 0:1] * ya_buf[slot] + g[:, 1:2] * yb_buf[slot]
    xo = _rms(x_ref[...] + mod_ref[5:6, :] * mix, fg_ref[...])

    @pl.when(i < T_PROMPT // tm)
    def _():
        outp_ref[...] = xo

    @pl.when(i >= T_PROMPT // tm)
    def _():
        outs_ref[...] = xo


def _combine(y, x, mods, gate_cols, pos1, pos2, final_g):
    tm = 256
    nt = T_ALL // tm
    ntp = T_PROMPT // tm
    cur = lambda: pl.BlockSpec((1, 1, tm), lambda i: (i, 0, 0), memory_space=pltpu.SMEM)
    nxt = lambda: pl.BlockSpec((1, 1, tm), lambda i: (jnp.minimum(i + 1, nt - 1), 0, 0), memory_space=pltpu.SMEM)
    p1, p2 = pos1.reshape(nt, 1, tm), pos2.reshape(nt, 1, tm)
    return pl.pallas_call(
        _combine_kernel,
        grid=(nt,),
        in_specs=[
            cur(), cur(), nxt(), nxt(),
            pl.BlockSpec(memory_space=pl.ANY),
            pl.BlockSpec((tm, D_MODEL), lambda i: (i, 0)),
            pl.BlockSpec((None, 6, D_MODEL), lambda i: (_mod_group(i, tm), 0, 0)),
            pl.BlockSpec((tm, 128), lambda i: (i, 0)),
            pl.BlockSpec((1, D_MODEL), lambda i: (0, 0)),
        ],
        out_specs=[
            pl.BlockSpec((tm, D_MODEL), lambda i: (jnp.minimum(i, ntp - 1), 0)),
            pl.BlockSpec((tm, D_MODEL), lambda i: (jnp.maximum(i - ntp, 0), 0)),
        ],
        out_shape=[
            jax.ShapeDtypeStruct((T_PROMPT, D_MODEL), F32), jax.ShapeDtypeStruct((T_SAMPLE, D_MODEL), F32),
        ],
        scratch_shapes=[
            pltpu.VMEM((2, tm, D_MODEL), F32), pltpu.VMEM((2, tm, D_MODEL), F32),
            pltpu.SemaphoreType.DMA((2, 2)),
        ],
        compiler_params=_cparams(("arbitrary",)),
        name="moe_combine",
    )(p1, p2, p1, p2, y, x, mods, gate_cols, final_g)


def _route_plan(meta, counts):
    i1, i2 = meta[2].astype(jnp.int32), meta[3].astype(jnp.int32)
    r1, r2 = meta[4].astype(jnp.int32), meta[5].astype(jnp.int32)
    cnt = counts[:N_EXPERTS, 0].astype(jnp.int32)
    hi = jnp.cumsum(cnt)
    lo = hi - cnt
    ex = jnp.arange(N_EXPERTS, dtype=jnp.int32)
    pos1 = jnp.sum(jnp.where(i1[:, None] == ex[None, :], lo[None, :], 0), axis=1) + r1
    pos2 = jnp.sum(jnp.where(i2[:, None] == ex[None, :], lo[None, :], 0), axis=1) + r2
    first_tile = lo // MOE_TILE
    n_vis_e = jnp.where(cnt > 0, (hi - 1) // MOE_TILE - first_tile + 1, 0)
    vis_hi = jnp.cumsum(n_vis_e)
    vis_lo = vis_hi - n_vis_e
    nv = vis_hi[-1]
    v = jnp.minimum(jnp.arange(MOE_VISITS, dtype=jnp.int32), nv - 1)
    ve = jnp.minimum(jnp.sum(v[:, None] >= vis_hi[None, :], axis=1), N_EXPERTS - 1).astype(jnp.int32)
    pick = lambda tab: jnp.sum(jnp.where(ve[:, None] == ex[None, :], tab[None, :], 0), axis=1)
    vt = (pick(first_tile) + v - pick(vis_lo)).astype(jnp.int32)
    return pos1, pos2, vt, ve, nv.reshape(1).astype(jnp.int32), lo.astype(jnp.int32), hi.astype(jnp.int32)


def _ffn_kernel(*refs, n_cast):
    h_ref, x_hbm, mod_ref, wg_ref, wu_ref, wd_ref = refs[:6]
    cast_in = refs[6:6 + n_cast]
    out_ref = refs[6 + n_cast]
    cast_out = refs[7 + n_cast:7 + 2 * n_cast]
    x_buf, sem = refs[7 + 2 * n_cast:]
    _side_cast(cast_in, cast_out)
    i, f = pl.program_id(0), pl.program_id(1)
    tm = h_ref.shape[0]
    x_copy = pltpu.make_async_copy(x_hbm.at[pl.ds(pl.multiple_of(i * tm, tm), tm)], x_buf, sem)

    @pl.when(f == 0)
    def _():
        x_copy.start()
        out_ref[...] = jnp.zeros_like(out_ref)

    h = h_ref[...]
    hid = _silu(_dot(h, wg_ref[0].astype(BF16))) * _dot(h, wu_ref[0].astype(BF16))
    out_ref[...] += _dot(hid.astype(BF16), wd_ref[0].astype(BF16))

    @pl.when(f == pl.num_programs(1) - 1)
    def _():
        x_copy.wait()
        out_ref[...] = x_buf[...] + mod_ref[5:6, :] * out_ref[...]


FFN_CAST_STEPS = 128


def _ffn(h, x, mods, wg, wu, wd, j, to_bf16):
    tm, tf = 1024, 256
    nf = F_DENSE // tf
    assert (T_ALL // tm) * nf >= FFN_CAST_STEPS

    cast_in, cast_out, cast_shapes = _side_cast_specs(to_bf16, FFN_CAST_STEPS, nf)
    outs = pl.pallas_call(
        functools.partial(_ffn_kernel, n_cast=len(to_bf16)),
        grid=(T_ALL // tm, nf),
        in_specs=[
            pl.BlockSpec((tm, D_MODEL), lambda i, f: (i, 0)),
            _ANY,
            pl.BlockSpec((None, 6, D_MODEL), lambda i, f: (_mod_group(i, tm), 0, 0)),
            pl.BlockSpec((1, D_MODEL, tf), lambda i, f: (j, 0, f)),
            pl.BlockSpec((1, D_MODEL, tf), lambda i, f: (j, 0, f)),
            pl.BlockSpec((1, tf, D_MODEL), lambda i, f: (j, f, 0)),
        ] + cast_in,
        out_specs=[pl.BlockSpec((tm, D_MODEL), lambda i, f: (i, 0))] + cast_out,
        out_shape=[jax.ShapeDtypeStruct((T_ALL, D_MODEL), F32)] + cast_shapes,
        scratch_shapes=[pltpu.VMEM((tm, D_MODEL), F32), pltpu.SemaphoreType.DMA(())],
        compiler_params=_cparams(("arbitrary", "arbitrary")),
        name="dense_ffn",
    )(h, x, mods, wg, wu, wd, *[part[0] for part in to_bf16])
    return outs[0], outs[1:]


def _rope_tables():
    n = DEC_SEQ
    rows = n // GRID_W
    t_row = jnp.repeat(jnp.arange(rows, dtype=F32), GRID_W)
    t_col = jnp.tile(jnp.arange(GRID_W, dtype=F32), rows)
    inv = 1.0 / (ROPE_THETA ** (jnp.arange(0, ROT_HALF, 2, dtype=F32) / ROT_HALF))
    ar, ac = t_row[:, None] * inv, t_col[:, None] * inv
    cos = jnp.concatenate([jnp.cos(ar), jnp.cos(ar), jnp.cos(ac), jnp.cos(ac)], axis=-1)
    sin_signed = jnp.concatenate([-jnp.sin(ar), jnp.sin(ar), -jnp.sin(ac), jnp.sin(ac)], axis=-1)
    return cos, sin_signed


def _pad_lanes(v, width=128):
    return jnp.pad(v, ((0, 0), (0, width - v.shape[-1])))


def kernel(x_prompt, x_sample, c, cache_k, cache_v, state_ssm_fwd, state_ssm_bwd, c_ctx, ada_w, ada_b, norm1_g, norm2_g, w_in, q_norm_g, k_norm_g, conv_w, conv_b, a_log_fwd, a_log_bwd, dt_bias_fwd, dt_bias_bwd, d_skip, ssd_norm_g, attn_out_g, w_out, ffn_w_gate, ffn_w_up, ffn_w_down, router_w, moe_w_gate, moe_w_up, moe_w_down, final_norm_g):
    assert DEPTH % 2 == 0
    cond = jnp.concatenate([c_ctx[None, :], c, jnp.zeros((N_COND - 1 - DEC_BATCH, D_MODEL), F32)], axis=0)
    mods_all = _ada_mods(cond, ada_w, ada_b).reshape(DEPTH, N_COND, 6, D_MODEL)
    cos, sin_signed = _rope_tables()

    w_in_t = jnp.swapaxes(w_in, 1, 2)
    x_segs = [x_prompt.reshape(T_PROMPT, D_MODEL), x_sample.reshape(T_SAMPLE, D_MODEL)]
    kv, states = None, None
    for l in range(DEPTH):
        mods = mods_all[l]
        proj, dt_raw, (w_o,) = _inproj(x_segs, mods, norm1_g[l][None, :], w_in_t, l, [(w_out.reshape(DEPTH * D_MODEL, D_MODEL), l * D_MODEL, D_MODEL)])

        qg, kg, og = q_norm_g[l][None, :], k_norm_g[l][None, :], attn_out_g[l][None, :]
        o_p, k_all, v_all = _attention_prompt(proj, 0, qg, kg, og, l, kv)
        kv = (k_all, v_all)
        o_s = _attention_sample(proj, T_PROMPT, qg, kg, og, cache_k, cache_v, cos, sin_signed, l)

        p = {
            'conv_w': conv_w[l], 'conv_b': conv_b[l][None, :],
            'dt_bias': _pad_lanes(jnp.concatenate([dt_bias_fwd[l], dt_bias_bwd[l]])[None, :]),
            'a_log': _pad_lanes(jnp.concatenate([a_log_fwd[l], a_log_bwd[l]])[None, :]),
            'd_skip': jnp.repeat(d_skip[l], SSD_HEAD_DIM)[None, :],
            'ssd_norm_g': ssd_norm_g[l][None, :],
        }
        y_p, sf, sb = _ssd(proj, dt_raw, p, SEQ, BATCH, 0, l, None, states)
        states = (sf, sb)
        y_s = _ssd(proj, dt_raw, p, DEC_SEQ, DEC_BATCH, T_PROMPT // DEC_SEQ, l, (state_ssm_fwd, state_ssm_bwd), None)

        j = l // 2
        g2 = norm2_g[l][None, :]
        if l % 2 == 0:
            x, h = _outproj([o_p, o_s], [y_p, y_s], x_segs, mods, g2, w_o, None)
            n_up, n_down = N_EXPERTS * D_MODEL, N_EXPERTS * F_EXPERT
            x, (eg, eu, ed) = _ffn(h, x, mods, ffn_w_gate, ffn_w_up, ffn_w_down, j, [
                (moe_w_gate.reshape(-1, F_EXPERT), j * n_up, n_up),
                (moe_w_up.reshape(-1, F_EXPERT), j * n_up, n_up),
                (moe_w_down.reshape(-1, D_MODEL), j * n_down, n_down)])
            expert_w = (eg.reshape(N_EXPERTS, D_MODEL, F_EXPERT), eu.reshape(N_EXPERTS, D_MODEL, F_EXPERT),
                        ed.reshape(N_EXPERTS, F_EXPERT, D_MODEL))
            x_segs = [x]
        else:
            router_wt = jnp.pad(router_w[j].T, ((0, 16 - N_EXPERTS), (0, 0)))
            x, h, meta, counts = _outproj([o_p, o_s], [y_p, y_s], x_segs, mods, g2, w_o, router_wt)
            pos1, pos2, vt, ve, nv, lo, hi = _route_plan(meta, counts)
            xs = _dispatch(h, pos1, pos2)
            ys = _experts(xs, *expert_w, vt, ve, nv, lo, hi)
            y_prompt, y_sample = _combine(ys, x, mods, _pad_lanes(meta[:2].T), pos1, pos2, final_norm_g[None, :])

    return (y_prompt.reshape(BATCH, SEQ, D_MODEL), y_sample.reshape(DEC_BATCH, DEC_SEQ, D_MODEL),
            kv[0], kv[1], states[0], states[1])
```

```python
import functools

import jax
import jax.numpy as jnp
from jax import lax
from jax.experimental import pallas as pl
from jax.experimental.pallas import tpu as pltpu

F32 = jnp.float32
BF16 = jnp.bfloat16

D_MODEL = 2048
BATCH = 16
SEQ = 256
DEPTH = 2
DEC_BATCH = 2
DEC_SEQ = 1024
PAST_LEN = 512
GRID_W = 64
D_ATTN = 1024
D_SSD = 1024
HEAD_DIM = 128
N_Q_HEADS = 8
N_KV_HEADS = 2
Q_PER_KV = 4
KV_DIM = 256
ROT_HALF = 64
ROPE_THETA = 10000.0
SSD_HEAD_DIM = 64
SSD_HEADS = 16
SSD_GROUPS = 2
HEADS_PER_GROUP = 8
D_STATE = 128
CONV_DIM = 1536
CHUNK = 128
N_MAIN = 4096
F_DENSE = 5632
N_EXPERTS = 8
F_EXPERT = 1024
EPS = 1e-6

T_PROMPT = BATCH * SEQ
T_SAMPLE = DEC_BATCH * DEC_SEQ
T_ALL = T_PROMPT + T_SAMPLE
N_COND = 16

VMEM_LIMIT = 58 * 1024 * 1024


def _cparams(sem):
    return pltpu.CompilerParams(dimension_semantics=sem, vmem_limit_bytes=VMEM_LIMIT)


def _mod_group(i, tm):
    return jnp.maximum(0, (i * tm - T_PROMPT + DEC_SEQ) // DEC_SEQ)


def _silu(x):
    return x * jax.nn.sigmoid(x)


def _rms(x, g):
    ms = jnp.mean(x * x, axis=-1, keepdims=True)
    return x * lax.rsqrt(ms + EPS) * g


def _dot(a, b):
    return jnp.dot(a, b, preferred_element_type=F32)


def _dot_nt(a, b):
    return lax.dot_general(a, b, (((1,), (1,)), ((), ())), preferred_element_type=F32)


def _split3(x):
    hi = x.astype(BF16)
    r1 = x - hi.astype(F32)
    mid = r1.astype(BF16)
    r2 = r1 - mid.astype(F32)
    return hi, mid, r2.astype(BF16)


def _ada_kernel(c_ref, w_ref, b_ref, o_ref):
    s = _silu(c_ref[...]).astype(BF16)
    o_ref[...] = _dot(s, w_ref[...].astype(BF16)) + b_ref[...]


def _ada_mods(cond, ada_w, ada_b):
    tn = 1024
    n_out = 6 * D_MODEL
    return pl.pallas_call(
        _ada_kernel,
        grid=(DEPTH, n_out // tn),
        in_specs=[
            pl.BlockSpec((N_COND, D_MODEL), lambda l, j: (0, 0)),
            pl.BlockSpec((None, D_MODEL, tn), lambda l, j: (l, 0, j)),
            pl.BlockSpec((None, 1, tn), lambda l, j: (l, 0, j)),
        ],
        out_specs=pl.BlockSpec((None, N_COND, tn), lambda l, j: (l, 0, j)),
        out_shape=jax.ShapeDtypeStruct((DEPTH, N_COND, n_out), F32),
        compiler_params=_cparams(("parallel", "parallel")),
        name="ada_mods",
    )(cond, ada_w, ada_b.reshape(DEPTH, 1, n_out))


def _side_cast_specs(parts, n_steps, linear_step):
    in_specs, out_specs, out_shapes = [], [], []
    for a, row0, nrows in parts:
        rows = nrows // n_steps
        blk0 = row0 // rows
        step = lambda *ids: jnp.minimum(linear_step(*ids), n_steps - 1)
        in_specs.append(pl.BlockSpec((rows, a.shape[1]), lambda *ids, blk0=blk0: (blk0 + step(*ids), 0)))
        out_specs.append(pl.BlockSpec((rows, a.shape[1]), lambda *ids: (step(*ids), 0)))
        out_shapes.append(jax.ShapeDtypeStruct((nrows, a.shape[1]), BF16))
    return in_specs, out_specs, out_shapes


def _side_cast(cast_in, cast_out):
    for src, dst in zip(cast_in, cast_out):
        dst[...] = src[...].astype(BF16)


INPROJ_NORM_ROWS = 1024
INPROJ_CAST_STEPS = 16


def _inproj_kernel(*refs, seg_rows, n_cast):
    n_seg = len(seg_rows)
    x_hbms = refs[:n_seg]
    mods_ref, g_ref, w_ref, wdt_ref = refs[n_seg:n_seg + 4]
    cast_in = refs[n_seg + 4:n_seg + 4 + n_cast]
    proj_ref, dt_ref = refs[n_seg + 4 + n_cast:n_seg + 6 + n_cast]
    cast_out = refs[n_seg + 6 + n_cast:n_seg + 6 + 2 * n_cast]
    x_buf, h_scr, sem = refs[n_seg + 6 + 2 * n_cast:]
    _side_cast(cast_in, cast_out)
    i, j = pl.program_id(0), pl.program_id(1)
    tm = x_buf.shape[0]

    def fetch(tile):
        start = 0
        for x_hbm, nrows in zip(x_hbms, seg_rows):
            b0, nb = start // tm, nrows // tm
            start += nrows

            @pl.when((tile >= b0) & (tile < b0 + nb))
            def _():
                r0 = pl.multiple_of((tile - b0) * tm, tm)
                pltpu.make_async_copy(x_hbm.at[pl.ds(r0, tm)], x_buf, sem).start()

    @pl.when(j == 0)
    def _():
        @pl.when(i == 0)
        def _():
            fetch(i)

        pltpu.make_async_copy(x_hbms[0].at[pl.ds(0, tm)], x_buf, sem).wait()
        for k in range(tm // INPROJ_NORM_ROWS):
            rows = slice(k * INPROJ_NORM_ROWS, (k + 1) * INPROJ_NORM_ROWS)
            mod = mods_ref[_mod_group(i * (tm // INPROJ_NORM_ROWS) + k, INPROJ_NORM_ROWS)]
            h = _rms(x_buf[rows, :], g_ref[...]) * (1.0 + mod[1:2, :]) + mod[0:1, :]
            h_scr[rows, :] = h.astype(BF16)

        @pl.when(i + 1 < pl.num_programs(0))
        def _():
            fetch(i + 1)

        n_dt = wdt_ref.shape[0]
        wdt = jnp.concatenate([wdt_ref[...], jnp.zeros((128 - n_dt, D_MODEL), F32)], axis=0)
        dt_ref[...] = _dot_nt(h_scr[...], wdt.astype(BF16))

    proj_ref[...] = _dot_nt(h_scr[...], w_ref[...].astype(BF16))


_ANY = pl.BlockSpec(memory_space=pl.ANY)


def _inproj(x_segs, mods, g, w_in_t, layer, to_bf16):
    tm, tn = 2048, 512
    n_dt = w_in_t.shape[1] - N_MAIN
    nj = N_MAIN // tn
    seg_rows = tuple(a.shape[0] for a in x_segs)
    assert all(r % tm == 0 for r in seg_rows) and sum(seg_rows) == T_ALL
    assert (T_ALL // tm) * nj >= INPROJ_CAST_STEPS
    cast_in, cast_out, cast_shapes = _side_cast_specs(to_bf16, INPROJ_CAST_STEPS, lambda i, j: i * nj + j)
    outs = pl.pallas_call(
        functools.partial(_inproj_kernel, seg_rows=seg_rows, n_cast=len(to_bf16)),
        grid=(T_ALL // tm, nj),
        in_specs=[_ANY] * len(x_segs) + [
            pl.BlockSpec(mods.shape, lambda i, j: (0, 0, 0)),
            pl.BlockSpec((1, D_MODEL), lambda i, j: (0, 0)),
            pl.BlockSpec((None, tn, D_MODEL), lambda i, j: (layer, j, 0)),
            pl.BlockSpec((None, n_dt, D_MODEL), lambda i, j: (layer, N_MAIN // n_dt, 0)),
        ] + cast_in,
        out_specs=[
            pl.BlockSpec((tm, tn), lambda i, j: (i, j)),
            pl.BlockSpec((tm, 128), lambda i, j: (i, 0)),
        ] + cast_out,
        out_shape=[
            jax.ShapeDtypeStruct((T_ALL, N_MAIN), F32),
            jax.ShapeDtypeStruct((T_ALL, 128), F32),
        ] + cast_shapes,
        scratch_shapes=[pltpu.VMEM((tm, D_MODEL), F32), pltpu.VMEM((tm, D_MODEL), BF16),
                        pltpu.SemaphoreType.DMA(())],
        compiler_params=_cparams(("arbitrary", "arbitrary")),
        name="inproj",
    )(*x_segs, mods, g, w_in_t, w_in_t, *[part[0] for part in to_bf16])
    return outs[0], outs[1], outs[2:]


def _rope(x, cos, sin_signed):
    lane = lax.broadcasted_iota(jnp.int32, x.shape, 1)
    first = (lane // (ROT_HALF // 2)) % 2 == 0
    swapped = jnp.where(first, pltpu.roll(x, HEAD_DIM - ROT_HALF // 2, 1), pltpu.roll(x, ROT_HALF // 2, 1))
    return x * cos + swapped * sin_signed


def _attn_kernel(*refs, nk_new, has_ctx, n_prev):
    if has_ctx:
        (q_ref, kv_ref, qg_ref, kg_ref, og_ref, ck_ref, cv_ref, cq_ref, sq_ref, ckk_ref, skk_ref,
         o_ref, kb_scr, vb_scr, o_scr) = refs
    elif n_prev:
        (q_ref, kv_ref, qg_ref, kg_ref, og_ref, pk_ref, pv_ref, o_ref, ko_ref, vo_ref,
         kb_scr, vb_scr, o_scr) = refs
    else:
        (q_ref, kv_ref, qg_ref, kg_ref, og_ref, o_ref, ko_ref, vo_ref, kb_scr, vb_scr, o_scr) = refs

    @pl.when(pl.program_id(1) == 0)
    def _():
        if not has_ctx and n_prev:
            ko_ref[0:n_prev] = pk_ref[...]
            vo_ref[0:n_prev] = pv_ref[...]
        for g in range(N_KV_HEADS):
            sl = slice(g * HEAD_DIM, (g + 1) * HEAD_DIM)
            kn = _rms(kv_ref[:, sl], kg_ref[...])
            v = kv_ref[:, KV_DIM + g * HEAD_DIM:KV_DIM + (g + 1) * HEAD_DIM]
            vsl = slice(2 * g * HEAD_DIM, (2 * g + 1) * HEAD_DIM)
            vb_scr[0:nk_new, vsl] = v.astype(BF16)
            vb_scr[:, (2 * g + 1) * HEAD_DIM:(2 * g + 2) * HEAD_DIM] = jnp.ones((vb_scr.shape[0], HEAD_DIM), BF16)
            if has_ctx:
                kb_scr[0:nk_new, sl] = _rope(kn, ckk_ref[...], skk_ref[...]).astype(BF16)
                kb_scr[nk_new:, sl] = ck_ref[:, g, :].astype(BF16)
                vb_scr[nk_new:, vsl] = cv_ref[:, g, :].astype(BF16)
            else:
                kb_scr[:, sl] = kn.astype(BF16)
                ko_ref[n_prev, :, g, :] = kn
                vo_ref[n_prev, :, g, :] = v

    scale_log2e = HEAD_DIM ** -0.5 * 1.4426950408889634
    for h in range(N_Q_HEADS):
        g = h // Q_PER_KV
        sl = slice(h * HEAD_DIM, (h + 1) * HEAD_DIM)
        gsl = slice(g * HEAD_DIM, (g + 1) * HEAD_DIM)
        qn = _rms(q_ref[:, sl], qg_ref[...])
        if has_ctx:
            qn = _rope(qn, cq_ref[...], sq_ref[...])
        s = _dot_nt((qn * scale_log2e).astype(BF16), kb_scr[:, gsl])
        e = jnp.exp2(s - jnp.max(s, axis=-1, keepdims=True))
        pv = _dot(e.astype(BF16), vb_scr[:, 2 * g * HEAD_DIM:(2 * g + 2) * HEAD_DIM])
        o_scr[:, sl] = pv[:, :HEAD_DIM] / pv[:, HEAD_DIM:]
    o_ref[...] = _rms(o_scr[...], og_ref[...]).astype(BF16)


def _attention_prompt(proj, row0, qg, kg, og, layer, prev_kv):
    n = SEQ
    blk0 = row0 // n
    kern = functools.partial(_attn_kernel, nk_new=n, has_ctx=False, n_prev=layer)
    vec = lambda w: pl.BlockSpec((1, w), lambda b, i: (0, 0))
    cache_spec = lambda k: pl.BlockSpec((None, k, n, N_KV_HEADS, HEAD_DIM), lambda b, i: (b, 0, 0, 0, 0))
    cache_shape = jax.ShapeDtypeStruct((BATCH, layer + 1, n, N_KV_HEADS, HEAD_DIM), F32)
    in_specs = [
        pl.BlockSpec((n, D_ATTN), lambda b, i: (blk0 + b, 0)),
        pl.BlockSpec((n, 2 * KV_DIM), lambda b, i: (blk0 + b, 2)),
        vec(HEAD_DIM), vec(HEAD_DIM), vec(D_ATTN),
    ]
    args = [proj, proj, qg, kg, og]
    if layer:
        in_specs += [cache_spec(layer), cache_spec(layer)]
        args += list(prev_kv)
    return pl.pallas_call(
        kern,
        grid=(BATCH, 1),
        in_specs=in_specs,
        out_specs=[pl.BlockSpec((n, D_ATTN), lambda b, i: (b, 0)), cache_spec(layer + 1), cache_spec(layer + 1)],
        out_shape=[jax.ShapeDtypeStruct((T_PROMPT, D_ATTN), BF16), cache_shape, cache_shape],
        scratch_shapes=[
            pltpu.VMEM((n, KV_DIM), BF16), pltpu.VMEM((n, 2 * KV_DIM), BF16), pltpu.VMEM((n, D_ATTN), F32),
        ],
        compiler_params=_cparams(("parallel", "arbitrary")),
        name="attn_prompt",
    )(*args)


def _attention_sample(proj, row0, qg, kg, og, ck, cv, cos, sin_signed, layer):
    n, tq = DEC_SEQ, 512
    nq = n // tq
    nk = n + PAST_LEN
    kern = functools.partial(_attn_kernel, nk_new=n, has_ctx=True, n_prev=0)
    vec = lambda w: pl.BlockSpec((1, w), lambda b, i: (0, 0))
    q_blk0 = row0 // tq
    kv_blk0 = row0 // n
    return pl.pallas_call(
        kern,
        grid=(DEC_BATCH, nq),
        in_specs=[
            pl.BlockSpec((tq, D_ATTN), lambda b, i: (q_blk0 + b * nq + i, 0)),
            pl.BlockSpec((n, 2 * KV_DIM), lambda b, i: (kv_blk0 + b, 2)),
            vec(HEAD_DIM), vec(HEAD_DIM), vec(D_ATTN),
            pl.BlockSpec((None, None, PAST_LEN, N_KV_HEADS, HEAD_DIM), lambda b, i: (b, layer, 0, 0, 0)),
            pl.BlockSpec((None, None, PAST_LEN, N_KV_HEADS, HEAD_DIM), lambda b, i: (b, layer, 0, 0, 0)),
            pl.BlockSpec((tq, HEAD_DIM), lambda b, i: (i, 0)),
            pl.BlockSpec((tq, HEAD_DIM), lambda b, i: (i, 0)),
            pl.BlockSpec((n, HEAD_DIM), lambda b, i: (0, 0)),
            pl.BlockSpec((n, HEAD_DIM), lambda b, i: (0, 0)),
        ],
        out_specs=pl.BlockSpec((tq, D_ATTN), lambda b, i: (b * nq + i, 0)),
        out_shape=jax.ShapeDtypeStruct((T_SAMPLE, D_ATTN), BF16),
        scratch_shapes=[
            pltpu.VMEM((nk, KV_DIM), BF16), pltpu.VMEM((nk, 2 * KV_DIM), BF16), pltpu.VMEM((tq, D_ATTN), F32),
        ],
        compiler_params=_cparams(("parallel", "arbitrary")),
        name="attn_sample",
    )(proj, proj, qg, kg, og, ck, cv, cos, sin_signed, cos, sin_signed)


def _conv_silu(x, w, b):
    n = x.shape[0]
    row = lax.broadcasted_iota(jnp.int32, (n, 1), 0)
    prev = jnp.where(row == 0, 0.0, pltpu.roll(x, 1, 0))
    nxt = jnp.where(row == n - 1, 0.0, pltpu.roll(x, n - 1, 0))
    return _silu(prev * w[0:1, :] + x * w[1:2, :] + nxt * w[2:3, :] + b)


def _softplus(x):
    return jnp.maximum(x, 0.0) + jnp.log1p(jnp.exp(-jnp.abs(x)))


def _ssd_kernel(*refs, n, has_init, n_prev, n_cast):
    refs = list(refs)
    (za_ref, zb_ref, xa_ref, xb_ref, bc_ref, dt_ref, cw_ref, cb_ref, dtb_ref, alog_ref, dsk_ref,
     ng_ref) = refs[:12]
    del refs[:12]
    if has_init:
        sf0_ref, sb0_ref = refs.pop(0), refs.pop(0)
    elif n_prev:
        psf_ref, psb_ref = refs.pop(0), refs.pop(0)
    cast_in = [refs.pop(0) for _ in range(n_cast)]
    y_ref = refs.pop(0)
    if not has_init:
        sf_ref, sb_ref = refs.pop(0), refs.pop(0)
    cast_out = [refs.pop(0) for _ in range(n_cast)]
    xc_scr, bcc_scr, dts_scr, xt_scr, yt_scr, s_scr = refs
    _side_cast(cast_in, cast_out)
    nc = n // CHUNK
    gw = HEADS_PER_GROUP * SSD_HEAD_DIM

    xc_scr[:, 0:gw] = _conv_silu(xa_ref[...], cw_ref[:, 0:gw], cb_ref[:, 0:gw])
    xc_scr[:, gw:] = _conv_silu(xb_ref[...], cw_ref[:, gw:2 * gw], cb_ref[:, gw:2 * gw])
    bcc_scr[...] = _conv_silu(bc_ref[...], cw_ref[:, 2 * gw:], cb_ref[:, 2 * gw:])
    dts_scr[...] = _softplus(dt_ref[...] + dtb_ref[...])
    for g in range(SSD_GROUPS):
        hs = slice(g * HEADS_PER_GROUP, (g + 1) * HEADS_PER_GROUP)
        if has_init:
            s_scr[0, g] = sf0_ref[hs].reshape(gw, D_STATE)
            s_scr[1, g] = sb0_ref[hs].reshape(gw, D_STATE)
        else:
            s_scr[0, g] = jnp.zeros((gw, D_STATE), F32)
            s_scr[1, g] = jnp.zeros((gw, D_STATE), F32)

    def to_channel_major(c, carry):
        rows = pl.ds(pl.multiple_of(c * CHUNK, CHUNK), CHUNK)
        xt_scr[c] = xc_scr[rows, :].T
        yt_scr[c] = jnp.zeros((D_SSD, CHUNK), F32)
        return carry

    lax.fori_loop(0, nc, to_channel_major, 0)

    a_row = -jnp.exp(alog_ref[...])
    ri = lax.broadcasted_iota(jnp.int32, (CHUNK, CHUNK), 0)
    ci = lax.broadcasted_iota(jnp.int32, (CHUNK, CHUNK), 1)
    lower, upper = ci <= ri, ci >= ri
    n_dirs_heads = 2 * SSD_HEADS

    def scan_chunk(dirn, c):
        tri = (lower if dirn == 0 else upper).astype(BF16)
        valid_st = upper if dirn == 0 else lower
        tri_t = valid_st.astype(BF16)
        row0 = dirn * SSD_HEADS
        rows = pl.ds(pl.multiple_of(c * CHUNK, CHUNK), CHUNK)
        dt = dts_scr[rows, :]
        d = dt * a_row
        d1, d2, d3 = _split3(d)
        cs = _dot(tri, d1) + _dot(tri, d2) + _dot(tri, d3)
        dt_t = dt.T[0:n_dirs_heads, :]
        e1, e2, e3 = _split3(d.T[0:n_dirs_heads, :])
        cs_t = _dot(e1, tri_t) + _dot(e2, tri_t) + _dot(e3, tri_t)
        total = cs_t[:, CHUNK - 1:CHUNK] if dirn == 0 else cs_t[:, 0:1]
        e_in_t = jnp.exp(cs_t)
        to_end_t = jnp.exp(total - cs_t) * dt_t
        dec_t = jnp.broadcast_to(jnp.exp(total), (n_dirs_heads, D_STATE))
        for g in range(SSD_GROUPS):
            bm = bcc_scr[rows, g * D_STATE:(g + 1) * D_STATE].astype(BF16)
            cm = bcc_scr[rows, (SSD_GROUPS + g) * D_STATE:(SSD_GROUPS + g + 1) * D_STATE]
            g_st = _dot_nt(bm, cm.astype(BF16))
            c_nt = cm.T
            st = s_scr[dirn, g]
            xs_parts, dec_parts = [], []
            for hh in range(HEADS_PER_GROUP):
                h = g * HEADS_PER_GROUP + hh
                r = row0 + h
                ch = slice(h * SSD_HEAD_DIM, (h + 1) * SSD_HEAD_DIM)
                x_t = xt_scr[c, ch, :]
                diff = cs_t[r:r + 1, :] - cs[:, r:r + 1]
                a_st = (g_st * jnp.exp(jnp.where(valid_st, diff, -jnp.inf))).astype(BF16)
                c_e = (c_nt * e_in_t[r:r + 1, :]).astype(BF16)
                x_dt = (x_t * dt_t[r:r + 1, :]).astype(BF16)
                s_h = st[hh * SSD_HEAD_DIM:(hh + 1) * SSD_HEAD_DIM, :].astype(BF16)
                y_h = _dot(jnp.concatenate([x_dt, s_h], axis=1), jnp.concatenate([a_st, c_e], axis=0))
                yt_scr[c, ch, :] = yt_scr[c, ch, :] + y_h
                xs_parts.append((x_t * to_end_t[r:r + 1, :]).astype(BF16))
                dec_parts.append(jnp.broadcast_to(dec_t[r:r + 1, :], (SSD_HEAD_DIM, D_STATE)))
            ds = _dot(jnp.concatenate(xs_parts, axis=0), bm)
            s_scr[dirn, g] = st * jnp.concatenate(dec_parts, axis=0) + ds

    def body(i, carry):
        scan_chunk(0, i)
        scan_chunk(1, nc - 1 - i)
        return carry

    lax.fori_loop(0, nc, body, 0)

    def finish(c, carry):
        rows = pl.ds(pl.multiple_of(c * CHUNK, CHUNK), CHUNK)
        y = yt_scr[c].T + xc_scr[rows, :] * dsk_ref[...]
        ya = y[:, 0:gw] * _silu(za_ref[rows, :])
        yb = y[:, gw:] * _silu(zb_ref[rows, :])
        ms = (jnp.sum(ya * ya, axis=-1, keepdims=True) + jnp.sum(yb * yb, axis=-1, keepdims=True)) / D_SSD
        inv = lax.rsqrt(ms + EPS)
        y_ref[rows, 0:gw] = (ya * inv * ng_ref[:, 0:gw]).astype(BF16)
        y_ref[rows, gw:] = (yb * inv * ng_ref[:, gw:]).astype(BF16)
        return carry

    lax.fori_loop(0, nc, finish, 0)

    if not has_init:
        if n_prev:
            sf_ref[0:n_prev] = psf_ref[...]
            sb_ref[0:n_prev] = psb_ref[...]
        for g in range(SSD_GROUPS):
            hs = slice(g * HEADS_PER_GROUP, (g + 1) * HEADS_PER_GROUP)
            sf_ref[n_prev, hs] = s_scr[0, g].reshape(HEADS_PER_GROUP, SSD_HEAD_DIM, D_STATE)
            sb_ref[n_prev, hs] = s_scr[1, g].reshape(HEADS_PER_GROUP, SSD_HEAD_DIM, D_STATE)


def _ssd(proj, dt_raw, p, n, nb, row_blk0, layer, init, prev_states, to_bf16=()):
    has_init = init is not None
    n_prev = 0 if has_init else layer
    kern = functools.partial(_ssd_kernel, n=n, has_init=has_init, n_prev=n_prev, n_cast=len(to_bf16))
    cast_in, cast_out, cast_shapes = _side_cast_specs(to_bf16, nb, lambda b: b)
    col = lambda cb: pl.BlockSpec((n, 512), lambda b: (row_blk0 + b, cb))
    vec = lambda r, w: pl.BlockSpec((r, w), lambda b: (0, 0))
    layers_spec = lambda k: pl.BlockSpec((None, k, SSD_HEADS, SSD_HEAD_DIM, D_STATE), lambda b: (b, 0, 0, 0, 0))
    in_specs = [
        col(3), col(4), col(5), col(6), col(7),
        pl.BlockSpec((n, 128), lambda b: (row_blk0 + b, 0)),
        vec(3, CONV_DIM), vec(1, CONV_DIM), vec(1, 128), vec(1, 128), vec(1, D_SSD), vec(1, D_SSD),
    ]
    args = [proj, proj, proj, proj, proj, dt_raw, p['conv_w'], p['conv_b'], p['dt_bias'], p['a_log'],
            p['d_skip'], p['ssd_norm_g']]
    y_spec = pl.BlockSpec((n, D_SSD), lambda b: (b, 0))
    y_shape = jax.ShapeDtypeStruct((nb * n, D_SSD), BF16)
    if has_init:
        init_spec = pl.BlockSpec((None, None, SSD_HEADS, SSD_HEAD_DIM, D_STATE), lambda b: (b, layer, 0, 0, 0))
        in_specs += [init_spec, init_spec]
        args += list(init)
        out_specs, out_shape = [y_spec], [y_shape]
    else:
        if n_prev:
            in_specs += [layers_spec(n_prev), layers_spec(n_prev)]
            args += list(prev_states)
        st_shape = jax.ShapeDtypeStruct((nb, layer + 1, SSD_HEADS, SSD_HEAD_DIM, D_STATE), F32)
        out_specs = [y_spec, layers_spec(layer + 1), layers_spec(layer + 1)]
        out_shape = [y_shape, st_shape, st_shape]
    return pl.pallas_call(
        kern,
        grid=(nb,),
        in_specs=in_specs + cast_in,
        out_specs=out_specs + cast_out,
        out_shape=out_shape + cast_shapes,
        scratch_shapes=[
            pltpu.VMEM((n, D_SSD), F32), pltpu.VMEM((n, 512), F32), pltpu.VMEM((n, 128), F32),
            pltpu.VMEM((n // CHUNK, D_SSD, CHUNK), F32), pltpu.VMEM((n // CHUNK, D_SSD, CHUNK), F32),
            pltpu.VMEM((2, SSD_GROUPS, 512, D_STATE), F32),
        ],
        compiler_params=_cparams(("parallel",)),
        name="ssd_sample" if has_init else "ssd_prompt",
    )(*args, *[part[0] for part in to_bf16])


def _seg_specs(segs, tm):
    specs, bounds, start = [], [], 0
    for a in segs:
        b0, nblk = start // tm, a.shape[0] // tm
        specs.append(pl.BlockSpec((tm, a.shape[1]), lambda i, b0=b0, nblk=nblk: (jnp.clip(i - b0, 0, nblk - 1), 0)))
        bounds.append(b0)
        start += a.shape[0]
    return specs, tuple(bounds)


def _seg_pick(refs, bounds):
    i = pl.program_id(0)
    v = refs[0][...]
    for ref, b0 in zip(refs[1:], bounds[1:]):
        v = jnp.where(i >= b0, ref[...], v)
    return v


def _outproj_kernel(*refs, with_router, o_bounds, y_bounds, x_bounds):
    refs = list(refs)
    o_refs = [refs.pop(0) for _ in o_bounds]
    y_refs = [refs.pop(0) for _ in y_bounds]
    x_refs = [refs.pop(0) for _ in x_bounds]
    if with_router:
        mod_ref, g_ref, w_ref, rw_ref, xo_ref, h_ref, meta_ref, cnt_ref, carry_scr = refs
    else:
        mod_ref, g_ref, w_ref, xo_ref, h_ref = refs
    tm = xo_ref.shape[0]
    a = jnp.concatenate([_seg_pick(o_refs, o_bounds), _seg_pick(y_refs, y_bounds)], axis=1)
    xn = _seg_pick(x_refs, x_bounds) + mod_ref[2:3, :] * _dot(a, w_ref[...])
    xo_ref[...] = xn
    h = _rms(xn, g_ref[...]) * (1.0 + mod_ref[4:5, :]) + mod_ref[3:4, :]
    h_ref[...] = h.astype(h_ref.dtype)
    if with_router:

        @pl.when(pl.program_id(0) == 0)
        def _():
            carry_scr[...] = jnp.zeros_like(carry_scr)

        h1, h2, _ = _split3(h)
        w1, w2, _ = _split3(rw_ref[...])
        logits = _dot_nt(w1, h1) + _dot_nt(w2, h1) + _dot_nt(w1, h2)
        row = lax.broadcasted_iota(jnp.int32, logits.shape, 0)
        logits = jnp.where(row < N_EXPERTS, logits, -jnp.inf)
        e = jnp.exp(logits - jnp.max(logits, axis=0, keepdims=True))
        probs = e / jnp.sum(e, axis=0, keepdims=True)
        p1 = jnp.max(probs, axis=0, keepdims=True)
        i1 = jnp.min(jnp.where(probs == p1, row, 16), axis=0, keepdims=True)
        rest = jnp.where(row == i1, -1.0, probs)
        p2 = jnp.max(rest, axis=0, keepdims=True)
        i2 = jnp.min(jnp.where(rest == p2, row, 16), axis=0, keepdims=True)
        hit1, hit2 = row == i1, row == i2
        onehot = jnp.where(hit1 | hit2, 1.0, 0.0)
        ti = lax.broadcasted_iota(jnp.int32, (tm, tm), 0)
        tj = lax.broadcasted_iota(jnp.int32, (tm, tm), 1)
        before = jnp.where(ti < tj, 1.0, 0.0).astype(BF16)
        rank = carry_scr[:, 0:1] + _dot(onehot.astype(BF16), before)
        r1 = jnp.sum(jnp.where(hit1, rank, 0.0), axis=0, keepdims=True)
        r2 = jnp.sum(jnp.where(hit2, rank, 0.0), axis=0, keepdims=True)
        carry_scr[...] = carry_scr[...] + jnp.sum(onehot, axis=1, keepdims=True)
        cnt_ref[...] = carry_scr[...]
        r8 = lax.broadcasted_iota(jnp.int32, (8, tm), 0)
        vals = [p1 / (p1 + p2), p2 / (p1 + p2), i1.astype(F32), i2.astype(F32), r1, r2]
        meta = jnp.zeros((8, tm), F32)
        for k, v in enumerate(vals):
            meta = jnp.where(r8 == k, v, meta)
        meta_ref[...] = meta


def _outproj(o_segs, y_segs, x_segs, mods, g, w_out, router_wt):
    tm = 512
    with_router = router_wt is not None
    o_specs, o_bounds = _seg_specs(o_segs, tm)
    y_specs, y_bounds = _seg_specs(y_segs, tm)
    x_specs, x_bounds = _seg_specs(x_segs, tm)
    kern = functools.partial(_outproj_kernel, with_router=with_router, o_bounds=o_bounds, y_bounds=y_bounds,
                             x_bounds=x_bounds)
    in_specs = o_specs + y_specs + x_specs + [
        pl.BlockSpec((None, 6, D_MODEL), lambda i: (_mod_group(i, tm), 0, 0)),
        pl.BlockSpec((1, D_MODEL), lambda i: (0, 0)),
        pl.BlockSpec((D_MODEL, D_MODEL), lambda i: (0, 0), pipeline_mode=pl.Buffered(1)),
    ]
    args = list(o_segs) + list(y_segs) + list(x_segs) + [mods, g, w_out]
    row_spec = pl.BlockSpec((tm, D_MODEL), lambda i: (i, 0))
    out_specs = [row_spec, row_spec]
    out_shape = [jax.ShapeDtypeStruct((T_ALL, D_MODEL), F32),
                 jax.ShapeDtypeStruct((T_ALL, D_MODEL), F32 if with_router else BF16)]
    scratch = []
    if with_router:
        in_specs.append(pl.BlockSpec((16, D_MODEL), lambda i: (0, 0)))
        args.append(router_wt)
        out_specs += [pl.BlockSpec((8, tm), lambda i: (0, i)), pl.BlockSpec((16, 128), lambda i: (0, 0))]
        out_shape += [jax.ShapeDtypeStruct((8, T_ALL), F32), jax.ShapeDtypeStruct((16, 128), F32)]
        scratch = [pltpu.VMEM((16, 128), F32)]
    return pl.pallas_call(
        kern,
        grid=(T_ALL // tm,),
        in_specs=in_specs,
        out_specs=out_specs,
        out_shape=out_shape,
        scratch_shapes=scratch,
        compiler_params=_cparams(("arbitrary",)),
        name="outproj_router" if with_router else "outproj",
    )(*args)


MOE_ROWS = 2 * T_ALL
MOE_TILE = 256
MOE_TILES = MOE_ROWS // MOE_TILE
MOE_VISITS = MOE_TILES + N_EXPERTS - 1


def _row_copy(src, s, dst, d, sem):
    return pltpu.make_async_copy(src.at[pl.ds(s, 1)], dst.at[pl.ds(d, 1)], sem)


def _dispatch_kernel(p1_ref, p2_ref, h_ref, xs_ref, sem):
    tm = h_ref.shape[0]

    def issue(r, c):
        _row_copy(h_ref, r, xs_ref, p1_ref[0, 0, r], sem.at[0]).start()
        _row_copy(h_ref, r, xs_ref, p2_ref[0, 0, r], sem.at[1]).start(priority=1)
        return c

    lax.fori_loop(0, tm, issue, 0, unroll=8)
    pltpu.make_async_copy(h_ref, xs_ref.at[pl.ds(0, tm)], sem.at[0]).wait()
    pltpu.make_async_copy(h_ref, xs_ref.at[pl.ds(0, tm)], sem.at[1]).wait()


def _dispatch(h, pos1, pos2):
    tm = 512
    nt = T_ALL // tm
    idx = lambda: pl.BlockSpec((1, 1, tm), lambda i: (i, 0, 0), memory_space=pltpu.SMEM)
    return pl.pallas_call(
        _dispatch_kernel,
        grid=(nt,),
        in_specs=[idx(), idx(), pl.BlockSpec((tm, D_MODEL), lambda i: (i, 0))],
        out_specs=pl.BlockSpec(memory_space=pl.ANY),
        out_shape=jax.ShapeDtypeStruct((MOE_ROWS, D_MODEL), F32),
        scratch_shapes=[pltpu.SemaphoreType.DMA((2,))],
        compiler_params=_cparams(("arbitrary",)),
        name="moe_dispatch",
    )(pos1.reshape(nt, 1, tm), pos2.reshape(nt, 1, tm), h)


def _experts_kernel(vt_ref, ve_ref, nv_ref, lo_ref, hi_ref, xs_ref, wg_ref, wu_ref, wd_ref, y_ref):
    v = pl.program_id(0)

    @pl.when(v < nv_ref[0])
    def _():
        e = ve_ref[v]
        x = xs_ref[...].astype(BF16)
        hid = _silu(_dot(x, wg_ref[...])) * _dot(x, wu_ref[...])
        y = _dot(hid.astype(BF16), wd_ref[...])
        row = vt_ref[v] * MOE_TILE + lax.broadcasted_iota(jnp.int32, (MOE_TILE, 1), 0)
        mine = (row >= lo_ref[e]) & (row < hi_ref[e])
        first_visit = (v == 0) | (vt_ref[jnp.maximum(v - 1, 0)] != vt_ref[v])

        @pl.when(first_visit)
        def _():
            y_ref[...] = jnp.where(mine, y, 0.0)

        @pl.when(jnp.logical_not(first_visit))
        def _():
            y_ref[...] = jnp.where(mine, y, y_ref[...])


def _experts(xs, wg, wu, wd, vt, ve, nv, lo, hi):
    grid_spec = pltpu.PrefetchScalarGridSpec(
        num_scalar_prefetch=5,
        grid=(MOE_VISITS,),
        in_specs=[
            pl.BlockSpec((MOE_TILE, D_MODEL), lambda v, vt, ve, nv, lo, hi: (vt[v], 0)),
            pl.BlockSpec((None, D_MODEL, F_EXPERT), lambda v, vt, ve, nv, lo, hi: (ve[v], 0, 0)),
            pl.BlockSpec((None, D_MODEL, F_EXPERT), lambda v, vt, ve, nv, lo, hi: (ve[v], 0, 0)),
            pl.BlockSpec((None, F_EXPERT, D_MODEL), lambda v, vt, ve, nv, lo, hi: (ve[v], 0, 0)),
        ],
        out_specs=pl.BlockSpec((MOE_TILE, D_MODEL), lambda v, vt, ve, nv, lo, hi: (vt[v], 0)),
    )
    return pl.pallas_call(
        _experts_kernel,
        grid_spec=grid_spec,
        out_shape=jax.ShapeDtypeStruct((MOE_ROWS, D_MODEL), F32),
        compiler_params=_cparams(("arbitrary",)),
        name="moe_experts",
    )(vt, ve, nv, lo, hi, xs, wg, wu, wd)


def _combine_kernel(p1c_ref, p2c_ref, p1n_ref, p2n_ref, y_hbm, x_ref, mod_ref, gate_ref, fg_ref,
                    outp_ref, outs_ref, ya_buf, yb_buf, sem):
    i = pl.program_id(0)
    n = pl.num_programs(0)
    tm = x_ref.shape[0]
    slot = i % 2

    def gather(pa_ref, pb_ref, s):
        def issue(r, c):
            _row_copy(y_hbm, pa_ref[0, 0, r], ya_buf.at[s], r, sem.at[0, s]).start()
            _row_copy(y_hbm, pb_ref[0, 0, r], yb_buf.at[s], r, sem.at[1, s]).start(priority=1)
            return c

        lax.fori_loop(0, tm, issue, 0, unroll=8)

    @pl.when(i == 0)
    def _():
        gather(p1c_ref, p2c_ref, 0)

    @pl.when(i + 1 < n)
    def _():
        gather(p1n_ref, p2n_ref, 1 - slot)

    pltpu.make_async_copy(y_hbm.at[pl.ds(0, tm)], ya_buf.at[slot], sem.at[0, slot]).wait()
    pltpu.make_async_copy(y_hbm.at[pl.ds(0, tm)], yb_buf.at[slot], sem.at[1, slot]).wait()
    g = gate_ref[...]
    mix = g[:, 0:1] * ya_buf[slot] + g[:, 1:2] * yb_buf[slot]
    xo = _rms(x_ref[...] + mod_ref[5:6, :] * mix, fg_ref[...])

    @pl.when(i < T_PROMPT // tm)
    def _():
        outp_ref[...] = xo

    @pl.when(i >= T_PROMPT // tm)
    def _():
        outs_ref[...] = xo


def _combine(y, x, mods, gate_cols, pos1, pos2, final_g):
    tm = 256
    nt = T_ALL // tm
    ntp = T_PROMPT // tm
    cur = lambda: pl.BlockSpec((1, 1, tm), lambda i: (i, 0, 0), memory_space=pltpu.SMEM)
    nxt = lambda: pl.BlockSpec((1, 1, tm), lambda i: (jnp.minimum(i + 1, nt - 1), 0, 0), memory_space=pltpu.SMEM)
    p1, p2 = pos1.reshape(nt, 1, tm), pos2.reshape(nt, 1, tm)
    return pl.pallas_call(
        _combine_kernel,
        grid=(nt,),
        in_specs=[
            cur(), cur(), nxt(), nxt(),
            pl.BlockSpec(memory_space=pl.ANY),
            pl.BlockSpec((tm, D_MODEL), lambda i: (i, 0)),
            pl.BlockSpec((None, 6, D_MODEL), lambda i: (_mod_group(i, tm), 0, 0)),
            pl.BlockSpec((tm, 128), lambda i: (i, 0)),
            pl.BlockSpec((1, D_MODEL), lambda i: (0, 0)),
        ],
        out_specs=[
            pl.BlockSpec((tm, D_MODEL), lambda i: (jnp.minimum(i, ntp - 1), 0)),
            pl.BlockSpec((tm, D_MODEL), lambda i: (jnp.maximum(i - ntp, 0), 0)),
        ],
        out_shape=[
            jax.ShapeDtypeStruct((T_PROMPT, D_MODEL), F32), jax.ShapeDtypeStruct((T_SAMPLE, D_MODEL), F32),
        ],
        scratch_shapes=[
            pltpu.VMEM((2, tm, D_MODEL), F32), pltpu.VMEM((2, tm, D_MODEL), F32),
            pltpu.SemaphoreType.DMA((2, 2)),
        ],
        compiler_params=_cparams(("arbitrary",)),
        name="moe_combine",
    )(p1, p2, p1, p2, y, x, mods, gate_cols, final_g)


def _route_plan(meta, counts):
    i1, i2 = meta[2].astype(jnp.int32), meta[3].astype(jnp.int32)
    r1, r2 = meta[4].astype(jnp.int32), meta[5].astype(jnp.int32)
    cnt = counts[:N_EXPERTS, 0].astype(jnp.int32)
    hi = jnp.cumsum(cnt)
    lo = hi - cnt
    ex = jnp.arange(N_EXPERTS, dtype=jnp.int32)
    pos1 = jnp.sum(jnp.where(i1[:, None] == ex[None, :], lo[None, :], 0), axis=1) + r1
    pos2 = jnp.sum(jnp.where(i2[:, None] == ex[None, :], lo[None, :], 0), axis=1) + r2
    first_tile = lo // MOE_TILE
    n_vis_e = jnp.where(cnt > 0, (hi - 1) // MOE_TILE - first_tile + 1, 0)
    vis_hi = jnp.cumsum(n_vis_e)
    vis_lo = vis_hi - n_vis_e
    nv = vis_hi[-1]
    v = jnp.minimum(jnp.arange(MOE_VISITS, dtype=jnp.int32), nv - 1)
    ve = jnp.minimum(jnp.sum(v[:, None] >= vis_hi[None, :], axis=1), N_EXPERTS - 1).astype(jnp.int32)
    pick = lambda tab: jnp.sum(jnp.where(ve[:, None] == ex[None, :], tab[None, :], 0), axis=1)
    vt = (pick(first_tile) + v - pick(vis_lo)).astype(jnp.int32)
    return pos1, pos2, vt, ve, nv.reshape(1).astype(jnp.int32), lo.astype(jnp.int32), hi.astype(jnp.int32)


def _ffn_kernel(*refs, n_cast):
    h_ref, x_hbm, mod_ref, wg_ref, wu_ref, wd_ref = refs[:6]
    cast_in = refs[6:6 + n_cast]
    out_ref = refs[6 + n_cast]
    cast_out = refs[7 + n_cast:7 + 2 * n_cast]
    x_buf, sem = refs[7 + 2 * n_cast:]
    _side_cast(cast_in, cast_out)
    i, f = pl.program_id(0), pl.program_id(1)
    tm = h_ref.shape[0]
    x_copy = pltpu.make_async_copy(x_hbm.at[pl.ds(pl.multiple_of(i * tm, tm), tm)], x_buf, sem)

    @pl.when(f == 0)
    def _():
        x_copy.start()
        out_ref[...] = jnp.zeros_like(out_ref)

    h = h_ref[...]
    hid = _silu(_dot(h, wg_ref[...])) * _dot(h, wu_ref[...])
    out_ref[...] += _dot(hid.astype(BF16), wd_ref[...])

    @pl.when(f == pl.num_programs(1) - 1)
    def _():
        x_copy.wait()
        out_ref[...] = x_buf[...] + mod_ref[5:6, :] * out_ref[...]


FFN_CAST_STEPS = 64


def _ffn(h, x, mods, wg, wu, wd, to_bf16):
    tm, tf = 1024, 512
    nf = F_DENSE // tf
    assert (T_ALL // tm) * nf >= FFN_CAST_STEPS

    cast_in, cast_out, cast_shapes = _side_cast_specs(to_bf16, FFN_CAST_STEPS, lambda i, f: i * nf + f)
    outs = pl.pallas_call(
        functools.partial(_ffn_kernel, n_cast=len(to_bf16)),
        grid=(T_ALL // tm, nf),
        in_specs=[
            pl.BlockSpec((tm, D_MODEL), lambda i, f: (i, 0)),
            _ANY,
            pl.BlockSpec((None, 6, D_MODEL), lambda i, f: (_mod_group(i, tm), 0, 0)),
            pl.BlockSpec((D_MODEL, tf), lambda i, f: (0, f)),
            pl.BlockSpec((D_MODEL, tf), lambda i, f: (0, f)),
            pl.BlockSpec((tf, D_MODEL), lambda i, f: (f, 0)),
        ] + cast_in,
        out_specs=[pl.BlockSpec((tm, D_MODEL), lambda i, f: (i, 0))] + cast_out,
        out_shape=[jax.ShapeDtypeStruct((T_ALL, D_MODEL), F32)] + cast_shapes,
        scratch_shapes=[pltpu.VMEM((tm, D_MODEL), F32), pltpu.SemaphoreType.DMA(())],
        compiler_params=_cparams(("arbitrary", "arbitrary")),
        name="dense_ffn",
    )(h, x, mods, wg, wu, wd, *[part[0] for part in to_bf16])
    return outs[0], outs[1:]


def _rope_tables():
    n = DEC_SEQ
    rows = n // GRID_W
    t_row = jnp.repeat(jnp.arange(rows, dtype=F32), GRID_W)
    t_col = jnp.tile(jnp.arange(GRID_W, dtype=F32), rows)
    inv = 1.0 / (ROPE_THETA ** (jnp.arange(0, ROT_HALF, 2, dtype=F32) / ROT_HALF))
    ar, ac = t_row[:, None] * inv, t_col[:, None] * inv
    cos = jnp.concatenate([jnp.cos(ar), jnp.cos(ar), jnp.cos(ac), jnp.cos(ac)], axis=-1)
    sin_signed = jnp.concatenate([-jnp.sin(ar), jnp.sin(ar), -jnp.sin(ac), jnp.sin(ac)], axis=-1)
    return cos, sin_signed


def _pad_lanes(v, width=128):
    return jnp.pad(v, ((0, 0), (0, width - v.shape[-1])))


def kernel(x_prompt, x_sample, c, cache_k, cache_v, state_ssm_fwd, state_ssm_bwd, c_ctx, ada_w, ada_b, norm1_g, norm2_g, w_in, q_norm_g, k_norm_g, conv_w, conv_b, a_log_fwd, a_log_bwd, dt_bias_fwd, dt_bias_bwd, d_skip, ssd_norm_g, attn_out_g, w_out, ffn_w_gate, ffn_w_up, ffn_w_down, router_w, moe_w_gate, moe_w_up, moe_w_down, final_norm_g):
    assert DEPTH % 2 == 0
    cond = jnp.concatenate([c_ctx[None, :], c, jnp.zeros((N_COND - 1 - DEC_BATCH, D_MODEL), F32)], axis=0)
    mods_all = _ada_mods(cond, ada_w, ada_b).reshape(DEPTH, N_COND, 6, D_MODEL)
    cos, sin_signed = _rope_tables()

    w_in_t = jnp.swapaxes(w_in, 1, 2)
    x_segs = [x_prompt.reshape(T_PROMPT, D_MODEL), x_sample.reshape(T_SAMPLE, D_MODEL)]
    kv, states = None, None
    for l in range(DEPTH):
        mods = mods_all[l]
        proj, dt_raw, (w_o,) = _inproj(x_segs, mods, norm1_g[l][None, :], w_in_t, l, [(w_out.reshape(DEPTH * D_MODEL, D_MODEL), l * D_MODEL, D_MODEL)])

        qg, kg, og = q_norm_g[l][None, :], k_norm_g[l][None, :], attn_out_g[l][None, :]
        o_p, k_all, v_all = _attention_prompt(proj, 0, qg, kg, og, l, kv)
        kv = (k_all, v_all)
        o_s = _attention_sample(proj, T_PROMPT, qg, kg, og, cache_k, cache_v, cos, sin_signed, l)

        p = {
            'conv_w': conv_w[l], 'conv_b': conv_b[l][None, :],
            'dt_bias': _pad_lanes(jnp.concatenate([dt_bias_fwd[l], dt_bias_bwd[l]])[None, :]),
            'a_log': _pad_lanes(jnp.concatenate([a_log_fwd[l], a_log_bwd[l]])[None, :]),
            'd_skip': jnp.repeat(d_skip[l], SSD_HEAD_DIM)[None, :],
            'ssd_norm_g': ssd_norm_g[l][None, :],
        }
        j = l // 2
        n_up, n_down = N_EXPERTS * D_MODEL, N_EXPERTS * F_EXPERT
        if l % 2 == 0:
            parts = [(ffn_w_gate.reshape(-1, F_DENSE), j * D_MODEL, D_MODEL),
                     (ffn_w_up.reshape(-1, F_DENSE), j * D_MODEL, D_MODEL),
                     (ffn_w_down.reshape(-1, D_MODEL), j * F_DENSE, F_DENSE)]
        else:
            parts = [(moe_w_down.reshape(-1, D_MODEL), j * n_down, n_down)]
        y_p, sf, sb, *mixer_w = _ssd(proj, dt_raw, p, SEQ, BATCH, 0, l, None, states, parts)
        states = (sf, sb)
        y_s, = _ssd(proj, dt_raw, p, DEC_SEQ, DEC_BATCH, T_PROMPT // DEC_SEQ, l, (state_ssm_fwd, state_ssm_bwd), None)

        g2 = norm2_g[l][None, :]
        if l % 2 == 0:
            x, h = _outproj([o_p, o_s], [y_p, y_s], x_segs, mods, g2, w_o, None)
            x, (eg, eu) = _ffn(h, x, mods, *mixer_w, [
                (moe_w_gate.reshape(-1, F_EXPERT), j * n_up, n_up),
                (moe_w_up.reshape(-1, F_EXPERT), j * n_up, n_up)])
            x_segs = [x]
        else:
            router_wt = jnp.pad(router_w[j].T, ((0, 16 - N_EXPERTS), (0, 0)))
            x, h, meta, counts = _outproj([o_p, o_s], [y_p, y_s], x_segs, mods, g2, w_o, router_wt)
            pos1, pos2, vt, ve, nv, lo, hi = _route_plan(meta, counts)
            xs = _dispatch(h, pos1, pos2)
            ys = _experts(xs, eg.reshape(N_EXPERTS, D_MODEL, F_EXPERT), eu.reshape(N_EXPERTS, D_MODEL, F_EXPERT),
                          mixer_w[0].reshape(N_EXPERTS, F_EXPERT, D_MODEL), vt, ve, nv, lo, hi)
            y_prompt, y_sample = _combine(ys, x, mods, _pad_lanes(meta[:2].T), pos1, pos2, final_norm_g[None, :])

    return (y_prompt.reshape(BATCH, SEQ, D_MODEL), y_sample.reshape(DEC_BATCH, DEC_SEQ, D_MODEL),
            kv[0], kv[1], states[0], states[1])
```

```python
import functools

import jax
import jax.numpy as jnp
from jax import lax
from jax.experimental import pallas as pl
from jax.experimental.pallas import tpu as pltpu

F32 = jnp.float32
BF16 = jnp.bfloat16

D_MODEL = 2048
BATCH = 16
SEQ = 256
DEPTH = 2
DEC_BATCH = 2
DEC_SEQ = 1024
PAST_LEN = 512
GRID_W = 64
D_ATTN = 1024
D_SSD = 1024
HEAD_DIM = 128
N_Q_HEADS = 8
N_KV_HEADS = 2
Q_PER_KV = 4
KV_DIM = 256
ROT_HALF = 64
ROPE_THETA = 10000.0
SSD_HEAD_DIM = 64
SSD_HEADS = 16
SSD_GROUPS = 2
HEADS_PER_GROUP = 8
D_STATE = 128
CONV_DIM = 1536
CHUNK = 128
N_MAIN = 4096
F_DENSE = 5632
N_EXPERTS = 8
F_EXPERT = 1024
EPS = 1e-6

T_PROMPT = BATCH * SEQ
T_SAMPLE = DEC_BATCH * DEC_SEQ
T_ALL = T_PROMPT + T_SAMPLE
N_COND = 16

VMEM_LIMIT = 58 * 1024 * 1024


def _cparams(sem):
    return pltpu.CompilerParams(dimension_semantics=sem, vmem_limit_bytes=VMEM_LIMIT)


def _mod_group(i, tm):
    return jnp.maximum(0, (i * tm - T_PROMPT + DEC_SEQ) // DEC_SEQ)


def _silu(x):
    return x * jax.nn.sigmoid(x)


def _rms(x, g):
    ms = jnp.mean(x * x, axis=-1, keepdims=True)
    return x * lax.rsqrt(ms + EPS) * g


def _dot(a, b):
    return jnp.dot(a, b, preferred_element_type=F32)


def _dot_nt(a, b):
    return lax.dot_general(a, b, (((1,), (1,)), ((), ())), preferred_element_type=F32)


def _split3(x):
    hi = x.astype(BF16)
    r1 = x - hi.astype(F32)
    mid = r1.astype(BF16)
    r2 = r1 - mid.astype(F32)
    return hi, mid, r2.astype(BF16)


def _ada_kernel(c_ref, w_ref, b_ref, o_ref):
    s = _silu(c_ref[...]).astype(BF16)
    o_ref[...] = _dot(s, w_ref[...].astype(BF16)) + b_ref[...]


def _ada_mods(cond, ada_w, ada_b):
    tn = 1024
    n_out = 6 * D_MODEL
    return pl.pallas_call(
        _ada_kernel,
        grid=(DEPTH, n_out // tn),
        in_specs=[
            pl.BlockSpec((N_COND, D_MODEL), lambda l, j: (0, 0)),
            pl.BlockSpec((None, D_MODEL, tn), lambda l, j: (l, 0, j)),
            pl.BlockSpec((None, 1, tn), lambda l, j: (l, 0, j)),
        ],
        out_specs=pl.BlockSpec((None, N_COND, tn), lambda l, j: (l, 0, j)),
        out_shape=jax.ShapeDtypeStruct((DEPTH, N_COND, n_out), F32),
        compiler_params=_cparams(("parallel", "parallel")),
        name="ada_mods",
    )(cond, ada_w, ada_b.reshape(DEPTH, 1, n_out))


def _side_cast_specs(parts, n_steps, linear_step):
    in_specs, out_specs, out_shapes = [], [], []
    for a, row0, nrows in parts:
        rows = nrows // n_steps
        blk0 = row0 // rows
        step = lambda *ids: jnp.minimum(linear_step(*ids), n_steps - 1)
        in_specs.append(pl.BlockSpec((rows, a.shape[1]), lambda *ids, blk0=blk0: (blk0 + step(*ids), 0)))
        out_specs.append(pl.BlockSpec((rows, a.shape[1]), lambda *ids: (step(*ids), 0)))
        out_shapes.append(jax.ShapeDtypeStruct((nrows, a.shape[1]), BF16))
    return in_specs, out_specs, out_shapes


def _side_cast(cast_in, cast_out):
    for src, dst in zip(cast_in, cast_out):
        dst[...] = src[...].astype(BF16)


INPROJ_NORM_ROWS = 1024
INPROJ_CAST_STEPS = 16


def _inproj_kernel(*refs, seg_rows, n_cast):
    n_seg = len(seg_rows)
    x_hbms = refs[:n_seg]
    mods_ref, g_ref, w_ref, wdt_ref = refs[n_seg:n_seg + 4]
    cast_in = refs[n_seg + 4:n_seg + 4 + n_cast]
    proj_ref, dt_ref = refs[n_seg + 4 + n_cast:n_seg + 6 + n_cast]
    cast_out = refs[n_seg + 6 + n_cast:n_seg + 6 + 2 * n_cast]
    x_buf, h_scr, sem = refs[n_seg + 6 + 2 * n_cast:]
    _side_cast(cast_in, cast_out)
    i, j = pl.program_id(0), pl.program_id(1)
    tm = x_buf.shape[0]

    def fetch(tile):
        start = 0
        for x_hbm, nrows in zip(x_hbms, seg_rows):
            b0, nb = start // tm, nrows // tm
            start += nrows

            @pl.when((tile >= b0) & (tile < b0 + nb))
            def _():
                r0 = pl.multiple_of((tile - b0) * tm, tm)
                pltpu.make_async_copy(x_hbm.at[pl.ds(r0, tm)], x_buf, sem).start()

    @pl.when(j == 0)
    def _():
        @pl.when(i == 0)
        def _():
            fetch(i)

        pltpu.make_async_copy(x_hbms[0].at[pl.ds(0, tm)], x_buf, sem).wait()
        for k in range(tm // INPROJ_NORM_ROWS):
            rows = slice(k * INPROJ_NORM_ROWS, (k + 1) * INPROJ_NORM_ROWS)
            mod = mods_ref[_mod_group(i * (tm // INPROJ_NORM_ROWS) + k, INPROJ_NORM_ROWS)]
            h = _rms(x_buf[rows, :], g_ref[...]) * (1.0 + mod[1:2, :]) + mod[0:1, :]
            h_scr[rows, :] = h.astype(BF16)

        @pl.when(i + 1 < pl.num_programs(0))
        def _():
            fetch(i + 1)

        n_dt = wdt_ref.shape[0]
        wdt = jnp.concatenate([wdt_ref[...], jnp.zeros((128 - n_dt, D_MODEL), F32)], axis=0)
        dt_ref[...] = _dot_nt(h_scr[...], wdt.astype(BF16))

    proj_ref[...] = _dot_nt(h_scr[...], w_ref[...].astype(BF16))


_ANY = pl.BlockSpec(memory_space=pl.ANY)


def _inproj(x_segs, mods, g, w_in_t, layer, to_bf16):
    tm, tn = 2048, 512
    n_dt = w_in_t.shape[1] - N_MAIN
    nj = N_MAIN // tn
    seg_rows = tuple(a.shape[0] for a in x_segs)
    assert all(r % tm == 0 for r in seg_rows) and sum(seg_rows) == T_ALL
    assert (T_ALL // tm) * nj >= INPROJ_CAST_STEPS
    cast_in, cast_out, cast_shapes = _side_cast_specs(to_bf16, INPROJ_CAST_STEPS, lambda i, j: i * nj + j)
    outs = pl.pallas_call(
        functools.partial(_inproj_kernel, seg_rows=seg_rows, n_cast=len(to_bf16)),
        grid=(T_ALL // tm, nj),
        in_specs=[_ANY] * len(x_segs) + [
            pl.BlockSpec(mods.shape, lambda i, j: (0, 0, 0)),
            pl.BlockSpec((1, D_MODEL), lambda i, j: (0, 0)),
            pl.BlockSpec((None, tn, D_MODEL), lambda i, j: (layer, j, 0)),
            pl.BlockSpec((None, n_dt, D_MODEL), lambda i, j: (layer, N_MAIN // n_dt, 0)),
        ] + cast_in,
        out_specs=[
            pl.BlockSpec((tm, tn), lambda i, j: (i, j)),
            pl.BlockSpec((tm, 128), lambda i, j: (i, 0)),
        ] + cast_out,
        out_shape=[
            jax.ShapeDtypeStruct((T_ALL, N_MAIN), F32),
            jax.ShapeDtypeStruct((T_ALL, 128), F32),
        ] + cast_shapes,
        scratch_shapes=[pltpu.VMEM((tm, D_MODEL), F32), pltpu.VMEM((tm, D_MODEL), BF16),
                        pltpu.SemaphoreType.DMA(())],
        compiler_params=_cparams(("arbitrary", "arbitrary")),
        name="inproj",
    )(*x_segs, mods, g, w_in_t, w_in_t, *[part[0] for part in to_bf16])
    return outs[0], outs[1], outs[2:]


def _rope(x, cos, sin_signed):
    lane = lax.broadcasted_iota(jnp.int32, x.shape, 1)
    first = (lane // (ROT_HALF // 2)) % 2 == 0
    swapped = jnp.where(first, pltpu.roll(x, HEAD_DIM - ROT_HALF // 2, 1), pltpu.roll(x, ROT_HALF // 2, 1))
    return x * cos + swapped * sin_signed


def _attn_kernel(*refs, nk_new, has_ctx, n_prev):
    if has_ctx:
        (q_ref, kv_ref, qg_ref, kg_ref, og_ref, ck_ref, cv_ref, cq_ref, sq_ref, ckk_ref, skk_ref,
         o_ref, kb_scr, vb_scr, o_scr) = refs
    elif n_prev:
        (q_ref, kv_ref, qg_ref, kg_ref, og_ref, pk_ref, pv_ref, o_ref, ko_ref, vo_ref,
         kb_scr, vb_scr, o_scr) = refs
    else:
        (q_ref, kv_ref, qg_ref, kg_ref, og_ref, o_ref, ko_ref, vo_ref, kb_scr, vb_scr, o_scr) = refs

    @pl.when(pl.program_id(1) == 0)
    def _():
        if not has_ctx and n_prev:
            ko_ref[0:n_prev] = pk_ref[...]
            vo_ref[0:n_prev] = pv_ref[...]
        for g in range(N_KV_HEADS):
            sl = slice(g * HEAD_DIM, (g + 1) * HEAD_DIM)
            kn = _rms(kv_ref[:, sl], kg_ref[...])
            v = kv_ref[:, KV_DIM + g * HEAD_DIM:KV_DIM + (g + 1) * HEAD_DIM]
            vsl = slice(2 * g * HEAD_DIM, (2 * g + 1) * HEAD_DIM)
            vb_scr[0:nk_new, vsl] = v.astype(BF16)
            vb_scr[:, (2 * g + 1) * HEAD_DIM:(2 * g + 2) * HEAD_DIM] = jnp.ones((vb_scr.shape[0], HEAD_DIM), BF16)
            if has_ctx:
                kb_scr[0:nk_new, sl] = _rope(kn, ckk_ref[...], skk_ref[...]).astype(BF16)
                kb_scr[nk_new:, sl] = ck_ref[:, g, :].astype(BF16)
                vb_scr[nk_new:, vsl] = cv_ref[:, g, :].astype(BF16)
            else:
                kb_scr[:, sl] = kn.astype(BF16)
                ko_ref[n_prev, :, g, :] = kn
                vo_ref[n_prev, :, g, :] = v

    scale_log2e = HEAD_DIM ** -0.5 * 1.4426950408889634
    for h in range(N_Q_HEADS):
        g = h // Q_PER_KV
        sl = slice(h * HEAD_DIM, (h + 1) * HEAD_DIM)
        gsl = slice(g * HEAD_DIM, (g + 1) * HEAD_DIM)
        qn = _rms(q_ref[:, sl], qg_ref[...])
        if has_ctx:
            qn = _rope(qn, cq_ref[...], sq_ref[...])
        s = _dot_nt((qn * scale_log2e).astype(BF16), kb_scr[:, gsl])
        e = jnp.exp2(s - jnp.max(s, axis=-1, keepdims=True))
        pv = _dot(e.astype(BF16), vb_scr[:, 2 * g * HEAD_DIM:(2 * g + 2) * HEAD_DIM])
        o_scr[:, sl] = pv[:, :HEAD_DIM] / pv[:, HEAD_DIM:]
    o_ref[...] = _rms(o_scr[...], og_ref[...]).astype(BF16)


def _attention_prompt(proj, row0, qg, kg, og, layer, prev_kv):
    n = SEQ
    blk0 = row0 // n
    kern = functools.partial(_attn_kernel, nk_new=n, has_ctx=False, n_prev=layer)
    vec = lambda w: pl.BlockSpec((1, w), lambda b, i: (0, 0))
    cache_spec = lambda k: pl.BlockSpec((None, k, n, N_KV_HEADS, HEAD_DIM), lambda b, i: (b, 0, 0, 0, 0))
    cache_shape = jax.ShapeDtypeStruct((BATCH, layer + 1, n, N_KV_HEADS, HEAD_DIM), F32)
    in_specs = [
        pl.BlockSpec((n, D_ATTN), lambda b, i: (blk0 + b, 0)),
        pl.BlockSpec((n, 2 * KV_DIM), lambda b, i: (blk0 + b, 2)),
        vec(HEAD_DIM), vec(HEAD_DIM), vec(D_ATTN),
    ]
    args = [proj, proj, qg, kg, og]
    if layer:
        in_specs += [cache_spec(layer), cache_spec(layer)]
        args += list(prev_kv)
    return pl.pallas_call(
        kern,
        grid=(BATCH, 1),
        in_specs=in_specs,
        out_specs=[pl.BlockSpec((n, D_ATTN), lambda b, i: (b, 0)), cache_spec(layer + 1), cache_spec(layer + 1)],
        out_shape=[jax.ShapeDtypeStruct((T_PROMPT, D_ATTN), BF16), cache_shape, cache_shape],
        scratch_shapes=[
            pltpu.VMEM((n, KV_DIM), BF16), pltpu.VMEM((n, 2 * KV_DIM), BF16), pltpu.VMEM((n, D_ATTN), F32),
        ],
        compiler_params=_cparams(("parallel", "arbitrary")),
        name="attn_prompt",
    )(*args)


def _attention_sample(proj, row0, qg, kg, og, ck, cv, cos, sin_signed, layer):
    n, tq = DEC_SEQ, 512
    nq = n // tq
    nk = n + PAST_LEN
    kern = functools.partial(_attn_kernel, nk_new=n, has_ctx=True, n_prev=0)
    vec = lambda w: pl.BlockSpec((1, w), lambda b, i: (0, 0))
    q_blk0 = row0 // tq
    kv_blk0 = row0 // n
    return pl.pallas_call(
        kern,
        grid=(DEC_BATCH, nq),
        in_specs=[
            pl.BlockSpec((tq, D_ATTN), lambda b, i: (q_blk0 + b * nq + i, 0)),
            pl.BlockSpec((n, 2 * KV_DIM), lambda b, i: (kv_blk0 + b, 2)),
            vec(HEAD_DIM), vec(HEAD_DIM), vec(D_ATTN),
            pl.BlockSpec((None, None, PAST_LEN, N_KV_HEADS, HEAD_DIM), lambda b, i: (b, layer, 0, 0, 0)),
            pl.BlockSpec((None, None, PAST_LEN, N_KV_HEADS, HEAD_DIM), lambda b, i: (b, layer, 0, 0, 0)),
            pl.BlockSpec((tq, HEAD_DIM), lambda b, i: (i, 0)),
            pl.BlockSpec((tq, HEAD_DIM), lambda b, i: (i, 0)),
            pl.BlockSpec((n, HEAD_DIM), lambda b, i: (0, 0)),
            pl.BlockSpec((n, HEAD_DIM), lambda b, i: (0, 0)),
        ],
        out_specs=pl.BlockSpec((tq, D_ATTN), lambda b, i: (b * nq + i, 0)),
        out_shape=jax.ShapeDtypeStruct((T_SAMPLE, D_ATTN), BF16),
        scratch_shapes=[
            pltpu.VMEM((nk, KV_DIM), BF16), pltpu.VMEM((nk, 2 * KV_DIM), BF16), pltpu.VMEM((tq, D_ATTN), F32),
        ],
        compiler_params=_cparams(("parallel", "arbitrary")),
        name="attn_sample",
    )(proj, proj, qg, kg, og, ck, cv, cos, sin_signed, cos, sin_signed)


def _conv_silu(x, w, b):
    n = x.shape[0]
    row = lax.broadcasted_iota(jnp.int32, (n, 1), 0)
    prev = jnp.where(row == 0, 0.0, pltpu.roll(x, 1, 0))
    nxt = jnp.where(row == n - 1, 0.0, pltpu.roll(x, n - 1, 0))
    return _silu(prev * w[0:1, :] + x * w[1:2, :] + nxt * w[2:3, :] + b)


def _softplus(x):
    return jnp.maximum(x, 0.0) + jnp.log1p(jnp.exp(-jnp.abs(x)))


def _ssd_kernel(*refs, n, has_init, n_prev, n_cast):
    refs = list(refs)
    (za_ref, zb_ref, xa_ref, xb_ref, bc_ref, dt_ref, cw_ref, cb_ref, dtb_ref, alog_ref, dsk_ref,
     ng_ref) = refs[:12]
    del refs[:12]
    if has_init:
        sf0_ref, sb0_ref = refs.pop(0), refs.pop(0)
    elif n_prev:
        psf_ref, psb_ref = refs.pop(0), refs.pop(0)
    cast_in = [refs.pop(0) for _ in range(n_cast)]
    y_ref = refs.pop(0)
    if not has_init:
        sf_ref, sb_ref = refs.pop(0), refs.pop(0)
    cast_out = [refs.pop(0) for _ in range(n_cast)]
    xc_scr, bcc_scr, dts_scr, xt_scr, yt_scr, s_scr = refs
    _side_cast(cast_in, cast_out)
    nc = n // CHUNK
    gw = HEADS_PER_GROUP * SSD_HEAD_DIM

    xc_scr[:, 0:gw] = _conv_silu(xa_ref[...], cw_ref[:, 0:gw], cb_ref[:, 0:gw])
    xc_scr[:, gw:] = _conv_silu(xb_ref[...], cw_ref[:, gw:2 * gw], cb_ref[:, gw:2 * gw])
    bcc_scr[...] = _conv_silu(bc_ref[...], cw_ref[:, 2 * gw:], cb_ref[:, 2 * gw:])
    dts_scr[...] = _softplus(dt_ref[...] + dtb_ref[...])
    for g in range(SSD_GROUPS):
        hs = slice(g * HEADS_PER_GROUP, (g + 1) * HEADS_PER_GROUP)
        if has_init:
            s_scr[0, g] = sf0_ref[hs].reshape(gw, D_STATE)
            s_scr[1, g] = sb0_ref[hs].reshape(gw, D_STATE)
        else:
            s_scr[0, g] = jnp.zeros((gw, D_STATE), F32)
            s_scr[1, g] = jnp.zeros((gw, D_STATE), F32)

    def to_channel_major(c, carry):
        rows = pl.ds(pl.multiple_of(c * CHUNK, CHUNK), CHUNK)
        xt_scr[c] = xc_scr[rows, :].T
        yt_scr[c] = jnp.zeros((D_SSD, CHUNK), F32)
        return carry

    lax.fori_loop(0, nc, to_channel_major, 0)

    a_row = -jnp.exp(alog_ref[...])
    ri = lax.broadcasted_iota(jnp.int32, (CHUNK, CHUNK), 0)
    ci = lax.broadcasted_iota(jnp.int32, (CHUNK, CHUNK), 1)
    lower, upper = ci <= ri, ci >= ri
    n_dirs_heads = 2 * SSD_HEADS

    def scan_chunk(dirn, c):
        tri = (lower if dirn == 0 else upper).astype(BF16)
        valid_st = upper if dirn == 0 else lower
        tri_t = valid_st.astype(BF16)
        row0 = dirn * SSD_HEADS
        rows = pl.ds(pl.multiple_of(c * CHUNK, CHUNK), CHUNK)
        dt = dts_scr[rows, :]
        d = dt * a_row
        d1, d2, d3 = _split3(d)
        cs = _dot(tri, d1) + _dot(tri, d2) + _dot(tri, d3)
        dt_t = dt.T[0:n_dirs_heads, :]
        e1, e2, e3 = _split3(d.T[0:n_dirs_heads, :])
        cs_t = _dot(e1, tri_t) + _dot(e2, tri_t) + _dot(e3, tri_t)
        total = cs_t[:, CHUNK - 1:CHUNK] if dirn == 0 else cs_t[:, 0:1]
        e_in_t = jnp.exp(cs_t)
        to_end_t = jnp.exp(total - cs_t) * dt_t
        dec_t = jnp.broadcast_to(jnp.exp(total), (n_dirs_heads, D_STATE))
        for g in range(SSD_GROUPS):
            bm = bcc_scr[rows, g * D_STATE:(g + 1) * D_STATE].astype(BF16)
            cm = bcc_scr[rows, (SSD_GROUPS + g) * D_STATE:(SSD_GROUPS + g + 1) * D_STATE]
            g_st = _dot_nt(bm, cm.astype(BF16))
            c_nt = cm.T
            st = s_scr[dirn, g]
            xs_parts, dec_parts = [], []
            for hh in range(HEADS_PER_GROUP):
                h = g * HEADS_PER_GROUP + hh
                r = row0 + h
                ch = slice(h * SSD_HEAD_DIM, (h + 1) * SSD_HEAD_DIM)
                x_t = xt_scr[c, ch, :]
                diff = cs_t[r:r + 1, :] - cs[:, r:r + 1]
                a_st = (g_st * jnp.exp(jnp.where(valid_st, diff, -jnp.inf))).astype(BF16)
                c_e = (c_nt * e_in_t[r:r + 1, :]).astype(BF16)
                x_dt = (x_t * dt_t[r:r + 1, :]).astype(BF16)
                s_h = st[hh * SSD_HEAD_DIM:(hh + 1) * SSD_HEAD_DIM, :].astype(BF16)
                y_h = _dot(jnp.concatenate([x_dt, s_h], axis=1), jnp.concatenate([a_st, c_e], axis=0))
                yt_scr[c, ch, :] = yt_scr[c, ch, :] + y_h
                xs_parts.append((x_t * to_end_t[r:r + 1, :]).astype(BF16))
                dec_parts.append(jnp.broadcast_to(dec_t[r:r + 1, :], (SSD_HEAD_DIM, D_STATE)))
            ds = _dot(jnp.concatenate(xs_parts, axis=0), bm)
            s_scr[dirn, g] = st * jnp.concatenate(dec_parts, axis=0) + ds

    def body(i, carry):
        scan_chunk(0, i)
        scan_chunk(1, nc - 1 - i)
        return carry

    lax.fori_loop(0, nc, body, 0)

    def finish(c, carry):
        rows = pl.ds(pl.multiple_of(c * CHUNK, CHUNK), CHUNK)
        y = yt_scr[c].T + xc_scr[rows, :] * dsk_ref[...]
        ya = y[:, 0:gw] * _silu(za_ref[rows, :])
        yb = y[:, gw:] * _silu(zb_ref[rows, :])
        ms = (jnp.sum(ya * ya, axis=-1, keepdims=True) + jnp.sum(yb * yb, axis=-1, keepdims=True)) / D_SSD
        inv = lax.rsqrt(ms + EPS)
        y_ref[rows, 0:gw] = (ya * inv * ng_ref[:, 0:gw]).astype(BF16)
        y_ref[rows, gw:] = (yb * inv * ng_ref[:, gw:]).astype(BF16)
        return carry

    lax.fori_loop(0, nc, finish, 0)

    if not has_init:
        if n_prev:
            sf_ref[0:n_prev] = psf_ref[...]
            sb_ref[0:n_prev] = psb_ref[...]
        for g in range(SSD_GROUPS):
            hs = slice(g * HEADS_PER_GROUP, (g + 1) * HEADS_PER_GROUP)
            sf_ref[n_prev, hs] = s_scr[0, g].reshape(HEADS_PER_GROUP, SSD_HEAD_DIM, D_STATE)
            sb_ref[n_prev, hs] = s_scr[1, g].reshape(HEADS_PER_GROUP, SSD_HEAD_DIM, D_STATE)


def _ssd(proj, dt_raw, p, n, nb, row_blk0, layer, init, prev_states, to_bf16=()):
    has_init = init is not None
    n_prev = 0 if has_init else layer
    kern = functools.partial(_ssd_kernel, n=n, has_init=has_init, n_prev=n_prev, n_cast=len(to_bf16))
    cast_in, cast_out, cast_shapes = _side_cast_specs(to_bf16, nb, lambda b: b)
    col = lambda cb: pl.BlockSpec((n, 512), lambda b: (row_blk0 + b, cb))
    vec = lambda r, w: pl.BlockSpec((r, w), lambda b: (0, 0))
    layers_spec = lambda k: pl.BlockSpec((None, k, SSD_HEADS, SSD_HEAD_DIM, D_STATE), lambda b: (b, 0, 0, 0, 0))
    in_specs = [
        col(3), col(4), col(5), col(6), col(7),
        pl.BlockSpec((n, 128), lambda b: (row_blk0 + b, 0)),
        vec(3, CONV_DIM), vec(1, CONV_DIM), vec(1, 128), vec(1, 128), vec(1, D_SSD), vec(1, D_SSD),
    ]
    args = [proj, proj, proj, proj, proj, dt_raw, p['conv_w'], p['conv_b'], p['dt_bias'], p['a_log'],
            p['d_skip'], p['ssd_norm_g']]
    y_spec = pl.BlockSpec((n, D_SSD), lambda b: (b, 0))
    y_shape = jax.ShapeDtypeStruct((nb * n, D_SSD), BF16)
    if has_init:
        init_spec = pl.BlockSpec((None, None, SSD_HEADS, SSD_HEAD_DIM, D_STATE), lambda b: (b, layer, 0, 0, 0))
        in_specs += [init_spec, init_spec]
        args += list(init)
        out_specs, out_shape = [y_spec], [y_shape]
    else:
        if n_prev:
            in_specs += [layers_spec(n_prev), layers_spec(n_prev)]
            args += list(prev_states)
        st_shape = jax.ShapeDtypeStruct((nb, layer + 1, SSD_HEADS, SSD_HEAD_DIM, D_STATE), F32)
        out_specs = [y_spec, layers_spec(layer + 1), layers_spec(layer + 1)]
        out_shape = [y_shape, st_shape, st_shape]
    return pl.pallas_call(
        kern,
        grid=(nb,),
        in_specs=in_specs + cast_in,
        out_specs=out_specs + cast_out,
        out_shape=out_shape + cast_shapes,
        scratch_shapes=[
            pltpu.VMEM((n, D_SSD), F32), pltpu.VMEM((n, 512), F32), pltpu.VMEM((n, 128), F32),
            pltpu.VMEM((n // CHUNK, D_SSD, CHUNK), F32), pltpu.VMEM((n // CHUNK, D_SSD, CHUNK), F32),
            pltpu.VMEM((2, SSD_GROUPS, 512, D_STATE), F32),
        ],
        compiler_params=_cparams(("parallel",)),
        name="ssd_sample" if has_init else "ssd_prompt",
    )(*args, *[part[0] for part in to_bf16])


def _seg_specs(segs, tm):
    specs, bounds, start = [], [], 0
    for a in segs:
        b0, nblk = start // tm, a.shape[0] // tm
        specs.append(pl.BlockSpec((tm, a.shape[1]), lambda i, b0=b0, nblk=nblk: (jnp.clip(i - b0, 0, nblk - 1), 0)))
        bounds.append(b0)
        start += a.shape[0]
    return specs, tuple(bounds)


def _seg_pick(refs, bounds):
    i = pl.program_id(0)
    v = refs[0][...]
    for ref, b0 in zip(refs[1:], bounds[1:]):
        v = jnp.where(i >= b0, ref[...], v)
    return v


def _outproj_kernel(*refs, with_router, o_bounds, y_bounds, x_bounds):
    refs = list(refs)
    o_refs = [refs.pop(0) for _ in o_bounds]
    y_refs = [refs.pop(0) for _ in y_bounds]
    x_refs = [refs.pop(0) for _ in x_bounds]
    if with_router:
        mod_ref, g_ref, w_ref, rw_ref, xo_ref, h_ref, meta_ref, cnt_ref, carry_scr = refs
    else:
        mod_ref, g_ref, w_ref, xo_ref, h_ref = refs
    tm = xo_ref.shape[0]
    a = jnp.concatenate([_seg_pick(o_refs, o_bounds), _seg_pick(y_refs, y_bounds)], axis=1)
    xn = _seg_pick(x_refs, x_bounds) + mod_ref[2:3, :] * _dot(a, w_ref[...])
    xo_ref[...] = xn
    h = _rms(xn, g_ref[...]) * (1.0 + mod_ref[4:5, :]) + mod_ref[3:4, :]
    h_ref[...] = h.astype(h_ref.dtype)
    if with_router:

        @pl.when(pl.program_id(0) == 0)
        def _():
            carry_scr[...] = jnp.zeros_like(carry_scr)

        h1, h2, _ = _split3(h)
        w1, w2, _ = _split3(rw_ref[...])
        logits = _dot_nt(w1, h1) + _dot_nt(w2, h1) + _dot_nt(w1, h2)
        row = lax.broadcasted_iota(jnp.int32, logits.shape, 0)
        logits = jnp.where(row < N_EXPERTS, logits, -jnp.inf)
        e = jnp.exp(logits - jnp.max(logits, axis=0, keepdims=True))
        probs = e / jnp.sum(e, axis=0, keepdims=True)
        p1 = jnp.max(probs, axis=0, keepdims=True)
        i1 = jnp.min(jnp.where(probs == p1, row, 16), axis=0, keepdims=True)
        rest = jnp.where(row == i1, -1.0, probs)
        p2 = jnp.max(rest, axis=0, keepdims=True)
        i2 = jnp.min(jnp.where(rest == p2, row, 16), axis=0, keepdims=True)
        hit1, hit2 = row == i1, row == i2
        onehot = jnp.where(hit1 | hit2, 1.0, 0.0)
        ti = lax.broadcasted_iota(jnp.int32, (tm, tm), 0)
        tj = lax.broadcasted_iota(jnp.int32, (tm, tm), 1)
        before = jnp.where(ti < tj, 1.0, 0.0).astype(BF16)
        rank = carry_scr[:, 0:1] + _dot(onehot.astype(BF16), before)
        r1 = jnp.sum(jnp.where(hit1, rank, 0.0), axis=0, keepdims=True)
        r2 = jnp.sum(jnp.where(hit2, rank, 0.0), axis=0, keepdims=True)
        carry_scr[...] = carry_scr[...] + jnp.sum(onehot, axis=1, keepdims=True)
        cnt_ref[...] = carry_scr[...]
        r8 = lax.broadcasted_iota(jnp.int32, (8, tm), 0)
        vals = [p1 / (p1 + p2), p2 / (p1 + p2), i1.astype(F32), i2.astype(F32), r1, r2]
        meta = jnp.zeros((8, tm), F32)
        for k, v in enumerate(vals):
            meta = jnp.where(r8 == k, v, meta)
        meta_ref[...] = meta


def _outproj(o_segs, y_segs, x_segs, mods, g, w_out, router_wt):
    tm = 512
    with_router = router_wt is not None
    o_specs, o_bounds = _seg_specs(o_segs, tm)
    y_specs, y_bounds = _seg_specs(y_segs, tm)
    x_specs, x_bounds = _seg_specs(x_segs, tm)
    kern = functools.partial(_outproj_kernel, with_router=with_router, o_bounds=o_bounds, y_bounds=y_bounds,
                             x_bounds=x_bounds)
    in_specs = o_specs + y_specs + x_specs + [
        pl.BlockSpec((None, 6, D_MODEL), lambda i: (_mod_group(i, tm), 0, 0)),
        pl.BlockSpec((1, D_MODEL), lambda i: (0, 0)),
        pl.BlockSpec((D_MODEL, D_MODEL), lambda i: (0, 0), pipeline_mode=pl.Buffered(1)),
    ]
    args = list(o_segs) + list(y_segs) + list(x_segs) + [mods, g, w_out]
    row_spec = pl.BlockSpec((tm, D_MODEL), lambda i: (i, 0))
    out_specs = [row_spec, row_spec]
    out_shape = [jax.ShapeDtypeStruct((T_ALL, D_MODEL), F32),
                 jax.ShapeDtypeStruct((T_ALL, D_MODEL), F32 if with_router else BF16)]
    scratch = []
    if with_router:
        in_specs.append(pl.BlockSpec((16, D_MODEL), lambda i: (0, 0)))
        args.append(router_wt)
        out_specs += [pl.BlockSpec((8, tm), lambda i: (0, i)), pl.BlockSpec((16, 128), lambda i: (0, 0))]
        out_shape += [jax.ShapeDtypeStruct((8, T_ALL), F32), jax.ShapeDtypeStruct((16, 128), F32)]
        scratch = [pltpu.VMEM((16, 128), F32)]
    return pl.pallas_call(
        kern,
        grid=(T_ALL // tm,),
        in_specs=in_specs,
        out_specs=out_specs,
        out_shape=out_shape,
        scratch_shapes=scratch,
        compiler_params=_cparams(("arbitrary",)),
        name="outproj_router" if with_router else "outproj",
    )(*args)


MOE_ROWS = 2 * T_ALL
MOE_TILE = 512
MOE_TILES = MOE_ROWS // MOE_TILE
MOE_VISITS = MOE_TILES + N_EXPERTS - 1


def _row_copy(src, s, dst, d, sem):
    return pltpu.make_async_copy(src.at[pl.ds(s, 1)], dst.at[pl.ds(d, 1)], sem)


def _dispatch_kernel(p1_ref, p2_ref, h_ref, xs_ref, sem):
    tm = h_ref.shape[0]

    def issue(r, c):
        _row_copy(h_ref, r, xs_ref, p1_ref[0, 0, r], sem.at[0]).start()
        _row_copy(h_ref, r, xs_ref, p2_ref[0, 0, r], sem.at[1]).start()
        return c

    lax.fori_loop(0, tm, issue, 0, unroll=8)
    pltpu.make_async_copy(h_ref, xs_ref.at[pl.ds(0, tm)], sem.at[0]).wait()
    pltpu.make_async_copy(h_ref, xs_ref.at[pl.ds(0, tm)], sem.at[1]).wait()


def _dispatch(h, pos1, pos2):
    tm = 512
    nt = T_ALL // tm
    idx = lambda: pl.BlockSpec((1, 1, tm), lambda i: (i, 0, 0), memory_space=pltpu.SMEM)
    return pl.pallas_call(
        _dispatch_kernel,
        grid=(nt,),
        in_specs=[idx(), idx(), pl.BlockSpec((tm, D_MODEL), lambda i: (i, 0))],
        out_specs=pl.BlockSpec(memory_space=pl.ANY),
        out_shape=jax.ShapeDtypeStruct((MOE_ROWS, D_MODEL), F32),
        scratch_shapes=[pltpu.SemaphoreType.DMA((2,))],
        compiler_params=_cparams(("arbitrary",)),
        name="moe_dispatch",
    )(pos1.reshape(nt, 1, tm), pos2.reshape(nt, 1, tm), h)


def _experts_kernel(vt_ref, ve_ref, nv_ref, lo_ref, hi_ref, xs_ref, wg_ref, wu_ref, wd_ref, y_ref):
    v = pl.program_id(0)

    @pl.when(v < nv_ref[0])
    def _():
        e = ve_ref[v]
        x = xs_ref[...].astype(BF16)
        hid = _silu(_dot(x, wg_ref[...])) * _dot(x, wu_ref[...])
        y = _dot(hid.astype(BF16), wd_ref[...])
        row = vt_ref[v] * MOE_TILE + lax.broadcasted_iota(jnp.int32, (MOE_TILE, 1), 0)
        mine = (row >= lo_ref[e]) & (row < hi_ref[e])
        first_visit = (v == 0) | (vt_ref[jnp.maximum(v - 1, 0)] != vt_ref[v])

        @pl.when(first_visit)
        def _():
            y_ref[...] = jnp.where(mine, y, 0.0)

        @pl.when(jnp.logical_not(first_visit))
        def _():
            y_ref[...] = jnp.where(mine, y, y_ref[...])


def _experts(xs, wg, wu, wd, vt, ve, nv, lo, hi):
    grid_spec = pltpu.PrefetchScalarGridSpec(
        num_scalar_prefetch=5,
        grid=(MOE_VISITS,),
        in_specs=[
            pl.BlockSpec((MOE_TILE, D_MODEL), lambda v, vt, ve, nv, lo, hi: (vt[v], 0)),
            pl.BlockSpec((None, D_MODEL, F_EXPERT), lambda v, vt, ve, nv, lo, hi: (ve[v], 0, 0)),
            pl.BlockSpec((None, D_MODEL, F_EXPERT), lambda v, vt, ve, nv, lo, hi: (ve[v], 0, 0)),
            pl.BlockSpec((None, F_EXPERT, D_MODEL), lambda v, vt, ve, nv, lo, hi: (ve[v], 0, 0)),
        ],
        out_specs=pl.BlockSpec((MOE_TILE, D_MODEL), lambda v, vt, ve, nv, lo, hi: (vt[v], 0)),
    )
    return pl.pallas_call(
        _experts_kernel,
        grid_spec=grid_spec,
        out_shape=jax.ShapeDtypeStruct((MOE_ROWS, D_MODEL), F32),
        compiler_params=_cparams(("arbitrary",)),
        name="moe_experts",
    )(vt, ve, nv, lo, hi, xs, wg, wu, wd)


def _combine_kernel(p1c_ref, p2c_ref, p1n_ref, p2n_ref, y_hbm, x_ref, mod_ref, gate_ref, fg_ref,
                    outp_ref, outs_ref, ya_buf, yb_buf, sem):
    i = pl.program_id(0)
    n = pl.num_programs(0)
    tm = x_ref.shape[0]
    slot = i % 2

    def gather(pa_ref, pb_ref, s):
        def issue(r, c):
            _row_copy(y_hbm, pa_ref[0, 0, r], ya_buf.at[s], r, sem.at[0, s]).start()
            _row_copy(y_hbm, pb_ref[0, 0, r], yb_buf.at[s], r, sem.at[1, s]).start()
            return c

        lax.fori_loop(0, tm, issue, 0, unroll=8)

    @pl.when(i == 0)
    def _():
        gather(p1c_ref, p2c_ref, 0)

    @pl.when(i + 1 < n)
    def _():
        gather(p1n_ref, p2n_ref, 1 - slot)

    pltpu.make_async_copy(y_hbm.at[pl.ds(0, tm)], ya_buf.at[slot], sem.at[0, slot]).wait()
    pltpu.make_async_copy(y_hbm.at[pl.ds(0, tm)], yb_buf.at[slot], sem.at[1, slot]).wait()
    g = gate_ref[...]
    mix = g[:, 0:1] * ya_buf[slot] + g[:, 1:2] * yb_buf[slot]
    xo = _rms(x_ref[...] + mod_ref[5:6, :] * mix, fg_ref[...])

    @pl.when(i < T_PROMPT // tm)
    def _():
        outp_ref[...] = xo

    @pl.when(i >= T_PROMPT // tm)
    def _():
        outs_ref[...] = xo


def _combine(y, x, mods, gate_cols, pos1, pos2, final_g):
    tm = 512
    nt = T_ALL // tm
    ntp = T_PROMPT // tm
    cur = lambda: pl.BlockSpec((1, 1, tm), lambda i: (i, 0, 0), memory_space=pltpu.SMEM)
    nxt = lambda: pl.BlockSpec((1, 1, tm), lambda i: (jnp.minimum(i + 1, nt - 1), 0, 0), memory_space=pltpu.SMEM)
    p1, p2 = pos1.reshape(nt, 1, tm), pos2.reshape(nt, 1, tm)
    return pl.pallas_call(
        _combine_kernel,
        grid=(nt,),
        in_specs=[
            cur(), cur(), nxt(), nxt(),
            pl.BlockSpec(memory_space=pl.ANY),
            pl.BlockSpec((tm, D_MODEL), lambda i: (i, 0)),
            pl.BlockSpec((None, 6, D_MODEL), lambda i: (_mod_group(i, tm), 0, 0)),
            pl.BlockSpec((tm, 128), lambda i: (i, 0)),
            pl.BlockSpec((1, D_MODEL), lambda i: (0, 0)),
        ],
        out_specs=[
            pl.BlockSpec((tm, D_MODEL), lambda i: (jnp.minimum(i, ntp - 1), 0)),
            pl.BlockSpec((tm, D_MODEL), lambda i: (jnp.maximum(i - ntp, 0), 0)),
        ],
        out_shape=[
            jax.ShapeDtypeStruct((T_PROMPT, D_MODEL), F32), jax.ShapeDtypeStruct((T_SAMPLE, D_MODEL), F32),
        ],
        scratch_shapes=[
            pltpu.VMEM((2, tm, D_MODEL), F32), pltpu.VMEM((2, tm, D_MODEL), F32),
            pltpu.SemaphoreType.DMA((2, 2)),
        ],
        compiler_params=_cparams(("arbitrary",)),
        name="moe_combine",
    )(p1, p2, p1, p2, y, x, mods, gate_cols, final_g)


def _route_plan(meta, counts):
    i1, i2 = meta[2].astype(jnp.int32), meta[3].astype(jnp.int32)
    r1, r2 = meta[4].astype(jnp.int32), meta[5].astype(jnp.int32)
    cnt = counts[:N_EXPERTS, 0].astype(jnp.int32)
    hi = jnp.cumsum(cnt)
    lo = hi - cnt
    ex = jnp.arange(N_EXPERTS, dtype=jnp.int32)
    pos1 = jnp.sum(jnp.where(i1[:, None] == ex[None, :], lo[None, :], 0), axis=1) + r1
    pos2 = jnp.sum(jnp.where(i2[:, None] == ex[None, :], lo[None, :], 0), axis=1) + r2
    first_tile = lo // MOE_TILE
    n_vis_e = jnp.where(cnt > 0, (hi - 1) // MOE_TILE - first_tile + 1, 0)
    vis_hi = jnp.cumsum(n_vis_e)
    vis_lo = vis_hi - n_vis_e
    nv = vis_hi[-1]
    v = jnp.minimum(jnp.arange(MOE_VISITS, dtype=jnp.int32), nv - 1)
    ve = jnp.minimum(jnp.sum(v[:, None] >= vis_hi[None, :], axis=1), N_EXPERTS - 1).astype(jnp.int32)
    pick = lambda tab: jnp.sum(jnp.where(ve[:, None] == ex[None, :], tab[None, :], 0), axis=1)
    vt = (pick(first_tile) + v - pick(vis_lo)).astype(jnp.int32)
    return pos1, pos2, vt, ve, nv.reshape(1).astype(jnp.int32), lo.astype(jnp.int32), hi.astype(jnp.int32)


def _ffn_kernel(*refs, n_cast):
    h_ref, x_hbm, mod_ref, wg_ref, wu_ref, wd_ref = refs[:6]
    cast_in = refs[6:6 + n_cast]
    out_ref = refs[6 + n_cast]
    cast_out = refs[7 + n_cast:7 + 2 * n_cast]
    x_buf, sem = refs[7 + 2 * n_cast:]
    _side_cast(cast_in, cast_out)
    i, f = pl.program_id(0), pl.program_id(1)
    tm = h_ref.shape[0]
    x_copy = pltpu.make_async_copy(x_hbm.at[pl.ds(pl.multiple_of(i * tm, tm), tm)], x_buf, sem)

    @pl.when(f == 0)
    def _():
        x_copy.start()
        out_ref[...] = jnp.zeros_like(out_ref)

    h = h_ref[...]
    hid = _silu(_dot(h, wg_ref[...])) * _dot(h, wu_ref[...])
    out_ref[...] += _dot(hid.astype(BF16), wd_ref[...])

    @pl.when(f == pl.num_programs(1) - 1)
    def _():
        x_copy.wait()
        out_ref[...] = x_buf[...] + mod_ref[5:6, :] * out_ref[...]


FFN_CAST_STEPS = 64


def _ffn(h, x, mods, wg, wu, wd, to_bf16):
    tm, tf = 1024, 512
    nf = F_DENSE // tf
    assert (T_ALL // tm) * nf >= FFN_CAST_STEPS

    cast_in, cast_out, cast_shapes = _side_cast_specs(to_bf16, FFN_CAST_STEPS, lambda i, f: i * nf + f)
    outs = pl.pallas_call(
        functools.partial(_ffn_kernel, n_cast=len(to_bf16)),
        grid=(T_ALL // tm, nf),
        in_specs=[
            pl.BlockSpec((tm, D_MODEL), lambda i, f: (i, 0)),
            _ANY,
            pl.BlockSpec((None, 6, D_MODEL), lambda i, f: (_mod_group(i, tm), 0, 0)),
            pl.BlockSpec((D_MODEL, tf), lambda i, f: (0, f)),
            pl.BlockSpec((D_MODEL, tf), lambda i, f: (0, f)),
            pl.BlockSpec((tf, D_MODEL), lambda i, f: (f, 0)),
        ] + cast_in,
        out_specs=[pl.BlockSpec((tm, D_MODEL), lambda i, f: (i, 0))] + cast_out,
        out_shape=[jax.ShapeDtypeStruct((T_ALL, D_MODEL), F32)] + cast_shapes,
        scratch_shapes=[pltpu.VMEM((tm, D_MODEL), F32), pltpu.SemaphoreType.DMA(())],
        compiler_params=_cparams(("arbitrary", "arbitrary")),
        name="dense_ffn",
    )(h, x, mods, wg, wu, wd, *[part[0] for part in to_bf16])
    return outs[0], outs[1:]


def _rope_tables():
    n = DEC_SEQ
    rows = n // GRID_W
    t_row = jnp.repeat(jnp.arange(rows, dtype=F32), GRID_W)
    t_col = jnp.tile(jnp.arange(GRID_W, dtype=F32), rows)
    inv = 1.0 / (ROPE_THETA ** (jnp.arange(0, ROT_HALF, 2, dtype=F32) / ROT_HALF))
    ar, ac = t_row[:, None] * inv, t_col[:, None] * inv
    cos = jnp.concatenate([jnp.cos(ar), jnp.cos(ar), jnp.cos(ac), jnp.cos(ac)], axis=-1)
    sin_signed = jnp.concatenate([-jnp.sin(ar), jnp.sin(ar), -jnp.sin(ac), jnp.sin(ac)], axis=-1)
    return cos, sin_signed


def _pad_lanes(v, width=128):
    return jnp.pad(v, ((0, 0), (0, width - v.shape[-1])))


def kernel(x_prompt, x_sample, c, cache_k, cache_v, state_ssm_fwd, state_ssm_bwd, c_ctx, ada_w, ada_b, norm1_g, norm2_g, w_in, q_norm_g, k_norm_g, conv_w, conv_b, a_log_fwd, a_log_bwd, dt_bias_fwd, dt_bias_bwd, d_skip, ssd_norm_g, attn_out_g, w_out, ffn_w_gate, ffn_w_up, ffn_w_down, router_w, moe_w_gate, moe_w_up, moe_w_down, final_norm_g):
    assert DEPTH % 2 == 0
    cond = jnp.concatenate([c_ctx[None, :], c, jnp.zeros((N_COND - 1 - DEC_BATCH, D_MODEL), F32)], axis=0)
    mods_all = _ada_mods(cond, ada_w, ada_b).reshape(DEPTH, N_COND, 6, D_MODEL)
    cos, sin_signed = _rope_tables()

    w_in_t = jnp.swapaxes(w_in, 1, 2)
    x_segs = [x_prompt.reshape(T_PROMPT, D_MODEL), x_sample.reshape(T_SAMPLE, D_MODEL)]
    kv, states = None, None
    for l in range(DEPTH):
        mods = mods_all[l]
        proj, dt_raw, (w_o,) = _inproj(x_segs, mods, norm1_g[l][None, :], w_in_t, l, [(w_out.reshape(DEPTH * D_MODEL, D_MODEL), l * D_MODEL, D_MODEL)])

        qg, kg, og = q_norm_g[l][None, :], k_norm_g[l][None, :], attn_out_g[l][None, :]
        o_p, k_all, v_all = _attention_prompt(proj, 0, qg, kg, og, l, kv)
        kv = (k_all, v_all)
        o_s = _attention_sample(proj, T_PROMPT, qg, kg, og, cache_k, cache_v, cos, sin_signed, l)

        p = {
            'conv_w': conv_w[l], 'conv_b': conv_b[l][None, :],
            'dt_bias': _pad_lanes(jnp.concatenate([dt_bias_fwd[l], dt_bias_bwd[l]])[None, :]),
            'a_log': _pad_lanes(jnp.concatenate([a_log_fwd[l], a_log_bwd[l]])[None, :]),
            'd_skip': jnp.repeat(d_skip[l], SSD_HEAD_DIM)[None, :],
            'ssd_norm_g': ssd_norm_g[l][None, :],
        }
        j = l // 2
        n_up, n_down = N_EXPERTS * D_MODEL, N_EXPERTS * F_EXPERT
        if l % 2 == 0:
            parts = [(ffn_w_gate.reshape(-1, F_DENSE), j * D_MODEL, D_MODEL),
                     (ffn_w_up.reshape(-1, F_DENSE), j * D_MODEL, D_MODEL),
                     (ffn_w_down.reshape(-1, D_MODEL), j * F_DENSE, F_DENSE)]
        else:
            parts = [(moe_w_down.reshape(-1, D_MODEL), j * n_down, n_down)]
        y_p, sf, sb, *mixer_w = _ssd(proj, dt_raw, p, SEQ, BATCH, 0, l, None, states, parts)
        states = (sf, sb)
        y_s, = _ssd(proj, dt_raw, p, DEC_SEQ, DEC_BATCH, T_PROMPT // DEC_SEQ, l, (state_ssm_fwd, state_ssm_bwd), None)

        g2 = norm2_g[l][None, :]
        if l % 2 == 0:
            x, h = _outproj([o_p, o_s], [y_p, y_s], x_segs, mods, g2, w_o, None)
            x, (eg, eu) = _ffn(h, x, mods, *mixer_w, [
                (moe_w_gate.reshape(-1, F_EXPERT), j * n_up, n_up),
                (moe_w_up.reshape(-1, F_EXPERT), j * n_up, n_up)])
            x_segs = [x]
        else:
            router_wt = jnp.pad(router_w[j].T, ((0, 16 - N_EXPERTS), (0, 0)))
            x, h, meta, counts = _outproj([o_p, o_s], [y_p, y_s], x_segs, mods, g2, w_o, router_wt)
            pos1, pos2, vt, ve, nv, lo, hi = _route_plan(meta, counts)
            xs = _dispatch(h, pos1, pos2)
            ys = _experts(xs, eg.reshape(N_EXPERTS, D_MODEL, F_EXPERT), eu.reshape(N_EXPERTS, D_MODEL, F_EXPERT),
                          mixer_w[0].reshape(N_EXPERTS, F_EXPERT, D_MODEL), vt, ve, nv, lo, hi)
            y_prompt, y_sample = _combine(ys, x, mods, _pad_lanes(meta[:2].T), pos1, pos2, final_norm_g[None, :])

    return (y_prompt.reshape(BATCH, SEQ, D_MODEL), y_sample.reshape(DEC_BATCH, DEC_SEQ, D_MODEL),
            kv[0], kv[1], states[0], states[1])
```

```python
import functools

import jax
import jax.numpy as jnp
from jax import lax
from jax.experimental import pallas as pl
from jax.experimental.pallas import tpu as pltpu

F32 = jnp.float32
BF16 = jnp.bfloat16

D_MODEL = 2048
BATCH = 16
SEQ = 256
DEPTH = 2
DEC_BATCH = 2
DEC_SEQ = 1024
PAST_LEN = 512
GRID_W = 64
D_ATTN = 1024
D_SSD = 1024
HEAD_DIM = 128
N_Q_HEADS = 8
N_KV_HEADS = 2
Q_PER_KV = 4
KV_DIM = 256
ROT_HALF = 64
ROPE_THETA = 10000.0
SSD_HEAD_DIM = 64
SSD_HEADS = 16
SSD_GROUPS = 2
HEADS_PER_GROUP = 8
D_STATE = 128
CONV_DIM = 1536
CHUNK = 128
N_MAIN = 4096
F_DENSE = 5632
N_EXPERTS = 8
F_EXPERT = 1024
EPS = 1e-6

T_PROMPT = BATCH * SEQ
T_SAMPLE = DEC_BATCH * DEC_SEQ
T_ALL = T_PROMPT + T_SAMPLE
N_COND = 16

VMEM_LIMIT = 58 * 1024 * 1024


def _cparams(sem):
    return pltpu.CompilerParams(dimension_semantics=sem, vmem_limit_bytes=VMEM_LIMIT)


def _mod_group(i, tm):
    return jnp.maximum(0, (i * tm - T_PROMPT + DEC_SEQ) // DEC_SEQ)


def _silu(x):
    return x * jax.nn.sigmoid(x)


def _rms(x, g):
    ms = jnp.mean(x * x, axis=-1, keepdims=True)
    return x * lax.rsqrt(ms + EPS) * g


def _dot(a, b):
    return jnp.dot(a, b, preferred_element_type=F32)


def _dot_nt(a, b):
    return lax.dot_general(a, b, (((1,), (1,)), ((), ())), preferred_element_type=F32)


def _split3(x):
    hi = x.astype(BF16)
    r1 = x - hi.astype(F32)
    mid = r1.astype(BF16)
    r2 = r1 - mid.astype(F32)
    return hi, mid, r2.astype(BF16)


def _ada_kernel(c_ref, w_ref, b_ref, o_ref):
    s = _silu(c_ref[...]).astype(BF16)
    o_ref[...] = _dot(s, w_ref[...].astype(BF16)) + b_ref[...]


def _ada_mods(cond, ada_w, ada_b):
    tn = 1024
    n_out = 6 * D_MODEL
    return pl.pallas_call(
        _ada_kernel,
        grid=(DEPTH, n_out // tn),
        in_specs=[
            pl.BlockSpec((N_COND, D_MODEL), lambda l, j: (0, 0)),
            pl.BlockSpec((None, D_MODEL, tn), lambda l, j: (l, 0, j)),
            pl.BlockSpec((None, 1, tn), lambda l, j: (l, 0, j)),
        ],
        out_specs=pl.BlockSpec((None, N_COND, tn), lambda l, j: (l, 0, j)),
        out_shape=jax.ShapeDtypeStruct((DEPTH, N_COND, n_out), F32),
        compiler_params=_cparams(("parallel", "parallel")),
        name="ada_mods",
    )(cond, ada_w, ada_b.reshape(DEPTH, 1, n_out))


def _side_cast_specs(parts, n_steps, linear_step):
    in_specs, out_specs, out_shapes = [], [], []
    for a, row0, nrows in parts:
        rows = nrows // n_steps
        blk0 = row0 // rows
        step = lambda *ids: jnp.minimum(linear_step(*ids), n_steps - 1)
        in_specs.append(pl.BlockSpec((rows, a.shape[1]), lambda *ids, blk0=blk0: (blk0 + step(*ids), 0)))
        out_specs.append(pl.BlockSpec((rows, a.shape[1]), lambda *ids: (step(*ids), 0)))
        out_shapes.append(jax.ShapeDtypeStruct((nrows, a.shape[1]), BF16))
    return in_specs, out_specs, out_shapes


def _side_cast(cast_in, cast_out):
    for src, dst in zip(cast_in, cast_out):
        dst[...] = src[...].astype(BF16)


INPROJ_NORM_ROWS = 1024
INPROJ_CAST_STEPS = 16


def _inproj_kernel(*refs, seg_rows, n_cast):
    n_seg = len(seg_rows)
    x_hbms = refs[:n_seg]
    mods_ref, g_ref, w_ref, wdt_ref = refs[n_seg:n_seg + 4]
    cast_in = refs[n_seg + 4:n_seg + 4 + n_cast]
    proj_ref, dt_ref = refs[n_seg + 4 + n_cast:n_seg + 6 + n_cast]
    cast_out = refs[n_seg + 6 + n_cast:n_seg + 6 + 2 * n_cast]
    x_buf, h_scr, sem = refs[n_seg + 6 + 2 * n_cast:]
    _side_cast(cast_in, cast_out)
    i, j = pl.program_id(0), pl.program_id(1)
    tm = x_buf.shape[0]

    def fetch(tile):
        start = 0
        for x_hbm, nrows in zip(x_hbms, seg_rows):
            b0, nb = start // tm, nrows // tm
            start += nrows

            @pl.when((tile >= b0) & (tile < b0 + nb))
            def _():
                r0 = pl.multiple_of((tile - b0) * tm, tm)
                pltpu.make_async_copy(x_hbm.at[pl.ds(r0, tm)], x_buf, sem).start()

    @pl.when(j == 0)
    def _():
        @pl.when(i == 0)
        def _():
            fetch(i)

        pltpu.make_async_copy(x_hbms[0].at[pl.ds(0, tm)], x_buf, sem).wait()
        for k in range(tm // INPROJ_NORM_ROWS):
            rows = slice(k * INPROJ_NORM_ROWS, (k + 1) * INPROJ_NORM_ROWS)
            mod = mods_ref[_mod_group(i * (tm // INPROJ_NORM_ROWS) + k, INPROJ_NORM_ROWS)]
            h = _rms(x_buf[rows, :], g_ref[...]) * (1.0 + mod[1:2, :]) + mod[0:1, :]
            h_scr[rows, :] = h.astype(BF16)

        @pl.when(i + 1 < pl.num_programs(0))
        def _():
            fetch(i + 1)

        n_dt = wdt_ref.shape[0]
        wdt = jnp.concatenate([wdt_ref[...], jnp.zeros((128 - n_dt, D_MODEL), F32)], axis=0)
        dt_ref[...] = _dot_nt(h_scr[...], wdt.astype(BF16))

    proj_ref[...] = _dot_nt(h_scr[...], w_ref[...].astype(BF16))


_ANY = pl.BlockSpec(memory_space=pl.ANY)


def _inproj(x_segs, mods, g, w_in_t, layer, to_bf16):
    tm, tn = 2048, 512
    n_dt = w_in_t.shape[1] - N_MAIN
    nj = N_MAIN // tn
    seg_rows = tuple(a.shape[0] for a in x_segs)
    assert all(r % tm == 0 for r in seg_rows) and sum(seg_rows) == T_ALL
    assert (T_ALL // tm) * nj >= INPROJ_CAST_STEPS
    cast_in, cast_out, cast_shapes = _side_cast_specs(to_bf16, INPROJ_CAST_STEPS, lambda i, j: i * nj + j)
    outs = pl.pallas_call(
        functools.partial(_inproj_kernel, seg_rows=seg_rows, n_cast=len(to_bf16)),
        grid=(T_ALL // tm, nj),
        in_specs=[_ANY] * len(x_segs) + [
            pl.BlockSpec(mods.shape, lambda i, j: (0, 0, 0)),
            pl.BlockSpec((1, D_MODEL), lambda i, j: (0, 0)),
            pl.BlockSpec((None, tn, D_MODEL), lambda i, j: (layer, j, 0)),
            pl.BlockSpec((None, n_dt, D_MODEL), lambda i, j: (layer, N_MAIN // n_dt, 0)),
        ] + cast_in,
        out_specs=[
            pl.BlockSpec((tm, tn), lambda i, j: (i, j)),
            pl.BlockSpec((tm, 128), lambda i, j: (i, 0)),
        ] + cast_out,
        out_shape=[
            jax.ShapeDtypeStruct((T_ALL, N_MAIN), F32),
            jax.ShapeDtypeStruct((T_ALL, 128), F32),
        ] + cast_shapes,
        scratch_shapes=[pltpu.VMEM((tm, D_MODEL), F32), pltpu.VMEM((tm, D_MODEL), BF16),
                        pltpu.SemaphoreType.DMA(())],
        compiler_params=_cparams(("arbitrary", "arbitrary")),
        name="inproj",
    )(*x_segs, mods, g, w_in_t, w_in_t, *[part[0] for part in to_bf16])
    return outs[0], outs[1], outs[2:]


def _rope(x, cos, sin_signed):
    lane = lax.broadcasted_iota(jnp.int32, x.shape, 1)
    first = (lane // (ROT_HALF // 2)) % 2 == 0
    swapped = jnp.where(first, pltpu.roll(x, HEAD_DIM - ROT_HALF // 2, 1), pltpu.roll(x, ROT_HALF // 2, 1))
    return x * cos + swapped * sin_signed


def _attn_kernel(*refs, nk_new, has_ctx, n_prev):
    if has_ctx:
        (q_ref, kv_ref, qg_ref, kg_ref, og_ref, ck_ref, cv_ref, cq_ref, sq_ref, ckk_ref, skk_ref,
         o_ref, kb_scr, vb_scr, o_scr) = refs
    elif n_prev:
        (q_ref, kv_ref, qg_ref, kg_ref, og_ref, pk_ref, pv_ref, o_ref, ko_ref, vo_ref,
         kb_scr, vb_scr, o_scr) = refs
    else:
        (q_ref, kv_ref, qg_ref, kg_ref, og_ref, o_ref, ko_ref, vo_ref, kb_scr, vb_scr, o_scr) = refs

    @pl.when(pl.program_id(1) == 0)
    def _():
        if not has_ctx and n_prev:
            ko_ref[0:n_prev] = pk_ref[...]
            vo_ref[0:n_prev] = pv_ref[...]
        for g in range(N_KV_HEADS):
            sl = slice(g * HEAD_DIM, (g + 1) * HEAD_DIM)
            kn = _rms(kv_ref[:, sl], kg_ref[...])
            v = kv_ref[:, KV_DIM + g * HEAD_DIM:KV_DIM + (g + 1) * HEAD_DIM]
            vsl = slice(2 * g * HEAD_DIM, (2 * g + 1) * HEAD_DIM)
            vb_scr[0:nk_new, vsl] = v.astype(BF16)
            vb_scr[:, (2 * g + 1) * HEAD_DIM:(2 * g + 2) * HEAD_DIM] = jnp.ones((vb_scr.shape[0], HEAD_DIM), BF16)
            if has_ctx:
                kb_scr[0:nk_new, sl] = _rope(kn, ckk_ref[...], skk_ref[...]).astype(BF16)
                kb_scr[nk_new:, sl] = ck_ref[:, g, :].astype(BF16)
                vb_scr[nk_new:, vsl] = cv_ref[:, g, :].astype(BF16)
            else:
                kb_scr[:, sl] = kn.astype(BF16)
                ko_ref[n_prev, :, g, :] = kn
                vo_ref[n_prev, :, g, :] = v

    scale_log2e = HEAD_DIM ** -0.5 * 1.4426950408889634
    for h in range(N_Q_HEADS):
        g = h // Q_PER_KV
        sl = slice(h * HEAD_DIM, (h + 1) * HEAD_DIM)
        gsl = slice(g * HEAD_DIM, (g + 1) * HEAD_DIM)
        qn = _rms(q_ref[:, sl], qg_ref[...])
        if has_ctx:
            qn = _rope(qn, cq_ref[...], sq_ref[...])
        s = _dot_nt((qn * scale_log2e).astype(BF16), kb_scr[:, gsl])
        e = jnp.exp2(s - jnp.max(s, axis=-1, keepdims=True))
        pv = _dot(e.astype(BF16), vb_scr[:, 2 * g * HEAD_DIM:(2 * g + 2) * HEAD_DIM])
        o_scr[:, sl] = pv[:, :HEAD_DIM] / pv[:, HEAD_DIM:]
    o_ref[...] = _rms(o_scr[...], og_ref[...]).astype(BF16)


def _attention_prompt(proj, row0, qg, kg, og, layer, prev_kv):
    n = SEQ
    blk0 = row0 // n
    kern = functools.partial(_attn_kernel, nk_new=n, has_ctx=False, n_prev=layer)
    vec = lambda w: pl.BlockSpec((1, w), lambda b, i: (0, 0))
    cache_spec = lambda k: pl.BlockSpec((None, k, n, N_KV_HEADS, HEAD_DIM), lambda b, i: (b, 0, 0, 0, 0))
    cache_shape = jax.ShapeDtypeStruct((BATCH, layer + 1, n, N_KV_HEADS, HEAD_DIM), F32)
    in_specs = [
        pl.BlockSpec((n, D_ATTN), lambda b, i: (blk0 + b, 0)),
        pl.BlockSpec((n, 2 * KV_DIM), lambda b, i: (blk0 + b, 2)),
        vec(HEAD_DIM), vec(HEAD_DIM), vec(D_ATTN),
    ]
    args = [proj, proj, qg, kg, og]
    if layer:
        in_specs += [cache_spec(layer), cache_spec(layer)]
        args += list(prev_kv)
    return pl.pallas_call(
        kern,
        grid=(BATCH, 1),
        in_specs=in_specs,
        out_specs=[pl.BlockSpec((n, D_ATTN), lambda b, i: (b, 0)), cache_spec(layer + 1), cache_spec(layer + 1)],
        out_shape=[jax.ShapeDtypeStruct((T_PROMPT, D_ATTN), BF16), cache_shape, cache_shape],
        scratch_shapes=[
            pltpu.VMEM((n, KV_DIM), BF16), pltpu.VMEM((n, 2 * KV_DIM), BF16), pltpu.VMEM((n, D_ATTN), F32),
        ],
        compiler_params=_cparams(("parallel", "arbitrary")),
        name="attn_prompt",
    )(*args)


def _attention_sample(proj, row0, qg, kg, og, ck, cv, cos, sin_signed, layer):
    n, tq = DEC_SEQ, 512
    nq = n // tq
    nk = n + PAST_LEN
    kern = functools.partial(_attn_kernel, nk_new=n, has_ctx=True, n_prev=0)
    vec = lambda w: pl.BlockSpec((1, w), lambda b, i: (0, 0))
    q_blk0 = row0 // tq
    kv_blk0 = row0 // n
    return pl.pallas_call(
        kern,
        grid=(DEC_BATCH, nq),
        in_specs=[
            pl.BlockSpec((tq, D_ATTN), lambda b, i: (q_blk0 + b * nq + i, 0)),
            pl.BlockSpec((n, 2 * KV_DIM), lambda b, i: (kv_blk0 + b, 2)),
            vec(HEAD_DIM), vec(HEAD_DIM), vec(D_ATTN),
            pl.BlockSpec((None, None, PAST_LEN, N_KV_HEADS, HEAD_DIM), lambda b, i: (b, layer, 0, 0, 0)),
            pl.BlockSpec((None, None, PAST_LEN, N_KV_HEADS, HEAD_DIM), lambda b, i: (b, layer, 0, 0, 0)),
            pl.BlockSpec((tq, HEAD_DIM), lambda b, i: (i, 0)),
            pl.BlockSpec((tq, HEAD_DIM), lambda b, i: (i, 0)),
            pl.BlockSpec((n, HEAD_DIM), lambda b, i: (0, 0)),
            pl.BlockSpec((n, HEAD_DIM), lambda b, i: (0, 0)),
        ],
        out_specs=pl.BlockSpec((tq, D_ATTN), lambda b, i: (b * nq + i, 0)),
        out_shape=jax.ShapeDtypeStruct((T_SAMPLE, D_ATTN), BF16),
        scratch_shapes=[
            pltpu.VMEM((nk, KV_DIM), BF16), pltpu.VMEM((nk, 2 * KV_DIM), BF16), pltpu.VMEM((tq, D_ATTN), F32),
        ],
        compiler_params=_cparams(("parallel", "arbitrary")),
        name="attn_sample",
    )(proj, proj, qg, kg, og, ck, cv, cos, sin_signed, cos, sin_signed)


def _conv_silu(x, w, b):
    n = x.shape[0]
    row = lax.broadcasted_iota(jnp.int32, (n, 1), 0)
    prev = jnp.where(row == 0, 0.0, pltpu.roll(x, 1, 0))
    nxt = jnp.where(row == n - 1, 0.0, pltpu.roll(x, n - 1, 0))
    return _silu(prev * w[0:1, :] + x * w[1:2, :] + nxt * w[2:3, :] + b)


def _softplus(x):
    return jnp.maximum(x, 0.0) + jnp.log1p(jnp.exp(-jnp.abs(x)))


def _ssd_kernel(*refs, n, has_init, n_prev, n_cast):
    refs = list(refs)
    (za_ref, zb_ref, xa_ref, xb_ref, bc_ref, dt_ref, cw_ref, cb_ref, dtb_ref, alog_ref, dsk_ref,
     ng_ref) = refs[:12]
    del refs[:12]
    if has_init:
        sf0_ref, sb0_ref = refs.pop(0), refs.pop(0)
    elif n_prev:
        psf_ref, psb_ref = refs.pop(0), refs.pop(0)
    cast_in = [refs.pop(0) for _ in range(n_cast)]
    y_ref = refs.pop(0)
    if not has_init:
        sf_ref, sb_ref = refs.pop(0), refs.pop(0)
    cast_out = [refs.pop(0) for _ in range(n_cast)]
    xc_scr, bcc_scr, dts_scr, xt_scr, yt_scr, s_scr = refs
    _side_cast(cast_in, cast_out)
    nc = n // CHUNK
    gw = HEADS_PER_GROUP * SSD_HEAD_DIM

    xc_scr[:, 0:gw] = _conv_silu(xa_ref[...], cw_ref[:, 0:gw], cb_ref[:, 0:gw])
    xc_scr[:, gw:] = _conv_silu(xb_ref[...], cw_ref[:, gw:2 * gw], cb_ref[:, gw:2 * gw])
    bcc_scr[...] = _conv_silu(bc_ref[...], cw_ref[:, 2 * gw:], cb_ref[:, 2 * gw:])
    dts_scr[...] = _softplus(dt_ref[...] + dtb_ref[...])
    for g in range(SSD_GROUPS):
        hs = slice(g * HEADS_PER_GROUP, (g + 1) * HEADS_PER_GROUP)
        if has_init:
            s_scr[0, g] = sf0_ref[hs].reshape(gw, D_STATE)
            s_scr[1, g] = sb0_ref[hs].reshape(gw, D_STATE)
        else:
            s_scr[0, g] = jnp.zeros((gw, D_STATE), F32)
            s_scr[1, g] = jnp.zeros((gw, D_STATE), F32)

    def to_channel_major(c, carry):
        rows = pl.ds(pl.multiple_of(c * CHUNK, CHUNK), CHUNK)
        xt_scr[c] = xc_scr[rows, :].T
        yt_scr[c] = jnp.zeros((D_SSD, CHUNK), F32)
        return carry

    lax.fori_loop(0, nc, to_channel_major, 0)

    a_row = -jnp.exp(alog_ref[...])
    ri = lax.broadcasted_iota(jnp.int32, (CHUNK, CHUNK), 0)
    ci = lax.broadcasted_iota(jnp.int32, (CHUNK, CHUNK), 1)
    lower, upper = ci <= ri, ci >= ri
    n_dirs_heads = 2 * SSD_HEADS

    def scan_chunk(dirn, c):
        tri = (lower if dirn == 0 else upper).astype(BF16)
        valid_st = upper if dirn == 0 else lower
        tri_t = valid_st.astype(BF16)
        row0 = dirn * SSD_HEADS
        rows = pl.ds(pl.multiple_of(c * CHUNK, CHUNK), CHUNK)
        dt = dts_scr[rows, :]
        d = dt * a_row
        d1, d2, d3 = _split3(d)
        cs = _dot(tri, d1) + _dot(tri, d2) + _dot(tri, d3)
        dt_t = dt.T[0:n_dirs_heads, :]
        e1, e2, e3 = _split3(d.T[0:n_dirs_heads, :])
        cs_t = _dot(e1, tri_t) + _dot(e2, tri_t) + _dot(e3, tri_t)
        total = cs_t[:, CHUNK - 1:CHUNK] if dirn == 0 else cs_t[:, 0:1]
        e_in_t = jnp.exp(cs_t)
        to_end_t = jnp.exp(total - cs_t) * dt_t
        dec_t = jnp.broadcast_to(jnp.exp(total), (n_dirs_heads, D_STATE))
        for g in range(SSD_GROUPS):
            bm = bcc_scr[rows, g * D_STATE:(g + 1) * D_STATE].astype(BF16)
            cm = bcc_scr[rows, (SSD_GROUPS + g) * D_STATE:(SSD_GROUPS + g + 1) * D_STATE]
            g_st = _dot_nt(bm, cm.astype(BF16))
            c_nt = cm.T
            st = s_scr[dirn, g]
            xs_parts, dec_parts = [], []
            for hh in range(HEADS_PER_GROUP):
                h = g * HEADS_PER_GROUP + hh
                r = row0 + h
                ch = slice(h * SSD_HEAD_DIM, (h + 1) * SSD_HEAD_DIM)
                x_t = xt_scr[c, ch, :]
                diff = cs_t[r:r + 1, :] - cs[:, r:r + 1]
                a_st = (g_st * jnp.exp(jnp.where(valid_st, diff, -jnp.inf))).astype(BF16)
                c_e = (c_nt * e_in_t[r:r + 1, :]).astype(BF16)
                x_dt = (x_t * dt_t[r:r + 1, :]).astype(BF16)
                s_h = st[hh * SSD_HEAD_DIM:(hh + 1) * SSD_HEAD_DIM, :].astype(BF16)
                y_h = _dot(jnp.concatenate([x_dt, s_h], axis=1), jnp.concatenate([a_st, c_e], axis=0))
                yt_scr[c, ch, :] = yt_scr[c, ch, :] + y_h
                xs_parts.append((x_t * to_end_t[r:r + 1, :]).astype(BF16))
                dec_parts.append(jnp.broadcast_to(dec_t[r:r + 1, :], (SSD_HEAD_DIM, D_STATE)))
            ds = _dot(jnp.concatenate(xs_parts, axis=0), bm)
            s_scr[dirn, g] = st * jnp.concatenate(dec_parts, axis=0) + ds

    def body(i, carry):
        scan_chunk(0, i)
        scan_chunk(1, nc - 1 - i)
        return carry

    lax.fori_loop(0, nc, body, 0)

    def finish(c, carry):
        rows = pl.ds(pl.multiple_of(c * CHUNK, CHUNK), CHUNK)
        y = yt_scr[c].T + xc_scr[rows, :] * dsk_ref[...]
        ya = y[:, 0:gw] * _silu(za_ref[rows, :])
        yb = y[:, gw:] * _silu(zb_ref[rows, :])
        ms = (jnp.sum(ya * ya, axis=-1, keepdims=True) + jnp.sum(yb * yb, axis=-1, keepdims=True)) / D_SSD
        inv = lax.rsqrt(ms + EPS)
        y_ref[rows, 0:gw] = (ya * inv * ng_ref[:, 0:gw]).astype(BF16)
        y_ref[rows, gw:] = (yb * inv * ng_ref[:, gw:]).astype(BF16)
        return carry

    lax.fori_loop(0, nc, finish, 0)

    if not has_init:
        if n_prev:
            sf_ref[0:n_prev] = psf_ref[...]
            sb_ref[0:n_prev] = psb_ref[...]
        for g in range(SSD_GROUPS):
            hs = slice(g * HEADS_PER_GROUP, (g + 1) * HEADS_PER_GROUP)
            sf_ref[n_prev, hs] = s_scr[0, g].reshape(HEADS_PER_GROUP, SSD_HEAD_DIM, D_STATE)
            sb_ref[n_prev, hs] = s_scr[1, g].reshape(HEADS_PER_GROUP, SSD_HEAD_DIM, D_STATE)


def _ssd(proj, dt_raw, p, n, nb, row_blk0, layer, init, prev_states, to_bf16=()):
    has_init = init is not None
    n_prev = 0 if has_init else layer
    kern = functools.partial(_ssd_kernel, n=n, has_init=has_init, n_prev=n_prev, n_cast=len(to_bf16))
    cast_in, cast_out, cast_shapes = _side_cast_specs(to_bf16, nb, lambda b: b)
    col = lambda cb: pl.BlockSpec((n, 512), lambda b: (row_blk0 + b, cb))
    vec = lambda r, w: pl.BlockSpec((r, w), lambda b: (0, 0))
    layers_spec = lambda k: pl.BlockSpec((None, k, SSD_HEADS, SSD_HEAD_DIM, D_STATE), lambda b: (b, 0, 0, 0, 0))
    in_specs = [
        col(3), col(4), col(5), col(6), col(7),
        pl.BlockSpec((n, 128), lambda b: (row_blk0 + b, 0)),
        vec(3, CONV_DIM), vec(1, CONV_DIM), vec(1, 128), vec(1, 128), vec(1, D_SSD), vec(1, D_SSD),
    ]
    args = [proj, proj, proj, proj, proj, dt_raw, p['conv_w'], p['conv_b'], p['dt_bias'], p['a_log'],
            p['d_skip'], p['ssd_norm_g']]
    y_spec = pl.BlockSpec((n, D_SSD), lambda b: (b, 0))
    y_shape = jax.ShapeDtypeStruct((nb * n, D_SSD), BF16)
    if has_init:
        init_spec = pl.BlockSpec((None, None, SSD_HEADS, SSD_HEAD_DIM, D_STATE), lambda b: (b, layer, 0, 0, 0))
        in_specs += [init_spec, init_spec]
        args += list(init)
        out_specs, out_shape = [y_spec], [y_shape]
    else:
        if n_prev:
            in_specs += [layers_spec(n_prev), layers_spec(n_prev)]
            args += list(prev_states)
        st_shape = jax.ShapeDtypeStruct((nb, layer + 1, SSD_HEADS, SSD_HEAD_DIM, D_STATE), F32)
        out_specs = [y_spec, layers_spec(layer + 1), layers_spec(layer + 1)]
        out_shape = [y_shape, st_shape, st_shape]
    return pl.pallas_call(
        kern,
        grid=(nb,),
        in_specs=in_specs + cast_in,
        out_specs=out_specs + cast_out,
        out_shape=out_shape + cast_shapes,
        scratch_shapes=[
            pltpu.VMEM((n, D_SSD), F32), pltpu.VMEM((n, 512), F32), pltpu.VMEM((n, 128), F32),
            pltpu.VMEM((n // CHUNK, D_SSD, CHUNK), F32), pltpu.VMEM((n // CHUNK, D_SSD, CHUNK), F32),
            pltpu.VMEM((2, SSD_GROUPS, 512, D_STATE), F32),
        ],
        compiler_params=_cparams(("parallel",)),
        name="ssd_sample" if has_init else "ssd_prompt",
    )(*args, *[part[0] for part in to_bf16])


def _seg_specs(segs, tm, lag=0):
    specs, bounds, start = [], [], 0
    for a in segs:
        b0, nblk = start // tm, a.shape[0] // tm
        specs.append(pl.BlockSpec((tm, a.shape[1]),
                                  lambda i, b0=b0, nblk=nblk: (jnp.clip(i - lag - b0, 0, nblk - 1), 0)))
        bounds.append(b0)
        start += a.shape[0]
    return specs, tuple(bounds)


def _seg_pick(refs, bounds, tile):
    v = refs[0][...]
    for ref, b0 in zip(refs[1:], bounds[1:]):
        v = jnp.where(tile >= b0, ref[...], v)
    return v


def _outproj_kernel(*refs, with_router, o_bounds, y_bounds, x_bounds):
    refs = list(refs)
    o_refs = [refs.pop(0) for _ in o_bounds]
    y_refs = [refs.pop(0) for _ in y_bounds]
    x_refs = [refs.pop(0) for _ in x_bounds]
    if with_router:
        mod_ref, g_ref, w_ref, rw_ref, xo_ref, h_ref, meta_ref, cnt_ref, acc_a, acc_b, carry_scr = refs
    else:
        mod_ref, g_ref, w_ref, xo_ref, h_ref, acc_a, acc_b = refs
        rw_ref = meta_ref = cnt_ref = carry_scr = None
    i = pl.program_id(0)

    @pl.when(i == 0)
    def _():
        acc_b[...] = jnp.zeros_like(acc_b)
        if with_router:
            carry_scr[...] = jnp.zeros_like(carry_scr)

    def step(acc_new, acc_done):
        a = jnp.concatenate([_seg_pick(o_refs, o_bounds, i), _seg_pick(y_refs, y_bounds, i)], axis=1)
        acc_new[...] = _dot(a, w_ref[...])
        _outproj_finish(acc_done, i - 1, x_refs, x_bounds, mod_ref, g_ref, rw_ref, xo_ref, h_ref, meta_ref,
                        cnt_ref, carry_scr)

    @pl.when(i % 2 == 0)
    def _():
        step(acc_a, acc_b)

    @pl.when(i % 2 == 1)
    def _():
        step(acc_b, acc_a)


def _outproj_finish(acc_ref, tile, x_refs, x_bounds, mod_ref, g_ref, rw_ref, xo_ref, h_ref, meta_ref, cnt_ref,
                    carry_scr):
    tm = xo_ref.shape[0]
    xn = _seg_pick(x_refs, x_bounds, tile) + mod_ref[2:3, :] * acc_ref[...]
    xo_ref[...] = xn
    h = _rms(xn, g_ref[...]) * (1.0 + mod_ref[4:5, :]) + mod_ref[3:4, :]
    h_ref[...] = h.astype(h_ref.dtype)
    if rw_ref is not None:
        h1, h2, _ = _split3(h)
        w1, w2, _ = _split3(rw_ref[...])
        logits = _dot_nt(w1, h1) + _dot_nt(w2, h1) + _dot_nt(w1, h2)
        row = lax.broadcasted_iota(jnp.int32, logits.shape, 0)
        logits = jnp.where(row < N_EXPERTS, logits, -jnp.inf)
        e = jnp.exp(logits - jnp.max(logits, axis=0, keepdims=True))
        probs = e / jnp.sum(e, axis=0, keepdims=True)
        p1 = jnp.max(probs, axis=0, keepdims=True)
        i1 = jnp.min(jnp.where(probs == p1, row, 16), axis=0, keepdims=True)
        rest = jnp.where(row == i1, -1.0, probs)
        p2 = jnp.max(rest, axis=0, keepdims=True)
        i2 = jnp.min(jnp.where(rest == p2, row, 16), axis=0, keepdims=True)
        hit1, hit2 = row == i1, row == i2
        onehot = jnp.where((hit1 | hit2) & (tile >= 0), 1.0, 0.0)
        ti = lax.broadcasted_iota(jnp.int32, (tm, tm), 0)
        tj = lax.broadcasted_iota(jnp.int32, (tm, tm), 1)
        before = jnp.where(ti < tj, 1.0, 0.0).astype(BF16)
        rank = carry_scr[:, 0:1] + _dot(onehot.astype(BF16), before)
        r1 = jnp.sum(jnp.where(hit1, rank, 0.0), axis=0, keepdims=True)
        r2 = jnp.sum(jnp.where(hit2, rank, 0.0), axis=0, keepdims=True)
        carry_scr[...] = carry_scr[...] + jnp.sum(onehot, axis=1, keepdims=True)
        cnt_ref[...] = carry_scr[...]
        r8 = lax.broadcasted_iota(jnp.int32, (8, tm), 0)
        vals = [p1 / (p1 + p2), p2 / (p1 + p2), i1.astype(F32), i2.astype(F32), r1, r2]
        meta = jnp.zeros((8, tm), F32)
        for k, v in enumerate(vals):
            meta = jnp.where(r8 == k, v, meta)
        meta_ref[...] = meta


def _outproj(o_segs, y_segs, x_segs, mods, g, w_out, router_wt):
    tm = 512
    with_router = router_wt is not None
    o_specs, o_bounds = _seg_specs(o_segs, tm)
    y_specs, y_bounds = _seg_specs(y_segs, tm)
    x_specs, x_bounds = _seg_specs(x_segs, tm, lag=1)
    kern = functools.partial(_outproj_kernel, with_router=with_router, o_bounds=o_bounds, y_bounds=y_bounds,
                             x_bounds=x_bounds)
    done = lambda i: jnp.maximum(i - 1, 0)
    in_specs = o_specs + y_specs + x_specs + [
        pl.BlockSpec((None, 6, D_MODEL), lambda i: (_mod_group(done(i), tm), 0, 0)),
        pl.BlockSpec((1, D_MODEL), lambda i: (0, 0)),
        pl.BlockSpec((D_MODEL, D_MODEL), lambda i: (0, 0), pipeline_mode=pl.Buffered(1)),
    ]
    args = list(o_segs) + list(y_segs) + list(x_segs) + [mods, g, w_out]
    row_spec = pl.BlockSpec((tm, D_MODEL), lambda i: (done(i), 0))
    out_specs = [row_spec, row_spec]
    out_shape = [jax.ShapeDtypeStruct((T_ALL, D_MODEL), F32),
                 jax.ShapeDtypeStruct((T_ALL, D_MODEL), F32 if with_router else BF16)]
    scratch = [pltpu.VMEM((tm, D_MODEL), F32), pltpu.VMEM((tm, D_MODEL), F32)]
    if with_router:
        in_specs.append(pl.BlockSpec((16, D_MODEL), lambda i: (0, 0)))
        args.append(router_wt)
        out_specs += [pl.BlockSpec((8, tm), lambda i: (0, done(i))), pl.BlockSpec((16, 128), lambda i: (0, 0))]
        out_shape += [jax.ShapeDtypeStruct((8, T_ALL), F32), jax.ShapeDtypeStruct((16, 128), F32)]
        scratch.append(pltpu.VMEM((16, 128), F32))
    return pl.pallas_call(
        kern,
        grid=(T_ALL // tm + 1,),
        in_specs=in_specs,
        out_specs=out_specs,
        out_shape=out_shape,
        scratch_shapes=scratch,
        compiler_params=_cparams(("arbitrary",)),
        name="outproj_router" if with_router else "outproj",
    )(*args)


MOE_ROWS = 2 * T_ALL
MOE_TILE = 256
MOE_TILES = MOE_ROWS // MOE_TILE
MOE_VISITS = MOE_TILES + N_EXPERTS - 1


def _row_copy(src, s, dst, d, sem):
    return pltpu.make_async_copy(src.at[pl.ds(s, 1)], dst.at[pl.ds(d, 1)], sem)


def _dispatch_kernel(p1_ref, p2_ref, h_ref, xs_ref, sem):
    tm = h_ref.shape[0]

    def issue(r, c):
        _row_copy(h_ref, r, xs_ref, p1_ref[0, 0, r], sem.at[0]).start()
        _row_copy(h_ref, r, xs_ref, p2_ref[0, 0, r], sem.at[1]).start()
        return c

    lax.fori_loop(0, tm, issue, 0, unroll=8)
    pltpu.make_async_copy(h_ref, xs_ref.at[pl.ds(0, tm)], sem.at[0]).wait()
    pltpu.make_async_copy(h_ref, xs_ref.at[pl.ds(0, tm)], sem.at[1]).wait()


def _dispatch(h, pos1, pos2):
    tm = 512
    nt = T_ALL // tm
    idx = lambda: pl.BlockSpec((1, 1, tm), lambda i: (i, 0, 0), memory_space=pltpu.SMEM)
    return pl.pallas_call(
        _dispatch_kernel,
        grid=(nt,),
        in_specs=[idx(), idx(), pl.BlockSpec((tm, D_MODEL), lambda i: (i, 0))],
        out_specs=pl.BlockSpec(memory_space=pl.ANY),
        out_shape=jax.ShapeDtypeStruct((MOE_ROWS, D_MODEL), F32),
        scratch_shapes=[pltpu.SemaphoreType.DMA((2,))],
        compiler_params=_cparams(("arbitrary",)),
        name="moe_dispatch",
    )(pos1.reshape(nt, 1, tm), pos2.reshape(nt, 1, tm), h)


def _experts_kernel(vt_ref, ve_ref, nv_ref, lo_ref, hi_ref, xs_ref, wg_ref, wu_ref, wd_ref, y_ref):
    v = pl.program_id(0)

    @pl.when(v < nv_ref[0])
    def _():
        e = ve_ref[v]
        x = xs_ref[...].astype(BF16)
        hid = _silu(_dot(x, wg_ref[...])) * _dot(x, wu_ref[...])
        y = _dot(hid.astype(BF16), wd_ref[...])
        row = vt_ref[v] * MOE_TILE + lax.broadcasted_iota(jnp.int32, (MOE_TILE, 1), 0)
        mine = (row >= lo_ref[e]) & (row < hi_ref[e])
        first_visit = (v == 0) | (vt_ref[jnp.maximum(v - 1, 0)] != vt_ref[v])

        @pl.when(first_visit)
        def _():
            y_ref[...] = jnp.where(mine, y, 0.0)

        @pl.when(jnp.logical_not(first_visit))
        def _():
            y_ref[...] = jnp.where(mine, y, y_ref[...])


def _experts(xs, wg, wu, wd, vt, ve, nv, lo, hi):
    grid_spec = pltpu.PrefetchScalarGridSpec(
        num_scalar_prefetch=5,
        grid=(MOE_VISITS,),
        in_specs=[
            pl.BlockSpec((MOE_TILE, D_MODEL), lambda v, vt, ve, nv, lo, hi: (vt[v], 0)),
            pl.BlockSpec((None, D_MODEL, F_EXPERT), lambda v, vt, ve, nv, lo, hi: (ve[v], 0, 0)),
            pl.BlockSpec((None, D_MODEL, F_EXPERT), lambda v, vt, ve, nv, lo, hi: (ve[v], 0, 0)),
            pl.BlockSpec((None, F_EXPERT, D_MODEL), lambda v, vt, ve, nv, lo, hi: (ve[v], 0, 0)),
        ],
        out_specs=pl.BlockSpec((MOE_TILE, D_MODEL), lambda v, vt, ve, nv, lo, hi: (vt[v], 0)),
    )
    return pl.pallas_call(
        _experts_kernel,
        grid_spec=grid_spec,
        out_shape=jax.ShapeDtypeStruct((MOE_ROWS, D_MODEL), F32),
        compiler_params=_cparams(("arbitrary",)),
        name="moe_experts",
    )(vt, ve, nv, lo, hi, xs, wg, wu, wd)


def _combine_kernel(p1c_ref, p2c_ref, p1n_ref, p2n_ref, y_hbm, x_ref, mod_ref, gate_ref, fg_ref,
                    outp_ref, outs_ref, ya_buf, yb_buf, sem):
    i = pl.program_id(0)
    n = pl.num_programs(0)
    tm = x_ref.shape[0]
    slot = i % 2

    def gather(pa_ref, pb_ref, s):
        def issue(r, c):
            _row_copy(y_hbm, pa_ref[0, 0, r], ya_buf.at[s], r, sem.at[0, s]).start()
            _row_copy(y_hbm, pb_ref[0, 0, r], yb_buf.at[s], r, sem.at[1, s]).start()
            return c

        lax.fori_loop(0, tm, issue, 0, unroll=8)

    @pl.when(i == 0)
    def _():
        gather(p1c_ref, p2c_ref, 0)

    @pl.when(i + 1 < n)
    def _():
        gather(p1n_ref, p2n_ref, 1 - slot)

    pltpu.make_async_copy(y_hbm.at[pl.ds(0, tm)], ya_buf.at[slot], sem.at[0, slot]).wait()
    pltpu.make_async_copy(y_hbm.at[pl.ds(0, tm)], yb_buf.at[slot], sem.at[1, slot]).wait()
    g = gate_ref[...]
    mix = g[:, 0:1] * ya_buf[slot] + g[:, 1:2] * yb_buf[slot]
    xo = _rms(x_ref[...] + mod_ref[5:6, :] * mix, fg_ref[...])

    @pl.when(i < T_PROMPT // tm)
    def _():
        outp_ref[...] = xo

    @pl.when(i >= T_PROMPT // tm)
    def _():
        outs_ref[...] = xo


def _combine(y, x, mods, gate_cols, pos1, pos2, final_g):
    tm = 256
    nt = T_ALL // tm
    ntp = T_PROMPT // tm
    cur = lambda: pl.BlockSpec((1, 1, tm), lambda i: (i, 0, 0), memory_space=pltpu.SMEM)
    nxt = lambda: pl.BlockSpec((1, 1, tm), lambda i: (jnp.minimum(i + 1, nt - 1), 0, 0), memory_space=pltpu.SMEM)
    p1, p2 = pos1.reshape(nt, 1, tm), pos2.reshape(nt, 1, tm)
    return pl.pallas_call(
        _combine_kernel,
        grid=(nt,),
        in_specs=[
            cur(), cur(), nxt(), nxt(),
            pl.BlockSpec(memory_space=pl.ANY),
            pl.BlockSpec((tm, D_MODEL), lambda i: (i, 0)),
            pl.BlockSpec((None, 6, D_MODEL), lambda i: (_mod_group(i, tm), 0, 0)),
            pl.BlockSpec((tm, 128), lambda i: (i, 0)),
            pl.BlockSpec((1, D_MODEL), lambda i: (0, 0)),
        ],
        out_specs=[
            pl.BlockSpec((tm, D_MODEL), lambda i: (jnp.minimum(i, ntp - 1), 0)),
            pl.BlockSpec((tm, D_MODEL), lambda i: (jnp.maximum(i - ntp, 0), 0)),
        ],
        out_shape=[
            jax.ShapeDtypeStruct((T_PROMPT, D_MODEL), F32), jax.ShapeDtypeStruct((T_SAMPLE, D_MODEL), F32),
        ],
        scratch_shapes=[
            pltpu.VMEM((2, tm, D_MODEL), F32), pltpu.VMEM((2, tm, D_MODEL), F32),
            pltpu.SemaphoreType.DMA((2, 2)),
        ],
        compiler_params=_cparams(("arbitrary",)),
        name="moe_combine",
    )(p1, p2, p1, p2, y, x, mods, gate_cols, final_g)


def _route_plan(meta, counts):
    i1, i2 = meta[2].astype(jnp.int32), meta[3].astype(jnp.int32)
    r1, r2 = meta[4].astype(jnp.int32), meta[5].astype(jnp.int32)
    cnt = counts[:N_EXPERTS, 0].astype(jnp.int32)
    hi = jnp.cumsum(cnt)
    lo = hi - cnt
    ex = jnp.arange(N_EXPERTS, dtype=jnp.int32)
    pos1 = jnp.sum(jnp.where(i1[:, None] == ex[None, :], lo[None, :], 0), axis=1) + r1
    pos2 = jnp.sum(jnp.where(i2[:, None] == ex[None, :], lo[None, :], 0), axis=1) + r2
    first_tile = lo // MOE_TILE
    n_vis_e = jnp.where(cnt > 0, (hi - 1) // MOE_TILE - first_tile + 1, 0)
    vis_hi = jnp.cumsum(n_vis_e)
    vis_lo = vis_hi - n_vis_e
    nv = vis_hi[-1]
    v = jnp.minimum(jnp.arange(MOE_VISITS, dtype=jnp.int32), nv - 1)
    ve = jnp.minimum(jnp.sum(v[:, None] >= vis_hi[None, :], axis=1), N_EXPERTS - 1).astype(jnp.int32)
    pick = lambda tab: jnp.sum(jnp.where(ve[:, None] == ex[None, :], tab[None, :], 0), axis=1)
    vt = (pick(first_tile) + v - pick(vis_lo)).astype(jnp.int32)
    return pos1, pos2, vt, ve, nv.reshape(1).astype(jnp.int32), lo.astype(jnp.int32), hi.astype(jnp.int32)


def _ffn_kernel(*refs, n_cast):
    h_ref, x_hbm, mod_ref, wg_ref, wu_ref, wd_ref = refs[:6]
    cast_in = refs[6:6 + n_cast]
    out_ref = refs[6 + n_cast]
    cast_out = refs[7 + n_cast:7 + 2 * n_cast]
    x_buf, sem = refs[7 + 2 * n_cast:]
    _side_cast(cast_in, cast_out)
    i, f = pl.program_id(0), pl.program_id(1)
    tm = h_ref.shape[0]
    x_copy = pltpu.make_async_copy(x_hbm.at[pl.ds(pl.multiple_of(i * tm, tm), tm)], x_buf, sem)

    @pl.when(f == 0)
    def _():
        x_copy.start()
        out_ref[...] = jnp.zeros_like(out_ref)

    h = h_ref[...]
    hid = _silu(_dot(h, wg_ref[...])) * _dot(h, wu_ref[...])
    out_ref[...] += _dot(hid.astype(BF16), wd_ref[...])

    @pl.when(f == pl.num_programs(1) - 1)
    def _():
        x_copy.wait()
        out_ref[...] = x_buf[...] + mod_ref[5:6, :] * out_ref[...]


FFN_CAST_STEPS = 64


def _ffn(h, x, mods, wg, wu, wd, to_bf16):
    tm, tf = 1024, 512
    nf = F_DENSE // tf
    assert (T_ALL // tm) * nf >= FFN_CAST_STEPS

    cast_in, cast_out, cast_shapes = _side_cast_specs(to_bf16, FFN_CAST_STEPS, lambda i, f: i * nf + f)
    outs = pl.pallas_call(
        functools.partial(_ffn_kernel, n_cast=len(to_bf16)),
        grid=(T_ALL // tm, nf),
        in_specs=[
            pl.BlockSpec((tm, D_MODEL), lambda i, f: (i, 0)),
            _ANY,
            pl.BlockSpec((None, 6, D_MODEL), lambda i, f: (_mod_group(i, tm), 0, 0)),
            pl.BlockSpec((D_MODEL, tf), lambda i, f: (0, f)),
            pl.BlockSpec((D_MODEL, tf), lambda i, f: (0, f)),
            pl.BlockSpec((tf, D_MODEL), lambda i, f: (f, 0)),
        ] + cast_in,
        out_specs=[pl.BlockSpec((tm, D_MODEL), lambda i, f: (i, 0))] + cast_out,
        out_shape=[jax.ShapeDtypeStruct((T_ALL, D_MODEL), F32)] + cast_shapes,
        scratch_shapes=[pltpu.VMEM((tm, D_MODEL), F32), pltpu.SemaphoreType.DMA(())],
        compiler_params=_cparams(("arbitrary", "arbitrary")),
        name="dense_ffn",
    )(h, x, mods, wg, wu, wd, *[part[0] for part in to_bf16])
    return outs[0], outs[1:]


def _rope_tables():
    n = DEC_SEQ
    rows = n // GRID_W
    t_row = jnp.repeat(jnp.arange(rows, dtype=F32), GRID_W)
    t_col = jnp.tile(jnp.arange(GRID_W, dtype=F32), rows)
    inv = 1.0 / (ROPE_THETA ** (jnp.arange(0, ROT_HALF, 2, dtype=F32) / ROT_HALF))
    ar, ac = t_row[:, None] * inv, t_col[:, None] * inv
    cos = jnp.concatenate([jnp.cos(ar), jnp.cos(ar), jnp.cos(ac), jnp.cos(ac)], axis=-1)
    sin_signed = jnp.concatenate([-jnp.sin(ar), jnp.sin(ar), -jnp.sin(ac), jnp.sin(ac)], axis=-1)
    return cos, sin_signed


def _pad_lanes(v, width=128):
    return jnp.pad(v, ((0, 0), (0, width - v.shape[-1])))


def kernel(x_prompt, x_sample, c, cache_k, cache_v, state_ssm_fwd, state_ssm_bwd, c_ctx, ada_w, ada_b, norm1_g, norm2_g, w_in, q_norm_g, k_norm_g, conv_w, conv_b, a_log_fwd, a_log_bwd, dt_bias_fwd, dt_bias_bwd, d_skip, ssd_norm_g, attn_out_g, w_out, ffn_w_gate, ffn_w_up, ffn_w_down, router_w, moe_w_gate, moe_w_up, moe_w_down, final_norm_g):
    assert DEPTH % 2 == 0
    cond = jnp.concatenate([c_ctx[None, :], c, jnp.zeros((N_COND - 1 - DEC_BATCH, D_MODEL), F32)], axis=0)
    mods_all = _ada_mods(cond, ada_w, ada_b).reshape(DEPTH, N_COND, 6, D_MODEL)
    cos, sin_signed = _rope_tables()

    w_in_t = jnp.swapaxes(w_in, 1, 2)
    x_segs = [x_prompt.reshape(T_PROMPT, D_MODEL), x_sample.reshape(T_SAMPLE, D_MODEL)]
    kv, states = None, None
    for l in range(DEPTH):
        mods = mods_all[l]
        proj, dt_raw, (w_o,) = _inproj(x_segs, mods, norm1_g[l][None, :], w_in_t, l, [(w_out.reshape(DEPTH * D_MODEL, D_MODEL), l * D_MODEL, D_MODEL)])

        qg, kg, og = q_norm_g[l][None, :], k_norm_g[l][None, :], attn_out_g[l][None, :]
        o_p, k_all, v_all = _attention_prompt(proj, 0, qg, kg, og, l, kv)
        kv = (k_all, v_all)
        o_s = _attention_sample(proj, T_PROMPT, qg, kg, og, cache_k, cache_v, cos, sin_signed, l)

        p = {
            'conv_w': conv_w[l], 'conv_b': conv_b[l][None, :],
            'dt_bias': _pad_lanes(jnp.concatenate([dt_bias_fwd[l], dt_bias_bwd[l]])[None, :]),
            'a_log': _pad_lanes(jnp.concatenate([a_log_fwd[l], a_log_bwd[l]])[None, :]),
            'd_skip': jnp.repeat(d_skip[l], SSD_HEAD_DIM)[None, :],
            'ssd_norm_g': ssd_norm_g[l][None, :],
        }
        j = l // 2
        n_up, n_down = N_EXPERTS * D_MODEL, N_EXPERTS * F_EXPERT
        if l % 2 == 0:
            parts = [(ffn_w_gate.reshape(-1, F_DENSE), j * D_MODEL, D_MODEL),
                     (ffn_w_up.reshape(-1, F_DENSE), j * D_MODEL, D_MODEL),
                     (ffn_w_down.reshape(-1, D_MODEL), j * F_DENSE, F_DENSE)]
        else:
            parts = [(moe_w_down.reshape(-1, D_MODEL), j * n_down, n_down)]
        y_p, sf, sb, *mixer_w = _ssd(proj, dt_raw, p, SEQ, BATCH, 0, l, None, states, parts)
        states = (sf, sb)
        y_s, = _ssd(proj, dt_raw, p, DEC_SEQ, DEC_BATCH, T_PROMPT // DEC_SEQ, l, (state_ssm_fwd, state_ssm_bwd), None)

        g2 = norm2_g[l][None, :]
        if l % 2 == 0:
            x, h = _outproj([o_p, o_s], [y_p, y_s], x_segs, mods, g2, w_o, None)
            x, (eg, eu) = _ffn(h, x, mods, *mixer_w, [
                (moe_w_gate.reshape(-1, F_EXPERT), j * n_up, n_up),
                (moe_w_up.reshape(-1, F_EXPERT), j * n_up, n_up)])
            x_segs = [x]
        else:
            router_wt = jnp.pad(router_w[j].T, ((0, 16 - N_EXPERTS), (0, 0)))
            x, h, meta, counts = _outproj([o_p, o_s], [y_p, y_s], x_segs, mods, g2, w_o, router_wt)
            pos1, pos2, vt, ve, nv, lo, hi = _route_plan(meta, counts)
            xs = _dispatch(h, pos1, pos2)
            ys = _experts(xs, eg.reshape(N_EXPERTS, D_MODEL, F_EXPERT), eu.reshape(N_EXPERTS, D_MODEL, F_EXPERT),
                          mixer_w[0].reshape(N_EXPERTS, F_EXPERT, D_MODEL), vt, ve, nv, lo, hi)
            y_prompt, y_sample = _combine(ys, x, mods, _pad_lanes(meta[:2].T), pos1, pos2, final_norm_g[None, :])

    return (y_prompt.reshape(BATCH, SEQ, D_MODEL), y_sample.reshape(DEC_BATCH, DEC_SEQ, D_MODEL),
            kv[0], kv[1], states[0], states[1])
```

```python
import functools

import jax
import jax.numpy as jnp
from jax import lax
from jax.experimental import pallas as pl
from jax.experimental.pallas import tpu as pltpu

F32 = jnp.float32
BF16 = jnp.bfloat16

D_MODEL = 2048
BATCH = 16
SEQ = 256
DEPTH = 2
DEC_BATCH = 2
DEC_SEQ = 1024
PAST_LEN = 512
GRID_W = 64
D_ATTN = 1024
D_SSD = 1024
HEAD_DIM = 128
N_Q_HEADS = 8
N_KV_HEADS = 2
Q_PER_KV = 4
KV_DIM = 256
ROT_HALF = 64
ROPE_THETA = 10000.0
SSD_HEAD_DIM = 64
SSD_HEADS = 16
SSD_GROUPS = 2
HEADS_PER_GROUP = 8
D_STATE = 128
CONV_DIM = 1536
CHUNK = 128
N_MAIN = 4096
F_DENSE = 5632
N_EXPERTS = 8
F_EXPERT = 1024
EPS = 1e-6

T_PROMPT = BATCH * SEQ
T_SAMPLE = DEC_BATCH * DEC_SEQ
T_ALL = T_PROMPT + T_SAMPLE
N_COND = 16

VMEM_LIMIT = 58 * 1024 * 1024


def _cparams(sem):
    return pltpu.CompilerParams(dimension_semantics=sem, vmem_limit_bytes=VMEM_LIMIT)


def _mod_group(i, tm):
    return jnp.maximum(0, (i * tm - T_PROMPT + DEC_SEQ) // DEC_SEQ)


def _silu(x):
    return x * jax.nn.sigmoid(x)


def _rms(x, g):
    ms = jnp.mean(x * x, axis=-1, keepdims=True)
    return x * lax.rsqrt(ms + EPS) * g


def _dot(a, b):
    return jnp.dot(a, b, preferred_element_type=F32)


def _dot_nt(a, b):
    return lax.dot_general(a, b, (((1,), (1,)), ((), ())), preferred_element_type=F32)


def _split3(x):
    hi = x.astype(BF16)
    r1 = x - hi.astype(F32)
    mid = r1.astype(BF16)
    r2 = r1 - mid.astype(F32)
    return hi, mid, r2.astype(BF16)


def _ada_kernel(c_ref, w_ref, b_ref, o_ref):
    s = _silu(c_ref[...]).astype(BF16)
    o_ref[...] = _dot(s, w_ref[...].astype(BF16)) + b_ref[...]


def _ada_mods(cond, ada_w, ada_b):
    tn = 1024
    n_out = 6 * D_MODEL
    return pl.pallas_call(
        _ada_kernel,
        grid=(DEPTH, n_out // tn),
        in_specs=[
            pl.BlockSpec((N_COND, D_MODEL), lambda l, j: (0, 0)),
            pl.BlockSpec((None, D_MODEL, tn), lambda l, j: (l, 0, j)),
            pl.BlockSpec((None, 1, tn), lambda l, j: (l, 0, j)),
        ],
        out_specs=pl.BlockSpec((None, N_COND, tn), lambda l, j: (l, 0, j)),
        out_shape=jax.ShapeDtypeStruct((DEPTH, N_COND, n_out), F32),
        compiler_params=_cparams(("parallel", "parallel")),
        name="ada_mods",
    )(cond, ada_w, ada_b.reshape(DEPTH, 1, n_out))


def _side_cast_specs(parts, n_steps, linear_step):
    in_specs, out_specs, out_shapes = [], [], []
    for a, row0, nrows in parts:
        rows = nrows // n_steps
        blk0 = row0 // rows
        step = lambda *ids: jnp.minimum(linear_step(*ids), n_steps - 1)
        in_specs.append(pl.BlockSpec((rows, a.shape[1]), lambda *ids, blk0=blk0: (blk0 + step(*ids), 0)))
        out_specs.append(pl.BlockSpec((rows, a.shape[1]), lambda *ids: (step(*ids), 0)))
        out_shapes.append(jax.ShapeDtypeStruct((nrows, a.shape[1]), BF16))
    return in_specs, out_specs, out_shapes


def _side_cast(cast_in, cast_out):
    for src, dst in zip(cast_in, cast_out):
        dst[...] = src[...].astype(BF16)


INPROJ_NORM_ROWS = 1024
INPROJ_CAST_STEPS = 16


def _inproj_kernel(*refs, seg_rows, n_cast):
    n_seg = len(seg_rows)
    x_hbms = refs[:n_seg]
    mods_ref, g_ref, w_ref, wdt_ref = refs[n_seg:n_seg + 4]
    cast_in = refs[n_seg + 4:n_seg + 4 + n_cast]
    proj_ref, dt_ref = refs[n_seg + 4 + n_cast:n_seg + 6 + n_cast]
    cast_out = refs[n_seg + 6 + n_cast:n_seg + 6 + 2 * n_cast]
    x_buf, h_scr, sem = refs[n_seg + 6 + 2 * n_cast:]
    _side_cast(cast_in, cast_out)
    i, j = pl.program_id(0), pl.program_id(1)
    tm = x_buf.shape[0]

    def fetch(tile):
        start = 0
        for x_hbm, nrows in zip(x_hbms, seg_rows):
            b0, nb = start // tm, nrows // tm
            start += nrows

            @pl.when((tile >= b0) & (tile < b0 + nb))
            def _():
                r0 = pl.multiple_of((tile - b0) * tm, tm)
                pltpu.make_async_copy(x_hbm.at[pl.ds(r0, tm)], x_buf, sem).start()

    @pl.when(j == 0)
    def _():
        @pl.when(i == 0)
        def _():
            fetch(i)

        pltpu.make_async_copy(x_hbms[0].at[pl.ds(0, tm)], x_buf, sem).wait()
        for k in range(tm // INPROJ_NORM_ROWS):
            rows = slice(k * INPROJ_NORM_ROWS, (k + 1) * INPROJ_NORM_ROWS)
            mod = mods_ref[_mod_group(i * (tm // INPROJ_NORM_ROWS) + k, INPROJ_NORM_ROWS)]
            h = _rms(x_buf[rows, :], g_ref[...]) * (1.0 + mod[1:2, :]) + mod[0:1, :]
            h_scr[rows, :] = h.astype(BF16)

        @pl.when(i + 1 < pl.num_programs(0))
        def _():
            fetch(i + 1)

        n_dt = wdt_ref.shape[0]
        wdt = jnp.concatenate([wdt_ref[...], jnp.zeros((128 - n_dt, D_MODEL), F32)], axis=0)
        dt_ref[...] = _dot_nt(h_scr[...], wdt.astype(BF16))

    proj_ref[...] = _dot_nt(h_scr[...], w_ref[...].astype(BF16))


_ANY = pl.BlockSpec(memory_space=pl.ANY)


def _inproj(x_segs, mods, g, w_in_t, layer, to_bf16):
    tm, tn = 2048, 512
    n_dt = w_in_t.shape[1] - N_MAIN
    nj = N_MAIN // tn
    seg_rows = tuple(a.shape[0] for a in x_segs)
    assert all(r % tm == 0 for r in seg_rows) and sum(seg_rows) == T_ALL
    assert (T_ALL // tm) * nj >= INPROJ_CAST_STEPS
    cast_in, cast_out, cast_shapes = _side_cast_specs(to_bf16, INPROJ_CAST_STEPS, lambda i, j: i * nj + j)
    outs = pl.pallas_call(
        functools.partial(_inproj_kernel, seg_rows=seg_rows, n_cast=len(to_bf16)),
        grid=(T_ALL // tm, nj),
        in_specs=[_ANY] * len(x_segs) + [
            pl.BlockSpec(mods.shape, lambda i, j: (0, 0, 0)),
            pl.BlockSpec((1, D_MODEL), lambda i, j: (0, 0)),
            pl.BlockSpec((None, tn, D_MODEL), lambda i, j: (layer, j, 0)),
            pl.BlockSpec((None, n_dt, D_MODEL), lambda i, j: (layer, N_MAIN // n_dt, 0)),
        ] + cast_in,
        out_specs=[
            pl.BlockSpec((tm, tn), lambda i, j: (i, j)),
            pl.BlockSpec((tm, 128), lambda i, j: (i, 0)),
        ] + cast_out,
        out_shape=[
            jax.ShapeDtypeStruct((T_ALL, N_MAIN), F32),
            jax.ShapeDtypeStruct((T_ALL, 128), F32),
        ] + cast_shapes,
        scratch_shapes=[pltpu.VMEM((tm, D_MODEL), F32), pltpu.VMEM((tm, D_MODEL), BF16),
                        pltpu.SemaphoreType.DMA(())],
        compiler_params=_cparams(("arbitrary", "arbitrary")),
        name="inproj",
    )(*x_segs, mods, g, w_in_t, w_in_t, *[part[0] for part in to_bf16])
    return outs[0], outs[1], outs[2:]


def _rope(x, cos, sin_signed):
    lane = lax.broadcasted_iota(jnp.int32, x.shape, 1)
    first = (lane // (ROT_HALF // 2)) % 2 == 0
    swapped = jnp.where(first, pltpu.roll(x, HEAD_DIM - ROT_HALF // 2, 1), pltpu.roll(x, ROT_HALF // 2, 1))
    return x * cos + swapped * sin_signed


N_SCORE_BUFS = 3


def _attn_kernel(*refs, nk_new, has_ctx, n_prev):
    if has_ctx:
        (q_ref, kv_ref, qg_ref, kg_ref, og_ref, ck_ref, cv_ref, cq_ref, sq_ref, ckk_ref, skk_ref,
         o_ref, kb_scr, vb_scr, o_scr, s_scr) = refs
    elif n_prev:
        (q_ref, kv_ref, qg_ref, kg_ref, og_ref, pk_ref, pv_ref, o_ref, ko_ref, vo_ref,
         kb_scr, vb_scr, o_scr, s_scr) = refs
    else:
        (q_ref, kv_ref, qg_ref, kg_ref, og_ref, o_ref, ko_ref, vo_ref, kb_scr, vb_scr, o_scr, s_scr) = refs

    @pl.when(pl.program_id(1) == 0)
    def _():
        if not has_ctx and n_prev:
            ko_ref[0:n_prev] = pk_ref[...]
            vo_ref[0:n_prev] = pv_ref[...]
        for g in range(N_KV_HEADS):
            sl = slice(g * HEAD_DIM, (g + 1) * HEAD_DIM)
            kn = _rms(kv_ref[:, sl], kg_ref[...])
            v = kv_ref[:, KV_DIM + g * HEAD_DIM:KV_DIM + (g + 1) * HEAD_DIM]
            vsl = slice(2 * g * HEAD_DIM, (2 * g + 1) * HEAD_DIM)
            vb_scr[0:nk_new, vsl] = v.astype(BF16)
            vb_scr[:, (2 * g + 1) * HEAD_DIM:(2 * g + 2) * HEAD_DIM] = jnp.ones((vb_scr.shape[0], HEAD_DIM), BF16)
            if has_ctx:
                kb_scr[0:nk_new, sl] = _rope(kn, ckk_ref[...], skk_ref[...]).astype(BF16)
                kb_scr[nk_new:, sl] = ck_ref[:, g, :].astype(BF16)
                vb_scr[nk_new:, vsl] = cv_ref[:, g, :].astype(BF16)
            else:
                kb_scr[:, sl] = kn.astype(BF16)
                ko_ref[n_prev, :, g, :] = kn
                vo_ref[n_prev, :, g, :] = v

    scale_log2e = HEAD_DIM ** -0.5 * 1.4426950408889634
    def scores(h):
        g = h // Q_PER_KV
        qn = _rms(q_ref[:, h * HEAD_DIM:(h + 1) * HEAD_DIM], qg_ref[...])
        if has_ctx:
            qn = _rope(qn, cq_ref[...], sq_ref[...])
        s_scr[h % N_SCORE_BUFS] = _dot_nt((qn * scale_log2e).astype(BF16), kb_scr[:, g * HEAD_DIM:(g + 1) * HEAD_DIM])

    for h in range(N_SCORE_BUFS - 1):
        scores(h)
    for h in range(N_Q_HEADS):
        if h + N_SCORE_BUFS - 1 < N_Q_HEADS:
            scores(h + N_SCORE_BUFS - 1)
        g = h // Q_PER_KV
        s = s_scr[h % N_SCORE_BUFS]
        e = jnp.exp2(s - jnp.max(s, axis=-1, keepdims=True))
        pv = _dot(e.astype(BF16), vb_scr[:, 2 * g * HEAD_DIM:(2 * g + 2) * HEAD_DIM])
        o_scr[:, h * HEAD_DIM:(h + 1) * HEAD_DIM] = pv[:, :HEAD_DIM] / pv[:, HEAD_DIM:]
    o_ref[...] = _rms(o_scr[...], og_ref[...]).astype(BF16)


def _attention_prompt(proj, row0, qg, kg, og, layer, prev_kv):
    n = SEQ
    blk0 = row0 // n
    kern = functools.partial(_attn_kernel, nk_new=n, has_ctx=False, n_prev=layer)
    vec = lambda w: pl.BlockSpec((1, w), lambda b, i: (0, 0))
    cache_spec = lambda k: pl.BlockSpec((None, k, n, N_KV_HEADS, HEAD_DIM), lambda b, i: (b, 0, 0, 0, 0))
    cache_shape = jax.ShapeDtypeStruct((BATCH, layer + 1, n, N_KV_HEADS, HEAD_DIM), F32)
    in_specs = [
        pl.BlockSpec((n, D_ATTN), lambda b, i: (blk0 + b, 0)),
        pl.BlockSpec((n, 2 * KV_DIM), lambda b, i: (blk0 + b, 2)),
        vec(HEAD_DIM), vec(HEAD_DIM), vec(D_ATTN),
    ]
    args = [proj, proj, qg, kg, og]
    if layer:
        in_specs += [cache_spec(layer), cache_spec(layer)]
        args += list(prev_kv)
    return pl.pallas_call(
        kern,
        grid=(BATCH, 1),
        in_specs=in_specs,
        out_specs=[pl.BlockSpec((n, D_ATTN), lambda b, i: (b, 0)), cache_spec(layer + 1), cache_spec(layer + 1)],
        out_shape=[jax.ShapeDtypeStruct((T_PROMPT, D_ATTN), BF16), cache_shape, cache_shape],
        scratch_shapes=[
            pltpu.VMEM((n, KV_DIM), BF16), pltpu.VMEM((n, 2 * KV_DIM), BF16), pltpu.VMEM((n, D_ATTN), F32),
            pltpu.VMEM((N_SCORE_BUFS, n, n), F32),
        ],
        compiler_params=_cparams(("parallel", "arbitrary")),
        name="attn_prompt",
    )(*args)


def _attention_sample(proj, row0, qg, kg, og, ck, cv, cos, sin_signed, layer):
    n, tq = DEC_SEQ, 512
    nq = n // tq
    nk = n + PAST_LEN
    kern = functools.partial(_attn_kernel, nk_new=n, has_ctx=True, n_prev=0)
    vec = lambda w: pl.BlockSpec((1, w), lambda b, i: (0, 0))
    q_blk0 = row0 // tq
    kv_blk0 = row0 // n
    return pl.pallas_call(
        kern,
        grid=(DEC_BATCH, nq),
        in_specs=[
            pl.BlockSpec((tq, D_ATTN), lambda b, i: (q_blk0 + b * nq + i, 0)),
            pl.BlockSpec((n, 2 * KV_DIM), lambda b, i: (kv_blk0 + b, 2)),
            vec(HEAD_DIM), vec(HEAD_DIM), vec(D_ATTN),
            pl.BlockSpec((None, None, PAST_LEN, N_KV_HEADS, HEAD_DIM), lambda b, i: (b, layer, 0, 0, 0)),
            pl.BlockSpec((None, None, PAST_LEN, N_KV_HEADS, HEAD_DIM), lambda b, i: (b, layer, 0, 0, 0)),
            pl.BlockSpec((tq, HEAD_DIM), lambda b, i: (i, 0)),
            pl.BlockSpec((tq, HEAD_DIM), lambda b, i: (i, 0)),
            pl.BlockSpec((n, HEAD_DIM), lambda b, i: (0, 0)),
            pl.BlockSpec((n, HEAD_DIM), lambda b, i: (0, 0)),
        ],
        out_specs=pl.BlockSpec((tq, D_ATTN), lambda b, i: (b * nq + i, 0)),
        out_shape=jax.ShapeDtypeStruct((T_SAMPLE, D_ATTN), BF16),
        scratch_shapes=[
            pltpu.VMEM((nk, KV_DIM), BF16), pltpu.VMEM((nk, 2 * KV_DIM), BF16), pltpu.VMEM((tq, D_ATTN), F32),
            pltpu.VMEM((N_SCORE_BUFS, tq, nk), F32),
        ],
        compiler_params=_cparams(("parallel", "arbitrary")),
        name="attn_sample",
    )(proj, proj, qg, kg, og, ck, cv, cos, sin_signed, cos, sin_signed)


def _conv_silu(x, w, b):
    n = x.shape[0]
    row = lax.broadcasted_iota(jnp.int32, (n, 1), 0)
    prev = jnp.where(row == 0, 0.0, pltpu.roll(x, 1, 0))
    nxt = jnp.where(row == n - 1, 0.0, pltpu.roll(x, n - 1, 0))
    return _silu(prev * w[0:1, :] + x * w[1:2, :] + nxt * w[2:3, :] + b)


def _softplus(x):
    return jnp.maximum(x, 0.0) + jnp.log1p(jnp.exp(-jnp.abs(x)))


def _ssd_kernel(*refs, n, has_init, n_prev, n_cast):
    refs = list(refs)
    (za_ref, zb_ref, xa_ref, xb_ref, bc_ref, dt_ref, cw_ref, cb_ref, dtb_ref, alog_ref, dsk_ref,
     ng_ref) = refs[:12]
    del refs[:12]
    if has_init:
        sf0_ref, sb0_ref = refs.pop(0), refs.pop(0)
    elif n_prev:
        psf_ref, psb_ref = refs.pop(0), refs.pop(0)
    cast_in = [refs.pop(0) for _ in range(n_cast)]
    y_ref = refs.pop(0)
    if not has_init:
        sf_ref, sb_ref = refs.pop(0), refs.pop(0)
    cast_out = [refs.pop(0) for _ in range(n_cast)]
    xc_scr, bcc_scr, dts_scr, xt_scr, yt_scr, s_scr = refs
    _side_cast(cast_in, cast_out)
    nc = n // CHUNK
    gw = HEADS_PER_GROUP * SSD_HEAD_DIM

    xc_scr[:, 0:gw] = _conv_silu(xa_ref[...], cw_ref[:, 0:gw], cb_ref[:, 0:gw])
    xc_scr[:, gw:] = _conv_silu(xb_ref[...], cw_ref[:, gw:2 * gw], cb_ref[:, gw:2 * gw])
    bcc_scr[...] = _conv_silu(bc_ref[...], cw_ref[:, 2 * gw:], cb_ref[:, 2 * gw:])
    dts_scr[...] = _softplus(dt_ref[...] + dtb_ref[...])
    for g in range(SSD_GROUPS):
        hs = slice(g * HEADS_PER_GROUP, (g + 1) * HEADS_PER_GROUP)
        if has_init:
            s_scr[0, g] = sf0_ref[hs].reshape(gw, D_STATE)
            s_scr[1, g] = sb0_ref[hs].reshape(gw, D_STATE)
        else:
            s_scr[0, g] = jnp.zeros((gw, D_STATE), F32)
            s_scr[1, g] = jnp.zeros((gw, D_STATE), F32)

    def to_channel_major(c, carry):
        rows = pl.ds(pl.multiple_of(c * CHUNK, CHUNK), CHUNK)
        xt_scr[c] = xc_scr[rows, :].T
        yt_scr[c] = jnp.zeros((D_SSD, CHUNK), F32)
        return carry

    lax.fori_loop(0, nc, to_channel_major, 0)

    a_row = -jnp.exp(alog_ref[...])
    ri = lax.broadcasted_iota(jnp.int32, (CHUNK, CHUNK), 0)
    ci = lax.broadcasted_iota(jnp.int32, (CHUNK, CHUNK), 1)
    lower, upper = ci <= ri, ci >= ri
    n_dirs_heads = 2 * SSD_HEADS

    def scan_chunk(dirn, c):
        tri = (lower if dirn == 0 else upper).astype(BF16)
        valid_st = upper if dirn == 0 else lower
        tri_t = valid_st.astype(BF16)
        row0 = dirn * SSD_HEADS
        rows = pl.ds(pl.multiple_of(c * CHUNK, CHUNK), CHUNK)
        dt = dts_scr[rows, :]
        d = dt * a_row
        d1, d2, d3 = _split3(d)
        cs = _dot(tri, d1) + _dot(tri, d2) + _dot(tri, d3)
        dt_t = dt.T[0:n_dirs_heads, :]
        e1, e2, e3 = _split3(d.T[0:n_dirs_heads, :])
        cs_t = _dot(e1, tri_t) + _dot(e2, tri_t) + _dot(e3, tri_t)
        total = cs_t[:, CHUNK - 1:CHUNK] if dirn == 0 else cs_t[:, 0:1]
        e_in_t = jnp.exp(cs_t)
        to_end_t = jnp.exp(total - cs_t) * dt_t
        dec_t = jnp.broadcast_to(jnp.exp(total), (n_dirs_heads, D_STATE))
        for g in range(SSD_GROUPS):
            bm = bcc_scr[rows, g * D_STATE:(g + 1) * D_STATE].astype(BF16)
            cm = bcc_scr[rows, (SSD_GROUPS + g) * D_STATE:(SSD_GROUPS + g + 1) * D_STATE]
            g_st = _dot_nt(bm, cm.astype(BF16))
            c_nt = cm.T
            st = s_scr[dirn, g]
            xs_parts, dec_parts = [], []
            for hh in range(HEADS_PER_GROUP):
                h = g * HEADS_PER_GROUP + hh
                r = row0 + h
                ch = slice(h * SSD_HEAD_DIM, (h + 1) * SSD_HEAD_DIM)
                x_t = xt_scr[c, ch, :]
                diff = cs_t[r:r + 1, :] - cs[:, r:r + 1]
                a_st = (g_st * jnp.exp(jnp.where(valid_st, diff, -jnp.inf))).astype(BF16)
                c_e = (c_nt * e_in_t[r:r + 1, :]).astype(BF16)
                x_dt = (x_t * dt_t[r:r + 1, :]).astype(BF16)
                s_h = st[hh * SSD_HEAD_DIM:(hh + 1) * SSD_HEAD_DIM, :].astype(BF16)
                y_h = _dot(jnp.concatenate([x_dt, s_h], axis=1), jnp.concatenate([a_st, c_e], axis=0))
                yt_scr[c, ch, :] = yt_scr[c, ch, :] + y_h
                xs_parts.append((x_t * to_end_t[r:r + 1, :]).astype(BF16))
                dec_parts.append(jnp.broadcast_to(dec_t[r:r + 1, :], (SSD_HEAD_DIM, D_STATE)))
            ds = _dot(jnp.concatenate(xs_parts, axis=0), bm)
            s_scr[dirn, g] = st * jnp.concatenate(dec_parts, axis=0) + ds

    def body(i, carry):
        scan_chunk(0, i)
        scan_chunk(1, nc - 1 - i)
        return carry

    lax.fori_loop(0, nc, body, 0)

    def finish(c, carry):
        rows = pl.ds(pl.multiple_of(c * CHUNK, CHUNK), CHUNK)
        y = yt_scr[c].T + xc_scr[rows, :] * dsk_ref[...]
        ya = y[:, 0:gw] * _silu(za_ref[rows, :])
        yb = y[:, gw:] * _silu(zb_ref[rows, :])
        ms = (jnp.sum(ya * ya, axis=-1, keepdims=True) + jnp.sum(yb * yb, axis=-1, keepdims=True)) / D_SSD
        inv = lax.rsqrt(ms + EPS)
        y_ref[rows, 0:gw] = (ya * inv * ng_ref[:, 0:gw]).astype(BF16)
        y_ref[rows, gw:] = (yb * inv * ng_ref[:, gw:]).astype(BF16)
        return carry

    lax.fori_loop(0, nc, finish, 0)

    if not has_init:
        if n_prev:
            sf_ref[0:n_prev] = psf_ref[...]
            sb_ref[0:n_prev] = psb_ref[...]
        for g in range(SSD_GROUPS):
            hs = slice(g * HEADS_PER_GROUP, (g + 1) * HEADS_PER_GROUP)
            sf_ref[n_prev, hs] = s_scr[0, g].reshape(HEADS_PER_GROUP, SSD_HEAD_DIM, D_STATE)
            sb_ref[n_prev, hs] = s_scr[1, g].reshape(HEADS_PER_GROUP, SSD_HEAD_DIM, D_STATE)


def _ssd(proj, dt_raw, p, n, nb, row_blk0, layer, init, prev_states, to_bf16=()):
    has_init = init is not None
    n_prev = 0 if has_init else layer
    kern = functools.partial(_ssd_kernel, n=n, has_init=has_init, n_prev=n_prev, n_cast=len(to_bf16))
    cast_in, cast_out, cast_shapes = _side_cast_specs(to_bf16, nb, lambda b: b)
    col = lambda cb: pl.BlockSpec((n, 512), lambda b: (row_blk0 + b, cb))
    vec = lambda r, w: pl.BlockSpec((r, w), lambda b: (0, 0))
    layers_spec = lambda k: pl.BlockSpec((None, k, SSD_HEADS, SSD_HEAD_DIM, D_STATE), lambda b: (b, 0, 0, 0, 0))
    in_specs = [
        col(3), col(4), col(5), col(6), col(7),
        pl.BlockSpec((n, 128), lambda b: (row_blk0 + b, 0)),
        vec(3, CONV_DIM), vec(1, CONV_DIM), vec(1, 128), vec(1, 128), vec(1, D_SSD), vec(1, D_SSD),
    ]
    args = [proj, proj, proj, proj, proj, dt_raw, p['conv_w'], p['conv_b'], p['dt_bias'], p['a_log'],
            p['d_skip'], p['ssd_norm_g']]
    y_spec = pl.BlockSpec((n, D_SSD), lambda b: (b, 0))
    y_shape = jax.ShapeDtypeStruct((nb * n, D_SSD), BF16)
    if has_init:
        init_spec = pl.BlockSpec((None, None, SSD_HEADS, SSD_HEAD_DIM, D_STATE), lambda b: (b, layer, 0, 0, 0))
        in_specs += [init_spec, init_spec]
        args += list(init)
        out_specs, out_shape = [y_spec], [y_shape]
    else:
        if n_prev:
            in_specs += [layers_spec(n_prev), layers_spec(n_prev)]
            args += list(prev_states)
        st_shape = jax.ShapeDtypeStruct((nb, layer + 1, SSD_HEADS, SSD_HEAD_DIM, D_STATE), F32)
        out_specs = [y_spec, layers_spec(layer + 1), layers_spec(layer + 1)]
        out_shape = [y_shape, st_shape, st_shape]
    return pl.pallas_call(
        kern,
        grid=(nb,),
        in_specs=in_specs + cast_in,
        out_specs=out_specs + cast_out,
        out_shape=out_shape + cast_shapes,
        scratch_shapes=[
            pltpu.VMEM((n, D_SSD), F32), pltpu.VMEM((n, 512), F32), pltpu.VMEM((n, 128), F32),
            pltpu.VMEM((n // CHUNK, D_SSD, CHUNK), F32), pltpu.VMEM((n // CHUNK, D_SSD, CHUNK), F32),
            pltpu.VMEM((2, SSD_GROUPS, 512, D_STATE), F32),
        ],
        compiler_params=_cparams(("parallel",)),
        name="ssd_sample" if has_init else "ssd_prompt",
    )(*args, *[part[0] for part in to_bf16])


def _seg_specs(segs, tm):
    specs, bounds, start = [], [], 0
    for a in segs:
        b0, nblk = start // tm, a.shape[0] // tm
        specs.append(pl.BlockSpec((tm, a.shape[1]), lambda i, b0=b0, nblk=nblk: (jnp.clip(i - b0, 0, nblk - 1), 0)))
        bounds.append(b0)
        start += a.shape[0]
    return specs, tuple(bounds)


def _seg_pick(refs, bounds):
    i = pl.program_id(0)
    v = refs[0][...]
    for ref, b0 in zip(refs[1:], bounds[1:]):
        v = jnp.where(i >= b0, ref[...], v)
    return v


def _outproj_kernel(*refs, with_router, o_bounds, y_bounds, x_bounds):
    refs = list(refs)
    o_refs = [refs.pop(0) for _ in o_bounds]
    y_refs = [refs.pop(0) for _ in y_bounds]
    x_refs = [refs.pop(0) for _ in x_bounds]
    if with_router:
        mod_ref, g_ref, w_ref, rw_ref, xo_ref, h_ref, meta_ref, cnt_ref, carry_scr = refs
    else:
        mod_ref, g_ref, w_ref, xo_ref, h_ref = refs
    tm = xo_ref.shape[0]
    a = jnp.concatenate([_seg_pick(o_refs, o_bounds), _seg_pick(y_refs, y_bounds)], axis=1)
    xn = _seg_pick(x_refs, x_bounds) + mod_ref[2:3, :] * _dot(a, w_ref[...])
    xo_ref[...] = xn
    h = _rms(xn, g_ref[...]) * (1.0 + mod_ref[4:5, :]) + mod_ref[3:4, :]
    h_ref[...] = h.astype(h_ref.dtype)
    if with_router:

        @pl.when(pl.program_id(0) == 0)
        def _():
            carry_scr[...] = jnp.zeros_like(carry_scr)

        h1, h2, _ = _split3(h)
        w1, w2, _ = _split3(rw_ref[...])
        logits = _dot_nt(w1, h1) + _dot_nt(w2, h1) + _dot_nt(w1, h2)
        row = lax.broadcasted_iota(jnp.int32, logits.shape, 0)
        logits = jnp.where(row < N_EXPERTS, logits, -jnp.inf)
        e = jnp.exp(logits - jnp.max(logits, axis=0, keepdims=True))
        probs = e / jnp.sum(e, axis=0, keepdims=True)
        p1 = jnp.max(probs, axis=0, keepdims=True)
        i1 = jnp.min(jnp.where(probs == p1, row, 16), axis=0, keepdims=True)
        rest = jnp.where(row == i1, -1.0, probs)
        p2 = jnp.max(rest, axis=0, keepdims=True)
        i2 = jnp.min(jnp.where(rest == p2, row, 16), axis=0, keepdims=True)
        hit1, hit2 = row == i1, row == i2
        onehot = jnp.where(hit1 | hit2, 1.0, 0.0)
        ti = lax.broadcasted_iota(jnp.int32, (tm, tm), 0)
        tj = lax.broadcasted_iota(jnp.int32, (tm, tm), 1)
        before = jnp.where(ti < tj, 1.0, 0.0).astype(BF16)
        rank = carry_scr[:, 0:1] + _dot(onehot.astype(BF16), before)
        r1 = jnp.sum(jnp.where(hit1, rank, 0.0), axis=0, keepdims=True)
        r2 = jnp.sum(jnp.where(hit2, rank, 0.0), axis=0, keepdims=True)
        carry_scr[...] = carry_scr[...] + jnp.sum(onehot, axis=1, keepdims=True)
        cnt_ref[...] = carry_scr[...]
        r8 = lax.broadcasted_iota(jnp.int32, (8, tm), 0)
        vals = [p1 / (p1 + p2), p2 / (p1 + p2), i1.astype(F32), i2.astype(F32), r1, r2]
        meta = jnp.zeros((8, tm), F32)
        for k, v in enumerate(vals):
            meta = jnp.where(r8 == k, v, meta)
        meta_ref[...] = meta


def _outproj(o_segs, y_segs, x_segs, mods, g, w_out, router_wt):
    tm = 512
    with_router = router_wt is not None
    o_specs, o_bounds = _seg_specs(o_segs, tm)
    y_specs, y_bounds = _seg_specs(y_segs, tm)
    x_specs, x_bounds = _seg_specs(x_segs, tm)
    kern = functools.partial(_outproj_kernel, with_router=with_router, o_bounds=o_bounds, y_bounds=y_bounds,
                             x_bounds=x_bounds)
    in_specs = o_specs + y_specs + x_specs + [
        pl.BlockSpec((None, 6, D_MODEL), lambda i: (_mod_group(i, tm), 0, 0)),
        pl.BlockSpec((1, D_MODEL), lambda i: (0, 0)),
        pl.BlockSpec((D_MODEL, D_MODEL), lambda i: (0, 0), pipeline_mode=pl.Buffered(1)),
    ]
    args = list(o_segs) + list(y_segs) + list(x_segs) + [mods, g, w_out]
    row_spec = pl.BlockSpec((tm, D_MODEL), lambda i: (i, 0))
    out_specs = [row_spec, row_spec]
    out_shape = [jax.ShapeDtypeStruct((T_ALL, D_MODEL), F32),
                 jax.ShapeDtypeStruct((T_ALL, D_MODEL), F32 if with_router else BF16)]
    scratch = []
    if with_router:
        in_specs.append(pl.BlockSpec((16, D_MODEL), lambda i: (0, 0)))
        args.append(router_wt)
        out_specs += [pl.BlockSpec((8, tm), lambda i: (0, i)), pl.BlockSpec((16, 128), lambda i: (0, 0))]
        out_shape += [jax.ShapeDtypeStruct((8, T_ALL), F32), jax.ShapeDtypeStruct((16, 128), F32)]
        scratch = [pltpu.VMEM((16, 128), F32)]
    return pl.pallas_call(
        kern,
        grid=(T_ALL // tm,),
        in_specs=in_specs,
        out_specs=out_specs,
        out_shape=out_shape,
        scratch_shapes=scratch,
        compiler_params=_cparams(("arbitrary",)),
        name="outproj_router" if with_router else "outproj",
    )(*args)


MOE_ROWS = 2 * T_ALL
MOE_TILE = 256
MOE_TILES = MOE_ROWS // MOE_TILE
MOE_VISITS = MOE_TILES + N_EXPERTS - 1


def _row_copy(src, s, dst, d, sem):
    return pltpu.make_async_copy(src.at[pl.ds(s, 1)], dst.at[pl.ds(d, 1)], sem)


def _dispatch_kernel(p1_ref, p2_ref, h_ref, xs_ref, sem):
    tm = h_ref.shape[0]

    def issue(r, c):
        _row_copy(h_ref, r, xs_ref, p1_ref[0, 0, r], sem.at[0]).start()
        _row_copy(h_ref, r, xs_ref, p2_ref[0, 0, r], sem.at[1]).start()
        return c

    lax.fori_loop(0, tm, issue, 0, unroll=8)
    pltpu.make_async_copy(h_ref, xs_ref.at[pl.ds(0, tm)], sem.at[0]).wait()
    pltpu.make_async_copy(h_ref, xs_ref.at[pl.ds(0, tm)], sem.at[1]).wait()


def _dispatch(h, pos1, pos2):
    tm = 512
    nt = T_ALL // tm
    idx = lambda: pl.BlockSpec((1, 1, tm), lambda i: (i, 0, 0), memory_space=pltpu.SMEM)
    return pl.pallas_call(
        _dispatch_kernel,
        grid=(nt,),
        in_specs=[idx(), idx(), pl.BlockSpec((tm, D_MODEL), lambda i: (i, 0))],
        out_specs=pl.BlockSpec(memory_space=pl.ANY),
        out_shape=jax.ShapeDtypeStruct((MOE_ROWS, D_MODEL), F32),
        scratch_shapes=[pltpu.SemaphoreType.DMA((2,))],
        compiler_params=_cparams(("arbitrary",)),
        name="moe_dispatch",
    )(pos1.reshape(nt, 1, tm), pos2.reshape(nt, 1, tm), h)


def _experts_kernel(vt_ref, ve_ref, nv_ref, lo_ref, hi_ref, xs_ref, wg_ref, wu_ref, wd_ref, y_ref):
    v = pl.program_id(0)

    @pl.when(v < nv_ref[0])
    def _():
        e = ve_ref[v]
        x = xs_ref[...].astype(BF16)
        hid = _silu(_dot(x, wg_ref[...])) * _dot(x, wu_ref[...])
        y = _dot(hid.astype(BF16), wd_ref[...])
        row = vt_ref[v] * MOE_TILE + lax.broadcasted_iota(jnp.int32, (MOE_TILE, 1), 0)
        mine = (row >= lo_ref[e]) & (row < hi_ref[e])
        first_visit = (v == 0) | (vt_ref[jnp.maximum(v - 1, 0)] != vt_ref[v])

        @pl.when(first_visit)
        def _():
            y_ref[...] = jnp.where(mine, y, 0.0)

        @pl.when(jnp.logical_not(first_visit))
        def _():
            y_ref[...] = jnp.where(mine, y, y_ref[...])


def _experts(xs, wg, wu, wd, vt, ve, nv, lo, hi):
    grid_spec = pltpu.PrefetchScalarGridSpec(
        num_scalar_prefetch=5,
        grid=(MOE_VISITS,),
        in_specs=[
            pl.BlockSpec((MOE_TILE, D_MODEL), lambda v, vt, ve, nv, lo, hi: (vt[v], 0)),
            pl.BlockSpec((None, D_MODEL, F_EXPERT), lambda v, vt, ve, nv, lo, hi: (ve[v], 0, 0)),
            pl.BlockSpec((None, D_MODEL, F_EXPERT), lambda v, vt, ve, nv, lo, hi: (ve[v], 0, 0)),
            pl.BlockSpec((None, F_EXPERT, D_MODEL), lambda v, vt, ve, nv, lo, hi: (ve[v], 0, 0)),
        ],
        out_specs=pl.BlockSpec((MOE_TILE, D_MODEL), lambda v, vt, ve, nv, lo, hi: (vt[v], 0)),
    )
    return pl.pallas_call(
        _experts_kernel,
        grid_spec=grid_spec,
        out_shape=jax.ShapeDtypeStruct((MOE_ROWS, D_MODEL), F32),
        compiler_params=_cparams(("arbitrary",)),
        name="moe_experts",
    )(vt, ve, nv, lo, hi, xs, wg, wu, wd)


def _combine_kernel(p1c_ref, p2c_ref, p1n_ref, p2n_ref, y_hbm, x_ref, mod_ref, gate_ref, fg_ref,
                    outp_ref, outs_ref, ya_buf, yb_buf, sem):
    i = pl.program_id(0)
    n = pl.num_programs(0)
    tm = x_ref.shape[0]
    slot = i % 2

    def gather(pa_ref, pb_ref, s):
        def issue(r, c):
            _row_copy(y_hbm, pa_ref[0, 0, r], ya_buf.at[s], r, sem.at[0, s]).start()
            _row_copy(y_hbm, pb_ref[0, 0, r], yb_buf.at[s], r, sem.at[1, s]).start()
            return c

        lax.fori_loop(0, tm, issue, 0, unroll=8)

    @pl.when(i == 0)
    def _():
        gather(p1c_ref, p2c_ref, 0)

    @pl.when(i + 1 < n)
    def _():
        gather(p1n_ref, p2n_ref, 1 - slot)

    pltpu.make_async_copy(y_hbm.at[pl.ds(0, tm)], ya_buf.at[slot], sem.at[0, slot]).wait()
    pltpu.make_async_copy(y_hbm.at[pl.ds(0, tm)], yb_buf.at[slot], sem.at[1, slot]).wait()
    g = gate_ref[...]
    mix = g[:, 0:1] * ya_buf[slot] + g[:, 1:2] * yb_buf[slot]
    xo = _rms(x_ref[...] + mod_ref[5:6, :] * mix, fg_ref[...])

    @pl.when(i < T_PROMPT // tm)
    def _():
        outp_ref[...] = xo

    @pl.when(i >= T_PROMPT // tm)
    def _():
        outs_ref[...] = xo


def _combine(y, x, mods, gate_cols, pos1, pos2, final_g):
    tm = 256
    nt = T_ALL // tm
    ntp = T_PROMPT // tm
    cur = lambda: pl.BlockSpec((1, 1, tm), lambda i: (i, 0, 0), memory_space=pltpu.SMEM)
    nxt = lambda: pl.BlockSpec((1, 1, tm), lambda i: (jnp.minimum(i + 1, nt - 1), 0, 0), memory_space=pltpu.SMEM)
    p1, p2 = pos1.reshape(nt, 1, tm), pos2.reshape(nt, 1, tm)
    return pl.pallas_call(
        _combine_kernel,
        grid=(nt,),
        in_specs=[
            cur(), cur(), nxt(), nxt(),
            pl.BlockSpec(memory_space=pl.ANY),
            pl.BlockSpec((tm, D_MODEL), lambda i: (i, 0)),
            pl.BlockSpec((None, 6, D_MODEL), lambda i: (_mod_group(i, tm), 0, 0)),
            pl.BlockSpec((tm, 128), lambda i: (i, 0)),
            pl.BlockSpec((1, D_MODEL), lambda i: (0, 0)),
        ],
        out_specs=[
            pl.BlockSpec((tm, D_MODEL), lambda i: (jnp.minimum(i, ntp - 1), 0)),
            pl.BlockSpec((tm, D_MODEL), lambda i: (jnp.maximum(i - ntp, 0), 0)),
        ],
        out_shape=[
            jax.ShapeDtypeStruct((T_PROMPT, D_MODEL), F32), jax.ShapeDtypeStruct((T_SAMPLE, D_MODEL), F32),
        ],
        scratch_shapes=[
            pltpu.VMEM((2, tm, D_MODEL), F32), pltpu.VMEM((2, tm, D_MODEL), F32),
            pltpu.SemaphoreType.DMA((2, 2)),
        ],
        compiler_params=_cparams(("arbitrary",)),
        name="moe_combine",
    )(p1, p2, p1, p2, y, x, mods, gate_cols, final_g)


def _route_plan(meta, counts):
    i1, i2 = meta[2].astype(jnp.int32), meta[3].astype(jnp.int32)
    r1, r2 = meta[4].astype(jnp.int32), meta[5].astype(jnp.int32)
    cnt = counts[:N_EXPERTS, 0].astype(jnp.int32)
    hi = jnp.cumsum(cnt)
    lo = hi - cnt
    ex = jnp.arange(N_EXPERTS, dtype=jnp.int32)
    pos1 = jnp.sum(jnp.where(i1[:, None] == ex[None, :], lo[None, :], 0), axis=1) + r1
    pos2 = jnp.sum(jnp.where(i2[:, None] == ex[None, :], lo[None, :], 0), axis=1) + r2
    first_tile = lo // MOE_TILE
    n_vis_e = jnp.where(cnt > 0, (hi - 1) // MOE_TILE - first_tile + 1, 0)
    vis_hi = jnp.cumsum(n_vis_e)
    vis_lo = vis_hi - n_vis_e
    nv = vis_hi[-1]
    v = jnp.minimum(jnp.arange(MOE_VISITS, dtype=jnp.int32), nv - 1)
    ve = jnp.minimum(jnp.sum(v[:, None] >= vis_hi[None, :], axis=1), N_EXPERTS - 1).astype(jnp.int32)
    pick = lambda tab: jnp.sum(jnp.where(ve[:, None] == ex[None, :], tab[None, :], 0), axis=1)
    vt = (pick(first_tile) + v - pick(vis_lo)).astype(jnp.int32)
    return pos1, pos2, vt, ve, nv.reshape(1).astype(jnp.int32), lo.astype(jnp.int32), hi.astype(jnp.int32)


def _ffn_kernel(*refs, n_cast):
    h_ref, x_hbm, mod_ref, wg_ref, wu_ref, wd_ref = refs[:6]
    cast_in = refs[6:6 + n_cast]
    out_ref = refs[6 + n_cast]
    cast_out = refs[7 + n_cast:7 + 2 * n_cast]
    x_buf, sem = refs[7 + 2 * n_cast:]
    _side_cast(cast_in, cast_out)
    i, f = pl.program_id(0), pl.program_id(1)
    tm = h_ref.shape[0]
    x_copy = pltpu.make_async_copy(x_hbm.at[pl.ds(pl.multiple_of(i * tm, tm), tm)], x_buf, sem)

    @pl.when(f == 0)
    def _():
        x_copy.start()
        out_ref[...] = jnp.zeros_like(out_ref)

    h = h_ref[...]
    hid = _silu(_dot(h, wg_ref[...])) * _dot(h, wu_ref[...])
    out_ref[...] += _dot(hid.astype(BF16), wd_ref[...])

    @pl.when(f == pl.num_programs(1) - 1)
    def _():
        x_copy.wait()
        out_ref[...] = x_buf[...] + mod_ref[5:6, :] * out_ref[...]


FFN_CAST_STEPS = 64


def _ffn(h, x, mods, wg, wu, wd, to_bf16):
    tm, tf = 1024, 512
    nf = F_DENSE // tf
    assert (T_ALL // tm) * nf >= FFN_CAST_STEPS

    cast_in, cast_out, cast_shapes = _side_cast_specs(to_bf16, FFN_CAST_STEPS, lambda i, f: i * nf + f)
    outs = pl.pallas_call(
        functools.partial(_ffn_kernel, n_cast=len(to_bf16)),
        grid=(T_ALL // tm, nf),
        in_specs=[
            pl.BlockSpec((tm, D_MODEL), lambda i, f: (i, 0)),
            _ANY,
            pl.BlockSpec((None, 6, D_MODEL), lambda i, f: (_mod_group(i, tm), 0, 0)),
            pl.BlockSpec((D_MODEL, tf), lambda i, f: (0, f)),
            pl.BlockSpec((D_MODEL, tf), lambda i, f: (0, f)),
            pl.BlockSpec((tf, D_MODEL), lambda i, f: (f, 0)),
        ] + cast_in,
        out_specs=[pl.BlockSpec((tm, D_MODEL), lambda i, f: (i, 0))] + cast_out,
        out_shape=[jax.ShapeDtypeStruct((T_ALL, D_MODEL), F32)] + cast_shapes,
        scratch_shapes=[pltpu.VMEM((tm, D_MODEL), F32), pltpu.SemaphoreType.DMA(())],
        compiler_params=_cparams(("arbitrary", "arbitrary")),
        name="dense_ffn",
    )(h, x, mods, wg, wu, wd, *[part[0] for part in to_bf16])
    return outs[0], outs[1:]


def _rope_tables():
    n = DEC_SEQ
    rows = n // GRID_W
    t_row = jnp.repeat(jnp.arange(rows, dtype=F32), GRID_W)
    t_col = jnp.tile(jnp.arange(GRID_W, dtype=F32), rows)
    inv = 1.0 / (ROPE_THETA ** (jnp.arange(0, ROT_HALF, 2, dtype=F32) / ROT_HALF))
    ar, ac = t_row[:, None] * inv, t_col[:, None] * inv
    cos = jnp.concatenate([jnp.cos(ar), jnp.cos(ar), jnp.cos(ac), jnp.cos(ac)], axis=-1)
    sin_signed = jnp.concatenate([-jnp.sin(ar), jnp.sin(ar), -jnp.sin(ac), jnp.sin(ac)], axis=-1)
    return cos, sin_signed


def _pad_lanes(v, width=128):
    return jnp.pad(v, ((0, 0), (0, width - v.shape[-1])))


def kernel(x_prompt, x_sample, c, cache_k, cache_v, state_ssm_fwd, state_ssm_bwd, c_ctx, ada_w, ada_b, norm1_g, norm2_g, w_in, q_norm_g, k_norm_g, conv_w, conv_b, a_log_fwd, a_log_bwd, dt_bias_fwd, dt_bias_bwd, d_skip, ssd_norm_g, attn_out_g, w_out, ffn_w_gate, ffn_w_up, ffn_w_down, router_w, moe_w_gate, moe_w_up, moe_w_down, final_norm_g):
    assert DEPTH % 2 == 0
    cond = jnp.concatenate([c_ctx[None, :], c, jnp.zeros((N_COND - 1 - DEC_BATCH, D_MODEL), F32)], axis=0)
    mods_all = _ada_mods(cond, ada_w, ada_b).reshape(DEPTH, N_COND, 6, D_MODEL)
    cos, sin_signed = _rope_tables()

    w_in_t = jnp.swapaxes(w_in, 1, 2)
    x_segs = [x_prompt.reshape(T_PROMPT, D_MODEL), x_sample.reshape(T_SAMPLE, D_MODEL)]
    kv, states = None, None
    for l in range(DEPTH):
        mods = mods_all[l]
        proj, dt_raw, (w_o,) = _inproj(x_segs, mods, norm1_g[l][None, :], w_in_t, l, [(w_out.reshape(DEPTH * D_MODEL, D_MODEL), l * D_MODEL, D_MODEL)])

        qg, kg, og = q_norm_g[l][None, :], k_norm_g[l][None, :], attn_out_g[l][None, :]
        o_p, k_all, v_all = _attention_prompt(proj, 0, qg, kg, og, l, kv)
        kv = (k_all, v_all)
        o_s = _attention_sample(proj, T_PROMPT, qg, kg, og, cache_k, cache_v, cos, sin_signed, l)

        p = {
            'conv_w': conv_w[l], 'conv_b': conv_b[l][None, :],
            'dt_bias': _pad_lanes(jnp.concatenate([dt_bias_fwd[l], dt_bias_bwd[l]])[None, :]),
            'a_log': _pad_lanes(jnp.concatenate([a_log_fwd[l], a_log_bwd[l]])[None, :]),
            'd_skip': jnp.repeat(d_skip[l], SSD_HEAD_DIM)[None, :],
            'ssd_norm_g': ssd_norm_g[l][None, :],
        }
        j = l // 2
        n_up, n_down = N_EXPERTS * D_MODEL, N_EXPERTS * F_EXPERT
        if l % 2 == 0:
            parts = [(ffn_w_gate.reshape(-1, F_DENSE), j * D_MODEL, D_MODEL),
                     (ffn_w_up.reshape(-1, F_DENSE), j * D_MODEL, D_MODEL),
                     (ffn_w_down.reshape(-1, D_MODEL), j * F_DENSE, F_DENSE)]
        else:
            parts = [(moe_w_down.reshape(-1, D_MODEL), j * n_down, n_down)]
        y_p, sf, sb, *mixer_w = _ssd(proj, dt_raw, p, SEQ, BATCH, 0, l, None, states, parts)
        states = (sf, sb)
        y_s, = _ssd(proj, dt_raw, p, DEC_SEQ, DEC_BATCH, T_PROMPT // DEC_SEQ, l, (state_ssm_fwd, state_ssm_bwd), None)

        g2 = norm2_g[l][None, :]
        if l % 2 == 0:
            x, h = _outproj([o_p, o_s], [y_p, y_s], x_segs, mods, g2, w_o, None)
            x, (eg, eu) = _ffn(h, x, mods, *mixer_w, [
                (moe_w_gate.reshape(-1, F_EXPERT), j * n_up, n_up),
                (moe_w_up.reshape(-1, F_EXPERT), j * n_up, n_up)])
            x_segs = [x]
        else:
            router_wt = jnp.pad(router_w[j].T, ((0, 16 - N_EXPERTS), (0, 0)))
            x, h, meta, counts = _outproj([o_p, o_s], [y_p, y_s], x_segs, mods, g2, w_o, router_wt)
            pos1, pos2, vt, ve, nv, lo, hi = _route_plan(meta, counts)
            xs = _dispatch(h, pos1, pos2)
            ys = _experts(xs, eg.reshape(N_EXPERTS, D_MODEL, F_EXPERT), eu.reshape(N_EXPERTS, D_MODEL, F_EXPERT),
                          mixer_w[0].reshape(N_EXPERTS, F_EXPERT, D_MODEL), vt, ve, nv, lo, hi)
            y_prompt, y_sample = _combine(ys, x, mods, _pad_lanes(meta[:2].T), pos1, pos2, final_norm_g[None, :])

    return (y_prompt.reshape(BATCH, SEQ, D_MODEL), y_sample.reshape(DEC_BATCH, DEC_SEQ, D_MODEL),
            kv[0], kv[1], states[0], states[1])
```

```python
import functools

import jax
import jax.numpy as jnp
from jax import lax
from jax.experimental import pallas as pl
from jax.experimental.pallas import tpu as pltpu

F32 = jnp.float32
BF16 = jnp.bfloat16

D_MODEL = 2048
BATCH = 16
SEQ = 256
DEPTH = 2
DEC_BATCH = 2
DEC_SEQ = 1024
PAST_LEN = 512
GRID_W = 64
D_ATTN = 1024
D_SSD = 1024
HEAD_DIM = 128
N_Q_HEADS = 8
N_KV_HEADS = 2
Q_PER_KV = 4
KV_DIM = 256
ROT_HALF = 64
ROPE_THETA = 10000.0
SSD_HEAD_DIM = 64
SSD_HEADS = 16
SSD_GROUPS = 2
HEADS_PER_GROUP = 8
D_STATE = 128
CONV_DIM = 1536
CHUNK = 128
N_MAIN = 4096
F_DENSE = 5632
N_EXPERTS = 8
F_EXPERT = 1024
EPS = 1e-6

T_PROMPT = BATCH * SEQ
T_SAMPLE = DEC_BATCH * DEC_SEQ
T_ALL = T_PROMPT + T_SAMPLE
N_COND = 16

VMEM_LIMIT = 58 * 1024 * 1024


def _cparams(sem):
    return pltpu.CompilerParams(dimension_semantics=sem, vmem_limit_bytes=VMEM_LIMIT)


def _mod_group(i, tm):
    return jnp.maximum(0, (i * tm - T_PROMPT + DEC_SEQ) // DEC_SEQ)


def _silu(x):
    return x * jax.nn.sigmoid(x)


def _rms(x, g):
    ms = jnp.mean(x * x, axis=-1, keepdims=True)
    return x * lax.rsqrt(ms + EPS) * g


def _dot(a, b):
    return jnp.dot(a, b, preferred_element_type=F32)


def _dot_nt(a, b):
    return lax.dot_general(a, b, (((1,), (1,)), ((), ())), preferred_element_type=F32)


def _split3(x):
    hi = x.astype(BF16)
    r1 = x - hi.astype(F32)
    mid = r1.astype(BF16)
    r2 = r1 - mid.astype(F32)
    return hi, mid, r2.astype(BF16)


ADA_CAST_STEPS = 16


def _ada_kernel(*refs, n_cast):
    c_ref, w_ref, b_ref = refs[:3]
    o_ref = refs[3 + n_cast]
    _side_cast(refs[3:3 + n_cast], refs[4 + n_cast:])
    s = _silu(c_ref[...]).astype(BF16)
    o_ref[...] = _dot(s, w_ref[...].astype(BF16)) + b_ref[...]


def _ada_mods(cond, ada_w, ada_b, to_bf16):
    tn = 1024
    n_out = 6 * D_MODEL
    nj = n_out // tn
    assert DEPTH * nj >= ADA_CAST_STEPS
    cast_in, cast_out, cast_shapes = _side_cast_specs(to_bf16, ADA_CAST_STEPS, lambda l, j: l * nj + j)
    outs = pl.pallas_call(
        functools.partial(_ada_kernel, n_cast=len(to_bf16)),
        grid=(DEPTH, nj),
        in_specs=[
            pl.BlockSpec((N_COND, D_MODEL), lambda l, j: (0, 0)),
            pl.BlockSpec((None, D_MODEL, tn), lambda l, j: (l, 0, j)),
            pl.BlockSpec((None, 1, tn), lambda l, j: (l, 0, j)),
        ] + cast_in,
        out_specs=[pl.BlockSpec((None, N_COND, tn), lambda l, j: (l, 0, j))] + cast_out,
        out_shape=[jax.ShapeDtypeStruct((DEPTH, N_COND, n_out), F32)] + cast_shapes,
        compiler_params=_cparams(("arbitrary", "arbitrary")),
        name="ada_mods",
    )(cond, ada_w, ada_b.reshape(DEPTH, 1, n_out), *[part[0] for part in to_bf16])
    return outs[0], outs[1:]


def _side_cast_specs(parts, n_steps, linear_step):
    in_specs, out_specs, out_shapes = [], [], []
    for a, row0, nrows, *lead in parts:
        rows = nrows // n_steps
        blk0 = row0 // rows
        step = lambda *ids: jnp.minimum(linear_step(*ids), n_steps - 1)
        if lead:
            in_specs.append(pl.BlockSpec((None, rows, a.shape[-1]),
                                         lambda *ids, blk0=blk0, k=lead[0]: (k, blk0 + step(*ids), 0)))
        else:
            in_specs.append(pl.BlockSpec((rows, a.shape[-1]), lambda *ids, blk0=blk0: (blk0 + step(*ids), 0)))
        out_specs.append(pl.BlockSpec((rows, a.shape[-1]), lambda *ids: (step(*ids), 0)))
        out_shapes.append(jax.ShapeDtypeStruct((nrows, a.shape[-1]), BF16))
    return in_specs, out_specs, out_shapes


def _side_cast(cast_in, cast_out):
    for src, dst in zip(cast_in, cast_out):
        dst[...] = src[...].astype(BF16)


INPROJ_NORM_ROWS = 1024
INPROJ_CAST_STEPS = 16


def _inproj_kernel(*refs, seg_rows, n_cast):
    n_seg = len(seg_rows)
    x_hbms = refs[:n_seg]
    mods_ref, g_ref, w_ref, wdt_ref = refs[n_seg:n_seg + 4]
    cast_in = refs[n_seg + 4:n_seg + 4 + n_cast]
    proj_ref, dt_ref = refs[n_seg + 4 + n_cast:n_seg + 6 + n_cast]
    cast_out = refs[n_seg + 6 + n_cast:n_seg + 6 + 2 * n_cast]
    x_buf, h_scr, sem = refs[n_seg + 6 + 2 * n_cast:]
    _side_cast(cast_in, cast_out)
    i, j = pl.program_id(0), pl.program_id(1)
    tm = x_buf.shape[0]

    def fetch(tile):
        start = 0
        for x_hbm, nrows in zip(x_hbms, seg_rows):
            b0, nb = start // tm, nrows // tm
            start += nrows

            @pl.when((tile >= b0) & (tile < b0 + nb))
            def _():
                r0 = pl.multiple_of((tile - b0) * tm, tm)
                pltpu.make_async_copy(x_hbm.at[pl.ds(r0, tm)], x_buf, sem).start()

    @pl.when(j == 0)
    def _():
        @pl.when(i == 0)
        def _():
            fetch(i)

        pltpu.make_async_copy(x_hbms[0].at[pl.ds(0, tm)], x_buf, sem).wait()
        for k in range(tm // INPROJ_NORM_ROWS):
            rows = slice(k * INPROJ_NORM_ROWS, (k + 1) * INPROJ_NORM_ROWS)
            mod = mods_ref[_mod_group(i * (tm // INPROJ_NORM_ROWS) + k, INPROJ_NORM_ROWS)]
            h = _rms(x_buf[rows, :], g_ref[...]) * (1.0 + mod[1:2, :]) + mod[0:1, :]
            h_scr[rows, :] = h.astype(BF16)

        @pl.when(i + 1 < pl.num_programs(0))
        def _():
            fetch(i + 1)

        n_dt = wdt_ref.shape[0]
        wdt = jnp.concatenate([wdt_ref[...], jnp.zeros((128 - n_dt, D_MODEL), F32)], axis=0)
        dt_ref[...] = _dot_nt(h_scr[...], wdt.astype(BF16))

    proj_ref[...] = _dot_nt(h_scr[...], w_ref[...])


_ANY = pl.BlockSpec(memory_space=pl.ANY)


def _inproj(x_segs, mods, g, w_main, w_in_t, layer, to_bf16):
    tm, tn = 2048, 512
    n_dt = w_in_t.shape[1] - N_MAIN
    nj = N_MAIN // tn
    seg_rows = tuple(a.shape[0] for a in x_segs)
    assert all(r % tm == 0 for r in seg_rows) and sum(seg_rows) == T_ALL
    assert (T_ALL // tm) * nj >= INPROJ_CAST_STEPS
    cast_in, cast_out, cast_shapes = _side_cast_specs(to_bf16, INPROJ_CAST_STEPS, lambda i, j: i * nj + j)
    outs = pl.pallas_call(
        functools.partial(_inproj_kernel, seg_rows=seg_rows, n_cast=len(to_bf16)),
        grid=(T_ALL // tm, nj),
        in_specs=[_ANY] * len(x_segs) + [
            pl.BlockSpec(mods.shape, lambda i, j: (0, 0, 0)),
            pl.BlockSpec((1, D_MODEL), lambda i, j: (0, 0)),
            pl.BlockSpec((tn, D_MODEL), lambda i, j: (j, 0)),
            pl.BlockSpec((None, n_dt, D_MODEL), lambda i, j: (layer, N_MAIN // n_dt, 0)),
        ] + cast_in,
        out_specs=[
            pl.BlockSpec((tm, tn), lambda i, j: (i, j)),
            pl.BlockSpec((tm, 128), lambda i, j: (i, 0)),
        ] + cast_out,
        out_shape=[
            jax.ShapeDtypeStruct((T_ALL, N_MAIN), F32),
            jax.ShapeDtypeStruct((T_ALL, 128), F32),
        ] + cast_shapes,
        scratch_shapes=[pltpu.VMEM((tm, D_MODEL), F32), pltpu.VMEM((tm, D_MODEL), BF16),
                        pltpu.SemaphoreType.DMA(())],
        compiler_params=_cparams(("arbitrary", "arbitrary")),
        name="inproj",
    )(*x_segs, mods, g, w_main, w_in_t, *[part[0] for part in to_bf16])
    return outs[0], outs[1], outs[2:]


def _rope(x, cos, sin_signed):
    lane = lax.broadcasted_iota(jnp.int32, x.shape, 1)
    first = (lane // (ROT_HALF // 2)) % 2 == 0
    swapped = jnp.where(first, pltpu.roll(x, HEAD_DIM - ROT_HALF // 2, 1), pltpu.roll(x, ROT_HALF // 2, 1))
    return x * cos + swapped * sin_signed


N_SCORE_BUFS = 3


def _attn_kernel(*refs, nk_new, has_ctx, n_prev):
    if has_ctx:
        (q_ref, kv_ref, qg_ref, kg_ref, og_ref, ck_ref, cv_ref, cq_ref, sq_ref, ckk_ref, skk_ref,
         o_ref, kb_scr, vb_scr, o_scr, s_scr) = refs
    elif n_prev:
        (q_ref, kv_ref, qg_ref, kg_ref, og_ref, pk_ref, pv_ref, o_ref, ko_ref, vo_ref,
         kb_scr, vb_scr, o_scr, s_scr) = refs
    else:
        (q_ref, kv_ref, qg_ref, kg_ref, og_ref, o_ref, ko_ref, vo_ref, kb_scr, vb_scr, o_scr, s_scr) = refs

    @pl.when(pl.program_id(1) == 0)
    def _():
        if not has_ctx and n_prev:
            ko_ref[0:n_prev] = pk_ref[...]
            vo_ref[0:n_prev] = pv_ref[...]
        for g in range(N_KV_HEADS):
            sl = slice(g * HEAD_DIM, (g + 1) * HEAD_DIM)
            kn = _rms(kv_ref[:, sl], kg_ref[...])
            v = kv_ref[:, KV_DIM + g * HEAD_DIM:KV_DIM + (g + 1) * HEAD_DIM]
            vsl = slice(2 * g * HEAD_DIM, (2 * g + 1) * HEAD_DIM)
            vb_scr[0:nk_new, vsl] = v.astype(BF16)
            vb_scr[:, (2 * g + 1) * HEAD_DIM:(2 * g + 2) * HEAD_DIM] = jnp.ones((vb_scr.shape[0], HEAD_DIM), BF16)
            if has_ctx:
                kb_scr[0:nk_new, sl] = _rope(kn, ckk_ref[...], skk_ref[...]).astype(BF16)
                kb_scr[nk_new:, sl] = ck_ref[:, g, :].astype(BF16)
                vb_scr[nk_new:, vsl] = cv_ref[:, g, :].astype(BF16)
            else:
                kb_scr[:, sl] = kn.astype(BF16)
                ko_ref[n_prev, :, g, :] = kn
                vo_ref[n_prev, :, g, :] = v

    scale_log2e = HEAD_DIM ** -0.5 * 1.4426950408889634
    def scores(h):
        g = h // Q_PER_KV
        qn = _rms(q_ref[:, h * HEAD_DIM:(h + 1) * HEAD_DIM], qg_ref[...])
        if has_ctx:
            qn = _rope(qn, cq_ref[...], sq_ref[...])
        s_scr[h % N_SCORE_BUFS] = _dot_nt((qn * scale_log2e).astype(BF16), kb_scr[:, g * HEAD_DIM:(g + 1) * HEAD_DIM])

    for h in range(N_SCORE_BUFS - 1):
        scores(h)
    for h in range(N_Q_HEADS):
        if h + N_SCORE_BUFS - 1 < N_Q_HEADS:
            scores(h + N_SCORE_BUFS - 1)
        g = h // Q_PER_KV
        s = s_scr[h % N_SCORE_BUFS]
        e = jnp.exp2(s - jnp.max(s, axis=-1, keepdims=True))
        pv = _dot(e.astype(BF16), vb_scr[:, 2 * g * HEAD_DIM:(2 * g + 2) * HEAD_DIM])
        o_scr[:, h * HEAD_DIM:(h + 1) * HEAD_DIM] = pv[:, :HEAD_DIM] / pv[:, HEAD_DIM:]
    o_ref[...] = _rms(o_scr[...], og_ref[...]).astype(BF16)


def _attention_prompt(proj, row0, qg, kg, og, layer, prev_kv):
    n = SEQ
    blk0 = row0 // n
    kern = functools.partial(_attn_kernel, nk_new=n, has_ctx=False, n_prev=layer)
    vec = lambda w: pl.BlockSpec((1, w), lambda b, i: (0, 0))
    cache_spec = lambda k: pl.BlockSpec((None, k, n, N_KV_HEADS, HEAD_DIM), lambda b, i: (b, 0, 0, 0, 0))
    cache_shape = jax.ShapeDtypeStruct((BATCH, layer + 1, n, N_KV_HEADS, HEAD_DIM), F32)
    in_specs = [
        pl.BlockSpec((n, D_ATTN), lambda b, i: (blk0 + b, 0)),
        pl.BlockSpec((n, 2 * KV_DIM), lambda b, i: (blk0 + b, 2)),
        vec(HEAD_DIM), vec(HEAD_DIM), vec(D_ATTN),
    ]
    args = [proj, proj, qg, kg, og]
    if layer:
        in_specs += [cache_spec(layer), cache_spec(layer)]
        args += list(prev_kv)
    return pl.pallas_call(
        kern,
        grid=(BATCH, 1),
        in_specs=in_specs,
        out_specs=[pl.BlockSpec((n, D_ATTN), lambda b, i: (b, 0)), cache_spec(layer + 1), cache_spec(layer + 1)],
        out_shape=[jax.ShapeDtypeStruct((T_PROMPT, D_ATTN), BF16), cache_shape, cache_shape],
        scratch_shapes=[
            pltpu.VMEM((n, KV_DIM), BF16), pltpu.VMEM((n, 2 * KV_DIM), BF16), pltpu.VMEM((n, D_ATTN), F32),
            pltpu.VMEM((N_SCORE_BUFS, n, n), F32),
        ],
        compiler_params=_cparams(("parallel", "arbitrary")),
        name="attn_prompt",
    )(*args)


def _attention_sample(proj, row0, qg, kg, og, ck, cv, cos, sin_signed, layer):
    n, tq = DEC_SEQ, 512
    nq = n // tq
    nk = n + PAST_LEN
    kern = functools.partial(_attn_kernel, nk_new=n, has_ctx=True, n_prev=0)
    vec = lambda w: pl.BlockSpec((1, w), lambda b, i: (0, 0))
    q_blk0 = row0 // tq
    kv_blk0 = row0 // n
    return pl.pallas_call(
        kern,
        grid=(DEC_BATCH, nq),
        in_specs=[
            pl.BlockSpec((tq, D_ATTN), lambda b, i: (q_blk0 + b * nq + i, 0)),
            pl.BlockSpec((n, 2 * KV_DIM), lambda b, i: (kv_blk0 + b, 2)),
            vec(HEAD_DIM), vec(HEAD_DIM), vec(D_ATTN),
            pl.BlockSpec((None, None, PAST_LEN, N_KV_HEADS, HEAD_DIM), lambda b, i: (b, layer, 0, 0, 0)),
            pl.BlockSpec((None, None, PAST_LEN, N_KV_HEADS, HEAD_DIM), lambda b, i: (b, layer, 0, 0, 0)),
            pl.BlockSpec((tq, HEAD_DIM), lambda b, i: (i, 0)),
            pl.BlockSpec((tq, HEAD_DIM), lambda b, i: (i, 0)),
            pl.BlockSpec((n, HEAD_DIM), lambda b, i: (0, 0)),
            pl.BlockSpec((n, HEAD_DIM), lambda b, i: (0, 0)),
        ],
        out_specs=pl.BlockSpec((tq, D_ATTN), lambda b, i: (b * nq + i, 0)),
        out_shape=jax.ShapeDtypeStruct((T_SAMPLE, D_ATTN), BF16),
        scratch_shapes=[
            pltpu.VMEM((nk, KV_DIM), BF16), pltpu.VMEM((nk, 2 * KV_DIM), BF16), pltpu.VMEM((tq, D_ATTN), F32),
            pltpu.VMEM((N_SCORE_BUFS, tq, nk), F32),
        ],
        compiler_params=_cparams(("parallel", "arbitrary")),
        name="attn_sample",
    )(proj, proj, qg, kg, og, ck, cv, cos, sin_signed, cos, sin_signed)


def _conv_silu(x, w, b):
    n = x.shape[0]
    row = lax.broadcasted_iota(jnp.int32, (n, 1), 0)
    prev = jnp.where(row == 0, 0.0, pltpu.roll(x, 1, 0))
    nxt = jnp.where(row == n - 1, 0.0, pltpu.roll(x, n - 1, 0))
    return _silu(prev * w[0:1, :] + x * w[1:2, :] + nxt * w[2:3, :] + b)


def _softplus(x):
    return jnp.maximum(x, 0.0) + jnp.log1p(jnp.exp(-jnp.abs(x)))


def _ssd_kernel(*refs, n, has_init, n_prev, n_cast):
    refs = list(refs)
    (za_ref, zb_ref, xa_ref, xb_ref, bc_ref, dt_ref, cw_ref, cb_ref, dtb_ref, alog_ref, dsk_ref,
     ng_ref) = refs[:12]
    del refs[:12]
    if has_init:
        sf0_ref, sb0_ref = refs.pop(0), refs.pop(0)
    elif n_prev:
        psf_ref, psb_ref = refs.pop(0), refs.pop(0)
    cast_in = [refs.pop(0) for _ in range(n_cast)]
    y_ref = refs.pop(0)
    if not has_init:
        sf_ref, sb_ref = refs.pop(0), refs.pop(0)
    cast_out = [refs.pop(0) for _ in range(n_cast)]
    xc_scr, bcc_scr, dts_scr, xt_scr, yt_scr, s_scr = refs
    _side_cast(cast_in, cast_out)
    nc = n // CHUNK
    gw = HEADS_PER_GROUP * SSD_HEAD_DIM

    xc_scr[:, 0:gw] = _conv_silu(xa_ref[...], cw_ref[:, 0:gw], cb_ref[:, 0:gw])
    xc_scr[:, gw:] = _conv_silu(xb_ref[...], cw_ref[:, gw:2 * gw], cb_ref[:, gw:2 * gw])
    bcc_scr[...] = _conv_silu(bc_ref[...], cw_ref[:, 2 * gw:], cb_ref[:, 2 * gw:])
    dts_scr[...] = _softplus(dt_ref[...] + dtb_ref[...])
    for g in range(SSD_GROUPS):
        hs = slice(g * HEADS_PER_GROUP, (g + 1) * HEADS_PER_GROUP)
        if has_init:
            s_scr[0, g] = sf0_ref[hs].reshape(gw, D_STATE)
            s_scr[1, g] = sb0_ref[hs].reshape(gw, D_STATE)
        else:
            s_scr[0, g] = jnp.zeros((gw, D_STATE), F32)
            s_scr[1, g] = jnp.zeros((gw, D_STATE), F32)

    def to_channel_major(c, carry):
        rows = pl.ds(pl.multiple_of(c * CHUNK, CHUNK), CHUNK)
        xt_scr[c] = xc_scr[rows, :].T
        yt_scr[c] = jnp.zeros((D_SSD, CHUNK), F32)
        return carry

    lax.fori_loop(0, nc, to_channel_major, 0)

    a_row = -jnp.exp(alog_ref[...])
    ri = lax.broadcasted_iota(jnp.int32, (CHUNK, CHUNK), 0)
    ci = lax.broadcasted_iota(jnp.int32, (CHUNK, CHUNK), 1)
    lower, upper = ci <= ri, ci >= ri
    n_dirs_heads = 2 * SSD_HEADS

    def scan_chunk(dirn, c):
        tri = (lower if dirn == 0 else upper).astype(BF16)
        valid_st = upper if dirn == 0 else lower
        tri_t = valid_st.astype(BF16)
        row0 = dirn * SSD_HEADS
        rows = pl.ds(pl.multiple_of(c * CHUNK, CHUNK), CHUNK)
        dt = dts_scr[rows, :]
        d = dt * a_row
        d1, d2, d3 = _split3(d)
        cs = _dot(tri, d1) + _dot(tri, d2) + _dot(tri, d3)
        dt_t = dt.T[0:n_dirs_heads, :]
        e1, e2, e3 = _split3(d.T[0:n_dirs_heads, :])
        cs_t = _dot(e1, tri_t) + _dot(e2, tri_t) + _dot(e3, tri_t)
        total = cs_t[:, CHUNK - 1:CHUNK] if dirn == 0 else cs_t[:, 0:1]
        e_in_t = jnp.exp(cs_t)
        to_end_t = jnp.exp(total - cs_t) * dt_t
        dec_t = jnp.broadcast_to(jnp.exp(total), (n_dirs_heads, D_STATE))
        for g in range(SSD_GROUPS):
            bm = bcc_scr[rows, g * D_STATE:(g + 1) * D_STATE].astype(BF16)
            cm = bcc_scr[rows, (SSD_GROUPS + g) * D_STATE:(SSD_GROUPS + g + 1) * D_STATE]
            g_st = _dot_nt(bm, cm.astype(BF16))
            c_nt = cm.T
            st = s_scr[dirn, g]
            xs_parts, dec_parts = [], []
            for hh in range(HEADS_PER_GROUP):
                h = g * HEADS_PER_GROUP + hh
                r = row0 + h
                ch = slice(h * SSD_HEAD_DIM, (h + 1) * SSD_HEAD_DIM)
                x_t = xt_scr[c, ch, :]
                diff = cs_t[r:r + 1, :] - cs[:, r:r + 1]
                a_st = (g_st * jnp.exp(jnp.where(valid_st, diff, -jnp.inf))).astype(BF16)
                c_e = (c_nt * e_in_t[r:r + 1, :]).astype(BF16)
                x_dt = (x_t * dt_t[r:r + 1, :]).astype(BF16)
                s_h = st[hh * SSD_HEAD_DIM:(hh + 1) * SSD_HEAD_DIM, :].astype(BF16)
                y_h = _dot(jnp.concatenate([x_dt, s_h], axis=1), jnp.concatenate([a_st, c_e], axis=0))
                yt_scr[c, ch, :] = yt_scr[c, ch, :] + y_h
                xs_parts.append((x_t * to_end_t[r:r + 1, :]).astype(BF16))
                dec_parts.append(jnp.broadcast_to(dec_t[r:r + 1, :], (SSD_HEAD_DIM, D_STATE)))
            ds = _dot(jnp.concatenate(xs_parts, axis=0), bm)
            s_scr[dirn, g] = st * jnp.concatenate(dec_parts, axis=0) + ds

    def body(i, carry):
        scan_chunk(0, i)
        scan_chunk(1, nc - 1 - i)
        return carry

    lax.fori_loop(0, nc, body, 0)

    def finish(c, carry):
        rows = pl.ds(pl.multiple_of(c * CHUNK, CHUNK), CHUNK)
        y = yt_scr[c].T + xc_scr[rows, :] * dsk_ref[...]
        ya = y[:, 0:gw] * _silu(za_ref[rows, :])
        yb = y[:, gw:] * _silu(zb_ref[rows, :])
        ms = (jnp.sum(ya * ya, axis=-1, keepdims=True) + jnp.sum(yb * yb, axis=-1, keepdims=True)) / D_SSD
        inv = lax.rsqrt(ms + EPS)
        y_ref[rows, 0:gw] = (ya * inv * ng_ref[:, 0:gw]).astype(BF16)
        y_ref[rows, gw:] = (yb * inv * ng_ref[:, gw:]).astype(BF16)
        return carry

    lax.fori_loop(0, nc, finish, 0)

    if not has_init:
        if n_prev:
            sf_ref[0:n_prev] = psf_ref[...]
            sb_ref[0:n_prev] = psb_ref[...]
        for g in range(SSD_GROUPS):
            hs = slice(g * HEADS_PER_GROUP, (g + 1) * HEADS_PER_GROUP)
            sf_ref[n_prev, hs] = s_scr[0, g].reshape(HEADS_PER_GROUP, SSD_HEAD_DIM, D_STATE)
            sb_ref[n_prev, hs] = s_scr[1, g].reshape(HEADS_PER_GROUP, SSD_HEAD_DIM, D_STATE)


def _ssd(proj, dt_raw, p, n, nb, row_blk0, layer, init, prev_states, to_bf16=()):
    has_init = init is not None
    n_prev = 0 if has_init else layer
    kern = functools.partial(_ssd_kernel, n=n, has_init=has_init, n_prev=n_prev, n_cast=len(to_bf16))
    cast_in, cast_out, cast_shapes = _side_cast_specs(to_bf16, nb, lambda b: b)
    col = lambda cb: pl.BlockSpec((n, 512), lambda b: (row_blk0 + b, cb))
    vec = lambda r, w: pl.BlockSpec((r, w), lambda b: (0, 0))
    layers_spec = lambda k: pl.BlockSpec((None, k, SSD_HEADS, SSD_HEAD_DIM, D_STATE), lambda b: (b, 0, 0, 0, 0))
    in_specs = [
        col(3), col(4), col(5), col(6), col(7),
        pl.BlockSpec((n, 128), lambda b: (row_blk0 + b, 0)),
        vec(3, CONV_DIM), vec(1, CONV_DIM), vec(1, 128), vec(1, 128), vec(1, D_SSD), vec(1, D_SSD),
    ]
    args = [proj, proj, proj, proj, proj, dt_raw, p['conv_w'], p['conv_b'], p['dt_bias'], p['a_log'],
            p['d_skip'], p['ssd_norm_g']]
    y_spec = pl.BlockSpec((n, D_SSD), lambda b: (b, 0))
    y_shape = jax.ShapeDtypeStruct((nb * n, D_SSD), BF16)
    if has_init:
        init_spec = pl.BlockSpec((None, None, SSD_HEADS, SSD_HEAD_DIM, D_STATE), lambda b: (b, layer, 0, 0, 0))
        in_specs += [init_spec, init_spec]
        args += list(init)
        out_specs, out_shape = [y_spec], [y_shape]
    else:
        if n_prev:
            in_specs += [layers_spec(n_prev), layers_spec(n_prev)]
            args += list(prev_states)
        st_shape = jax.ShapeDtypeStruct((nb, layer + 1, SSD_HEADS, SSD_HEAD_DIM, D_STATE), F32)
        out_specs = [y_spec, layers_spec(layer + 1), layers_spec(layer + 1)]
        out_shape = [y_shape, st_shape, st_shape]
    return pl.pallas_call(
        kern,
        grid=(nb,),
        in_specs=in_specs + cast_in,
        out_specs=out_specs + cast_out,
        out_shape=out_shape + cast_shapes,
        scratch_shapes=[
            pltpu.VMEM((n, D_SSD), F32), pltpu.VMEM((n, 512), F32), pltpu.VMEM((n, 128), F32),
            pltpu.VMEM((n // CHUNK, D_SSD, CHUNK), F32), pltpu.VMEM((n // CHUNK, D_SSD, CHUNK), F32),
            pltpu.VMEM((2, SSD_GROUPS, 512, D_STATE), F32),
        ],
        compiler_params=_cparams(("parallel",)),
        name="ssd_sample" if has_init else "ssd_prompt",
    )(*args, *[part[0] for part in to_bf16])


def _seg_specs(segs, tm):
    specs, bounds, start = [], [], 0
    for a in segs:
        b0, nblk = start // tm, a.shape[0] // tm
        specs.append(pl.BlockSpec((tm, a.shape[1]), lambda i, b0=b0, nblk=nblk: (jnp.clip(i - b0, 0, nblk - 1), 0)))
        bounds.append(b0)
        start += a.shape[0]
    return specs, tuple(bounds)


def _seg_pick(refs, bounds):
    i = pl.program_id(0)
    v = refs[0][...]
    for ref, b0 in zip(refs[1:], bounds[1:]):
        v = jnp.where(i >= b0, ref[...], v)
    return v


def _outproj_kernel(*refs, with_router, o_bounds, y_bounds, x_bounds):
    refs = list(refs)
    o_refs = [refs.pop(0) for _ in o_bounds]
    y_refs = [refs.pop(0) for _ in y_bounds]
    x_refs = [refs.pop(0) for _ in x_bounds]
    if with_router:
        mod_ref, g_ref, w_ref, rw_ref, xo_ref, h_ref, meta_ref, cnt_ref, carry_scr = refs
    else:
        mod_ref, g_ref, w_ref, xo_ref, h_ref = refs
    tm = xo_ref.shape[0]
    a = jnp.concatenate([_seg_pick(o_refs, o_bounds), _seg_pick(y_refs, y_bounds)], axis=1)
    xn = _seg_pick(x_refs, x_bounds) + mod_ref[2:3, :] * _dot(a, w_ref[...])
    xo_ref[...] = xn
    h = _rms(xn, g_ref[...]) * (1.0 + mod_ref[4:5, :]) + mod_ref[3:4, :]
    h_ref[...] = h.astype(h_ref.dtype)
    if with_router:

        @pl.when(pl.program_id(0) == 0)
        def _():
            carry_scr[...] = jnp.zeros_like(carry_scr)

        h1, h2, _ = _split3(h)
        w1, w2, _ = _split3(rw_ref[...])
        logits = _dot_nt(w1, h1) + _dot_nt(w2, h1) + _dot_nt(w1, h2)
        row = lax.broadcasted_iota(jnp.int32, logits.shape, 0)
        logits = jnp.where(row < N_EXPERTS, logits, -jnp.inf)
        e = jnp.exp(logits - jnp.max(logits, axis=0, keepdims=True))
        probs = e / jnp.sum(e, axis=0, keepdims=True)
        p1 = jnp.max(probs, axis=0, keepdims=True)
        i1 = jnp.min(jnp.where(probs == p1, row, 16), axis=0, keepdims=True)
        rest = jnp.where(row == i1, -1.0, probs)
        p2 = jnp.max(rest, axis=0, keepdims=True)
        i2 = jnp.min(jnp.where(rest == p2, row, 16), axis=0, keepdims=True)
        hit1, hit2 = row == i1, row == i2
        onehot = jnp.where(hit1 | hit2, 1.0, 0.0)
        ti = lax.broadcasted_iota(jnp.int32, (tm, tm), 0)
        tj = lax.broadcasted_iota(jnp.int32, (tm, tm), 1)
        before = jnp.where(ti < tj, 1.0, 0.0).astype(BF16)
        rank = carry_scr[:, 0:1] + _dot(onehot.astype(BF16), before)
        r1 = jnp.sum(jnp.where(hit1, rank, 0.0), axis=0, keepdims=True)
        r2 = jnp.sum(jnp.where(hit2, rank, 0.0), axis=0, keepdims=True)
        carry_scr[...] = carry_scr[...] + jnp.sum(onehot, axis=1, keepdims=True)
        cnt_ref[...] = carry_scr[...]
        r8 = lax.broadcasted_iota(jnp.int32, (8, tm), 0)
        vals = [p1 / (p1 + p2), p2 / (p1 + p2), i1.astype(F32), i2.astype(F32), r1, r2]
        meta = jnp.zeros((8, tm), F32)
        for k, v in enumerate(vals):
            meta = jnp.where(r8 == k, v, meta)
        meta_ref[...] = meta


def _outproj(o_segs, y_segs, x_segs, mods, g, w_out, router_wt):
    tm = 512
    with_router = router_wt is not None
    o_specs, o_bounds = _seg_specs(o_segs, tm)
    y_specs, y_bounds = _seg_specs(y_segs, tm)
    x_specs, x_bounds = _seg_specs(x_segs, tm)
    kern = functools.partial(_outproj_kernel, with_router=with_router, o_bounds=o_bounds, y_bounds=y_bounds,
                             x_bounds=x_bounds)
    in_specs = o_specs + y_specs + x_specs + [
        pl.BlockSpec((None, 6, D_MODEL), lambda i: (_mod_group(i, tm), 0, 0)),
        pl.BlockSpec((1, D_MODEL), lambda i: (0, 0)),
        pl.BlockSpec((D_MODEL, D_MODEL), lambda i: (0, 0), pipeline_mode=pl.Buffered(1)),
    ]
    args = list(o_segs) + list(y_segs) + list(x_segs) + [mods, g, w_out]
    row_spec = pl.BlockSpec((tm, D_MODEL), lambda i: (i, 0))
    out_specs = [row_spec, row_spec]
    out_shape = [jax.ShapeDtypeStruct((T_ALL, D_MODEL), F32),
                 jax.ShapeDtypeStruct((T_ALL, D_MODEL), F32 if with_router else BF16)]
    scratch = []
    if with_router:
        in_specs.append(pl.BlockSpec((16, D_MODEL), lambda i: (0, 0)))
        args.append(router_wt)
        out_specs += [pl.BlockSpec((8, tm), lambda i: (0, i)), pl.BlockSpec((16, 128), lambda i: (0, 0))]
        out_shape += [jax.ShapeDtypeStruct((8, T_ALL), F32), jax.ShapeDtypeStruct((16, 128), F32)]
        scratch = [pltpu.VMEM((16, 128), F32)]
    return pl.pallas_call(
        kern,
        grid=(T_ALL // tm,),
        in_specs=in_specs,
        out_specs=out_specs,
        out_shape=out_shape,
        scratch_shapes=scratch,
        compiler_params=_cparams(("arbitrary",)),
        name="outproj_router" if with_router else "outproj",
    )(*args)


MOE_ROWS = 2 * T_ALL
MOE_TILE = 256
MOE_TILES = MOE_ROWS // MOE_TILE
MOE_VISITS = MOE_TILES + N_EXPERTS - 1


def _row_copy(src, s, dst, d, sem):
    return pltpu.make_async_copy(src.at[pl.ds(s, 1)], dst.at[pl.ds(d, 1)], sem)


def _dispatch_kernel(p1_ref, p2_ref, h_ref, xs_ref, sem):
    tm = h_ref.shape[0]

    def issue(r, c):
        _row_copy(h_ref, r, xs_ref, p1_ref[0, 0, r], sem.at[0]).start()
        _row_copy(h_ref, r, xs_ref, p2_ref[0, 0, r], sem.at[1]).start()
        return c

    lax.fori_loop(0, tm, issue, 0, unroll=8)
    pltpu.make_async_copy(h_ref, xs_ref.at[pl.ds(0, tm)], sem.at[0]).wait()
    pltpu.make_async_copy(h_ref, xs_ref.at[pl.ds(0, tm)], sem.at[1]).wait()


def _dispatch(h, pos1, pos2):
    tm = 512
    nt = T_ALL // tm
    idx = lambda: pl.BlockSpec((1, 1, tm), lambda i: (i, 0, 0), memory_space=pltpu.SMEM)
    return pl.pallas_call(
        _dispatch_kernel,
        grid=(nt,),
        in_specs=[idx(), idx(), pl.BlockSpec((tm, D_MODEL), lambda i: (i, 0))],
        out_specs=pl.BlockSpec(memory_space=pl.ANY),
        out_shape=jax.ShapeDtypeStruct((MOE_ROWS, D_MODEL), F32),
        scratch_shapes=[pltpu.SemaphoreType.DMA((2,))],
        compiler_params=_cparams(("arbitrary",)),
        name="moe_dispatch",
    )(pos1.reshape(nt, 1, tm), pos2.reshape(nt, 1, tm), h)


def _experts_kernel(vt_ref, ve_ref, nv_ref, lo_ref, hi_ref, xs_ref, wg_ref, wu_ref, wd_ref, y_ref):
    v = pl.program_id(0)

    @pl.when(v < nv_ref[0])
    def _():
        e = ve_ref[v]
        x = xs_ref[...].astype(BF16)
        hid = _silu(_dot(x, wg_ref[...])) * _dot(x, wu_ref[...])
        y = _dot(hid.astype(BF16), wd_ref[...])
        row = vt_ref[v] * MOE_TILE + lax.broadcasted_iota(jnp.int32, (MOE_TILE, 1), 0)
        mine = (row >= lo_ref[e]) & (row < hi_ref[e])
        first_visit = (v == 0) | (vt_ref[jnp.maximum(v - 1, 0)] != vt_ref[v])

        @pl.when(first_visit)
        def _():
            y_ref[...] = jnp.where(mine, y, 0.0)

        @pl.when(jnp.logical_not(first_visit))
        def _():
            y_ref[...] = jnp.where(mine, y, y_ref[...])


def _experts(xs, wg, wu, wd, vt, ve, nv, lo, hi):
    grid_spec = pltpu.PrefetchScalarGridSpec(
        num_scalar_prefetch=5,
        grid=(MOE_VISITS,),
        in_specs=[
            pl.BlockSpec((MOE_TILE, D_MODEL), lambda v, vt, ve, nv, lo, hi: (vt[v], 0)),
            pl.BlockSpec((None, D_MODEL, F_EXPERT), lambda v, vt, ve, nv, lo, hi: (ve[v], 0, 0)),
            pl.BlockSpec((None, D_MODEL, F_EXPERT), lambda v, vt, ve, nv, lo, hi: (ve[v], 0, 0)),
            pl.BlockSpec((None, F_EXPERT, D_MODEL), lambda v, vt, ve, nv, lo, hi: (ve[v], 0, 0)),
        ],
        out_specs=pl.BlockSpec((MOE_TILE, D_MODEL), lambda v, vt, ve, nv, lo, hi: (vt[v], 0)),
    )
    return pl.pallas_call(
        _experts_kernel,
        grid_spec=grid_spec,
        out_shape=jax.ShapeDtypeStruct((MOE_ROWS, D_MODEL), F32),
        compiler_params=_cparams(("arbitrary",)),
        name="moe_experts",
    )(vt, ve, nv, lo, hi, xs, wg, wu, wd)


def _combine_kernel(p1c_ref, p2c_ref, p1n_ref, p2n_ref, y_hbm, x_ref, mod_ref, gate_ref, fg_ref,
                    outp_ref, outs_ref, ya_buf, yb_buf, sem):
    i = pl.program_id(0)
    n = pl.num_programs(0)
    tm = x_ref.shape[0]
    slot = i % 2

    def gather(pa_ref, pb_ref, s):
        def issue(r, c):
            _row_copy(y_hbm, pa_ref[0, 0, r], ya_buf.at[s], r, sem.at[0, s]).start()
            _row_copy(y_hbm, pb_ref[0, 0, r], yb_buf.at[s], r, sem.at[1, s]).start()
            return c

        lax.fori_loop(0, tm, issue, 0, unroll=8)

    @pl.when(i == 0)
    def _():
        gather(p1c_ref, p2c_ref, 0)

    @pl.when(i + 1 < n)
    def _():
        gather(p1n_ref, p2n_ref, 1 - slot)

    pltpu.make_async_copy(y_hbm.at[pl.ds(0, tm)], ya_buf.at[slot], sem.at[0, slot]).wait()
    pltpu.make_async_copy(y_hbm.at[pl.ds(0, tm)], yb_buf.at[slot], sem.at[1, slot]).wait()
    g = gate_ref[...]
    mix = g[:, 0:1] * ya_buf[slot] + g[:, 1:2] * yb_buf[slot]
    xo = _rms(x_ref[...] + mod_ref[5:6, :] * mix, fg_ref[...])

    @pl.when(i < T_PROMPT // tm)
    def _():
        outp_ref[...] = xo

    @pl.when(i >= T_PROMPT // tm)
    def _():
        outs_ref[...] = xo


def _combine(y, x, mods, gate_cols, pos1, pos2, final_g):
    tm = 256
    nt = T_ALL // tm
    ntp = T_PROMPT // tm
    cur = lambda: pl.BlockSpec((1, 1, tm), lambda i: (i, 0, 0), memory_space=pltpu.SMEM)
    nxt = lambda: pl.BlockSpec((1, 1, tm), lambda i: (jnp.minimum(i + 1, nt - 1), 0, 0), memory_space=pltpu.SMEM)
    p1, p2 = pos1.reshape(nt, 1, tm), pos2.reshape(nt, 1, tm)
    return pl.pallas_call(
        _combine_kernel,
        grid=(nt,),
        in_specs=[
            cur(), cur(), nxt(), nxt(),
            pl.BlockSpec(memory_space=pl.ANY),
            pl.BlockSpec((tm, D_MODEL), lambda i: (i, 0)),
            pl.BlockSpec((None, 6, D_MODEL), lambda i: (_mod_group(i, tm), 0, 0)),
            pl.BlockSpec((tm, 128), lambda i: (i, 0)),
            pl.BlockSpec((1, D_MODEL), lambda i: (0, 0)),
        ],
        out_specs=[
            pl.BlockSpec((tm, D_MODEL), lambda i: (jnp.minimum(i, ntp - 1), 0)),
            pl.BlockSpec((tm, D_MODEL), lambda i: (jnp.maximum(i - ntp, 0), 0)),
        ],
        out_shape=[
            jax.ShapeDtypeStruct((T_PROMPT, D_MODEL), F32), jax.ShapeDtypeStruct((T_SAMPLE, D_MODEL), F32),
        ],
        scratch_shapes=[
            pltpu.VMEM((2, tm, D_MODEL), F32), pltpu.VMEM((2, tm, D_MODEL), F32),
            pltpu.SemaphoreType.DMA((2, 2)),
        ],
        compiler_params=_cparams(("arbitrary",)),
        name="moe_combine",
    )(p1, p2, p1, p2, y, x, mods, gate_cols, final_g)


def _route_plan(meta, counts):
    i1, i2 = meta[2].astype(jnp.int32), meta[3].astype(jnp.int32)
    r1, r2 = meta[4].astype(jnp.int32), meta[5].astype(jnp.int32)
    cnt = counts[:N_EXPERTS, 0].astype(jnp.int32)
    hi = jnp.cumsum(cnt)
    lo = hi - cnt
    ex = jnp.arange(N_EXPERTS, dtype=jnp.int32)
    pos1 = jnp.sum(jnp.where(i1[:, None] == ex[None, :], lo[None, :], 0), axis=1) + r1
    pos2 = jnp.sum(jnp.where(i2[:, None] == ex[None, :], lo[None, :], 0), axis=1) + r2
    first_tile = lo // MOE_TILE
    n_vis_e = jnp.where(cnt > 0, (hi - 1) // MOE_TILE - first_tile + 1, 0)
    vis_hi = jnp.cumsum(n_vis_e)
    vis_lo = vis_hi - n_vis_e
    nv = vis_hi[-1]
    v = jnp.minimum(jnp.arange(MOE_VISITS, dtype=jnp.int32), nv - 1)
    ve = jnp.minimum(jnp.sum(v[:, None] >= vis_hi[None, :], axis=1), N_EXPERTS - 1).astype(jnp.int32)
    pick = lambda tab: jnp.sum(jnp.where(ve[:, None] == ex[None, :], tab[None, :], 0), axis=1)
    vt = (pick(first_tile) + v - pick(vis_lo)).astype(jnp.int32)
    return pos1, pos2, vt, ve, nv.reshape(1).astype(jnp.int32), lo.astype(jnp.int32), hi.astype(jnp.int32)


def _ffn_kernel(*refs, n_cast):
    h_ref, x_hbm, mod_ref, wg_ref, wu_ref, wd_ref = refs[:6]
    cast_in = refs[6:6 + n_cast]
    out_ref = refs[6 + n_cast]
    cast_out = refs[7 + n_cast:7 + 2 * n_cast]
    x_buf, sem = refs[7 + 2 * n_cast:]
    _side_cast(cast_in, cast_out)
    i, f = pl.program_id(0), pl.program_id(1)
    tm = h_ref.shape[0]
    x_copy = pltpu.make_async_copy(x_hbm.at[pl.ds(pl.multiple_of(i * tm, tm), tm)], x_buf, sem)

    @pl.when(f == 0)
    def _():
        x_copy.start()
        out_ref[...] = jnp.zeros_like(out_ref)

    h = h_ref[...]
    hid = _silu(_dot(h, wg_ref[...])) * _dot(h, wu_ref[...])
    out_ref[...] += _dot(hid.astype(BF16), wd_ref[...])

    @pl.when(f == pl.num_programs(1) - 1)
    def _():
        x_copy.wait()
        out_ref[...] = x_buf[...] + mod_ref[5:6, :] * out_ref[...]


FFN_CAST_STEPS = 64


def _ffn(h, x, mods, wg, wu, wd, to_bf16):
    tm, tf = 1024, 512
    nf = F_DENSE // tf
    assert (T_ALL // tm) * nf >= FFN_CAST_STEPS

    cast_in, cast_out, cast_shapes = _side_cast_specs(to_bf16, FFN_CAST_STEPS, lambda i, f: i * nf + f)
    outs = pl.pallas_call(
        functools.partial(_ffn_kernel, n_cast=len(to_bf16)),
        grid=(T_ALL // tm, nf),
        in_specs=[
            pl.BlockSpec((tm, D_MODEL), lambda i, f: (i, 0)),
            _ANY,
            pl.BlockSpec((None, 6, D_MODEL), lambda i, f: (_mod_group(i, tm), 0, 0)),
            pl.BlockSpec((D_MODEL, tf), lambda i, f: (0, f)),
            pl.BlockSpec((D_MODEL, tf), lambda i, f: (0, f)),
            pl.BlockSpec((tf, D_MODEL), lambda i, f: (f, 0)),
        ] + cast_in,
        out_specs=[pl.BlockSpec((tm, D_MODEL), lambda i, f: (i, 0))] + cast_out,
        out_shape=[jax.ShapeDtypeStruct((T_ALL, D_MODEL), F32)] + cast_shapes,
        scratch_shapes=[pltpu.VMEM((tm, D_MODEL), F32), pltpu.SemaphoreType.DMA(())],
        compiler_params=_cparams(("arbitrary", "arbitrary")),
        name="dense_ffn",
    )(h, x, mods, wg, wu, wd, *[part[0] for part in to_bf16])
    return outs[0], outs[1:]


def _rope_tables():
    n = DEC_SEQ
    rows = n // GRID_W
    t_row = jnp.repeat(jnp.arange(rows, dtype=F32), GRID_W)
    t_col = jnp.tile(jnp.arange(GRID_W, dtype=F32), rows)
    inv = 1.0 / (ROPE_THETA ** (jnp.arange(0, ROT_HALF, 2, dtype=F32) / ROT_HALF))
    ar, ac = t_row[:, None] * inv, t_col[:, None] * inv
    cos = jnp.concatenate([jnp.cos(ar), jnp.cos(ar), jnp.cos(ac), jnp.cos(ac)], axis=-1)
    sin_signed = jnp.concatenate([-jnp.sin(ar), jnp.sin(ar), -jnp.sin(ac), jnp.sin(ac)], axis=-1)
    return cos, sin_signed


def _pad_lanes(v, width=128):
    return jnp.pad(v, ((0, 0), (0, width - v.shape[-1])))


def kernel(x_prompt, x_sample, c, cache_k, cache_v, state_ssm_fwd, state_ssm_bwd, c_ctx, ada_w, ada_b, norm1_g, norm2_g, w_in, q_norm_g, k_norm_g, conv_w, conv_b, a_log_fwd, a_log_bwd, dt_bias_fwd, dt_bias_bwd, d_skip, ssd_norm_g, attn_out_g, w_out, ffn_w_gate, ffn_w_up, ffn_w_down, router_w, moe_w_gate, moe_w_up, moe_w_down, final_norm_g):
    assert DEPTH % 2 == 0
    cond = jnp.concatenate([c_ctx[None, :], c, jnp.zeros((N_COND - 1 - DEC_BATCH, D_MODEL), F32)], axis=0)
    w_in_t = jnp.swapaxes(w_in, 1, 2)
    mods_all, (w_main,) = _ada_mods(cond, ada_w, ada_b, [(w_in_t, 0, N_MAIN, 0)])
    mods_all = mods_all.reshape(DEPTH, N_COND, 6, D_MODEL)
    cos, sin_signed = _rope_tables()

    x_segs = [x_prompt.reshape(T_PROMPT, D_MODEL), x_sample.reshape(T_SAMPLE, D_MODEL)]
    kv, states = None, None
    for l in range(DEPTH):
        mods = mods_all[l]
        proj, dt_raw, (w_o,) = _inproj(x_segs, mods, norm1_g[l][None, :], w_main, w_in_t, l,
                                       [(w_out, 0, D_MODEL, l)])

        qg, kg, og = q_norm_g[l][None, :], k_norm_g[l][None, :], attn_out_g[l][None, :]
        o_p, k_all, v_all = _attention_prompt(proj, 0, qg, kg, og, l, kv)
        kv = (k_all, v_all)
        o_s = _attention_sample(proj, T_PROMPT, qg, kg, og, cache_k, cache_v, cos, sin_signed, l)

        p = {
            'conv_w': conv_w[l], 'conv_b': conv_b[l][None, :],
            'dt_bias': _pad_lanes(jnp.concatenate([dt_bias_fwd[l], dt_bias_bwd[l]])[None, :]),
            'a_log': _pad_lanes(jnp.concatenate([a_log_fwd[l], a_log_bwd[l]])[None, :]),
            'd_skip': jnp.repeat(d_skip[l], SSD_HEAD_DIM)[None, :],
            'ssd_norm_g': ssd_norm_g[l][None, :],
        }
        j = l // 2
        n_up, n_down = N_EXPERTS * D_MODEL, N_EXPERTS * F_EXPERT
        if l % 2 == 0:
            parts = [(ffn_w_gate.reshape(-1, F_DENSE), j * D_MODEL, D_MODEL),
                     (ffn_w_up.reshape(-1, F_DENSE), j * D_MODEL, D_MODEL),
                     (ffn_w_down.reshape(-1, D_MODEL), j * F_DENSE, F_DENSE)]
        else:
            parts = [(moe_w_down.reshape(-1, D_MODEL), j * n_down, n_down)]
        y_p, sf, sb, *mixer_w = _ssd(proj, dt_raw, p, SEQ, BATCH, 0, l, None, states, parts)
        states = (sf, sb)
        y_s, = _ssd(proj, dt_raw, p, DEC_SEQ, DEC_BATCH, T_PROMPT // DEC_SEQ, l, (state_ssm_fwd, state_ssm_bwd), None)

        g2 = norm2_g[l][None, :]
        if l % 2 == 0:
            x, h = _outproj([o_p, o_s], [y_p, y_s], x_segs, mods, g2, w_o, None)
            x, (eg, eu, w_main) = _ffn(h, x, mods, *mixer_w, [
                (moe_w_gate.reshape(-1, F_EXPERT), j * n_up, n_up),
                (moe_w_up.reshape(-1, F_EXPERT), j * n_up, n_up),
                (w_in_t, 0, N_MAIN, l + 1)])
            x_segs = [x]
        else:
            router_wt = jnp.pad(router_w[j].T, ((0, 16 - N_EXPERTS), (0, 0)))
            x, h, meta, counts = _outproj([o_p, o_s], [y_p, y_s], x_segs, mods, g2, w_o, router_wt)
            pos1, pos2, vt, ve, nv, lo, hi = _route_plan(meta, counts)
            xs = _dispatch(h, pos1, pos2)
            ys = _experts(xs, eg.reshape(N_EXPERTS, D_MODEL, F_EXPERT), eu.reshape(N_EXPERTS, D_MODEL, F_EXPERT),
                          mixer_w[0].reshape(N_EXPERTS, F_EXPERT, D_MODEL), vt, ve, nv, lo, hi)
            y_prompt, y_sample = _combine(ys, x, mods, _pad_lanes(meta[:2].T), pos1, pos2, final_norm_g[None, :])

    return (y_prompt.reshape(BATCH, SEQ, D_MODEL), y_sample.reshape(DEC_BATCH, DEC_SEQ, D_MODEL),
            kv[0], kv[1], states[0], states[1])
```

```python
import functools

import jax
import jax.numpy as jnp
from jax import lax
from jax.experimental import pallas as pl
from jax.experimental.pallas import tpu as pltpu

F32 = jnp.float32
BF16 = jnp.bfloat16

D_MODEL = 2048
BATCH = 16
SEQ = 256
DEPTH = 2
DEC_BATCH = 2
DEC_SEQ = 1024
PAST_LEN = 512
GRID_W = 64
D_ATTN = 1024
D_SSD = 1024
HEAD_DIM = 128
N_Q_HEADS = 8
N_KV_HEADS = 2
Q_PER_KV = 4
KV_DIM = 256
ROT_HALF = 64
ROPE_THETA = 10000.0
SSD_HEAD_DIM = 64
SSD_HEADS = 16
SSD_GROUPS = 2
HEADS_PER_GROUP = 8
D_STATE = 128
CONV_DIM = 1536
CHUNK = 128
N_MAIN = 4096
F_DENSE = 5632
N_EXPERTS = 8
F_EXPERT = 1024
EPS = 1e-6

T_PROMPT = BATCH * SEQ
T_SAMPLE = DEC_BATCH * DEC_SEQ
T_ALL = T_PROMPT + T_SAMPLE
N_COND = 16

VMEM_LIMIT = 58 * 1024 * 1024


def _cparams(sem):
    return pltpu.CompilerParams(dimension_semantics=sem, vmem_limit_bytes=VMEM_LIMIT)


def _mod_group(i, tm):
    return jnp.maximum(0, (i * tm - T_PROMPT + DEC_SEQ) // DEC_SEQ)


def _silu(x):
    return x * jax.nn.sigmoid(x)


def _rms(x, g):
    ms = jnp.mean(x * x, axis=-1, keepdims=True)
    return x * lax.rsqrt(ms + EPS) * g


def _dot(a, b):
    return jnp.dot(a, b, preferred_element_type=F32)


def _dot_nt(a, b):
    return lax.dot_general(a, b, (((1,), (1,)), ((), ())), preferred_element_type=F32)


def _split3(x):
    hi = x.astype(BF16)
    r1 = x - hi.astype(F32)
    mid = r1.astype(BF16)
    r2 = r1 - mid.astype(F32)
    return hi, mid, r2.astype(BF16)


def _ada_kernel(c_ref, w_ref, b_ref, o_ref):
    s = _silu(c_ref[...]).astype(BF16)
    o_ref[...] = _dot(s, w_ref[...].astype(BF16)) + b_ref[...]


def _ada_mods(cond, ada_w, ada_b):
    tn = 1024
    n_out = 6 * D_MODEL
    return pl.pallas_call(
        _ada_kernel,
        grid=(DEPTH, n_out // tn),
        in_specs=[
            pl.BlockSpec((N_COND, D_MODEL), lambda l, j: (0, 0)),
            pl.BlockSpec((None, D_MODEL, tn), lambda l, j: (l, 0, j)),
            pl.BlockSpec((None, 1, tn), lambda l, j: (l, 0, j)),
        ],
        out_specs=pl.BlockSpec((None, N_COND, tn), lambda l, j: (l, 0, j)),
        out_shape=jax.ShapeDtypeStruct((DEPTH, N_COND, n_out), F32),
        compiler_params=_cparams(("parallel", "parallel")),
        name="ada_mods",
    )(cond, ada_w, ada_b.reshape(DEPTH, 1, n_out))


def _side_cast_specs(parts, n_steps, linear_step):
    in_specs, out_specs, out_shapes = [], [], []
    for a, row0, nrows in parts:
        rows = nrows // n_steps
        blk0 = row0 // rows
        step = lambda *ids: jnp.minimum(linear_step(*ids), n_steps - 1)
        in_specs.append(pl.BlockSpec((rows, a.shape[1]), lambda *ids, blk0=blk0: (blk0 + step(*ids), 0)))
        out_specs.append(pl.BlockSpec((rows, a.shape[1]), lambda *ids: (step(*ids), 0)))
        out_shapes.append(jax.ShapeDtypeStruct((nrows, a.shape[1]), BF16))
    return in_specs, out_specs, out_shapes


def _side_cast(cast_in, cast_out):
    for src, dst in zip(cast_in, cast_out):
        dst[...] = src[...].astype(BF16)


INPROJ_NORM_ROWS = 1024
INPROJ_CAST_STEPS = 16


def _inproj_kernel(*refs, seg_rows, n_cast):
    n_seg = len(seg_rows)
    x_hbms = refs[:n_seg]
    mods_ref, g_ref, w_ref, wdt_ref = refs[n_seg:n_seg + 4]
    cast_in = refs[n_seg + 4:n_seg + 4 + n_cast]
    proj_ref, dt_ref = refs[n_seg + 4 + n_cast:n_seg + 6 + n_cast]
    cast_out = refs[n_seg + 6 + n_cast:n_seg + 6 + 2 * n_cast]
    x_buf, h_scr, sem = refs[n_seg + 6 + 2 * n_cast:]
    _side_cast(cast_in, cast_out)
    i, j = pl.program_id(0), pl.program_id(1)
    tm = x_buf.shape[0]

    def fetch(tile):
        start = 0
        for x_hbm, nrows in zip(x_hbms, seg_rows):
            b0, nb = start // tm, nrows // tm
            start += nrows

            @pl.when((tile >= b0) & (tile < b0 + nb))
            def _():
                r0 = pl.multiple_of((tile - b0) * tm, tm)
                pltpu.make_async_copy(x_hbm.at[pl.ds(r0, tm)], x_buf, sem).start()

    @pl.when(j == 0)
    def _():
        @pl.when(i == 0)
        def _():
            fetch(i)

        pltpu.make_async_copy(x_hbms[0].at[pl.ds(0, tm)], x_buf, sem).wait()
        for k in range(tm // INPROJ_NORM_ROWS):
            rows = slice(k * INPROJ_NORM_ROWS, (k + 1) * INPROJ_NORM_ROWS)
            mod = mods_ref[_mod_group(i * (tm // INPROJ_NORM_ROWS) + k, INPROJ_NORM_ROWS)]
            h = _rms(x_buf[rows, :], g_ref[...]) * (1.0 + mod[1:2, :]) + mod[0:1, :]
            h_scr[rows, :] = h.astype(BF16)

        @pl.when(i + 1 < pl.num_programs(0))
        def _():
            fetch(i + 1)

        n_dt = wdt_ref.shape[0]
        wdt = jnp.concatenate([wdt_ref[...], jnp.zeros((128 - n_dt, D_MODEL), F32)], axis=0)
        dt_ref[...] = _dot_nt(h_scr[...], wdt.astype(BF16))

    proj_ref[...] = _dot_nt(h_scr[...], w_ref[...].astype(BF16))


_ANY = pl.BlockSpec(memory_space=pl.ANY)


def _inproj(x_segs, mods, g, w_in_t, layer, to_bf16):
    tm, tn = 2048, 512
    n_dt = w_in_t.shape[1] - N_MAIN
    nj = N_MAIN // tn
    seg_rows = tuple(a.shape[0] for a in x_segs)
    assert all(r % tm == 0 for r in seg_rows) and sum(seg_rows) == T_ALL
    assert (T_ALL // tm) * nj >= INPROJ_CAST_STEPS
    cast_in, cast_out, cast_shapes = _side_cast_specs(to_bf16, INPROJ_CAST_STEPS, lambda i, j: i * nj + j)
    outs = pl.pallas_call(
        functools.partial(_inproj_kernel, seg_rows=seg_rows, n_cast=len(to_bf16)),
        grid=(T_ALL // tm, nj),
        in_specs=[_ANY] * len(x_segs) + [
            pl.BlockSpec(mods.shape, lambda i, j: (0, 0, 0)),
            pl.BlockSpec((1, D_MODEL), lambda i, j: (0, 0)),
            pl.BlockSpec((None, tn, D_MODEL), lambda i, j: (layer, j, 0)),
            pl.BlockSpec((None, n_dt, D_MODEL), lambda i, j: (layer, N_MAIN // n_dt, 0)),
        ] + cast_in,
        out_specs=[
            pl.BlockSpec((tm, tn), lambda i, j: (i, j)),
            pl.BlockSpec((tm, 128), lambda i, j: (i, 0)),
        ] + cast_out,
        out_shape=[
            jax.ShapeDtypeStruct((T_ALL, N_MAIN), F32),
            jax.ShapeDtypeStruct((T_ALL, 128), F32),
        ] + cast_shapes,
        scratch_shapes=[pltpu.VMEM((tm, D_MODEL), F32), pltpu.VMEM((tm, D_MODEL), BF16),
                        pltpu.SemaphoreType.DMA(())],
        compiler_params=_cparams(("arbitrary", "arbitrary")),
        name="inproj",
    )(*x_segs, mods, g, w_in_t, w_in_t, *[part[0] for part in to_bf16])
    return outs[0], outs[1], outs[2:]


def _rope(x, cos, sin_signed):
    lane = lax.broadcasted_iota(jnp.int32, x.shape, 1)
    first = (lane // (ROT_HALF // 2)) % 2 == 0
    swapped = jnp.where(first, pltpu.roll(x, HEAD_DIM - ROT_HALF // 2, 1), pltpu.roll(x, ROT_HALF // 2, 1))
    return x * cos + swapped * sin_signed


N_SCORE_BUFS = 3


def _attn_kernel(*refs, nk_new, has_ctx, n_prev):
    if has_ctx:
        (q_ref, kv_ref, qg_ref, kg_ref, og_ref, ck_ref, cv_ref, cq_ref, sq_ref, ckk_ref, skk_ref,
         o_ref, kb_scr, vb_scr, o_scr, s_scr) = refs
    elif n_prev:
        (q_ref, kv_ref, qg_ref, kg_ref, og_ref, pk_ref, pv_ref, o_ref, ko_ref, vo_ref,
         kb_scr, vb_scr, o_scr, s_scr) = refs
    else:
        (q_ref, kv_ref, qg_ref, kg_ref, og_ref, o_ref, ko_ref, vo_ref, kb_scr, vb_scr, o_scr, s_scr) = refs

    @pl.when(pl.program_id(1) == 0)
    def _():
        if not has_ctx and n_prev:
            ko_ref[0:n_prev] = pk_ref[...]
            vo_ref[0:n_prev] = pv_ref[...]
        for g in range(N_KV_HEADS):
            sl = slice(g * HEAD_DIM, (g + 1) * HEAD_DIM)
            kn = _rms(kv_ref[:, sl], kg_ref[...])
            v = kv_ref[:, KV_DIM + g * HEAD_DIM:KV_DIM + (g + 1) * HEAD_DIM]
            vsl = slice(2 * g * HEAD_DIM, (2 * g + 1) * HEAD_DIM)
            vb_scr[0:nk_new, vsl] = v.astype(BF16)
            vb_scr[:, (2 * g + 1) * HEAD_DIM:(2 * g + 2) * HEAD_DIM] = jnp.ones((vb_scr.shape[0], HEAD_DIM), BF16)
            if has_ctx:
                kb_scr[0:nk_new, sl] = _rope(kn, ckk_ref[...], skk_ref[...]).astype(BF16)
                kb_scr[nk_new:, sl] = ck_ref[:, g, :].astype(BF16)
                vb_scr[nk_new:, vsl] = cv_ref[:, g, :].astype(BF16)
            else:
                kb_scr[:, sl] = kn.astype(BF16)
                ko_ref[n_prev, :, g, :] = kn
                vo_ref[n_prev, :, g, :] = v

    scale_log2e = HEAD_DIM ** -0.5 * 1.4426950408889634

    def scores(h):
        g = h // Q_PER_KV
        qn = _rms(q_ref[:, h * HEAD_DIM:(h + 1) * HEAD_DIM], qg_ref[...])
        if has_ctx:
            qn = _rope(qn, cq_ref[...], sq_ref[...])
        qb = (qn * scale_log2e).astype(BF16)
        s_scr[h % N_SCORE_BUFS] = _dot_nt(qb, kb_scr[:, g * HEAD_DIM:(g + 1) * HEAD_DIM])

    for h in range(N_SCORE_BUFS - 1):
        scores(h)
    for h in range(N_Q_HEADS):
        if h + N_SCORE_BUFS - 1 < N_Q_HEADS:
            scores(h + N_SCORE_BUFS - 1)
        g = h // Q_PER_KV
        s = s_scr[h % N_SCORE_BUFS]
        e = jnp.exp2(s - jnp.max(s, axis=-1, keepdims=True))
        pv = _dot(e.astype(BF16), vb_scr[:, 2 * g * HEAD_DIM:(2 * g + 2) * HEAD_DIM])
        o_scr[:, h * HEAD_DIM:(h + 1) * HEAD_DIM] = pv[:, :HEAD_DIM] / pv[:, HEAD_DIM:]
    o_ref[...] = _rms(o_scr[...], og_ref[...]).astype(BF16)


def _attention_prompt(proj, row0, qg, kg, og, layer, prev_kv):
    n = SEQ
    blk0 = row0 // n
    kern = functools.partial(_attn_kernel, nk_new=n, has_ctx=False, n_prev=layer)
    vec = lambda w: pl.BlockSpec((1, w), lambda b, i: (0, 0))
    cache_spec = lambda k: pl.BlockSpec((None, k, n, N_KV_HEADS, HEAD_DIM), lambda b, i: (b, 0, 0, 0, 0))
    cache_shape = jax.ShapeDtypeStruct((BATCH, layer + 1, n, N_KV_HEADS, HEAD_DIM), F32)
    in_specs = [
        pl.BlockSpec((n, D_ATTN), lambda b, i: (blk0 + b, 0)),
        pl.BlockSpec((n, 2 * KV_DIM), lambda b, i: (blk0 + b, 2)),
        vec(HEAD_DIM), vec(HEAD_DIM), vec(D_ATTN),
    ]
    args = [proj, proj, qg, kg, og]
    if layer:
        in_specs += [cache_spec(layer), cache_spec(layer)]
        args += list(prev_kv)
    return pl.pallas_call(
        kern,
        grid=(BATCH, 1),
        in_specs=in_specs,
        out_specs=[pl.BlockSpec((n, D_ATTN), lambda b, i: (b, 0)), cache_spec(layer + 1), cache_spec(layer + 1)],
        out_shape=[jax.ShapeDtypeStruct((T_PROMPT, D_ATTN), BF16), cache_shape, cache_shape],
        scratch_shapes=[
            pltpu.VMEM((n, KV_DIM), BF16), pltpu.VMEM((n, 2 * KV_DIM), BF16), pltpu.VMEM((n, D_ATTN), F32),
            pltpu.VMEM((N_SCORE_BUFS, n, n), F32),
        ],
        compiler_params=_cparams(("parallel", "arbitrary")),
        name="attn_prompt",
    )(*args)


def _attention_sample(proj, row0, qg, kg, og, ck, cv, cos, sin_signed, layer):
    n, tq = DEC_SEQ, 512
    nq = n // tq
    nk = n + PAST_LEN
    kern = functools.partial(_attn_kernel, nk_new=n, has_ctx=True, n_prev=0)
    vec = lambda w: pl.BlockSpec((1, w), lambda b, i: (0, 0))
    q_blk0 = row0 // tq
    kv_blk0 = row0 // n
    return pl.pallas_call(
        kern,
        grid=(DEC_BATCH, nq),
        in_specs=[
            pl.BlockSpec((tq, D_ATTN), lambda b, i: (q_blk0 + b * nq + i, 0)),
            pl.BlockSpec((n, 2 * KV_DIM), lambda b, i: (kv_blk0 + b, 2)),
            vec(HEAD_DIM), vec(HEAD_DIM), vec(D_ATTN),
            pl.BlockSpec((None, None, PAST_LEN, N_KV_HEADS, HEAD_DIM), lambda b, i: (b, layer, 0, 0, 0)),
            pl.BlockSpec((None, None, PAST_LEN, N_KV_HEADS, HEAD_DIM), lambda b, i: (b, layer, 0, 0, 0)),
            pl.BlockSpec((tq, HEAD_DIM), lambda b, i: (i, 0)),
            pl.BlockSpec((tq, HEAD_DIM), lambda b, i: (i, 0)),
            pl.BlockSpec((n, HEAD_DIM), lambda b, i: (0, 0)),
            pl.BlockSpec((n, HEAD_DIM), lambda b, i: (0, 0)),
        ],
        out_specs=pl.BlockSpec((tq, D_ATTN), lambda b, i: (b * nq + i, 0)),
        out_shape=jax.ShapeDtypeStruct((T_SAMPLE, D_ATTN), BF16),
        scratch_shapes=[
            pltpu.VMEM((nk, KV_DIM), BF16), pltpu.VMEM((nk, 2 * KV_DIM), BF16), pltpu.VMEM((tq, D_ATTN), F32),
            pltpu.VMEM((N_SCORE_BUFS, tq, nk), F32),
        ],
        compiler_params=_cparams(("parallel", "arbitrary")),
        name="attn_sample",
    )(proj, proj, qg, kg, og, ck, cv, cos, sin_signed, cos, sin_signed)


def _conv_silu(x, w, b):
    n = x.shape[0]
    row = lax.broadcasted_iota(jnp.int32, (n, 1), 0)
    prev = jnp.where(row == 0, 0.0, pltpu.roll(x, 1, 0))
    nxt = jnp.where(row == n - 1, 0.0, pltpu.roll(x, n - 1, 0))
    return _silu(prev * w[0:1, :] + x * w[1:2, :] + nxt * w[2:3, :] + b)


def _softplus(x):
    return jnp.maximum(x, 0.0) + jnp.log1p(jnp.exp(-jnp.abs(x)))


def _ssd_kernel(*refs, n, has_init, n_prev, n_cast):
    refs = list(refs)
    (za_ref, zb_ref, xa_ref, xb_ref, bc_ref, dt_ref, cw_ref, cb_ref, dtb_ref, alog_ref, dsk_ref,
     ng_ref) = refs[:12]
    del refs[:12]
    if has_init:
        sf0_ref, sb0_ref = refs.pop(0), refs.pop(0)
    elif n_prev:
        psf_ref, psb_ref = refs.pop(0), refs.pop(0)
    cast_in = [refs.pop(0) for _ in range(n_cast)]
    y_ref = refs.pop(0)
    if not has_init:
        sf_ref, sb_ref = refs.pop(0), refs.pop(0)
    cast_out = [refs.pop(0) for _ in range(n_cast)]
    xc_scr, bcc_scr, dts_scr, xt_scr, yt_scr, s_scr = refs
    _side_cast(cast_in, cast_out)
    nc = n // CHUNK
    gw = HEADS_PER_GROUP * SSD_HEAD_DIM

    xc_scr[:, 0:gw] = _conv_silu(xa_ref[...], cw_ref[:, 0:gw], cb_ref[:, 0:gw])
    xc_scr[:, gw:] = _conv_silu(xb_ref[...], cw_ref[:, gw:2 * gw], cb_ref[:, gw:2 * gw])
    bcc_scr[...] = _conv_silu(bc_ref[...], cw_ref[:, 2 * gw:], cb_ref[:, 2 * gw:])
    dts_scr[...] = _softplus(dt_ref[...] + dtb_ref[...])
    for g in range(SSD_GROUPS):
        hs = slice(g * HEADS_PER_GROUP, (g + 1) * HEADS_PER_GROUP)
        if has_init:
            s_scr[0, g] = sf0_ref[hs].reshape(gw, D_STATE)
            s_scr[1, g] = sb0_ref[hs].reshape(gw, D_STATE)
        else:
            s_scr[0, g] = jnp.zeros((gw, D_STATE), F32)
            s_scr[1, g] = jnp.zeros((gw, D_STATE), F32)

    def to_channel_major(c, carry):
        rows = pl.ds(pl.multiple_of(c * CHUNK, CHUNK), CHUNK)
        xt_scr[c] = xc_scr[rows, :].T
        yt_scr[c] = jnp.zeros((D_SSD, CHUNK), F32)
        return carry

    lax.fori_loop(0, nc, to_channel_major, 0)

    a_row = -jnp.exp(alog_ref[...])
    ri = lax.broadcasted_iota(jnp.int32, (CHUNK, CHUNK), 0)
    ci = lax.broadcasted_iota(jnp.int32, (CHUNK, CHUNK), 1)
    lower, upper = ci <= ri, ci >= ri
    n_dirs_heads = 2 * SSD_HEADS

    def scan_chunk(dirn, c):
        tri = (lower if dirn == 0 else upper).astype(BF16)
        valid_st = upper if dirn == 0 else lower
        tri_t = valid_st.astype(BF16)
        row0 = dirn * SSD_HEADS
        rows = pl.ds(pl.multiple_of(c * CHUNK, CHUNK), CHUNK)
        dt = dts_scr[rows, :]
        d = dt * a_row
        d1, d2, d3 = _split3(d)
        cs = _dot(tri, d1) + _dot(tri, d2) + _dot(tri, d3)
        dt_t = dt.T[0:n_dirs_heads, :]
        e1, e2, e3 = _split3(d.T[0:n_dirs_heads, :])
        cs_t = _dot(e1, tri_t) + _dot(e2, tri_t) + _dot(e3, tri_t)
        total = cs_t[:, CHUNK - 1:CHUNK] if dirn == 0 else cs_t[:, 0:1]
        e_in_t = jnp.exp(cs_t)
        to_end_t = jnp.exp(total - cs_t) * dt_t
        dec_t = jnp.broadcast_to(jnp.exp(total), (n_dirs_heads, D_STATE))
        for g in range(SSD_GROUPS):
            bm = bcc_scr[rows, g * D_STATE:(g + 1) * D_STATE].astype(BF16)
            cm = bcc_scr[rows, (SSD_GROUPS + g) * D_STATE:(SSD_GROUPS + g + 1) * D_STATE]
            g_st = _dot_nt(bm, cm.astype(BF16))
            c_nt = cm.T
            st = s_scr[dirn, g]
            xs_parts, dec_parts = [], []
            for hh in range(HEADS_PER_GROUP):
                h = g * HEADS_PER_GROUP + hh
                r = row0 + h
                ch = slice(h * SSD_HEAD_DIM, (h + 1) * SSD_HEAD_DIM)
                x_t = xt_scr[c, ch, :]
                diff = cs_t[r:r + 1, :] - cs[:, r:r + 1]
                a_st = (g_st * jnp.exp(jnp.where(valid_st, diff, -jnp.inf))).astype(BF16)
                c_e = (c_nt * e_in_t[r:r + 1, :]).astype(BF16)
                x_dt = (x_t * dt_t[r:r + 1, :]).astype(BF16)
                s_h = st[hh * SSD_HEAD_DIM:(hh + 1) * SSD_HEAD_DIM, :].astype(BF16)
                y_h = _dot(jnp.concatenate([x_dt, s_h], axis=1), jnp.concatenate([a_st, c_e], axis=0))
                yt_scr[c, ch, :] = yt_scr[c, ch, :] + y_h
                xs_parts.append((x_t * to_end_t[r:r + 1, :]).astype(BF16))
                dec_parts.append(jnp.broadcast_to(dec_t[r:r + 1, :], (SSD_HEAD_DIM, D_STATE)))
            ds = _dot(jnp.concatenate(xs_parts, axis=0), bm)
            s_scr[dirn, g] = st * jnp.concatenate(dec_parts, axis=0) + ds

    def body(i, carry):
        scan_chunk(0, i)
        scan_chunk(1, nc - 1 - i)
        return carry

    lax.fori_loop(0, nc, body, 0)

    def finish(c, carry):
        rows = pl.ds(pl.multiple_of(c * CHUNK, CHUNK), CHUNK)
        y = yt_scr[c].T + xc_scr[rows, :] * dsk_ref[...]
        ya = y[:, 0:gw] * _silu(za_ref[rows, :])
        yb = y[:, gw:] * _silu(zb_ref[rows, :])
        ms = (jnp.sum(ya * ya, axis=-1, keepdims=True) + jnp.sum(yb * yb, axis=-1, keepdims=True)) / D_SSD
        inv = lax.rsqrt(ms + EPS)
        y_ref[rows, 0:gw] = (ya * inv * ng_ref[:, 0:gw]).astype(BF16)
        y_ref[rows, gw:] = (yb * inv * ng_ref[:, gw:]).astype(BF16)
        return carry

    lax.fori_loop(0, nc, finish, 0)

    if not has_init:
        if n_prev:
            sf_ref[0:n_prev] = psf_ref[...]
            sb_ref[0:n_prev] = psb_ref[...]
        for g in range(SSD_GROUPS):
            hs = slice(g * HEADS_PER_GROUP, (g + 1) * HEADS_PER_GROUP)
            sf_ref[n_prev, hs] = s_scr[0, g].reshape(HEADS_PER_GROUP, SSD_HEAD_DIM, D_STATE)
            sb_ref[n_prev, hs] = s_scr[1, g].reshape(HEADS_PER_GROUP, SSD_HEAD_DIM, D_STATE)


def _ssd(proj, dt_raw, p, n, nb, row_blk0, layer, init, prev_states, to_bf16=()):
    has_init = init is not None
    n_prev = 0 if has_init else layer
    kern = functools.partial(_ssd_kernel, n=n, has_init=has_init, n_prev=n_prev, n_cast=len(to_bf16))
    cast_in, cast_out, cast_shapes = _side_cast_specs(to_bf16, nb, lambda b: b)
    col = lambda cb: pl.BlockSpec((n, 512), lambda b: (row_blk0 + b, cb))
    vec = lambda r, w: pl.BlockSpec((r, w), lambda b: (0, 0))
    layers_spec = lambda k: pl.BlockSpec((None, k, SSD_HEADS, SSD_HEAD_DIM, D_STATE), lambda b: (b, 0, 0, 0, 0))
    in_specs = [
        col(3), col(4), col(5), col(6), col(7),
        pl.BlockSpec((n, 128), lambda b: (row_blk0 + b, 0)),
        vec(3, CONV_DIM), vec(1, CONV_DIM), vec(1, 128), vec(1, 128), vec(1, D_SSD), vec(1, D_SSD),
    ]
    args = [proj, proj, proj, proj, proj, dt_raw, p['conv_w'], p['conv_b'], p['dt_bias'], p['a_log'],
            p['d_skip'], p['ssd_norm_g']]
    y_spec = pl.BlockSpec((n, D_SSD), lambda b: (b, 0))
    y_shape = jax.ShapeDtypeStruct((nb * n, D_SSD), BF16)
    if has_init:
        init_spec = pl.BlockSpec((None, None, SSD_HEADS, SSD_HEAD_DIM, D_STATE), lambda b: (b, layer, 0, 0, 0))
        in_specs += [init_spec, init_spec]
        args += list(init)
        out_specs, out_shape = [y_spec], [y_shape]
    else:
        if n_prev:
            in_specs += [layers_spec(n_prev), layers_spec(n_prev)]
            args += list(prev_states)
        st_shape = jax.ShapeDtypeStruct((nb, layer + 1, SSD_HEADS, SSD_HEAD_DIM, D_STATE), F32)
        out_specs = [y_spec, layers_spec(layer + 1), layers_spec(layer + 1)]
        out_shape = [y_shape, st_shape, st_shape]
    return pl.pallas_call(
        kern,
        grid=(nb,),
        in_specs=in_specs + cast_in,
        out_specs=out_specs + cast_out,
        out_shape=out_shape + cast_shapes,
        scratch_shapes=[
            pltpu.VMEM((n, D_SSD), F32), pltpu.VMEM((n, 512), F32), pltpu.VMEM((n, 128), F32),
            pltpu.VMEM((n // CHUNK, D_SSD, CHUNK), F32), pltpu.VMEM((n // CHUNK, D_SSD, CHUNK), F32),
            pltpu.VMEM((2, SSD_GROUPS, 512, D_STATE), F32),
        ],
        compiler_params=_cparams(("parallel",)),
        name="ssd_sample" if has_init else "ssd_prompt",
    )(*args, *[part[0] for part in to_bf16])


def _seg_specs(segs, tm):
    specs, bounds, start = [], [], 0
    for a in segs:
        b0, nblk = start // tm, a.shape[0] // tm
        specs.append(pl.BlockSpec((tm, a.shape[1]), lambda i, b0=b0, nblk=nblk: (jnp.clip(i - b0, 0, nblk - 1), 0)))
        bounds.append(b0)
        start += a.shape[0]
    return specs, tuple(bounds)


def _seg_pick(refs, bounds):
    i = pl.program_id(0)
    v = refs[0][...]
    for ref, b0 in zip(refs[1:], bounds[1:]):
        v = jnp.where(i >= b0, ref[...], v)
    return v


def _outproj_kernel(*refs, with_router, o_bounds, y_bounds, x_bounds):
    refs = list(refs)
    o_refs = [refs.pop(0) for _ in o_bounds]
    y_refs = [refs.pop(0) for _ in y_bounds]
    x_refs = [refs.pop(0) for _ in x_bounds]
    if with_router:
        mod_ref, g_ref, w_ref, rw_ref, xo_ref, h_ref, meta_ref, cnt_ref, carry_scr = refs
    else:
        mod_ref, g_ref, w_ref, xo_ref, h_ref = refs
    tm = xo_ref.shape[0]
    a = jnp.concatenate([_seg_pick(o_refs, o_bounds), _seg_pick(y_refs, y_bounds)], axis=1)
    xn = _seg_pick(x_refs, x_bounds) + mod_ref[2:3, :] * _dot(a, w_ref[...])
    xo_ref[...] = xn
    h = _rms(xn, g_ref[...]) * (1.0 + mod_ref[4:5, :]) + mod_ref[3:4, :]
    h_ref[...] = h.astype(h_ref.dtype)
    if with_router:

        @pl.when(pl.program_id(0) == 0)
        def _():
            carry_scr[...] = jnp.zeros_like(carry_scr)

        h1, h2, _ = _split3(h)
        w1, w2, _ = _split3(rw_ref[...])
        logits = _dot_nt(w1, h1) + _dot_nt(w2, h1) + _dot_nt(w1, h2)
        row = lax.broadcasted_iota(jnp.int32, logits.shape, 0)
        logits = jnp.where(row < N_EXPERTS, logits, -jnp.inf)
        e = jnp.exp(logits - jnp.max(logits, axis=0, keepdims=True))
        probs = e / jnp.sum(e, axis=0, keepdims=True)
        p1 = jnp.max(probs, axis=0, keepdims=True)
        i1 = jnp.min(jnp.where(probs == p1, row, 16), axis=0, keepdims=True)
        rest = jnp.where(row == i1, -1.0, probs)
        p2 = jnp.max(rest, axis=0, keepdims=True)
        i2 = jnp.min(jnp.where(rest == p2, row, 16), axis=0, keepdims=True)
        hit1, hit2 = row == i1, row == i2
        onehot = jnp.where(hit1 | hit2, 1.0, 0.0)
        ti = lax.broadcasted_iota(jnp.int32, (tm, tm), 0)
        tj = lax.broadcasted_iota(jnp.int32, (tm, tm), 1)
        before = jnp.where(ti < tj, 1.0, 0.0).astype(BF16)
        rank = carry_scr[:, 0:1] + _dot(onehot.astype(BF16), before)
        r1 = jnp.sum(jnp.where(hit1, rank, 0.0), axis=0, keepdims=True)
        r2 = jnp.sum(jnp.where(hit2, rank, 0.0), axis=0, keepdims=True)
        carry_scr[...] = carry_scr[...] + jnp.sum(onehot, axis=1, keepdims=True)
        cnt_ref[...] = carry_scr[...]
        r8 = lax.broadcasted_iota(jnp.int32, (8, tm), 0)
        vals = [p1 / (p1 + p2), p2 / (p1 + p2), i1.astype(F32), i2.astype(F32), r1, r2]
        meta = jnp.zeros((8, tm), F32)
        for k, v in enumerate(vals):
            meta = jnp.where(r8 == k, v, meta)
        meta_ref[...] = meta


def _outproj(o_segs, y_segs, x_segs, mods, g, w_out, router_wt):
    tm = 512
    with_router = router_wt is not None
    o_specs, o_bounds = _seg_specs(o_segs, tm)
    y_specs, y_bounds = _seg_specs(y_segs, tm)
    x_specs, x_bounds = _seg_specs(x_segs, tm)
    kern = functools.partial(_outproj_kernel, with_router=with_router, o_bounds=o_bounds, y_bounds=y_bounds,
                             x_bounds=x_bounds)
    in_specs = o_specs + y_specs + x_specs + [
        pl.BlockSpec((None, 6, D_MODEL), lambda i: (_mod_group(i, tm), 0, 0)),
        pl.BlockSpec((1, D_MODEL), lambda i: (0, 0)),
        pl.BlockSpec((D_MODEL, D_MODEL), lambda i: (0, 0), pipeline_mode=pl.Buffered(1)),
    ]
    args = list(o_segs) + list(y_segs) + list(x_segs) + [mods, g, w_out]
    row_spec = pl.BlockSpec((tm, D_MODEL), lambda i: (i, 0))
    out_specs = [row_spec, row_spec]
    out_shape = [jax.ShapeDtypeStruct((T_ALL, D_MODEL), F32),
                 jax.ShapeDtypeStruct((T_ALL, D_MODEL), F32 if with_router else BF16)]
    scratch = []
    if with_router:
        in_specs.append(pl.BlockSpec((16, D_MODEL), lambda i: (0, 0)))
        args.append(router_wt)
        out_specs += [pl.BlockSpec((8, tm), lambda i: (0, i)), pl.BlockSpec((16, 128), lambda i: (0, 0))]
        out_shape += [jax.ShapeDtypeStruct((8, T_ALL), F32), jax.ShapeDtypeStruct((16, 128), F32)]
        scratch = [pltpu.VMEM((16, 128), F32)]
    return pl.pallas_call(
        kern,
        grid=(T_ALL // tm,),
        in_specs=in_specs,
        out_specs=out_specs,
        out_shape=out_shape,
        scratch_shapes=scratch,
        compiler_params=_cparams(("arbitrary",)),
        name="outproj_router" if with_router else "outproj",
    )(*args)


MOE_ROWS = 2 * T_ALL
MOE_TILE = 256
MOE_TILES = MOE_ROWS // MOE_TILE
MOE_VISITS = MOE_TILES + N_EXPERTS - 1
ROW_DMA_UNROLL = 16


def _row_copy(src, s, dst, d, sem):
    return pltpu.make_async_copy(src.at[pl.ds(s, 1)], dst.at[pl.ds(d, 1)], sem)


def _dispatch_kernel(p1_ref, p2_ref, h_ref, xs_ref, sem):
    tm = h_ref.shape[0]

    def issue(r, c):
        _row_copy(h_ref, r, xs_ref, p1_ref[0, 0, r], sem.at[0]).start()
        _row_copy(h_ref, r, xs_ref, p2_ref[0, 0, r], sem.at[1]).start()
        return c

    lax.fori_loop(0, tm, issue, 0, unroll=ROW_DMA_UNROLL)
    pltpu.make_async_copy(h_ref, xs_ref.at[pl.ds(0, tm)], sem.at[0]).wait()
    pltpu.make_async_copy(h_ref, xs_ref.at[pl.ds(0, tm)], sem.at[1]).wait()


def _dispatch(h, pos1, pos2):
    tm = 512
    nt = T_ALL // tm
    idx = lambda: pl.BlockSpec((1, 1, tm), lambda i: (i, 0, 0), memory_space=pltpu.SMEM)
    return pl.pallas_call(
        _dispatch_kernel,
        grid=(nt,),
        in_specs=[idx(), idx(), pl.BlockSpec((tm, D_MODEL), lambda i: (i, 0))],
        out_specs=pl.BlockSpec(memory_space=pl.ANY),
        out_shape=jax.ShapeDtypeStruct((MOE_ROWS, D_MODEL), F32),
        scratch_shapes=[pltpu.SemaphoreType.DMA((2,))],
        compiler_params=_cparams(("arbitrary",)),
        name="moe_dispatch",
    )(pos1.reshape(nt, 1, tm), pos2.reshape(nt, 1, tm), h)


def _experts_kernel(vt_ref, ve_ref, nv_ref, lo_ref, hi_ref, xs_ref, wg_ref, wu_ref, wd_ref, y_ref):
    v = pl.program_id(0)

    @pl.when(v < nv_ref[0])
    def _():
        e = ve_ref[v]
        x = xs_ref[...].astype(BF16)
        hid = _silu(_dot(x, wg_ref[...])) * _dot(x, wu_ref[...])
        y = _dot(hid.astype(BF16), wd_ref[...])
        row = vt_ref[v] * MOE_TILE + lax.broadcasted_iota(jnp.int32, (MOE_TILE, 1), 0)
        mine = (row >= lo_ref[e]) & (row < hi_ref[e])
        first_visit = (v == 0) | (vt_ref[jnp.maximum(v - 1, 0)] != vt_ref[v])

        @pl.when(first_visit)
        def _():
            y_ref[...] = jnp.where(mine, y, 0.0)

        @pl.when(jnp.logical_not(first_visit))
        def _():
            y_ref[...] = jnp.where(mine, y, y_ref[...])


def _experts(xs, wg, wu, wd, vt, ve, nv, lo, hi):
    grid_spec = pltpu.PrefetchScalarGridSpec(
        num_scalar_prefetch=5,
        grid=(MOE_VISITS,),
        in_specs=[
            pl.BlockSpec((MOE_TILE, D_MODEL), lambda v, vt, ve, nv, lo, hi: (vt[v], 0)),
            pl.BlockSpec((None, D_MODEL, F_EXPERT), lambda v, vt, ve, nv, lo, hi: (ve[v], 0, 0)),
            pl.BlockSpec((None, D_MODEL, F_EXPERT), lambda v, vt, ve, nv, lo, hi: (ve[v], 0, 0)),
            pl.BlockSpec((None, F_EXPERT, D_MODEL), lambda v, vt, ve, nv, lo, hi: (ve[v], 0, 0)),
        ],
        out_specs=pl.BlockSpec((MOE_TILE, D_MODEL), lambda v, vt, ve, nv, lo, hi: (vt[v], 0)),
    )
    return pl.pallas_call(
        _experts_kernel,
        grid_spec=grid_spec,
        out_shape=jax.ShapeDtypeStruct((MOE_ROWS, D_MODEL), F32),
        compiler_params=_cparams(("arbitrary",)),
        name="moe_experts",
    )(vt, ve, nv, lo, hi, xs, wg, wu, wd)


def _combine_kernel(p1c_ref, p2c_ref, p1n_ref, p2n_ref, y_hbm, x_ref, mod_ref, gate_ref, fg_ref,
                    outp_ref, outs_ref, ya_buf, yb_buf, sem):
    i = pl.program_id(0)
    n = pl.num_programs(0)
    tm = x_ref.shape[0]
    slot = i % 2

    def gather(pa_ref, pb_ref, s):
        def issue(r, c):
            _row_copy(y_hbm, pa_ref[0, 0, r], ya_buf.at[s], r, sem.at[0, s]).start()
            _row_copy(y_hbm, pb_ref[0, 0, r], yb_buf.at[s], r, sem.at[1, s]).start()
            return c

        lax.fori_loop(0, tm, issue, 0, unroll=ROW_DMA_UNROLL)

    @pl.when(i == 0)
    def _():
        gather(p1c_ref, p2c_ref, 0)

    @pl.when(i + 1 < n)
    def _():
        gather(p1n_ref, p2n_ref, 1 - slot)

    pltpu.make_async_copy(y_hbm.at[pl.ds(0, tm)], ya_buf.at[slot], sem.at[0, slot]).wait()
    pltpu.make_async_copy(y_hbm.at[pl.ds(0, tm)], yb_buf.at[slot], sem.at[1, slot]).wait()
    g = gate_ref[...]
    mix = g[:, 0:1] * ya_buf[slot] + g[:, 1:2] * yb_buf[slot]
    xo = _rms(x_ref[...] + mod_ref[5:6, :] * mix, fg_ref[...])

    @pl.when(i < T_PROMPT // tm)
    def _():
        outp_ref[...] = xo

    @pl.when(i >= T_PROMPT // tm)
    def _():
        outs_ref[...] = xo


def _combine(y, x, mods, gate_cols, pos1, pos2, final_g):
    tm = 256
    nt = T_ALL // tm
    ntp = T_PROMPT // tm
    cur = lambda: pl.BlockSpec((1, 1, tm), lambda i: (i, 0, 0), memory_space=pltpu.SMEM)
    nxt = lambda: pl.BlockSpec((1, 1, tm), lambda i: (jnp.minimum(i + 1, nt - 1), 0, 0), memory_space=pltpu.SMEM)
    p1, p2 = pos1.reshape(nt, 1, tm), pos2.reshape(nt, 1, tm)
    return pl.pallas_call(
        _combine_kernel,
        grid=(nt,),
        in_specs=[
            cur(), cur(), nxt(), nxt(),
            pl.BlockSpec(memory_space=pl.ANY),
            pl.BlockSpec((tm, D_MODEL), lambda i: (i, 0)),
            pl.BlockSpec((None, 6, D_MODEL), lambda i: (_mod_group(i, tm), 0, 0)),
            pl.BlockSpec((tm, 128), lambda i: (i, 0)),
            pl.BlockSpec((1, D_MODEL), lambda i: (0, 0)),
        ],
        out_specs=[
            pl.BlockSpec((tm, D_MODEL), lambda i: (jnp.minimum(i, ntp - 1), 0)),
            pl.BlockSpec((tm, D_MODEL), lambda i: (jnp.maximum(i - ntp, 0), 0)),
        ],
        out_shape=[
            jax.ShapeDtypeStruct((T_PROMPT, D_MODEL), F32), jax.ShapeDtypeStruct((T_SAMPLE, D_MODEL), F32),
        ],
        scratch_shapes=[
            pltpu.VMEM((2, tm, D_MODEL), F32), pltpu.VMEM((2, tm, D_MODEL), F32),
            pltpu.SemaphoreType.DMA((2, 2)),
        ],
        compiler_params=_cparams(("arbitrary",)),
        name="moe_combine",
    )(p1, p2, p1, p2, y, x, mods, gate_cols, final_g)


def _route_plan(meta, counts):
    i1, i2 = meta[2].astype(jnp.int32), meta[3].astype(jnp.int32)
    r1, r2 = meta[4].astype(jnp.int32), meta[5].astype(jnp.int32)
    cnt = counts[:N_EXPERTS, 0].astype(jnp.int32)
    hi = jnp.cumsum(cnt)
    lo = hi - cnt
    ex = jnp.arange(N_EXPERTS, dtype=jnp.int32)
    pos1 = jnp.sum(jnp.where(i1[:, None] == ex[None, :], lo[None, :], 0), axis=1) + r1
    pos2 = jnp.sum(jnp.where(i2[:, None] == ex[None, :], lo[None, :], 0), axis=1) + r2
    first_tile = lo // MOE_TILE
    n_vis_e = jnp.where(cnt > 0, (hi - 1) // MOE_TILE - first_tile + 1, 0)
    vis_hi = jnp.cumsum(n_vis_e)
    vis_lo = vis_hi - n_vis_e
    nv = vis_hi[-1]
    v = jnp.minimum(jnp.arange(MOE_VISITS, dtype=jnp.int32), nv - 1)
    ve = jnp.minimum(jnp.sum(v[:, None] >= vis_hi[None, :], axis=1), N_EXPERTS - 1).astype(jnp.int32)
    pick = lambda tab: jnp.sum(jnp.where(ve[:, None] == ex[None, :], tab[None, :], 0), axis=1)
    vt = (pick(first_tile) + v - pick(vis_lo)).astype(jnp.int32)
    return pos1, pos2, vt, ve, nv.reshape(1).astype(jnp.int32), lo.astype(jnp.int32), hi.astype(jnp.int32)


def _ffn_kernel(*refs, n_cast):
    h_ref, x_hbm, mod_ref, wg_ref, wu_ref, wd_ref = refs[:6]
    cast_in = refs[6:6 + n_cast]
    out_ref = refs[6 + n_cast]
    cast_out = refs[7 + n_cast:7 + 2 * n_cast]
    x_buf, sem = refs[7 + 2 * n_cast:]
    _side_cast(cast_in, cast_out)
    i, f = pl.program_id(0), pl.program_id(1)
    tm = h_ref.shape[0]
    x_copy = pltpu.make_async_copy(x_hbm.at[pl.ds(pl.multiple_of(i * tm, tm), tm)], x_buf, sem)

    @pl.when(f == 0)
    def _():
        x_copy.start()
        out_ref[...] = jnp.zeros_like(out_ref)

    h = h_ref[...]
    hid = _silu(_dot(h, wg_ref[...])) * _dot(h, wu_ref[...])
    out_ref[...] += _dot(hid.astype(BF16), wd_ref[...])

    @pl.when(f == pl.num_programs(1) - 1)
    def _():
        x_copy.wait()
        out_ref[...] = x_buf[...] + mod_ref[5:6, :] * out_ref[...]


FFN_CAST_STEPS = 64


def _ffn(h, x, mods, wg, wu, wd, to_bf16):
    tm, tf = 1024, 512
    nf = F_DENSE // tf
    assert (T_ALL // tm) * nf >= FFN_CAST_STEPS

    cast_in, cast_out, cast_shapes = _side_cast_specs(to_bf16, FFN_CAST_STEPS, lambda i, f: i * nf + f)
    outs = pl.pallas_call(
        functools.partial(_ffn_kernel, n_cast=len(to_bf16)),
        grid=(T_ALL // tm, nf),
        in_specs=[
            pl.BlockSpec((tm, D_MODEL), lambda i, f: (i, 0)),
            _ANY,
            pl.BlockSpec((None, 6, D_MODEL), lambda i, f: (_mod_group(i, tm), 0, 0)),
            pl.BlockSpec((D_MODEL, tf), lambda i, f: (0, f)),
            pl.BlockSpec((D_MODEL, tf), lambda i, f: (0, f)),
            pl.BlockSpec((tf, D_MODEL), lambda i, f: (f, 0)),
        ] + cast_in,
        out_specs=[pl.BlockSpec((tm, D_MODEL), lambda i, f: (i, 0))] + cast_out,
        out_shape=[jax.ShapeDtypeStruct((T_ALL, D_MODEL), F32)] + cast_shapes,
        scratch_shapes=[pltpu.VMEM((tm, D_MODEL), F32), pltpu.SemaphoreType.DMA(())],
        compiler_params=_cparams(("arbitrary", "arbitrary")),
        name="dense_ffn",
    )(h, x, mods, wg, wu, wd, *[part[0] for part in to_bf16])
    return outs[0], outs[1:]


def _rope_tables():
    n = DEC_SEQ
    rows = n // GRID_W
    t_row = jnp.repeat(jnp.arange(rows, dtype=F32), GRID_W)
    t_col = jnp.tile(jnp.arange(GRID_W, dtype=F32), rows)
    inv = 1.0 / (ROPE_THETA ** (jnp.arange(0, ROT_HALF, 2, dtype=F32) / ROT_HALF))
    ar, ac = t_row[:, None] * inv, t_col[:, None] * inv
    cos = jnp.concatenate([jnp.cos(ar), jnp.cos(ar), jnp.cos(ac), jnp.cos(ac)], axis=-1)
    sin_signed = jnp.concatenate([-jnp.sin(ar), jnp.sin(ar), -jnp.sin(ac), jnp.sin(ac)], axis=-1)
    return cos, sin_signed


def _pad_lanes(v, width=128):
    return jnp.pad(v, ((0, 0), (0, width - v.shape[-1])))


def kernel(x_prompt, x_sample, c, cache_k, cache_v, state_ssm_fwd, state_ssm_bwd, c_ctx, ada_w, ada_b, norm1_g, norm2_g, w_in, q_norm_g, k_norm_g, conv_w, conv_b, a_log_fwd, a_log_bwd, dt_bias_fwd, dt_bias_bwd, d_skip, ssd_norm_g, attn_out_g, w_out, ffn_w_gate, ffn_w_up, ffn_w_down, router_w, moe_w_gate, moe_w_up, moe_w_down, final_norm_g):
    assert DEPTH % 2 == 0
    cond = jnp.concatenate([c_ctx[None, :], c, jnp.zeros((N_COND - 1 - DEC_BATCH, D_MODEL), F32)], axis=0)
    mods_all = _ada_mods(cond, ada_w, ada_b).reshape(DEPTH, N_COND, 6, D_MODEL)
    cos, sin_signed = _rope_tables()

    w_in_t = jnp.swapaxes(w_in, 1, 2)
    x_segs = [x_prompt.reshape(T_PROMPT, D_MODEL), x_sample.reshape(T_SAMPLE, D_MODEL)]
    kv, states = None, None
    for l in range(DEPTH):
        mods = mods_all[l]
        w_out_rows = (w_out.reshape(DEPTH * D_MODEL, D_MODEL), l * D_MODEL, D_MODEL)
        proj, dt_raw, (w_o,) = _inproj(x_segs, mods, norm1_g[l][None, :], w_in_t, l, [w_out_rows])

        qg, kg, og = q_norm_g[l][None, :], k_norm_g[l][None, :], attn_out_g[l][None, :]
        o_p, k_all, v_all = _attention_prompt(proj, 0, qg, kg, og, l, kv)
        kv = (k_all, v_all)
        o_s = _attention_sample(proj, T_PROMPT, qg, kg, og, cache_k, cache_v, cos, sin_signed, l)

        p = {
            'conv_w': conv_w[l], 'conv_b': conv_b[l][None, :],
            'dt_bias': _pad_lanes(jnp.concatenate([dt_bias_fwd[l], dt_bias_bwd[l]])[None, :]),
            'a_log': _pad_lanes(jnp.concatenate([a_log_fwd[l], a_log_bwd[l]])[None, :]),
            'd_skip': jnp.repeat(d_skip[l], SSD_HEAD_DIM)[None, :],
            'ssd_norm_g': ssd_norm_g[l][None, :],
        }
        j = l // 2
        n_up, n_down = N_EXPERTS * D_MODEL, N_EXPERTS * F_EXPERT
        if l % 2 == 0:
            parts = [(ffn_w_gate.reshape(-1, F_DENSE), j * D_MODEL, D_MODEL),
                     (ffn_w_up.reshape(-1, F_DENSE), j * D_MODEL, D_MODEL),
                     (ffn_w_down.reshape(-1, D_MODEL), j * F_DENSE, F_DENSE)]
        else:
            parts = [(moe_w_down.reshape(-1, D_MODEL), j * n_down, n_down)]
        y_p, sf, sb, *mixer_w = _ssd(proj, dt_raw, p, SEQ, BATCH, 0, l, None, states, parts)
        states = (sf, sb)
        y_s, = _ssd(proj, dt_raw, p, DEC_SEQ, DEC_BATCH, T_PROMPT // DEC_SEQ, l, (state_ssm_fwd, state_ssm_bwd), None)

        g2 = norm2_g[l][None, :]
        if l % 2 == 0:
            x, h = _outproj([o_p, o_s], [y_p, y_s], x_segs, mods, g2, w_o, None)
            x, (eg, eu) = _ffn(h, x, mods, *mixer_w, [
                (moe_w_gate.reshape(-1, F_EXPERT), j * n_up, n_up),
                (moe_w_up.reshape(-1, F_EXPERT), j * n_up, n_up)])
            x_segs = [x]
        else:
            router_wt = jnp.pad(router_w[j].T, ((0, 16 - N_EXPERTS), (0, 0)))
            x, h, meta, counts = _outproj([o_p, o_s], [y_p, y_s], x_segs, mods, g2, w_o, router_wt)
            pos1, pos2, vt, ve, nv, lo, hi = _route_plan(meta, counts)
            xs = _dispatch(h, pos1, pos2)
            ys = _experts(xs, eg.reshape(N_EXPERTS, D_MODEL, F_EXPERT), eu.reshape(N_EXPERTS, D_MODEL, F_EXPERT),
                          mixer_w[0].reshape(N_EXPERTS, F_EXPERT, D_MODEL), vt, ve, nv, lo, hi)
            y_prompt, y_sample = _combine(ys, x, mods, _pad_lanes(meta[:2].T), pos1, pos2, final_norm_g[None, :])

    return (y_prompt.reshape(BATCH, SEQ, D_MODEL), y_sample.reshape(DEC_BATCH, DEC_SEQ, D_MODEL),
            kv[0], kv[1], states[0], states[1])
```

```python
import functools

import jax
import jax.numpy as jnp
from jax import lax
from jax.experimental import pallas as pl
from jax.experimental.pallas import tpu as pltpu

F32 = jnp.float32
BF16 = jnp.bfloat16

D_MODEL = 2048
BATCH = 16
SEQ = 256
DEPTH = 2
DEC_BATCH = 2
DEC_SEQ = 1024
PAST_LEN = 512
GRID_W = 64
D_ATTN = 1024
D_SSD = 1024
HEAD_DIM = 128
N_Q_HEADS = 8
N_KV_HEADS = 2
Q_PER_KV = 4
KV_DIM = 256
ROT_HALF = 64
ROPE_THETA = 10000.0
SSD_HEAD_DIM = 64
SSD_HEADS = 16
SSD_GROUPS = 2
HEADS_PER_GROUP = 8
D_STATE = 128
CONV_DIM = 1536
CHUNK = 128
N_MAIN = 4096
F_DENSE = 5632
N_EXPERTS = 8
F_EXPERT = 1024
EPS = 1e-6

T_PROMPT = BATCH * SEQ
T_SAMPLE = DEC_BATCH * DEC_SEQ
T_ALL = T_PROMPT + T_SAMPLE
N_COND = 16

VMEM_LIMIT = 58 * 1024 * 1024


def _cparams(sem):
    return pltpu.CompilerParams(dimension_semantics=sem, vmem_limit_bytes=VMEM_LIMIT)


def _mod_group(i, tm):
    return jnp.maximum(0, (i * tm - T_PROMPT + DEC_SEQ) // DEC_SEQ)


def _silu(x):
    return x * jax.nn.sigmoid(x)


def _rms(x, g):
    ms = jnp.mean(x * x, axis=-1, keepdims=True)
    return x * lax.rsqrt(ms + EPS) * g


def _dot(a, b):
    return jnp.dot(a, b, preferred_element_type=F32)


def _dot_nt(a, b):
    return lax.dot_general(a, b, (((1,), (1,)), ((), ())), preferred_element_type=F32)


def _split3(x):
    hi = x.astype(BF16)
    r1 = x - hi.astype(F32)
    mid = r1.astype(BF16)
    r2 = r1 - mid.astype(F32)
    return hi, mid, r2.astype(BF16)


def _ada_kernel(c_ref, w_ref, b_ref, o_ref):
    s = _silu(c_ref[...]).astype(BF16)
    o_ref[...] = _dot(s, w_ref[...].astype(BF16)) + b_ref[...]


def _ada_mods(cond, ada_w, ada_b):
    tn = 1024
    n_out = 6 * D_MODEL
    return pl.pallas_call(
        _ada_kernel,
        grid=(DEPTH, n_out // tn),
        in_specs=[
            pl.BlockSpec((N_COND, D_MODEL), lambda l, j: (0, 0)),
            pl.BlockSpec((None, D_MODEL, tn), lambda l, j: (l, 0, j)),
            pl.BlockSpec((None, 1, tn), lambda l, j: (l, 0, j)),
        ],
        out_specs=pl.BlockSpec((None, N_COND, tn), lambda l, j: (l, 0, j)),
        out_shape=jax.ShapeDtypeStruct((DEPTH, N_COND, n_out), F32),
        compiler_params=_cparams(("parallel", "parallel")),
        name="ada_mods",
    )(cond, ada_w, ada_b.reshape(DEPTH, 1, n_out))


def _side_cast_specs(parts, n_steps, linear_step):
    in_specs, out_specs, out_shapes = [], [], []
    for a, row0, nrows in parts:
        rows = nrows // n_steps
        blk0 = row0 // rows
        step = lambda *ids: jnp.minimum(linear_step(*ids), n_steps - 1)
        in_specs.append(pl.BlockSpec((rows, a.shape[1]), lambda *ids, blk0=blk0: (blk0 + step(*ids), 0)))
        out_specs.append(pl.BlockSpec((rows, a.shape[1]), lambda *ids: (step(*ids), 0)))
        out_shapes.append(jax.ShapeDtypeStruct((nrows, a.shape[1]), BF16))
    return in_specs, out_specs, out_shapes


def _side_cast(cast_in, cast_out):
    for src, dst in zip(cast_in, cast_out):
        dst[...] = src[...].astype(BF16)


INPROJ_NORM_ROWS = 1024
INPROJ_CAST_STEPS = 16


def _inproj_kernel(*refs, seg_rows, n_cast):
    n_seg = len(seg_rows)
    x_hbms = refs[:n_seg]
    mods_ref, g_ref, w_ref, wdt_ref = refs[n_seg:n_seg + 4]
    cast_in = refs[n_seg + 4:n_seg + 4 + n_cast]
    proj_ref, dt_ref = refs[n_seg + 4 + n_cast:n_seg + 6 + n_cast]
    cast_out = refs[n_seg + 6 + n_cast:n_seg + 6 + 2 * n_cast]
    x_buf, h_scr, sem = refs[n_seg + 6 + 2 * n_cast:]
    _side_cast(cast_in, cast_out)
    i, j = pl.program_id(0), pl.program_id(1)
    tm = x_buf.shape[0]

    def fetch(tile):
        start = 0
        for x_hbm, nrows in zip(x_hbms, seg_rows):
            b0, nb = start // tm, nrows // tm
            start += nrows

            @pl.when((tile >= b0) & (tile < b0 + nb))
            def _():
                r0 = pl.multiple_of((tile - b0) * tm, tm)
                pltpu.make_async_copy(x_hbm.at[pl.ds(r0, tm)], x_buf, sem).start()

    @pl.when(j == 0)
    def _():
        @pl.when(i == 0)
        def _():
            fetch(i)

        pltpu.make_async_copy(x_hbms[0].at[pl.ds(0, tm)], x_buf, sem).wait()
        for k in range(tm // INPROJ_NORM_ROWS):
            rows = slice(k * INPROJ_NORM_ROWS, (k + 1) * INPROJ_NORM_ROWS)
            mod = mods_ref[_mod_group(i * (tm // INPROJ_NORM_ROWS) + k, INPROJ_NORM_ROWS)]
            h = _rms(x_buf[rows, :], g_ref[...]) * (1.0 + mod[1:2, :]) + mod[0:1, :]
            h_scr[rows, :] = h.astype(BF16)

        @pl.when(i + 1 < pl.num_programs(0))
        def _():
            fetch(i + 1)

        n_dt = wdt_ref.shape[0]
        wdt = jnp.concatenate([wdt_ref[...], jnp.zeros((128 - n_dt, D_MODEL), F32)], axis=0)
        dt_ref[...] = _dot_nt(h_scr[...], wdt.astype(BF16))

    proj_ref[...] = _dot_nt(h_scr[...], w_ref[...].astype(BF16))


_ANY = pl.BlockSpec(memory_space=pl.ANY)


def _inproj(x_segs, mods, g, w_in_t, layer, to_bf16):
    tm, tn = 2048, 512
    n_dt = w_in_t.shape[1] - N_MAIN
    nj = N_MAIN // tn
    seg_rows = tuple(a.shape[0] for a in x_segs)
    assert all(r % tm == 0 for r in seg_rows) and sum(seg_rows) == T_ALL
    assert (T_ALL // tm) * nj >= INPROJ_CAST_STEPS
    cast_in, cast_out, cast_shapes = _side_cast_specs(to_bf16, INPROJ_CAST_STEPS, lambda i, j: i * nj + j)
    outs = pl.pallas_call(
        functools.partial(_inproj_kernel, seg_rows=seg_rows, n_cast=len(to_bf16)),
        grid=(T_ALL // tm, nj),
        in_specs=[_ANY] * len(x_segs) + [
            pl.BlockSpec(mods.shape, lambda i, j: (0, 0, 0)),
            pl.BlockSpec((1, D_MODEL), lambda i, j: (0, 0)),
            pl.BlockSpec((None, tn, D_MODEL), lambda i, j: (layer, j, 0)),
            pl.BlockSpec((None, n_dt, D_MODEL), lambda i, j: (layer, N_MAIN // n_dt, 0)),
        ] + cast_in,
        out_specs=[
            pl.BlockSpec((tm, tn), lambda i, j: (i, j)),
            pl.BlockSpec((tm, 128), lambda i, j: (i, 0)),
        ] + cast_out,
        out_shape=[
            jax.ShapeDtypeStruct((T_ALL, N_MAIN), F32),
            jax.ShapeDtypeStruct((T_ALL, 128), F32),
        ] + cast_shapes,
        scratch_shapes=[pltpu.VMEM((tm, D_MODEL), F32), pltpu.VMEM((tm, D_MODEL), BF16),
                        pltpu.SemaphoreType.DMA(())],
        compiler_params=_cparams(("arbitrary", "arbitrary")),
        name="inproj",
    )(*x_segs, mods, g, w_in_t, w_in_t, *[part[0] for part in to_bf16])
    return outs[0], outs[1], outs[2:]


def _rope(x, cos, sin_signed):
    lane = lax.broadcasted_iota(jnp.int32, x.shape, 1)
    first = (lane // (ROT_HALF // 2)) % 2 == 0
    swapped = jnp.where(first, pltpu.roll(x, HEAD_DIM - ROT_HALF // 2, 1), pltpu.roll(x, ROT_HALF // 2, 1))
    return x * cos + swapped * sin_signed


N_SCORE_BUFS = 3


def _attn_kernel(*refs, nk_new, has_ctx, n_prev):
    if has_ctx:
        (q_ref, kv_ref, qg_ref, kg_ref, og_ref, ck_ref, cv_ref, cq_ref, sq_ref, ckk_ref, skk_ref,
         o_ref, kb_scr, vb_scr, o_scr, s_scr) = refs
    elif n_prev:
        (q_ref, kv_ref, qg_ref, kg_ref, og_ref, pk_ref, pv_ref, o_ref, ko_ref, vo_ref,
         kb_scr, vb_scr, o_scr, s_scr) = refs
    else:
        (q_ref, kv_ref, qg_ref, kg_ref, og_ref, o_ref, ko_ref, vo_ref, kb_scr, vb_scr, o_scr, s_scr) = refs

    @pl.when(pl.program_id(1) == 0)
    def _():
        if not has_ctx and n_prev:
            ko_ref[0:n_prev] = pk_ref[...]
            vo_ref[0:n_prev] = pv_ref[...]
        for g in range(N_KV_HEADS):
            sl = slice(g * HEAD_DIM, (g + 1) * HEAD_DIM)
            kn = _rms(kv_ref[:, sl], kg_ref[...])
            v = kv_ref[:, KV_DIM + g * HEAD_DIM:KV_DIM + (g + 1) * HEAD_DIM]
            vsl = slice(2 * g * HEAD_DIM, (2 * g + 1) * HEAD_DIM)
            vb_scr[0:nk_new, vsl] = v.astype(BF16)
            vb_scr[:, (2 * g + 1) * HEAD_DIM:(2 * g + 2) * HEAD_DIM] = jnp.ones((vb_scr.shape[0], HEAD_DIM), BF16)
            if has_ctx:
                kb_scr[0:nk_new, sl] = _rope(kn, ckk_ref[...], skk_ref[...]).astype(BF16)
                kb_scr[nk_new:, sl] = ck_ref[:, g, :].astype(BF16)
                vb_scr[nk_new:, vsl] = cv_ref[:, g, :].astype(BF16)
            else:
                kb_scr[:, sl] = kn.astype(BF16)
                ko_ref[n_prev, :, g, :] = kn
                vo_ref[n_prev, :, g, :] = v

    scale_log2e = HEAD_DIM ** -0.5 * 1.4426950408889634

    def scores(h):
        g = h // Q_PER_KV
        qn = _rms(q_ref[:, h * HEAD_DIM:(h + 1) * HEAD_DIM], qg_ref[...])
        if has_ctx:
            qn = _rope(qn, cq_ref[...], sq_ref[...])
        qb = (qn * scale_log2e).astype(BF16)
        s_scr[h % N_SCORE_BUFS] = _dot_nt(qb, kb_scr[:, g * HEAD_DIM:(g + 1) * HEAD_DIM])

    for h in range(N_SCORE_BUFS - 1):
        scores(h)
    for h in range(N_Q_HEADS):
        if h + N_SCORE_BUFS - 1 < N_Q_HEADS:
            scores(h + N_SCORE_BUFS - 1)
        g = h // Q_PER_KV
        s = s_scr[h % N_SCORE_BUFS]
        e = jnp.exp2(s - jnp.max(s, axis=-1, keepdims=True))
        pv = _dot(e.astype(BF16), vb_scr[:, 2 * g * HEAD_DIM:(2 * g + 2) * HEAD_DIM])
        o_scr[:, h * HEAD_DIM:(h + 1) * HEAD_DIM] = pv[:, :HEAD_DIM] / pv[:, HEAD_DIM:]
    o_ref[...] = _rms(o_scr[...], og_ref[...]).astype(BF16)


def _attention_prompt(proj, row0, qg, kg, og, layer, prev_kv):
    n = SEQ
    blk0 = row0 // n
    kern = functools.partial(_attn_kernel, nk_new=n, has_ctx=False, n_prev=layer)
    vec = lambda w: pl.BlockSpec((1, w), lambda b, i: (0, 0))
    cache_spec = lambda k: pl.BlockSpec((None, k, n, N_KV_HEADS, HEAD_DIM), lambda b, i: (b, 0, 0, 0, 0))
    cache_shape = jax.ShapeDtypeStruct((BATCH, layer + 1, n, N_KV_HEADS, HEAD_DIM), F32)
    in_specs = [
        pl.BlockSpec((n, D_ATTN), lambda b, i: (blk0 + b, 0)),
        pl.BlockSpec((n, 2 * KV_DIM), lambda b, i: (blk0 + b, 2)),
        vec(HEAD_DIM), vec(HEAD_DIM), vec(D_ATTN),
    ]
    args = [proj, proj, qg, kg, og]
    if layer:
        in_specs += [cache_spec(layer), cache_spec(layer)]
        args += list(prev_kv)
    return pl.pallas_call(
        kern,
        grid=(BATCH, 1),
        in_specs=in_specs,
        out_specs=[pl.BlockSpec((n, D_ATTN), lambda b, i: (b, 0)), cache_spec(layer + 1), cache_spec(layer + 1)],
        out_shape=[jax.ShapeDtypeStruct((T_PROMPT, D_ATTN), BF16), cache_shape, cache_shape],
        scratch_shapes=[
            pltpu.VMEM((n, KV_DIM), BF16), pltpu.VMEM((n, 2 * KV_DIM), BF16), pltpu.VMEM((n, D_ATTN), F32),
            pltpu.VMEM((N_SCORE_BUFS, n, n), F32),
        ],
        compiler_params=_cparams(("parallel", "arbitrary")),
        name="attn_prompt",
    )(*args)


def _attention_sample(proj, row0, qg, kg, og, ck, cv, cos, sin_signed, layer):
    n, tq = DEC_SEQ, 512
    nq = n // tq
    nk = n + PAST_LEN
    kern = functools.partial(_attn_kernel, nk_new=n, has_ctx=True, n_prev=0)
    vec = lambda w: pl.BlockSpec((1, w), lambda b, i: (0, 0))
    q_blk0 = row0 // tq
    kv_blk0 = row0 // n
    return pl.pallas_call(
        kern,
        grid=(DEC_BATCH, nq),
        in_specs=[
            pl.BlockSpec((tq, D_ATTN), lambda b, i: (q_blk0 + b * nq + i, 0)),
            pl.BlockSpec((n, 2 * KV_DIM), lambda b, i: (kv_blk0 + b, 2)),
            vec(HEAD_DIM), vec(HEAD_DIM), vec(D_ATTN),
            pl.BlockSpec((None, None, PAST_LEN, N_KV_HEADS, HEAD_DIM), lambda b, i: (b, layer, 0, 0, 0)),
            pl.BlockSpec((None, None, PAST_LEN, N_KV_HEADS, HEAD_DIM), lambda b, i: (b, layer, 0, 0, 0)),
            pl.BlockSpec((tq, HEAD_DIM), lambda b, i: (i, 0)),
            pl.BlockSpec((tq, HEAD_DIM), lambda b, i: (i, 0)),
            pl.BlockSpec((n, HEAD_DIM), lambda b, i: (0, 0)),
            pl.BlockSpec((n, HEAD_DIM), lambda b, i: (0, 0)),
        ],
        out_specs=pl.BlockSpec((tq, D_ATTN), lambda b, i: (b * nq + i, 0)),
        out_shape=jax.ShapeDtypeStruct((T_SAMPLE, D_ATTN), BF16),
        scratch_shapes=[
            pltpu.VMEM((nk, KV_DIM), BF16), pltpu.VMEM((nk, 2 * KV_DIM), BF16), pltpu.VMEM((tq, D_ATTN), F32),
            pltpu.VMEM((N_SCORE_BUFS, tq, nk), F32),
        ],
        compiler_params=_cparams(("parallel", "arbitrary")),
        name="attn_sample",
    )(proj, proj, qg, kg, og, ck, cv, cos, sin_signed, cos, sin_signed)


def _conv_silu(x, w, b):
    n = x.shape[0]
    row = lax.broadcasted_iota(jnp.int32, (n, 1), 0)
    prev = jnp.where(row == 0, 0.0, pltpu.roll(x, 1, 0))
    nxt = jnp.where(row == n - 1, 0.0, pltpu.roll(x, n - 1, 0))
    return _silu(prev * w[0:1, :] + x * w[1:2, :] + nxt * w[2:3, :] + b)


def _softplus(x):
    return jnp.maximum(x, 0.0) + jnp.log1p(jnp.exp(-jnp.abs(x)))


def _ssd_kernel(*refs, n, has_init, n_prev, n_cast):
    refs = list(refs)
    (za_ref, zb_ref, xa_ref, xb_ref, bc_ref, dt_ref, cw_ref, cb_ref, dtb_ref, alog_ref, dsk_ref,
     ng_ref) = refs[:12]
    del refs[:12]
    if has_init:
        sf0_ref, sb0_ref = refs.pop(0), refs.pop(0)
    elif n_prev:
        psf_ref, psb_ref = refs.pop(0), refs.pop(0)
    cast_in = [refs.pop(0) for _ in range(n_cast)]
    y_ref = refs.pop(0)
    if not has_init:
        sf_ref, sb_ref = refs.pop(0), refs.pop(0)
    cast_out = [refs.pop(0) for _ in range(n_cast)]
    xc_scr, bcc_scr, dts_scr, xt_scr, yt_scr, s_scr = refs
    _side_cast(cast_in, cast_out)
    nc = n // CHUNK
    gw = HEADS_PER_GROUP * SSD_HEAD_DIM

    xc_scr[:, 0:gw] = _conv_silu(xa_ref[...], cw_ref[:, 0:gw], cb_ref[:, 0:gw])
    xc_scr[:, gw:] = _conv_silu(xb_ref[...], cw_ref[:, gw:2 * gw], cb_ref[:, gw:2 * gw])
    bcc_scr[...] = _conv_silu(bc_ref[...], cw_ref[:, 2 * gw:], cb_ref[:, 2 * gw:])
    dts_scr[...] = _softplus(dt_ref[...] + dtb_ref[...])
    for g in range(SSD_GROUPS):
        hs = slice(g * HEADS_PER_GROUP, (g + 1) * HEADS_PER_GROUP)
        if has_init:
            s_scr[0, g] = sf0_ref[hs].reshape(gw, D_STATE)
            s_scr[1, g] = sb0_ref[hs].reshape(gw, D_STATE)
        else:
            s_scr[0, g] = jnp.zeros((gw, D_STATE), F32)
            s_scr[1, g] = jnp.zeros((gw, D_STATE), F32)

    def to_channel_major(c, carry):
        rows = pl.ds(pl.multiple_of(c * CHUNK, CHUNK), CHUNK)
        xt_scr[c] = xc_scr[rows, :].T
        yt_scr[c] = jnp.zeros((D_SSD, CHUNK), F32)
        return carry

    lax.fori_loop(0, nc, to_channel_major, 0)

    a_row = -jnp.exp(alog_ref[...])
    ri = lax.broadcasted_iota(jnp.int32, (CHUNK, CHUNK), 0)
    ci = lax.broadcasted_iota(jnp.int32, (CHUNK, CHUNK), 1)
    lower, upper = ci <= ri, ci >= ri
    n_dirs_heads = 2 * SSD_HEADS

    def chunk_sums(dirn, c):
        tri = (lower if dirn == 0 else upper).astype(BF16)
        valid_st = upper if dirn == 0 else lower
        tri_t = valid_st.astype(BF16)
        rows = pl.ds(pl.multiple_of(c * CHUNK, CHUNK), CHUNK)
        dt = dts_scr[rows, :]
        d = dt * a_row
        d1, d2, d3 = _split3(d)
        cs = _dot(tri, d1) + _dot(tri, d2) + _dot(tri, d3)
        dt_t = dt.T[0:n_dirs_heads, :]
        e1, e2, e3 = _split3(d.T[0:n_dirs_heads, :])
        cs_t = _dot(e1, tri_t) + _dot(e2, tri_t) + _dot(e3, tri_t)
        total = cs_t[:, CHUNK - 1:CHUNK] if dirn == 0 else cs_t[:, 0:1]
        return dict(
            dirn=dirn, c=c, rows=rows, valid_st=valid_st, cs=cs, cs_t=cs_t, dt_t=dt_t,
            e_in_t=jnp.exp(cs_t),
            to_end_t=jnp.exp(total - cs_t) * dt_t,
            dec_t=jnp.broadcast_to(jnp.exp(total), (n_dirs_heads, D_STATE)))

    def group_inputs(q, g):
        bm = bcc_scr[q['rows'], g * D_STATE:(g + 1) * D_STATE].astype(BF16)
        cm = bcc_scr[q['rows'], (SSD_GROUPS + g) * D_STATE:(SSD_GROUPS + g + 1) * D_STATE]
        return dict(bm=bm, g_st=_dot_nt(bm, cm.astype(BF16)),
                    c_nt=cm.T,
                    st=s_scr[q['dirn'], g],
                    xs_parts=[], dec_parts=[])

    def head(q, gi, g, hh):
        h = g * HEADS_PER_GROUP + hh
        r = q['dirn'] * SSD_HEADS + h
        c = q['c']
        ch = slice(h * SSD_HEAD_DIM, (h + 1) * SSD_HEAD_DIM)
        x_t = xt_scr[c, ch, :]
        diff = q['cs_t'][r:r + 1, :] - q['cs'][:, r:r + 1]
        a_st = (gi['g_st'] * jnp.exp(jnp.where(q['valid_st'], diff, -jnp.inf))).astype(BF16)
        c_e = (gi['c_nt'] * q['e_in_t'][r:r + 1, :]).astype(BF16)
        x_dt = (x_t * q['dt_t'][r:r + 1, :]).astype(BF16)
        s_h = gi['st'][hh * SSD_HEAD_DIM:(hh + 1) * SSD_HEAD_DIM, :].astype(BF16)
        y_h = _dot(jnp.concatenate([x_dt, s_h], axis=1), jnp.concatenate([a_st, c_e], axis=0))
        yt_scr[c, ch, :] = yt_scr[c, ch, :] + y_h
        gi['xs_parts'].append((x_t * q['to_end_t'][r:r + 1, :]).astype(BF16))
        gi['dec_parts'].append(jnp.broadcast_to(q['dec_t'][r:r + 1, :], (SSD_HEAD_DIM, D_STATE)))

    def group_update(q, gi, g):
        ds = _dot(jnp.concatenate(gi['xs_parts'], axis=0), gi['bm'])
        s_scr[q['dirn'], g] = gi['st'] * jnp.concatenate(gi['dec_parts'], axis=0) + ds

    def body(i, carry):
        qs = [chunk_sums(0, i), chunk_sums(1, nc - 1 - i)]
        chains = [(q, group_inputs(q, g), g) for g in range(SSD_GROUPS) for q in qs]
        for hh in range(HEADS_PER_GROUP):
            for q, gi, g in chains:
                head(q, gi, g, hh)
        for q, gi, g in chains:
            group_update(q, gi, g)
        return carry

    lax.fori_loop(0, nc, body, 0)

    def finish(c, carry):
        rows = pl.ds(pl.multiple_of(c * CHUNK, CHUNK), CHUNK)
        y = yt_scr[c].T + xc_scr[rows, :] * dsk_ref[...]
        ya = y[:, 0:gw] * _silu(za_ref[rows, :])
        yb = y[:, gw:] * _silu(zb_ref[rows, :])
        ms = (jnp.sum(ya * ya, axis=-1, keepdims=True) + jnp.sum(yb * yb, axis=-1, keepdims=True)) / D_SSD
        inv = lax.rsqrt(ms + EPS)
        y_ref[rows, 0:gw] = (ya * inv * ng_ref[:, 0:gw]).astype(BF16)
        y_ref[rows, gw:] = (yb * inv * ng_ref[:, gw:]).astype(BF16)
        return carry

    lax.fori_loop(0, nc, finish, 0)

    if not has_init:
        if n_prev:
            sf_ref[0:n_prev] = psf_ref[...]
            sb_ref[0:n_prev] = psb_ref[...]
        for g in range(SSD_GROUPS):
            hs = slice(g * HEADS_PER_GROUP, (g + 1) * HEADS_PER_GROUP)
            sf_ref[n_prev, hs] = s_scr[0, g].reshape(HEADS_PER_GROUP, SSD_HEAD_DIM, D_STATE)
            sb_ref[n_prev, hs] = s_scr[1, g].reshape(HEADS_PER_GROUP, SSD_HEAD_DIM, D_STATE)


def _ssd(proj, dt_raw, p, n, nb, row_blk0, layer, init, prev_states, to_bf16=()):
    has_init = init is not None
    n_prev = 0 if has_init else layer
    kern = functools.partial(_ssd_kernel, n=n, has_init=has_init, n_prev=n_prev, n_cast=len(to_bf16))
    cast_in, cast_out, cast_shapes = _side_cast_specs(to_bf16, nb, lambda b: b)
    col = lambda cb: pl.BlockSpec((n, 512), lambda b: (row_blk0 + b, cb))
    vec = lambda r, w: pl.BlockSpec((r, w), lambda b: (0, 0))
    layers_spec = lambda k: pl.BlockSpec((None, k, SSD_HEADS, SSD_HEAD_DIM, D_STATE), lambda b: (b, 0, 0, 0, 0))
    in_specs = [
        col(3), col(4), col(5), col(6), col(7),
        pl.BlockSpec((n, 128), lambda b: (row_blk0 + b, 0)),
        vec(3, CONV_DIM), vec(1, CONV_DIM), vec(1, 128), vec(1, 128), vec(1, D_SSD), vec(1, D_SSD),
    ]
    args = [proj, proj, proj, proj, proj, dt_raw, p['conv_w'], p['conv_b'], p['dt_bias'], p['a_log'],
            p['d_skip'], p['ssd_norm_g']]
    y_spec = pl.BlockSpec((n, D_SSD), lambda b: (b, 0))
    y_shape = jax.ShapeDtypeStruct((nb * n, D_SSD), BF16)
    if has_init:
        init_spec = pl.BlockSpec((None, None, SSD_HEADS, SSD_HEAD_DIM, D_STATE), lambda b: (b, layer, 0, 0, 0))
        in_specs += [init_spec, init_spec]
        args += list(init)
        out_specs, out_shape = [y_spec], [y_shape]
    else:
        if n_prev:
            in_specs += [layers_spec(n_prev), layers_spec(n_prev)]
            args += list(prev_states)
        st_shape = jax.ShapeDtypeStruct((nb, layer + 1, SSD_HEADS, SSD_HEAD_DIM, D_STATE), F32)
        out_specs = [y_spec, layers_spec(layer + 1), layers_spec(layer + 1)]
        out_shape = [y_shape, st_shape, st_shape]
    return pl.pallas_call(
        kern,
        grid=(nb,),
        in_specs=in_specs + cast_in,
        out_specs=out_specs + cast_out,
        out_shape=out_shape + cast_shapes,
        scratch_shapes=[
            pltpu.VMEM((n, D_SSD), F32), pltpu.VMEM((n, 512), F32), pltpu.VMEM((n, 128), F32),
            pltpu.VMEM((n // CHUNK, D_SSD, CHUNK), F32), pltpu.VMEM((n // CHUNK, D_SSD, CHUNK), F32),
            pltpu.VMEM((2, SSD_GROUPS, 512, D_STATE), F32),
        ],
        compiler_params=_cparams(("parallel",)),
        name="ssd_sample" if has_init else "ssd_prompt",
    )(*args, *[part[0] for part in to_bf16])


def _seg_specs(segs, tm):
    specs, bounds, start = [], [], 0
    for a in segs:
        b0, nblk = start // tm, a.shape[0] // tm
        specs.append(pl.BlockSpec((tm, a.shape[1]), lambda i, b0=b0, nblk=nblk: (jnp.clip(i - b0, 0, nblk - 1), 0)))
        bounds.append(b0)
        start += a.shape[0]
    return specs, tuple(bounds)


def _seg_pick(refs, bounds):
    i = pl.program_id(0)
    v = refs[0][...]
    for ref, b0 in zip(refs[1:], bounds[1:]):
        v = jnp.where(i >= b0, ref[...], v)
    return v


def _outproj_kernel(*refs, with_router, o_bounds, y_bounds, x_bounds):
    refs = list(refs)
    o_refs = [refs.pop(0) for _ in o_bounds]
    y_refs = [refs.pop(0) for _ in y_bounds]
    x_refs = [refs.pop(0) for _ in x_bounds]
    if with_router:
        mod_ref, g_ref, w_ref, rw_ref, xo_ref, h_ref, meta_ref, cnt_ref, carry_scr = refs
    else:
        mod_ref, g_ref, w_ref, xo_ref, h_ref = refs
    tm = xo_ref.shape[0]
    a = jnp.concatenate([_seg_pick(o_refs, o_bounds), _seg_pick(y_refs, y_bounds)], axis=1)
    xn = _seg_pick(x_refs, x_bounds) + mod_ref[2:3, :] * _dot(a, w_ref[...])
    xo_ref[...] = xn
    h = _rms(xn, g_ref[...]) * (1.0 + mod_ref[4:5, :]) + mod_ref[3:4, :]
    h_ref[...] = h.astype(h_ref.dtype)
    if with_router:

        @pl.when(pl.program_id(0) == 0)
        def _():
            carry_scr[...] = jnp.zeros_like(carry_scr)

        h1, h2, _ = _split3(h)
        w1, w2, _ = _split3(rw_ref[...])
        logits = _dot_nt(w1, h1) + _dot_nt(w2, h1) + _dot_nt(w1, h2)
        row = lax.broadcasted_iota(jnp.int32, logits.shape, 0)
        logits = jnp.where(row < N_EXPERTS, logits, -jnp.inf)
        e = jnp.exp(logits - jnp.max(logits, axis=0, keepdims=True))
        probs = e / jnp.sum(e, axis=0, keepdims=True)
        p1 = jnp.max(probs, axis=0, keepdims=True)
        i1 = jnp.min(jnp.where(probs == p1, row, 16), axis=0, keepdims=True)
        rest = jnp.where(row == i1, -1.0, probs)
        p2 = jnp.max(rest, axis=0, keepdims=True)
        i2 = jnp.min(jnp.where(rest == p2, row, 16), axis=0, keepdims=True)
        hit1, hit2 = row == i1, row == i2
        onehot = jnp.where(hit1 | hit2, 1.0, 0.0)
        ti = lax.broadcasted_iota(jnp.int32, (tm, tm), 0)
        tj = lax.broadcasted_iota(jnp.int32, (tm, tm), 1)
        before = jnp.where(ti < tj, 1.0, 0.0).astype(BF16)
        rank = carry_scr[:, 0:1] + _dot(onehot.astype(BF16), before)
        r1 = jnp.sum(jnp.where(hit1, rank, 0.0), axis=0, keepdims=True)
        r2 = jnp.sum(jnp.where(hit2, rank, 0.0), axis=0, keepdims=True)
        carry_scr[...] = carry_scr[...] + jnp.sum(onehot, axis=1, keepdims=True)
        cnt_ref[...] = carry_scr[...]
        r8 = lax.broadcasted_iota(jnp.int32, (8, tm), 0)
        vals = [p1 / (p1 + p2), p2 / (p1 + p2), i1.astype(F32), i2.astype(F32), r1, r2]
        meta = jnp.zeros((8, tm), F32)
        for k, v in enumerate(vals):
            meta = jnp.where(r8 == k, v, meta)
        meta_ref[...] = meta


def _outproj(o_segs, y_segs, x_segs, mods, g, w_out, router_wt):
    tm = 512
    with_router = router_wt is not None
    o_specs, o_bounds = _seg_specs(o_segs, tm)
    y_specs, y_bounds = _seg_specs(y_segs, tm)
    x_specs, x_bounds = _seg_specs(x_segs, tm)
    kern = functools.partial(_outproj_kernel, with_router=with_router, o_bounds=o_bounds, y_bounds=y_bounds,
                             x_bounds=x_bounds)
    in_specs = o_specs + y_specs + x_specs + [
        pl.BlockSpec((None, 6, D_MODEL), lambda i: (_mod_group(i, tm), 0, 0)),
        pl.BlockSpec((1, D_MODEL), lambda i: (0, 0)),
        pl.BlockSpec((D_MODEL, D_MODEL), lambda i: (0, 0), pipeline_mode=pl.Buffered(1)),
    ]
    args = list(o_segs) + list(y_segs) + list(x_segs) + [mods, g, w_out]
    row_spec = pl.BlockSpec((tm, D_MODEL), lambda i: (i, 0))
    out_specs = [row_spec, row_spec]
    out_shape = [jax.ShapeDtypeStruct((T_ALL, D_MODEL), F32),
                 jax.ShapeDtypeStruct((T_ALL, D_MODEL), F32 if with_router else BF16)]
    scratch = []
    if with_router:
        in_specs.append(pl.BlockSpec((16, D_MODEL), lambda i: (0, 0)))
        args.append(router_wt)
        out_specs += [pl.BlockSpec((8, tm), lambda i: (0, i)), pl.BlockSpec((16, 128), lambda i: (0, 0))]
        out_shape += [jax.ShapeDtypeStruct((8, T_ALL), F32), jax.ShapeDtypeStruct((16, 128), F32)]
        scratch = [pltpu.VMEM((16, 128), F32)]
    return pl.pallas_call(
        kern,
        grid=(T_ALL // tm,),
        in_specs=in_specs,
        out_specs=out_specs,
        out_shape=out_shape,
        scratch_shapes=scratch,
        compiler_params=_cparams(("arbitrary",)),
        name="outproj_router" if with_router else "outproj",
    )(*args)


MOE_ROWS = 2 * T_ALL
MOE_TILE = 256
MOE_TILES = MOE_ROWS // MOE_TILE
MOE_VISITS = MOE_TILES + N_EXPERTS - 1
ROW_DMA_UNROLL = 16


def _row_copy(src, s, dst, d, sem):
    return pltpu.make_async_copy(src.at[pl.ds(s, 1)], dst.at[pl.ds(d, 1)], sem)


def _dispatch_kernel(p1_ref, p2_ref, h_ref, xs_ref, sem):
    tm = h_ref.shape[0]

    def issue(r, c):
        _row_copy(h_ref, r, xs_ref, p1_ref[0, 0, r], sem.at[0]).start()
        _row_copy(h_ref, r, xs_ref, p2_ref[0, 0, r], sem.at[1]).start()
        return c

    lax.fori_loop(0, tm, issue, 0, unroll=ROW_DMA_UNROLL)
    pltpu.make_async_copy(h_ref, xs_ref.at[pl.ds(0, tm)], sem.at[0]).wait()
    pltpu.make_async_copy(h_ref, xs_ref.at[pl.ds(0, tm)], sem.at[1]).wait()


def _dispatch(h, pos1, pos2):
    tm = 512
    nt = T_ALL // tm
    idx = lambda: pl.BlockSpec((1, 1, tm), lambda i: (i, 0, 0), memory_space=pltpu.SMEM)
    return pl.pallas_call(
        _dispatch_kernel,
        grid=(nt,),
        in_specs=[idx(), idx(), pl.BlockSpec((tm, D_MODEL), lambda i: (i, 0))],
        out_specs=pl.BlockSpec(memory_space=pl.ANY),
        out_shape=jax.ShapeDtypeStruct((MOE_ROWS, D_MODEL), F32),
        scratch_shapes=[pltpu.SemaphoreType.DMA((2,))],
        compiler_params=_cparams(("arbitrary",)),
        name="moe_dispatch",
    )(pos1.reshape(nt, 1, tm), pos2.reshape(nt, 1, tm), h)


def _experts_kernel(vt_ref, ve_ref, nv_ref, lo_ref, hi_ref, xs_ref, wg_ref, wu_ref, wd_ref, y_ref):
    v = pl.program_id(0)

    @pl.when(v < nv_ref[0])
    def _():
        e = ve_ref[v]
        x = xs_ref[...].astype(BF16)
        hid = _silu(_dot(x, wg_ref[...])) * _dot(x, wu_ref[...])
        y = _dot(hid.astype(BF16), wd_ref[...])
        row = vt_ref[v] * MOE_TILE + lax.broadcasted_iota(jnp.int32, (MOE_TILE, 1), 0)
        mine = (row >= lo_ref[e]) & (row < hi_ref[e])
        first_visit = (v == 0) | (vt_ref[jnp.maximum(v - 1, 0)] != vt_ref[v])

        @pl.when(first_visit)
        def _():
            y_ref[...] = jnp.where(mine, y, 0.0)

        @pl.when(jnp.logical_not(first_visit))
        def _():
            y_ref[...] = jnp.where(mine, y, y_ref[...])


def _experts(xs, wg, wu, wd, vt, ve, nv, lo, hi):
    grid_spec = pltpu.PrefetchScalarGridSpec(
        num_scalar_prefetch=5,
        grid=(MOE_VISITS,),
        in_specs=[
            pl.BlockSpec((MOE_TILE, D_MODEL), lambda v, vt, ve, nv, lo, hi: (vt[v], 0)),
            pl.BlockSpec((None, D_MODEL, F_EXPERT), lambda v, vt, ve, nv, lo, hi: (ve[v], 0, 0)),
            pl.BlockSpec((None, D_MODEL, F_EXPERT), lambda v, vt, ve, nv, lo, hi: (ve[v], 0, 0)),
            pl.BlockSpec((None, F_EXPERT, D_MODEL), lambda v, vt, ve, nv, lo, hi: (ve[v], 0, 0)),
        ],
        out_specs=pl.BlockSpec((MOE_TILE, D_MODEL), lambda v, vt, ve, nv, lo, hi: (vt[v], 0)),
    )
    return pl.pallas_call(
        _experts_kernel,
        grid_spec=grid_spec,
        out_shape=jax.ShapeDtypeStruct((MOE_ROWS, D_MODEL), F32),
        compiler_params=_cparams(("arbitrary",)),
        name="moe_experts",
    )(vt, ve, nv, lo, hi, xs, wg, wu, wd)


def _combine_kernel(p1c_ref, p2c_ref, p1n_ref, p2n_ref, y_hbm, x_ref, mod_ref, gate_ref, fg_ref,
                    outp_ref, outs_ref, ya_buf, yb_buf, sem):
    i = pl.program_id(0)
    n = pl.num_programs(0)
    tm = x_ref.shape[0]
    slot = i % 2

    def gather(pa_ref, pb_ref, s):
        def issue(r, c):
            _row_copy(y_hbm, pa_ref[0, 0, r], ya_buf.at[s], r, sem.at[0, s]).start()
            _row_copy(y_hbm, pb_ref[0, 0, r], yb_buf.at[s], r, sem.at[1, s]).start()
            return c

        lax.fori_loop(0, tm, issue, 0, unroll=ROW_DMA_UNROLL)

    @pl.when(i == 0)
    def _():
        gather(p1c_ref, p2c_ref, 0)

    @pl.when(i + 1 < n)
    def _():
        gather(p1n_ref, p2n_ref, 1 - slot)

    pltpu.make_async_copy(y_hbm.at[pl.ds(0, tm)], ya_buf.at[slot], sem.at[0, slot]).wait()
    pltpu.make_async_copy(y_hbm.at[pl.ds(0, tm)], yb_buf.at[slot], sem.at[1, slot]).wait()
    g = gate_ref[...]
    mix = g[:, 0:1] * ya_buf[slot] + g[:, 1:2] * yb_buf[slot]
    xo = _rms(x_ref[...] + mod_ref[5:6, :] * mix, fg_ref[...])

    @pl.when(i < T_PROMPT // tm)
    def _():
        outp_ref[...] = xo

    @pl.when(i >= T_PROMPT // tm)
    def _():
        outs_ref[...] = xo


def _combine(y, x, mods, gate_cols, pos1, pos2, final_g):
    tm = 256
    nt = T_ALL // tm
    ntp = T_PROMPT // tm
    cur = lambda: pl.BlockSpec((1, 1, tm), lambda i: (i, 0, 0), memory_space=pltpu.SMEM)
    nxt = lambda: pl.BlockSpec((1, 1, tm), lambda i: (jnp.minimum(i + 1, nt - 1), 0, 0), memory_space=pltpu.SMEM)
    p1, p2 = pos1.reshape(nt, 1, tm), pos2.reshape(nt, 1, tm)
    return pl.pallas_call(
        _combine_kernel,
        grid=(nt,),
        in_specs=[
            cur(), cur(), nxt(), nxt(),
            pl.BlockSpec(memory_space=pl.ANY),
            pl.BlockSpec((tm, D_MODEL), lambda i: (i, 0)),
            pl.BlockSpec((None, 6, D_MODEL), lambda i: (_mod_group(i, tm), 0, 0)),
            pl.BlockSpec((tm, 128), lambda i: (i, 0)),
            pl.BlockSpec((1, D_MODEL), lambda i: (0, 0)),
        ],
        out_specs=[
            pl.BlockSpec((tm, D_MODEL), lambda i: (jnp.minimum(i, ntp - 1), 0)),
            pl.BlockSpec((tm, D_MODEL), lambda i: (jnp.maximum(i - ntp, 0), 0)),
        ],
        out_shape=[
            jax.ShapeDtypeStruct((T_PROMPT, D_MODEL), F32), jax.ShapeDtypeStruct((T_SAMPLE, D_MODEL), F32),
        ],
        scratch_shapes=[
            pltpu.VMEM((2, tm, D_MODEL), F32), pltpu.VMEM((2, tm, D_MODEL), F32),
            pltpu.SemaphoreType.DMA((2, 2)),
        ],
        compiler_params=_cparams(("arbitrary",)),
        name="moe_combine",
    )(p1, p2, p1, p2, y, x, mods, gate_cols, final_g)


def _route_plan(meta, counts):
    i1, i2 = meta[2].astype(jnp.int32), meta[3].astype(jnp.int32)
    r1, r2 = meta[4].astype(jnp.int32), meta[5].astype(jnp.int32)
    cnt = counts[:N_EXPERTS, 0].astype(jnp.int32)
    hi = jnp.cumsum(cnt)
    lo = hi - cnt
    ex = jnp.arange(N_EXPERTS, dtype=jnp.int32)
    pos1 = jnp.sum(jnp.where(i1[:, None] == ex[None, :], lo[None, :], 0), axis=1) + r1
    pos2 = jnp.sum(jnp.where(i2[:, None] == ex[None, :], lo[None, :], 0), axis=1) + r2
    first_tile = lo // MOE_TILE
    n_vis_e = jnp.where(cnt > 0, (hi - 1) // MOE_TILE - first_tile + 1, 0)
    vis_hi = jnp.cumsum(n_vis_e)
    vis_lo = vis_hi - n_vis_e
    nv = vis_hi[-1]
    v = jnp.minimum(jnp.arange(MOE_VISITS, dtype=jnp.int32), nv - 1)
    ve = jnp.minimum(jnp.sum(v[:, None] >= vis_hi[None, :], axis=1), N_EXPERTS - 1).astype(jnp.int32)
    pick = lambda tab: jnp.sum(jnp.where(ve[:, None] == ex[None, :], tab[None, :], 0), axis=1)
    vt = (pick(first_tile) + v - pick(vis_lo)).astype(jnp.int32)
    return pos1, pos2, vt, ve, nv.reshape(1).astype(jnp.int32), lo.astype(jnp.int32), hi.astype(jnp.int32)


def _ffn_kernel(*refs, n_cast):
    h_ref, x_hbm, mod_ref, wg_ref, wu_ref, wd_ref = refs[:6]
    cast_in = refs[6:6 + n_cast]
    out_ref = refs[6 + n_cast]
    cast_out = refs[7 + n_cast:7 + 2 * n_cast]
    x_buf, sem = refs[7 + 2 * n_cast:]
    _side_cast(cast_in, cast_out)
    i, f = pl.program_id(0), pl.program_id(1)
    tm = h_ref.shape[0]
    x_copy = pltpu.make_async_copy(x_hbm.at[pl.ds(pl.multiple_of(i * tm, tm), tm)], x_buf, sem)

    @pl.when(f == 0)
    def _():
        x_copy.start()
        out_ref[...] = jnp.zeros_like(out_ref)

    h = h_ref[...]
    hid = _silu(_dot(h, wg_ref[...])) * _dot(h, wu_ref[...])
    out_ref[...] += _dot(hid.astype(BF16), wd_ref[...])

    @pl.when(f == pl.num_programs(1) - 1)
    def _():
        x_copy.wait()
        out_ref[...] = x_buf[...] + mod_ref[5:6, :] * out_ref[...]


FFN_CAST_STEPS = 64


def _ffn(h, x, mods, wg, wu, wd, to_bf16):
    tm, tf = 1024, 512
    nf = F_DENSE // tf
    assert (T_ALL // tm) * nf >= FFN_CAST_STEPS

    cast_in, cast_out, cast_shapes = _side_cast_specs(to_bf16, FFN_CAST_STEPS, lambda i, f: i * nf + f)
    outs = pl.pallas_call(
        functools.partial(_ffn_kernel, n_cast=len(to_bf16)),
        grid=(T_ALL // tm, nf),
        in_specs=[
            pl.BlockSpec((tm, D_MODEL), lambda i, f: (i, 0)),
            _ANY,
            pl.BlockSpec((None, 6, D_MODEL), lambda i, f: (_mod_group(i, tm), 0, 0)),
            pl.BlockSpec((D_MODEL, tf), lambda i, f: (0, f)),
            pl.BlockSpec((D_MODEL, tf), lambda i, f: (0, f)),
            pl.BlockSpec((tf, D_MODEL), lambda i, f: (f, 0)),
        ] + cast_in,
        out_specs=[pl.BlockSpec((tm, D_MODEL), lambda i, f: (i, 0))] + cast_out,
        out_shape=[jax.ShapeDtypeStruct((T_ALL, D_MODEL), F32)] + cast_shapes,
        scratch_shapes=[pltpu.VMEM((tm, D_MODEL), F32), pltpu.SemaphoreType.DMA(())],
        compiler_params=_cparams(("arbitrary", "arbitrary")),
        name="dense_ffn",
    )(h, x, mods, wg, wu, wd, *[part[0] for part in to_bf16])
    return outs[0], outs[1:]


def _rope_tables():
    n = DEC_SEQ
    rows = n // GRID_W
    t_row = jnp.repeat(jnp.arange(rows, dtype=F32), GRID_W)
    t_col = jnp.tile(jnp.arange(GRID_W, dtype=F32), rows)
    inv = 1.0 / (ROPE_THETA ** (jnp.arange(0, ROT_HALF, 2, dtype=F32) / ROT_HALF))
    ar, ac = t_row[:, None] * inv, t_col[:, None] * inv
    cos = jnp.concatenate([jnp.cos(ar), jnp.cos(ar), jnp.cos(ac), jnp.cos(ac)], axis=-1)
    sin_signed = jnp.concatenate([-jnp.sin(ar), jnp.sin(ar), -jnp.sin(ac), jnp.sin(ac)], axis=-1)
    return cos, sin_signed


def _pad_lanes(v, width=128):
    return jnp.pad(v, ((0, 0), (0, width - v.shape[-1])))


def kernel(x_prompt, x_sample, c, cache_k, cache_v, state_ssm_fwd, state_ssm_bwd, c_ctx, ada_w, ada_b, norm1_g, norm2_g, w_in, q_norm_g, k_norm_g, conv_w, conv_b, a_log_fwd, a_log_bwd, dt_bias_fwd, dt_bias_bwd, d_skip, ssd_norm_g, attn_out_g, w_out, ffn_w_gate, ffn_w_up, ffn_w_down, router_w, moe_w_gate, moe_w_up, moe_w_down, final_norm_g):
    assert DEPTH % 2 == 0
    cond = jnp.concatenate([c_ctx[None, :], c, jnp.zeros((N_COND - 1 - DEC_BATCH, D_MODEL), F32)], axis=0)
    mods_all = _ada_mods(cond, ada_w, ada_b).reshape(DEPTH, N_COND, 6, D_MODEL)
    cos, sin_signed = _rope_tables()

    w_in_t = jnp.swapaxes(w_in, 1, 2)
    x_segs = [x_prompt.reshape(T_PROMPT, D_MODEL), x_sample.reshape(T_SAMPLE, D_MODEL)]
    kv, states = None, None
    for l in range(DEPTH):
        mods = mods_all[l]
        w_out_rows = (w_out.reshape(DEPTH * D_MODEL, D_MODEL), l * D_MODEL, D_MODEL)
        proj, dt_raw, (w_o,) = _inproj(x_segs, mods, norm1_g[l][None, :], w_in_t, l, [w_out_rows])

        qg, kg, og = q_norm_g[l][None, :], k_norm_g[l][None, :], attn_out_g[l][None, :]
        o_p, k_all, v_all = _attention_prompt(proj, 0, qg, kg, og, l, kv)
        kv = (k_all, v_all)
        o_s = _attention_sample(proj, T_PROMPT, qg, kg, og, cache_k, cache_v, cos, sin_signed, l)

        p = {
            'conv_w': conv_w[l], 'conv_b': conv_b[l][None, :],
            'dt_bias': _pad_lanes(jnp.concatenate([dt_bias_fwd[l], dt_bias_bwd[l]])[None, :]),
            'a_log': _pad_lanes(jnp.concatenate([a_log_fwd[l], a_log_bwd[l]])[None, :]),
            'd_skip': jnp.repeat(d_skip[l], SSD_HEAD_DIM)[None, :],
            'ssd_norm_g': ssd_norm_g[l][None, :],
        }
        j = l // 2
        n_up, n_down = N_EXPERTS * D_MODEL, N_EXPERTS * F_EXPERT
        if l % 2 == 0:
            parts = [(ffn_w_gate.reshape(-1, F_DENSE), j * D_MODEL, D_MODEL),
                     (ffn_w_up.reshape(-1, F_DENSE), j * D_MODEL, D_MODEL),
                     (ffn_w_down.reshape(-1, D_MODEL), j * F_DENSE, F_DENSE)]
        else:
            parts = [(moe_w_down.reshape(-1, D_MODEL), j * n_down, n_down)]
        y_p, sf, sb, *mixer_w = _ssd(proj, dt_raw, p, SEQ, BATCH, 0, l, None, states, parts)
        states = (sf, sb)
        y_s, = _ssd(proj, dt_raw, p, DEC_SEQ, DEC_BATCH, T_PROMPT // DEC_SEQ, l, (state_ssm_fwd, state_ssm_bwd), None)

        g2 = norm2_g[l][None, :]
        if l % 2 == 0:
            x, h = _outproj([o_p, o_s], [y_p, y_s], x_segs, mods, g2, w_o, None)
            x, (eg, eu) = _ffn(h, x, mods, *mixer_w, [
                (moe_w_gate.reshape(-1, F_EXPERT), j * n_up, n_up),
                (moe_w_up.reshape(-1, F_EXPERT), j * n_up, n_up)])
            x_segs = [x]
        else:
            router_wt = jnp.pad(router_w[j].T, ((0, 16 - N_EXPERTS), (0, 0)))
            x, h, meta, counts = _outproj([o_p, o_s], [y_p, y_s], x_segs, mods, g2, w_o, router_wt)
            pos1, pos2, vt, ve, nv, lo, hi = _route_plan(meta, counts)
            xs = _dispatch(h, pos1, pos2)
            ys = _experts(xs, eg.reshape(N_EXPERTS, D_MODEL, F_EXPERT), eu.reshape(N_EXPERTS, D_MODEL, F_EXPERT),
                          mixer_w[0].reshape(N_EXPERTS, F_EXPERT, D_MODEL), vt, ve, nv, lo, hi)
            y_prompt, y_sample = _combine(ys, x, mods, _pad_lanes(meta[:2].T), pos1, pos2, final_norm_g[None, :])

    return (y_prompt.reshape(BATCH, SEQ, D_MODEL), y_sample.reshape(DEC_BATCH, DEC_SEQ, D_MODEL),
            kv[0], kv[1], states[0], states[1])
```

```python
import functools

import jax
import jax.numpy as jnp
from jax import lax
from jax.experimental import pallas as pl
from jax.experimental.pallas import tpu as pltpu

F32 = jnp.float32
BF16 = jnp.bfloat16

D_MODEL = 2048
BATCH = 16
SEQ = 256
DEPTH = 2
DEC_BATCH = 2
DEC_SEQ = 1024
PAST_LEN = 512
GRID_W = 64
D_ATTN = 1024
D_SSD = 1024
HEAD_DIM = 128
N_Q_HEADS = 8
N_KV_HEADS = 2
Q_PER_KV = 4
KV_DIM = 256
ROT_HALF = 64
ROPE_THETA = 10000.0
SSD_HEAD_DIM = 64
SSD_HEADS = 16
SSD_GROUPS = 2
HEADS_PER_GROUP = 8
D_STATE = 128
CONV_DIM = 1536
CHUNK = 128
N_MAIN = 4096
F_DENSE = 5632
N_EXPERTS = 8
F_EXPERT = 1024
EPS = 1e-6

T_PROMPT = BATCH * SEQ
T_SAMPLE = DEC_BATCH * DEC_SEQ
T_ALL = T_PROMPT + T_SAMPLE
N_COND = 16

VMEM_LIMIT = 58 * 1024 * 1024


def _cparams(sem):
    return pltpu.CompilerParams(dimension_semantics=sem, vmem_limit_bytes=VMEM_LIMIT)


def _mod_group(i, tm):
    return jnp.maximum(0, (i * tm - T_PROMPT + DEC_SEQ) // DEC_SEQ)


def _silu(x):
    return x * jax.nn.sigmoid(x)


def _rms(x, g):
    ms = jnp.mean(x * x, axis=-1, keepdims=True)
    return x * lax.rsqrt(ms + EPS) * g


def _dot(a, b):
    return jnp.dot(a, b, preferred_element_type=F32)


def _dot_nt(a, b):
    return lax.dot_general(a, b, (((1,), (1,)), ((), ())), preferred_element_type=F32)


def _split3(x):
    hi = x.astype(BF16)
    r1 = x - hi.astype(F32)
    mid = r1.astype(BF16)
    r2 = r1 - mid.astype(F32)
    return hi, mid, r2.astype(BF16)


def _ada_kernel(c_ref, w_ref, b_ref, o_ref):
    s = _silu(c_ref[...]).astype(BF16)
    o_ref[...] = _dot(s, w_ref[...].astype(BF16)) + b_ref[...]


def _ada_mods(cond, ada_w, ada_b):
    tn = 1024
    n_out = 6 * D_MODEL
    return pl.pallas_call(
        _ada_kernel,
        grid=(DEPTH, n_out // tn),
        in_specs=[
            pl.BlockSpec((N_COND, D_MODEL), lambda l, j: (0, 0)),
            pl.BlockSpec((None, D_MODEL, tn), lambda l, j: (l, 0, j)),
            pl.BlockSpec((None, 1, tn), lambda l, j: (l, 0, j)),
        ],
        out_specs=pl.BlockSpec((None, N_COND, tn), lambda l, j: (l, 0, j)),
        out_shape=jax.ShapeDtypeStruct((DEPTH, N_COND, n_out), F32),
        compiler_params=_cparams(("parallel", "parallel")),
        name="ada_mods",
    )(cond, ada_w, ada_b.reshape(DEPTH, 1, n_out))


def _side_cast_specs(parts, n_steps, linear_step):
    in_specs, out_specs, out_shapes = [], [], []
    for a, row0, nrows in parts:
        rows = nrows // n_steps
        blk0 = row0 // rows
        step = lambda *ids: jnp.minimum(linear_step(*ids), n_steps - 1)
        in_specs.append(pl.BlockSpec((rows, a.shape[1]), lambda *ids, blk0=blk0: (blk0 + step(*ids), 0)))
        out_specs.append(pl.BlockSpec((rows, a.shape[1]), lambda *ids: (step(*ids), 0)))
        out_shapes.append(jax.ShapeDtypeStruct((nrows, a.shape[1]), BF16))
    return in_specs, out_specs, out_shapes


def _side_cast(cast_in, cast_out):
    for src, dst in zip(cast_in, cast_out):
        dst[...] = src[...].astype(BF16)


INPROJ_NORM_ROWS = 1024
INPROJ_CAST_STEPS = 16


def _inproj_kernel(*refs, seg_rows, n_cast):
    n_seg = len(seg_rows)
    x_hbms = refs[:n_seg]
    mods_ref, g_ref, w_ref, wdt_ref = refs[n_seg:n_seg + 4]
    cast_in = refs[n_seg + 4:n_seg + 4 + n_cast]
    proj_ref, dt_ref = refs[n_seg + 4 + n_cast:n_seg + 6 + n_cast]
    cast_out = refs[n_seg + 6 + n_cast:n_seg + 6 + 2 * n_cast]
    x_buf, h_scr, sem = refs[n_seg + 6 + 2 * n_cast:]
    _side_cast(cast_in, cast_out)
    i, j = pl.program_id(0), pl.program_id(1)
    tm = x_buf.shape[0]

    def fetch(tile):
        start = 0
        for x_hbm, nrows in zip(x_hbms, seg_rows):
            b0, nb = start // tm, nrows // tm
            start += nrows

            @pl.when((tile >= b0) & (tile < b0 + nb))
            def _():
                r0 = pl.multiple_of((tile - b0) * tm, tm)
                pltpu.make_async_copy(x_hbm.at[pl.ds(r0, tm)], x_buf, sem).start()

    @pl.when(j == 0)
    def _():
        @pl.when(i == 0)
        def _():
            fetch(i)

        pltpu.make_async_copy(x_hbms[0].at[pl.ds(0, tm)], x_buf, sem).wait()
        for k in range(tm // INPROJ_NORM_ROWS):
            rows = slice(k * INPROJ_NORM_ROWS, (k + 1) * INPROJ_NORM_ROWS)
            mod = mods_ref[_mod_group(i * (tm // INPROJ_NORM_ROWS) + k, INPROJ_NORM_ROWS)]
            h = _rms(x_buf[rows, :], g_ref[...]) * (1.0 + mod[1:2, :]) + mod[0:1, :]
            h_scr[rows, :] = h.astype(BF16)

        @pl.when(i + 1 < pl.num_programs(0))
        def _():
            fetch(i + 1)

        n_dt = wdt_ref.shape[0]
        wdt = jnp.concatenate([wdt_ref[...], jnp.zeros((128 - n_dt, D_MODEL), F32)], axis=0)
        dt_ref[...] = _dot_nt(h_scr[...], wdt.astype(BF16))

    proj_ref[...] = _dot_nt(h_scr[...], w_ref[...].astype(BF16))


_ANY = pl.BlockSpec(memory_space=pl.ANY)


def _inproj(x_segs, mods, g, w_in_t, layer, to_bf16):
    tm, tn = 2048, 512
    n_dt = w_in_t.shape[1] - N_MAIN
    nj = N_MAIN // tn
    seg_rows = tuple(a.shape[0] for a in x_segs)
    assert all(r % tm == 0 for r in seg_rows) and sum(seg_rows) == T_ALL
    assert (T_ALL // tm) * nj >= INPROJ_CAST_STEPS
    cast_in, cast_out, cast_shapes = _side_cast_specs(to_bf16, INPROJ_CAST_STEPS, lambda i, j: i * nj + j)
    outs = pl.pallas_call(
        functools.partial(_inproj_kernel, seg_rows=seg_rows, n_cast=len(to_bf16)),
        grid=(T_ALL // tm, nj),
        in_specs=[_ANY] * len(x_segs) + [
            pl.BlockSpec(mods.shape, lambda i, j: (0, 0, 0)),
            pl.BlockSpec((1, D_MODEL), lambda i, j: (0, 0)),
            pl.BlockSpec((None, tn, D_MODEL), lambda i, j: (layer, j, 0)),
            pl.BlockSpec((None, n_dt, D_MODEL), lambda i, j: (layer, N_MAIN // n_dt, 0)),
        ] + cast_in,
        out_specs=[
            pl.BlockSpec((tm, tn), lambda i, j: (i, j)),
            pl.BlockSpec((tm, 128), lambda i, j: (i, 0)),
        ] + cast_out,
        out_shape=[
            jax.ShapeDtypeStruct((T_ALL, N_MAIN), F32),
            jax.ShapeDtypeStruct((T_ALL, 128), F32),
        ] + cast_shapes,
        scratch_shapes=[pltpu.VMEM((tm, D_MODEL), F32), pltpu.VMEM((tm, D_MODEL), BF16),
                        pltpu.SemaphoreType.DMA(())],
        compiler_params=_cparams(("arbitrary", "arbitrary")),
        name="inproj",
    )(*x_segs, mods, g, w_in_t, w_in_t, *[part[0] for part in to_bf16])
    return outs[0], outs[1], outs[2:]


def _rope(x, cos, sin_signed):
    lane = lax.broadcasted_iota(jnp.int32, x.shape, 1)
    first = (lane // (ROT_HALF // 2)) % 2 == 0
    swapped = jnp.where(first, pltpu.roll(x, HEAD_DIM - ROT_HALF // 2, 1), pltpu.roll(x, ROT_HALF // 2, 1))
    return x * cos + swapped * sin_signed


N_SCORE_BUFS = 3


def _attn_kernel(*refs, nk_new, has_ctx, n_prev):
    if has_ctx:
        (q_ref, kv_ref, qg_ref, kg_ref, og_ref, ck_ref, cv_ref, cq_ref, sq_ref, ckk_ref, skk_ref,
         o_ref, kb_scr, vb_scr, o_scr, s_scr) = refs
    elif n_prev:
        (q_ref, kv_ref, qg_ref, kg_ref, og_ref, pk_ref, pv_ref, o_ref, ko_ref, vo_ref,
         kb_scr, vb_scr, o_scr, s_scr) = refs
    else:
        (q_ref, kv_ref, qg_ref, kg_ref, og_ref, o_ref, ko_ref, vo_ref, kb_scr, vb_scr, o_scr, s_scr) = refs

    @pl.when(pl.program_id(1) == 0)
    def _():
        if not has_ctx and n_prev:
            ko_ref[0:n_prev] = pk_ref[...]
            vo_ref[0:n_prev] = pv_ref[...]
        for g in range(N_KV_HEADS):
            sl = slice(g * HEAD_DIM, (g + 1) * HEAD_DIM)
            kn = _rms(kv_ref[:, sl], kg_ref[...])
            v = kv_ref[:, KV_DIM + g * HEAD_DIM:KV_DIM + (g + 1) * HEAD_DIM]
            vsl = slice(2 * g * HEAD_DIM, (2 * g + 1) * HEAD_DIM)
            vb_scr[0:nk_new, vsl] = v.astype(BF16)
            vb_scr[:, (2 * g + 1) * HEAD_DIM:(2 * g + 2) * HEAD_DIM] = jnp.ones((vb_scr.shape[0], HEAD_DIM), BF16)
            if has_ctx:
                kb_scr[0:nk_new, sl] = _rope(kn, ckk_ref[...], skk_ref[...]).astype(BF16)
                kb_scr[nk_new:, sl] = ck_ref[:, g, :].astype(BF16)
                vb_scr[nk_new:, vsl] = cv_ref[:, g, :].astype(BF16)
            else:
                kb_scr[:, sl] = kn.astype(BF16)
                ko_ref[n_prev, :, g, :] = kn
                vo_ref[n_prev, :, g, :] = v

    scale_log2e = HEAD_DIM ** -0.5 * 1.4426950408889634

    def scores(h):
        g = h // Q_PER_KV
        qn = _rms(q_ref[:, h * HEAD_DIM:(h + 1) * HEAD_DIM], qg_ref[...])
        if has_ctx:
            qn = _rope(qn, cq_ref[...], sq_ref[...])
        qb = (qn * scale_log2e).astype(BF16)
        s_scr[h % N_SCORE_BUFS] = _dot_nt(qb, kb_scr[:, g * HEAD_DIM:(g + 1) * HEAD_DIM])

    for h in range(N_SCORE_BUFS - 1):
        scores(h)
    for h in range(N_Q_HEADS):
        if h + N_SCORE_BUFS - 1 < N_Q_HEADS:
            scores(h + N_SCORE_BUFS - 1)
        g = h // Q_PER_KV
        s = s_scr[h % N_SCORE_BUFS]
        e = jnp.exp2(s - jnp.max(s, axis=-1, keepdims=True))
        pv = _dot(e.astype(BF16), vb_scr[:, 2 * g * HEAD_DIM:(2 * g + 2) * HEAD_DIM])
        o_scr[:, h * HEAD_DIM:(h + 1) * HEAD_DIM] = pv[:, :HEAD_DIM] / pv[:, HEAD_DIM:]
    o_ref[...] = _rms(o_scr[...], og_ref[...]).astype(BF16)


def _attention_prompt(proj, row0, qg, kg, og, layer, prev_kv):
    n = SEQ
    blk0 = row0 // n
    kern = functools.partial(_attn_kernel, nk_new=n, has_ctx=False, n_prev=layer)
    vec = lambda w: pl.BlockSpec((1, w), lambda b, i: (0, 0))
    cache_spec = lambda k: pl.BlockSpec((None, k, n, N_KV_HEADS, HEAD_DIM), lambda b, i: (b, 0, 0, 0, 0))
    cache_shape = jax.ShapeDtypeStruct((BATCH, layer + 1, n, N_KV_HEADS, HEAD_DIM), F32)
    in_specs = [
        pl.BlockSpec((n, D_ATTN), lambda b, i: (blk0 + b, 0)),
        pl.BlockSpec((n, 2 * KV_DIM), lambda b, i: (blk0 + b, 2)),
        vec(HEAD_DIM), vec(HEAD_DIM), vec(D_ATTN),
    ]
    args = [proj, proj, qg, kg, og]
    if layer:
        in_specs += [cache_spec(layer), cache_spec(layer)]
        args += list(prev_kv)
    return pl.pallas_call(
        kern,
        grid=(BATCH, 1),
        in_specs=in_specs,
        out_specs=[pl.BlockSpec((n, D_ATTN), lambda b, i: (b, 0)), cache_spec(layer + 1), cache_spec(layer + 1)],
        out_shape=[jax.ShapeDtypeStruct((T_PROMPT, D_ATTN), BF16), cache_shape, cache_shape],
        scratch_shapes=[
            pltpu.VMEM((n, KV_DIM), BF16), pltpu.VMEM((n, 2 * KV_DIM), BF16), pltpu.VMEM((n, D_ATTN), F32),
            pltpu.VMEM((N_SCORE_BUFS, n, n), F32),
        ],
        compiler_params=_cparams(("parallel", "arbitrary")),
        name="attn_prompt",
    )(*args)


def _attention_sample(proj, row0, qg, kg, og, ck, cv, cos, sin_signed, layer):
    n, tq = DEC_SEQ, 512
    nq = n // tq
    nk = n + PAST_LEN
    kern = functools.partial(_attn_kernel, nk_new=n, has_ctx=True, n_prev=0)
    vec = lambda w: pl.BlockSpec((1, w), lambda b, i: (0, 0))
    q_blk0 = row0 // tq
    kv_blk0 = row0 // n
    return pl.pallas_call(
        kern,
        grid=(DEC_BATCH, nq),
        in_specs=[
            pl.BlockSpec((tq, D_ATTN), lambda b, i: (q_blk0 + b * nq + i, 0)),
            pl.BlockSpec((n, 2 * KV_DIM), lambda b, i: (kv_blk0 + b, 2)),
            vec(HEAD_DIM), vec(HEAD_DIM), vec(D_ATTN),
            pl.BlockSpec((None, None, PAST_LEN, N_KV_HEADS, HEAD_DIM), lambda b, i: (b, layer, 0, 0, 0)),
            pl.BlockSpec((None, None, PAST_LEN, N_KV_HEADS, HEAD_DIM), lambda b, i: (b, layer, 0, 0, 0)),
            pl.BlockSpec((tq, HEAD_DIM), lambda b, i: (i, 0)),
            pl.BlockSpec((tq, HEAD_DIM), lambda b, i: (i, 0)),
            pl.BlockSpec((n, HEAD_DIM), lambda b, i: (0, 0)),
            pl.BlockSpec((n, HEAD_DIM), lambda b, i: (0, 0)),
        ],
        out_specs=pl.BlockSpec((tq, D_ATTN), lambda b, i: (b * nq + i, 0)),
        out_shape=jax.ShapeDtypeStruct((T_SAMPLE, D_ATTN), BF16),
        scratch_shapes=[
            pltpu.VMEM((nk, KV_DIM), BF16), pltpu.VMEM((nk, 2 * KV_DIM), BF16), pltpu.VMEM((tq, D_ATTN), F32),
            pltpu.VMEM((N_SCORE_BUFS, tq, nk), F32),
        ],
        compiler_params=_cparams(("parallel", "arbitrary")),
        name="attn_sample",
    )(proj, proj, qg, kg, og, ck, cv, cos, sin_signed, cos, sin_signed)


def _conv_silu(x, w, b):
    n = x.shape[0]
    row = lax.broadcasted_iota(jnp.int32, (n, 1), 0)
    prev = jnp.where(row == 0, 0.0, pltpu.roll(x, 1, 0))
    nxt = jnp.where(row == n - 1, 0.0, pltpu.roll(x, n - 1, 0))
    return _silu(prev * w[0:1, :] + x * w[1:2, :] + nxt * w[2:3, :] + b)


def _softplus(x):
    return jnp.maximum(x, 0.0) + jnp.log1p(jnp.exp(-jnp.abs(x)))


def _ssd_kernel(*refs, n, has_init, n_prev, n_cast):
    refs = list(refs)
    (za_ref, zb_ref, xa_ref, xb_ref, bc_ref, dt_ref, cw_ref, cb_ref, dtb_ref, alog_ref, dsk_ref,
     ng_ref) = refs[:12]
    del refs[:12]
    if has_init:
        sf0_ref, sb0_ref = refs.pop(0), refs.pop(0)
    elif n_prev:
        psf_ref, psb_ref = refs.pop(0), refs.pop(0)
    cast_in = [refs.pop(0) for _ in range(n_cast)]
    y_ref = refs.pop(0)
    if not has_init:
        sf_ref, sb_ref = refs.pop(0), refs.pop(0)
    cast_out = [refs.pop(0) for _ in range(n_cast)]
    xc_scr, bcc_scr, dts_scr, xt_scr, yt_scr, s_scr = refs
    _side_cast(cast_in, cast_out)
    nc = n // CHUNK
    gw = HEADS_PER_GROUP * SSD_HEAD_DIM

    xc_scr[:, 0:gw] = _conv_silu(xa_ref[...], cw_ref[:, 0:gw], cb_ref[:, 0:gw])
    xc_scr[:, gw:] = _conv_silu(xb_ref[...], cw_ref[:, gw:2 * gw], cb_ref[:, gw:2 * gw])
    bcc_scr[...] = _conv_silu(bc_ref[...], cw_ref[:, 2 * gw:], cb_ref[:, 2 * gw:])
    dts_scr[...] = _softplus(dt_ref[...] + dtb_ref[...])
    for g in range(SSD_GROUPS):
        hs = slice(g * HEADS_PER_GROUP, (g + 1) * HEADS_PER_GROUP)
        if has_init:
            s_scr[0, g] = sf0_ref[hs].reshape(gw, D_STATE)
            s_scr[1, g] = sb0_ref[hs].reshape(gw, D_STATE)
        else:
            s_scr[0, g] = jnp.zeros((gw, D_STATE), F32)
            s_scr[1, g] = jnp.zeros((gw, D_STATE), F32)

    def to_channel_major(c, carry):
        rows = pl.ds(pl.multiple_of(c * CHUNK, CHUNK), CHUNK)
        xt_scr[c] = xc_scr[rows, :].T
        yt_scr[c] = jnp.zeros((D_SSD, CHUNK), F32)
        return carry

    lax.fori_loop(0, nc, to_channel_major, 0, unroll=2)

    a_row = -jnp.exp(alog_ref[...])
    ri = lax.broadcasted_iota(jnp.int32, (CHUNK, CHUNK), 0)
    ci = lax.broadcasted_iota(jnp.int32, (CHUNK, CHUNK), 1)
    lower, upper = ci <= ri, ci >= ri
    n_dirs_heads = 2 * SSD_HEADS

    def chunk_sums(dirn, c):
        tri = (lower if dirn == 0 else upper).astype(BF16)
        valid_st = upper if dirn == 0 else lower
        tri_t = valid_st.astype(BF16)
        rows = pl.ds(pl.multiple_of(c * CHUNK, CHUNK), CHUNK)
        dt = dts_scr[rows, :]
        d = dt * a_row
        d1, d2, d3 = _split3(d)
        cs = _dot(tri, d1) + _dot(tri, d2) + _dot(tri, d3)
        dt_t = dt.T[0:n_dirs_heads, :]
        e1, e2, e3 = _split3(d.T[0:n_dirs_heads, :])
        cs_t = _dot(e1, tri_t) + _dot(e2, tri_t) + _dot(e3, tri_t)
        total = cs_t[:, CHUNK - 1:CHUNK] if dirn == 0 else cs_t[:, 0:1]
        return dict(
            dirn=dirn, c=c, rows=rows, valid_st=valid_st, cs=cs, cs_t=cs_t, dt_t=dt_t,
            e_in_t=jnp.exp(cs_t),
            to_end_t=jnp.exp(total - cs_t) * dt_t,
            dec_t=jnp.broadcast_to(jnp.exp(total), (n_dirs_heads, D_STATE)))

    def group_inputs(q, g):
        bm = bcc_scr[q['rows'], g * D_STATE:(g + 1) * D_STATE].astype(BF16)
        cm = bcc_scr[q['rows'], (SSD_GROUPS + g) * D_STATE:(SSD_GROUPS + g + 1) * D_STATE]
        return dict(bm=bm, g_st=_dot_nt(bm, cm.astype(BF16)),
                    c_nt=cm.T,
                    st=s_scr[q['dirn'], g],
                    xs_parts=[], dec_parts=[])

    def head(q, gi, g, hh):
        h = g * HEADS_PER_GROUP + hh
        r = q['dirn'] * SSD_HEADS + h
        c = q['c']
        ch = slice(h * SSD_HEAD_DIM, (h + 1) * SSD_HEAD_DIM)
        x_t = xt_scr[c, ch, :]
        diff = q['cs_t'][r:r + 1, :] - q['cs'][:, r:r + 1]
        a_st = (gi['g_st'] * jnp.exp(jnp.where(q['valid_st'], diff, -jnp.inf))).astype(BF16)
        c_e = (gi['c_nt'] * q['e_in_t'][r:r + 1, :]).astype(BF16)
        x_dt = (x_t * q['dt_t'][r:r + 1, :]).astype(BF16)
        s_h = gi['st'][hh * SSD_HEAD_DIM:(hh + 1) * SSD_HEAD_DIM, :].astype(BF16)
        y_h = _dot(jnp.concatenate([x_dt, s_h], axis=1), jnp.concatenate([a_st, c_e], axis=0))
        yt_scr[c, ch, :] = yt_scr[c, ch, :] + y_h
        gi['xs_parts'].append((x_t * q['to_end_t'][r:r + 1, :]).astype(BF16))
        gi['dec_parts'].append(jnp.broadcast_to(q['dec_t'][r:r + 1, :], (SSD_HEAD_DIM, D_STATE)))

    def group_update(q, gi, g):
        ds = _dot(jnp.concatenate(gi['xs_parts'], axis=0), gi['bm'])
        s_scr[q['dirn'], g] = gi['st'] * jnp.concatenate(gi['dec_parts'], axis=0) + ds

    def body(i, carry):
        qs = [chunk_sums(0, i), chunk_sums(1, nc - 1 - i)]
        chains = [(q, group_inputs(q, g), g) for g in range(SSD_GROUPS) for q in qs]
        for hh in range(HEADS_PER_GROUP):
            for q, gi, g in chains:
                head(q, gi, g, hh)
        for q, gi, g in chains:
            group_update(q, gi, g)
        return carry

    lax.fori_loop(0, nc, body, 0, unroll=2)

    def finish(c, carry):
        rows = pl.ds(pl.multiple_of(c * CHUNK, CHUNK), CHUNK)
        y = yt_scr[c].T + xc_scr[rows, :] * dsk_ref[...]
        ya = y[:, 0:gw] * _silu(za_ref[rows, :])
        yb = y[:, gw:] * _silu(zb_ref[rows, :])
        ms = (jnp.sum(ya * ya, axis=-1, keepdims=True) + jnp.sum(yb * yb, axis=-1, keepdims=True)) / D_SSD
        inv = lax.rsqrt(ms + EPS)
        y_ref[rows, 0:gw] = (ya * inv * ng_ref[:, 0:gw]).astype(BF16)
        y_ref[rows, gw:] = (yb * inv * ng_ref[:, gw:]).astype(BF16)
        return carry

    lax.fori_loop(0, nc, finish, 0, unroll=2)

    if not has_init:
        if n_prev:
            sf_ref[0:n_prev] = psf_ref[...]
            sb_ref[0:n_prev] = psb_ref[...]
        for g in range(SSD_GROUPS):
            hs = slice(g * HEADS_PER_GROUP, (g + 1) * HEADS_PER_GROUP)
            sf_ref[n_prev, hs] = s_scr[0, g].reshape(HEADS_PER_GROUP, SSD_HEAD_DIM, D_STATE)
            sb_ref[n_prev, hs] = s_scr[1, g].reshape(HEADS_PER_GROUP, SSD_HEAD_DIM, D_STATE)


def _ssd(proj, dt_raw, p, n, nb, row_blk0, layer, init, prev_states, to_bf16=()):
    has_init = init is not None
    n_prev = 0 if has_init else layer
    kern = functools.partial(_ssd_kernel, n=n, has_init=has_init, n_prev=n_prev, n_cast=len(to_bf16))
    cast_in, cast_out, cast_shapes = _side_cast_specs(to_bf16, nb, lambda b: b)
    col = lambda cb: pl.BlockSpec((n, 512), lambda b: (row_blk0 + b, cb))
    vec = lambda r, w: pl.BlockSpec((r, w), lambda b: (0, 0))
    layers_spec = lambda k: pl.BlockSpec((None, k, SSD_HEADS, SSD_HEAD_DIM, D_STATE), lambda b: (b, 0, 0, 0, 0))
    in_specs = [
        col(3), col(4), col(5), col(6), col(7),
        pl.BlockSpec((n, 128), lambda b: (row_blk0 + b, 0)),
        vec(3, CONV_DIM), vec(1, CONV_DIM), vec(1, 128), vec(1, 128), vec(1, D_SSD), vec(1, D_SSD),
    ]
    args = [proj, proj, proj, proj, proj, dt_raw, p['conv_w'], p['conv_b'], p['dt_bias'], p['a_log'],
            p['d_skip'], p['ssd_norm_g']]
    y_spec = pl.BlockSpec((n, D_SSD), lambda b: (b, 0))
    y_shape = jax.ShapeDtypeStruct((nb * n, D_SSD), BF16)
    if has_init:
        init_spec = pl.BlockSpec((None, None, SSD_HEADS, SSD_HEAD_DIM, D_STATE), lambda b: (b, layer, 0, 0, 0))
        in_specs += [init_spec, init_spec]
        args += list(init)
        out_specs, out_shape = [y_spec], [y_shape]
    else:
        if n_prev:
            in_specs += [layers_spec(n_prev), layers_spec(n_prev)]
            args += list(prev_states)
        st_shape = jax.ShapeDtypeStruct((nb, layer + 1, SSD_HEADS, SSD_HEAD_DIM, D_STATE), F32)
        out_specs = [y_spec, layers_spec(layer + 1), layers_spec(layer + 1)]
        out_shape = [y_shape, st_shape, st_shape]
    return pl.pallas_call(
        kern,
        grid=(nb,),
        in_specs=in_specs + cast_in,
        out_specs=out_specs + cast_out,
        out_shape=out_shape + cast_shapes,
        scratch_shapes=[
            pltpu.VMEM((n, D_SSD), F32), pltpu.VMEM((n, 512), F32), pltpu.VMEM((n, 128), F32),
            pltpu.VMEM((n // CHUNK, D_SSD, CHUNK), F32), pltpu.VMEM((n // CHUNK, D_SSD, CHUNK), F32),
            pltpu.VMEM((2, SSD_GROUPS, 512, D_STATE), F32),
        ],
        compiler_params=_cparams(("parallel",)),
        name="ssd_sample" if has_init else "ssd_prompt",
    )(*args, *[part[0] for part in to_bf16])


def _seg_specs(segs, tm):
    specs, bounds, start = [], [], 0
    for a in segs:
        b0, nblk = start // tm, a.shape[0] // tm
        specs.append(pl.BlockSpec((tm, a.shape[1]), lambda i, b0=b0, nblk=nblk: (jnp.clip(i - b0, 0, nblk - 1), 0)))
        bounds.append(b0)
        start += a.shape[0]
    return specs, tuple(bounds)


def _seg_pick(refs, bounds):
    i = pl.program_id(0)
    v = refs[0][...]
    for ref, b0 in zip(refs[1:], bounds[1:]):
        v = jnp.where(i >= b0, ref[...], v)
    return v


def _outproj_kernel(*refs, with_router, o_bounds, y_bounds, x_bounds):
    refs = list(refs)
    o_refs = [refs.pop(0) for _ in o_bounds]
    y_refs = [refs.pop(0) for _ in y_bounds]
    x_refs = [refs.pop(0) for _ in x_bounds]
    if with_router:
        mod_ref, g_ref, w_ref, rw_ref, xo_ref, h_ref, meta_ref, cnt_ref, carry_scr = refs
    else:
        mod_ref, g_ref, w_ref, xo_ref, h_ref = refs
    tm = xo_ref.shape[0]
    a = jnp.concatenate([_seg_pick(o_refs, o_bounds), _seg_pick(y_refs, y_bounds)], axis=1)
    xn = _seg_pick(x_refs, x_bounds) + mod_ref[2:3, :] * _dot(a, w_ref[...])
    xo_ref[...] = xn
    h = _rms(xn, g_ref[...]) * (1.0 + mod_ref[4:5, :]) + mod_ref[3:4, :]
    h_ref[...] = h.astype(h_ref.dtype)
    if with_router:

        @pl.when(pl.program_id(0) == 0)
        def _():
            carry_scr[...] = jnp.zeros_like(carry_scr)

        h1, h2, _ = _split3(h)
        w1, w2, _ = _split3(rw_ref[...])
        logits = _dot_nt(w1, h1) + _dot_nt(w2, h1) + _dot_nt(w1, h2)
        row = lax.broadcasted_iota(jnp.int32, logits.shape, 0)
        logits = jnp.where(row < N_EXPERTS, logits, -jnp.inf)
        e = jnp.exp(logits - jnp.max(logits, axis=0, keepdims=True))
        probs = e / jnp.sum(e, axis=0, keepdims=True)
        p1 = jnp.max(probs, axis=0, keepdims=True)
        i1 = jnp.min(jnp.where(probs == p1, row, 16), axis=0, keepdims=True)
        rest = jnp.where(row == i1, -1.0, probs)
        p2 = jnp.max(rest, axis=0, keepdims=True)
        i2 = jnp.min(jnp.where(rest == p2, row, 16), axis=0, keepdims=True)
        hit1, hit2 = row == i1, row == i2
        onehot = jnp.where(hit1 | hit2, 1.0, 0.0)
        ti = lax.broadcasted_iota(jnp.int32, (tm, tm), 0)
        tj = lax.broadcasted_iota(jnp.int32, (tm, tm), 1)
        before = jnp.where(ti < tj, 1.0, 0.0).astype(BF16)
        rank = carry_scr[:, 0:1] + _dot(onehot.astype(BF16), before)
        r1 = jnp.sum(jnp.where(hit1, rank, 0.0), axis=0, keepdims=True)
        r2 = jnp.sum(jnp.where(hit2, rank, 0.0), axis=0, keepdims=True)
        carry_scr[...] = carry_scr[...] + jnp.sum(onehot, axis=1, keepdims=True)
        cnt_ref[...] = carry_scr[...]
        r8 = lax.broadcasted_iota(jnp.int32, (8, tm), 0)
        vals = [p1 / (p1 + p2), p2 / (p1 + p2), i1.astype(F32), i2.astype(F32), r1, r2]
        meta = jnp.zeros((8, tm), F32)
        for k, v in enumerate(vals):
            meta = jnp.where(r8 == k, v, meta)
        meta_ref[...] = meta


def _outproj(o_segs, y_segs, x_segs, mods, g, w_out, router_wt):
    tm = 512
    with_router = router_wt is not None
    o_specs, o_bounds = _seg_specs(o_segs, tm)
    y_specs, y_bounds = _seg_specs(y_segs, tm)
    x_specs, x_bounds = _seg_specs(x_segs, tm)
    kern = functools.partial(_outproj_kernel, with_router=with_router, o_bounds=o_bounds, y_bounds=y_bounds,
                             x_bounds=x_bounds)
    in_specs = o_specs + y_specs + x_specs + [
        pl.BlockSpec((None, 6, D_MODEL), lambda i: (_mod_group(i, tm), 0, 0)),
        pl.BlockSpec((1, D_MODEL), lambda i: (0, 0)),
        pl.BlockSpec((D_MODEL, D_MODEL), lambda i: (0, 0), pipeline_mode=pl.Buffered(1)),
    ]
    args = list(o_segs) + list(y_segs) + list(x_segs) + [mods, g, w_out]
    row_spec = pl.BlockSpec((tm, D_MODEL), lambda i: (i, 0))
    out_specs = [row_spec, row_spec]
    out_shape = [jax.ShapeDtypeStruct((T_ALL, D_MODEL), F32),
                 jax.ShapeDtypeStruct((T_ALL, D_MODEL), F32 if with_router else BF16)]
    scratch = []
    if with_router:
        in_specs.append(pl.BlockSpec((16, D_MODEL), lambda i: (0, 0)))
        args.append(router_wt)
        out_specs += [pl.BlockSpec((8, tm), lambda i: (0, i)), pl.BlockSpec((16, 128), lambda i: (0, 0))]
        out_shape += [jax.ShapeDtypeStruct((8, T_ALL), F32), jax.ShapeDtypeStruct((16, 128), F32)]
        scratch = [pltpu.VMEM((16, 128), F32)]
    return pl.pallas_call(
        kern,
        grid=(T_ALL // tm,),
        in_specs=in_specs,
        out_specs=out_specs,
        out_shape=out_shape,
        scratch_shapes=scratch,
        compiler_params=_cparams(("arbitrary",)),
        name="outproj_router" if with_router else "outproj",
    )(*args)


MOE_ROWS = 2 * T_ALL
MOE_TILE = 256
MOE_TILES = MOE_ROWS // MOE_TILE
MOE_VISITS = MOE_TILES + N_EXPERTS - 1
ROW_DMA_UNROLL = 16


def _row_copy(src, s, dst, d, sem):
    return pltpu.make_async_copy(src.at[pl.ds(s, 1)], dst.at[pl.ds(d, 1)], sem)


def _dispatch_kernel(p1_ref, p2_ref, h_ref, xs_ref, sem):
    tm = h_ref.shape[0]

    def issue(r, c):
        _row_copy(h_ref, r, xs_ref, p1_ref[0, 0, r], sem.at[0]).start()
        _row_copy(h_ref, r, xs_ref, p2_ref[0, 0, r], sem.at[1]).start()
        return c

    lax.fori_loop(0, tm, issue, 0, unroll=ROW_DMA_UNROLL)
    pltpu.make_async_copy(h_ref, xs_ref.at[pl.ds(0, tm)], sem.at[0]).wait()
    pltpu.make_async_copy(h_ref, xs_ref.at[pl.ds(0, tm)], sem.at[1]).wait()


def _dispatch(h, pos1, pos2):
    tm = 512
    nt = T_ALL // tm
    idx = lambda: pl.BlockSpec((1, 1, tm), lambda i: (i, 0, 0), memory_space=pltpu.SMEM)
    return pl.pallas_call(
        _dispatch_kernel,
        grid=(nt,),
        in_specs=[idx(), idx(), pl.BlockSpec((tm, D_MODEL), lambda i: (i, 0))],
        out_specs=pl.BlockSpec(memory_space=pl.ANY),
        out_shape=jax.ShapeDtypeStruct((MOE_ROWS, D_MODEL), F32),
        scratch_shapes=[pltpu.SemaphoreType.DMA((2,))],
        compiler_params=_cparams(("arbitrary",)),
        name="moe_dispatch",
    )(pos1.reshape(nt, 1, tm), pos2.reshape(nt, 1, tm), h)


def _experts_kernel(vt_ref, ve_ref, nv_ref, lo_ref, hi_ref, xs_ref, wg_ref, wu_ref, wd_ref, y_ref):
    v = pl.program_id(0)

    @pl.when(v < nv_ref[0])
    def _():
        e = ve_ref[v]
        x = xs_ref[...].astype(BF16)
        hid = _silu(_dot(x, wg_ref[...])) * _dot(x, wu_ref[...])
        y = _dot(hid.astype(BF16), wd_ref[...])
        row = vt_ref[v] * MOE_TILE + lax.broadcasted_iota(jnp.int32, (MOE_TILE, 1), 0)
        mine = (row >= lo_ref[e]) & (row < hi_ref[e])
        first_visit = (v == 0) | (vt_ref[jnp.maximum(v - 1, 0)] != vt_ref[v])

        @pl.when(first_visit)
        def _():
            y_ref[...] = jnp.where(mine, y, 0.0)

        @pl.when(jnp.logical_not(first_visit))
        def _():
            y_ref[...] = jnp.where(mine, y, y_ref[...])


def _experts(xs, wg, wu, wd, vt, ve, nv, lo, hi):
    grid_spec = pltpu.PrefetchScalarGridSpec(
        num_scalar_prefetch=5,
        grid=(MOE_VISITS,),
        in_specs=[
            pl.BlockSpec((MOE_TILE, D_MODEL), lambda v, vt, ve, nv, lo, hi: (vt[v], 0)),
            pl.BlockSpec((None, D_MODEL, F_EXPERT), lambda v, vt, ve, nv, lo, hi: (ve[v], 0, 0)),
            pl.BlockSpec((None, D_MODEL, F_EXPERT), lambda v, vt, ve, nv, lo, hi: (ve[v], 0, 0)),
            pl.BlockSpec((None, F_EXPERT, D_MODEL), lambda v, vt, ve, nv, lo, hi: (ve[v], 0, 0)),
        ],
        out_specs=pl.BlockSpec((MOE_TILE, D_MODEL), lambda v, vt, ve, nv, lo, hi: (vt[v], 0)),
    )
    return pl.pallas_call(
        _experts_kernel,
        grid_spec=grid_spec,
        out_shape=jax.ShapeDtypeStruct((MOE_ROWS, D_MODEL), F32),
        compiler_params=_cparams(("arbitrary",)),
        name="moe_experts",
    )(vt, ve, nv, lo, hi, xs, wg, wu, wd)


def _combine_kernel(p1c_ref, p2c_ref, p1n_ref, p2n_ref, y_hbm, x_ref, mod_ref, gate_ref, fg_ref,
                    outp_ref, outs_ref, ya_buf, yb_buf, sem):
    i = pl.program_id(0)
    n = pl.num_programs(0)
    tm = x_ref.shape[0]
    slot = i % 2

    def gather(pa_ref, pb_ref, s):
        def issue(r, c):
            _row_copy(y_hbm, pa_ref[0, 0, r], ya_buf.at[s], r, sem.at[0, s]).start()
            _row_copy(y_hbm, pb_ref[0, 0, r], yb_buf.at[s], r, sem.at[1, s]).start()
            return c

        lax.fori_loop(0, tm, issue, 0, unroll=ROW_DMA_UNROLL)

    @pl.when(i == 0)
    def _():
        gather(p1c_ref, p2c_ref, 0)

    @pl.when(i + 1 < n)
    def _():
        gather(p1n_ref, p2n_ref, 1 - slot)

    pltpu.make_async_copy(y_hbm.at[pl.ds(0, tm)], ya_buf.at[slot], sem.at[0, slot]).wait()
    pltpu.make_async_copy(y_hbm.at[pl.ds(0, tm)], yb_buf.at[slot], sem.at[1, slot]).wait()
    g = gate_ref[...]
    mix = g[:, 0:1] * ya_buf[slot] + g[:, 1:2] * yb_buf[slot]
    xo = _rms(x_ref[...] + mod_ref[5:6, :] * mix, fg_ref[...])

    @pl.when(i < T_PROMPT // tm)
    def _():
        outp_ref[...] = xo

    @pl.when(i >= T_PROMPT // tm)
    def _():
        outs_ref[...] = xo


def _combine(y, x, mods, gate_cols, pos1, pos2, final_g):
    tm = 256
    nt = T_ALL // tm
    ntp = T_PROMPT // tm
    cur = lambda: pl.BlockSpec((1, 1, tm), lambda i: (i, 0, 0), memory_space=pltpu.SMEM)
    nxt = lambda: pl.BlockSpec((1, 1, tm), lambda i: (jnp.minimum(i + 1, nt - 1), 0, 0), memory_space=pltpu.SMEM)
    p1, p2 = pos1.reshape(nt, 1, tm), pos2.reshape(nt, 1, tm)
    return pl.pallas_call(
        _combine_kernel,
        grid=(nt,),
        in_specs=[
            cur(), cur(), nxt(), nxt(),
            pl.BlockSpec(memory_space=pl.ANY),
            pl.BlockSpec((tm, D_MODEL), lambda i: (i, 0)),
            pl.BlockSpec((None, 6, D_MODEL), lambda i: (_mod_group(i, tm), 0, 0)),
            pl.BlockSpec((tm, 128), lambda i: (i, 0)),
            pl.BlockSpec((1, D_MODEL), lambda i: (0, 0)),
        ],
        out_specs=[
            pl.BlockSpec((tm, D_MODEL), lambda i: (jnp.minimum(i, ntp - 1), 0)),
            pl.BlockSpec((tm, D_MODEL), lambda i: (jnp.maximum(i - ntp, 0), 0)),
        ],
        out_shape=[
            jax.ShapeDtypeStruct((T_PROMPT, D_MODEL), F32), jax.ShapeDtypeStruct((T_SAMPLE, D_MODEL), F32),
        ],
        scratch_shapes=[
            pltpu.VMEM((2, tm, D_MODEL), F32), pltpu.VMEM((2, tm, D_MODEL), F32),
            pltpu.SemaphoreType.DMA((2, 2)),
        ],
        compiler_params=_cparams(("arbitrary",)),
        name="moe_combine",
    )(p1, p2, p1, p2, y, x, mods, gate_cols, final_g)


def _route_plan(meta, counts):
    i1, i2 = meta[2].astype(jnp.int32), meta[3].astype(jnp.int32)
    r1, r2 = meta[4].astype(jnp.int32), meta[5].astype(jnp.int32)
    cnt = counts[:N_EXPERTS, 0].astype(jnp.int32)
    hi = jnp.cumsum(cnt)
    lo = hi - cnt
    ex = jnp.arange(N_EXPERTS, dtype=jnp.int32)
    pos1 = jnp.sum(jnp.where(i1[:, None] == ex[None, :], lo[None, :], 0), axis=1) + r1
    pos2 = jnp.sum(jnp.where(i2[:, None] == ex[None, :], lo[None, :], 0), axis=1) + r2
    first_tile = lo // MOE_TILE
    n_vis_e = jnp.where(cnt > 0, (hi - 1) // MOE_TILE - first_tile + 1, 0)
    vis_hi = jnp.cumsum(n_vis_e)
    vis_lo = vis_hi - n_vis_e
    nv = vis_hi[-1]
    v = jnp.minimum(jnp.arange(MOE_VISITS, dtype=jnp.int32), nv - 1)
    ve = jnp.minimum(jnp.sum(v[:, None] >= vis_hi[None, :], axis=1), N_EXPERTS - 1).astype(jnp.int32)
    pick = lambda tab: jnp.sum(jnp.where(ve[:, None] == ex[None, :], tab[None, :], 0), axis=1)
    vt = (pick(first_tile) + v - pick(vis_lo)).astype(jnp.int32)
    return pos1, pos2, vt, ve, nv.reshape(1).astype(jnp.int32), lo.astype(jnp.int32), hi.astype(jnp.int32)


def _ffn_kernel(*refs, n_cast):
    h_ref, x_hbm, mod_ref, wg_ref, wu_ref, wd_ref = refs[:6]
    cast_in = refs[6:6 + n_cast]
    out_ref = refs[6 + n_cast]
    cast_out = refs[7 + n_cast:7 + 2 * n_cast]
    x_buf, sem = refs[7 + 2 * n_cast:]
    _side_cast(cast_in, cast_out)
    i, f = pl.program_id(0), pl.program_id(1)
    tm = h_ref.shape[0]
    x_copy = pltpu.make_async_copy(x_hbm.at[pl.ds(pl.multiple_of(i * tm, tm), tm)], x_buf, sem)

    @pl.when(f == 0)
    def _():
        x_copy.start()
        out_ref[...] = jnp.zeros_like(out_ref)

    h = h_ref[...]
    hid = _silu(_dot(h, wg_ref[...])) * _dot(h, wu_ref[...])
    out_ref[...] += _dot(hid.astype(BF16), wd_ref[...])

    @pl.when(f == pl.num_programs(1) - 1)
    def _():
        x_copy.wait()
        out_ref[...] = x_buf[...] + mod_ref[5:6, :] * out_ref[...]


FFN_CAST_STEPS = 64


def _ffn(h, x, mods, wg, wu, wd, to_bf16):
    tm, tf = 1024, 512
    nf = F_DENSE // tf
    assert (T_ALL // tm) * nf >= FFN_CAST_STEPS

    cast_in, cast_out, cast_shapes = _side_cast_specs(to_bf16, FFN_CAST_STEPS, lambda i, f: i * nf + f)
    outs = pl.pallas_call(
        functools.partial(_ffn_kernel, n_cast=len(to_bf16)),
        grid=(T_ALL // tm, nf),
        in_specs=[
            pl.BlockSpec((tm, D_MODEL), lambda i, f: (i, 0)),
            _ANY,
            pl.BlockSpec((None, 6, D_MODEL), lambda i, f: (_mod_group(i, tm), 0, 0)),
            pl.BlockSpec((D_MODEL, tf), lambda i, f: (0, f)),
            pl.BlockSpec((D_MODEL, tf), lambda i, f: (0, f)),
            pl.BlockSpec((tf, D_MODEL), lambda i, f: (f, 0)),
        ] + cast_in,
        out_specs=[pl.BlockSpec((tm, D_MODEL), lambda i, f: (i, 0))] + cast_out,
        out_shape=[jax.ShapeDtypeStruct((T_ALL, D_MODEL), F32)] + cast_shapes,
        scratch_shapes=[pltpu.VMEM((tm, D_MODEL), F32), pltpu.SemaphoreType.DMA(())],
        compiler_params=_cparams(("arbitrary", "arbitrary")),
        name="dense_ffn",
    )(h, x, mods, wg, wu, wd, *[part[0] for part in to_bf16])
    return outs[0], outs[1:]


def _rope_tables():
    n = DEC_SEQ
    rows = n // GRID_W
    t_row = jnp.repeat(jnp.arange(rows, dtype=F32), GRID_W)
    t_col = jnp.tile(jnp.arange(GRID_W, dtype=F32), rows)
    inv = 1.0 / (ROPE_THETA ** (jnp.arange(0, ROT_HALF, 2, dtype=F32) / ROT_HALF))
    ar, ac = t_row[:, None] * inv, t_col[:, None] * inv
    cos = jnp.concatenate([jnp.cos(ar), jnp.cos(ar), jnp.cos(ac), jnp.cos(ac)], axis=-1)
    sin_signed = jnp.concatenate([-jnp.sin(ar), jnp.sin(ar), -jnp.sin(ac), jnp.sin(ac)], axis=-1)
    return cos, sin_signed


def _pad_lanes(v, width=128):
    return jnp.pad(v, ((0, 0), (0, width - v.shape[-1])))


def kernel(x_prompt, x_sample, c, cache_k, cache_v, state_ssm_fwd, state_ssm_bwd, c_ctx, ada_w, ada_b, norm1_g, norm2_g, w_in, q_norm_g, k_norm_g, conv_w, conv_b, a_log_fwd, a_log_bwd, dt_bias_fwd, dt_bias_bwd, d_skip, ssd_norm_g, attn_out_g, w_out, ffn_w_gate, ffn_w_up, ffn_w_down, router_w, moe_w_gate, moe_w_up, moe_w_down, final_norm_g):
    assert DEPTH % 2 == 0
    cond = jnp.concatenate([c_ctx[None, :], c, jnp.zeros((N_COND - 1 - DEC_BATCH, D_MODEL), F32)], axis=0)
    mods_all = _ada_mods(cond, ada_w, ada_b).reshape(DEPTH, N_COND, 6, D_MODEL)
    cos, sin_signed = _rope_tables()

    w_in_t = jnp.swapaxes(w_in, 1, 2)
    x_segs = [x_prompt.reshape(T_PROMPT, D_MODEL), x_sample.reshape(T_SAMPLE, D_MODEL)]
    kv, states = None, None
    for l in range(DEPTH):
        mods = mods_all[l]
        w_out_rows = (w_out.reshape(DEPTH * D_MODEL, D_MODEL), l * D_MODEL, D_MODEL)
        proj, dt_raw, (w_o,) = _inproj(x_segs, mods, norm1_g[l][None, :], w_in_t, l, [w_out_rows])

        qg, kg, og = q_norm_g[l][None, :], k_norm_g[l][None, :], attn_out_g[l][None, :]
        o_p, k_all, v_all = _attention_prompt(proj, 0, qg, kg, og, l, kv)
        kv = (k_all, v_all)
        o_s = _attention_sample(proj, T_PROMPT, qg, kg, og, cache_k, cache_v, cos, sin_signed, l)

        p = {
            'conv_w': conv_w[l], 'conv_b': conv_b[l][None, :],
            'dt_bias': _pad_lanes(jnp.concatenate([dt_bias_fwd[l], dt_bias_bwd[l]])[None, :]),
            'a_log': _pad_lanes(jnp.concatenate([a_log_fwd[l], a_log_bwd[l]])[None, :]),
            'd_skip': jnp.repeat(d_skip[l], SSD_HEAD_DIM)[None, :],
            'ssd_norm_g': ssd_norm_g[l][None, :],
        }
        j = l // 2
        n_up, n_down = N_EXPERTS * D_MODEL, N_EXPERTS * F_EXPERT
        if l % 2 == 0:
            parts = [(ffn_w_gate.reshape(-1, F_DENSE), j * D_MODEL, D_MODEL),
                     (ffn_w_up.reshape(-1, F_DENSE), j * D_MODEL, D_MODEL),
                     (ffn_w_down.reshape(-1, D_MODEL), j * F_DENSE, F_DENSE)]
        else:
            parts = [(moe_w_down.reshape(-1, D_MODEL), j * n_down, n_down)]
        y_p, sf, sb, *mixer_w = _ssd(proj, dt_raw, p, SEQ, BATCH, 0, l, None, states, parts)
        states = (sf, sb)
        y_s, = _ssd(proj, dt_raw, p, DEC_SEQ, DEC_BATCH, T_PROMPT // DEC_SEQ, l, (state_ssm_fwd, state_ssm_bwd), None)

        g2 = norm2_g[l][None, :]
        if l % 2 == 0:
            x, h = _outproj([o_p, o_s], [y_p, y_s], x_segs, mods, g2, w_o, None)
            x, (eg, eu) = _ffn(h, x, mods, *mixer_w, [
                (moe_w_gate.reshape(-1, F_EXPERT), j * n_up, n_up),
                (moe_w_up.reshape(-1, F_EXPERT), j * n_up, n_up)])
            x_segs = [x]
        else:
            router_wt = jnp.pad(router_w[j].T, ((0, 16 - N_EXPERTS), (0, 0)))
            x, h, meta, counts = _outproj([o_p, o_s], [y_p, y_s], x_segs, mods, g2, w_o, router_wt)
            pos1, pos2, vt, ve, nv, lo, hi = _route_plan(meta, counts)
            xs = _dispatch(h, pos1, pos2)
            ys = _experts(xs, eg.reshape(N_EXPERTS, D_MODEL, F_EXPERT), eu.reshape(N_EXPERTS, D_MODEL, F_EXPERT),
                          mixer_w[0].reshape(N_EXPERTS, F_EXPERT, D_MODEL), vt, ve, nv, lo, hi)
            y_prompt, y_sample = _combine(ys, x, mods, _pad_lanes(meta[:2].T), pos1, pos2, final_norm_g[None, :])

    return (y_prompt.reshape(BATCH, SEQ, D_MODEL), y_sample.reshape(DEC_BATCH, DEC_SEQ, D_MODEL),
            kv[0], kv[1], states[0], states[1])
```

```python
import functools

import jax
import jax.numpy as jnp
from jax import lax
from jax.experimental import pallas as pl
from jax.experimental.pallas import tpu as pltpu

F32 = jnp.float32
BF16 = jnp.bfloat16

D_MODEL = 2048
BATCH = 16
SEQ = 256
DEPTH = 2
DEC_BATCH = 2
DEC_SEQ = 1024
PAST_LEN = 512
GRID_W = 64
D_ATTN = 1024
D_SSD = 1024
HEAD_DIM = 128
N_Q_HEADS = 8
N_KV_HEADS = 2
Q_PER_KV = 4
KV_DIM = 256
ROT_HALF = 64
ROPE_THETA = 10000.0
SSD_HEAD_DIM = 64
SSD_HEADS = 16
SSD_GROUPS = 2
HEADS_PER_GROUP = 8
D_STATE = 128
CONV_DIM = 1536
CHUNK = 128
N_MAIN = 4096
F_DENSE = 5632
N_EXPERTS = 8
F_EXPERT = 1024
EPS = 1e-6

T_PROMPT = BATCH * SEQ
T_SAMPLE = DEC_BATCH * DEC_SEQ
T_ALL = T_PROMPT + T_SAMPLE
N_COND = 16

VMEM_LIMIT = 58 * 1024 * 1024


def _cparams(sem):
    return pltpu.CompilerParams(dimension_semantics=sem, vmem_limit_bytes=VMEM_LIMIT)


def _mod_group(i, tm):
    return jnp.maximum(0, (i * tm - T_PROMPT + DEC_SEQ) // DEC_SEQ)


def _silu(x):
    return x * jax.nn.sigmoid(x)


def _rms(x, g):
    ms = jnp.mean(x * x, axis=-1, keepdims=True)
    return x * lax.rsqrt(ms + EPS) * g


def _dot(a, b):
    return jnp.dot(a, b, preferred_element_type=F32)


def _dot_nt(a, b):
    return lax.dot_general(a, b, (((1,), (1,)), ((), ())), preferred_element_type=F32)


def _split3(x):
    hi = x.astype(BF16)
    r1 = x - hi.astype(F32)
    mid = r1.astype(BF16)
    r2 = r1 - mid.astype(F32)
    return hi, mid, r2.astype(BF16)


def _ada_kernel(c_ref, w_ref, b_ref, o_ref):
    s = _silu(c_ref[...]).astype(BF16)
    o_ref[...] = _dot(s, w_ref[...].astype(BF16)) + b_ref[...]


def _ada_mods(cond, ada_w, ada_b):
    tn = 1024
    n_out = 6 * D_MODEL
    return pl.pallas_call(
        _ada_kernel,
        grid=(DEPTH, n_out // tn),
        in_specs=[
            pl.BlockSpec((N_COND, D_MODEL), lambda l, j: (0, 0)),
            pl.BlockSpec((None, D_MODEL, tn), lambda l, j: (l, 0, j)),
            pl.BlockSpec((None, 1, tn), lambda l, j: (l, 0, j)),
        ],
        out_specs=pl.BlockSpec((None, N_COND, tn), lambda l, j: (l, 0, j)),
        out_shape=jax.ShapeDtypeStruct((DEPTH, N_COND, n_out), F32),
        compiler_params=_cparams(("parallel", "parallel")),
        name="ada_mods",
    )(cond, ada_w, ada_b.reshape(DEPTH, 1, n_out))


def _side_cast_specs(parts, n_steps, linear_step):
    in_specs, out_specs, out_shapes = [], [], []
    for a, row0, nrows in parts:
        rows = nrows // n_steps
        blk0 = row0 // rows
        step = lambda *ids: jnp.minimum(linear_step(*ids), n_steps - 1)
        in_specs.append(pl.BlockSpec((rows, a.shape[1]), lambda *ids, blk0=blk0: (blk0 + step(*ids), 0)))
        out_specs.append(pl.BlockSpec((rows, a.shape[1]), lambda *ids: (step(*ids), 0)))
        out_shapes.append(jax.ShapeDtypeStruct((nrows, a.shape[1]), BF16))
    return in_specs, out_specs, out_shapes


def _side_cast(cast_in, cast_out):
    for src, dst in zip(cast_in, cast_out):
        dst[...] = src[...].astype(BF16)


INPROJ_NORM_ROWS = 1024
INPROJ_CAST_STEPS = 16


def _inproj_kernel(*refs, seg_rows, n_cast):
    n_seg = len(seg_rows)
    x_hbms = refs[:n_seg]
    mods_ref, g_ref, w_ref, wdt_ref = refs[n_seg:n_seg + 4]
    cast_in = refs[n_seg + 4:n_seg + 4 + n_cast]
    proj_ref, dt_ref = refs[n_seg + 4 + n_cast:n_seg + 6 + n_cast]
    cast_out = refs[n_seg + 6 + n_cast:n_seg + 6 + 2 * n_cast]
    x_buf, h_scr, sem = refs[n_seg + 6 + 2 * n_cast:]
    _side_cast(cast_in, cast_out)
    i, j = pl.program_id(0), pl.program_id(1)
    tm = x_buf.shape[0]

    def fetch(tile):
        start = 0
        for x_hbm, nrows in zip(x_hbms, seg_rows):
            b0, nb = start // tm, nrows // tm
            start += nrows

            @pl.when((tile >= b0) & (tile < b0 + nb))
            def _():
                r0 = pl.multiple_of((tile - b0) * tm, tm)
                pltpu.make_async_copy(x_hbm.at[pl.ds(r0, tm)], x_buf, sem).start()

    @pl.when(j == 0)
    def _():
        @pl.when(i == 0)
        def _():
            fetch(i)

        pltpu.make_async_copy(x_hbms[0].at[pl.ds(0, tm)], x_buf, sem).wait()
        for k in range(tm // INPROJ_NORM_ROWS):
            rows = slice(k * INPROJ_NORM_ROWS, (k + 1) * INPROJ_NORM_ROWS)
            mod = mods_ref[_mod_group(i * (tm // INPROJ_NORM_ROWS) + k, INPROJ_NORM_ROWS)]
            h = _rms(x_buf[rows, :], g_ref[...]) * (1.0 + mod[1:2, :]) + mod[0:1, :]
            h_scr[rows, :] = h.astype(BF16)

        @pl.when(i + 1 < pl.num_programs(0))
        def _():
            fetch(i + 1)

        n_dt = wdt_ref.shape[0]
        wdt = jnp.concatenate([wdt_ref[...], jnp.zeros((128 - n_dt, D_MODEL), F32)], axis=0)
        dt_ref[...] = _dot_nt(h_scr[...], wdt.astype(BF16))

    proj_ref[...] = _dot_nt(h_scr[...], w_ref[...].astype(BF16))


_ANY = pl.BlockSpec(memory_space=pl.ANY)


def _inproj(x_segs, mods, g, w_in_t, layer, to_bf16):
    tm, tn = 2048, 512
    n_dt = w_in_t.shape[1] - N_MAIN
    nj = N_MAIN // tn
    seg_rows = tuple(a.shape[0] for a in x_segs)
    assert all(r % tm == 0 for r in seg_rows) and sum(seg_rows) == T_ALL
    assert (T_ALL // tm) * nj >= INPROJ_CAST_STEPS
    cast_in, cast_out, cast_shapes = _side_cast_specs(to_bf16, INPROJ_CAST_STEPS, lambda i, j: i * nj + j)
    outs = pl.pallas_call(
        functools.partial(_inproj_kernel, seg_rows=seg_rows, n_cast=len(to_bf16)),
        grid=(T_ALL // tm, nj),
        in_specs=[_ANY] * len(x_segs) + [
            pl.BlockSpec(mods.shape, lambda i, j: (0, 0, 0)),
            pl.BlockSpec((1, D_MODEL), lambda i, j: (0, 0)),
            pl.BlockSpec((None, tn, D_MODEL), lambda i, j: (layer, j, 0)),
            pl.BlockSpec((None, n_dt, D_MODEL), lambda i, j: (layer, N_MAIN // n_dt, 0)),
        ] + cast_in,
        out_specs=[
            pl.BlockSpec((tm, tn), lambda i, j: (i, j)),
            pl.BlockSpec((tm, 128), lambda i, j: (i, 0)),
        ] + cast_out,
        out_shape=[
            jax.ShapeDtypeStruct((T_ALL, N_MAIN), F32),
            jax.ShapeDtypeStruct((T_ALL, 128), F32),
        ] + cast_shapes,
        scratch_shapes=[pltpu.VMEM((tm, D_MODEL), F32), pltpu.VMEM((tm, D_MODEL), BF16),
                        pltpu.SemaphoreType.DMA(())],
        compiler_params=_cparams(("arbitrary", "arbitrary")),
        name="inproj",
    )(*x_segs, mods, g, w_in_t, w_in_t, *[part[0] for part in to_bf16])
    return outs[0], outs[1], outs[2:]


def _rope(x, cos, sin_signed):
    lane = lax.broadcasted_iota(jnp.int32, x.shape, 1)
    first = (lane // (ROT_HALF // 2)) % 2 == 0
    swapped = jnp.where(first, pltpu.roll(x, HEAD_DIM - ROT_HALF // 2, 1), pltpu.roll(x, ROT_HALF // 2, 1))
    return x * cos + swapped * sin_signed


N_SCORE_BUFS = 3


def _attn_kernel(*refs, nk_new, has_ctx, n_prev):
    if has_ctx:
        (q_ref, kv_ref, qg_ref, kg_ref, og_ref, ck_ref, cv_ref, cq_ref, sq_ref, ckk_ref, skk_ref,
         o_ref, kb_scr, vb_scr, o_scr, s_scr) = refs
    elif n_prev:
        (q_ref, kv_ref, qg_ref, kg_ref, og_ref, pk_ref, pv_ref, o_ref, ko_ref, vo_ref,
         kb_scr, vb_scr, o_scr, s_scr) = refs
    else:
        (q_ref, kv_ref, qg_ref, kg_ref, og_ref, o_ref, ko_ref, vo_ref, kb_scr, vb_scr, o_scr, s_scr) = refs

    def prepare_kv():
        if not has_ctx and n_prev:
            ko_ref[0:n_prev] = pk_ref[...]
            vo_ref[0:n_prev] = pv_ref[...]
        for g in range(N_KV_HEADS):
            sl = slice(g * HEAD_DIM, (g + 1) * HEAD_DIM)
            kn = _rms(kv_ref[:, sl], kg_ref[...])
            v = kv_ref[:, KV_DIM + g * HEAD_DIM:KV_DIM + (g + 1) * HEAD_DIM]
            vsl = slice(2 * g * HEAD_DIM, (2 * g + 1) * HEAD_DIM)
            vb_scr[0:nk_new, vsl] = v.astype(BF16)
            vb_scr[:, (2 * g + 1) * HEAD_DIM:(2 * g + 2) * HEAD_DIM] = jnp.ones((vb_scr.shape[0], HEAD_DIM), BF16)
            if has_ctx:
                kb_scr[0:nk_new, sl] = _rope(kn, ckk_ref[...], skk_ref[...]).astype(BF16)
                kb_scr[nk_new:, sl] = ck_ref[:, g, :].astype(BF16)
                vb_scr[nk_new:, vsl] = cv_ref[:, g, :].astype(BF16)
            else:
                kb_scr[:, sl] = kn.astype(BF16)
                ko_ref[n_prev, :, g, :] = kn
                vo_ref[n_prev, :, g, :] = v

    prepare_kv()

    scale_log2e = HEAD_DIM ** -0.5 * 1.4426950408889634

    def scores(h):
        g = h // Q_PER_KV
        qn = _rms(q_ref[:, h * HEAD_DIM:(h + 1) * HEAD_DIM], qg_ref[...])
        if has_ctx:
            qn = _rope(qn, cq_ref[...], sq_ref[...])
        qb = (qn * scale_log2e).astype(BF16)
        s_scr[h % N_SCORE_BUFS] = _dot_nt(qb, kb_scr[:, g * HEAD_DIM:(g + 1) * HEAD_DIM])

    for h in range(N_SCORE_BUFS - 1):
        scores(h)
    for h in range(N_Q_HEADS):
        if h + N_SCORE_BUFS - 1 < N_Q_HEADS:
            scores(h + N_SCORE_BUFS - 1)
        g = h // Q_PER_KV
        s = s_scr[h % N_SCORE_BUFS]
        e = jnp.exp2(s - jnp.max(s, axis=-1, keepdims=True))
        pv = _dot(e.astype(BF16), vb_scr[:, 2 * g * HEAD_DIM:(2 * g + 2) * HEAD_DIM])
        o_scr[:, h * HEAD_DIM:(h + 1) * HEAD_DIM] = pv[:, :HEAD_DIM] / pv[:, HEAD_DIM:]
    o_ref[...] = _rms(o_scr[...], og_ref[...]).astype(BF16)


def _attention_prompt(proj, row0, qg, kg, og, layer, prev_kv):
    n = SEQ
    blk0 = row0 // n
    kern = functools.partial(_attn_kernel, nk_new=n, has_ctx=False, n_prev=layer)
    vec = lambda w: pl.BlockSpec((1, w), lambda b, i: (0, 0))
    cache_spec = lambda k: pl.BlockSpec((None, k, n, N_KV_HEADS, HEAD_DIM), lambda b, i: (b, 0, 0, 0, 0))
    cache_shape = jax.ShapeDtypeStruct((BATCH, layer + 1, n, N_KV_HEADS, HEAD_DIM), F32)
    in_specs = [
        pl.BlockSpec((n, D_ATTN), lambda b, i: (blk0 + b, 0)),
        pl.BlockSpec((n, 2 * KV_DIM), lambda b, i: (blk0 + b, 2)),
        vec(HEAD_DIM), vec(HEAD_DIM), vec(D_ATTN),
    ]
    args = [proj, proj, qg, kg, og]
    if layer:
        in_specs += [cache_spec(layer), cache_spec(layer)]
        args += list(prev_kv)
    return pl.pallas_call(
        kern,
        grid=(BATCH, 1),
        in_specs=in_specs,
        out_specs=[pl.BlockSpec((n, D_ATTN), lambda b, i: (b, 0)), cache_spec(layer + 1), cache_spec(layer + 1)],
        out_shape=[jax.ShapeDtypeStruct((T_PROMPT, D_ATTN), BF16), cache_shape, cache_shape],
        scratch_shapes=[
            pltpu.VMEM((n, KV_DIM), BF16), pltpu.VMEM((n, 2 * KV_DIM), BF16), pltpu.VMEM((n, D_ATTN), F32),
            pltpu.VMEM((N_SCORE_BUFS, n, n), F32),
        ],
        compiler_params=_cparams(("parallel", "arbitrary")),
        name="attn_prompt",
    )(*args)


def _attention_sample(proj, row0, qg, kg, og, ck, cv, cos, sin_signed, layer):
    n, tq = DEC_SEQ, DEC_SEQ
    nq = n // tq
    nk = n + PAST_LEN
    kern = functools.partial(_attn_kernel, nk_new=n, has_ctx=True, n_prev=0)
    vec = lambda w: pl.BlockSpec((1, w), lambda b, i: (0, 0))
    q_blk0 = row0 // tq
    kv_blk0 = row0 // n
    return pl.pallas_call(
        kern,
        grid=(DEC_BATCH, nq),
        in_specs=[
            pl.BlockSpec((tq, D_ATTN), lambda b, i: (q_blk0 + b * nq + i, 0)),
            pl.BlockSpec((n, 2 * KV_DIM), lambda b, i: (kv_blk0 + b, 2)),
            vec(HEAD_DIM), vec(HEAD_DIM), vec(D_ATTN),
            pl.BlockSpec((None, None, PAST_LEN, N_KV_HEADS, HEAD_DIM), lambda b, i: (b, layer, 0, 0, 0)),
            pl.BlockSpec((None, None, PAST_LEN, N_KV_HEADS, HEAD_DIM), lambda b, i: (b, layer, 0, 0, 0)),
            pl.BlockSpec((tq, HEAD_DIM), lambda b, i: (i, 0)),
            pl.BlockSpec((tq, HEAD_DIM), lambda b, i: (i, 0)),
            pl.BlockSpec((n, HEAD_DIM), lambda b, i: (0, 0)),
            pl.BlockSpec((n, HEAD_DIM), lambda b, i: (0, 0)),
        ],
        out_specs=pl.BlockSpec((tq, D_ATTN), lambda b, i: (b * nq + i, 0)),
        out_shape=jax.ShapeDtypeStruct((T_SAMPLE, D_ATTN), BF16),
        scratch_shapes=[
            pltpu.VMEM((nk, KV_DIM), BF16), pltpu.VMEM((nk, 2 * KV_DIM), BF16), pltpu.VMEM((tq, D_ATTN), F32),
            pltpu.VMEM((N_SCORE_BUFS, tq, nk), F32),
        ],
        compiler_params=_cparams(("parallel", "arbitrary")),
        name="attn_sample",
    )(proj, proj, qg, kg, og, ck, cv, cos, sin_signed, cos, sin_signed)


def _conv_silu(x, w, b):
    n = x.shape[0]
    row = lax.broadcasted_iota(jnp.int32, (n, 1), 0)
    prev = jnp.where(row == 0, 0.0, pltpu.roll(x, 1, 0))
    nxt = jnp.where(row == n - 1, 0.0, pltpu.roll(x, n - 1, 0))
    return _silu(prev * w[0:1, :] + x * w[1:2, :] + nxt * w[2:3, :] + b)


def _softplus(x):
    return jnp.maximum(x, 0.0) + jnp.log1p(jnp.exp(-jnp.abs(x)))


def _ssd_kernel(*refs, n, has_init, n_prev, n_cast):
    refs = list(refs)
    (za_ref, zb_ref, xa_ref, xb_ref, bc_ref, dt_ref, cw_ref, cb_ref, dtb_ref, alog_ref, dsk_ref,
     ng_ref) = refs[:12]
    del refs[:12]
    if has_init:
        sf0_ref, sb0_ref = refs.pop(0), refs.pop(0)
    elif n_prev:
        psf_ref, psb_ref = refs.pop(0), refs.pop(0)
    cast_in = [refs.pop(0) for _ in range(n_cast)]
    y_ref = refs.pop(0)
    if not has_init:
        sf_ref, sb_ref = refs.pop(0), refs.pop(0)
    cast_out = [refs.pop(0) for _ in range(n_cast)]
    xc_scr, bcc_scr, dts_scr, xt_scr, yt_scr, s_scr = refs
    _side_cast(cast_in, cast_out)
    nc = n // CHUNK
    gw = HEADS_PER_GROUP * SSD_HEAD_DIM

    xc_scr[:, 0:gw] = _conv_silu(xa_ref[...], cw_ref[:, 0:gw], cb_ref[:, 0:gw])
    xc_scr[:, gw:] = _conv_silu(xb_ref[...], cw_ref[:, gw:2 * gw], cb_ref[:, gw:2 * gw])
    bcc_scr[...] = _conv_silu(bc_ref[...], cw_ref[:, 2 * gw:], cb_ref[:, 2 * gw:])
    dts_scr[...] = _softplus(dt_ref[...] + dtb_ref[...])
    for g in range(SSD_GROUPS):
        hs = slice(g * HEADS_PER_GROUP, (g + 1) * HEADS_PER_GROUP)
        if has_init:
            s_scr[0, g] = sf0_ref[hs].reshape(gw, D_STATE)
            s_scr[1, g] = sb0_ref[hs].reshape(gw, D_STATE)
        else:
            s_scr[0, g] = jnp.zeros((gw, D_STATE), F32)
            s_scr[1, g] = jnp.zeros((gw, D_STATE), F32)

    def to_channel_major(c, carry):
        rows = pl.ds(pl.multiple_of(c * CHUNK, CHUNK), CHUNK)
        xt_scr[c] = xc_scr[rows, :].T
        yt_scr[c] = jnp.zeros((D_SSD, CHUNK), F32)
        return carry

    lax.fori_loop(0, nc, to_channel_major, 0, unroll=2)

    a_row = -jnp.exp(alog_ref[...])
    ri = lax.broadcasted_iota(jnp.int32, (CHUNK, CHUNK), 0)
    ci = lax.broadcasted_iota(jnp.int32, (CHUNK, CHUNK), 1)
    lower, upper = ci <= ri, ci >= ri
    n_dirs_heads = 2 * SSD_HEADS

    def chunk_sums(dirn, c):
        tri = (lower if dirn == 0 else upper).astype(BF16)
        valid_st = upper if dirn == 0 else lower
        tri_t = valid_st.astype(BF16)
        rows = pl.ds(pl.multiple_of(c * CHUNK, CHUNK), CHUNK)
        dt = dts_scr[rows, :]
        d = dt * a_row
        d1, d2, d3 = _split3(d)
        cs = _dot(tri, d1) + _dot(tri, d2) + _dot(tri, d3)
        dt_t = dt.T[0:n_dirs_heads, :]
        e1, e2, e3 = _split3(d.T[0:n_dirs_heads, :])
        cs_t = _dot(e1, tri_t) + _dot(e2, tri_t) + _dot(e3, tri_t)
        total = cs_t[:, CHUNK - 1:CHUNK] if dirn == 0 else cs_t[:, 0:1]
        return dict(
            dirn=dirn, c=c, rows=rows, valid_st=valid_st, cs=cs, cs_t=cs_t, dt_t=dt_t,
            e_in_t=jnp.exp(cs_t),
            to_end_t=jnp.exp(total - cs_t) * dt_t,
            dec_t=jnp.broadcast_to(jnp.exp(total), (n_dirs_heads, D_STATE)))

    def group_inputs(q, g):
        bm = bcc_scr[q['rows'], g * D_STATE:(g + 1) * D_STATE].astype(BF16)
        cm = bcc_scr[q['rows'], (SSD_GROUPS + g) * D_STATE:(SSD_GROUPS + g + 1) * D_STATE]
        return dict(bm=bm, g_st=_dot_nt(bm, cm.astype(BF16)),
                    c_nt=cm.T,
                    st=s_scr[q['dirn'], g],
                    xs_parts=[], dec_parts=[])

    def head(q, gi, g, hh):
        h = g * HEADS_PER_GROUP + hh
        r = q['dirn'] * SSD_HEADS + h
        c = q['c']
        ch = slice(h * SSD_HEAD_DIM, (h + 1) * SSD_HEAD_DIM)
        x_t = xt_scr[c, ch, :]
        diff = q['cs_t'][r:r + 1, :] - q['cs'][:, r:r + 1]
        a_st = (gi['g_st'] * jnp.exp(jnp.where(q['valid_st'], diff, -jnp.inf))).astype(BF16)
        c_e = (gi['c_nt'] * q['e_in_t'][r:r + 1, :]).astype(BF16)
        x_dt = (x_t * q['dt_t'][r:r + 1, :]).astype(BF16)
        s_h = gi['st'][hh * SSD_HEAD_DIM:(hh + 1) * SSD_HEAD_DIM, :].astype(BF16)
        y_h = _dot(jnp.concatenate([x_dt, s_h], axis=1), jnp.concatenate([a_st, c_e], axis=0))
        yt_scr[c, ch, :] = yt_scr[c, ch, :] + y_h
        gi['xs_parts'].append((x_t * q['to_end_t'][r:r + 1, :]).astype(BF16))
        gi['dec_parts'].append(jnp.broadcast_to(q['dec_t'][r:r + 1, :], (SSD_HEAD_DIM, D_STATE)))

    def group_update(q, gi, g):
        ds = _dot(jnp.concatenate(gi['xs_parts'], axis=0), gi['bm'])
        s_scr[q['dirn'], g] = gi['st'] * jnp.concatenate(gi['dec_parts'], axis=0) + ds

    def body(i, carry):
        qs = [chunk_sums(0, i), chunk_sums(1, nc - 1 - i)]
        chains = [(q, group_inputs(q, g), g) for g in range(SSD_GROUPS) for q in qs]
        for hh in range(HEADS_PER_GROUP):
            for q, gi, g in chains:
                head(q, gi, g, hh)
        for q, gi, g in chains:
            group_update(q, gi, g)
        return carry

    lax.fori_loop(0, nc, body, 0, unroll=2)

    def finish(c, carry):
        rows = pl.ds(pl.multiple_of(c * CHUNK, CHUNK), CHUNK)
        y = yt_scr[c].T + xc_scr[rows, :] * dsk_ref[...]
        ya = y[:, 0:gw] * _silu(za_ref[rows, :])
        yb = y[:, gw:] * _silu(zb_ref[rows, :])
        ms = (jnp.sum(ya * ya, axis=-1, keepdims=True) + jnp.sum(yb * yb, axis=-1, keepdims=True)) / D_SSD
        inv = lax.rsqrt(ms + EPS)
        y_ref[rows, 0:gw] = (ya * inv * ng_ref[:, 0:gw]).astype(BF16)
        y_ref[rows, gw:] = (yb * inv * ng_ref[:, gw:]).astype(BF16)
        return carry

    lax.fori_loop(0, nc, finish, 0, unroll=2)

    if not has_init:
        if n_prev:
            sf_ref[0:n_prev] = psf_ref[...]
            sb_ref[0:n_prev] = psb_ref[...]
        for g in range(SSD_GROUPS):
            hs = slice(g * HEADS_PER_GROUP, (g + 1) * HEADS_PER_GROUP)
            sf_ref[n_prev, hs] = s_scr[0, g].reshape(HEADS_PER_GROUP, SSD_HEAD_DIM, D_STATE)
            sb_ref[n_prev, hs] = s_scr[1, g].reshape(HEADS_PER_GROUP, SSD_HEAD_DIM, D_STATE)


def _ssd(proj, dt_raw, p, n, nb, row_blk0, layer, init, prev_states, to_bf16=()):
    has_init = init is not None
    n_prev = 0 if has_init else layer
    kern = functools.partial(_ssd_kernel, n=n, has_init=has_init, n_prev=n_prev, n_cast=len(to_bf16))
    cast_in, cast_out, cast_shapes = _side_cast_specs(to_bf16, nb, lambda b: b)
    col = lambda cb: pl.BlockSpec((n, 512), lambda b: (row_blk0 + b, cb))
    vec = lambda r, w: pl.BlockSpec((r, w), lambda b: (0, 0))
    layers_spec = lambda k: pl.BlockSpec((None, k, SSD_HEADS, SSD_HEAD_DIM, D_STATE), lambda b: (b, 0, 0, 0, 0))
    in_specs = [
        col(3), col(4), col(5), col(6), col(7),
        pl.BlockSpec((n, 128), lambda b: (row_blk0 + b, 0)),
        vec(3, CONV_DIM), vec(1, CONV_DIM), vec(1, 128), vec(1, 128), vec(1, D_SSD), vec(1, D_SSD),
    ]
    args = [proj, proj, proj, proj, proj, dt_raw, p['conv_w'], p['conv_b'], p['dt_bias'], p['a_log'],
            p['d_skip'], p['ssd_norm_g']]
    y_spec = pl.BlockSpec((n, D_SSD), lambda b: (b, 0))
    y_shape = jax.ShapeDtypeStruct((nb * n, D_SSD), BF16)
    if has_init:
        init_spec = pl.BlockSpec((None, None, SSD_HEADS, SSD_HEAD_DIM, D_STATE), lambda b: (b, layer, 0, 0, 0))
        in_specs += [init_spec, init_spec]
        args += list(init)
        out_specs, out_shape = [y_spec], [y_shape]
    else:
        if n_prev:
            in_specs += [layers_spec(n_prev), layers_spec(n_prev)]
            args += list(prev_states)
        st_shape = jax.ShapeDtypeStruct((nb, layer + 1, SSD_HEADS, SSD_HEAD_DIM, D_STATE), F32)
        out_specs = [y_spec, layers_spec(layer + 1), layers_spec(layer + 1)]
        out_shape = [y_shape, st_shape, st_shape]
    return pl.pallas_call(
        kern,
        grid=(nb,),
        in_specs=in_specs + cast_in,
        out_specs=out_specs + cast_out,
        out_shape=out_shape + cast_shapes,
        scratch_shapes=[
            pltpu.VMEM((n, D_SSD), F32), pltpu.VMEM((n, 512), F32), pltpu.VMEM((n, 128), F32),
            pltpu.VMEM((n // CHUNK, D_SSD, CHUNK), F32), pltpu.VMEM((n // CHUNK, D_SSD, CHUNK), F32),
            pltpu.VMEM((2, SSD_GROUPS, 512, D_STATE), F32),
        ],
        compiler_params=_cparams(("parallel",)),
        name="ssd_sample" if has_init else "ssd_prompt",
    )(*args, *[part[0] for part in to_bf16])


def _seg_specs(segs, tm):
    specs, bounds, start = [], [], 0
    for a in segs:
        b0, nblk = start // tm, a.shape[0] // tm
        specs.append(pl.BlockSpec((tm, a.shape[1]), lambda i, b0=b0, nblk=nblk: (jnp.clip(i - b0, 0, nblk - 1), 0)))
        bounds.append(b0)
        start += a.shape[0]
    return specs, tuple(bounds)


def _seg_pick(refs, bounds):
    i = pl.program_id(0)
    v = refs[0][...]
    for ref, b0 in zip(refs[1:], bounds[1:]):
        v = jnp.where(i >= b0, ref[...], v)
    return v


def _outproj_kernel(*refs, with_router, o_bounds, y_bounds, x_bounds):
    refs = list(refs)
    o_refs = [refs.pop(0) for _ in o_bounds]
    y_refs = [refs.pop(0) for _ in y_bounds]
    x_refs = [refs.pop(0) for _ in x_bounds]
    if with_router:
        mod_ref, g_ref, w_ref, rw_ref, xo_ref, h_ref, meta_ref, cnt_ref, carry_scr = refs
    else:
        mod_ref, g_ref, w_ref, xo_ref, h_ref = refs
    tm = xo_ref.shape[0]
    a = jnp.concatenate([_seg_pick(o_refs, o_bounds), _seg_pick(y_refs, y_bounds)], axis=1)
    xn = _seg_pick(x_refs, x_bounds) + mod_ref[2:3, :] * _dot(a, w_ref[...])
    xo_ref[...] = xn
    h = _rms(xn, g_ref[...]) * (1.0 + mod_ref[4:5, :]) + mod_ref[3:4, :]
    h_ref[...] = h.astype(h_ref.dtype)
    if with_router:

        @pl.when(pl.program_id(0) == 0)
        def _():
            carry_scr[...] = jnp.zeros_like(carry_scr)

        h1, h2, _ = _split3(h)
        w1, w2, _ = _split3(rw_ref[...])
        logits = _dot_nt(w1, h1) + _dot_nt(w2, h1) + _dot_nt(w1, h2)
        row = lax.broadcasted_iota(jnp.int32, logits.shape, 0)
        logits = jnp.where(row < N_EXPERTS, logits, -jnp.inf)
        e = jnp.exp(logits - jnp.max(logits, axis=0, keepdims=True))
        probs = e / jnp.sum(e, axis=0, keepdims=True)
        p1 = jnp.max(probs, axis=0, keepdims=True)
        i1 = jnp.min(jnp.where(probs == p1, row, 16), axis=0, keepdims=True)
        rest = jnp.where(row == i1, -1.0, probs)
        p2 = jnp.max(rest, axis=0, keepdims=True)
        i2 = jnp.min(jnp.where(rest == p2, row, 16), axis=0, keepdims=True)
        hit1, hit2 = row == i1, row == i2
        onehot = jnp.where(hit1 | hit2, 1.0, 0.0)
        ti = lax.broadcasted_iota(jnp.int32, (tm, tm), 0)
        tj = lax.broadcasted_iota(jnp.int32, (tm, tm), 1)
        before = jnp.where(ti < tj, 1.0, 0.0).astype(BF16)
        rank = carry_scr[:, 0:1] + _dot(onehot.astype(BF16), before)
        r1 = jnp.sum(jnp.where(hit1, rank, 0.0), axis=0, keepdims=True)
        r2 = jnp.sum(jnp.where(hit2, rank, 0.0), axis=0, keepdims=True)
        carry_scr[...] = carry_scr[...] + jnp.sum(onehot, axis=1, keepdims=True)
        cnt_ref[...] = carry_scr[...]
        r8 = lax.broadcasted_iota(jnp.int32, (8, tm), 0)
        vals = [p1 / (p1 + p2), p2 / (p1 + p2), i1.astype(F32), i2.astype(F32), r1, r2]
        meta = jnp.zeros((8, tm), F32)
        for k, v in enumerate(vals):
            meta = jnp.where(r8 == k, v, meta)
        meta_ref[...] = meta


def _outproj(o_segs, y_segs, x_segs, mods, g, w_out, router_wt):
    tm = 512
    with_router = router_wt is not None
    o_specs, o_bounds = _seg_specs(o_segs, tm)
    y_specs, y_bounds = _seg_specs(y_segs, tm)
    x_specs, x_bounds = _seg_specs(x_segs, tm)
    kern = functools.partial(_outproj_kernel, with_router=with_router, o_bounds=o_bounds, y_bounds=y_bounds,
                             x_bounds=x_bounds)
    in_specs = o_specs + y_specs + x_specs + [
        pl.BlockSpec((None, 6, D_MODEL), lambda i: (_mod_group(i, tm), 0, 0)),
        pl.BlockSpec((1, D_MODEL), lambda i: (0, 0)),
        pl.BlockSpec((D_MODEL, D_MODEL), lambda i: (0, 0), pipeline_mode=pl.Buffered(1)),
    ]
    args = list(o_segs) + list(y_segs) + list(x_segs) + [mods, g, w_out]
    row_spec = pl.BlockSpec((tm, D_MODEL), lambda i: (i, 0))
    out_specs = [row_spec, row_spec]
    out_shape = [jax.ShapeDtypeStruct((T_ALL, D_MODEL), F32),
                 jax.ShapeDtypeStruct((T_ALL, D_MODEL), F32 if with_router else BF16)]
    scratch = []
    if with_router:
        in_specs.append(pl.BlockSpec((16, D_MODEL), lambda i: (0, 0)))
        args.append(router_wt)
        out_specs += [pl.BlockSpec((8, tm), lambda i: (0, i)), pl.BlockSpec((16, 128), lambda i: (0, 0))]
        out_shape += [jax.ShapeDtypeStruct((8, T_ALL), F32), jax.ShapeDtypeStruct((16, 128), F32)]
        scratch = [pltpu.VMEM((16, 128), F32)]
    return pl.pallas_call(
        kern,
        grid=(T_ALL // tm,),
        in_specs=in_specs,
        out_specs=out_specs,
        out_shape=out_shape,
        scratch_shapes=scratch,
        compiler_params=_cparams(("arbitrary",)),
        name="outproj_router" if with_router else "outproj",
    )(*args)


MOE_ROWS = 2 * T_ALL
MOE_TILE = 256
MOE_TILES = MOE_ROWS // MOE_TILE
MOE_VISITS = MOE_TILES + N_EXPERTS - 1
ROW_DMA_UNROLL = 16


def _row_copy(src, s, dst, d, sem):
    return pltpu.make_async_copy(src.at[pl.ds(s, 1)], dst.at[pl.ds(d, 1)], sem)


def _dispatch_kernel(p1_ref, p2_ref, h_ref, xs_ref, sem):
    tm = h_ref.shape[0]

    def issue(r, c):
        _row_copy(h_ref, r, xs_ref, p1_ref[0, 0, r], sem.at[0]).start()
        _row_copy(h_ref, r, xs_ref, p2_ref[0, 0, r], sem.at[1]).start()
        return c

    lax.fori_loop(0, tm, issue, 0, unroll=ROW_DMA_UNROLL)
    pltpu.make_async_copy(h_ref, xs_ref.at[pl.ds(0, tm)], sem.at[0]).wait()
    pltpu.make_async_copy(h_ref, xs_ref.at[pl.ds(0, tm)], sem.at[1]).wait()


def _dispatch(h, pos1, pos2):
    tm = 512
    nt = T_ALL // tm
    idx = lambda: pl.BlockSpec((1, 1, tm), lambda i: (i, 0, 0), memory_space=pltpu.SMEM)
    return pl.pallas_call(
        _dispatch_kernel,
        grid=(nt,),
        in_specs=[idx(), idx(), pl.BlockSpec((tm, D_MODEL), lambda i: (i, 0))],
        out_specs=pl.BlockSpec(memory_space=pl.ANY),
        out_shape=jax.ShapeDtypeStruct((MOE_ROWS, D_MODEL), F32),
        scratch_shapes=[pltpu.SemaphoreType.DMA((2,))],
        compiler_params=_cparams(("arbitrary",)),
        name="moe_dispatch",
    )(pos1.reshape(nt, 1, tm), pos2.reshape(nt, 1, tm), h)


def _experts_kernel(vt_ref, ve_ref, nv_ref, lo_ref, hi_ref, xs_ref, wg_ref, wu_ref, wd_ref, y_ref):
    v = pl.program_id(0)

    @pl.when(v < nv_ref[0])
    def _():
        e = ve_ref[v]
        x = xs_ref[...].astype(BF16)
        hid = _silu(_dot(x, wg_ref[...])) * _dot(x, wu_ref[...])
        y = _dot(hid.astype(BF16), wd_ref[...])
        row = vt_ref[v] * MOE_TILE + lax.broadcasted_iota(jnp.int32, (MOE_TILE, 1), 0)
        mine = (row >= lo_ref[e]) & (row < hi_ref[e])
        first_visit = (v == 0) | (vt_ref[jnp.maximum(v - 1, 0)] != vt_ref[v])

        @pl.when(first_visit)
        def _():
            y_ref[...] = jnp.where(mine, y, 0.0)

        @pl.when(jnp.logical_not(first_visit))
        def _():
            y_ref[...] = jnp.where(mine, y, y_ref[...])


def _experts(xs, wg, wu, wd, vt, ve, nv, lo, hi):
    grid_spec = pltpu.PrefetchScalarGridSpec(
        num_scalar_prefetch=5,
        grid=(MOE_VISITS,),
        in_specs=[
            pl.BlockSpec((MOE_TILE, D_MODEL), lambda v, vt, ve, nv, lo, hi: (vt[v], 0)),
            pl.BlockSpec((None, D_MODEL, F_EXPERT), lambda v, vt, ve, nv, lo, hi: (ve[v], 0, 0)),
            pl.BlockSpec((None, D_MODEL, F_EXPERT), lambda v, vt, ve, nv, lo, hi: (ve[v], 0, 0)),
            pl.BlockSpec((None, F_EXPERT, D_MODEL), lambda v, vt, ve, nv, lo, hi: (ve[v], 0, 0)),
        ],
        out_specs=pl.BlockSpec((MOE_TILE, D_MODEL), lambda v, vt, ve, nv, lo, hi: (vt[v], 0)),
    )
    return pl.pallas_call(
        _experts_kernel,
        grid_spec=grid_spec,
        out_shape=jax.ShapeDtypeStruct((MOE_ROWS, D_MODEL), F32),
        compiler_params=_cparams(("arbitrary",)),
        name="moe_experts",
    )(vt, ve, nv, lo, hi, xs, wg, wu, wd)


def _combine_kernel(p1c_ref, p2c_ref, p1n_ref, p2n_ref, y_hbm, x_ref, mod_ref, gate_ref, fg_ref,
                    outp_ref, outs_ref, ya_buf, yb_buf, sem):
    i = pl.program_id(0)
    n = pl.num_programs(0)
    tm = x_ref.shape[0]
    slot = i % 2

    def gather(pa_ref, pb_ref, s):
        def issue(r, c):
            _row_copy(y_hbm, pa_ref[0, 0, r], ya_buf.at[s], r, sem.at[0, s]).start()
            _row_copy(y_hbm, pb_ref[0, 0, r], yb_buf.at[s], r, sem.at[1, s]).start()
            return c

        lax.fori_loop(0, tm, issue, 0, unroll=ROW_DMA_UNROLL)

    @pl.when(i == 0)
    def _():
        gather(p1c_ref, p2c_ref, 0)

    @pl.when(i + 1 < n)
    def _():
        gather(p1n_ref, p2n_ref, 1 - slot)

    pltpu.make_async_copy(y_hbm.at[pl.ds(0, tm)], ya_buf.at[slot], sem.at[0, slot]).wait()
    pltpu.make_async_copy(y_hbm.at[pl.ds(0, tm)], yb_buf.at[slot], sem.at[1, slot]).wait()
    g = gate_ref[...]
    mix = g[:, 0:1] * ya_buf[slot] + g[:, 1:2] * yb_buf[slot]
    xo = _rms(x_ref[...] + mod_ref[5:6, :] * mix, fg_ref[...])

    @pl.when(i < T_PROMPT // tm)
    def _():
        outp_ref[...] = xo

    @pl.when(i >= T_PROMPT // tm)
    def _():
        outs_ref[...] = xo


def _combine(y, x, mods, gate_cols, pos1, pos2, final_g):
    tm = 256
    nt = T_ALL // tm
    ntp = T_PROMPT // tm
    cur = lambda: pl.BlockSpec((1, 1, tm), lambda i: (i, 0, 0), memory_space=pltpu.SMEM)
    nxt = lambda: pl.BlockSpec((1, 1, tm), lambda i: (jnp.minimum(i + 1, nt - 1), 0, 0), memory_space=pltpu.SMEM)
    p1, p2 = pos1.reshape(nt, 1, tm), pos2.reshape(nt, 1, tm)
    return pl.pallas_call(
        _combine_kernel,
        grid=(nt,),
        in_specs=[
            cur(), cur(), nxt(), nxt(),
            pl.BlockSpec(memory_space=pl.ANY),
            pl.BlockSpec((tm, D_MODEL), lambda i: (i, 0)),
            pl.BlockSpec((None, 6, D_MODEL), lambda i: (_mod_group(i, tm), 0, 0)),
            pl.BlockSpec((tm, 128), lambda i: (i, 0)),
            pl.BlockSpec((1, D_MODEL), lambda i: (0, 0)),
        ],
        out_specs=[
            pl.BlockSpec((tm, D_MODEL), lambda i: (jnp.minimum(i, ntp - 1), 0)),
            pl.BlockSpec((tm, D_MODEL), lambda i: (jnp.maximum(i - ntp, 0), 0)),
        ],
        out_shape=[
            jax.ShapeDtypeStruct((T_PROMPT, D_MODEL), F32), jax.ShapeDtypeStruct((T_SAMPLE, D_MODEL), F32),
        ],
        scratch_shapes=[
            pltpu.VMEM((2, tm, D_MODEL), F32), pltpu.VMEM((2, tm, D_MODEL), F32),
            pltpu.SemaphoreType.DMA((2, 2)),
        ],
        compiler_params=_cparams(("arbitrary",)),
        name="moe_combine",
    )(p1, p2, p1, p2, y, x, mods, gate_cols, final_g)


def _route_plan(meta, counts):
    i1, i2 = meta[2].astype(jnp.int32), meta[3].astype(jnp.int32)
    r1, r2 = meta[4].astype(jnp.int32), meta[5].astype(jnp.int32)
    cnt = counts[:N_EXPERTS, 0].astype(jnp.int32)
    hi = jnp.cumsum(cnt)
    lo = hi - cnt
    ex = jnp.arange(N_EXPERTS, dtype=jnp.int32)
    pos1 = jnp.sum(jnp.where(i1[:, None] == ex[None, :], lo[None, :], 0), axis=1) + r1
    pos2 = jnp.sum(jnp.where(i2[:, None] == ex[None, :], lo[None, :], 0), axis=1) + r2
    first_tile = lo // MOE_TILE
    n_vis_e = jnp.where(cnt > 0, (hi - 1) // MOE_TILE - first_tile + 1, 0)
    vis_hi = jnp.cumsum(n_vis_e)
    vis_lo = vis_hi - n_vis_e
    nv = vis_hi[-1]
    v = jnp.minimum(jnp.arange(MOE_VISITS, dtype=jnp.int32), nv - 1)
    ve = jnp.minimum(jnp.sum(v[:, None] >= vis_hi[None, :], axis=1), N_EXPERTS - 1).astype(jnp.int32)
    pick = lambda tab: jnp.sum(jnp.where(ve[:, None] == ex[None, :], tab[None, :], 0), axis=1)
    vt = (pick(first_tile) + v - pick(vis_lo)).astype(jnp.int32)
    return pos1, pos2, vt, ve, nv.reshape(1).astype(jnp.int32), lo.astype(jnp.int32), hi.astype(jnp.int32)


def _ffn_kernel(*refs, n_cast):
    h_ref, x_hbm, mod_ref, wg_ref, wu_ref, wd_ref = refs[:6]
    cast_in = refs[6:6 + n_cast]
    out_ref = refs[6 + n_cast]
    cast_out = refs[7 + n_cast:7 + 2 * n_cast]
    x_buf, sem = refs[7 + 2 * n_cast:]
    _side_cast(cast_in, cast_out)
    i, f = pl.program_id(0), pl.program_id(1)
    tm = h_ref.shape[0]
    x_copy = pltpu.make_async_copy(x_hbm.at[pl.ds(pl.multiple_of(i * tm, tm), tm)], x_buf, sem)

    @pl.when(f == 0)
    def _():
        x_copy.start()
        out_ref[...] = jnp.zeros_like(out_ref)

    h = h_ref[...]
    hid = _silu(_dot(h, wg_ref[...])) * _dot(h, wu_ref[...])
    out_ref[...] += _dot(hid.astype(BF16), wd_ref[...])

    @pl.when(f == pl.num_programs(1) - 1)
    def _():
        x_copy.wait()
        out_ref[...] = x_buf[...] + mod_ref[5:6, :] * out_ref[...]


FFN_CAST_STEPS = 64


def _ffn(h, x, mods, wg, wu, wd, to_bf16):
    tm, tf = 1024, 512
    nf = F_DENSE // tf
    assert (T_ALL // tm) * nf >= FFN_CAST_STEPS

    cast_in, cast_out, cast_shapes = _side_cast_specs(to_bf16, FFN_CAST_STEPS, lambda i, f: i * nf + f)
    outs = pl.pallas_call(
        functools.partial(_ffn_kernel, n_cast=len(to_bf16)),
        grid=(T_ALL // tm, nf),
        in_specs=[
            pl.BlockSpec((tm, D_MODEL), lambda i, f: (i, 0)),
            _ANY,
            pl.BlockSpec((None, 6, D_MODEL), lambda i, f: (_mod_group(i, tm), 0, 0)),
            pl.BlockSpec((D_MODEL, tf), lambda i, f: (0, f)),
            pl.BlockSpec((D_MODEL, tf), lambda i, f: (0, f)),
            pl.BlockSpec((tf, D_MODEL), lambda i, f: (f, 0)),
        ] + cast_in,
        out_specs=[pl.BlockSpec((tm, D_MODEL), lambda i, f: (i, 0))] + cast_out,
        out_shape=[jax.ShapeDtypeStruct((T_ALL, D_MODEL), F32)] + cast_shapes,
        scratch_shapes=[pltpu.VMEM((tm, D_MODEL), F32), pltpu.SemaphoreType.DMA(())],
        compiler_params=_cparams(("arbitrary", "arbitrary")),
        name="dense_ffn",
    )(h, x, mods, wg, wu, wd, *[part[0] for part in to_bf16])
    return outs[0], outs[1:]


def _rope_tables():
    n = DEC_SEQ
    rows = n // GRID_W
    t_row = jnp.repeat(jnp.arange(rows, dtype=F32), GRID_W)
    t_col = jnp.tile(jnp.arange(GRID_W, dtype=F32), rows)
    inv = 1.0 / (ROPE_THETA ** (jnp.arange(0, ROT_HALF, 2, dtype=F32) / ROT_HALF))
    ar, ac = t_row[:, None] * inv, t_col[:, None] * inv
    cos = jnp.concatenate([jnp.cos(ar), jnp.cos(ar), jnp.cos(ac), jnp.cos(ac)], axis=-1)
    sin_signed = jnp.concatenate([-jnp.sin(ar), jnp.sin(ar), -jnp.sin(ac), jnp.sin(ac)], axis=-1)
    return cos, sin_signed


def _pad_lanes(v, width=128):
    return jnp.pad(v, ((0, 0), (0, width - v.shape[-1])))


def kernel(x_prompt, x_sample, c, cache_k, cache_v, state_ssm_fwd, state_ssm_bwd, c_ctx, ada_w, ada_b, norm1_g, norm2_g, w_in, q_norm_g, k_norm_g, conv_w, conv_b, a_log_fwd, a_log_bwd, dt_bias_fwd, dt_bias_bwd, d_skip, ssd_norm_g, attn_out_g, w_out, ffn_w_gate, ffn_w_up, ffn_w_down, router_w, moe_w_gate, moe_w_up, moe_w_down, final_norm_g):
    assert DEPTH % 2 == 0
    cond = jnp.concatenate([c_ctx[None, :], c, jnp.zeros((N_COND - 1 - DEC_BATCH, D_MODEL), F32)], axis=0)
    mods_all = _ada_mods(cond, ada_w, ada_b).reshape(DEPTH, N_COND, 6, D_MODEL)
    cos, sin_signed = _rope_tables()

    w_in_t = jnp.swapaxes(w_in, 1, 2)
    x_segs = [x_prompt.reshape(T_PROMPT, D_MODEL), x_sample.reshape(T_SAMPLE, D_MODEL)]
    kv, states = None, None
    for l in range(DEPTH):
        mods = mods_all[l]
        w_out_rows = (w_out.reshape(DEPTH * D_MODEL, D_MODEL), l * D_MODEL, D_MODEL)
        proj, dt_raw, (w_o,) = _inproj(x_segs, mods, norm1_g[l][None, :], w_in_t, l, [w_out_rows])

        qg, kg, og = q_norm_g[l][None, :], k_norm_g[l][None, :], attn_out_g[l][None, :]
        o_p, k_all, v_all = _attention_prompt(proj, 0, qg, kg, og, l, kv)
        kv = (k_all, v_all)
        o_s = _attention_sample(proj, T_PROMPT, qg, kg, og, cache_k, cache_v, cos, sin_signed, l)

        p = {
            'conv_w': conv_w[l], 'conv_b': conv_b[l][None, :],
            'dt_bias': _pad_lanes(jnp.concatenate([dt_bias_fwd[l], dt_bias_bwd[l]])[None, :]),
            'a_log': _pad_lanes(jnp.concatenate([a_log_fwd[l], a_log_bwd[l]])[None, :]),
            'd_skip': jnp.repeat(d_skip[l], SSD_HEAD_DIM)[None, :],
            'ssd_norm_g': ssd_norm_g[l][None, :],
        }
        j = l // 2
        n_up, n_down = N_EXPERTS * D_MODEL, N_EXPERTS * F_EXPERT
        if l % 2 == 0:
            parts = [(ffn_w_gate.reshape(-1, F_DENSE), j * D_MODEL, D_MODEL),
                     (ffn_w_up.reshape(-1, F_DENSE), j * D_MODEL, D_MODEL),
                     (ffn_w_down.reshape(-1, D_MODEL), j * F_DENSE, F_DENSE)]
        else:
            parts = [(moe_w_down.reshape(-1, D_MODEL), j * n_down, n_down)]
        y_p, sf, sb, *mixer_w = _ssd(proj, dt_raw, p, SEQ, BATCH, 0, l, None, states, parts)
        states = (sf, sb)
        y_s, = _ssd(proj, dt_raw, p, DEC_SEQ, DEC_BATCH, T_PROMPT // DEC_SEQ, l, (state_ssm_fwd, state_ssm_bwd), None)

        g2 = norm2_g[l][None, :]
        if l % 2 == 0:
            x, h = _outproj([o_p, o_s], [y_p, y_s], x_segs, mods, g2, w_o, None)
            x, (eg, eu) = _ffn(h, x, mods, *mixer_w, [
                (moe_w_gate.reshape(-1, F_EXPERT), j * n_up, n_up),
                (moe_w_up.reshape(-1, F_EXPERT), j * n_up, n_up)])
            x_segs = [x]
        else:
            router_wt = jnp.pad(router_w[j].T, ((0, 16 - N_EXPERTS), (0, 0)))
            x, h, meta, counts = _outproj([o_p, o_s], [y_p, y_s], x_segs, mods, g2, w_o, router_wt)
            pos1, pos2, vt, ve, nv, lo, hi = _route_plan(meta, counts)
            xs = _dispatch(h, pos1, pos2)
            ys = _experts(xs, eg.reshape(N_EXPERTS, D_MODEL, F_EXPERT), eu.reshape(N_EXPERTS, D_MODEL, F_EXPERT),
                          mixer_w[0].reshape(N_EXPERTS, F_EXPERT, D_MODEL), vt, ve, nv, lo, hi)
            y_prompt, y_sample = _combine(ys, x, mods, _pad_lanes(meta[:2].T), pos1, pos2, final_norm_g[None, :])

    return (y_prompt.reshape(BATCH, SEQ, D_MODEL), y_sample.reshape(DEC_BATCH, DEC_SEQ, D_MODEL),
            kv[0], kv[1], states[0], states[1])
```

```python
import functools

import jax
import jax.numpy as jnp
from jax import lax
from jax.experimental import pallas as pl
from jax.experimental.pallas import tpu as pltpu

F32 = jnp.float32
BF16 = jnp.bfloat16

D_MODEL = 2048
BATCH = 16
SEQ = 256
DEPTH = 2
DEC_BATCH = 2
DEC_SEQ = 1024
PAST_LEN = 512
GRID_W = 64
D_ATTN = 1024
D_SSD = 1024
HEAD_DIM = 128
N_Q_HEADS = 8
N_KV_HEADS = 2
Q_PER_KV = 4
KV_DIM = 256
ROT_HALF = 64
ROPE_THETA = 10000.0
SSD_HEAD_DIM = 64
SSD_HEADS = 16
SSD_GROUPS = 2
HEADS_PER_GROUP = 8
D_STATE = 128
CONV_DIM = 1536
CHUNK = 128
N_MAIN = 4096
F_DENSE = 5632
N_EXPERTS = 8
F_EXPERT = 1024
EPS = 1e-6

T_PROMPT = BATCH * SEQ
T_SAMPLE = DEC_BATCH * DEC_SEQ
T_ALL = T_PROMPT + T_SAMPLE
N_COND = 16

VMEM_LIMIT = 58 * 1024 * 1024


def _cparams(sem):
    return pltpu.CompilerParams(dimension_semantics=sem, vmem_limit_bytes=VMEM_LIMIT)


def _mod_group(i, tm):
    return jnp.maximum(0, (i * tm - T_PROMPT + DEC_SEQ) // DEC_SEQ)


def _silu(x):
    return x * jax.nn.sigmoid(x)


def _rms(x, g):
    ms = jnp.mean(x * x, axis=-1, keepdims=True)
    return x * lax.rsqrt(ms + EPS) * g


def _dot(a, b):
    return jnp.dot(a, b, preferred_element_type=F32)


def _dot_nt(a, b):
    return lax.dot_general(a, b, (((1,), (1,)), ((), ())), preferred_element_type=F32)


def _split3(x):
    hi = x.astype(BF16)
    r1 = x - hi.astype(F32)
    mid = r1.astype(BF16)
    r2 = r1 - mid.astype(F32)
    return hi, mid, r2.astype(BF16)


def _ada_kernel(c_ref, w_ref, b_ref, o_ref):
    s = _silu(c_ref[...]).astype(BF16)
    o_ref[...] = _dot(s, w_ref[...].astype(BF16)) + b_ref[...]


def _ada_mods(cond, ada_w, ada_b):
    tn = 1024
    n_out = 6 * D_MODEL
    return pl.pallas_call(
        _ada_kernel,
        grid=(DEPTH, n_out // tn),
        in_specs=[
            pl.BlockSpec((N_COND, D_MODEL), lambda l, j: (0, 0)),
            pl.BlockSpec((None, D_MODEL, tn), lambda l, j: (l, 0, j)),
            pl.BlockSpec((None, 1, tn), lambda l, j: (l, 0, j)),
        ],
        out_specs=pl.BlockSpec((None, N_COND, tn), lambda l, j: (l, 0, j)),
        out_shape=jax.ShapeDtypeStruct((DEPTH, N_COND, n_out), F32),
        compiler_params=_cparams(("parallel", "parallel")),
        name="ada_mods",
    )(cond, ada_w, ada_b.reshape(DEPTH, 1, n_out))


def _side_cast_specs(parts, n_steps, linear_step):
    in_specs, out_specs, out_shapes = [], [], []
    for a, row0, nrows in parts:
        rows = nrows // n_steps
        blk0 = row0 // rows
        step = lambda *ids: jnp.minimum(linear_step(*ids), n_steps - 1)
        in_specs.append(pl.BlockSpec((rows, a.shape[1]), lambda *ids, blk0=blk0: (blk0 + step(*ids), 0)))
        out_specs.append(pl.BlockSpec((rows, a.shape[1]), lambda *ids: (step(*ids), 0)))
        out_shapes.append(jax.ShapeDtypeStruct((nrows, a.shape[1]), BF16))
    return in_specs, out_specs, out_shapes


def _side_cast(cast_in, cast_out):
    for src, dst in zip(cast_in, cast_out):
        dst[...] = src[...].astype(BF16)


INPROJ_NORM_ROWS = 1024
INPROJ_CAST_STEPS = 16


def _inproj_kernel(*refs, seg_rows, n_cast):
    n_seg = len(seg_rows)
    x_hbms = refs[:n_seg]
    mods_ref, g_ref, w_ref, wdt_ref = refs[n_seg:n_seg + 4]
    cast_in = refs[n_seg + 4:n_seg + 4 + n_cast]
    proj_ref, dt_ref = refs[n_seg + 4 + n_cast:n_seg + 6 + n_cast]
    cast_out = refs[n_seg + 6 + n_cast:n_seg + 6 + 2 * n_cast]
    x_buf, h_scr, sem = refs[n_seg + 6 + 2 * n_cast:]
    _side_cast(cast_in, cast_out)
    i, j = pl.program_id(0), pl.program_id(1)
    tm = x_buf.shape[0]

    def fetch(tile):
        start = 0
        for x_hbm, nrows in zip(x_hbms, seg_rows):
            b0, nb = start // tm, nrows // tm
            start += nrows

            @pl.when((tile >= b0) & (tile < b0 + nb))
            def _():
                r0 = pl.multiple_of((tile - b0) * tm, tm)
                pltpu.make_async_copy(x_hbm.at[pl.ds(r0, tm)], x_buf, sem).start()

    @pl.when(j == 0)
    def _():
        @pl.when(i == 0)
        def _():
            fetch(i)

        pltpu.make_async_copy(x_hbms[0].at[pl.ds(0, tm)], x_buf, sem).wait()
        for k in range(tm // INPROJ_NORM_ROWS):
            rows = slice(k * INPROJ_NORM_ROWS, (k + 1) * INPROJ_NORM_ROWS)
            mod = mods_ref[_mod_group(i * (tm // INPROJ_NORM_ROWS) + k, INPROJ_NORM_ROWS)]
            h = _rms(x_buf[rows, :], g_ref[...]) * (1.0 + mod[1:2, :]) + mod[0:1, :]
            h_scr[rows, :] = h.astype(BF16)

        @pl.when(i + 1 < pl.num_programs(0))
        def _():
            fetch(i + 1)

        n_dt = wdt_ref.shape[0]
        wdt = jnp.concatenate([wdt_ref[...], jnp.zeros((128 - n_dt, D_MODEL), F32)], axis=0)
        dt_ref[...] = _dot_nt(h_scr[...], wdt.astype(BF16))

    proj_ref[...] = _dot_nt(h_scr[...], w_ref[...].astype(BF16))


_ANY = pl.BlockSpec(memory_space=pl.ANY)


def _inproj(x_segs, mods, g, w_in_t, layer, to_bf16):
    tm, tn = 2048, 512
    n_dt = w_in_t.shape[1] - N_MAIN
    nj = N_MAIN // tn
    seg_rows = tuple(a.shape[0] for a in x_segs)
    assert all(r % tm == 0 for r in seg_rows) and sum(seg_rows) == T_ALL
    assert (T_ALL // tm) * nj >= INPROJ_CAST_STEPS
    cast_in, cast_out, cast_shapes = _side_cast_specs(to_bf16, INPROJ_CAST_STEPS, lambda i, j: i * nj + j)
    outs = pl.pallas_call(
        functools.partial(_inproj_kernel, seg_rows=seg_rows, n_cast=len(to_bf16)),
        grid=(T_ALL // tm, nj),
        in_specs=[_ANY] * len(x_segs) + [
            pl.BlockSpec(mods.shape, lambda i, j: (0, 0, 0)),
            pl.BlockSpec((1, D_MODEL), lambda i, j: (0, 0)),
            pl.BlockSpec((None, tn, D_MODEL), lambda i, j: (layer, j, 0)),
            pl.BlockSpec((None, n_dt, D_MODEL), lambda i, j: (layer, N_MAIN // n_dt, 0)),
        ] + cast_in,
        out_specs=[
            pl.BlockSpec((tm, tn), lambda i, j: (i, j)),
            pl.BlockSpec((tm, 128), lambda i, j: (i, 0)),
        ] + cast_out,
        out_shape=[
            jax.ShapeDtypeStruct((T_ALL, N_MAIN), F32),
            jax.ShapeDtypeStruct((T_ALL, 128), F32),
        ] + cast_shapes,
        scratch_shapes=[pltpu.VMEM((tm, D_MODEL), F32), pltpu.VMEM((tm, D_MODEL), BF16),
                        pltpu.SemaphoreType.DMA(())],
        compiler_params=_cparams(("arbitrary", "arbitrary")),
        name="inproj",
    )(*x_segs, mods, g, w_in_t, w_in_t, *[part[0] for part in to_bf16])
    return outs[0], outs[1], outs[2:]


def _rope(x, cos, sin_signed):
    lane = lax.broadcasted_iota(jnp.int32, x.shape, 1)
    first = (lane // (ROT_HALF // 2)) % 2 == 0
    swapped = jnp.where(first, pltpu.roll(x, HEAD_DIM - ROT_HALF // 2, 1), pltpu.roll(x, ROT_HALF // 2, 1))
    return x * cos + swapped * sin_signed


N_SCORE_BUFS = 3


def _attn_sequence(*refs, nk_new, has_ctx, n_prev):
    if has_ctx:
        (q_ref, kv_ref, qg_ref, kg_ref, og_ref, ck_ref, cv_ref, cq_ref, sq_ref, ckk_ref, skk_ref,
         o_ref, kb_scr, vb_scr, o_scr, s_scr) = refs
    elif n_prev:
        (q_ref, kv_ref, qg_ref, kg_ref, og_ref, pk_ref, pv_ref, o_ref, ko_ref, vo_ref,
         kb_scr, vb_scr, o_scr, s_scr) = refs
    else:
        (q_ref, kv_ref, qg_ref, kg_ref, og_ref, o_ref, ko_ref, vo_ref, kb_scr, vb_scr, o_scr, s_scr) = refs

    @pl.when(pl.program_id(1) == 0)
    def _():
        if not has_ctx and n_prev:
            ko_ref[0:n_prev] = pk_ref[...]
            vo_ref[0:n_prev] = pv_ref[...]
        for g in range(N_KV_HEADS):
            sl = slice(g * HEAD_DIM, (g + 1) * HEAD_DIM)
            kn = _rms(kv_ref[:, sl], kg_ref[...])
            v = kv_ref[:, KV_DIM + g * HEAD_DIM:KV_DIM + (g + 1) * HEAD_DIM]
            vsl = slice(2 * g * HEAD_DIM, (2 * g + 1) * HEAD_DIM)
            vb_scr[0:nk_new, vsl] = v.astype(BF16)
            vb_scr[:, (2 * g + 1) * HEAD_DIM:(2 * g + 2) * HEAD_DIM] = jnp.ones((vb_scr.shape[0], HEAD_DIM), BF16)
            if has_ctx:
                kb_scr[0:nk_new, sl] = _rope(kn, ckk_ref[...], skk_ref[...]).astype(BF16)
                kb_scr[nk_new:, sl] = ck_ref[:, g, :].astype(BF16)
                vb_scr[nk_new:, vsl] = cv_ref[:, g, :].astype(BF16)
            else:
                kb_scr[:, sl] = kn.astype(BF16)
                ko_ref[n_prev, :, g, :] = kn
                vo_ref[n_prev, :, g, :] = v

    scale_log2e = HEAD_DIM ** -0.5 * 1.4426950408889634

    def scores(h):
        g = h // Q_PER_KV
        qn = _rms(q_ref[:, h * HEAD_DIM:(h + 1) * HEAD_DIM], qg_ref[...])
        if has_ctx:
            qn = _rope(qn, cq_ref[...], sq_ref[...])
        qb = (qn * scale_log2e).astype(BF16)
        s_scr[h % N_SCORE_BUFS] = _dot_nt(qb, kb_scr[:, g * HEAD_DIM:(g + 1) * HEAD_DIM])

    for h in range(N_SCORE_BUFS - 1):
        scores(h)
    for h in range(N_Q_HEADS):
        if h + N_SCORE_BUFS - 1 < N_Q_HEADS:
            scores(h + N_SCORE_BUFS - 1)
        g = h // Q_PER_KV
        s = s_scr[h % N_SCORE_BUFS]
        e = jnp.exp2(s - jnp.max(s, axis=-1, keepdims=True))
        pv = _dot(e.astype(BF16), vb_scr[:, 2 * g * HEAD_DIM:(2 * g + 2) * HEAD_DIM])
        o_scr[:, h * HEAD_DIM:(h + 1) * HEAD_DIM] = pv[:, :HEAD_DIM] / pv[:, HEAD_DIM:]
    o_ref[...] = _rms(o_scr[...], og_ref[...]).astype(BF16)


PROMPT_SEQS_PER_STEP = 2


def _attn_kernel(*refs, nk_new, has_ctx, n_prev, n_seq):
    if n_seq == 1:
        return _attn_sequence(*refs, nk_new=nk_new, has_ctx=has_ctx, n_prev=n_prev)
    assert not has_ctx
    n_in = 7 if n_prev else 5
    q_ref, kv_ref, qg_ref, kg_ref, og_ref = refs[:5]
    o_ref, ko_ref, vo_ref, kb_scr, vb_scr, o_scr, s_scr = refs[n_in:]
    for s in range(n_seq):
        rows = pl.ds(s * nk_new, nk_new)
        prev = [r.at[s] for r in refs[5:n_in]]
        _attn_sequence(q_ref.at[rows], kv_ref.at[rows], qg_ref, kg_ref, og_ref, *prev,
                       o_ref.at[rows], ko_ref.at[s], vo_ref.at[s], kb_scr.at[rows], vb_scr.at[rows],
                       o_scr.at[rows], s_scr.at[pl.ds(s * N_SCORE_BUFS, N_SCORE_BUFS)],
                       nk_new=nk_new, has_ctx=False, n_prev=n_prev)


def _attention_prompt(proj, row0, qg, kg, og, layer, prev_kv):
    n, ns = SEQ, PROMPT_SEQS_PER_STEP
    blk0 = row0 // (ns * n)
    kern = functools.partial(_attn_kernel, nk_new=n, has_ctx=False, n_prev=layer, n_seq=ns)
    vec = lambda w: pl.BlockSpec((1, w), lambda b, i: (0, 0))
    cache_spec = lambda k: pl.BlockSpec((ns, k, n, N_KV_HEADS, HEAD_DIM), lambda b, i: (b, 0, 0, 0, 0))
    cache_shape = jax.ShapeDtypeStruct((BATCH, layer + 1, n, N_KV_HEADS, HEAD_DIM), F32)
    in_specs = [
        pl.BlockSpec((ns * n, D_ATTN), lambda b, i: (blk0 + b, 0)),
        pl.BlockSpec((ns * n, 2 * KV_DIM), lambda b, i: (blk0 + b, 2)),
        vec(HEAD_DIM), vec(HEAD_DIM), vec(D_ATTN),
    ]
    args = [proj, proj, qg, kg, og]
    if layer:
        in_specs += [cache_spec(layer), cache_spec(layer)]
        args += list(prev_kv)
    return pl.pallas_call(
        kern,
        grid=(BATCH // ns, 1),
        in_specs=in_specs,
        out_specs=[pl.BlockSpec((ns * n, D_ATTN), lambda b, i: (b, 0)), cache_spec(layer + 1),
                   cache_spec(layer + 1)],
        out_shape=[jax.ShapeDtypeStruct((T_PROMPT, D_ATTN), BF16), cache_shape, cache_shape],
        scratch_shapes=[
            pltpu.VMEM((ns * n, KV_DIM), BF16), pltpu.VMEM((ns * n, 2 * KV_DIM), BF16),
            pltpu.VMEM((ns * n, D_ATTN), F32), pltpu.VMEM((ns * N_SCORE_BUFS, n, n), F32),
        ],
        compiler_params=_cparams(("parallel", "arbitrary")),
        name="attn_prompt",
    )(*args)


def _attention_sample(proj, row0, qg, kg, og, ck, cv, cos, sin_signed, layer):
    n, tq = DEC_SEQ, 512
    nq = n // tq
    nk = n + PAST_LEN
    kern = functools.partial(_attn_kernel, nk_new=n, has_ctx=True, n_prev=0, n_seq=1)
    vec = lambda w: pl.BlockSpec((1, w), lambda b, i: (0, 0))
    q_blk0 = row0 // tq
    kv_blk0 = row0 // n
    return pl.pallas_call(
        kern,
        grid=(DEC_BATCH, nq),
        in_specs=[
            pl.BlockSpec((tq, D_ATTN), lambda b, i: (q_blk0 + b * nq + i, 0)),
            pl.BlockSpec((n, 2 * KV_DIM), lambda b, i: (kv_blk0 + b, 2)),
            vec(HEAD_DIM), vec(HEAD_DIM), vec(D_ATTN),
            pl.BlockSpec((None, None, PAST_LEN, N_KV_HEADS, HEAD_DIM), lambda b, i: (b, layer, 0, 0, 0)),
            pl.BlockSpec((None, None, PAST_LEN, N_KV_HEADS, HEAD_DIM), lambda b, i: (b, layer, 0, 0, 0)),
            pl.BlockSpec((tq, HEAD_DIM), lambda b, i: (i, 0)),
            pl.BlockSpec((tq, HEAD_DIM), lambda b, i: (i, 0)),
            pl.BlockSpec((n, HEAD_DIM), lambda b, i: (0, 0)),
            pl.BlockSpec((n, HEAD_DIM), lambda b, i: (0, 0)),
        ],
        out_specs=pl.BlockSpec((tq, D_ATTN), lambda b, i: (b * nq + i, 0)),
        out_shape=jax.ShapeDtypeStruct((T_SAMPLE, D_ATTN), BF16),
        scratch_shapes=[
            pltpu.VMEM((nk, KV_DIM), BF16), pltpu.VMEM((nk, 2 * KV_DIM), BF16), pltpu.VMEM((tq, D_ATTN), F32),
            pltpu.VMEM((N_SCORE_BUFS, tq, nk), F32),
        ],
        compiler_params=_cparams(("parallel", "arbitrary")),
        name="attn_sample",
    )(proj, proj, qg, kg, og, ck, cv, cos, sin_signed, cos, sin_signed)


def _conv_silu(x, w, b):
    n = x.shape[0]
    row = lax.broadcasted_iota(jnp.int32, (n, 1), 0)
    prev = jnp.where(row == 0, 0.0, pltpu.roll(x, 1, 0))
    nxt = jnp.where(row == n - 1, 0.0, pltpu.roll(x, n - 1, 0))
    return _silu(prev * w[0:1, :] + x * w[1:2, :] + nxt * w[2:3, :] + b)


def _softplus(x):
    return jnp.maximum(x, 0.0) + jnp.log1p(jnp.exp(-jnp.abs(x)))


def _ssd_kernel(*refs, n, has_init, n_prev, n_cast):
    refs = list(refs)
    (za_ref, zb_ref, xa_ref, xb_ref, bc_ref, dt_ref, cw_ref, cb_ref, dtb_ref, alog_ref, dsk_ref,
     ng_ref) = refs[:12]
    del refs[:12]
    if has_init:
        sf0_ref, sb0_ref = refs.pop(0), refs.pop(0)
    elif n_prev:
        psf_ref, psb_ref = refs.pop(0), refs.pop(0)
    cast_in = [refs.pop(0) for _ in range(n_cast)]
    y_ref = refs.pop(0)
    if not has_init:
        sf_ref, sb_ref = refs.pop(0), refs.pop(0)
    cast_out = [refs.pop(0) for _ in range(n_cast)]
    xc_scr, bcc_scr, dts_scr, xt_scr, yt_scr, s_scr = refs
    _side_cast(cast_in, cast_out)
    nc = n // CHUNK
    gw = HEADS_PER_GROUP * SSD_HEAD_DIM

    xc_scr[:, 0:gw] = _conv_silu(xa_ref[...], cw_ref[:, 0:gw], cb_ref[:, 0:gw])
    xc_scr[:, gw:] = _conv_silu(xb_ref[...], cw_ref[:, gw:2 * gw], cb_ref[:, gw:2 * gw])
    bcc_scr[...] = _conv_silu(bc_ref[...], cw_ref[:, 2 * gw:], cb_ref[:, 2 * gw:])
    dts_scr[...] = _softplus(dt_ref[...] + dtb_ref[...])
    for g in range(SSD_GROUPS):
        hs = slice(g * HEADS_PER_GROUP, (g + 1) * HEADS_PER_GROUP)
        if has_init:
            s_scr[0, g] = sf0_ref[hs].reshape(gw, D_STATE)
            s_scr[1, g] = sb0_ref[hs].reshape(gw, D_STATE)
        else:
            s_scr[0, g] = jnp.zeros((gw, D_STATE), F32)
            s_scr[1, g] = jnp.zeros((gw, D_STATE), F32)

    def to_channel_major(c, carry):
        rows = pl.ds(pl.multiple_of(c * CHUNK, CHUNK), CHUNK)
        xt_scr[c] = xc_scr[rows, :].T
        yt_scr[c] = jnp.zeros((D_SSD, CHUNK), F32)
        return carry

    lax.fori_loop(0, nc, to_channel_major, 0, unroll=2)

    a_row = -jnp.exp(alog_ref[...])
    ri = lax.broadcasted_iota(jnp.int32, (CHUNK, CHUNK), 0)
    ci = lax.broadcasted_iota(jnp.int32, (CHUNK, CHUNK), 1)
    lower, upper = ci <= ri, ci >= ri
    n_dirs_heads = 2 * SSD_HEADS

    def chunk_sums(dirn, c):
        tri = (lower if dirn == 0 else upper).astype(BF16)
        valid_st = upper if dirn == 0 else lower
        tri_t = valid_st.astype(BF16)
        rows = pl.ds(pl.multiple_of(c * CHUNK, CHUNK), CHUNK)
        dt = dts_scr[rows, :]
        d = dt * a_row
        d1, d2, d3 = _split3(d)
        cs = _dot(tri, d1) + _dot(tri, d2) + _dot(tri, d3)
        dt_t = dt.T[0:n_dirs_heads, :]
        e1, e2, e3 = _split3(d.T[0:n_dirs_heads, :])
        cs_t = _dot(e1, tri_t) + _dot(e2, tri_t) + _dot(e3, tri_t)
        total = cs_t[:, CHUNK - 1:CHUNK] if dirn == 0 else cs_t[:, 0:1]
        return dict(
            dirn=dirn, c=c, rows=rows, valid_st=valid_st, cs=cs, cs_t=cs_t, dt_t=dt_t,
            e_in_t=jnp.exp(cs_t),
            to_end_t=jnp.exp(total - cs_t) * dt_t,
            dec_t=jnp.broadcast_to(jnp.exp(total), (n_dirs_heads, D_STATE)))

    def group_inputs(q, g):
        bm = bcc_scr[q['rows'], g * D_STATE:(g + 1) * D_STATE].astype(BF16)
        cm = bcc_scr[q['rows'], (SSD_GROUPS + g) * D_STATE:(SSD_GROUPS + g + 1) * D_STATE]
        return dict(bm=bm, g_st=_dot_nt(bm, cm.astype(BF16)),
                    c_nt=cm.T,
                    st=s_scr[q['dirn'], g],
                    xs_parts=[], dec_parts=[])

    def head(q, gi, g, hh):
        h = g * HEADS_PER_GROUP + hh
        r = q['dirn'] * SSD_HEADS + h
        c = q['c']
        ch = slice(h * SSD_HEAD_DIM, (h + 1) * SSD_HEAD_DIM)
        x_t = xt_scr[c, ch, :]
        diff = q['cs_t'][r:r + 1, :] - q['cs'][:, r:r + 1]
        a_st = (gi['g_st'] * jnp.exp(jnp.where(q['valid_st'], diff, -jnp.inf))).astype(BF16)
        c_e = (gi['c_nt'] * q['e_in_t'][r:r + 1, :]).astype(BF16)
        x_dt = (x_t * q['dt_t'][r:r + 1, :]).astype(BF16)
        s_h = gi['st'][hh * SSD_HEAD_DIM:(hh + 1) * SSD_HEAD_DIM, :].astype(BF16)
        y_h = _dot(jnp.concatenate([x_dt, s_h], axis=1), jnp.concatenate([a_st, c_e], axis=0))
        yt_scr[c, ch, :] = yt_scr[c, ch, :] + y_h
        gi['xs_parts'].append((x_t * q['to_end_t'][r:r + 1, :]).astype(BF16))
        gi['dec_parts'].append(jnp.broadcast_to(q['dec_t'][r:r + 1, :], (SSD_HEAD_DIM, D_STATE)))

    def group_update(q, gi, g):
        ds = _dot(jnp.concatenate(gi['xs_parts'], axis=0), gi['bm'])
        s_scr[q['dirn'], g] = gi['st'] * jnp.concatenate(gi['dec_parts'], axis=0) + ds

    def body(i, carry):
        qs = [chunk_sums(0, i), chunk_sums(1, nc - 1 - i)]
        chains = [(q, group_inputs(q, g), g) for g in range(SSD_GROUPS) for q in qs]
        for hh in range(HEADS_PER_GROUP):
            for q, gi, g in chains:
                head(q, gi, g, hh)
        for q, gi, g in chains:
            group_update(q, gi, g)
        return carry

    lax.fori_loop(0, nc, body, 0, unroll=2)

    def finish(c, carry):
        rows = pl.ds(pl.multiple_of(c * CHUNK, CHUNK), CHUNK)
        y = yt_scr[c].T + xc_scr[rows, :] * dsk_ref[...]
        ya = y[:, 0:gw] * _silu(za_ref[rows, :])
        yb = y[:, gw:] * _silu(zb_ref[rows, :])
        ms = (jnp.sum(ya * ya, axis=-1, keepdims=True) + jnp.sum(yb * yb, axis=-1, keepdims=True)) / D_SSD
        inv = lax.rsqrt(ms + EPS)
        y_ref[rows, 0:gw] = (ya * inv * ng_ref[:, 0:gw]).astype(BF16)
        y_ref[rows, gw:] = (yb * inv * ng_ref[:, gw:]).astype(BF16)
        return carry

    lax.fori_loop(0, nc, finish, 0, unroll=2)

    if not has_init:
        if n_prev:
            sf_ref[0:n_prev] = psf_ref[...]
            sb_ref[0:n_prev] = psb_ref[...]
        for g in range(SSD_GROUPS):
            hs = slice(g * HEADS_PER_GROUP, (g + 1) * HEADS_PER_GROUP)
            sf_ref[n_prev, hs] = s_scr[0, g].reshape(HEADS_PER_GROUP, SSD_HEAD_DIM, D_STATE)
            sb_ref[n_prev, hs] = s_scr[1, g].reshape(HEADS_PER_GROUP, SSD_HEAD_DIM, D_STATE)


def _ssd(proj, dt_raw, p, n, nb, row_blk0, layer, init, prev_states, to_bf16=()):
    has_init = init is not None
    n_prev = 0 if has_init else layer
    kern = functools.partial(_ssd_kernel, n=n, has_init=has_init, n_prev=n_prev, n_cast=len(to_bf16))
    cast_in, cast_out, cast_shapes = _side_cast_specs(to_bf16, nb, lambda b: b)
    col = lambda cb: pl.BlockSpec((n, 512), lambda b: (row_blk0 + b, cb))
    vec = lambda r, w: pl.BlockSpec((r, w), lambda b: (0, 0))
    layers_spec = lambda k: pl.BlockSpec((None, k, SSD_HEADS, SSD_HEAD_DIM, D_STATE), lambda b: (b, 0, 0, 0, 0))
    in_specs = [
        col(3), col(4), col(5), col(6), col(7),
        pl.BlockSpec((n, 128), lambda b: (row_blk0 + b, 0)),
        vec(3, CONV_DIM), vec(1, CONV_DIM), vec(1, 128), vec(1, 128), vec(1, D_SSD), vec(1, D_SSD),
    ]
    args = [proj, proj, proj, proj, proj, dt_raw, p['conv_w'], p['conv_b'], p['dt_bias'], p['a_log'],
            p['d_skip'], p['ssd_norm_g']]
    y_spec = pl.BlockSpec((n, D_SSD), lambda b: (b, 0))
    y_shape = jax.ShapeDtypeStruct((nb * n, D_SSD), BF16)
    if has_init:
        init_spec = pl.BlockSpec((None, None, SSD_HEADS, SSD_HEAD_DIM, D_STATE), lambda b: (b, layer, 0, 0, 0))
        in_specs += [init_spec, init_spec]
        args += list(init)
        out_specs, out_shape = [y_spec], [y_shape]
    else:
        if n_prev:
            in_specs += [layers_spec(n_prev), layers_spec(n_prev)]
            args += list(prev_states)
        st_shape = jax.ShapeDtypeStruct((nb, layer + 1, SSD_HEADS, SSD_HEAD_DIM, D_STATE), F32)
        out_specs = [y_spec, layers_spec(layer + 1), layers_spec(layer + 1)]
        out_shape = [y_shape, st_shape, st_shape]
    return pl.pallas_call(
        kern,
        grid=(nb,),
        in_specs=in_specs + cast_in,
        out_specs=out_specs + cast_out,
        out_shape=out_shape + cast_shapes,
        scratch_shapes=[
            pltpu.VMEM((n, D_SSD), F32), pltpu.VMEM((n, 512), F32), pltpu.VMEM((n, 128), F32),
            pltpu.VMEM((n // CHUNK, D_SSD, CHUNK), F32), pltpu.VMEM((n // CHUNK, D_SSD, CHUNK), F32),
            pltpu.VMEM((2, SSD_GROUPS, 512, D_STATE), F32),
        ],
        compiler_params=_cparams(("parallel",)),
        name="ssd_sample" if has_init else "ssd_prompt",
    )(*args, *[part[0] for part in to_bf16])


def _seg_specs(segs, tm):
    specs, bounds, start = [], [], 0
    for a in segs:
        b0, nblk = start // tm, a.shape[0] // tm
        specs.append(pl.BlockSpec((tm, a.shape[1]), lambda i, b0=b0, nblk=nblk: (jnp.clip(i - b0, 0, nblk - 1), 0)))
        bounds.append(b0)
        start += a.shape[0]
    return specs, tuple(bounds)


def _seg_pick(refs, bounds):
    i = pl.program_id(0)
    v = refs[0][...]
    for ref, b0 in zip(refs[1:], bounds[1:]):
        v = jnp.where(i >= b0, ref[...], v)
    return v


def _outproj_kernel(*refs, with_router, o_bounds, y_bounds, x_bounds):
    refs = list(refs)
    o_refs = [refs.pop(0) for _ in o_bounds]
    y_refs = [refs.pop(0) for _ in y_bounds]
    x_refs = [refs.pop(0) for _ in x_bounds]
    if with_router:
        mod_ref, g_ref, w_ref, rw_ref, xo_ref, h_ref, meta_ref, cnt_ref, carry_scr = refs
    else:
        mod_ref, g_ref, w_ref, xo_ref, h_ref = refs
    tm = xo_ref.shape[0]
    a = jnp.concatenate([_seg_pick(o_refs, o_bounds), _seg_pick(y_refs, y_bounds)], axis=1)
    xn = _seg_pick(x_refs, x_bounds) + mod_ref[2:3, :] * _dot(a, w_ref[...])
    xo_ref[...] = xn
    h = _rms(xn, g_ref[...]) * (1.0 + mod_ref[4:5, :]) + mod_ref[3:4, :]
    h_ref[...] = h.astype(h_ref.dtype)
    if with_router:

        @pl.when(pl.program_id(0) == 0)
        def _():
            carry_scr[...] = jnp.zeros_like(carry_scr)

        h1, h2, _ = _split3(h)
        w1, w2, _ = _split3(rw_ref[...])
        logits = _dot_nt(w1, h1) + _dot_nt(w2, h1) + _dot_nt(w1, h2)
        row = lax.broadcasted_iota(jnp.int32, logits.shape, 0)
        logits = jnp.where(row < N_EXPERTS, logits, -jnp.inf)
        e = jnp.exp(logits - jnp.max(logits, axis=0, keepdims=True))
        probs = e / jnp.sum(e, axis=0, keepdims=True)
        p1 = jnp.max(probs, axis=0, keepdims=True)
        i1 = jnp.min(jnp.where(probs == p1, row, 16), axis=0, keepdims=True)
        rest = jnp.where(row == i1, -1.0, probs)
        p2 = jnp.max(rest, axis=0, keepdims=True)
        i2 = jnp.min(jnp.where(rest == p2, row, 16), axis=0, keepdims=True)
        hit1, hit2 = row == i1, row == i2
        onehot = jnp.where(hit1 | hit2, 1.0, 0.0)
        ti = lax.broadcasted_iota(jnp.int32, (tm, tm), 0)
        tj = lax.broadcasted_iota(jnp.int32, (tm, tm), 1)
        before = jnp.where(ti < tj, 1.0, 0.0).astype(BF16)
        rank = carry_scr[:, 0:1] + _dot(onehot.astype(BF16), before)
        r1 = jnp.sum(jnp.where(hit1, rank, 0.0), axis=0, keepdims=True)
        r2 = jnp.sum(jnp.where(hit2, rank, 0.0), axis=0, keepdims=True)
        carry_scr[...] = carry_scr[...] + jnp.sum(onehot, axis=1, keepdims=True)
        cnt_ref[...] = carry_scr[...]
        r8 = lax.broadcasted_iota(jnp.int32, (8, tm), 0)
        vals = [p1 / (p1 + p2), p2 / (p1 + p2), i1.astype(F32), i2.astype(F32), r1, r2]
        meta = jnp.zeros((8, tm), F32)
        for k, v in enumerate(vals):
            meta = jnp.where(r8 == k, v, meta)
        meta_ref[...] = meta


def _outproj(o_segs, y_segs, x_segs, mods, g, w_out, router_wt):
    tm = 512
    with_router = router_wt is not None
    o_specs, o_bounds = _seg_specs(o_segs, tm)
    y_specs, y_bounds = _seg_specs(y_segs, tm)
    x_specs, x_bounds = _seg_specs(x_segs, tm)
    kern = functools.partial(_outproj_kernel, with_router=with_router, o_bounds=o_bounds, y_bounds=y_bounds,
                             x_bounds=x_bounds)
    in_specs = o_specs + y_specs + x_specs + [
        pl.BlockSpec((None, 6, D_MODEL), lambda i: (_mod_group(i, tm), 0, 0)),
        pl.BlockSpec((1, D_MODEL), lambda i: (0, 0)),
        pl.BlockSpec((D_MODEL, D_MODEL), lambda i: (0, 0), pipeline_mode=pl.Buffered(1)),
    ]
    args = list(o_segs) + list(y_segs) + list(x_segs) + [mods, g, w_out]
    row_spec = pl.BlockSpec((tm, D_MODEL), lambda i: (i, 0))
    out_specs = [row_spec, row_spec]
    out_shape = [jax.ShapeDtypeStruct((T_ALL, D_MODEL), F32),
                 jax.ShapeDtypeStruct((T_ALL, D_MODEL), F32 if with_router else BF16)]
    scratch = []
    if with_router:
        in_specs.append(pl.BlockSpec((16, D_MODEL), lambda i: (0, 0)))
        args.append(router_wt)
        out_specs += [pl.BlockSpec((8, tm), lambda i: (0, i)), pl.BlockSpec((16, 128), lambda i: (0, 0))]
        out_shape += [jax.ShapeDtypeStruct((8, T_ALL), F32), jax.ShapeDtypeStruct((16, 128), F32)]
        scratch = [pltpu.VMEM((16, 128), F32)]
    return pl.pallas_call(
        kern,
        grid=(T_ALL // tm,),
        in_specs=in_specs,
        out_specs=out_specs,
        out_shape=out_shape,
        scratch_shapes=scratch,
        compiler_params=_cparams(("arbitrary",)),
        name="outproj_router" if with_router else "outproj",
    )(*args)


MOE_ROWS = 2 * T_ALL
MOE_TILE = 256
MOE_TILES = MOE_ROWS // MOE_TILE
MOE_VISITS = MOE_TILES + N_EXPERTS - 1
ROW_DMA_UNROLL = 16


def _row_copy(src, s, dst, d, sem):
    return pltpu.make_async_copy(src.at[pl.ds(s, 1)], dst.at[pl.ds(d, 1)], sem)


def _dispatch_kernel(p1_ref, p2_ref, h_ref, xs_ref, sem):
    tm = h_ref.shape[0]

    def issue(r, c):
        _row_copy(h_ref, r, xs_ref, p1_ref[0, 0, r], sem.at[0]).start()
        _row_copy(h_ref, r, xs_ref, p2_ref[0, 0, r], sem.at[1]).start()
        return c

    lax.fori_loop(0, tm, issue, 0, unroll=ROW_DMA_UNROLL)
    pltpu.make_async_copy(h_ref, xs_ref.at[pl.ds(0, tm)], sem.at[0]).wait()
    pltpu.make_async_copy(h_ref, xs_ref.at[pl.ds(0, tm)], sem.at[1]).wait()


def _dispatch(h, pos1, pos2):
    tm = 512
    nt = T_ALL // tm
    idx = lambda: pl.BlockSpec((1, 1, tm), lambda i: (i, 0, 0), memory_space=pltpu.SMEM)
    return pl.pallas_call(
        _dispatch_kernel,
        grid=(nt,),
        in_specs=[idx(), idx(), pl.BlockSpec((tm, D_MODEL), lambda i: (i, 0))],
        out_specs=pl.BlockSpec(memory_space=pl.ANY),
        out_shape=jax.ShapeDtypeStruct((MOE_ROWS, D_MODEL), F32),
        scratch_shapes=[pltpu.SemaphoreType.DMA((2,))],
        compiler_params=_cparams(("arbitrary",)),
        name="moe_dispatch",
    )(pos1.reshape(nt, 1, tm), pos2.reshape(nt, 1, tm), h)


def _experts_kernel(vt_ref, ve_ref, nv_ref, lo_ref, hi_ref, xs_ref, wg_ref, wu_ref, wd_ref, y_ref):
    v = pl.program_id(0)

    @pl.when(v < nv_ref[0])
    def _():
        e = ve_ref[v]
        x = xs_ref[...].astype(BF16)
        hid = _silu(_dot(x, wg_ref[...])) * _dot(x, wu_ref[...])
        y = _dot(hid.astype(BF16), wd_ref[...])
        row = vt_ref[v] * MOE_TILE + lax.broadcasted_iota(jnp.int32, (MOE_TILE, 1), 0)
        mine = (row >= lo_ref[e]) & (row < hi_ref[e])
        first_visit = (v == 0) | (vt_ref[jnp.maximum(v - 1, 0)] != vt_ref[v])

        @pl.when(first_visit)
        def _():
            y_ref[...] = jnp.where(mine, y, 0.0)

        @pl.when(jnp.logical_not(first_visit))
        def _():
            y_ref[...] = jnp.where(mine, y, y_ref[...])


def _experts(xs, wg, wu, wd, vt, ve, nv, lo, hi):
    grid_spec = pltpu.PrefetchScalarGridSpec(
        num_scalar_prefetch=5,
        grid=(MOE_VISITS,),
        in_specs=[
            pl.BlockSpec((MOE_TILE, D_MODEL), lambda v, vt, ve, nv, lo, hi: (vt[v], 0)),
            pl.BlockSpec((None, D_MODEL, F_EXPERT), lambda v, vt, ve, nv, lo, hi: (ve[v], 0, 0)),
            pl.BlockSpec((None, D_MODEL, F_EXPERT), lambda v, vt, ve, nv, lo, hi: (ve[v], 0, 0)),
            pl.BlockSpec((None, F_EXPERT, D_MODEL), lambda v, vt, ve, nv, lo, hi: (ve[v], 0, 0)),
        ],
        out_specs=pl.BlockSpec((MOE_TILE, D_MODEL), lambda v, vt, ve, nv, lo, hi: (vt[v], 0)),
    )
    return pl.pallas_call(
        _experts_kernel,
        grid_spec=grid_spec,
        out_shape=jax.ShapeDtypeStruct((MOE_ROWS, D_MODEL), F32),
        compiler_params=_cparams(("arbitrary",)),
        name="moe_experts",
    )(vt, ve, nv, lo, hi, xs, wg, wu, wd)


def _combine_kernel(p1c_ref, p2c_ref, p1n_ref, p2n_ref, y_hbm, x_ref, mod_ref, gate_ref, fg_ref,
                    outp_ref, outs_ref, ya_buf, yb_buf, sem):
    i = pl.program_id(0)
    n = pl.num_programs(0)
    tm = x_ref.shape[0]
    slot = i % 2

    def gather(pa_ref, pb_ref, s):
        def issue(r, c):
            _row_copy(y_hbm, pa_ref[0, 0, r], ya_buf.at[s], r, sem.at[0, s]).start()
            _row_copy(y_hbm, pb_ref[0, 0, r], yb_buf.at[s], r, sem.at[1, s]).start()
            return c

        lax.fori_loop(0, tm, issue, 0, unroll=ROW_DMA_UNROLL)

    @pl.when(i == 0)
    def _():
        gather(p1c_ref, p2c_ref, 0)

    @pl.when(i + 1 < n)
    def _():
        gather(p1n_ref, p2n_ref, 1 - slot)

    pltpu.make_async_copy(y_hbm.at[pl.ds(0, tm)], ya_buf.at[slot], sem.at[0, slot]).wait()
    pltpu.make_async_copy(y_hbm.at[pl.ds(0, tm)], yb_buf.at[slot], sem.at[1, slot]).wait()
    g = gate_ref[...]
    mix = g[:, 0:1] * ya_buf[slot] + g[:, 1:2] * yb_buf[slot]
    xo = _rms(x_ref[...] + mod_ref[5:6, :] * mix, fg_ref[...])

    @pl.when(i < T_PROMPT // tm)
    def _():
        outp_ref[...] = xo

    @pl.when(i >= T_PROMPT // tm)
    def _():
        outs_ref[...] = xo


def _combine(y, x, mods, gate_cols, pos1, pos2, final_g):
    tm = 256
    nt = T_ALL // tm
    ntp = T_PROMPT // tm
    cur = lambda: pl.BlockSpec((1, 1, tm), lambda i: (i, 0, 0), memory_space=pltpu.SMEM)
    nxt = lambda: pl.BlockSpec((1, 1, tm), lambda i: (jnp.minimum(i + 1, nt - 1), 0, 0), memory_space=pltpu.SMEM)
    p1, p2 = pos1.reshape(nt, 1, tm), pos2.reshape(nt, 1, tm)
    return pl.pallas_call(
        _combine_kernel,
        grid=(nt,),
        in_specs=[
            cur(), cur(), nxt(), nxt(),
            pl.BlockSpec(memory_space=pl.ANY),
            pl.BlockSpec((tm, D_MODEL), lambda i: (i, 0)),
            pl.BlockSpec((None, 6, D_MODEL), lambda i: (_mod_group(i, tm), 0, 0)),
            pl.BlockSpec((tm, 128), lambda i: (i, 0)),
            pl.BlockSpec((1, D_MODEL), lambda i: (0, 0)),
        ],
        out_specs=[
            pl.BlockSpec((tm, D_MODEL), lambda i: (jnp.minimum(i, ntp - 1), 0)),
            pl.BlockSpec((tm, D_MODEL), lambda i: (jnp.maximum(i - ntp, 0), 0)),
        ],
        out_shape=[
            jax.ShapeDtypeStruct((T_PROMPT, D_MODEL), F32), jax.ShapeDtypeStruct((T_SAMPLE, D_MODEL), F32),
        ],
        scratch_shapes=[
            pltpu.VMEM((2, tm, D_MODEL), F32), pltpu.VMEM((2, tm, D_MODEL), F32),
            pltpu.SemaphoreType.DMA((2, 2)),
        ],
        compiler_params=_cparams(("arbitrary",)),
        name="moe_combine",
    )(p1, p2, p1, p2, y, x, mods, gate_cols, final_g)


def _route_plan(meta, counts):
    i1, i2 = meta[2].astype(jnp.int32), meta[3].astype(jnp.int32)
    r1, r2 = meta[4].astype(jnp.int32), meta[5].astype(jnp.int32)
    cnt = counts[:N_EXPERTS, 0].astype(jnp.int32)
    hi = jnp.cumsum(cnt)
    lo = hi - cnt
    ex = jnp.arange(N_EXPERTS, dtype=jnp.int32)
    pos1 = jnp.sum(jnp.where(i1[:, None] == ex[None, :], lo[None, :], 0), axis=1) + r1
    pos2 = jnp.sum(jnp.where(i2[:, None] == ex[None, :], lo[None, :], 0), axis=1) + r2
    first_tile = lo // MOE_TILE
    n_vis_e = jnp.where(cnt > 0, (hi - 1) // MOE_TILE - first_tile + 1, 0)
    vis_hi = jnp.cumsum(n_vis_e)
    vis_lo = vis_hi - n_vis_e
    nv = vis_hi[-1]
    v = jnp.minimum(jnp.arange(MOE_VISITS, dtype=jnp.int32), nv - 1)
    ve = jnp.minimum(jnp.sum(v[:, None] >= vis_hi[None, :], axis=1), N_EXPERTS - 1).astype(jnp.int32)
    pick = lambda tab: jnp.sum(jnp.where(ve[:, None] == ex[None, :], tab[None, :], 0), axis=1)
    vt = (pick(first_tile) + v - pick(vis_lo)).astype(jnp.int32)
    return pos1, pos2, vt, ve, nv.reshape(1).astype(jnp.int32), lo.astype(jnp.int32), hi.astype(jnp.int32)


def _ffn_kernel(*refs, n_cast):
    h_ref, x_hbm, mod_ref, wg_ref, wu_ref, wd_ref = refs[:6]
    cast_in = refs[6:6 + n_cast]
    out_ref = refs[6 + n_cast]
    cast_out = refs[7 + n_cast:7 + 2 * n_cast]
    x_buf, sem = refs[7 + 2 * n_cast:]
    _side_cast(cast_in, cast_out)
    i, f = pl.program_id(0), pl.program_id(1)
    tm = h_ref.shape[0]
    x_copy = pltpu.make_async_copy(x_hbm.at[pl.ds(pl.multiple_of(i * tm, tm), tm)], x_buf, sem)

    @pl.when(f == 0)
    def _():
        x_copy.start()
        out_ref[...] = jnp.zeros_like(out_ref)

    h = h_ref[...]
    hid = _silu(_dot(h, wg_ref[...])) * _dot(h, wu_ref[...])
    out_ref[...] += _dot(hid.astype(BF16), wd_ref[...])

    @pl.when(f == pl.num_programs(1) - 1)
    def _():
        x_copy.wait()
        out_ref[...] = x_buf[...] + mod_ref[5:6, :] * out_ref[...]


FFN_CAST_STEPS = 64


def _ffn(h, x, mods, wg, wu, wd, to_bf16):
    tm, tf = 1024, 512
    nf = F_DENSE // tf
    assert (T_ALL // tm) * nf >= FFN_CAST_STEPS

    cast_in, cast_out, cast_shapes = _side_cast_specs(to_bf16, FFN_CAST_STEPS, lambda i, f: i * nf + f)
    outs = pl.pallas_call(
        functools.partial(_ffn_kernel, n_cast=len(to_bf16)),
        grid=(T_ALL // tm, nf),
        in_specs=[
            pl.BlockSpec((tm, D_MODEL), lambda i, f: (i, 0)),
            _ANY,
            pl.BlockSpec((None, 6, D_MODEL), lambda i, f: (_mod_group(i, tm), 0, 0)),
            pl.BlockSpec((D_MODEL, tf), lambda i, f: (0, f)),
            pl.BlockSpec((D_MODEL, tf), lambda i, f: (0, f)),
            pl.BlockSpec((tf, D_MODEL), lambda i, f: (f, 0)),
        ] + cast_in,
        out_specs=[pl.BlockSpec((tm, D_MODEL), lambda i, f: (i, 0))] + cast_out,
        out_shape=[jax.ShapeDtypeStruct((T_ALL, D_MODEL), F32)] + cast_shapes,
        scratch_shapes=[pltpu.VMEM((tm, D_MODEL), F32), pltpu.SemaphoreType.DMA(())],
        compiler_params=_cparams(("arbitrary", "arbitrary")),
        name="dense_ffn",
    )(h, x, mods, wg, wu, wd, *[part[0] for part in to_bf16])
    return outs[0], outs[1:]


def _rope_tables():
    n = DEC_SEQ
    rows = n // GRID_W
    t_row = jnp.repeat(jnp.arange(rows, dtype=F32), GRID_W)
    t_col = jnp.tile(jnp.arange(GRID_W, dtype=F32), rows)
    inv = 1.0 / (ROPE_THETA ** (jnp.arange(0, ROT_HALF, 2, dtype=F32) / ROT_HALF))
    ar, ac = t_row[:, None] * inv, t_col[:, None] * inv
    cos = jnp.concatenate([jnp.cos(ar), jnp.cos(ar), jnp.cos(ac), jnp.cos(ac)], axis=-1)
    sin_signed = jnp.concatenate([-jnp.sin(ar), jnp.sin(ar), -jnp.sin(ac), jnp.sin(ac)], axis=-1)
    return cos, sin_signed


def _pad_lanes(v, width=128):
    return jnp.pad(v, ((0, 0), (0, width - v.shape[-1])))


def kernel(x_prompt, x_sample, c, cache_k, cache_v, state_ssm_fwd, state_ssm_bwd, c_ctx, ada_w, ada_b, norm1_g, norm2_g, w_in, q_norm_g, k_norm_g, conv_w, conv_b, a_log_fwd, a_log_bwd, dt_bias_fwd, dt_bias_bwd, d_skip, ssd_norm_g, attn_out_g, w_out, ffn_w_gate, ffn_w_up, ffn_w_down, router_w, moe_w_gate, moe_w_up, moe_w_down, final_norm_g):
    assert DEPTH % 2 == 0
    cond = jnp.concatenate([c_ctx[None, :], c, jnp.zeros((N_COND - 1 - DEC_BATCH, D_MODEL), F32)], axis=0)
    mods_all = _ada_mods(cond, ada_w, ada_b).reshape(DEPTH, N_COND, 6, D_MODEL)
    cos, sin_signed = _rope_tables()

    w_in_t = jnp.swapaxes(w_in, 1, 2)
    x_segs = [x_prompt.reshape(T_PROMPT, D_MODEL), x_sample.reshape(T_SAMPLE, D_MODEL)]
    kv, states = None, None
    for l in range(DEPTH):
        mods = mods_all[l]
        w_out_rows = (w_out.reshape(DEPTH * D_MODEL, D_MODEL), l * D_MODEL, D_MODEL)
        proj, dt_raw, (w_o,) = _inproj(x_segs, mods, norm1_g[l][None, :], w_in_t, l, [w_out_rows])

        qg, kg, og = q_norm_g[l][None, :], k_norm_g[l][None, :], attn_out_g[l][None, :]
        o_p, k_all, v_all = _attention_prompt(proj, 0, qg, kg, og, l, kv)
        kv = (k_all, v_all)
        o_s = _attention_sample(proj, T_PROMPT, qg, kg, og, cache_k, cache_v, cos, sin_signed, l)

        p = {
            'conv_w': conv_w[l], 'conv_b': conv_b[l][None, :],
            'dt_bias': _pad_lanes(jnp.concatenate([dt_bias_fwd[l], dt_bias_bwd[l]])[None, :]),
            'a_log': _pad_lanes(jnp.concatenate([a_log_fwd[l], a_log_bwd[l]])[None, :]),
            'd_skip': jnp.repeat(d_skip[l], SSD_HEAD_DIM)[None, :],
            'ssd_norm_g': ssd_norm_g[l][None, :],
        }
        j = l // 2
        n_up, n_down = N_EXPERTS * D_MODEL, N_EXPERTS * F_EXPERT
        if l % 2 == 0:
            parts = [(ffn_w_gate.reshape(-1, F_DENSE), j * D_MODEL, D_MODEL),
                     (ffn_w_up.reshape(-1, F_DENSE), j * D_MODEL, D_MODEL),
                     (ffn_w_down.reshape(-1, D_MODEL), j * F_DENSE, F_DENSE)]
        else:
            parts = [(moe_w_down.reshape(-1, D_MODEL), j * n_down, n_down)]
        y_p, sf, sb, *mixer_w = _ssd(proj, dt_raw, p, SEQ, BATCH, 0, l, None, states, parts)
        states = (sf, sb)
        y_s, = _ssd(proj, dt_raw, p, DEC_SEQ, DEC_BATCH, T_PROMPT // DEC_SEQ, l, (state_ssm_fwd, state_ssm_bwd), None)

        g2 = norm2_g[l][None, :]
        if l % 2 == 0:
            x, h = _outproj([o_p, o_s], [y_p, y_s], x_segs, mods, g2, w_o, None)
            x, (eg, eu) = _ffn(h, x, mods, *mixer_w, [
                (moe_w_gate.reshape(-1, F_EXPERT), j * n_up, n_up),
                (moe_w_up.reshape(-1, F_EXPERT), j * n_up, n_up)])
            x_segs = [x]
        else:
            router_wt = jnp.pad(router_w[j].T, ((0, 16 - N_EXPERTS), (0, 0)))
            x, h, meta, counts = _outproj([o_p, o_s], [y_p, y_s], x_segs, mods, g2, w_o, router_wt)
            pos1, pos2, vt, ve, nv, lo, hi = _route_plan(meta, counts)
            xs = _dispatch(h, pos1, pos2)
            ys = _experts(xs, eg.reshape(N_EXPERTS, D_MODEL, F_EXPERT), eu.reshape(N_EXPERTS, D_MODEL, F_EXPERT),
                          mixer_w[0].reshape(N_EXPERTS, F_EXPERT, D_MODEL), vt, ve, nv, lo, hi)
            y_prompt, y_sample = _combine(ys, x, mods, _pad_lanes(meta[:2].T), pos1, pos2, final_norm_g[None, :])

    return (y_prompt.reshape(BATCH, SEQ, D_MODEL), y_sample.reshape(DEC_BATCH, DEC_SEQ, D_MODEL),
            kv[0], kv[1], states[0], states[1])
```

```python
import functools

import jax
import jax.numpy as jnp
from jax import lax
from jax.experimental import pallas as pl
from jax.experimental.pallas import tpu as pltpu

F32 = jnp.float32
BF16 = jnp.bfloat16

D_MODEL = 2048
BATCH = 16
SEQ = 256
DEPTH = 2
DEC_BATCH = 2
DEC_SEQ = 1024
PAST_LEN = 512
GRID_W = 64
D_ATTN = 1024
D_SSD = 1024
HEAD_DIM = 128
N_Q_HEADS = 8
N_KV_HEADS = 2
Q_PER_KV = 4
KV_DIM = 256
ROT_HALF = 64
ROPE_THETA = 10000.0
SSD_HEAD_DIM = 64
SSD_HEADS = 16
SSD_GROUPS = 2
HEADS_PER_GROUP = 8
D_STATE = 128
CONV_DIM = 1536
CHUNK = 128
N_MAIN = 4096
F_DENSE = 5632
N_EXPERTS = 8
F_EXPERT = 1024
EPS = 1e-6

T_PROMPT = BATCH * SEQ
T_SAMPLE = DEC_BATCH * DEC_SEQ
T_ALL = T_PROMPT + T_SAMPLE
N_COND = 16

VMEM_LIMIT = 58 * 1024 * 1024


def _cparams(sem):
    return pltpu.CompilerParams(dimension_semantics=sem, vmem_limit_bytes=VMEM_LIMIT)


def _mod_group(i, tm):
    return jnp.maximum(0, (i * tm - T_PROMPT + DEC_SEQ) // DEC_SEQ)


def _silu(x):
    return x * jax.nn.sigmoid(x)


def _rms(x, g):
    ms = jnp.mean(x * x, axis=-1, keepdims=True)
    return x * lax.rsqrt(ms + EPS) * g


def _dot(a, b):
    return jnp.dot(a, b, preferred_element_type=F32)


def _dot_nt(a, b):
    return lax.dot_general(a, b, (((1,), (1,)), ((), ())), preferred_element_type=F32)


def _split3(x):
    hi = x.astype(BF16)
    r1 = x - hi.astype(F32)
    mid = r1.astype(BF16)
    r2 = r1 - mid.astype(F32)
    return hi, mid, r2.astype(BF16)


def _ada_kernel(c_ref, w_ref, b_ref, o_ref):
    s = _silu(c_ref[...]).astype(BF16)
    o_ref[...] = _dot(s, w_ref[...].astype(BF16)) + b_ref[...]


def _ada_mods(cond, ada_w, ada_b):
    tn = 1024
    n_out = 6 * D_MODEL
    return pl.pallas_call(
        _ada_kernel,
        grid=(DEPTH, n_out // tn),
        in_specs=[
            pl.BlockSpec((N_COND, D_MODEL), lambda l, j: (0, 0)),
            pl.BlockSpec((None, D_MODEL, tn), lambda l, j: (l, 0, j)),
            pl.BlockSpec((None, 1, tn), lambda l, j: (l, 0, j)),
        ],
        out_specs=pl.BlockSpec((None, N_COND, tn), lambda l, j: (l, 0, j)),
        out_shape=jax.ShapeDtypeStruct((DEPTH, N_COND, n_out), F32),
        compiler_params=_cparams(("parallel", "parallel")),
        name="ada_mods",
    )(cond, ada_w, ada_b.reshape(DEPTH, 1, n_out))


def _side_cast_specs(parts, n_steps, linear_step):
    in_specs, out_specs, out_shapes = [], [], []
    for a, row0, nrows in parts:
        rows = nrows // n_steps
        blk0 = row0 // rows
        step = lambda *ids: jnp.minimum(linear_step(*ids), n_steps - 1)
        in_specs.append(pl.BlockSpec((rows, a.shape[1]), lambda *ids, blk0=blk0: (blk0 + step(*ids), 0)))
        out_specs.append(pl.BlockSpec((rows, a.shape[1]), lambda *ids: (step(*ids), 0)))
        out_shapes.append(jax.ShapeDtypeStruct((nrows, a.shape[1]), BF16))
    return in_specs, out_specs, out_shapes


def _side_cast(cast_in, cast_out):
    for src, dst in zip(cast_in, cast_out):
        dst[...] = src[...].astype(BF16)


INPROJ_NORM_ROWS = 1024
INPROJ_CAST_STEPS = 16


def _inproj_kernel(*refs, seg_rows, n_cast):
    n_seg = len(seg_rows)
    x_hbms = refs[:n_seg]
    mods_ref, g_ref, w_ref, wdt_ref = refs[n_seg:n_seg + 4]
    cast_in = refs[n_seg + 4:n_seg + 4 + n_cast]
    proj_ref, dt_ref = refs[n_seg + 4 + n_cast:n_seg + 6 + n_cast]
    cast_out = refs[n_seg + 6 + n_cast:n_seg + 6 + 2 * n_cast]
    x_buf, h_scr, sem = refs[n_seg + 6 + 2 * n_cast:]
    _side_cast(cast_in, cast_out)
    i, j = pl.program_id(0), pl.program_id(1)
    tm = x_buf.shape[0]

    def fetch(tile):
        start = 0
        for x_hbm, nrows in zip(x_hbms, seg_rows):
            b0, nb = start // tm, nrows // tm
            start += nrows

            @pl.when((tile >= b0) & (tile < b0 + nb))
            def _():
                r0 = pl.multiple_of((tile - b0) * tm, tm)
                pltpu.make_async_copy(x_hbm.at[pl.ds(r0, tm)], x_buf, sem).start()

    @pl.when(j == 0)
    def _():
        @pl.when(i == 0)
        def _():
            fetch(i)

        pltpu.make_async_copy(x_hbms[0].at[pl.ds(0, tm)], x_buf, sem).wait()
        for k in range(tm // INPROJ_NORM_ROWS):
            rows = slice(k * INPROJ_NORM_ROWS, (k + 1) * INPROJ_NORM_ROWS)
            mod = mods_ref[_mod_group(i * (tm // INPROJ_NORM_ROWS) + k, INPROJ_NORM_ROWS)]
            h = _rms(x_buf[rows, :], g_ref[...]) * (1.0 + mod[1:2, :]) + mod[0:1, :]
            h_scr[rows, :] = h.astype(BF16)

        @pl.when(i + 1 < pl.num_programs(0))
        def _():
            fetch(i + 1)

        n_dt = wdt_ref.shape[0]
        wdt = jnp.concatenate([wdt_ref[...], jnp.zeros((128 - n_dt, D_MODEL), F32)], axis=0)
        dt_ref[...] = _dot_nt(h_scr[...], wdt.astype(BF16))

    proj_ref[...] = _dot_nt(h_scr[...], w_ref[...].astype(BF16))


_ANY = pl.BlockSpec(memory_space=pl.ANY)


def _inproj(x_segs, mods, g, w_in_t, layer, to_bf16):
    tm, tn = 2048, 512
    n_dt = w_in_t.shape[1] - N_MAIN
    nj = N_MAIN // tn
    seg_rows = tuple(a.shape[0] for a in x_segs)
    assert all(r % tm == 0 for r in seg_rows) and sum(seg_rows) == T_ALL
    assert (T_ALL // tm) * nj >= INPROJ_CAST_STEPS
    cast_in, cast_out, cast_shapes = _side_cast_specs(to_bf16, INPROJ_CAST_STEPS, lambda i, j: i * nj + j)
    outs = pl.pallas_call(
        functools.partial(_inproj_kernel, seg_rows=seg_rows, n_cast=len(to_bf16)),
        grid=(T_ALL // tm, nj),
        in_specs=[_ANY] * len(x_segs) + [
            pl.BlockSpec(mods.shape, lambda i, j: (0, 0, 0)),
            pl.BlockSpec((1, D_MODEL), lambda i, j: (0, 0)),
            pl.BlockSpec((None, tn, D_MODEL), lambda i, j: (layer, j, 0)),
            pl.BlockSpec((None, n_dt, D_MODEL), lambda i, j: (layer, N_MAIN // n_dt, 0)),
        ] + cast_in,
        out_specs=[
            pl.BlockSpec((tm, tn), lambda i, j: (i, j)),
            pl.BlockSpec((tm, 128), lambda i, j: (i, 0)),
        ] + cast_out,
        out_shape=[
            jax.ShapeDtypeStruct((T_ALL, N_MAIN), F32),
            jax.ShapeDtypeStruct((T_ALL, 128), F32),
        ] + cast_shapes,
        scratch_shapes=[pltpu.VMEM((tm, D_MODEL), F32), pltpu.VMEM((tm, D_MODEL), BF16),
                        pltpu.SemaphoreType.DMA(())],
        compiler_params=_cparams(("arbitrary", "arbitrary")),
        name="inproj",
    )(*x_segs, mods, g, w_in_t, w_in_t, *[part[0] for part in to_bf16])
    return outs[0], outs[1], outs[2:]


def _rope(x, cos, sin_signed):
    lane = lax.broadcasted_iota(jnp.int32, x.shape, 1)
    first = (lane // (ROT_HALF // 2)) % 2 == 0
    swapped = jnp.where(first, pltpu.roll(x, HEAD_DIM - ROT_HALF // 2, 1), pltpu.roll(x, ROT_HALF // 2, 1))
    return x * cos + swapped * sin_signed


N_SCORE_BUFS = 3


def _attn_sequence(*refs, nk_new, has_ctx, n_prev):
    if has_ctx:
        (q_ref, kv_ref, qg_ref, kg_ref, og_ref, ck_ref, cv_ref, cq_ref, sq_ref, ckk_ref, skk_ref,
         o_ref, kb_scr, vb_scr, o_scr, s_scr) = refs
    elif n_prev:
        (q_ref, kv_ref, qg_ref, kg_ref, og_ref, pk_ref, pv_ref, o_ref, ko_ref, vo_ref,
         kb_scr, vb_scr, o_scr, s_scr) = refs
    else:
        (q_ref, kv_ref, qg_ref, kg_ref, og_ref, o_ref, ko_ref, vo_ref, kb_scr, vb_scr, o_scr, s_scr) = refs

    @pl.when(pl.program_id(1) == 0)
    def _():
        if not has_ctx and n_prev:
            ko_ref[0:n_prev] = pk_ref[...]
            vo_ref[0:n_prev] = pv_ref[...]
        for g in range(N_KV_HEADS):
            sl = slice(g * HEAD_DIM, (g + 1) * HEAD_DIM)
            kn = _rms(kv_ref[:, sl], kg_ref[...])
            v = kv_ref[:, KV_DIM + g * HEAD_DIM:KV_DIM + (g + 1) * HEAD_DIM]
            vsl = slice(2 * g * HEAD_DIM, (2 * g + 1) * HEAD_DIM)
            vb_scr[0:nk_new, vsl] = v.astype(BF16)
            vb_scr[:, (2 * g + 1) * HEAD_DIM:(2 * g + 2) * HEAD_DIM] = jnp.ones((vb_scr.shape[0], HEAD_DIM), BF16)
            if has_ctx:
                kb_scr[0:nk_new, sl] = _rope(kn, ckk_ref[...], skk_ref[...]).astype(BF16)
                kb_scr[nk_new:, sl] = ck_ref[:, g, :].astype(BF16)
                vb_scr[nk_new:, vsl] = cv_ref[:, g, :].astype(BF16)
            else:
                kb_scr[:, sl] = kn.astype(BF16)
                ko_ref[n_prev, :, g, :] = kn
                vo_ref[n_prev, :, g, :] = v

    scale_log2e = HEAD_DIM ** -0.5 * 1.4426950408889634

    def scores(h):
        g = h // Q_PER_KV
        qn = _rms(q_ref[:, h * HEAD_DIM:(h + 1) * HEAD_DIM], qg_ref[...])
        if has_ctx:
            qn = _rope(qn, cq_ref[...], sq_ref[...])
        qb = (qn * scale_log2e).astype(BF16)
        s_scr[h % N_SCORE_BUFS] = _dot_nt(qb, kb_scr[:, g * HEAD_DIM:(g + 1) * HEAD_DIM])

    for h in range(N_SCORE_BUFS - 1):
        scores(h)
    for h in range(N_Q_HEADS):
        if h + N_SCORE_BUFS - 1 < N_Q_HEADS:
            scores(h + N_SCORE_BUFS - 1)
        g = h // Q_PER_KV
        s = s_scr[h % N_SCORE_BUFS]
        e = jnp.exp2(s - jnp.max(s, axis=-1, keepdims=True))
        pv = _dot(e.astype(BF16), vb_scr[:, 2 * g * HEAD_DIM:(2 * g + 2) * HEAD_DIM])
        o_scr[:, h * HEAD_DIM:(h + 1) * HEAD_DIM] = pv[:, :HEAD_DIM] / pv[:, HEAD_DIM:]
    o_ref[...] = _rms(o_scr[...], og_ref[...]).astype(BF16)


PROMPT_SEQS_PER_STEP = 4


def _attn_kernel(*refs, nk_new, has_ctx, n_prev, n_seq):
    if n_seq == 1:
        return _attn_sequence(*refs, nk_new=nk_new, has_ctx=has_ctx, n_prev=n_prev)
    assert not has_ctx
    n_in = 7 if n_prev else 5
    q_ref, kv_ref, qg_ref, kg_ref, og_ref = refs[:5]
    o_ref, ko_ref, vo_ref, kb_scr, vb_scr, o_scr, s_scr = refs[n_in:]
    for s in range(n_seq):
        rows = pl.ds(s * nk_new, nk_new)
        prev = [r.at[s] for r in refs[5:n_in]]
        _attn_sequence(q_ref.at[rows], kv_ref.at[rows], qg_ref, kg_ref, og_ref, *prev,
                       o_ref.at[rows], ko_ref.at[s], vo_ref.at[s], kb_scr.at[rows], vb_scr.at[rows],
                       o_scr.at[rows], s_scr.at[pl.ds(s * N_SCORE_BUFS, N_SCORE_BUFS)],
                       nk_new=nk_new, has_ctx=False, n_prev=n_prev)


def _attention_prompt(proj, row0, qg, kg, og, layer, prev_kv):
    n, ns = SEQ, PROMPT_SEQS_PER_STEP
    blk0 = row0 // (ns * n)
    kern = functools.partial(_attn_kernel, nk_new=n, has_ctx=False, n_prev=layer, n_seq=ns)
    vec = lambda w: pl.BlockSpec((1, w), lambda b, i: (0, 0))
    cache_spec = lambda k: pl.BlockSpec((ns, k, n, N_KV_HEADS, HEAD_DIM), lambda b, i: (b, 0, 0, 0, 0))
    cache_shape = jax.ShapeDtypeStruct((BATCH, layer + 1, n, N_KV_HEADS, HEAD_DIM), F32)
    in_specs = [
        pl.BlockSpec((ns * n, D_ATTN), lambda b, i: (blk0 + b, 0)),
        pl.BlockSpec((ns * n, 2 * KV_DIM), lambda b, i: (blk0 + b, 2)),
        vec(HEAD_DIM), vec(HEAD_DIM), vec(D_ATTN),
    ]
    args = [proj, proj, qg, kg, og]
    if layer:
        in_specs += [cache_spec(layer), cache_spec(layer)]
        args += list(prev_kv)
    return pl.pallas_call(
        kern,
        grid=(BATCH // ns, 1),
        in_specs=in_specs,
        out_specs=[pl.BlockSpec((ns * n, D_ATTN), lambda b, i: (b, 0)), cache_spec(layer + 1),
                   cache_spec(layer + 1)],
        out_shape=[jax.ShapeDtypeStruct((T_PROMPT, D_ATTN), BF16), cache_shape, cache_shape],
        scratch_shapes=[
            pltpu.VMEM((ns * n, KV_DIM), BF16), pltpu.VMEM((ns * n, 2 * KV_DIM), BF16),
            pltpu.VMEM((ns * n, D_ATTN), F32), pltpu.VMEM((ns * N_SCORE_BUFS, n, n), F32),
        ],
        compiler_params=_cparams(("parallel", "arbitrary")),
        name="attn_prompt",
    )(*args)


def _attention_sample(proj, row0, qg, kg, og, ck, cv, cos, sin_signed, layer):
    n, tq = DEC_SEQ, 512
    nq = n // tq
    nk = n + PAST_LEN
    kern = functools.partial(_attn_kernel, nk_new=n, has_ctx=True, n_prev=0, n_seq=1)
    vec = lambda w: pl.BlockSpec((1, w), lambda b, i: (0, 0))
    q_blk0 = row0 // tq
    kv_blk0 = row0 // n
    return pl.pallas_call(
        kern,
        grid=(DEC_BATCH, nq),
        in_specs=[
            pl.BlockSpec((tq, D_ATTN), lambda b, i: (q_blk0 + b * nq + i, 0)),
            pl.BlockSpec((n, 2 * KV_DIM), lambda b, i: (kv_blk0 + b, 2)),
            vec(HEAD_DIM), vec(HEAD_DIM), vec(D_ATTN),
            pl.BlockSpec((None, None, PAST_LEN, N_KV_HEADS, HEAD_DIM), lambda b, i: (b, layer, 0, 0, 0)),
            pl.BlockSpec((None, None, PAST_LEN, N_KV_HEADS, HEAD_DIM), lambda b, i: (b, layer, 0, 0, 0)),
            pl.BlockSpec((tq, HEAD_DIM), lambda b, i: (i, 0)),
            pl.BlockSpec((tq, HEAD_DIM), lambda b, i: (i, 0)),
            pl.BlockSpec((n, HEAD_DIM), lambda b, i: (0, 0)),
            pl.BlockSpec((n, HEAD_DIM), lambda b, i: (0, 0)),
        ],
        out_specs=pl.BlockSpec((tq, D_ATTN), lambda b, i: (b * nq + i, 0)),
        out_shape=jax.ShapeDtypeStruct((T_SAMPLE, D_ATTN), BF16),
        scratch_shapes=[
            pltpu.VMEM((nk, KV_DIM), BF16), pltpu.VMEM((nk, 2 * KV_DIM), BF16), pltpu.VMEM((tq, D_ATTN), F32),
            pltpu.VMEM((N_SCORE_BUFS, tq, nk), F32),
        ],
        compiler_params=_cparams(("parallel", "arbitrary")),
        name="attn_sample",
    )(proj, proj, qg, kg, og, ck, cv, cos, sin_signed, cos, sin_signed)


def _conv_silu(x, w, b):
    n = x.shape[0]
    row = lax.broadcasted_iota(jnp.int32, (n, 1), 0)
    prev = jnp.where(row == 0, 0.0, pltpu.roll(x, 1, 0))
    nxt = jnp.where(row == n - 1, 0.0, pltpu.roll(x, n - 1, 0))
    return _silu(prev * w[0:1, :] + x * w[1:2, :] + nxt * w[2:3, :] + b)


def _softplus(x):
    return jnp.maximum(x, 0.0) + jnp.log1p(jnp.exp(-jnp.abs(x)))


def _ssd_kernel(*refs, n, has_init, n_prev, n_cast):
    refs = list(refs)
    (za_ref, zb_ref, xa_ref, xb_ref, bc_ref, dt_ref, cw_ref, cb_ref, dtb_ref, alog_ref, dsk_ref,
     ng_ref) = refs[:12]
    del refs[:12]
    if has_init:
        sf0_ref, sb0_ref = refs.pop(0), refs.pop(0)
    elif n_prev:
        psf_ref, psb_ref = refs.pop(0), refs.pop(0)
    cast_in = [refs.pop(0) for _ in range(n_cast)]
    y_ref = refs.pop(0)
    if not has_init:
        sf_ref, sb_ref = refs.pop(0), refs.pop(0)
    cast_out = [refs.pop(0) for _ in range(n_cast)]
    xc_scr, bcc_scr, dts_scr, xt_scr, yt_scr, s_scr = refs
    _side_cast(cast_in, cast_out)
    nc = n // CHUNK
    gw = HEADS_PER_GROUP * SSD_HEAD_DIM

    xc_scr[:, 0:gw] = _conv_silu(xa_ref[...], cw_ref[:, 0:gw], cb_ref[:, 0:gw])
    xc_scr[:, gw:] = _conv_silu(xb_ref[...], cw_ref[:, gw:2 * gw], cb_ref[:, gw:2 * gw])
    bcc_scr[...] = _conv_silu(bc_ref[...], cw_ref[:, 2 * gw:], cb_ref[:, 2 * gw:])
    dts_scr[...] = _softplus(dt_ref[...] + dtb_ref[...])
    for g in range(SSD_GROUPS):
        hs = slice(g * HEADS_PER_GROUP, (g + 1) * HEADS_PER_GROUP)
        if has_init:
            s_scr[0, g] = sf0_ref[hs].reshape(gw, D_STATE)
            s_scr[1, g] = sb0_ref[hs].reshape(gw, D_STATE)
        else:
            s_scr[0, g] = jnp.zeros((gw, D_STATE), F32)
            s_scr[1, g] = jnp.zeros((gw, D_STATE), F32)

    def to_channel_major(c, carry):
        rows = pl.ds(pl.multiple_of(c * CHUNK, CHUNK), CHUNK)
        xt_scr[c] = xc_scr[rows, :].T
        yt_scr[c] = jnp.zeros((D_SSD, CHUNK), F32)
        return carry

    lax.fori_loop(0, nc, to_channel_major, 0, unroll=2)

    a_row = -jnp.exp(alog_ref[...])
    ri = lax.broadcasted_iota(jnp.int32, (CHUNK, CHUNK), 0)
    ci = lax.broadcasted_iota(jnp.int32, (CHUNK, CHUNK), 1)
    lower, upper = ci <= ri, ci >= ri
    n_dirs_heads = 2 * SSD_HEADS

    def chunk_sums(dirn, c):
        tri = (lower if dirn == 0 else upper).astype(BF16)
        valid_st = upper if dirn == 0 else lower
        tri_t = valid_st.astype(BF16)
        rows = pl.ds(pl.multiple_of(c * CHUNK, CHUNK), CHUNK)
        dt = dts_scr[rows, :]
        d = dt * a_row
        d1, d2, d3 = _split3(d)
        cs = _dot(tri, d1) + _dot(tri, d2) + _dot(tri, d3)
        dt_t = dt.T[0:n_dirs_heads, :]
        e1, e2, e3 = _split3(d.T[0:n_dirs_heads, :])
        cs_t = _dot(e1, tri_t) + _dot(e2, tri_t) + _dot(e3, tri_t)
        total = cs_t[:, CHUNK - 1:CHUNK] if dirn == 0 else cs_t[:, 0:1]
        return dict(
            dirn=dirn, c=c, rows=rows, valid_st=valid_st, cs=cs, cs_t=cs_t, dt_t=dt_t,
            e_in_t=jnp.exp(cs_t),
            to_end_t=jnp.exp(total - cs_t) * dt_t,
            dec_t=jnp.broadcast_to(jnp.exp(total), (n_dirs_heads, D_STATE)))

    def group_inputs(q, g):
        bm = bcc_scr[q['rows'], g * D_STATE:(g + 1) * D_STATE].astype(BF16)
        cm = bcc_scr[q['rows'], (SSD_GROUPS + g) * D_STATE:(SSD_GROUPS + g + 1) * D_STATE]
        return dict(bm=bm, g_st=_dot_nt(bm, cm.astype(BF16)),
                    c_nt=cm.T,
                    st=s_scr[q['dirn'], g],
                    xs_parts=[], dec_parts=[])

    def head(q, gi, g, hh):
        h = g * HEADS_PER_GROUP + hh
        r = q['dirn'] * SSD_HEADS + h
        c = q['c']
        ch = slice(h * SSD_HEAD_DIM, (h + 1) * SSD_HEAD_DIM)
        x_t = xt_scr[c, ch, :]
        diff = q['cs_t'][r:r + 1, :] - q['cs'][:, r:r + 1]
        a_st = (gi['g_st'] * jnp.exp(jnp.where(q['valid_st'], diff, -jnp.inf))).astype(BF16)
        c_e = (gi['c_nt'] * q['e_in_t'][r:r + 1, :]).astype(BF16)
        x_dt = (x_t * q['dt_t'][r:r + 1, :]).astype(BF16)
        s_h = gi['st'][hh * SSD_HEAD_DIM:(hh + 1) * SSD_HEAD_DIM, :].astype(BF16)
        y_h = _dot(jnp.concatenate([x_dt, s_h], axis=1), jnp.concatenate([a_st, c_e], axis=0))
        yt_scr[c, ch, :] = yt_scr[c, ch, :] + y_h
        gi['xs_parts'].append((x_t * q['to_end_t'][r:r + 1, :]).astype(BF16))
        gi['dec_parts'].append(jnp.broadcast_to(q['dec_t'][r:r + 1, :], (SSD_HEAD_DIM, D_STATE)))

    def group_update(q, gi, g):
        ds = _dot(jnp.concatenate(gi['xs_parts'], axis=0), gi['bm'])
        s_scr[q['dirn'], g] = gi['st'] * jnp.concatenate(gi['dec_parts'], axis=0) + ds

    def body(i, carry):
        qs = [chunk_sums(0, i), chunk_sums(1, nc - 1 - i)]
        chains = [(q, group_inputs(q, g), g) for g in range(SSD_GROUPS) for q in qs]
        for hh in range(HEADS_PER_GROUP):
            for q, gi, g in chains:
                head(q, gi, g, hh)
        for q, gi, g in chains:
            group_update(q, gi, g)
        return carry

    lax.fori_loop(0, nc, body, 0, unroll=2)

    def finish(c, carry):
        rows = pl.ds(pl.multiple_of(c * CHUNK, CHUNK), CHUNK)
        y = yt_scr[c].T + xc_scr[rows, :] * dsk_ref[...]
        ya = y[:, 0:gw] * _silu(za_ref[rows, :])
        yb = y[:, gw:] * _silu(zb_ref[rows, :])
        ms = (jnp.sum(ya * ya, axis=-1, keepdims=True) + jnp.sum(yb * yb, axis=-1, keepdims=True)) / D_SSD
        inv = lax.rsqrt(ms + EPS)
        y_ref[rows, 0:gw] = (ya * inv * ng_ref[:, 0:gw]).astype(BF16)
        y_ref[rows, gw:] = (yb * inv * ng_ref[:, gw:]).astype(BF16)
        return carry

    lax.fori_loop(0, nc, finish, 0, unroll=2)

    if not has_init:
        if n_prev:
            sf_ref[0:n_prev] = psf_ref[...]
            sb_ref[0:n_prev] = psb_ref[...]
        for g in range(SSD_GROUPS):
            hs = slice(g * HEADS_PER_GROUP, (g + 1) * HEADS_PER_GROUP)
            sf_ref[n_prev, hs] = s_scr[0, g].reshape(HEADS_PER_GROUP, SSD_HEAD_DIM, D_STATE)
            sb_ref[n_prev, hs] = s_scr[1, g].reshape(HEADS_PER_GROUP, SSD_HEAD_DIM, D_STATE)


def _ssd(proj, dt_raw, p, n, nb, row_blk0, layer, init, prev_states, to_bf16=()):
    has_init = init is not None
    n_prev = 0 if has_init else layer
    kern = functools.partial(_ssd_kernel, n=n, has_init=has_init, n_prev=n_prev, n_cast=len(to_bf16))
    cast_in, cast_out, cast_shapes = _side_cast_specs(to_bf16, nb, lambda b: b)
    col = lambda cb: pl.BlockSpec((n, 512), lambda b: (row_blk0 + b, cb))
    vec = lambda r, w: pl.BlockSpec((r, w), lambda b: (0, 0))
    layers_spec = lambda k: pl.BlockSpec((None, k, SSD_HEADS, SSD_HEAD_DIM, D_STATE), lambda b: (b, 0, 0, 0, 0))
    in_specs = [
        col(3), col(4), col(5), col(6), col(7),
        pl.BlockSpec((n, 128), lambda b: (row_blk0 + b, 0)),
        vec(3, CONV_DIM), vec(1, CONV_DIM), vec(1, 128), vec(1, 128), vec(1, D_SSD), vec(1, D_SSD),
    ]
    args = [proj, proj, proj, proj, proj, dt_raw, p['conv_w'], p['conv_b'], p['dt_bias'], p['a_log'],
            p['d_skip'], p['ssd_norm_g']]
    y_spec = pl.BlockSpec((n, D_SSD), lambda b: (b, 0))
    y_shape = jax.ShapeDtypeStruct((nb * n, D_SSD), BF16)
    if has_init:
        init_spec = pl.BlockSpec((None, None, SSD_HEADS, SSD_HEAD_DIM, D_STATE), lambda b: (b, layer, 0, 0, 0))
        in_specs += [init_spec, init_spec]
        args += list(init)
        out_specs, out_shape = [y_spec], [y_shape]
    else:
        if n_prev:
            in_specs += [layers_spec(n_prev), layers_spec(n_prev)]
            args += list(prev_states)
        st_shape = jax.ShapeDtypeStruct((nb, layer + 1, SSD_HEADS, SSD_HEAD_DIM, D_STATE), F32)
        out_specs = [y_spec, layers_spec(layer + 1), layers_spec(layer + 1)]
        out_shape = [y_shape, st_shape, st_shape]
    return pl.pallas_call(
        kern,
        grid=(nb,),
        in_specs=in_specs + cast_in,
        out_specs=out_specs + cast_out,
        out_shape=out_shape + cast_shapes,
        scratch_shapes=[
            pltpu.VMEM((n, D_SSD), F32), pltpu.VMEM((n, 512), F32), pltpu.VMEM((n, 128), F32),
            pltpu.VMEM((n // CHUNK, D_SSD, CHUNK), F32), pltpu.VMEM((n // CHUNK, D_SSD, CHUNK), F32),
            pltpu.VMEM((2, SSD_GROUPS, 512, D_STATE), F32),
        ],
        compiler_params=_cparams(("parallel",)),
        name="ssd_sample" if has_init else "ssd_prompt",
    )(*args, *[part[0] for part in to_bf16])


def _seg_specs(segs, tm):
    specs, bounds, start = [], [], 0
    for a in segs:
        b0, nblk = start // tm, a.shape[0] // tm
        specs.append(pl.BlockSpec((tm, a.shape[1]), lambda i, b0=b0, nblk=nblk: (jnp.clip(i - b0, 0, nblk - 1), 0)))
        bounds.append(b0)
        start += a.shape[0]
    return specs, tuple(bounds)


def _seg_pick(refs, bounds):
    i = pl.program_id(0)
    v = refs[0][...]
    for ref, b0 in zip(refs[1:], bounds[1:]):
        v = jnp.where(i >= b0, ref[...], v)
    return v


def _outproj_kernel(*refs, with_router, o_bounds, y_bounds, x_bounds):
    refs = list(refs)
    o_refs = [refs.pop(0) for _ in o_bounds]
    y_refs = [refs.pop(0) for _ in y_bounds]
    x_refs = [refs.pop(0) for _ in x_bounds]
    if with_router:
        mod_ref, g_ref, w_ref, rw_ref, xo_ref, h_ref, meta_ref, cnt_ref, carry_scr = refs
    else:
        mod_ref, g_ref, w_ref, xo_ref, h_ref = refs
    tm = xo_ref.shape[0]
    a = jnp.concatenate([_seg_pick(o_refs, o_bounds), _seg_pick(y_refs, y_bounds)], axis=1)
    xn = _seg_pick(x_refs, x_bounds) + mod_ref[2:3, :] * _dot(a, w_ref[...])
    xo_ref[...] = xn
    h = _rms(xn, g_ref[...]) * (1.0 + mod_ref[4:5, :]) + mod_ref[3:4, :]
    h_ref[...] = h.astype(h_ref.dtype)
    if with_router:

        @pl.when(pl.program_id(0) == 0)
        def _():
            carry_scr[...] = jnp.zeros_like(carry_scr)

        h1, h2, _ = _split3(h)
        w1, w2, _ = _split3(rw_ref[...])
        logits = _dot_nt(w1, h1) + _dot_nt(w2, h1) + _dot_nt(w1, h2)
        row = lax.broadcasted_iota(jnp.int32, logits.shape, 0)
        logits = jnp.where(row < N_EXPERTS, logits, -jnp.inf)
        e = jnp.exp(logits - jnp.max(logits, axis=0, keepdims=True))
        probs = e / jnp.sum(e, axis=0, keepdims=True)
        p1 = jnp.max(probs, axis=0, keepdims=True)
        i1 = jnp.min(jnp.where(probs == p1, row, 16), axis=0, keepdims=True)
        rest = jnp.where(row == i1, -1.0, probs)
        p2 = jnp.max(rest, axis=0, keepdims=True)
        i2 = jnp.min(jnp.where(rest == p2, row, 16), axis=0, keepdims=True)
        hit1, hit2 = row == i1, row == i2
        onehot = jnp.where(hit1 | hit2, 1.0, 0.0)
        ti = lax.broadcasted_iota(jnp.int32, (tm, tm), 0)
        tj = lax.broadcasted_iota(jnp.int32, (tm, tm), 1)
        before = jnp.where(ti < tj, 1.0, 0.0).astype(BF16)
        rank = carry_scr[:, 0:1] + _dot(onehot.astype(BF16), before)
        r1 = jnp.sum(jnp.where(hit1, rank, 0.0), axis=0, keepdims=True)
        r2 = jnp.sum(jnp.where(hit2, rank, 0.0), axis=0, keepdims=True)
        carry_scr[...] = carry_scr[...] + jnp.sum(onehot, axis=1, keepdims=True)
        cnt_ref[...] = carry_scr[...]
        r8 = lax.broadcasted_iota(jnp.int32, (8, tm), 0)
        vals = [p1 / (p1 + p2), p2 / (p1 + p2), i1.astype(F32), i2.astype(F32), r1, r2]
        meta = jnp.zeros((8, tm), F32)
        for k, v in enumerate(vals):
            meta = jnp.where(r8 == k, v, meta)
        meta_ref[...] = meta


def _outproj(o_segs, y_segs, x_segs, mods, g, w_out, router_wt):
    tm = 512
    with_router = router_wt is not None
    o_specs, o_bounds = _seg_specs(o_segs, tm)
    y_specs, y_bounds = _seg_specs(y_segs, tm)
    x_specs, x_bounds = _seg_specs(x_segs, tm)
    kern = functools.partial(_outproj_kernel, with_router=with_router, o_bounds=o_bounds, y_bounds=y_bounds,
                             x_bounds=x_bounds)
    in_specs = o_specs + y_specs + x_specs + [
        pl.BlockSpec((None, 6, D_MODEL), lambda i: (_mod_group(i, tm), 0, 0)),
        pl.BlockSpec((1, D_MODEL), lambda i: (0, 0)),
        pl.BlockSpec((D_MODEL, D_MODEL), lambda i: (0, 0), pipeline_mode=pl.Buffered(1)),
    ]
    args = list(o_segs) + list(y_segs) + list(x_segs) + [mods, g, w_out]
    row_spec = pl.BlockSpec((tm, D_MODEL), lambda i: (i, 0))
    out_specs = [row_spec, row_spec]
    out_shape = [jax.ShapeDtypeStruct((T_ALL, D_MODEL), F32),
                 jax.ShapeDtypeStruct((T_ALL, D_MODEL), F32 if with_router else BF16)]
    scratch = []
    if with_router:
        in_specs.append(pl.BlockSpec((16, D_MODEL), lambda i: (0, 0)))
        args.append(router_wt)
        out_specs += [pl.BlockSpec((8, tm), lambda i: (0, i)), pl.BlockSpec((16, 128), lambda i: (0, 0))]
        out_shape += [jax.ShapeDtypeStruct((8, T_ALL), F32), jax.ShapeDtypeStruct((16, 128), F32)]
        scratch = [pltpu.VMEM((16, 128), F32)]
    return pl.pallas_call(
        kern,
        grid=(T_ALL // tm,),
        in_specs=in_specs,
        out_specs=out_specs,
        out_shape=out_shape,
        scratch_shapes=scratch,
        compiler_params=_cparams(("arbitrary",)),
        name="outproj_router" if with_router else "outproj",
    )(*args)


MOE_ROWS = 2 * T_ALL
MOE_TILE = 256
MOE_TILES = MOE_ROWS // MOE_TILE
MOE_VISITS = MOE_TILES + N_EXPERTS - 1
ROW_DMA_UNROLL = 16


def _row_copy(src, s, dst, d, sem):
    return pltpu.make_async_copy(src.at[pl.ds(s, 1)], dst.at[pl.ds(d, 1)], sem)


def _dispatch_kernel(p1_ref, p2_ref, h_ref, xs_ref, sem):
    tm = h_ref.shape[0]

    def issue(r, c):
        _row_copy(h_ref, r, xs_ref, p1_ref[0, 0, r], sem.at[0]).start()
        _row_copy(h_ref, r, xs_ref, p2_ref[0, 0, r], sem.at[1]).start()
        return c

    lax.fori_loop(0, tm, issue, 0, unroll=ROW_DMA_UNROLL)
    pltpu.make_async_copy(h_ref, xs_ref.at[pl.ds(0, tm)], sem.at[0]).wait()
    pltpu.make_async_copy(h_ref, xs_ref.at[pl.ds(0, tm)], sem.at[1]).wait()


def _dispatch(h, pos1, pos2):
    tm = 512
    nt = T_ALL // tm
    idx = lambda: pl.BlockSpec((1, 1, tm), lambda i: (i, 0, 0), memory_space=pltpu.SMEM)
    return pl.pallas_call(
        _dispatch_kernel,
        grid=(nt,),
        in_specs=[idx(), idx(), pl.BlockSpec((tm, D_MODEL), lambda i: (i, 0))],
        out_specs=pl.BlockSpec(memory_space=pl.ANY),
        out_shape=jax.ShapeDtypeStruct((MOE_ROWS, D_MODEL), F32),
        scratch_shapes=[pltpu.SemaphoreType.DMA((2,))],
        compiler_params=_cparams(("arbitrary",)),
        name="moe_dispatch",
    )(pos1.reshape(nt, 1, tm), pos2.reshape(nt, 1, tm), h)


def _experts_kernel(vt_ref, ve_ref, nv_ref, lo_ref, hi_ref, xs_ref, wg_ref, wu_ref, wd_ref, y_ref):
    v = pl.program_id(0)

    @pl.when(v < nv_ref[0])
    def _():
        e = ve_ref[v]
        x = xs_ref[...].astype(BF16)
        hid = _silu(_dot(x, wg_ref[...])) * _dot(x, wu_ref[...])
        y = _dot(hid.astype(BF16), wd_ref[...])
        row = vt_ref[v] * MOE_TILE + lax.broadcasted_iota(jnp.int32, (MOE_TILE, 1), 0)
        mine = (row >= lo_ref[e]) & (row < hi_ref[e])
        first_visit = (v == 0) | (vt_ref[jnp.maximum(v - 1, 0)] != vt_ref[v])

        @pl.when(first_visit)
        def _():
            y_ref[...] = jnp.where(mine, y, 0.0)

        @pl.when(jnp.logical_not(first_visit))
        def _():
            y_ref[...] = jnp.where(mine, y, y_ref[...])


def _experts(xs, wg, wu, wd, vt, ve, nv, lo, hi):
    grid_spec = pltpu.PrefetchScalarGridSpec(
        num_scalar_prefetch=5,
        grid=(MOE_VISITS,),
        in_specs=[
            pl.BlockSpec((MOE_TILE, D_MODEL), lambda v, vt, ve, nv, lo, hi: (vt[v], 0)),
            pl.BlockSpec((None, D_MODEL, F_EXPERT), lambda v, vt, ve, nv, lo, hi: (ve[v], 0, 0)),
            pl.BlockSpec((None, D_MODEL, F_EXPERT), lambda v, vt, ve, nv, lo, hi: (ve[v], 0, 0)),
            pl.BlockSpec((None, F_EXPERT, D_MODEL), lambda v, vt, ve, nv, lo, hi: (ve[v], 0, 0)),
        ],
        out_specs=pl.BlockSpec((MOE_TILE, D_MODEL), lambda v, vt, ve, nv, lo, hi: (vt[v], 0)),
    )
    return pl.pallas_call(
        _experts_kernel,
        grid_spec=grid_spec,
        out_shape=jax.ShapeDtypeStruct((MOE_ROWS, D_MODEL), F32),
        compiler_params=_cparams(("arbitrary",)),
        name="moe_experts",
    )(vt, ve, nv, lo, hi, xs, wg, wu, wd)


def _combine_kernel(p1c_ref, p2c_ref, p1n_ref, p2n_ref, y_hbm, x_ref, mod_ref, gate_ref, fg_ref,
                    outp_ref, outs_ref, ya_buf, yb_buf, sem):
    i = pl.program_id(0)
    n = pl.num_programs(0)
    tm = x_ref.shape[0]
    slot = i % 2

    def gather(pa_ref, pb_ref, s):
        def issue(r, c):
            _row_copy(y_hbm, pa_ref[0, 0, r], ya_buf.at[s], r, sem.at[0, s]).start()
            _row_copy(y_hbm, pb_ref[0, 0, r], yb_buf.at[s], r, sem.at[1, s]).start()
            return c

        lax.fori_loop(0, tm, issue, 0, unroll=ROW_DMA_UNROLL)

    @pl.when(i == 0)
    def _():
        gather(p1c_ref, p2c_ref, 0)

    @pl.when(i + 1 < n)
    def _():
        gather(p1n_ref, p2n_ref, 1 - slot)

    pltpu.make_async_copy(y_hbm.at[pl.ds(0, tm)], ya_buf.at[slot], sem.at[0, slot]).wait()
    pltpu.make_async_copy(y_hbm.at[pl.ds(0, tm)], yb_buf.at[slot], sem.at[1, slot]).wait()
    g = gate_ref[...]
    mix = g[:, 0:1] * ya_buf[slot] + g[:, 1:2] * yb_buf[slot]
    xo = _rms(x_ref[...] + mod_ref[5:6, :] * mix, fg_ref[...])

    @pl.when(i < T_PROMPT // tm)
    def _():
        outp_ref[...] = xo

    @pl.when(i >= T_PROMPT // tm)
    def _():
        outs_ref[...] = xo


def _combine(y, x, mods, gate_cols, pos1, pos2, final_g):
    tm = 256
    nt = T_ALL // tm
    ntp = T_PROMPT // tm
    cur = lambda: pl.BlockSpec((1, 1, tm), lambda i: (i, 0, 0), memory_space=pltpu.SMEM)
    nxt = lambda: pl.BlockSpec((1, 1, tm), lambda i: (jnp.minimum(i + 1, nt - 1), 0, 0), memory_space=pltpu.SMEM)
    p1, p2 = pos1.reshape(nt, 1, tm), pos2.reshape(nt, 1, tm)
    return pl.pallas_call(
        _combine_kernel,
        grid=(nt,),
        in_specs=[
            cur(), cur(), nxt(), nxt(),
            pl.BlockSpec(memory_space=pl.ANY),
            pl.BlockSpec((tm, D_MODEL), lambda i: (i, 0)),
            pl.BlockSpec((None, 6, D_MODEL), lambda i: (_mod_group(i, tm), 0, 0)),
            pl.BlockSpec((tm, 128), lambda i: (i, 0)),
            pl.BlockSpec((1, D_MODEL), lambda i: (0, 0)),
        ],
        out_specs=[
            pl.BlockSpec((tm, D_MODEL), lambda i: (jnp.minimum(i, ntp - 1), 0)),
            pl.BlockSpec((tm, D_MODEL), lambda i: (jnp.maximum(i - ntp, 0), 0)),
        ],
        out_shape=[
            jax.ShapeDtypeStruct((T_PROMPT, D_MODEL), F32), jax.ShapeDtypeStruct((T_SAMPLE, D_MODEL), F32),
        ],
        scratch_shapes=[
            pltpu.VMEM((2, tm, D_MODEL), F32), pltpu.VMEM((2, tm, D_MODEL), F32),
            pltpu.SemaphoreType.DMA((2, 2)),
        ],
        compiler_params=_cparams(("arbitrary",)),
        name="moe_combine",
    )(p1, p2, p1, p2, y, x, mods, gate_cols, final_g)


def _route_plan(meta, counts):
    i1, i2 = meta[2].astype(jnp.int32), meta[3].astype(jnp.int32)
    r1, r2 = meta[4].astype(jnp.int32), meta[5].astype(jnp.int32)
    cnt = counts[:N_EXPERTS, 0].astype(jnp.int32)
    hi = jnp.cumsum(cnt)
    lo = hi - cnt
    ex = jnp.arange(N_EXPERTS, dtype=jnp.int32)
    pos1 = jnp.sum(jnp.where(i1[:, None] == ex[None, :], lo[None, :], 0), axis=1) + r1
    pos2 = jnp.sum(jnp.where(i2[:, None] == ex[None, :], lo[None, :], 0), axis=1) + r2
    first_tile = lo // MOE_TILE
    n_vis_e = jnp.where(cnt > 0, (hi - 1) // MOE_TILE - first_tile + 1, 0)
    vis_hi = jnp.cumsum(n_vis_e)
    vis_lo = vis_hi - n_vis_e
    nv = vis_hi[-1]
    v = jnp.minimum(jnp.arange(MOE_VISITS, dtype=jnp.int32), nv - 1)
    ve = jnp.minimum(jnp.sum(v[:, None] >= vis_hi[None, :], axis=1), N_EXPERTS - 1).astype(jnp.int32)
    pick = lambda tab: jnp.sum(jnp.where(ve[:, None] == ex[None, :], tab[None, :], 0), axis=1)
    vt = (pick(first_tile) + v - pick(vis_lo)).astype(jnp.int32)
    return pos1, pos2, vt, ve, nv.reshape(1).astype(jnp.int32), lo.astype(jnp.int32), hi.astype(jnp.int32)


def _ffn_kernel(*refs, n_cast):
    h_ref, x_hbm, mod_ref, wg_ref, wu_ref, wd_ref = refs[:6]
    cast_in = refs[6:6 + n_cast]
    out_ref = refs[6 + n_cast]
    cast_out = refs[7 + n_cast:7 + 2 * n_cast]
    x_buf, sem = refs[7 + 2 * n_cast:]
    _side_cast(cast_in, cast_out)
    i, f = pl.program_id(0), pl.program_id(1)
    tm = h_ref.shape[0]
    x_copy = pltpu.make_async_copy(x_hbm.at[pl.ds(pl.multiple_of(i * tm, tm), tm)], x_buf, sem)

    @pl.when(f == 0)
    def _():
        x_copy.start()
        out_ref[...] = jnp.zeros_like(out_ref)

    h = h_ref[...]
    hid = _silu(_dot(h, wg_ref[...])) * _dot(h, wu_ref[...])
    out_ref[...] += _dot(hid.astype(BF16), wd_ref[...])

    @pl.when(f == pl.num_programs(1) - 1)
    def _():
        x_copy.wait()
        out_ref[...] = x_buf[...] + mod_ref[5:6, :] * out_ref[...]


FFN_CAST_STEPS = 64


def _ffn(h, x, mods, wg, wu, wd, to_bf16):
    tm, tf = 1024, 512
    nf = F_DENSE // tf
    assert (T_ALL // tm) * nf >= FFN_CAST_STEPS

    cast_in, cast_out, cast_shapes = _side_cast_specs(to_bf16, FFN_CAST_STEPS, lambda i, f: i * nf + f)
    outs = pl.pallas_call(
        functools.partial(_ffn_kernel, n_cast=len(to_bf16)),
        grid=(T_ALL // tm, nf),
        in_specs=[
            pl.BlockSpec((tm, D_MODEL), lambda i, f: (i, 0)),
            _ANY,
            pl.BlockSpec((None, 6, D_MODEL), lambda i, f: (_mod_group(i, tm), 0, 0)),
            pl.BlockSpec((D_MODEL, tf), lambda i, f: (0, f)),
            pl.BlockSpec((D_MODEL, tf), lambda i, f: (0, f)),
            pl.BlockSpec((tf, D_MODEL), lambda i, f: (f, 0)),
        ] + cast_in,
        out_specs=[pl.BlockSpec((tm, D_MODEL), lambda i, f: (i, 0))] + cast_out,
        out_shape=[jax.ShapeDtypeStruct((T_ALL, D_MODEL), F32)] + cast_shapes,
        scratch_shapes=[pltpu.VMEM((tm, D_MODEL), F32), pltpu.SemaphoreType.DMA(())],
        compiler_params=_cparams(("arbitrary", "arbitrary")),
        name="dense_ffn",
    )(h, x, mods, wg, wu, wd, *[part[0] for part in to_bf16])
    return outs[0], outs[1:]


def _rope_tables():
    n = DEC_SEQ
    rows = n // GRID_W
    t_row = jnp.repeat(jnp.arange(rows, dtype=F32), GRID_W)
    t_col = jnp.tile(jnp.arange(GRID_W, dtype=F32), rows)
    inv = 1.0 / (ROPE_THETA ** (jnp.arange(0, ROT_HALF, 2, dtype=F32) / ROT_HALF))
    ar, ac = t_row[:, None] * inv, t_col[:, None] * inv
    cos = jnp.concatenate([jnp.cos(ar), jnp.cos(ar), jnp.cos(ac), jnp.cos(ac)], axis=-1)
    sin_signed = jnp.concatenate([-jnp.sin(ar), jnp.sin(ar), -jnp.sin(ac), jnp.sin(ac)], axis=-1)
    return cos, sin_signed


def _pad_lanes(v, width=128):
    return jnp.pad(v, ((0, 0), (0, width - v.shape[-1])))


def kernel(x_prompt, x_sample, c, cache_k, cache_v, state_ssm_fwd, state_ssm_bwd, c_ctx, ada_w, ada_b, norm1_g, norm2_g, w_in, q_norm_g, k_norm_g, conv_w, conv_b, a_log_fwd, a_log_bwd, dt_bias_fwd, dt_bias_bwd, d_skip, ssd_norm_g, attn_out_g, w_out, ffn_w_gate, ffn_w_up, ffn_w_down, router_w, moe_w_gate, moe_w_up, moe_w_down, final_norm_g):
    assert DEPTH % 2 == 0
    cond = jnp.concatenate([c_ctx[None, :], c, jnp.zeros((N_COND - 1 - DEC_BATCH, D_MODEL), F32)], axis=0)
    mods_all = _ada_mods(cond, ada_w, ada_b).reshape(DEPTH, N_COND, 6, D_MODEL)
    cos, sin_signed = _rope_tables()

    w_in_t = jnp.swapaxes(w_in, 1, 2)
    x_segs = [x_prompt.reshape(T_PROMPT, D_MODEL), x_sample.reshape(T_SAMPLE, D_MODEL)]
    kv, states = None, None
    for l in range(DEPTH):
        mods = mods_all[l]
        w_out_rows = (w_out.reshape(DEPTH * D_MODEL, D_MODEL), l * D_MODEL, D_MODEL)
        proj, dt_raw, (w_o,) = _inproj(x_segs, mods, norm1_g[l][None, :], w_in_t, l, [w_out_rows])

        qg, kg, og = q_norm_g[l][None, :], k_norm_g[l][None, :], attn_out_g[l][None, :]
        o_p, k_all, v_all = _attention_prompt(proj, 0, qg, kg, og, l, kv)
        kv = (k_all, v_all)
        o_s = _attention_sample(proj, T_PROMPT, qg, kg, og, cache_k, cache_v, cos, sin_signed, l)

        p = {
            'conv_w': conv_w[l], 'conv_b': conv_b[l][None, :],
            'dt_bias': _pad_lanes(jnp.concatenate([dt_bias_fwd[l], dt_bias_bwd[l]])[None, :]),
            'a_log': _pad_lanes(jnp.concatenate([a_log_fwd[l], a_log_bwd[l]])[None, :]),
            'd_skip': jnp.repeat(d_skip[l], SSD_HEAD_DIM)[None, :],
            'ssd_norm_g': ssd_norm_g[l][None, :],
        }
        j = l // 2
        n_up, n_down = N_EXPERTS * D_MODEL, N_EXPERTS * F_EXPERT
        if l % 2 == 0:
            parts = [(ffn_w_gate.reshape(-1, F_DENSE), j * D_MODEL, D_MODEL),
                     (ffn_w_up.reshape(-1, F_DENSE), j * D_MODEL, D_MODEL),
                     (ffn_w_down.reshape(-1, D_MODEL), j * F_DENSE, F_DENSE)]
        else:
            parts = [(moe_w_down.reshape(-1, D_MODEL), j * n_down, n_down)]
        y_p, sf, sb, *mixer_w = _ssd(proj, dt_raw, p, SEQ, BATCH, 0, l, None, states, parts)
        states = (sf, sb)
        y_s, = _ssd(proj, dt_raw, p, DEC_SEQ, DEC_BATCH, T_PROMPT // DEC_SEQ, l, (state_ssm_fwd, state_ssm_bwd), None)

        g2 = norm2_g[l][None, :]
        if l % 2 == 0:
            x, h = _outproj([o_p, o_s], [y_p, y_s], x_segs, mods, g2, w_o, None)
            x, (eg, eu) = _ffn(h, x, mods, *mixer_w, [
                (moe_w_gate.reshape(-1, F_EXPERT), j * n_up, n_up),
                (moe_w_up.reshape(-1, F_EXPERT), j * n_up, n_up)])
            x_segs = [x]
        else:
            router_wt = jnp.pad(router_w[j].T, ((0, 16 - N_EXPERTS), (0, 0)))
            x, h, meta, counts = _outproj([o_p, o_s], [y_p, y_s], x_segs, mods, g2, w_o, router_wt)
            pos1, pos2, vt, ve, nv, lo, hi = _route_plan(meta, counts)
            xs = _dispatch(h, pos1, pos2)
            ys = _experts(xs, eg.reshape(N_EXPERTS, D_MODEL, F_EXPERT), eu.reshape(N_EXPERTS, D_MODEL, F_EXPERT),
                          mixer_w[0].reshape(N_EXPERTS, F_EXPERT, D_MODEL), vt, ve, nv, lo, hi)
            y_prompt, y_sample = _combine(ys, x, mods, _pad_lanes(meta[:2].T), pos1, pos2, final_norm_g[None, :])

    return (y_prompt.reshape(BATCH, SEQ, D_MODEL), y_sample.reshape(DEC_BATCH, DEC_SEQ, D_MODEL),
            kv[0], kv[1], states[0], states[1])
```
